```python
import math
import jax, jax.numpy as jnp
from jax import lax
import numpy as np

D_MODEL = 2048
BATCH = 8
SEQ = 2048
DEPTH = 4

MLA_HEADS = 8
MLA_NOPE = 128
MLA_ROPE = 64
MLA_V = 128
MLA_Q_LORA = 512
MLA_KV_LORA = 256
MLA_WIDTH = MLA_HEADS * MLA_V
ROPE_THETA = 10000.0

SSM_WIDTH = D_MODEL // 4
SSM_GROUP = 16
SSM_GROUPS = SSM_WIDTH // SSM_GROUP
SSM_STATE = 64

DIL_WIDTH = D_MODEL // 4
DIL_HEAD_DIM = 64
DIL_HEADS = DIL_WIDTH // DIL_HEAD_DIM
DIL_PATTERNS = ((128, 1), (512, 4), (2048, 16))

BLOCK = 128
MIX_WIDTH = MLA_WIDTH + SSM_WIDTH + DIL_WIDTH
IN_SPLITS = (MLA_Q_LORA, MLA_KV_LORA, MLA_ROPE, SSM_WIDTH, DIL_WIDTH, DIL_WIDTH, DIL_WIDTH)
IN_WIDTH = sum(IN_SPLITS)
D_FF = ((8 * D_MODEL + 3 * 256 - 1) // (3 * 256)) * 256
NORM_EPS = 1e-6

kernel_name = "hymba_mla_s5_dilated_hybrid"


def rms_norm(x, g):
    xf = x.astype(jnp.float32)
    y = xf * lax.rsqrt(jnp.mean(xf * xf, axis=-1, keepdims=True) + NORM_EPS)
    return (y * g.astype(jnp.float32)).astype(x.dtype)


def apply_rope(x, pos):
    half = x.shape[-1] // 2
    inv_freq = ROPE_THETA ** (-jnp.arange(half, dtype=jnp.float32) / half)
    ang = pos.astype(jnp.float32)[:, None] * inv_freq[None, :]
    cos = jnp.cos(ang)[None, :, None, :]
    sin = jnp.sin(ang)[None, :, None, :]
    xf = x.astype(jnp.float32)
    x1, x2 = xf[..., :half], xf[..., half:]
    return jnp.concatenate([x1 * cos - x2 * sin, x2 * cos + x1 * sin], axis=-1).astype(x.dtype)


def mla_mixer(c_q, c_kv, k_rope, g_q, w_uq, g_kv, w_ukv):
    B, S, _ = c_q.shape
    pos = jnp.arange(S)
    q = (rms_norm(c_q, g_q) @ w_uq).reshape(B, S, MLA_HEADS, MLA_NOPE + MLA_ROPE)
    q_nope = q[..., :MLA_NOPE]
    q_pe = apply_rope(q[..., MLA_NOPE:], pos)
    kv = (rms_norm(c_kv, g_kv) @ w_ukv).reshape(B, S, MLA_HEADS, MLA_NOPE + MLA_V)
    k_nope, v = kv[..., :MLA_NOPE], kv[..., MLA_NOPE:]
    k_pe = apply_rope(k_rope[:, :, None, :], pos)[:, :, 0]
    scale = (MLA_NOPE + MLA_ROPE) ** -0.5
    nb = S // BLOCK
    qn_b = q_nope.reshape(B, nb, BLOCK, MLA_HEADS, MLA_NOPE).transpose(1, 0, 2, 3, 4)
    qp_b = q_pe.reshape(B, nb, BLOCK, MLA_HEADS, MLA_ROPE).transpose(1, 0, 2, 3, 4)
    kpos = jnp.arange(S)

    def one_block(args):
        b, qn, qp = args
        s = (jnp.einsum('bqhd,bkhd->bhqk', qn, k_nope).astype(jnp.float32)
             + jnp.einsum('bqhd,bkd->bhqk', qp, k_pe).astype(jnp.float32)) * scale
        qpos = b * BLOCK + jnp.arange(BLOCK)
        causal = qpos[:, None] >= kpos[None, :]
        s = jnp.where(causal[None, None], s, -jnp.inf)
        p = jax.nn.softmax(s, axis=-1).astype(v.dtype)
        return jnp.einsum('bhqk,bkhd->bqhd', p, v)

    out = lax.map(one_block, (jnp.arange(nb), qn_b, qp_b))
    return out.transpose(1, 0, 2, 3, 4).reshape(B, S, MLA_WIDTH)


def s5_mixer(u, a_re, a_im, b_re, b_im, c_re, c_im, d_skip, log_dt, w_glu, b_glu):
    B, S, _ = u.shape
    f32 = jnp.float32
    uf = u.astype(f32).reshape(B, S, SSM_GROUPS, SSM_GROUP)
    lam = lax.complex(jnp.minimum(a_re.astype(f32), -1e-4), a_im.astype(f32))
    dt = jnp.exp(log_dt.astype(f32))[:, None]
    a_bar = jnp.exp(lam * dt)
    b_cplx = lax.complex(b_re.astype(f32), b_im.astype(f32))
    b_bar = ((a_bar - 1.0) / lam)[..., None] * b_cplx
    bu = jnp.einsum('gnp,bsgp->bsgn', b_bar, uf.astype(jnp.complex64))
    a_seq = jnp.broadcast_to(a_bar, bu.shape)

    def combine(left, right):
        a_l, h_l = left
        a_r, h_r = right
        return a_r * a_l, a_r * h_l + h_r

    _, h = lax.associative_scan(combine, (a_seq, bu), axis=1)
    c_cplx = lax.complex(c_re.astype(f32), c_im.astype(f32))
    y = jnp.einsum('gpn,bsgn->bsgp', c_cplx, h).real + d_skip.astype(f32) * uf
    y = jax.nn.gelu(y.reshape(B, S, SSM_WIDTH))
    z = y @ w_glu.astype(f32) + b_glu.astype(f32)
    out = z[..., :SSM_WIDTH] * jax.nn.sigmoid(z[..., SSM_WIDTH:])
    return out.astype(u.dtype)


def strided_fold(x, dil):
    B, S = x.shape[:2]
    rest = x.shape[2:]
    return x.reshape(B, S // dil, dil, *rest).swapaxes(1, 2).reshape(B * dil, S // dil, *rest)


def strided_unfold(x, batch, dil):
    L = x.shape[1]
    rest = x.shape[2:]
    return x.reshape(batch, dil, L, *rest).swapaxes(1, 2).reshape(batch, L * dil, *rest)


def banded_window_attention(q, k, v, span):
    Z, L, H, D = q.shape
    nb = -(-L // BLOCK)
    Lp = nb * BLOCK
    pad = ((0, 0), (0, Lp - L), (0, 0), (0, 0))
    qb, kb, vb = [jnp.pad(t, pad).reshape(Z, nb, BLOCK, H, D) for t in (q, k, v)]

    def with_prev(t):
        prev = jnp.pad(t, ((0, 0), (1, 0), (0, 0), (0, 0), (0, 0)))[:, :-1]
        return jnp.concatenate([prev, t], axis=2)

    kk, vv = with_prev(kb), with_prev(vb)
    s = jnp.einsum('znqhd,znkhd->znhqk', qb, kk).astype(jnp.float32) * (D ** -0.5)
    qpos = jnp.arange(nb)[:, None] * BLOCK + jnp.arange(BLOCK)[None, :]
    kpos = (jnp.arange(nb)[:, None] - 1) * BLOCK + jnp.arange(2 * BLOCK)[None, :]
    dist = qpos[:, :, None] - kpos[:, None, :]
    mask = (dist >= 0) & (dist <= span) & (kpos[:, None, :] >= 0)
    s = jnp.where(mask[None, :, None], s, -jnp.inf)
    m = jnp.max(s, axis=-1, keepdims=True)
    p = jnp.exp(s - m)
    l = jnp.sum(p, axis=-1, keepdims=True)
    o = jnp.einsum('znhqk,znkhd->znqhd', (p / l).astype(v.dtype), vv)
    lse = (m + jnp.log(l))[..., 0].transpose(0, 1, 3, 2).reshape(Z, Lp, H)
    return o.reshape(Z, Lp, H, D)[:, :L], lse[:, :L]


def dilated_mixer(qd, kd, vd):
    B, S, _ = qd.shape
    q, k, v = [t.reshape(B, S, DIL_HEADS, DIL_HEAD_DIM) for t in (qd, kd, vd)]
    outs, lses = [], []
    for window, dil in DIL_PATTERNS:
        o, lse = banded_window_attention(strided_fold(q, dil), strided_fold(k, dil),
                                         strided_fold(v, dil), window // dil)
        outs.append(strided_unfold(o, B, dil).astype(jnp.float32))
        lses.append(strided_unfold(lse, B, dil))
    wts = jax.nn.softmax(jnp.stack(lses, axis=0), axis=0)
    out = wts[0][..., None] * outs[0] + wts[1][..., None] * outs[1] + wts[2][..., None] * outs[2]
    return out.reshape(B, S, DIL_WIDTH).astype(qd.dtype)


def _fwd_setup_inputs(seed: int = 0) -> dict:
    key = jax.random.key(seed)
    ks = jax.random.split(key, 32)
    L = DEPTH
    nrm = lambda k, shape, scale: jax.random.normal(k, shape, jnp.float32) * scale
    gain = lambda k, n: 1.0 + 0.02 * jax.random.normal(k, (L, n), jnp.float32)
    out_scale = (2 * DEPTH) ** -0.5
    n_idx = jnp.arange(SSM_STATE, dtype=jnp.float32)
    return {
        "x": jax.random.normal(ks[0], (BATCH, SEQ, D_MODEL), jnp.float32),
        "g_mix": gain(ks[1], D_MODEL),
        "w_in": nrm(ks[2], (L, D_MODEL, IN_WIDTH), D_MODEL ** -0.5),
        "g_q": gain(ks[3], MLA_Q_LORA),
        "w_uq": nrm(ks[4], (L, MLA_Q_LORA, MLA_HEADS * (MLA_NOPE + MLA_ROPE)), MLA_Q_LORA ** -0.5),
        "g_kv": gain(ks[5], MLA_KV_LORA),
        "w_ukv": nrm(ks[6], (L, MLA_KV_LORA, MLA_HEADS * (MLA_NOPE + MLA_V)), MLA_KV_LORA ** -0.5),
        "a_re": -0.5 + nrm(ks[7], (L, SSM_GROUPS, SSM_STATE), 0.01),
        "a_im": math.pi * n_idx + nrm(ks[8], (L, SSM_GROUPS, SSM_STATE), 0.01),
        "b_re": nrm(ks[9], (L, SSM_GROUPS, SSM_STATE, SSM_GROUP), (2 * SSM_GROUP) ** -0.5),
        "b_im": nrm(ks[10], (L, SSM_GROUPS, SSM_STATE, SSM_GROUP), (2 * SSM_GROUP) ** -0.5),
        "c_re": nrm(ks[11], (L, SSM_GROUPS, SSM_GROUP, SSM_STATE), 0.5),
        "c_im": nrm(ks[12], (L, SSM_GROUPS, SSM_GROUP, SSM_STATE), 0.5),
        "d_skip": nrm(ks[13], (L, SSM_GROUPS, SSM_GROUP), 1.0),
        "log_dt": jax.random.uniform(ks[14], (L, SSM_GROUPS), jnp.float32,
                                     math.log(1e-3), math.log(1e-1)),
        "w_glu": nrm(ks[15], (L, SSM_WIDTH, 2 * SSM_WIDTH), SSM_WIDTH ** -0.5),
        "b_glu": nrm(ks[16], (L, 2 * SSM_WIDTH), 0.01),
        "g_out_mla": gain(ks[17], MLA_WIDTH),
        "g_out_ssm": gain(ks[18], SSM_WIDTH),
        "g_out_dil": gain(ks[19], DIL_WIDTH),
        "w_o": nrm(ks[20], (L, MIX_WIDTH, D_MODEL), MIX_WIDTH ** -0.5 * out_scale),
        "g_ffn": gain(ks[21], D_MODEL),
        "w_gate": nrm(ks[22], (L, D_MODEL, D_FF), D_MODEL ** -0.5),
        "w_up": nrm(ks[23], (L, D_MODEL, D_FF), D_MODEL ** -0.5),
        "w_down": nrm(ks[24], (L, D_FF, D_MODEL), D_FF ** -0.5 * out_scale),
        "g_final": 1.0 + 0.02 * jax.random.normal(ks[25], (D_MODEL,), jnp.float32),
    }


def _fwd_reference(x, g_mix, w_in, g_q, w_uq, g_kv, w_ukv, a_re, a_im, b_re, b_im, c_re, c_im,
              d_skip, log_dt, w_glu, b_glu, g_out_mla, g_out_ssm, g_out_dil, w_o,
              g_ffn, w_gate, w_up, w_down, g_final):
    split_at = np.cumsum(IN_SPLITS)[:-1].tolist()
    for l in range(DEPTH):
        h = rms_norm(x, g_mix[l])
        proj = h @ w_in[l]
        c_q, c_kv, k_rope, u, qd, kd, vd = jnp.split(proj, split_at, axis=-1)
        y_mla = mla_mixer(c_q, c_kv, k_rope, g_q[l], w_uq[l], g_kv[l], w_ukv[l])
        y_ssm = s5_mixer(u, a_re[l], a_im[l], b_re[l], b_im[l], c_re[l], c_im[l],
                         d_skip[l], log_dt[l], w_glu[l], b_glu[l])
        y_dil = dilated_mixer(qd, kd, vd)
        y = jnp.concatenate([rms_norm(y_mla, g_out_mla[l]),
                             rms_norm(y_ssm, g_out_ssm[l]),
                             rms_norm(y_dil, g_out_dil[l])], axis=-1)
        x = x + y @ w_o[l]
        h = rms_norm(x, g_ffn[l])
        x = x + (jax.nn.silu(h @ w_gate[l]) * (h @ w_up[l])) @ w_down[l]
    return rms_norm(x, g_final)


import jax as _jax
import jax.numpy as _jnp

TWIN_FORMAT = 'train_step'
FWD_PARAMS = ['x', 'g_mix', 'w_in', 'g_q', 'w_uq', 'g_kv', 'w_ukv', 'a_re', 'a_im', 'b_re', 'b_im', 'c_re', 'c_im', 'd_skip', 'log_dt', 'w_glu', 'b_glu', 'g_out_mla', 'g_out_ssm', 'g_out_dil', 'w_o', 'g_ffn', 'w_gate', 'w_up', 'w_down', 'g_final']
TWIN_WEIGHTS = ['g_mix', 'w_in', 'g_q', 'w_uq', 'g_kv', 'w_ukv', 'a_re', 'a_im', 'b_re', 'b_im', 'c_re', 'c_im', 'd_skip', 'log_dt', 'w_glu', 'b_glu', 'g_out_mla', 'g_out_ssm', 'g_out_dil', 'w_o', 'g_ffn', 'w_gate', 'w_up', 'w_down', 'g_final']
TWIN_DIFF_INPUT = 'x'
TWIN_INPUTS = ['x', 'g_mix', 'w_in', 'g_q', 'w_uq', 'g_kv', 'w_ukv', 'a_re', 'a_im', 'b_re', 'b_im', 'c_re', 'c_im', 'd_skip', 'log_dt', 'w_glu', 'b_glu', 'g_out_mla', 'g_out_ssm', 'g_out_dil', 'w_o', 'g_ffn', 'w_gate', 'w_up', 'w_down', 'g_final', 'loss_target', 'm_g_mix', 'm_w_in', 'm_g_q', 'm_w_uq', 'm_g_kv', 'm_w_ukv', 'm_a_re', 'm_a_im', 'm_b_re', 'm_b_im', 'm_c_re', 'm_c_im', 'm_d_skip', 'm_log_dt', 'm_w_glu', 'm_b_glu', 'm_g_out_mla', 'm_g_out_ssm', 'm_g_out_dil', 'm_w_o', 'm_g_ffn', 'm_w_gate', 'm_w_up', 'm_w_down', 'm_g_final', 'v_g_mix', 'v_w_in', 'v_g_q', 'v_w_uq', 'v_g_kv', 'v_w_ukv', 'v_a_re', 'v_a_im', 'v_b_re', 'v_b_im', 'v_c_re', 'v_c_im', 'v_d_skip', 'v_log_dt', 'v_w_glu', 'v_b_glu', 'v_g_out_mla', 'v_g_out_ssm', 'v_g_out_dil', 'v_w_o', 'v_g_ffn', 'v_w_gate', 'v_w_up', 'v_w_down', 'v_g_final']
TWIN_OUTPUTS = ['loss', 'grad_x', 'grad_g_mix', 'grad_w_in', 'grad_g_q', 'grad_w_uq', 'grad_g_kv', 'grad_w_ukv', 'grad_a_re', 'grad_a_im', 'grad_b_re', 'grad_b_im', 'grad_c_re', 'grad_c_im', 'grad_d_skip', 'grad_log_dt', 'grad_w_glu', 'grad_b_glu', 'grad_g_out_mla', 'grad_g_out_ssm', 'grad_g_out_dil', 'grad_w_o', 'grad_g_ffn', 'grad_w_gate', 'grad_w_up', 'grad_w_down', 'grad_g_final', 'delta_g_mix', 'delta_w_in', 'delta_g_q', 'delta_w_uq', 'delta_g_kv', 'delta_w_ukv', 'delta_a_re', 'delta_a_im', 'delta_b_re', 'delta_b_im', 'delta_c_re', 'delta_c_im', 'delta_d_skip', 'delta_log_dt', 'delta_w_glu', 'delta_b_glu', 'delta_g_out_mla', 'delta_g_out_ssm', 'delta_g_out_dil', 'delta_w_o', 'delta_g_ffn', 'delta_w_gate', 'delta_w_up', 'delta_w_down', 'delta_g_final', 'new_m_g_mix', 'new_m_w_in', 'new_m_g_q', 'new_m_w_uq', 'new_m_g_kv', 'new_m_w_ukv', 'new_m_a_re', 'new_m_a_im', 'new_m_b_re', 'new_m_b_im', 'new_m_c_re', 'new_m_c_im', 'new_m_d_skip', 'new_m_log_dt', 'new_m_w_glu', 'new_m_b_glu', 'new_m_g_out_mla', 'new_m_g_out_ssm', 'new_m_g_out_dil', 'new_m_w_o', 'new_m_g_ffn', 'new_m_w_gate', 'new_m_w_up', 'new_m_w_down', 'new_m_g_final', 'new_v_g_mix', 'new_v_w_in', 'new_v_g_q', 'new_v_w_uq', 'new_v_g_kv', 'new_v_w_ukv', 'new_v_a_re', 'new_v_a_im', 'new_v_b_re', 'new_v_b_im', 'new_v_c_re', 'new_v_c_im', 'new_v_d_skip', 'new_v_log_dt', 'new_v_w_glu', 'new_v_b_glu', 'new_v_g_out_mla', 'new_v_g_out_ssm', 'new_v_g_out_dil', 'new_v_w_o', 'new_v_g_ffn', 'new_v_w_gate', 'new_v_w_up', 'new_v_w_down', 'new_v_g_final']
TWIN_LEAF_KINDS = {'loss': 'loss', 'grad_x': 'grad_x', 'grad_g_mix': 'grad_w', 'grad_w_in': 'grad_w', 'grad_g_q': 'grad_w', 'grad_w_uq': 'grad_w', 'grad_g_kv': 'grad_w', 'grad_w_ukv': 'grad_w', 'grad_a_re': 'grad_w', 'grad_a_im': 'grad_w', 'grad_b_re': 'grad_w', 'grad_b_im': 'grad_w', 'grad_c_re': 'grad_w', 'grad_c_im': 'grad_w', 'grad_d_skip': 'grad_w', 'grad_log_dt': 'grad_w', 'grad_w_glu': 'grad_w', 'grad_b_glu': 'grad_w', 'grad_g_out_mla': 'grad_w', 'grad_g_out_ssm': 'grad_w', 'grad_g_out_dil': 'grad_w', 'grad_w_o': 'grad_w', 'grad_g_ffn': 'grad_w', 'grad_w_gate': 'grad_w', 'grad_w_up': 'grad_w', 'grad_w_down': 'grad_w', 'grad_g_final': 'grad_w', 'delta_g_mix': 'delta_w', 'delta_w_in': 'delta_w', 'delta_g_q': 'delta_w', 'delta_w_uq': 'delta_w', 'delta_g_kv': 'delta_w', 'delta_w_ukv': 'delta_w', 'delta_a_re': 'delta_w', 'delta_a_im': 'delta_w', 'delta_b_re': 'delta_w', 'delta_b_im': 'delta_w', 'delta_c_re': 'delta_w', 'delta_c_im': 'delta_w', 'delta_d_skip': 'delta_w', 'delta_log_dt': 'delta_w', 'delta_w_glu': 'delta_w', 'delta_b_glu': 'delta_w', 'delta_g_out_mla': 'delta_w', 'delta_g_out_ssm': 'delta_w', 'delta_g_out_dil': 'delta_w', 'delta_w_o': 'delta_w', 'delta_g_ffn': 'delta_w', 'delta_w_gate': 'delta_w', 'delta_w_up': 'delta_w', 'delta_w_down': 'delta_w', 'delta_g_final': 'delta_w', 'new_m_g_mix': 'new_m', 'new_m_w_in': 'new_m', 'new_m_g_q': 'new_m', 'new_m_w_uq': 'new_m', 'new_m_g_kv': 'new_m', 'new_m_w_ukv': 'new_m', 'new_m_a_re': 'new_m', 'new_m_a_im': 'new_m', 'new_m_b_re': 'new_m', 'new_m_b_im': 'new_m', 'new_m_c_re': 'new_m', 'new_m_c_im': 'new_m', 'new_m_d_skip': 'new_m', 'new_m_log_dt': 'new_m', 'new_m_w_glu': 'new_m', 'new_m_b_glu': 'new_m', 'new_m_g_out_mla': 'new_m', 'new_m_g_out_ssm': 'new_m', 'new_m_g_out_dil': 'new_m', 'new_m_w_o': 'new_m', 'new_m_g_ffn': 'new_m', 'new_m_w_gate': 'new_m', 'new_m_w_up': 'new_m', 'new_m_w_down': 'new_m', 'new_m_g_final': 'new_m', 'new_v_g_mix': 'new_v', 'new_v_w_in': 'new_v', 'new_v_g_q': 'new_v', 'new_v_w_uq': 'new_v', 'new_v_g_kv': 'new_v', 'new_v_w_ukv': 'new_v', 'new_v_a_re': 'new_v', 'new_v_a_im': 'new_v', 'new_v_b_re': 'new_v', 'new_v_b_im': 'new_v', 'new_v_c_re': 'new_v', 'new_v_c_im': 'new_v', 'new_v_d_skip': 'new_v', 'new_v_log_dt': 'new_v', 'new_v_w_glu': 'new_v', 'new_v_b_glu': 'new_v', 'new_v_g_out_mla': 'new_v', 'new_v_g_out_ssm': 'new_v', 'new_v_g_out_dil': 'new_v', 'new_v_w_o': 'new_v', 'new_v_g_ffn': 'new_v', 'new_v_w_gate': 'new_v', 'new_v_w_up': 'new_v', 'new_v_w_down': 'new_v', 'new_v_g_final': 'new_v'}


def _forward(args):
    return _fwd_reference(*[args[k] for k in FWD_PARAMS])


def _output_shape():
    out = _jax.eval_shape(lambda: _forward(_fwd_setup_inputs(0)))
    return out.shape, out.dtype

N_MICROBATCH = 1
ADAM_LR = 0.001
ADAM_B1 = 0.9
ADAM_B2 = 0.999
ADAM_EPS = 1e-08
ADAM_WD = 0.01
ADAM_STEP = 10
PER_EXAMPLE_BATCH_AXIS = {'x': 0, 'loss_target': 0}
SHARED_INPUTS = []
_WEIGHT_DTYPES = {'g_mix': _jnp.float32, 'w_in': _jnp.float32, 'g_q': _jnp.float32, 'w_uq': _jnp.float32, 'g_kv': _jnp.float32, 'w_ukv': _jnp.float32, 'a_re': _jnp.float32, 'a_im': _jnp.float32, 'b_re': _jnp.float32, 'b_im': _jnp.float32, 'c_re': _jnp.float32, 'c_im': _jnp.float32, 'd_skip': _jnp.float32, 'log_dt': _jnp.float32, 'w_glu': _jnp.float32, 'b_glu': _jnp.float32, 'g_out_mla': _jnp.float32, 'g_out_ssm': _jnp.float32, 'g_out_dil': _jnp.float32, 'w_o': _jnp.float32, 'g_ffn': _jnp.float32, 'w_gate': _jnp.float32, 'w_up': _jnp.float32, 'w_down': _jnp.float32, 'g_final': _jnp.float32}
MOMENT_SCALE = {'g_mix': 2.928574e-02, 'w_in': 2.510360e-02, 'g_q': 1.960069e-02, 'w_uq': 1.113923e-02, 'g_kv': 6.837914e-02, 'w_ukv': 2.275876e-02, 'a_re': 7.051590e-03, 'a_im': 6.828140e-03, 'b_re': 4.322422e-03, 'b_im': 4.287481e-03, 'c_re': 1.530647e-03, 'c_im': 1.591922e-03, 'd_skip': 2.965551e-02, 'log_dt': 4.934073e+00, 'w_glu': 2.096110e-02, 'b_glu': 4.991617e-02, 'g_out_mla': 3.131306e-02, 'g_out_ssm': 2.934127e-02, 'g_out_dil': 2.663449e-02, 'w_o': 8.313791e-02, 'g_ffn': 1.640048e-02, 'w_gate': 7.122514e-03, 'w_up': 6.923150e-03, 'w_down': 3.249247e-02, 'g_final': 8.062369e+00}


def _to_microbatches(a, axis):
    t = _jnp.moveaxis(a, axis, 0)
    t = t.reshape((N_MICROBATCH, t.shape[0] // N_MICROBATCH) + t.shape[1:])
    return _jnp.moveaxis(t, 1, axis + 1)


def setup_inputs(seed: int = 0) -> dict:
    inp = _fwd_setup_inputs(seed)
    key = _jax.random.fold_in(_jax.random.key(seed), 7919)
    shape, _ = _output_shape()
    out = dict(inp)
    out["loss_target"] = _jax.random.normal(_jax.random.fold_in(key, 0), shape, _jnp.float32)
    for i, name in enumerate(TWIN_WEIGHTS):
        w = inp[name].astype(_jnp.float32)
        if MOMENT_SCALE is None:
            s = _jnp.sqrt(_jnp.mean(_jnp.square(w)) + 1e-30)
        else:
            s = MOMENT_SCALE[name]
        km, kv = _jax.random.split(_jax.random.fold_in(key, i + 1))
        out[name] = w
        out["m_" + name] = s * _jax.random.normal(km, w.shape, _jnp.float32)
        out["v_" + name] = (s * s) * _jax.random.uniform(kv, w.shape, _jnp.float32, 0.5, 1.5)
    if N_MICROBATCH > 1:
        for name, axis in PER_EXAMPLE_BATCH_AXIS.items():
            out[name] = _to_microbatches(out[name], axis)
    return {'x': out['x'], 'g_mix': out['g_mix'], 'w_in': out['w_in'], 'g_q': out['g_q'], 'w_uq': out['w_uq'], 'g_kv': out['g_kv'], 'w_ukv': out['w_ukv'], 'a_re': out['a_re'], 'a_im': out['a_im'], 'b_re': out['b_re'], 'b_im': out['b_im'], 'c_re': out['c_re'], 'c_im': out['c_im'], 'd_skip': out['d_skip'], 'log_dt': out['log_dt'], 'w_glu': out['w_glu'], 'b_glu': out['b_glu'], 'g_out_mla': out['g_out_mla'], 'g_out_ssm': out['g_out_ssm'], 'g_out_dil': out['g_out_dil'], 'w_o': out['w_o'], 'g_ffn': out['g_ffn'], 'w_gate': out['w_gate'], 'w_up': out['w_up'], 'w_down': out['w_down'], 'g_final': out['g_final'], 'loss_target': out['loss_target'], 'm_g_mix': out['m_g_mix'], 'm_w_in': out['m_w_in'], 'm_g_q': out['m_g_q'], 'm_w_uq': out['m_w_uq'], 'm_g_kv': out['m_g_kv'], 'm_w_ukv': out['m_w_ukv'], 'm_a_re': out['m_a_re'], 'm_a_im': out['m_a_im'], 'm_b_re': out['m_b_re'], 'm_b_im': out['m_b_im'], 'm_c_re': out['m_c_re'], 'm_c_im': out['m_c_im'], 'm_d_skip': out['m_d_skip'], 'm_log_dt': out['m_log_dt'], 'm_w_glu': out['m_w_glu'], 'm_b_glu': out['m_b_glu'], 'm_g_out_mla': out['m_g_out_mla'], 'm_g_out_ssm': out['m_g_out_ssm'], 'm_g_out_dil': out['m_g_out_dil'], 'm_w_o': out['m_w_o'], 'm_g_ffn': out['m_g_ffn'], 'm_w_gate': out['m_w_gate'], 'm_w_up': out['m_w_up'], 'm_w_down': out['m_w_down'], 'm_g_final': out['m_g_final'], 'v_g_mix': out['v_g_mix'], 'v_w_in': out['v_w_in'], 'v_g_q': out['v_g_q'], 'v_w_uq': out['v_w_uq'], 'v_g_kv': out['v_g_kv'], 'v_w_ukv': out['v_w_ukv'], 'v_a_re': out['v_a_re'], 'v_a_im': out['v_a_im'], 'v_b_re': out['v_b_re'], 'v_b_im': out['v_b_im'], 'v_c_re': out['v_c_re'], 'v_c_im': out['v_c_im'], 'v_d_skip': out['v_d_skip'], 'v_log_dt': out['v_log_dt'], 'v_w_glu': out['v_w_glu'], 'v_b_glu': out['v_b_glu'], 'v_g_out_mla': out['v_g_out_mla'], 'v_g_out_ssm': out['v_g_out_ssm'], 'v_g_out_dil': out['v_g_out_dil'], 'v_w_o': out['v_w_o'], 'v_g_ffn': out['v_g_ffn'], 'v_w_gate': out['v_w_gate'], 'v_w_up': out['v_w_up'], 'v_w_down': out['v_w_down'], 'v_g_final': out['v_g_final']}


def _loss(weights, diff, rest, loss_target):
    with _jax.named_scope("forward"):
        args = {**rest, TWIN_DIFF_INPUT: diff, **{k: w.astype(_WEIGHT_DTYPES[k]) for k, w in weights.items()}}
        y = _forward(args)
    with _jax.named_scope("loss_head"):
        err = _jnp.square(y.astype(_jnp.float32) - loss_target)
        return 0.5 * _jnp.sum(_jnp.mean(err, axis=-1)) if err.ndim else 0.5 * err


def _adamw(w, g, m, v):
    m = ADAM_B1 * m + (1.0 - ADAM_B1) * g
    v = ADAM_B2 * v + (1.0 - ADAM_B2) * _jnp.square(g)
    m_hat = m / (1.0 - ADAM_B1 ** ADAM_STEP)
    v_hat = v / (1.0 - ADAM_B2 ** ADAM_STEP)
    delta = -ADAM_LR * (m_hat / (_jnp.sqrt(v_hat) + ADAM_EPS) + ADAM_WD * w)
    return delta, m, v


def reference(x, g_mix, w_in, g_q, w_uq, g_kv, w_ukv, a_re, a_im, b_re, b_im, c_re, c_im, d_skip, log_dt, w_glu, b_glu, g_out_mla, g_out_ssm, g_out_dil, w_o, g_ffn, w_gate, w_up, w_down, g_final, loss_target, m_g_mix, m_w_in, m_g_q, m_w_uq, m_g_kv, m_w_ukv, m_a_re, m_a_im, m_b_re, m_b_im, m_c_re, m_c_im, m_d_skip, m_log_dt, m_w_glu, m_b_glu, m_g_out_mla, m_g_out_ssm, m_g_out_dil, m_w_o, m_g_ffn, m_w_gate, m_w_up, m_w_down, m_g_final, v_g_mix, v_w_in, v_g_q, v_w_uq, v_g_kv, v_w_ukv, v_a_re, v_a_im, v_b_re, v_b_im, v_c_re, v_c_im, v_d_skip, v_log_dt, v_w_glu, v_b_glu, v_g_out_mla, v_g_out_ssm, v_g_out_dil, v_w_o, v_g_ffn, v_w_gate, v_w_up, v_w_down, v_g_final):
    given = dict(x=x, g_mix=g_mix, w_in=w_in, g_q=g_q, w_uq=w_uq, g_kv=g_kv, w_ukv=w_ukv, a_re=a_re, a_im=a_im, b_re=b_re, b_im=b_im, c_re=c_re, c_im=c_im, d_skip=d_skip, log_dt=log_dt, w_glu=w_glu, b_glu=b_glu, g_out_mla=g_out_mla, g_out_ssm=g_out_ssm, g_out_dil=g_out_dil, w_o=w_o, g_ffn=g_ffn, w_gate=w_gate, w_up=w_up, w_down=w_down, g_final=g_final, loss_target=loss_target, m_g_mix=m_g_mix, m_w_in=m_w_in, m_g_q=m_g_q, m_w_uq=m_w_uq, m_g_kv=m_g_kv, m_w_ukv=m_w_ukv, m_a_re=m_a_re, m_a_im=m_a_im, m_b_re=m_b_re, m_b_im=m_b_im, m_c_re=m_c_re, m_c_im=m_c_im, m_d_skip=m_d_skip, m_log_dt=m_log_dt, m_w_glu=m_w_glu, m_b_glu=m_b_glu, m_g_out_mla=m_g_out_mla, m_g_out_ssm=m_g_out_ssm, m_g_out_dil=m_g_out_dil, m_w_o=m_w_o, m_g_ffn=m_g_ffn, m_w_gate=m_w_gate, m_w_up=m_w_up, m_w_down=m_w_down, m_g_final=m_g_final, v_g_mix=v_g_mix, v_w_in=v_w_in, v_g_q=v_g_q, v_w_uq=v_w_uq, v_g_kv=v_g_kv, v_w_ukv=v_w_ukv, v_a_re=v_a_re, v_a_im=v_a_im, v_b_re=v_b_re, v_b_im=v_b_im, v_c_re=v_c_re, v_c_im=v_c_im, v_d_skip=v_d_skip, v_log_dt=v_log_dt, v_w_glu=v_w_glu, v_b_glu=v_b_glu, v_g_out_mla=v_g_out_mla, v_g_out_ssm=v_g_out_ssm, v_g_out_dil=v_g_out_dil, v_w_o=v_w_o, v_g_ffn=v_g_ffn, v_w_gate=v_w_gate, v_w_up=v_w_up, v_w_down=v_w_down, v_g_final=v_g_final)
    weights = {n: given[n] for n in TWIN_WEIGHTS}
    shared = {n: given[n] for n in SHARED_INPUTS}
    per_example = {n: given[n] for n in ['x']}
    grad_fn = _jax.value_and_grad(_loss, argnums=(0, 1))

    def one_microbatch(ex, loss_target):
        ex = dict(ex)
        diff = ex.pop(TWIN_DIFF_INPUT)
        return grad_fn(weights, diff, {**shared, **ex}, loss_target)

    if N_MICROBATCH == 1:
        loss, (grad_w, grad_x) = one_microbatch(per_example, given["loss_target"])
    else:
        def body(carry, xs):
            loss_sum, grad_sum = carry
            l_k, (gw_k, gx_k) = one_microbatch(xs[0], xs[1])
            with _jax.named_scope("update"):
                return (loss_sum + l_k, _jax.tree.map(_jnp.add, grad_sum, gw_k)), gx_k

        init = (_jnp.zeros((), _jnp.float32), _jax.tree.map(_jnp.zeros_like, weights))
        (loss, grad_w), grad_x = _jax.lax.scan(body, init, (per_example, given["loss_target"]))
    with _jax.named_scope("update"):
        delta_w, new_m, new_v = {}, {}, {}
        for n in TWIN_WEIGHTS:
            delta_w[n], new_m[n], new_v[n] = _adamw(weights[n], grad_w[n], given["m_" + n], given["v_" + n])
    return (loss, grad_x, *[grad_w[n] for n in TWIN_WEIGHTS], *[delta_w[n] for n in TWIN_WEIGHTS],
            *[new_m[n] for n in TWIN_WEIGHTS], *[new_v[n] for n in TWIN_WEIGHTS])
```

```python
import functools
import math

import jax
import jax.numpy as jnp
import numpy as np
from jax import lax
from jax.experimental import pallas as pl
from jax.experimental.pallas import tpu as pltpu

F32 = jnp.float32
BF16 = jnp.bfloat16

T = 2048
D = 2048
DEPTH = 4
N_DEV = 8
H_MLA, NOPE, ROPE, VDIM = 8, 128, 64, 128
QK = NOPE + ROPE
Q_LORA, KV_LORA = 512, 256
SSM_W, SSM_G, SSM_P, SSM_N = 512, 32, 16, 64
SSM_S = SSM_G * SSM_N
DIL_W, DIL_H, DIL_D = 512, 8, 64
BLK = 128
IN_SPLITS = (Q_LORA, KV_LORA, ROPE, SSM_W, DIL_W, DIL_W, DIL_W)
IN_W = sum(IN_SPLITS)
D_FF = 5632
EPS = 1e-6
ROPE_THETA = 10000.0
MLA_SCALE = QK ** -0.5
DIL_SCALE = DIL_D ** -0.5

ADAM_LR, ADAM_B1, ADAM_B2, ADAM_EPS, ADAM_WD, ADAM_STEP = 0.001, 0.9, 0.999, 1e-08, 0.01, 10

VMEM_LIMIT_V7X = 52 * 1024 * 1024
LANES = 128
PACK_C = 1024
ROW_BLOCK_BYTES = 2 * 1024 * 1024

NT = (((1,), (1,)), ((), ()))
TN = (((0,), (0,)), ((), ()))
HI = lax.Precision.HIGHEST
MESH = pl.DeviceIdType.MESH

PARAMS = ['g_mix', 'w_in', 'g_q', 'w_uq', 'g_kv', 'w_ukv', 'a_re', 'a_im', 'b_re', 'b_im', 'c_re', 'c_im',
          'd_skip', 'log_dt', 'w_glu', 'b_glu', 'g_out_mla', 'g_out_ssm', 'g_out_dil', 'w_o', 'g_ffn',
          'w_gate', 'w_up', 'w_down', 'g_final']
BIG = {'w_in': (D, IN_W, 1), 'w_uq': (Q_LORA, H_MLA * QK, 1), 'w_ukv': (KV_LORA, H_MLA * (NOPE + VDIM), 1),
       'w_glu': (SSM_W, 2 * SSM_W, 1), 'w_o': (D, D, 0), 'w_gate': (D, D_FF, 1), 'w_up': (D, D_FF, 1),
       'w_down': (D_FF, D, 0)}
BIG_ROWS = {n: k * m // N_DEV // PACK_C for n, (k, m, _) in BIG.items()}
PACK_R = sum(BIG_ROWS.values())
SMALL = {'g_mix': (D,), 'g_q': (Q_LORA,), 'g_kv': (KV_LORA,), 'a_re': (SSM_G, SSM_N), 'a_im': (SSM_G, SSM_N),
         'b_re': (SSM_G, SSM_N, SSM_P), 'b_im': (SSM_G, SSM_N, SSM_P), 'c_re': (SSM_G, SSM_P, SSM_N),
         'c_im': (SSM_G, SSM_P, SSM_N), 'd_skip': (SSM_G, SSM_P), 'log_dt': (SSM_G,), 'b_glu': (2 * SSM_W,),
         'g_out_mla': (H_MLA * VDIM,), 'g_out_ssm': (SSM_W,), 'g_out_dil': (DIL_W,), 'g_ffn': (D,)}
SMALL_N = DEPTH * sum(int(np.prod(s)) for s in SMALL.values()) + D
SMALL_R = 576


def _tile(dim, target, align=LANES):
    best = None
    for t in range(align, min(dim, target) + 1, align):
        if dim % t == 0:
            best = t
    return best if best is not None else dim


def _params(sem=None):
    return pltpu.CompilerParams(dimension_semantics=sem, vmem_limit_bytes=VMEM_LIMIT_V7X)


def _dot(a, b, dims=None, prec=None):
    if dims is None:
        return jnp.dot(a, b, preferred_element_type=F32, precision=prec)
    return lax.dot_general(a, b, dims, preferred_element_type=F32, precision=prec)


def _mm(name, a, b, mode, res=None, prec=None):
    if mode == 'nn':
        (M, K), (K2, N) = a.shape, b.shape
    elif mode == 'nt':
        (M, K), (N, K2) = a.shape, b.shape
    else:
        (K, M), (K2, N) = a.shape, b.shape
    assert K == K2, (name, a.shape, b.shape, mode)
    tn = _tile(N, 512)
    tm = _tile(M, 512 if tn > 1024 else 1024)
    tk = _tile(K, 512)
    nk = K // tk
    dims = {'nn': None, 'nt': NT, 'tn': TN}[mode]

    def body(*refs):
        if res is None:
            a_ref, b_ref, o_ref, acc_ref = refs
        else:
            a_ref, b_ref, r_ref, o_ref, acc_ref = refs
        k = pl.program_id(2)

        @pl.when(k == 0)
        def _():
            acc_ref[...] = jnp.zeros_like(acc_ref)

        acc_ref[...] += _dot(a_ref[...], b_ref[...], dims, prec)

        @pl.when(k == nk - 1)
        def _():
            if res is None:
                o_ref[...] = acc_ref[...]
            else:
                o_ref[...] = acc_ref[...] + r_ref[...]

    a_spec = {'nn': pl.BlockSpec((tm, tk), lambda i, j, k: (i, k)),
              'nt': pl.BlockSpec((tm, tk), lambda i, j, k: (i, k)),
              'tn': pl.BlockSpec((tk, tm), lambda i, j, k: (k, i))}[mode]
    b_spec = {'nn': pl.BlockSpec((tk, tn), lambda i, j, k: (k, j)),
              'nt': pl.BlockSpec((tn, tk), lambda i, j, k: (j, k)),
              'tn': pl.BlockSpec((tk, tn), lambda i, j, k: (k, j))}[mode]
    o_spec = pl.BlockSpec((tm, tn), lambda i, j, k: (i, j))
    in_specs = [a_spec, b_spec] + ([o_spec] if res is not None else [])
    args = (a, b) + ((res,) if res is not None else ())
    return pl.pallas_call(
        body, name=name, out_shape=jax.ShapeDtypeStruct((M, N), F32),
        grid=(M // tm, N // tn, nk), in_specs=in_specs, out_specs=o_spec,
        scratch_shapes=[pltpu.VMEM((tm, tn), F32)],
        compiler_params=_params(("parallel", "parallel", "arbitrary")),
    )(*args)


def _row(a, tr=None, axis=-2):
    axis = axis % a.ndim
    n = a.shape[axis]
    if tr is None:
        row_bytes = a.size // n * 4
        tr = _tile(n, max(8, min(256, ROW_BLOCK_BYTES // row_bytes)), 8)
    return (a, axis, tr)


def _row_spec(shape, axis, tr):
    nd = len(shape)
    blk = tuple(tr if d == axis else s for d, s in enumerate(shape))
    return pl.BlockSpec(blk, lambda i: tuple(i if d == axis else 0 for d in range(nd)))


def _full_spec(shape):
    nd = len(shape)
    return pl.BlockSpec(tuple(shape), lambda i: (0,) * nd)


def _steps(entries):
    ns = {a.shape[ax] // tr for a, ax, tr in entries}
    assert len(ns) == 1, [(a.shape, ax, tr) for a, ax, tr in entries]
    return ns.pop()


def _as_tuple(r):
    return tuple(r) if isinstance(r, (tuple, list)) else (r,)


def _rows_fwd(name, fn, rows, bcast, outs):
    steps = _steps(rows)
    nr, nb = len(rows), len(bcast)

    def body(*refs):
        vals = [r[...] for r in refs[:nr + nb]]
        res = _as_tuple(fn(*vals))
        for o_ref, r in zip(refs[nr + nb:], res):
            o_ref[...] = r.astype(o_ref.dtype)

    in_specs = [_row_spec(a.shape, ax, tr) for a, ax, tr in rows] + [_full_spec(b.shape) for b in bcast]
    out_specs = [_row_spec(s, ax % len(s), tr) for s, _, ax, tr in outs]
    res = pl.pallas_call(
        body, name=name, out_shape=[jax.ShapeDtypeStruct(s, dt) for s, dt, _, _ in outs],
        grid=(steps,), in_specs=in_specs, out_specs=out_specs,
        compiler_params=_params(("parallel",)),
    )(*[a for a, _, _ in rows], *bcast)
    return res


def _rows_vjp(name, fn, drows, dbc, arows, cts, primal=False):
    entries = list(drows) + list(arows) + list(cts)
    steps = _steps(entries)
    ndr, ndb, nar, nct = len(drows), len(dbc), len(arows), len(cts)

    def body(*refs):
        p = 0
        dr = [r[...] for r in refs[p:p + ndr]]; p += ndr
        db = [r[...] for r in refs[p:p + ndb]]; p += ndb
        ar = [r[...] for r in refs[p:p + nar]]; p += nar
        ct = [r[...] for r in refs[p:p + nct]]; p += nct
        g_rows = refs[p:p + ndr]; p += ndr
        g_bc = refs[p:p + ndb]; p += ndb
        prim_refs = refs[p:]

        def f(*d):
            return _as_tuple(fn(*d, *ar))

        outs, pullback = jax.vjp(f, *dr, *db)
        grads = pullback(tuple(c.astype(o.dtype) for c, o in zip(ct, outs)))
        for k in range(ndr):
            g_rows[k][...] = grads[k].astype(g_rows[k].dtype)
        if ndb:
            @pl.when(pl.program_id(0) == 0)
            def _():
                for r in g_bc:
                    r[...] = jnp.zeros_like(r)
            for k in range(ndb):
                g_bc[k][...] += grads[ndr + k]
        for r, o in zip(prim_refs, outs):
            r[...] = o.astype(r.dtype)

    in_specs = ([_row_spec(a.shape, ax, tr) for a, ax, tr in drows] + [_full_spec(b.shape) for b in dbc]
                + [_row_spec(a.shape, ax, tr) for a, ax, tr in arows]
                + [_row_spec(a.shape, ax, tr) for a, ax, tr in cts])
    out_shape = ([jax.ShapeDtypeStruct(a.shape, F32) for a, _, _ in drows]
                 + [jax.ShapeDtypeStruct(b.shape, F32) for b in dbc])
    out_specs = ([_row_spec(a.shape, ax, tr) for a, ax, tr in drows] + [_full_spec(b.shape) for b in dbc])
    if primal:
        out_shape += [jax.ShapeDtypeStruct(a.shape, F32) for a, _, _ in cts]
        out_specs += [_row_spec(a.shape, ax, tr) for a, ax, tr in cts]
    return pl.pallas_call(
        body, name=name, out_shape=out_shape, grid=(steps,), in_specs=in_specs, out_specs=out_specs,
        compiler_params=_params(("arbitrary",)),
    )(*[a for a, _, _ in drows], *dbc, *[a for a, _, _ in arows], *[a for a, _, _ in cts])


def _rms_fn(x, g):
    return x * lax.rsqrt(jnp.mean(x * x, axis=-1, keepdims=True) + EPS) * g


def _rope_fn(x1, x2, cos, sin):
    return x1 * cos - x2 * sin, x2 * cos + x1 * sin


def _s5_act_fn(ymm, u, d):
    return jax.nn.gelu(ymm + d * u)


def _glu_fn(z1, z2, b1, b2):
    return (z1 + b1) * jax.nn.sigmoid(z2 + b2)


def _outnorm_fn(ym, ys, yd, gm, gs, gd):
    return jnp.concatenate([_rms_fn(ym, gm), _rms_fn(ys, gs), _rms_fn(yd, gd)], axis=-1)


def _swiglu_fn(a, b):
    return jax.nn.silu(a) * b


def _loss_fn(x, g, tgt):
    err = _rms_fn(x, g) - tgt
    return 0.5 * jnp.mean(err * err, axis=-1, keepdims=True)


def _dil_mix_fn(o0, o1, o2, l0, l1, l2):
    m = jnp.maximum(jnp.maximum(l0, l1), l2)
    e0, e1, e2 = jnp.exp(l0 - m), jnp.exp(l1 - m), jnp.exp(l2 - m)
    s = e0 + e1 + e2
    return (e0 / s) * o0 + (e1 / s) * o1 + (e2 / s) * o2


def _s5_disc_fn(a_re, a_im, ldt, b_r, b_i):
    lr = jnp.minimum(a_re.reshape(1, SSM_N), -1e-4)
    li = a_im.reshape(1, SSM_N)
    dt = jnp.exp(ldt.reshape(1, 1))
    e = jnp.exp(lr * dt)
    ar = e * jnp.cos(li * dt)
    ai = e * jnp.sin(li * dt)
    nr, ni = ar - 1.0, ai
    den = lr * lr + li * li
    cr = (nr * lr + ni * li) / den
    ci = (ni * lr - nr * li) / den
    return ar.reshape(1, 1, SSM_N), ai.reshape(1, 1, SSM_N), cr * b_r - ci * b_i, cr * b_i + ci * b_r


def _adamw_fn(w, g, m, v):
    m = ADAM_B1 * m + (1.0 - ADAM_B1) * g
    v = ADAM_B2 * v + (1.0 - ADAM_B2) * jnp.square(g)
    m_hat = m / (1.0 - ADAM_B1 ** ADAM_STEP)
    v_hat = v / (1.0 - ADAM_B2 ** ADAM_STEP)
    delta = -ADAM_LR * (m_hat / (jnp.sqrt(v_hat) + ADAM_EPS) + ADAM_WD * w)
    return delta, m, v


def _rms_fwd(name, x, g):
    (h,) = _rows_fwd(name, _rms_fn, [_row(x)], [g], [(x.shape, BF16, -2, _row(x)[2])])
    return h


def _rms_res_fn(x, g):
    return _rms_fn(x, g), x


def _rms_bwd(name, x, g, dh, dres=None):
    if dres is None:
        dx, dg = _rows_vjp(name, _rms_fn, [_row(x)], [g], [], [_row(dh)])
    else:
        dx, dg = _rows_vjp(name, _rms_res_fn, [_row(x)], [g], [], [_row(dh), _row(dres)])
    return dx, dg


MLA_TQ = 256


def _mla_fwd(q, k, v):
    tq = MLA_TQ

    def body(q_ref, k_ref, v_ref, o_ref, lse_ref):
        i = pl.program_id(1)
        s = _dot(q_ref[0], k_ref[0], NT) * MLA_SCALE
        row = i * tq + lax.broadcasted_iota(jnp.int32, (tq, T), 0)
        col = lax.broadcasted_iota(jnp.int32, (tq, T), 1)
        s = jnp.where(row >= col, s, -jnp.inf)
        m = jnp.max(s, axis=-1, keepdims=True)
        p = jnp.exp(s - m)
        l = jnp.sum(p, axis=-1, keepdims=True)
        o_ref[0] = _dot((p / l).astype(BF16), v_ref[0])
        lse_ref[0] = m + jnp.log(l)

    return pl.pallas_call(
        body, name="mla_fwd",
        out_shape=[jax.ShapeDtypeStruct((H_MLA, T, VDIM), F32), jax.ShapeDtypeStruct((H_MLA, T, 1), F32)],
        grid=(H_MLA, T // tq),
        in_specs=[pl.BlockSpec((1, tq, QK), lambda h, i: (h, i, 0)),
                  pl.BlockSpec((1, T, QK), lambda h, i: (h, 0, 0)),
                  pl.BlockSpec((1, T, VDIM), lambda h, i: (h, 0, 0))],
        out_specs=[pl.BlockSpec((1, tq, VDIM), lambda h, i: (h, i, 0)),
                   pl.BlockSpec((1, tq, 1), lambda h, i: (h, i, 0))],
        compiler_params=_params(("parallel", "parallel")),
    )(q, k, v)


def _mla_bwd(q, k, v, do, lse):
    tq = MLA_TQ

    def body(q_ref, k_ref, v_ref, do_ref, lse_ref, dq_ref, dk_ref, dv_ref, dkpe_ref):
        h, i = pl.program_id(0), pl.program_id(1)

        @pl.when(i == 0)
        def _():
            dk_ref[...] = jnp.zeros_like(dk_ref)
            dv_ref[...] = jnp.zeros_like(dv_ref)

        @pl.when((i == 0) & (h == 0))
        def _():
            dkpe_ref[...] = jnp.zeros_like(dkpe_ref)

        q, k, v = q_ref[0], k_ref[0], v_ref[0]
        do = do_ref[0].astype(BF16)
        s = _dot(q, k, NT) * MLA_SCALE
        row = i * tq + lax.broadcasted_iota(jnp.int32, (tq, T), 0)
        col = lax.broadcasted_iota(jnp.int32, (tq, T), 1)
        p = jnp.where(row >= col, jnp.exp(s - lse_ref[0]), 0.0)
        dp = _dot(do, v, NT)
        delta = jnp.sum(dp * p, axis=-1, keepdims=True)
        ds = (p * (dp - delta) * MLA_SCALE).astype(BF16)
        dq_ref[0] = _dot(ds, k)
        dk = _dot(ds, q, TN)
        dk_ref[0] += dk
        dkpe_ref[...] += dk[:, NOPE:]
        dv_ref[0] += _dot(p.astype(BF16), do, TN)

    return pl.pallas_call(
        body, name="mla_bwd",
        out_shape=[jax.ShapeDtypeStruct((H_MLA, T, QK), F32), jax.ShapeDtypeStruct((H_MLA, T, QK), F32),
                   jax.ShapeDtypeStruct((H_MLA, T, VDIM), F32), jax.ShapeDtypeStruct((T, ROPE), F32)],
        grid=(H_MLA, T // tq),
        in_specs=[pl.BlockSpec((1, tq, QK), lambda h, i: (h, i, 0)),
                  pl.BlockSpec((1, T, QK), lambda h, i: (h, 0, 0)),
                  pl.BlockSpec((1, T, VDIM), lambda h, i: (h, 0, 0)),
                  pl.BlockSpec((1, tq, VDIM), lambda h, i: (h, i, 0)),
                  pl.BlockSpec((1, tq, 1), lambda h, i: (h, i, 0))],
        out_specs=[pl.BlockSpec((1, tq, QK), lambda h, i: (h, i, 0)),
                   pl.BlockSpec((1, T, QK), lambda h, i: (h, 0, 0)),
                   pl.BlockSpec((1, T, VDIM), lambda h, i: (h, 0, 0)),
                   pl.BlockSpec((T, ROPE), lambda h, i: (0, 0))],
        compiler_params=_params(("arbitrary", "arbitrary")),
    )(q, k, v, do, lse)


NBLK = T // BLK


def _band_masks():
    r = lax.broadcasted_iota(jnp.int32, (BLK, BLK), 0)
    j = lax.broadcasted_iota(jnp.int32, (BLK, BLK), 1)
    return j <= r, j >= r


def _seq_start(p, i):
    per_seq = lax.shift_right_logical(jnp.int32(NBLK), 2 * p)
    return lax.rem(i, per_seq) == 0


def _band_fwd(q, k, v):
    def body(q_ref, kp_ref, kc_ref, vp_ref, vc_ref, o_ref, lse_ref):
        p, i = pl.program_id(0), pl.program_id(1)
        has_prev = jnp.logical_not(_seq_start(p, i))
        m_cur, m_prev = _band_masks()
        m_prev = m_prev & has_prev
        for h in range(DIL_H):
            qh = q_ref[0, h]
            s_c = jnp.where(m_cur, _dot(qh, kc_ref[0, h], NT) * DIL_SCALE, -jnp.inf)
            s_p = jnp.where(m_prev, _dot(qh, kp_ref[0, h], NT) * DIL_SCALE, -jnp.inf)
            m = jnp.maximum(jnp.max(s_c, axis=-1, keepdims=True), jnp.max(s_p, axis=-1, keepdims=True))
            e_c, e_p = jnp.exp(s_c - m), jnp.exp(s_p - m)
            l = jnp.sum(e_c, axis=-1, keepdims=True) + jnp.sum(e_p, axis=-1, keepdims=True)
            o_ref[0, h] = (_dot((e_p / l).astype(BF16), vp_ref[0, h]) + _dot((e_c / l).astype(BF16), vc_ref[0, h]))
            lse_ref[0, h] = m + jnp.log(l)

    blk = lambda w: (1, DIL_H, BLK, w)
    cur = lambda p, i: (p, 0, i, 0)
    prev = lambda p, i: (p, 0, jnp.maximum(i - 1, 0), 0)
    return pl.pallas_call(
        body, name="band_fwd",
        out_shape=[jax.ShapeDtypeStruct((3, DIL_H, T, DIL_D), F32), jax.ShapeDtypeStruct((3, DIL_H, T, 1), F32)],
        grid=(3, NBLK),
        in_specs=[pl.BlockSpec(blk(DIL_D), cur), pl.BlockSpec(blk(DIL_D), prev), pl.BlockSpec(blk(DIL_D), cur),
                  pl.BlockSpec(blk(DIL_D), prev), pl.BlockSpec(blk(DIL_D), cur)],
        out_specs=[pl.BlockSpec(blk(DIL_D), cur), pl.BlockSpec(blk(1), cur)],
        compiler_params=_params(("parallel", "parallel")),
    )(q, k, k, v, v)


def _band_bwd(q, k, v, o, lse, do, dlse):
    def body(qc_ref, qn_ref, kp_ref, kc_ref, vp_ref, vc_ref, oc_ref, on_ref, lc_ref, ln_ref,
             doc_ref, don_ref, dlc_ref, dln_ref, dq_ref, dk_ref, dv_ref):
        p, i = pl.program_id(0), pl.program_id(1)
        has_prev = jnp.logical_not(_seq_start(p, i))
        has_next = jnp.logical_not(_seq_start(p, i + 1)) & (i + 1 < NBLK)
        m_cur, m_prev = _band_masks()

        def probs(qh, kh, lse_h, mask):
            return jnp.where(mask, jnp.exp(_dot(qh, kh, NT) * DIL_SCALE - lse_h), 0.0)

        def dscore(pr, do_h, vh, shift):
            return (pr * (_dot(do_h, vh, NT) + shift) * DIL_SCALE).astype(BF16)

        for h in range(DIL_H):
            qc, qn, kp, kc, vp, vc = qc_ref[0, h], qn_ref[0, h], kp_ref[0, h], kc_ref[0, h], vp_ref[0, h], vc_ref[0, h]
            doc, don = doc_ref[0, h], don_ref[0, h]
            sh_c = dlc_ref[0, h] - jnp.sum(doc * oc_ref[0, h], axis=-1, keepdims=True)
            sh_n = dln_ref[0, h] - jnp.sum(don * on_ref[0, h], axis=-1, keepdims=True)
            doc, don = doc.astype(BF16), don.astype(BF16)
            p_cc = probs(qc, kc, lc_ref[0, h], m_cur)
            p_cp = probs(qc, kp, lc_ref[0, h], m_prev & has_prev)
            p_nc = probs(qn, kc, ln_ref[0, h], m_prev & has_next)
            ds_cc = dscore(p_cc, doc, vc, sh_c)
            ds_cp = dscore(p_cp, doc, vp, sh_c)
            ds_nc = dscore(p_nc, don, vc, sh_n)
            dq_ref[0, h] = _dot(ds_cc, kc) + _dot(ds_cp, kp)
            dk_ref[0, h] = _dot(ds_cc, qc, TN) + _dot(ds_nc, qn, TN)
            dv_ref[0, h] = _dot(p_cc.astype(BF16), doc, TN) + _dot(p_nc.astype(BF16), don, TN)

    blk = lambda w: (1, DIL_H, BLK, w)
    cur = lambda p, i: (p, 0, i, 0)
    prev = lambda p, i: (p, 0, jnp.maximum(i - 1, 0), 0)
    nxt = lambda p, i: (p, 0, jnp.minimum(i + 1, NBLK - 1), 0)
    w, one = pl.BlockSpec(blk(DIL_D), cur), pl.BlockSpec(blk(1), cur)
    wn, onen = pl.BlockSpec(blk(DIL_D), nxt), pl.BlockSpec(blk(1), nxt)
    wp = pl.BlockSpec(blk(DIL_D), prev)
    return pl.pallas_call(
        body, name="band_bwd",
        out_shape=[jax.ShapeDtypeStruct((3, DIL_H, T, DIL_D), F32)] * 3,
        grid=(3, NBLK),
        in_specs=[w, wn, wp, w, wp, w, w, wn, one, onen, w, wn, one, onen],
        out_specs=[w, w, w],
        compiler_params=_params(("parallel", "parallel")),
    )(q, q, k, k, v, v, o, o, lse, lse, do, do, dlse, dlse)


SCAN_TC = 256


def _scan_fwd(bu, ar, ai):
    tc, S = SCAN_TC, SSM_S

    def body(bu_ref, ar_ref, ai_ref, h_ref, cr_ref, ci_ref):
        @pl.when(pl.program_id(0) == 0)
        def _():
            cr_ref[...] = jnp.zeros_like(cr_ref)
            ci_ref[...] = jnp.zeros_like(ci_ref)

        a_r, a_i = ar_ref[...], ai_ref[...]

        def step(j, carry):
            hr, hi = carry
            for r in range(8):
                t = pl.multiple_of(j * 8, 8) + r
                br = bu_ref[pl.ds(t, 1), pl.ds(0, S)]
                bi = bu_ref[pl.ds(t, 1), pl.ds(S, S)]
                hr, hi = a_r * hr - a_i * hi + br, a_r * hi + a_i * hr + bi
                h_ref[pl.ds(t, 1), pl.ds(0, S)] = hr
                h_ref[pl.ds(t, 1), pl.ds(S, S)] = hi
            return hr, hi

        hr, hi = lax.fori_loop(0, tc // 8, step, (cr_ref[...], ci_ref[...]))
        cr_ref[...] = hr
        ci_ref[...] = hi

    return pl.pallas_call(
        body, name="s5_scan_fwd", out_shape=jax.ShapeDtypeStruct((T, 2 * S), F32),
        grid=(T // tc,),
        in_specs=[pl.BlockSpec((tc, 2 * S), lambda i: (i, 0)), _full_spec((1, S)), _full_spec((1, S))],
        out_specs=pl.BlockSpec((tc, 2 * S), lambda i: (i, 0)),
        scratch_shapes=[pltpu.VMEM((1, S), F32), pltpu.VMEM((1, S), F32)],
        compiler_params=_params(("arbitrary",)),
    )(bu, ar, ai)


def _scan_bwd(dh, h, ar, ai):
    tc, S = SCAN_TC, SSM_S
    nc = T // tc

    def body(dh_ref, h_ref, hp_ref, ar_ref, ai_ref, g_ref, dar_ref, dai_ref, cr_ref, ci_ref):
        i = pl.program_id(0)

        @pl.when(i == 0)
        def _():
            cr_ref[...] = jnp.zeros_like(cr_ref)
            ci_ref[...] = jnp.zeros_like(ci_ref)
            dar_ref[...] = jnp.zeros_like(dar_ref)
            dai_ref[...] = jnp.zeros_like(dai_ref)

        a_r, a_i = ar_ref[...], ai_ref[...]
        first_chunk = (i == nc - 1)
        edge = jnp.where(first_chunk, 0.0, 1.0)
        hpr = hp_ref[pl.ds(7, 1), pl.ds(0, S)] * edge
        hpi = hp_ref[pl.ds(7, 1), pl.ds(S, S)] * edge

        def step(jj, carry):
            gr, gi, dar, dai = carry
            j = tc // 8 - 1 - jj
            for r in range(7, -1, -1):
                t = pl.multiple_of(j * 8, 8) + r
                tp = jnp.maximum(t - 1, 0)
                inside = t > 0
                pr = jnp.where(inside, h_ref[pl.ds(tp, 1), pl.ds(0, S)], hpr)
                pi = jnp.where(inside, h_ref[pl.ds(tp, 1), pl.ds(S, S)], hpi)
                gr, gi = (dh_ref[pl.ds(t, 1), pl.ds(0, S)] + a_r * gr + a_i * gi,
                          dh_ref[pl.ds(t, 1), pl.ds(S, S)] + a_r * gi - a_i * gr)
                g_ref[pl.ds(t, 1), pl.ds(0, S)] = gr
                g_ref[pl.ds(t, 1), pl.ds(S, S)] = gi
                dar = dar + gr * pr + gi * pi
                dai = dai + gi * pr - gr * pi
            return gr, gi, dar, dai

        zero = jnp.zeros((1, S), F32)
        gr, gi, dar, dai = lax.fori_loop(0, tc // 8, step, (cr_ref[...], ci_ref[...], zero, zero))
        cr_ref[...] = gr
        ci_ref[...] = gi
        dar_ref[...] += dar
        dai_ref[...] += dai

    rev = lambda i: (nc - 1 - i, 0)
    before = lambda i: (jnp.maximum((nc - 1 - i) * (tc // 8) - 1, 0), 0)
    return pl.pallas_call(
        body, name="s5_scan_bwd",
        out_shape=[jax.ShapeDtypeStruct((T, 2 * S), F32), jax.ShapeDtypeStruct((1, S), F32),
                   jax.ShapeDtypeStruct((1, S), F32)],
        grid=(nc,),
        in_specs=[pl.BlockSpec((tc, 2 * S), rev), pl.BlockSpec((tc, 2 * S), rev), pl.BlockSpec((8, 2 * S), before),
                  _full_spec((1, S)), _full_spec((1, S))],
        out_specs=[pl.BlockSpec((tc, 2 * S), rev), _full_spec((1, S)), _full_spec((1, S))],
        scratch_shapes=[pltpu.VMEM((1, S), F32), pltpu.VMEM((1, S), F32)],
        compiler_params=_params(("arbitrary",)),
    )(dh, h, h, ar, ai)


def _sum_rows(name, x):
    def body(x_ref, o_ref):
        o_ref[...] = jnp.sum(x_ref[...], axis=0, keepdims=True)

    return pl.pallas_call(body, name=name, out_shape=jax.ShapeDtypeStruct((1, 1), F32),
                          in_specs=[_full_spec(x.shape)], out_specs=_full_spec((1, 1)), grid=(1,))(x)


ANY = pl.BlockSpec(memory_space=pl.ANY)


def _place():
    return lax.axis_index("x"), lax.axis_index("y"), lax.axis_index("c")


def _all_gather(name, xs):
    def body(x_ref, out_ref, send_sems, recv_sems, local_sem):
        x, y, c = _place()
        me, sibling = (x, y, c), (x, y, 1 - c)
        chips = [(1 - x, y), (x, 1 - y), (1 - x, 1 - y)]

        def slot(px, py, pc):
            return out_ref.at[4 * px + 2 * py + pc]

        def copy(k, block, to, src=None):
            return pltpu.make_async_remote_copy(
                src_ref=slot(*block) if src is None else src, dst_ref=slot(*block),
                send_sem=send_sems.at[k], recv_sem=recv_sems.at[k], device_id=to, device_id_type=MESH)

        mine = pltpu.make_async_copy(x_ref, slot(*me), local_sem)
        mine.start()
        first = [copy(0, me, sibling, src=x_ref)]
        first += [copy(1 + j, me, (*chip, c), src=x_ref) for j, chip in enumerate(chips)]
        for cp in first:
            cp.start()
        passed = [copy(4 + j, (*chip, c), sibling) for j, chip in enumerate(chips)]
        for j, chip in enumerate(chips):
            copy(1 + j, (*chip, c), me).wait_recv()
            passed[j].start()
        copy(0, sibling, me).wait_recv()
        for j, chip in enumerate(chips):
            copy(4 + j, (*chip, 1 - c), me).wait_recv()
        for cp in first + passed:
            cp.wait_send()
        mine.wait()

    return pl.pallas_call(
        body, name=name, out_shape=jax.ShapeDtypeStruct((N_DEV,) + xs.shape, xs.dtype),
        in_specs=[ANY], out_specs=ANY,
        scratch_shapes=[pltpu.SemaphoreType.DMA((7,)), pltpu.SemaphoreType.DMA((7,)), pltpu.SemaphoreType.DMA],
    )(xs)


def _swap_sibling(name, p4):
    _, _, R, C = p4.shape

    def body(p_ref, out_ref, send_sems, recv_sems):
        x, y, c = _place()
        cps = [pltpu.make_async_remote_copy(
            src_ref=p_ref.at[j, 1 - c], dst_ref=out_ref.at[j], send_sem=send_sems.at[j], recv_sem=recv_sems.at[j],
            device_id=(x, y, 1 - c), device_id_type=MESH) for j in range(4)]
        for cp in cps:
            cp.start()
        for cp in cps:
            cp.wait()

    return pl.pallas_call(
        body, name=name, out_shape=jax.ShapeDtypeStruct((4, R, C), p4.dtype), in_specs=[ANY], out_specs=ANY,
        scratch_shapes=[pltpu.SemaphoreType.DMA((4,)), pltpu.SemaphoreType.DMA((4,))],
    )(p4)


def _swap_chips(name, s4):
    _, R, C = s4.shape

    def body(s_ref, out_ref, send_sems, recv_sems):
        x, y, c = _place()
        chips = [(1 - x, y), (x, 1 - y), (1 - x, 1 - y)]
        cps = [pltpu.make_async_remote_copy(
            src_ref=s_ref.at[2 * px + py], dst_ref=out_ref.at[k], send_sem=send_sems.at[k], recv_sem=recv_sems.at[k],
            device_id=(px, py, c), device_id_type=MESH) for k, (px, py) in enumerate(chips)]
        for cp in cps:
            cp.start()
        for cp in cps:
            cp.wait()

    return pl.pallas_call(
        body, name=name, out_shape=jax.ShapeDtypeStruct((3, R, C), s4.dtype), in_specs=[ANY], out_specs=ANY,
        scratch_shapes=[pltpu.SemaphoreType.DMA((3,)), pltpu.SemaphoreType.DMA((3,))],
    )(s4)


def _pair_sum(name, p4, got, core):
    _, _, R, C = p4.shape
    tr = _tile(R, 640, 8)

    def body(core_ref, p_ref, g_ref, o_ref):
        o_ref[...] = p_ref[:, 0] + g_ref[...]

    return pl.pallas_call(
        body, name=name, out_shape=jax.ShapeDtypeStruct((4, R, C), F32),
        grid_spec=pltpu.PrefetchScalarGridSpec(
            num_scalar_prefetch=1, grid=(4, R // tr),
            in_specs=[pl.BlockSpec((1, 1, tr, C), lambda j, i, core: (j, core[0], i, 0)),
                      pl.BlockSpec((1, tr, C), lambda j, i, core: (j, i, 0))],
            out_specs=pl.BlockSpec((1, tr, C), lambda j, i, core: (j, i, 0))),
        compiler_params=_params(("parallel", "parallel")),
    )(core, p4, got)


def _chip_sum(name, s4, got, chip):
    _, R, C = s4.shape
    tr = _tile(R, 640, 8)

    def body(chip_ref, s_ref, g_ref, o_ref):
        o_ref[...] = ((s_ref[0] + g_ref[0]) + g_ref[1]) + g_ref[2]

    return pl.pallas_call(
        body, name=name, out_shape=jax.ShapeDtypeStruct((R, C), F32),
        grid_spec=pltpu.PrefetchScalarGridSpec(
            num_scalar_prefetch=1, grid=(R // tr,),
            in_specs=[pl.BlockSpec((1, tr, C), lambda i, chip: (chip[0], i, 0)),
                      pl.BlockSpec((3, tr, C), lambda i, chip: (0, i, 0))],
            out_specs=pl.BlockSpec((tr, C), lambda i, chip: (i, 0))),
        compiler_params=_params(("parallel",)),
    )(chip, s4, got)


def _sum_devices(name, g8):
    _, R, C = g8.shape
    tr = _tile(R, 64, 8)

    def body(g_ref, o_ref):
        acc = g_ref[0]
        for d in range(1, N_DEV):
            acc = acc + g_ref[d]
        o_ref[...] = acc

    return pl.pallas_call(
        body, name=name, out_shape=jax.ShapeDtypeStruct((R, C), F32), grid=(R // tr,),
        in_specs=[pl.BlockSpec((N_DEV, tr, C), lambda i: (0, i, 0))], out_specs=pl.BlockSpec((tr, C), lambda i: (i, 0)),
        compiler_params=_params(("parallel",)),
    )(g8)


def _reduce_scatter(p8, core, chip):
    R, C = p8.shape[1:]
    p4 = p8.reshape(4, 2, R, C)
    s4 = _pair_sum("rs_pair_sum", p4, _swap_sibling("rs_swap_sibling", p4), core)
    return _chip_sum("rs_chip_sum", s4, _swap_chips("rs_swap_chips", s4), chip)


def _pack_shards(shards):
    return jnp.concatenate([shards[n].reshape(-1, PACK_C) for n in BIG], axis=0)


def _unpack_full(g8):
    out, r0 = {}, 0
    for n, (k, m, ax) in BIG.items():
        blk = g8[:, r0:r0 + BIG_ROWS[n]]
        r0 += BIG_ROWS[n]
        if ax == 0:
            out[n] = blk.reshape(k, m)
        else:
            out[n] = blk.reshape(N_DEV, k, m // N_DEV).transpose(1, 0, 2).reshape(k, m)
    return out


def _pack_full(full):
    parts = []
    for n, (k, m, ax) in BIG.items():
        a = full[n]
        if ax == 1:
            a = a.reshape(k, N_DEV, m // N_DEV).transpose(1, 0, 2)
        parts.append(a.reshape(N_DEV, BIG_ROWS[n], PACK_C))
    return jnp.concatenate(parts, axis=1)


def _unpack_shard(blk):
    out, r0 = {}, 0
    for n, (k, m, ax) in BIG.items():
        part = blk[r0:r0 + BIG_ROWS[n]]
        r0 += BIG_ROWS[n]
        out[n] = part.reshape((k // N_DEV, m) if ax == 0 else (k, m // N_DEV))
    return out


def _block_diag(v):
    eye = jnp.eye(SSM_G, dtype=v.dtype)
    return (eye[:, None, :, None] * v[:, :, None, :]).reshape(SSM_G * SSM_P, SSM_G * SSM_N)


def _diag_blocks(m):
    m4 = m.reshape(SSM_G, SSM_P, SSM_G, SSM_N)
    eye = jnp.eye(SSM_G, dtype=m.dtype)
    return (m4 * eye[:, None, :, None]).sum(axis=2)


def _fold(a, dil):
    if dil == 1:
        return a
    return a.reshape((T // dil, dil) + a.shape[1:]).swapaxes(0, 1).reshape(a.shape)


def _unfold(a, dil):
    if dil == 1:
        return a
    return a.reshape((dil, T // dil) + a.shape[1:]).swapaxes(0, 1).reshape(a.shape)


DILS = (1, 4, 16)


def _fold3(a):
    return jnp.stack([_fold(a, d) for d in DILS]).transpose(0, 2, 1, 3)


def _unfold3(a):
    return jnp.stack([_unfold(a[p].swapaxes(0, 1), d).swapaxes(0, 1) for p, d in enumerate(DILS)])


def _refold3(a):
    return jnp.stack([_fold(a[p].swapaxes(0, 1), d).swapaxes(0, 1) for p, d in enumerate(DILS)])


def _rope_tables():
    half = ROPE // 2
    inv_freq = ROPE_THETA ** (-jnp.arange(half, dtype=F32) / half)
    ang = jnp.arange(T).astype(F32)[:, None] * inv_freq[None, :]
    reps = H_MLA + 1
    return jnp.tile(jnp.cos(ang), (1, reps)), jnp.tile(jnp.sin(ang), (1, reps))


def _layer_fwd(x, w, sp, cos9, sin9):
    s = {}
    s['x'] = x
    h = _rms_fwd("rms_mix", x, sp['g_mix'])
    proj = _mm("mm_in", h, w['w_in'], 'nn')
    offs = np.cumsum((0,) + IN_SPLITS)
    c_q, c_kv, k_rope, u, qd, kd, vd = [proj[:, offs[i]:offs[i + 1]] for i in range(7)]
    s.update(h=h, c_q=c_q, c_kv=c_kv, u=u)

    cqn = _rms_fwd("rms_q", c_q, sp['g_q'])
    ckvn = _rms_fwd("rms_kv", c_kv, sp['g_kv'])
    q = _mm("mm_uq", cqn, w['w_uq'], 'nn').reshape(T, H_MLA, QK)
    kv = _mm("mm_ukv", ckvn, w['w_ukv'], 'nn').reshape(T, H_MLA, NOPE + VDIM)
    half = ROPE // 2
    x1 = jnp.concatenate([q[:, :, NOPE:NOPE + half].reshape(T, -1), k_rope[:, :half]], axis=1)
    x2 = jnp.concatenate([q[:, :, NOPE + half:].reshape(T, -1), k_rope[:, half:]], axis=1)
    tr = _row(x1)[2]
    o1, o2 = _rows_fwd("rope", _rope_fn, [_row(x1), _row(x2), _row(cos9), _row(sin9)], [],
                       [(x1.shape, F32, -2, tr), (x2.shape, F32, -2, tr)])
    nq = H_MLA * half
    q_pe = jnp.concatenate([o1[:, :nq].reshape(T, H_MLA, half), o2[:, :nq].reshape(T, H_MLA, half)], axis=-1)
    k_pe = jnp.concatenate([o1[:, nq:], o2[:, nq:]], axis=-1)
    qh = jnp.concatenate([q[:, :, :NOPE], q_pe], axis=-1).transpose(1, 0, 2).astype(BF16)
    kh = jnp.concatenate([kv[:, :, :NOPE], jnp.broadcast_to(k_pe[:, None, :], (T, H_MLA, ROPE))],
                         axis=-1).transpose(1, 0, 2).astype(BF16)
    vh = kv[:, :, NOPE:].transpose(1, 0, 2).astype(BF16)
    o_mla, lse_mla = _mla_fwd(qh, kh, vh)
    y_mla = o_mla.transpose(1, 0, 2).reshape(T, H_MLA * VDIM)
    s.update(cqn=cqn, ckvn=ckvn, qh=qh, kh=kh, vh=vh, lse_mla=lse_mla)

    a3 = lambda n: sp[n].reshape(SSM_G, 1, SSM_N)
    b2 = lambda n: sp[n].transpose(0, 2, 1).reshape(SSM_G * SSM_P, SSM_N)
    disc_rows = [_row(a3('a_re'), 1, 0), _row(a3('a_im'), 1, 0), _row(sp['log_dt'].reshape(SSM_G, 1, 1), 1, 0),
                 _row(b2('b_re'), SSM_P, 0), _row(b2('b_im'), SSM_P, 0)]
    abr, abi, bbr, bbi = _rows_fwd(
        "s5_disc", _s5_disc_fn, disc_rows, [],
        [((SSM_G, 1, SSM_N), F32, 0, 1), ((SSM_G, 1, SSM_N), F32, 0, 1),
         ((SSM_G * SSM_P, SSM_N), F32, 0, SSM_P), ((SSM_G * SSM_P, SSM_N), F32, 0, SSM_P)])
    ar, ai = abr.reshape(1, SSM_S), abi.reshape(1, SSM_S)
    b_mat = jnp.concatenate([_block_diag(bbr.reshape(SSM_G, SSM_P, SSM_N)),
                             _block_diag(bbi.reshape(SSM_G, SSM_P, SSM_N))], axis=1)
    c_mat = jnp.concatenate([_block_diag(sp['c_re']), _block_diag(-sp['c_im'])], axis=1)
    bu = _mm("mm_s5_b", u, b_mat, 'nn', prec=HI)
    hst = _scan_fwd(bu, ar, ai)
    ymm = _mm("mm_s5_c", hst, c_mat, 'nt', prec=HI)
    d_row = sp['d_skip'].reshape(1, SSM_W)
    (yg,) = _rows_fwd("s5_act", _s5_act_fn, [_row(ymm), _row(u)], [d_row], [((T, SSM_W), BF16, -2, _row(u)[2])])
    z = _mm("mm_glu", yg, w['w_glu'], 'nn')
    glu_rows = [_row(z[:, :SSM_W]), _row(z[:, SSM_W:])]
    glu_b = [sp['b_glu'][:SSM_W].reshape(1, -1), sp['b_glu'][SSM_W:].reshape(1, -1)]
    (y_ssm,) = _rows_fwd("s5_glu", _glu_fn, glu_rows, glu_b, [((T, SSM_W), F32, -2, glu_rows[0][2])])
    s.update(disc_rows=disc_rows, ar=ar, ai=ai, b_mat=b_mat, c_mat=c_mat, hst=hst, ymm=ymm, d_row=d_row, yg=yg,
             glu_rows=glu_rows, glu_b=glu_b)

    f3 = lambda a: _fold3(a.reshape(T, DIL_H, DIL_D)).astype(BF16)
    qf, kf, vf = f3(qd), f3(kd), f3(vd)
    o_f, lse_f = _band_fwd(qf, kf, vf)
    o_n, lse_n = _unfold3(o_f), _unfold3(lse_f)
    mix_rows = [_row(o_n[p], 256, 1) for p in range(3)] + [_row(lse_n[p], 256, 1) for p in range(3)]
    (y_dh,) = _rows_fwd("dil_mix", _dil_mix_fn, mix_rows, [], [((DIL_H, T, DIL_D), F32, 1, 256)])
    y_dil = y_dh.transpose(1, 0, 2).reshape(T, DIL_W)
    s.update(qf=qf, kf=kf, vf=vf, o_f=o_f, lse_f=lse_f, mix_rows=mix_rows)

    gm, gs, gd = sp['g_out_mla'].reshape(1, -1), sp['g_out_ssm'].reshape(1, -1), sp['g_out_dil'].reshape(1, -1)
    on_rows = [_row(y_mla), _row(y_ssm), _row(y_dil)]
    (ycat,) = _rows_fwd("out_norm", _outnorm_fn, on_rows, [gm, gs, gd], [((T, D), BF16, -2, on_rows[0][2])])
    x1_ = _mm("mm_o", ycat, w['w_o'], 'nn', res=x)
    h2 = _rms_fwd("rms_ffn", x1_, sp['g_ffn'])
    ga = _mm("mm_gate", h2, w['w_gate'], 'nn')
    gb = _mm("mm_up", h2, w['w_up'], 'nn')
    (zf,) = _rows_fwd("swiglu", _swiglu_fn, [_row(ga), _row(gb)], [], [((T, D_FF), BF16, -2, _row(ga)[2])])
    x2_ = _mm("mm_down", zf, w['w_down'], 'nn', res=x1_)
    s.update(on_rows=on_rows, on_g=[gm, gs, gd], ycat=ycat, x1=x1_, h2=h2, ga=ga, gb=gb, zf=zf)
    return x2_, s


def _layer_bwd(dx2, s, w, sp, cos9, sin9):
    gw, gs_ = {}, {}
    b16 = lambda a: a.astype(BF16)
    dx2b = b16(dx2)
    dzf = _mm("mm_down_dx", dx2b, w['w_down'], 'nt')
    gw['w_down'] = _mm("mm_down_dw", s['zf'], dx2b, 'tn')
    dga, dgb = _rows_vjp("swiglu_bwd", _swiglu_fn, [_row(s['ga']), _row(s['gb'])], [], [], [_row(dzf)])
    dga, dgb = b16(dga), b16(dgb)
    gw['w_gate'] = _mm("mm_gate_dw", s['h2'], dga, 'tn')
    gw['w_up'] = _mm("mm_up_dw", s['h2'], dgb, 'tn')
    dh2 = _mm("mm_up_dx", dgb, w['w_up'], 'nt', res=_mm("mm_gate_dx", dga, w['w_gate'], 'nt'))
    dx1, gs_['g_ffn'] = _rms_bwd("rms_ffn_bwd", s['x1'], sp['g_ffn'], dh2, dx2)
    dx1b = b16(dx1)
    dycat = _mm("mm_o_dx", dx1b, w['w_o'], 'nt')
    gw['w_o'] = _mm("mm_o_dw", s['ycat'], dx1b, 'tn')
    dy_mla, dy_ssm, dy_dil, gs_['g_out_mla'], gs_['g_out_ssm'], gs_['g_out_dil'] = _rows_vjp(
        "out_norm_bwd", _outnorm_fn, s['on_rows'], s['on_g'], [], [_row(dycat)])

    do_h = dy_dil.reshape(T, DIL_H, DIL_D).transpose(1, 0, 2)
    dmix = _rows_vjp("dil_mix_bwd", _dil_mix_fn, s['mix_rows'], [], [], [_row(do_h, 256, 1)])
    do_n, dlse_n = jnp.stack(dmix[:3]), jnp.stack(dmix[3:])
    dqf, dkf, dvf = _band_bwd(s['qf'], s['kf'], s['vf'], s['o_f'], s['lse_f'], _refold3(do_n), _refold3(dlse_n))
    back = lambda a: _unfold3(a).sum(axis=0).transpose(1, 0, 2).reshape(T, DIL_W)
    dqd, dkd, dvd = back(dqf), back(dkf), back(dvf)

    dz1, dz2, db1, db2 = _rows_vjp("s5_glu_bwd", _glu_fn, s['glu_rows'], s['glu_b'], [], [_row(dy_ssm)])
    gs_['b_glu'] = jnp.concatenate([db1, db2], axis=1)
    dzb = b16(jnp.concatenate([dz1, dz2], axis=1))
    dyg = _mm("mm_glu_dx", dzb, w['w_glu'], 'nt')
    gw['w_glu'] = _mm("mm_glu_dw", s['yg'], dzb, 'tn')
    dymm, du_act, dd = _rows_vjp("s5_act_bwd", _s5_act_fn, [_row(s['ymm']), _row(s['u'])], [s['d_row']], [], [_row(dyg)])
    gs_['d_skip'] = dd
    dhst = _mm("mm_s5_c_dx", dymm, s['c_mat'], 'nn', prec=HI)
    dc_mat = _mm("mm_s5_c_dw", dymm, s['hst'], 'tn', prec=HI)
    g, dar, dai = _scan_bwd(dhst, s['hst'], s['ar'], s['ai'])
    du = _mm("mm_s5_b_dx", g, s['b_mat'], 'nt', prec=HI, res=du_act)
    db_mat = _mm("mm_s5_b_dw", s['u'], g, 'tn', prec=HI)
    gs_['c_re'] = _diag_blocks(dc_mat[:, :SSM_S])
    gs_['c_im'] = -_diag_blocks(dc_mat[:, SSM_S:])
    dbbr = _diag_blocks(db_mat[:, :SSM_S]).reshape(SSM_G * SSM_P, SSM_N)
    dbbi = _diag_blocks(db_mat[:, SSM_S:]).reshape(SSM_G * SSM_P, SSM_N)
    disc_cts = [_row(dar.reshape(SSM_G, 1, SSM_N), 1, 0), _row(dai.reshape(SSM_G, 1, SSM_N), 1, 0),
                _row(dbbr, SSM_P, 0), _row(dbbi, SSM_P, 0)]
    da_re, da_im, dldt, db_r, db_i = _rows_vjp("s5_disc_bwd", _s5_disc_fn, s['disc_rows'], [], [], disc_cts)
    gs_['a_re'], gs_['a_im'], gs_['log_dt'] = da_re, da_im, dldt
    unb = lambda a: a.reshape(SSM_G, SSM_P, SSM_N).transpose(0, 2, 1)
    gs_['b_re'], gs_['b_im'] = unb(db_r), unb(db_i)

    do_mla = dy_mla.reshape(T, H_MLA, VDIM).transpose(1, 0, 2)
    dqh, dkh, dvh, dkpe = _mla_bwd(s['qh'], s['kh'], s['vh'], do_mla, s['lse_mla'])
    dq3 = dqh.transpose(1, 0, 2)
    half = ROPE // 2
    ct1 = jnp.concatenate([dq3[:, :, NOPE:NOPE + half].reshape(T, -1), dkpe[:, :half]], axis=1)
    ct2 = jnp.concatenate([dq3[:, :, NOPE + half:].reshape(T, -1), dkpe[:, half:]], axis=1)
    zero = jnp.zeros_like(ct1)
    dx1r, dx2r = _rows_vjp("rope_bwd", _rope_fn, [_row(zero), _row(zero)], [], [_row(cos9), _row(sin9)],
                           [_row(ct1), _row(ct2)])
    nq = H_MLA * half
    dq = jnp.concatenate([dq3[:, :, :NOPE], dx1r[:, :nq].reshape(T, H_MLA, half), dx2r[:, :nq].reshape(T, H_MLA, half)],
                         axis=-1).reshape(T, H_MLA * QK)
    dk_rope = jnp.concatenate([dx1r[:, nq:], dx2r[:, nq:]], axis=-1)
    dkv = jnp.concatenate([dkh.transpose(1, 0, 2)[:, :, :NOPE], dvh.transpose(1, 0, 2)], axis=-1).reshape(T, -1)
    dqb, dkvb = b16(dq), b16(dkv)
    dcqn = _mm("mm_uq_dx", dqb, w['w_uq'], 'nt')
    gw['w_uq'] = _mm("mm_uq_dw", s['cqn'], dqb, 'tn')
    dckvn = _mm("mm_ukv_dx", dkvb, w['w_ukv'], 'nt')
    gw['w_ukv'] = _mm("mm_ukv_dw", s['ckvn'], dkvb, 'tn')
    dc_q, gs_['g_q'] = _rms_bwd("rms_q_bwd", s['c_q'], sp['g_q'], dcqn)
    dc_kv, gs_['g_kv'] = _rms_bwd("rms_kv_bwd", s['c_kv'], sp['g_kv'], dckvn)

    dproj = b16(jnp.concatenate([dc_q, dc_kv, dk_rope, du, dqd, dkd, dvd], axis=1))
    dh = _mm("mm_in_dx", dproj, w['w_in'], 'nt')
    gw['w_in'] = _mm("mm_in_dw", s['h'], dproj, 'tn')
    dx, gs_['g_mix'] = _rms_bwd("rms_mix_bwd", s['x'], sp['g_mix'], dh, dx1)
    return dx, gw, gs_


def _adamw(name, wt, g, m, v):
    shape = wt.shape
    two = (lambda a: a.reshape(1, -1)) if wt.ndim == 1 else (lambda a: a.reshape(-1, shape[-1]))
    w2, g2, m2, v2 = two(wt), two(g), two(m), two(v)
    tr = _row(w2, None, 0)[2]
    outs = [(w2.shape, F32, 0, tr)] * 3
    d, nm, nv = _rows_fwd(name, _adamw_fn, [_row(a, tr, 0) for a in (w2, g2, m2, v2)], [], outs)
    return d.reshape(shape), nm.reshape(shape), nv.reshape(shape)


def kernel(x, g_mix, w_in, g_q, w_uq, g_kv, w_ukv, a_re, a_im, b_re, b_im, c_re, c_im, d_skip, log_dt, w_glu, b_glu, g_out_mla, g_out_ssm, g_out_dil, w_o, g_ffn, w_gate, w_up, w_down, g_final, loss_target, m_g_mix, m_w_in, m_g_q, m_w_uq, m_g_kv, m_w_ukv, m_a_re, m_a_im, m_b_re, m_b_im, m_c_re, m_c_im, m_d_skip, m_log_dt, m_w_glu, m_b_glu, m_g_out_mla, m_g_out_ssm, m_g_out_dil, m_w_o, m_g_ffn, m_w_gate, m_w_up, m_w_down, m_g_final, v_g_mix, v_w_in, v_g_q, v_w_uq, v_g_kv, v_w_ukv, v_a_re, v_a_im, v_b_re, v_b_im, v_c_re, v_c_im, v_d_skip, v_log_dt, v_w_glu, v_b_glu, v_g_out_mla, v_g_out_ssm, v_g_out_dil, v_w_o, v_g_ffn, v_w_gate, v_w_up, v_w_down, v_g_final):
    W = dict(zip(PARAMS, (g_mix, w_in, g_q, w_uq, g_kv, w_ukv, a_re, a_im, b_re, b_im, c_re, c_im, d_skip, log_dt,
                          w_glu, b_glu, g_out_mla, g_out_ssm, g_out_dil, w_o, g_ffn, w_gate, w_up, w_down, g_final)))
    M = dict(zip(PARAMS, (m_g_mix, m_w_in, m_g_q, m_w_uq, m_g_kv, m_w_ukv, m_a_re, m_a_im, m_b_re, m_b_im, m_c_re,
                          m_c_im, m_d_skip, m_log_dt, m_w_glu, m_b_glu, m_g_out_mla, m_g_out_ssm, m_g_out_dil, m_w_o,
                          m_g_ffn, m_w_gate, m_w_up, m_w_down, m_g_final)))
    V = dict(zip(PARAMS, (v_g_mix, v_w_in, v_g_q, v_w_uq, v_g_kv, v_w_ukv, v_a_re, v_a_im, v_b_re, v_b_im, v_c_re,
                          v_c_im, v_d_skip, v_log_dt, v_w_glu, v_b_glu, v_g_out_mla, v_g_out_ssm, v_g_out_dil, v_w_o,
                          v_g_ffn, v_w_gate, v_w_up, v_w_down, v_g_final)))
    cx, cy, cc = _place()
    core = cc.astype(jnp.int32).reshape(1)
    chip = (2 * cx + cy).astype(jnp.int32).reshape(1)
    cos9, sin9 = _rope_tables()

    full = []
    for l in range(DEPTH):
        packed = _pack_shards({n: W[n][l] for n in BIG}).astype(BF16)
        full.append(_unpack_full(_all_gather("gather_weights", packed)))
    small = [{n: W[n][l] for n in SMALL} for l in range(DEPTH)]
    for sp in small:
        for n in ('g_mix', 'g_q', 'g_kv', 'g_ffn'):
            sp[n] = sp[n].reshape(1, -1)

    xa = x[0]
    saved = []
    for l in range(DEPTH):
        xa, s = _layer_fwd(xa, full[l], small[l], cos9, sin9)
        saved.append(s)
    gf = g_final.reshape(1, D)
    ones = jnp.ones((T, 1), F32)
    dxa, dgf, loss_rows = _rows_vjp("loss", _loss_fn, [_row(xa)], [gf], [_row(loss_target[0])], [_row(ones)],
                                    primal=True)
    loss = lax.psum(_sum_rows("loss_sum", loss_rows)[0, 0], ("x", "y", "c"))

    g_small = [None] * DEPTH
    g_shard = [None] * DEPTH
    for l in reversed(range(DEPTH)):
        dxa, gw, g_small[l] = _layer_bwd(dxa, saved[l], full[l], small[l], cos9, sin9)
        g_shard[l] = _unpack_shard(_reduce_scatter(_pack_full(gw), core, chip))

    flat = [g_small[l][n].reshape(-1) for l in range(DEPTH) for n in SMALL] + [dgf.reshape(-1)]
    flat = jnp.concatenate(flat + [jnp.zeros((SMALL_R * PACK_C - SMALL_N,), F32)]).reshape(SMALL_R, PACK_C)
    tot = _sum_devices("small_sum", _all_gather("gather_small", flat)).reshape(-1)
    grads, off = {}, 0
    per_layer = {n: [] for n in SMALL}
    for l in range(DEPTH):
        for n, shp in SMALL.items():
            k = int(np.prod(shp))
            per_layer[n].append(tot[off:off + k].reshape(shp))
            off += k
    for n in SMALL:
        grads[n] = jnp.stack(per_layer[n])
    grads['g_final'] = tot[off:off + D]
    for n in BIG:
        grads[n] = jnp.stack([g_shard[l][n] for l in range(DEPTH)])

    delta, new_m, new_v = {}, {}, {}
    for n in PARAMS:
        delta[n], new_m[n], new_v[n] = _adamw("adamw_" + n, W[n], grads[n], M[n], V[n])
    return (loss, dxa[None], *[grads[n] for n in PARAMS], *[delta[n] for n in PARAMS],
            *[new_m[n] for n in PARAMS], *[new_v[n] for n in PARAMS])
```

```python
import jax
import jax.numpy as jnp
import numpy as np
from jax import lax
from jax.experimental import pallas as pl
from jax.experimental.pallas import tpu as pltpu

F32 = jnp.float32
BF16 = jnp.bfloat16

T = 2048
D = 2048
DEPTH = 4
N_DEV = 8
H_MLA, NOPE, ROPE, VDIM = 8, 128, 64, 128
QK = NOPE + ROPE
Q_LORA, KV_LORA = 512, 256
SSM_W, SSM_G, SSM_P, SSM_N = 512, 32, 16, 64
SSM_S = SSM_G * SSM_N
DIL_W, DIL_H, DIL_D = 512, 8, 64
BLK = 128
IN_SPLITS = (Q_LORA, KV_LORA, ROPE, SSM_W, DIL_W, DIL_W, DIL_W)
IN_W = sum(IN_SPLITS)
D_FF = 5632
EPS = 1e-6
ROPE_THETA = 10000.0
MLA_SCALE = QK ** -0.5
DIL_SCALE = DIL_D ** -0.5

ADAM_LR, ADAM_B1, ADAM_B2, ADAM_EPS, ADAM_WD, ADAM_STEP = 0.001, 0.9, 0.999, 1e-08, 0.01, 10

VMEM_LIMIT_V7X = 52 * 1024 * 1024
LANES = 128
PACK_C = 1024
ROW_BLOCK_BYTES = 2 * 1024 * 1024

NT = (((1,), (1,)), ((), ()))
TN = (((0,), (0,)), ((), ()))
HI = lax.Precision.HIGHEST
MESH = pl.DeviceIdType.MESH

PARAMS = ['g_mix', 'w_in', 'g_q', 'w_uq', 'g_kv', 'w_ukv', 'a_re', 'a_im', 'b_re', 'b_im', 'c_re', 'c_im',
          'd_skip', 'log_dt', 'w_glu', 'b_glu', 'g_out_mla', 'g_out_ssm', 'g_out_dil', 'w_o', 'g_ffn',
          'w_gate', 'w_up', 'w_down', 'g_final']
BIG = {'w_in': 'c', 'w_uq': 'c', 'w_ukv': 'c', 'w_glu': 'c', 'w_o': 'r', 'w_gate': 'c', 'w_up': 'c', 'w_down': 'r'}
SMALL = {'g_mix': (D,), 'g_q': (Q_LORA,), 'g_kv': (KV_LORA,), 'a_re': (SSM_G, SSM_N), 'a_im': (SSM_G, SSM_N),
         'b_re': (SSM_G, SSM_N, SSM_P), 'b_im': (SSM_G, SSM_N, SSM_P), 'c_re': (SSM_G, SSM_P, SSM_N),
         'c_im': (SSM_G, SSM_P, SSM_N), 'd_skip': (SSM_G, SSM_P), 'log_dt': (SSM_G,), 'b_glu': (2 * SSM_W,),
         'g_out_mla': (H_MLA * VDIM,), 'g_out_ssm': (SSM_W,), 'g_out_dil': (DIL_W,), 'g_ffn': (D,)}
SMALL_ROW_ALIGN = 64


def _tile(dim, target, align=LANES):
    best = None
    for t in range(align, min(dim, target) + 1, align):
        if dim % t == 0:
            best = t
    return best if best is not None else dim


def _params(sem=None):
    return pltpu.CompilerParams(dimension_semantics=sem, vmem_limit_bytes=VMEM_LIMIT_V7X)


def _dot(a, b, dims=None, prec=None):
    if dims is None:
        return jnp.dot(a, b, preferred_element_type=F32, precision=prec)
    return lax.dot_general(a, b, dims, preferred_element_type=F32, precision=prec)


def _mm_spec(shape, blk, t_r, t_c, rc):
    if blk is None:
        return pl.BlockSpec((t_r, t_c), rc)
    _, R, C = shape
    if blk == 'r':
        per = R // t_r
        return pl.BlockSpec((1, t_r, t_c), lambda i, j, k: (rc(i, j, k)[0] // per, rc(i, j, k)[0] % per, rc(i, j, k)[1]))
    per = C // t_c
    return pl.BlockSpec((1, t_r, t_c), lambda i, j, k: (rc(i, j, k)[1] // per, rc(i, j, k)[0], rc(i, j, k)[1] % per))


def _logical(shape, blk):
    if blk is None:
        return tuple(shape)
    G, R, C = shape
    return (G * R, C) if blk == 'r' else (R, G * C)


def _mm(name, a, b, mode, ab=None, bb=None, ob=None, res=None, prec=None):
    la, lb = _logical(a.shape, ab), _logical(b.shape, bb)
    am, ak = (0, 1) if mode != 'tn' else (1, 0)
    bk, bn = (0, 1) if mode != 'nt' else (1, 0)
    M, K, N = la[am], la[ak], lb[bn]
    assert lb[bk] == K, (name, a.shape, b.shape, mode)
    if ob is None:
        out_shape = (M, N)
    elif ob == 'r':
        G = N_DEV
        out_shape = (G, M // G, N)
    else:
        G = N_DEV
        out_shape = (G, M, N // G)
    em = min(a.shape[-2:][am], out_shape[-2])
    en = min(b.shape[-2:][bn], out_shape[-1])
    ek = min(a.shape[-2:][ak], b.shape[-2:][bk])
    tn = _tile(en, 512)
    tm = _tile(em, 512 if tn > 1024 else 1024)
    tk = _tile(ek, 512)
    nk = K // tk
    dims = {'nn': None, 'nt': NT, 'tn': TN}[mode]

    def val(ref):
        return ref[...] if len(ref.shape) == 2 else ref[0]

    def body(*refs):
        if res is None:
            a_ref, b_ref, o_ref, acc_ref = refs
        else:
            a_ref, b_ref, r_ref, o_ref, acc_ref = refs
        k = pl.program_id(2)

        @pl.when(k == 0)
        def _():
            acc_ref[...] = jnp.zeros_like(acc_ref)

        acc_ref[...] += _dot(val(a_ref), val(b_ref), dims, prec)

        @pl.when(k == nk - 1)
        def _():
            r = acc_ref[...] if res is None else acc_ref[...] + val(r_ref)
            if len(o_ref.shape) == 2:
                o_ref[...] = r
            else:
                o_ref[0] = r

    if mode == 'tn':
        a_spec = _mm_spec(a.shape, ab, tk, tm, lambda i, j, k: (k, i))
    else:
        a_spec = _mm_spec(a.shape, ab, tm, tk, lambda i, j, k: (i, k))
    if mode == 'nt':
        b_spec = _mm_spec(b.shape, bb, tn, tk, lambda i, j, k: (j, k))
    else:
        b_spec = _mm_spec(b.shape, bb, tk, tn, lambda i, j, k: (k, j))
    o_spec = _mm_spec(out_shape, ob, tm, tn, lambda i, j, k: (i, j))
    in_specs = [a_spec, b_spec] + ([o_spec] if res is not None else [])
    args = (a, b) + ((res,) if res is not None else ())
    return pl.pallas_call(
        body, name=name, out_shape=jax.ShapeDtypeStruct(out_shape, F32),
        grid=(M // tm, N // tn, nk), in_specs=in_specs, out_specs=o_spec,
        scratch_shapes=[pltpu.VMEM((tm, tn), F32)],
        compiler_params=_params(("parallel", "parallel", "arbitrary")),
    )(*args)


def _row(a, tr=None, axis=-2):
    axis = axis % a.ndim
    n = a.shape[axis]
    if tr is None:
        row_bytes = a.size // n * 4
        tr = _tile(n, max(8, min(256, ROW_BLOCK_BYTES // row_bytes)), 8)
    return (a, axis, tr)


def _row_spec(shape, axis, tr):
    nd = len(shape)
    blk = tuple(tr if d == axis else s for d, s in enumerate(shape))
    return pl.BlockSpec(blk, lambda i: tuple(i if d == axis else 0 for d in range(nd)))


def _full_spec(shape):
    nd = len(shape)
    return pl.BlockSpec(tuple(shape), lambda i: (0,) * nd)


def _steps(entries):
    ns = {a.shape[ax] // tr for a, ax, tr in entries}
    assert len(ns) == 1, [(a.shape, ax, tr) for a, ax, tr in entries]
    return ns.pop()


def _as_tuple(r):
    return tuple(r) if isinstance(r, (tuple, list)) else (r,)


def _rows_fwd(name, fn, rows, bcast, outs):
    steps = _steps(rows)
    nr, nb = len(rows), len(bcast)

    def body(*refs):
        vals = [r[...] for r in refs[:nr + nb]]
        res = _as_tuple(fn(*vals))
        for o_ref, r in zip(refs[nr + nb:], res):
            o_ref[...] = r.astype(o_ref.dtype)

    in_specs = [_row_spec(a.shape, ax, tr) for a, ax, tr in rows] + [_full_spec(b.shape) for b in bcast]
    out_specs = [_row_spec(s, ax % len(s), tr) for s, _, ax, tr in outs]
    res = pl.pallas_call(
        body, name=name, out_shape=[jax.ShapeDtypeStruct(s, dt) for s, dt, _, _ in outs],
        grid=(steps,), in_specs=in_specs, out_specs=out_specs,
        compiler_params=_params(("parallel",)),
    )(*[a for a, _, _ in rows], *bcast)
    return res


def _rows_vjp(name, fn, drows, dbc, arows, cts, primal=False, grad_dtypes=None):
    entries = list(drows) + list(arows) + list(cts)
    steps = _steps(entries)
    ndr, ndb, nar, nct = len(drows), len(dbc), len(arows), len(cts)
    gdt = list(grad_dtypes) if grad_dtypes is not None else [F32] * ndr

    def body(*refs):
        p = 0
        dr = [r[...] for r in refs[p:p + ndr]]; p += ndr
        db = [r[...] for r in refs[p:p + ndb]]; p += ndb
        ar = [r[...] for r in refs[p:p + nar]]; p += nar
        ct = [r[...] for r in refs[p:p + nct]]; p += nct
        g_rows = refs[p:p + ndr]; p += ndr
        g_bc = refs[p:p + ndb]; p += ndb
        prim_refs = refs[p:]

        def f(*d):
            return _as_tuple(fn(*d, *ar))

        outs, pullback = jax.vjp(f, *dr, *db)
        grads = pullback(tuple(c.astype(o.dtype) for c, o in zip(ct, outs)))
        for k in range(ndr):
            g_rows[k][...] = grads[k].astype(g_rows[k].dtype)
        if ndb:
            @pl.when(pl.program_id(0) == 0)
            def _():
                for r in g_bc:
                    r[...] = jnp.zeros_like(r)
            for k in range(ndb):
                g_bc[k][...] += grads[ndr + k]
        for r, o in zip(prim_refs, outs):
            r[...] = o.astype(r.dtype)

    in_specs = ([_row_spec(a.shape, ax, tr) for a, ax, tr in drows] + [_full_spec(b.shape) for b in dbc]
                + [_row_spec(a.shape, ax, tr) for a, ax, tr in arows]
                + [_row_spec(a.shape, ax, tr) for a, ax, tr in cts])
    out_shape = ([jax.ShapeDtypeStruct(a.shape, dt) for (a, _, _), dt in zip(drows, gdt)]
                 + [jax.ShapeDtypeStruct(b.shape, F32) for b in dbc])
    out_specs = ([_row_spec(a.shape, ax, tr) for a, ax, tr in drows] + [_full_spec(b.shape) for b in dbc])
    if primal:
        out_shape += [jax.ShapeDtypeStruct(a.shape, F32) for a, _, _ in cts]
        out_specs += [_row_spec(a.shape, ax, tr) for a, ax, tr in cts]
    return pl.pallas_call(
        body, name=name, out_shape=out_shape, grid=(steps,), in_specs=in_specs, out_specs=out_specs,
        compiler_params=_params(("arbitrary",)),
    )(*[a for a, _, _ in drows], *dbc, *[a for a, _, _ in arows], *[a for a, _, _ in cts])


def _rms_fn(x, g):
    return x * lax.rsqrt(jnp.mean(x * x, axis=-1, keepdims=True) + EPS) * g


def _rms_res_fn(x, g):
    return _rms_fn(x, g), x


def _rope_fn(q1, q2, k1, k2, cos, sin):
    return q1 * cos - q2 * sin, q2 * cos + q1 * sin, k1 * cos - k2 * sin, k2 * cos + k1 * sin


def _s5_act_fn(ymm, u, d):
    return jax.nn.gelu(ymm + d * u)


def _glu_fn(z1, z2, b1, b2):
    return (z1 + b1) * jax.nn.sigmoid(z2 + b2)


def _outnorm_fn(ym, ys, yd, gm, gs, gd):
    return jnp.concatenate([_rms_fn(ym, gm), _rms_fn(ys, gs), _rms_fn(yd, gd)], axis=-1)


def _swiglu_fn(a, b):
    return jax.nn.silu(a) * b


def _loss_fn(x, g, tgt):
    err = _rms_fn(x, g) - tgt
    return 0.5 * jnp.mean(err * err, axis=-1, keepdims=True)


def _dil_mix_fn(o0, o1, o2, l0, l1, l2):
    m = jnp.maximum(jnp.maximum(l0, l1), l2)
    e0, e1, e2 = jnp.exp(l0 - m), jnp.exp(l1 - m), jnp.exp(l2 - m)
    s = e0 + e1 + e2
    return (e0 / s) * o0 + (e1 / s) * o1 + (e2 / s) * o2


def _s5_disc_fn(a_re, a_im, ldt, b_r, b_i):
    lr = jnp.minimum(a_re.reshape(1, SSM_N), -1e-4)
    li = a_im.reshape(1, SSM_N)
    dt = jnp.exp(ldt.reshape(1, 1))
    e = jnp.exp(lr * dt)
    ar = e * jnp.cos(li * dt)
    ai = e * jnp.sin(li * dt)
    nr, ni = ar - 1.0, ai
    den = lr * lr + li * li
    cr = (nr * lr + ni * li) / den
    ci = (ni * lr - nr * li) / den
    return ar.reshape(1, 1, SSM_N), ai.reshape(1, 1, SSM_N), cr * b_r - ci * b_i, cr * b_i + ci * b_r


def _adamw_fn(w, g, m, v):
    m = ADAM_B1 * m + (1.0 - ADAM_B1) * g
    v = ADAM_B2 * v + (1.0 - ADAM_B2) * jnp.square(g)
    m_hat = m / (1.0 - ADAM_B1 ** ADAM_STEP)
    v_hat = v / (1.0 - ADAM_B2 ** ADAM_STEP)
    delta = -ADAM_LR * (m_hat / (jnp.sqrt(v_hat) + ADAM_EPS) + ADAM_WD * w)
    return delta, m, v


def _rms_fwd(name, x, g):
    (h,) = _rows_fwd(name, _rms_fn, [_row(x)], [g], [(x.shape, BF16, -2, _row(x)[2])])
    return h


def _rms_bwd(name, x, g, dh, dres=None):
    if dres is None:
        dx, dg = _rows_vjp(name, _rms_fn, [_row(x)], [g], [], [_row(dh)])
    else:
        dx, dg = _rows_vjp(name, _rms_res_fn, [_row(x)], [g], [], [_row(dh), _row(dres)])
    return dx, dg


MLA_TQ = 256


def _mla_fwd(q, k, v):
    tq = MLA_TQ

    def body(q_ref, k_ref, v_ref, o_ref, lse_ref):
        i = pl.program_id(1)
        s = _dot(q_ref[0], k_ref[0], NT) * MLA_SCALE
        row = i * tq + lax.broadcasted_iota(jnp.int32, (tq, T), 0)
        col = lax.broadcasted_iota(jnp.int32, (tq, T), 1)
        s = jnp.where(row >= col, s, -jnp.inf)
        m = jnp.max(s, axis=-1, keepdims=True)
        p = jnp.exp(s - m)
        l = jnp.sum(p, axis=-1, keepdims=True)
        o_ref[...] = _dot((p / l).astype(BF16), v_ref[0])
        lse_ref[0] = m + jnp.log(l)

    return pl.pallas_call(
        body, name="mla_fwd",
        out_shape=[jax.ShapeDtypeStruct((T, H_MLA * VDIM), F32), jax.ShapeDtypeStruct((H_MLA, T, 1), F32)],
        grid=(H_MLA, T // tq),
        in_specs=[pl.BlockSpec((1, tq, QK), lambda h, i: (h, i, 0)),
                  pl.BlockSpec((1, T, QK), lambda h, i: (h, 0, 0)),
                  pl.BlockSpec((1, T, VDIM), lambda h, i: (h, 0, 0))],
        out_specs=[pl.BlockSpec((tq, VDIM), lambda h, i: (i, h)),
                   pl.BlockSpec((1, tq, 1), lambda h, i: (h, i, 0))],
        compiler_params=_params(("parallel", "parallel")),
    )(q, k, v)


def _mla_bwd(q, k, v, do, lse):
    tq = MLA_TQ

    def body(q_ref, k_ref, v_ref, do_ref, lse_ref, dq_ref, dk_ref, dv_ref, dkpe_ref):
        h, i = pl.program_id(0), pl.program_id(1)

        @pl.when(i == 0)
        def _():
            dk_ref[...] = jnp.zeros_like(dk_ref)
            dv_ref[...] = jnp.zeros_like(dv_ref)

        @pl.when((i == 0) & (h == 0))
        def _():
            dkpe_ref[...] = jnp.zeros_like(dkpe_ref)

        q, k, v = q_ref[0], k_ref[0], v_ref[0]
        do = do_ref[...].astype(BF16)
        s = _dot(q, k, NT) * MLA_SCALE
        row = i * tq + lax.broadcasted_iota(jnp.int32, (tq, T), 0)
        col = lax.broadcasted_iota(jnp.int32, (tq, T), 1)
        p = jnp.where(row >= col, jnp.exp(s - lse_ref[0]), 0.0)
        dp = _dot(do, v, NT)
        delta = jnp.sum(dp * p, axis=-1, keepdims=True)
        ds = (p * (dp - delta) * MLA_SCALE).astype(BF16)
        dq_ref[0] = _dot(ds, k)
        dk = _dot(ds, q, TN)
        dk_ref[0] += dk
        dkpe_ref[...] += dk[:, NOPE:]
        dv_ref[0] += _dot(p.astype(BF16), do, TN)

    return pl.pallas_call(
        body, name="mla_bwd",
        out_shape=[jax.ShapeDtypeStruct((H_MLA, T, QK), F32), jax.ShapeDtypeStruct((H_MLA, T, QK), F32),
                   jax.ShapeDtypeStruct((H_MLA, T, VDIM), F32), jax.ShapeDtypeStruct((T, ROPE), F32)],
        grid=(H_MLA, T // tq),
        in_specs=[pl.BlockSpec((1, tq, QK), lambda h, i: (h, i, 0)),
                  pl.BlockSpec((1, T, QK), lambda h, i: (h, 0, 0)),
                  pl.BlockSpec((1, T, VDIM), lambda h, i: (h, 0, 0)),
                  pl.BlockSpec((tq, VDIM), lambda h, i: (i, h)),
                  pl.BlockSpec((1, tq, 1), lambda h, i: (h, i, 0))],
        out_specs=[pl.BlockSpec((1, tq, QK), lambda h, i: (h, i, 0)),
                   pl.BlockSpec((1, T, QK), lambda h, i: (h, 0, 0)),
                   pl.BlockSpec((1, T, VDIM), lambda h, i: (h, 0, 0)),
                   pl.BlockSpec((T, ROPE), lambda h, i: (0, 0))],
        compiler_params=_params(("arbitrary", "arbitrary")),
    )(q, k, v, do, lse)


NBLK = T // BLK


def _band_masks():
    r = lax.broadcasted_iota(jnp.int32, (BLK, BLK), 0)
    j = lax.broadcasted_iota(jnp.int32, (BLK, BLK), 1)
    return j <= r, j >= r


def _seq_start(p, i):
    per_seq = lax.shift_right_logical(jnp.int32(NBLK), 2 * p)
    return lax.rem(i, per_seq) == 0


def _band_fwd(q, k, v):
    def body(q_ref, kp_ref, kc_ref, vp_ref, vc_ref, o_ref, lse_ref):
        p, i = pl.program_id(0), pl.program_id(1)
        has_prev = jnp.logical_not(_seq_start(p, i))
        m_cur, m_prev = _band_masks()
        m_prev = m_prev & has_prev
        for h in range(DIL_H):
            qh = q_ref[0, h]
            s_c = jnp.where(m_cur, _dot(qh, kc_ref[0, h], NT) * DIL_SCALE, -jnp.inf)
            s_p = jnp.where(m_prev, _dot(qh, kp_ref[0, h], NT) * DIL_SCALE, -jnp.inf)
            m = jnp.maximum(jnp.max(s_c, axis=-1, keepdims=True), jnp.max(s_p, axis=-1, keepdims=True))
            e_c, e_p = jnp.exp(s_c - m), jnp.exp(s_p - m)
            l = jnp.sum(e_c, axis=-1, keepdims=True) + jnp.sum(e_p, axis=-1, keepdims=True)
            o_ref[0, h] = (_dot((e_p / l).astype(BF16), vp_ref[0, h]) + _dot((e_c / l).astype(BF16), vc_ref[0, h]))
            lse_ref[0, h] = m + jnp.log(l)

    blk = lambda w: (1, DIL_H, BLK, w)
    cur = lambda p, i: (p, 0, i, 0)
    prev = lambda p, i: (p, 0, jnp.maximum(i - 1, 0), 0)
    return pl.pallas_call(
        body, name="band_fwd",
        out_shape=[jax.ShapeDtypeStruct((3, DIL_H, T, DIL_D), F32), jax.ShapeDtypeStruct((3, DIL_H, T, 1), F32)],
        grid=(3, NBLK),
        in_specs=[pl.BlockSpec(blk(DIL_D), cur), pl.BlockSpec(blk(DIL_D), prev), pl.BlockSpec(blk(DIL_D), cur),
                  pl.BlockSpec(blk(DIL_D), prev), pl.BlockSpec(blk(DIL_D), cur)],
        out_specs=[pl.BlockSpec(blk(DIL_D), cur), pl.BlockSpec(blk(1), cur)],
        compiler_params=_params(("parallel", "parallel")),
    )(q, k, k, v, v)


def _band_bwd(q, k, v, o, lse, do, dlse):
    def body(qc_ref, qn_ref, kp_ref, kc_ref, vp_ref, vc_ref, oc_ref, on_ref, lc_ref, ln_ref,
             doc_ref, don_ref, dlc_ref, dln_ref, dq_ref, dk_ref, dv_ref):
        p, i = pl.program_id(0), pl.program_id(1)
        has_prev = jnp.logical_not(_seq_start(p, i))
        has_next = jnp.logical_not(_seq_start(p, i + 1)) & (i + 1 < NBLK)
        m_cur, m_prev = _band_masks()

        def probs(qh, kh, lse_h, mask):
            return jnp.where(mask, jnp.exp(_dot(qh, kh, NT) * DIL_SCALE - lse_h), 0.0)

        def dscore(pr, do_h, vh, shift):
            return (pr * (_dot(do_h, vh, NT) + shift) * DIL_SCALE).astype(BF16)

        for h in range(DIL_H):
            qc, qn, kp, kc, vp, vc = qc_ref[0, h], qn_ref[0, h], kp_ref[0, h], kc_ref[0, h], vp_ref[0, h], vc_ref[0, h]
            doc, don = doc_ref[0, h], don_ref[0, h]
            sh_c = dlc_ref[0, h] - jnp.sum(doc * oc_ref[0, h], axis=-1, keepdims=True)
            sh_n = dln_ref[0, h] - jnp.sum(don * on_ref[0, h], axis=-1, keepdims=True)
            doc, don = doc.astype(BF16), don.astype(BF16)
            p_cc = probs(qc, kc, lc_ref[0, h], m_cur)
            p_cp = probs(qc, kp, lc_ref[0, h], m_prev & has_prev)
            p_nc = probs(qn, kc, ln_ref[0, h], m_prev & has_next)
            ds_cc = dscore(p_cc, doc, vc, sh_c)
            ds_cp = dscore(p_cp, doc, vp, sh_c)
            ds_nc = dscore(p_nc, don, vc, sh_n)
            dq_ref[0, h] = _dot(ds_cc, kc) + _dot(ds_cp, kp)
            dk_ref[0, h] = _dot(ds_cc, qc, TN) + _dot(ds_nc, qn, TN)
            dv_ref[0, h] = _dot(p_cc.astype(BF16), doc, TN) + _dot(p_nc.astype(BF16), don, TN)

    blk = lambda w: (1, DIL_H, BLK, w)
    cur = lambda p, i: (p, 0, i, 0)
    prev = lambda p, i: (p, 0, jnp.maximum(i - 1, 0), 0)
    nxt = lambda p, i: (p, 0, jnp.minimum(i + 1, NBLK - 1), 0)
    w, one = pl.BlockSpec(blk(DIL_D), cur), pl.BlockSpec(blk(1), cur)
    wn, onen = pl.BlockSpec(blk(DIL_D), nxt), pl.BlockSpec(blk(1), nxt)
    wp = pl.BlockSpec(blk(DIL_D), prev)
    return pl.pallas_call(
        body, name="band_bwd",
        out_shape=[jax.ShapeDtypeStruct((3, DIL_H, T, DIL_D), F32)] * 3,
        grid=(3, NBLK),
        in_specs=[w, wn, wp, w, wp, w, w, wn, one, onen, w, wn, one, onen],
        out_specs=[w, w, w],
        compiler_params=_params(("parallel", "parallel")),
    )(q, q, k, k, v, v, o, o, lse, lse, do, do, dlse, dlse)


SCAN_TC = 256


def _scan_fwd(bu, ar, ai):
    tc, S = SCAN_TC, SSM_S

    def body(bu_ref, ar_ref, ai_ref, h_ref, cr_ref, ci_ref):
        @pl.when(pl.program_id(0) == 0)
        def _():
            cr_ref[...] = jnp.zeros_like(cr_ref)
            ci_ref[...] = jnp.zeros_like(ci_ref)

        a_r, a_i = ar_ref[...], ai_ref[...]

        def step(j, carry):
            hr, hi = carry
            for r in range(8):
                t = pl.multiple_of(j * 8, 8) + r
                br = bu_ref[pl.ds(t, 1), pl.ds(0, S)]
                bi = bu_ref[pl.ds(t, 1), pl.ds(S, S)]
                hr, hi = a_r * hr - a_i * hi + br, a_r * hi + a_i * hr + bi
                h_ref[pl.ds(t, 1), pl.ds(0, S)] = hr
                h_ref[pl.ds(t, 1), pl.ds(S, S)] = hi
            return hr, hi

        hr, hi = lax.fori_loop(0, tc // 8, step, (cr_ref[...], ci_ref[...]))
        cr_ref[...] = hr
        ci_ref[...] = hi

    return pl.pallas_call(
        body, name="s5_scan_fwd", out_shape=jax.ShapeDtypeStruct((T, 2 * S), F32),
        grid=(T // tc,),
        in_specs=[pl.BlockSpec((tc, 2 * S), lambda i: (i, 0)), _full_spec((1, S)), _full_spec((1, S))],
        out_specs=pl.BlockSpec((tc, 2 * S), lambda i: (i, 0)),
        scratch_shapes=[pltpu.VMEM((1, S), F32), pltpu.VMEM((1, S), F32)],
        compiler_params=_params(("arbitrary",)),
    )(bu, ar, ai)


def _scan_bwd(dh, h, ar, ai):
    tc, S = SCAN_TC, SSM_S
    nc = T // tc

    def body(dh_ref, h_ref, hp_ref, ar_ref, ai_ref, g_ref, dar_ref, dai_ref, cr_ref, ci_ref):
        i = pl.program_id(0)

        @pl.when(i == 0)
        def _():
            cr_ref[...] = jnp.zeros_like(cr_ref)
            ci_ref[...] = jnp.zeros_like(ci_ref)
            dar_ref[...] = jnp.zeros_like(dar_ref)
            dai_ref[...] = jnp.zeros_like(dai_ref)

        a_r, a_i = ar_ref[...], ai_ref[...]
        first_chunk = (i == nc - 1)
        edge = jnp.where(first_chunk, 0.0, 1.0)
        hpr = hp_ref[pl.ds(7, 1), pl.ds(0, S)] * edge
        hpi = hp_ref[pl.ds(7, 1), pl.ds(S, S)] * edge

        def step(jj, carry):
            gr, gi, dar, dai = carry
            j = tc // 8 - 1 - jj
            for r in range(7, -1, -1):
                t = pl.multiple_of(j * 8, 8) + r
                tp = jnp.maximum(t - 1, 0)
                inside = t > 0
                pr = jnp.where(inside, h_ref[pl.ds(tp, 1), pl.ds(0, S)], hpr)
                pi = jnp.where(inside, h_ref[pl.ds(tp, 1), pl.ds(S, S)], hpi)
                gr, gi = (dh_ref[pl.ds(t, 1), pl.ds(0, S)] + a_r * gr + a_i * gi,
                          dh_ref[pl.ds(t, 1), pl.ds(S, S)] + a_r * gi - a_i * gr)
                g_ref[pl.ds(t, 1), pl.ds(0, S)] = gr
                g_ref[pl.ds(t, 1), pl.ds(S, S)] = gi
                dar = dar + gr * pr + gi * pi
                dai = dai + gi * pr - gr * pi
            return gr, gi, dar, dai

        zero = jnp.zeros((1, S), F32)
        gr, gi, dar, dai = lax.fori_loop(0, tc // 8, step, (cr_ref[...], ci_ref[...], zero, zero))
        cr_ref[...] = gr
        ci_ref[...] = gi
        dar_ref[...] += dar
        dai_ref[...] += dai

    rev = lambda i: (nc - 1 - i, 0)
    before = lambda i: (jnp.maximum((nc - 1 - i) * (tc // 8) - 1, 0), 0)
    return pl.pallas_call(
        body, name="s5_scan_bwd",
        out_shape=[jax.ShapeDtypeStruct((T, 2 * S), F32), jax.ShapeDtypeStruct((1, S), F32),
                   jax.ShapeDtypeStruct((1, S), F32)],
        grid=(nc,),
        in_specs=[pl.BlockSpec((tc, 2 * S), rev), pl.BlockSpec((tc, 2 * S), rev), pl.BlockSpec((8, 2 * S), before),
                  _full_spec((1, S)), _full_spec((1, S))],
        out_specs=[pl.BlockSpec((tc, 2 * S), rev), _full_spec((1, S)), _full_spec((1, S))],
        scratch_shapes=[pltpu.VMEM((1, S), F32), pltpu.VMEM((1, S), F32)],
        compiler_params=_params(("arbitrary",)),
    )(dh, h, h, ar, ai)


def _sum_rows(name, x):
    def body(x_ref, o_ref):
        o_ref[...] = jnp.sum(x_ref[...], axis=0, keepdims=True)

    return pl.pallas_call(body, name=name, out_shape=jax.ShapeDtypeStruct((1, 1), F32),
                          in_specs=[_full_spec(x.shape)], out_specs=_full_spec((1, 1)), grid=(1,))(x)


ANY = pl.BlockSpec(memory_space=pl.ANY)


def _place():
    return lax.axis_index("x"), lax.axis_index("y"), lax.axis_index("c")


def _all_gather(name, shards):
    n = len(shards)

    def body(*refs):
        xs, outs = refs[:n], refs[n:2 * n]
        send_sems, recv_sems, local_sems = refs[2 * n:]
        x, y, c = _place()
        me, sibling = (x, y, c), (x, y, 1 - c)
        chips = [(1 - x, y), (x, 1 - y), (1 - x, 1 - y)]

        def slot(t, px, py, pc):
            return outs[t].at[4 * px + 2 * py + pc]

        def copy(t, k, block, to, src=None):
            return pltpu.make_async_remote_copy(
                src_ref=slot(t, *block) if src is None else src, dst_ref=slot(t, *block),
                send_sem=send_sems.at[7 * t + k], recv_sem=recv_sems.at[7 * t + k], device_id=to, device_id_type=MESH)

        mine = [pltpu.make_async_copy(xs[t], slot(t, *me), local_sems.at[t]) for t in range(n)]
        sent = []
        for t in range(n):
            mine[t].start()
            sent.append(copy(t, 0, me, sibling, src=xs[t]))
            sent += [copy(t, 1 + j, me, (*chip, c), src=xs[t]) for j, chip in enumerate(chips)]
        for cp in sent:
            cp.start()
        for j, chip in enumerate(chips):
            for t in range(n):
                copy(t, 1 + j, (*chip, c), me).wait_recv()
                fwd = copy(t, 4 + j, (*chip, c), sibling)
                fwd.start()
                sent.append(fwd)
        for t in range(n):
            copy(t, 0, sibling, me).wait_recv()
            for j, chip in enumerate(chips):
                copy(t, 4 + j, (*chip, 1 - c), me).wait_recv()
        for cp in sent:
            cp.wait_send()
        for cp in mine:
            cp.wait()

    return pl.pallas_call(
        body, name=name, out_shape=[jax.ShapeDtypeStruct((N_DEV,) + s.shape, s.dtype) for s in shards],
        in_specs=[ANY] * n, out_specs=[ANY] * n,
        scratch_shapes=[pltpu.SemaphoreType.DMA((7 * n,)), pltpu.SemaphoreType.DMA((7 * n,)),
                        pltpu.SemaphoreType.DMA((n,))],
    )(*shards)


def _swap_sibling(name, p4s):
    n = len(p4s)

    def body(*refs):
        ps, outs = refs[:n], refs[n:2 * n]
        send_sems, recv_sems = refs[2 * n:]
        x, y, c = _place()
        cps = [pltpu.make_async_remote_copy(
            src_ref=ps[t].at[j, 1 - c], dst_ref=outs[t].at[j], send_sem=send_sems.at[4 * t + j],
            recv_sem=recv_sems.at[4 * t + j], device_id=(x, y, 1 - c), device_id_type=MESH)
            for t in range(n) for j in range(4)]
        for cp in cps:
            cp.start()
        for cp in cps:
            cp.wait()

    return pl.pallas_call(
        body, name=name, out_shape=[jax.ShapeDtypeStruct((4,) + p.shape[2:], p.dtype) for p in p4s],
        in_specs=[ANY] * n, out_specs=[ANY] * n,
        scratch_shapes=[pltpu.SemaphoreType.DMA((4 * n,)), pltpu.SemaphoreType.DMA((4 * n,))],
    )(*p4s)


def _swap_chips(name, s4s):
    n = len(s4s)

    def body(*refs):
        ss, outs = refs[:n], refs[n:2 * n]
        send_sems, recv_sems = refs[2 * n:]
        x, y, c = _place()
        chips = [(1 - x, y), (x, 1 - y), (1 - x, 1 - y)]
        cps = [pltpu.make_async_remote_copy(
            src_ref=ss[t].at[2 * px + py], dst_ref=outs[t].at[k], send_sem=send_sems.at[3 * t + k],
            recv_sem=recv_sems.at[3 * t + k], device_id=(px, py, c), device_id_type=MESH)
            for t in range(n) for k, (px, py) in enumerate(chips)]
        for cp in cps:
            cp.start()
        for cp in cps:
            cp.wait()

    return pl.pallas_call(
        body, name=name, out_shape=[jax.ShapeDtypeStruct((3,) + s.shape[1:], s.dtype) for s in s4s],
        in_specs=[ANY] * n, out_specs=[ANY] * n,
        scratch_shapes=[pltpu.SemaphoreType.DMA((3 * n,)), pltpu.SemaphoreType.DMA((3 * n,))],
    )(*s4s)


def _sum_tile(R, C):
    return _tile(R, max(8, min(1024, ROW_BLOCK_BYTES // (C * 4))), 8)


def _pair_sum(name, p4, got, core):
    _, _, R, C = p4.shape
    tr = _sum_tile(R, C)

    def body(core_ref, p_ref, g_ref, o_ref):
        o_ref[...] = p_ref[:, 0] + g_ref[...]

    return pl.pallas_call(
        body, name=name, out_shape=jax.ShapeDtypeStruct((4, R, C), F32),
        grid_spec=pltpu.PrefetchScalarGridSpec(
            num_scalar_prefetch=1, grid=(4, R // tr),
            in_specs=[pl.BlockSpec((1, 1, tr, C), lambda j, i, core: (j, core[0], i, 0)),
                      pl.BlockSpec((1, tr, C), lambda j, i, core: (j, i, 0))],
            out_specs=pl.BlockSpec((1, tr, C), lambda j, i, core: (j, i, 0))),
        compiler_params=_params(("parallel", "parallel")),
    )(core, p4, got)


def _sum_devices(name, g8):
    _, R, C = g8.shape
    tr = _tile(R, SMALL_ROW_ALIGN, 8)

    def body(g_ref, o_ref):
        acc = g_ref[0]
        for d in range(1, N_DEV):
            acc = acc + g_ref[d]
        o_ref[...] = acc

    return pl.pallas_call(
        body, name=name, out_shape=jax.ShapeDtypeStruct((R, C), F32), grid=(R // tr,),
        in_specs=[pl.BlockSpec((N_DEV, tr, C), lambda i: (0, i, 0))], out_specs=pl.BlockSpec((tr, C), lambda i: (i, 0)),
        compiler_params=_params(("parallel",)),
    )(g8)


def _adamw_shard(name, layer, w, m, v, s4, got, chip, bufs):
    _, R, C = w.shape
    tr = _sum_tile(R, C) // 2 if _sum_tile(R, C) % 16 == 0 else _sum_tile(R, C)

    def body(chip_ref, w_ref, m_ref, v_ref, s_ref, g_ref, b0, b1, b2, b3, og, od, om, ov):
        g = ((s_ref[0] + g_ref[0]) + g_ref[1]) + g_ref[2]
        d, nm, nv = _adamw_fn(w_ref[0], g, m_ref[0], v_ref[0])
        og[0], od[0], om[0], ov[0] = g, d, nm, nv

    lay = pl.BlockSpec((1, tr, C), lambda i, chip: (layer, i, 0))
    if bufs is None:
        bufs = [lax.empty(w.shape, F32) for _ in range(4)]
    return pl.pallas_call(
        body, name=name, out_shape=[jax.ShapeDtypeStruct(w.shape, F32)] * 4,
        grid_spec=pltpu.PrefetchScalarGridSpec(
            num_scalar_prefetch=1, grid=(R // tr,),
            in_specs=[lay, lay, lay, pl.BlockSpec((1, tr, C), lambda i, chip: (chip[0], i, 0)),
                      pl.BlockSpec((3, tr, C), lambda i, chip: (0, i, 0)), ANY, ANY, ANY, ANY],
            out_specs=[lay] * 4),
        input_output_aliases={6: 0, 7: 1, 8: 2, 9: 3},
        compiler_params=_params(("parallel",)),
    )(chip, w, m, v, s4, got, *bufs)


def _adamw(name, wt, g, m, v):
    shape = wt.shape
    two = (lambda a: a.reshape(1, -1)) if wt.ndim == 1 else (lambda a: a.reshape(-1, shape[-1]))
    w2, g2, m2, v2 = two(wt), two(g), two(m), two(v)
    tr = _row(w2, None, 0)[2]
    outs = [(w2.shape, F32, 0, tr)] * 3
    d, nm, nv = _rows_fwd(name, _adamw_fn, [_row(a, tr, 0) for a in (w2, g2, m2, v2)], [], outs)
    return d.reshape(shape), nm.reshape(shape), nv.reshape(shape)


def _block_diag(v):
    eye = jnp.eye(SSM_G, dtype=v.dtype)
    return (eye[:, None, :, None] * v[:, :, None, :]).reshape(SSM_G * SSM_P, SSM_G * SSM_N)


def _diag_blocks(m):
    m4 = m.reshape(SSM_G, SSM_P, SSM_G, SSM_N)
    eye = jnp.eye(SSM_G, dtype=m.dtype)
    return (m4 * eye[:, None, :, None]).sum(axis=2)


def _fold(a, dil):
    if dil == 1:
        return a
    return a.reshape((T // dil, dil) + a.shape[1:]).swapaxes(0, 1).reshape(a.shape)


def _unfold(a, dil):
    if dil == 1:
        return a
    return a.reshape((dil, T // dil) + a.shape[1:]).swapaxes(0, 1).reshape(a.shape)


DILS = (1, 4, 16)


def _fold3(a):
    return jnp.stack([_fold(a, d) for d in DILS]).transpose(0, 2, 1, 3)


def _unfold3(a):
    return jnp.stack([_unfold(a[p].swapaxes(0, 1), d).swapaxes(0, 1) for p, d in enumerate(DILS)])


def _refold3(a):
    return jnp.stack([_fold(a[p].swapaxes(0, 1), d).swapaxes(0, 1) for p, d in enumerate(DILS)])


def _rope_tables():
    half = ROPE // 2
    inv_freq = ROPE_THETA ** (-jnp.arange(half, dtype=F32) / half)
    ang = jnp.arange(T).astype(F32)[:, None] * inv_freq[None, :]
    return jnp.cos(ang), jnp.sin(ang)


def _from_blocks(a8):
    return a8.transpose(1, 0, 2).reshape(a8.shape[1], -1)


def _to_blocks(a):
    return a.reshape(a.shape[0], N_DEV, -1).transpose(1, 0, 2)


HALF = ROPE // 2


def _rope_rows(q1, q2, k1, k2, cos, sin):
    return [_row(q1, 256, 1), _row(q2, 256, 1), _row(k1, 256, 0), _row(k2, 256, 0)], [_row(cos, 256, 0), _row(sin, 256, 0)]


def _layer_fwd(x, w, sp, cos, sin):
    s = {}
    s['x'] = x
    h = _rms_fwd("rms_mix", x, sp['g_mix'])
    proj = _from_blocks(_mm("mm_in", h, w['w_in'], 'nn', bb='c', ob='c'))
    offs = np.cumsum((0,) + IN_SPLITS)
    c_q, c_kv, k_rope, u, qd, kd, vd = [proj[:, offs[i]:offs[i + 1]] for i in range(7)]
    s.update(h=h, c_q=c_q, c_kv=c_kv, u=u)

    cqn = _rms_fwd("rms_q", c_q, sp['g_q'])
    ckvn = _rms_fwd("rms_kv", c_kv, sp['g_kv'])
    q8 = _mm("mm_uq", cqn, w['w_uq'], 'nn', bb='c', ob='c')
    kv8 = _mm("mm_ukv", ckvn, w['w_ukv'], 'nn', bb='c', ob='c')
    rows, aux = _rope_rows(q8[:, :, NOPE:NOPE + HALF], q8[:, :, NOPE + HALF:], k_rope[:, :HALF], k_rope[:, HALF:], cos, sin)
    oq1, oq2, ok1, ok2 = _rows_fwd("rope", _rope_fn, rows + aux, [],
                                   [((H_MLA, T, HALF), F32, 1, 256)] * 2 + [((T, HALF), F32, 0, 256)] * 2)
    k_pe = jnp.concatenate([ok1, ok2], axis=-1)
    qh = jnp.concatenate([q8[:, :, :NOPE], oq1, oq2], axis=-1).astype(BF16)
    kh = jnp.concatenate([kv8[:, :, :NOPE], jnp.broadcast_to(k_pe[None], (H_MLA, T, ROPE))], axis=-1).astype(BF16)
    vh = kv8[:, :, NOPE:].astype(BF16)
    y_mla, lse_mla = _mla_fwd(qh, kh, vh)
    s.update(cqn=cqn, ckvn=ckvn, qh=qh, kh=kh, vh=vh, lse_mla=lse_mla)

    a3 = lambda n: sp[n].reshape(SSM_G, 1, SSM_N)
    b2 = lambda n: sp[n].transpose(0, 2, 1).reshape(SSM_G * SSM_P, SSM_N)
    disc_rows = [_row(a3('a_re'), 1, 0), _row(a3('a_im'), 1, 0), _row(sp['log_dt'].reshape(SSM_G, 1, 1), 1, 0),
                 _row(b2('b_re'), SSM_P, 0), _row(b2('b_im'), SSM_P, 0)]
    abr, abi, bbr, bbi = _rows_fwd(
        "s5_disc", _s5_disc_fn, disc_rows, [],
        [((SSM_G, 1, SSM_N), F32, 0, 1), ((SSM_G, 1, SSM_N), F32, 0, 1),
         ((SSM_G * SSM_P, SSM_N), F32, 0, SSM_P), ((SSM_G * SSM_P, SSM_N), F32, 0, SSM_P)])
    ar, ai = abr.reshape(1, SSM_S), abi.reshape(1, SSM_S)
    b_mat = jnp.concatenate([_block_diag(bbr.reshape(SSM_G, SSM_P, SSM_N)),
                             _block_diag(bbi.reshape(SSM_G, SSM_P, SSM_N))], axis=1)
    c_mat = jnp.concatenate([_block_diag(sp['c_re']), _block_diag(-sp['c_im'])], axis=1)
    bu = _mm("mm_s5_b", u, b_mat, 'nn', prec=HI)
    hst = _scan_fwd(bu, ar, ai)
    ymm = _mm("mm_s5_c", hst, c_mat, 'nt', prec=HI)
    d_row = sp['d_skip'].reshape(1, SSM_W)
    (yg,) = _rows_fwd("s5_act", _s5_act_fn, [_row(ymm), _row(u)], [d_row], [((T, SSM_W), BF16, -2, _row(u)[2])])
    z = _mm("mm_glu", yg, w['w_glu'], 'nn', bb='c')
    glu_rows = [_row(z[:, :SSM_W]), _row(z[:, SSM_W:])]
    glu_b = [sp['b_glu'][:SSM_W].reshape(1, -1), sp['b_glu'][SSM_W:].reshape(1, -1)]
    (y_ssm,) = _rows_fwd("s5_glu", _glu_fn, glu_rows, glu_b, [((T, SSM_W), F32, -2, glu_rows[0][2])])
    s.update(disc_rows=disc_rows, ar=ar, ai=ai, b_mat=b_mat, c_mat=c_mat, hst=hst, ymm=ymm, d_row=d_row, yg=yg,
             glu_rows=glu_rows, glu_b=glu_b)

    f3 = lambda a: _fold3(a.reshape(T, DIL_H, DIL_D)).astype(BF16)
    qf, kf, vf = f3(qd), f3(kd), f3(vd)
    o_f, lse_f = _band_fwd(qf, kf, vf)
    o_n, lse_n = _unfold3(o_f), _unfold3(lse_f)
    mix_rows = [_row(o_n[p], 256, 1) for p in range(3)] + [_row(lse_n[p], 256, 1) for p in range(3)]
    (y_dh,) = _rows_fwd("dil_mix", _dil_mix_fn, mix_rows, [], [((DIL_H, T, DIL_D), F32, 1, 256)])
    y_dil = y_dh.transpose(1, 0, 2).reshape(T, DIL_W)
    s.update(qf=qf, kf=kf, vf=vf, o_f=o_f, lse_f=lse_f, mix_rows=mix_rows)

    gm, gs, gd = sp['g_out_mla'].reshape(1, -1), sp['g_out_ssm'].reshape(1, -1), sp['g_out_dil'].reshape(1, -1)
    on_rows = [_row(y_mla), _row(y_ssm), _row(y_dil)]
    (ycat,) = _rows_fwd("out_norm", _outnorm_fn, on_rows, [gm, gs, gd], [((T, D), BF16, -2, on_rows[0][2])])
    x1_ = _mm("mm_o", ycat, w['w_o'], 'nn', bb='r', res=x)
    h2 = _rms_fwd("rms_ffn", x1_, sp['g_ffn'])
    ga = _mm("mm_gate", h2, w['w_gate'], 'nn', bb='c', ob='c')
    gb = _mm("mm_up", h2, w['w_up'], 'nn', bb='c', ob='c')
    ffn_rows = [_row(ga, None, 1), _row(gb, None, 1)]
    (zf,) = _rows_fwd("swiglu", _swiglu_fn, ffn_rows, [], [(ga.shape, BF16, 1, ffn_rows[0][2])])
    x2_ = _mm("mm_down", zf, w['w_down'], 'nn', ab='c', bb='r', res=x1_)
    s.update(on_rows=on_rows, on_g=[gm, gs, gd], ycat=ycat, x1=x1_, h2=h2, ffn_rows=ffn_rows, zf=zf)
    return x2_, s


def _layer_bwd(dx2, s, w, sp, cos, sin):
    gw, gs_ = {}, {}
    b16 = lambda a: a.astype(BF16)
    dx2b = b16(dx2)
    dzf = _mm("mm_down_dx", dx2b, w['w_down'], 'nt', bb='r', ob='c')
    gw['w_down'] = _mm("mm_down_dw", s['zf'], dx2b, 'tn', ab='c', ob='r')
    tr = s['ffn_rows'][0][2]
    dga, dgb = _rows_vjp("swiglu_bwd", _swiglu_fn, s['ffn_rows'], [], [], [_row(dzf, tr, 1)], grad_dtypes=[BF16, BF16])
    gw['w_gate'] = _mm("mm_gate_dw", s['h2'], dga, 'tn', bb='c', ob='c')
    gw['w_up'] = _mm("mm_up_dw", s['h2'], dgb, 'tn', bb='c', ob='c')
    dh2 = _mm("mm_up_dx", dgb, w['w_up'], 'nt', ab='c', bb='c',
              res=_mm("mm_gate_dx", dga, w['w_gate'], 'nt', ab='c', bb='c'))
    dx1, gs_['g_ffn'] = _rms_bwd("rms_ffn_bwd", s['x1'], sp['g_ffn'], dh2, dx2)
    dx1b = b16(dx1)
    dycat = _mm("mm_o_dx", dx1b, w['w_o'], 'nt', bb='r')
    gw['w_o'] = _mm("mm_o_dw", s['ycat'], dx1b, 'tn', ob='r')
    dy_mla, dy_ssm, dy_dil, gs_['g_out_mla'], gs_['g_out_ssm'], gs_['g_out_dil'] = _rows_vjp(
        "out_norm_bwd", _outnorm_fn, s['on_rows'], s['on_g'], [], [_row(dycat)])

    do_h = dy_dil.reshape(T, DIL_H, DIL_D).transpose(1, 0, 2)
    dmix = _rows_vjp("dil_mix_bwd", _dil_mix_fn, s['mix_rows'], [], [], [_row(do_h, 256, 1)])
    do_n, dlse_n = jnp.stack(dmix[:3]), jnp.stack(dmix[3:])
    dqf, dkf, dvf = _band_bwd(s['qf'], s['kf'], s['vf'], s['o_f'], s['lse_f'], _refold3(do_n), _refold3(dlse_n))
    back = lambda a: _unfold3(a).sum(axis=0).transpose(1, 0, 2).reshape(T, DIL_W)
    dqd, dkd, dvd = back(dqf), back(dkf), back(dvf)

    dz1, dz2, db1, db2 = _rows_vjp("s5_glu_bwd", _glu_fn, s['glu_rows'], s['glu_b'], [], [_row(dy_ssm)])
    gs_['b_glu'] = jnp.concatenate([db1, db2], axis=1)
    dzb = b16(jnp.concatenate([dz1, dz2], axis=1))
    dyg = _mm("mm_glu_dx", dzb, w['w_glu'], 'nt', bb='c')
    gw['w_glu'] = _mm("mm_glu_dw", s['yg'], dzb, 'tn', ob='c')
    dymm, du_act, dd = _rows_vjp("s5_act_bwd", _s5_act_fn, [_row(s['ymm']), _row(s['u'])], [s['d_row']], [], [_row(dyg)])
    gs_['d_skip'] = dd
    dhst = _mm("mm_s5_c_dx", dymm, s['c_mat'], 'nn', prec=HI)
    dc_mat = _mm("mm_s5_c_dw", dymm, s['hst'], 'tn', prec=HI)
    g, dar, dai = _scan_bwd(dhst, s['hst'], s['ar'], s['ai'])
    du = _mm("mm_s5_b_dx", g, s['b_mat'], 'nt', prec=HI, res=du_act)
    db_mat = _mm("mm_s5_b_dw", s['u'], g, 'tn', prec=HI)
    gs_['c_re'] = _diag_blocks(dc_mat[:, :SSM_S])
    gs_['c_im'] = -_diag_blocks(dc_mat[:, SSM_S:])
    dbbr = _diag_blocks(db_mat[:, :SSM_S]).reshape(SSM_G * SSM_P, SSM_N)
    dbbi = _diag_blocks(db_mat[:, SSM_S:]).reshape(SSM_G * SSM_P, SSM_N)
    disc_cts = [_row(dar.reshape(SSM_G, 1, SSM_N), 1, 0), _row(dai.reshape(SSM_G, 1, SSM_N), 1, 0),
                _row(dbbr, SSM_P, 0), _row(dbbi, SSM_P, 0)]
    da_re, da_im, dldt, db_r, db_i = _rows_vjp("s5_disc_bwd", _s5_disc_fn, s['disc_rows'], [], [], disc_cts)
    gs_['a_re'], gs_['a_im'], gs_['log_dt'] = da_re, da_im, dldt
    unb = lambda a: a.reshape(SSM_G, SSM_P, SSM_N).transpose(0, 2, 1)
    gs_['b_re'], gs_['b_im'] = unb(db_r), unb(db_i)

    dqh, dkh, dvh, dkpe = _mla_bwd(s['qh'], s['kh'], s['vh'], dy_mla, s['lse_mla'])
    zq, zk = jnp.zeros((H_MLA, T, HALF), F32), jnp.zeros((T, HALF), F32)
    rows, aux = _rope_rows(zq, zq, zk, zk, cos, sin)
    cts, _ = _rope_rows(dqh[:, :, NOPE:NOPE + HALF], dqh[:, :, NOPE + HALF:], dkpe[:, :HALF], dkpe[:, HALF:], cos, sin)
    dq1, dq2, dk1, dk2 = _rows_vjp("rope_bwd", _rope_fn, rows, [], aux, cts)
    dq8 = b16(jnp.concatenate([dqh[:, :, :NOPE], dq1, dq2], axis=-1))
    dkv8 = b16(jnp.concatenate([dkh[:, :, :NOPE], dvh], axis=-1))
    dk_rope = jnp.concatenate([dk1, dk2], axis=-1)
    dcqn = _mm("mm_uq_dx", dq8, w['w_uq'], 'nt', ab='c', bb='c')
    gw['w_uq'] = _mm("mm_uq_dw", s['cqn'], dq8, 'tn', bb='c', ob='c')
    dckvn = _mm("mm_ukv_dx", dkv8, w['w_ukv'], 'nt', ab='c', bb='c')
    gw['w_ukv'] = _mm("mm_ukv_dw", s['ckvn'], dkv8, 'tn', bb='c', ob='c')
    dc_q, gs_['g_q'] = _rms_bwd("rms_q_bwd", s['c_q'], sp['g_q'], dcqn)
    dc_kv, gs_['g_kv'] = _rms_bwd("rms_kv_bwd", s['c_kv'], sp['g_kv'], dckvn)

    dproj = _to_blocks(b16(jnp.concatenate([dc_q, dc_kv, dk_rope, du, dqd, dkd, dvd], axis=1)))
    dh = _mm("mm_in_dx", dproj, w['w_in'], 'nt', ab='c', bb='c')
    gw['w_in'] = _mm("mm_in_dw", s['h'], dproj, 'tn', bb='c', ob='c')
    dx, gs_['g_mix'] = _rms_bwd("rms_mix_bwd", s['x'], sp['g_mix'], dh, dx1)
    return dx, gw, gs_


def kernel(x, g_mix, w_in, g_q, w_uq, g_kv, w_ukv, a_re, a_im, b_re, b_im, c_re, c_im, d_skip, log_dt, w_glu, b_glu, g_out_mla, g_out_ssm, g_out_dil, w_o, g_ffn, w_gate, w_up, w_down, g_final, loss_target, m_g_mix, m_w_in, m_g_q, m_w_uq, m_g_kv, m_w_ukv, m_a_re, m_a_im, m_b_re, m_b_im, m_c_re, m_c_im, m_d_skip, m_log_dt, m_w_glu, m_b_glu, m_g_out_mla, m_g_out_ssm, m_g_out_dil, m_w_o, m_g_ffn, m_w_gate, m_w_up, m_w_down, m_g_final, v_g_mix, v_w_in, v_g_q, v_w_uq, v_g_kv, v_w_ukv, v_a_re, v_a_im, v_b_re, v_b_im, v_c_re, v_c_im, v_d_skip, v_log_dt, v_w_glu, v_b_glu, v_g_out_mla, v_g_out_ssm, v_g_out_dil, v_w_o, v_g_ffn, v_w_gate, v_w_up, v_w_down, v_g_final):
    W = dict(zip(PARAMS, (g_mix, w_in, g_q, w_uq, g_kv, w_ukv, a_re, a_im, b_re, b_im, c_re, c_im, d_skip, log_dt,
                          w_glu, b_glu, g_out_mla, g_out_ssm, g_out_dil, w_o, g_ffn, w_gate, w_up, w_down, g_final)))
    M = dict(zip(PARAMS, (m_g_mix, m_w_in, m_g_q, m_w_uq, m_g_kv, m_w_ukv, m_a_re, m_a_im, m_b_re, m_b_im, m_c_re,
                          m_c_im, m_d_skip, m_log_dt, m_w_glu, m_b_glu, m_g_out_mla, m_g_out_ssm, m_g_out_dil, m_w_o,
                          m_g_ffn, m_w_gate, m_w_up, m_w_down, m_g_final)))
    V = dict(zip(PARAMS, (v_g_mix, v_w_in, v_g_q, v_w_uq, v_g_kv, v_w_ukv, v_a_re, v_a_im, v_b_re, v_b_im, v_c_re,
                          v_c_im, v_d_skip, v_log_dt, v_w_glu, v_b_glu, v_g_out_mla, v_g_out_ssm, v_g_out_dil, v_w_o,
                          v_g_ffn, v_w_gate, v_w_up, v_w_down, v_g_final)))
    cx, cy, cc = _place()
    core = cc.astype(jnp.int32).reshape(1)
    chip = (2 * cx + cy).astype(jnp.int32).reshape(1)
    cos, sin = _rope_tables()
    big = list(BIG)

    full = []
    for l in range(DEPTH):
        got = _all_gather("gather_weights", [W[n][l].astype(BF16) for n in big])
        full.append(dict(zip(big, got)))
    small = [{n: W[n][l] for n in SMALL} for l in range(DEPTH)]
    for sp in small:
        for n in ('g_mix', 'g_q', 'g_kv', 'g_ffn'):
            sp[n] = sp[n].reshape(1, -1)

    xa = x[0]
    saved = []
    for l in range(DEPTH):
        xa, s = _layer_fwd(xa, full[l], small[l], cos, sin)
        saved.append(s)
    gf = g_final.reshape(1, D)
    ones = jnp.ones((T, 1), F32)
    dxa, dgf, loss_rows = _rows_vjp("loss", _loss_fn, [_row(xa)], [gf], [_row(loss_target[0])], [_row(ones)],
                                    primal=True)
    loss = lax.psum(_sum_rows("loss_sum", loss_rows)[0, 0], ("x", "y", "c"))

    g_small = [None] * DEPTH
    bufs = {n: None for n in big}
    for l in reversed(range(DEPTH)):
        dxa, gw, g_small[l] = _layer_bwd(dxa, saved[l], full[l], small[l], cos, sin)
        p4 = [gw[n].reshape((4, 2) + gw[n].shape[1:]) for n in big]
        from_sibling = _swap_sibling("rs_swap_sibling", p4)
        s4 = [_pair_sum("rs_pair_sum_" + n, p, g, core) for n, p, g in zip(big, p4, from_sibling)]
        from_chips = _swap_chips("rs_swap_chips", s4)
        for n, s4n, got in zip(big, s4, from_chips):
            bufs[n] = _adamw_shard("adamw_" + n, l, W[n], M[n], V[n], s4n, got, chip, bufs[n])

    flat = [g_small[l][n].reshape(-1) for l in range(DEPTH) for n in SMALL] + [dgf.reshape(-1)]
    n_small = sum(int(f.shape[0]) for f in flat)
    rows = -(-n_small // (PACK_C * SMALL_ROW_ALIGN)) * SMALL_ROW_ALIGN
    flat = jnp.concatenate(flat + [jnp.zeros((rows * PACK_C - n_small,), F32)]).reshape(rows, PACK_C)
    (gathered,) = _all_gather("gather_small", [flat])
    tot = _sum_devices("small_sum", gathered).reshape(-1)
    grads, off = {}, 0
    per_layer = {n: [] for n in SMALL}
    for l in range(DEPTH):
        for n, shp in SMALL.items():
            k = int(np.prod(shp))
            per_layer[n].append(tot[off:off + k].reshape(shp))
            off += k
    for n in SMALL:
        grads[n] = jnp.stack(per_layer[n])
    grads['g_final'] = tot[off:off + D]

    delta, new_m, new_v = {}, {}, {}
    for n in PARAMS:
        if n in BIG:
            grads[n], delta[n], new_m[n], new_v[n] = bufs[n]
        else:
            delta[n], new_m[n], new_v[n] = _adamw("adamw_" + n, W[n], grads[n], M[n], V[n])
    return (loss, dxa[None], *[grads[n] for n in PARAMS], *[delta[n] for n in PARAMS],
            *[new_m[n] for n in PARAMS], *[new_v[n] for n in PARAMS])
```

```python
import jax
import jax.numpy as jnp
import numpy as np
from jax import lax
from jax.experimental import pallas as pl
from jax.experimental.pallas import tpu as pltpu

F32 = jnp.float32
BF16 = jnp.bfloat16

T = 2048
D = 2048
DEPTH = 4
N_DEV = 8
H_MLA, NOPE, ROPE, VDIM = 8, 128, 64, 128
QK = NOPE + ROPE
Q_LORA, KV_LORA = 512, 256
SSM_W, SSM_G, SSM_P, SSM_N = 512, 32, 16, 64
SSM_S = SSM_G * SSM_N
DIL_W, DIL_H, DIL_D = 512, 8, 64
BLK = 128
IN_SPLITS = (Q_LORA, KV_LORA, ROPE, SSM_W, DIL_W, DIL_W, DIL_W)
IN_W = sum(IN_SPLITS)
D_FF = 5632
EPS = 1e-6
ROPE_THETA = 10000.0
MLA_SCALE = QK ** -0.5
DIL_SCALE = DIL_D ** -0.5

ADAM_LR, ADAM_B1, ADAM_B2, ADAM_EPS, ADAM_WD, ADAM_STEP = 0.001, 0.9, 0.999, 1e-08, 0.01, 10

VMEM_LIMIT_V7X = 52 * 1024 * 1024
LANES = 128
PACK_C = 1024
ROW_BLOCK_BYTES = 2 * 1024 * 1024

NT = (((1,), (1,)), ((), ()))
TN = (((0,), (0,)), ((), ()))
HI = lax.Precision.HIGHEST
MESH = pl.DeviceIdType.MESH

PARAMS = ['g_mix', 'w_in', 'g_q', 'w_uq', 'g_kv', 'w_ukv', 'a_re', 'a_im', 'b_re', 'b_im', 'c_re', 'c_im',
          'd_skip', 'log_dt', 'w_glu', 'b_glu', 'g_out_mla', 'g_out_ssm', 'g_out_dil', 'w_o', 'g_ffn',
          'w_gate', 'w_up', 'w_down', 'g_final']
BIG = {'w_in': 'c', 'w_uq': 'c', 'w_ukv': 'c', 'w_glu': 'c', 'w_o': 'r', 'w_gate': 'c', 'w_up': 'c', 'w_down': 'r'}
MIXER_W = ['w_in', 'w_uq', 'w_ukv', 'w_glu', 'w_o']
FFN_W = ['w_gate', 'w_up', 'w_down']
SMALL = {'g_mix': (D,), 'g_q': (Q_LORA,), 'g_kv': (KV_LORA,), 'a_re': (SSM_G, SSM_N), 'a_im': (SSM_G, SSM_N),
         'b_re': (SSM_G, SSM_N, SSM_P), 'b_im': (SSM_G, SSM_N, SSM_P), 'c_re': (SSM_G, SSM_P, SSM_N),
         'c_im': (SSM_G, SSM_P, SSM_N), 'd_skip': (SSM_G, SSM_P), 'log_dt': (SSM_G,), 'b_glu': (2 * SSM_W,),
         'g_out_mla': (H_MLA * VDIM,), 'g_out_ssm': (SSM_W,), 'g_out_dil': (DIL_W,), 'g_ffn': (D,)}
SMALL_ROW_ALIGN = 64


def _tile(dim, target, align=LANES):
    best = None
    for t in range(align, min(dim, target) + 1, align):
        if dim % t == 0:
            best = t
    return best if best is not None else dim


def _params(sem=None):
    return pltpu.CompilerParams(dimension_semantics=sem, vmem_limit_bytes=VMEM_LIMIT_V7X)


def _dot(a, b, dims=None, prec=None):
    if dims is None:
        return jnp.dot(a, b, preferred_element_type=F32, precision=prec)
    return lax.dot_general(a, b, dims, preferred_element_type=F32, precision=prec)


def _mm_spec(shape, blk, t_r, t_c, rc):
    if blk is None:
        return pl.BlockSpec((t_r, t_c), rc)
    _, R, C = shape
    if blk == 'r':
        per = R // t_r
        return pl.BlockSpec((1, t_r, t_c), lambda i, j, k: (rc(i, j, k)[0] // per, rc(i, j, k)[0] % per, rc(i, j, k)[1]))
    per = C // t_c
    return pl.BlockSpec((1, t_r, t_c), lambda i, j, k: (rc(i, j, k)[1] // per, rc(i, j, k)[0], rc(i, j, k)[1] % per))


def _logical(shape, blk):
    if blk is None:
        return tuple(shape)
    G, R, C = shape
    return (G * R, C) if blk == 'r' else (R, G * C)


def _mm(name, a, b, mode, ab=None, bb=None, ob=None, res=None, prec=None):
    la, lb = _logical(a.shape, ab), _logical(b.shape, bb)
    am, ak = (0, 1) if mode != 'tn' else (1, 0)
    bk, bn = (0, 1) if mode != 'nt' else (1, 0)
    M, K, N = la[am], la[ak], lb[bn]
    assert lb[bk] == K, (name, a.shape, b.shape, mode)
    if ob is None:
        out_shape = (M, N)
    elif ob == 'r':
        G = N_DEV
        out_shape = (G, M // G, N)
    else:
        G = N_DEV
        out_shape = (G, M, N // G)
    em = min(a.shape[-2:][am], out_shape[-2])
    en = min(b.shape[-2:][bn], out_shape[-1])
    ek = min(a.shape[-2:][ak], b.shape[-2:][bk])
    tn = _tile(en, 512)
    tm = _tile(em, 512 if tn > 1024 else 1024)
    tk = _tile(ek, 512)
    nk = K // tk
    dims = {'nn': None, 'nt': NT, 'tn': TN}[mode]

    def val(ref):
        return ref[...] if len(ref.shape) == 2 else ref[0]

    def body(*refs):
        if res is None:
            a_ref, b_ref, o_ref, acc_ref = refs
        else:
            a_ref, b_ref, r_ref, o_ref, acc_ref = refs
        k = pl.program_id(2)

        @pl.when(k == 0)
        def _():
            acc_ref[...] = jnp.zeros_like(acc_ref)

        acc_ref[...] += _dot(val(a_ref), val(b_ref), dims, prec)

        @pl.when(k == nk - 1)
        def _():
            r = acc_ref[...] if res is None else acc_ref[...] + val(r_ref)
            if len(o_ref.shape) == 2:
                o_ref[...] = r
            else:
                o_ref[0] = r

    if mode == 'tn':
        a_spec = _mm_spec(a.shape, ab, tk, tm, lambda i, j, k: (k, i))
    else:
        a_spec = _mm_spec(a.shape, ab, tm, tk, lambda i, j, k: (i, k))
    if mode == 'nt':
        b_spec = _mm_spec(b.shape, bb, tn, tk, lambda i, j, k: (j, k))
    else:
        b_spec = _mm_spec(b.shape, bb, tk, tn, lambda i, j, k: (k, j))
    o_spec = _mm_spec(out_shape, ob, tm, tn, lambda i, j, k: (i, j))
    in_specs = [a_spec, b_spec] + ([o_spec] if res is not None else [])
    args = (a, b) + ((res,) if res is not None else ())
    return pl.pallas_call(
        body, name=name, out_shape=jax.ShapeDtypeStruct(out_shape, F32),
        grid=(M // tm, N // tn, nk), in_specs=in_specs, out_specs=o_spec,
        scratch_shapes=[pltpu.VMEM((tm, tn), F32)],
        compiler_params=_params(("parallel", "parallel", "arbitrary")),
    )(*args)


def _row(a, tr=None, axis=-2):
    axis = axis % a.ndim
    n = a.shape[axis]
    if tr is None:
        row_bytes = a.size // n * 4
        tr = _tile(n, max(8, min(256, ROW_BLOCK_BYTES // row_bytes)), 8)
    return (a, axis, tr)


def _row_spec(shape, axis, tr):
    nd = len(shape)
    blk = tuple(tr if d == axis else s for d, s in enumerate(shape))
    return pl.BlockSpec(blk, lambda i: tuple(i if d == axis else 0 for d in range(nd)))


def _full_spec(shape):
    nd = len(shape)
    return pl.BlockSpec(tuple(shape), lambda i: (0,) * nd)


def _steps(entries):
    ns = {a.shape[ax] // tr for a, ax, tr in entries}
    assert len(ns) == 1, [(a.shape, ax, tr) for a, ax, tr in entries]
    return ns.pop()


def _as_tuple(r):
    return tuple(r) if isinstance(r, (tuple, list)) else (r,)


def _rows_fwd(name, fn, rows, bcast, outs):
    steps = _steps(rows)
    nr, nb = len(rows), len(bcast)

    def body(*refs):
        vals = [r[...] for r in refs[:nr + nb]]
        res = _as_tuple(fn(*vals))
        for o_ref, r in zip(refs[nr + nb:], res):
            o_ref[...] = r.astype(o_ref.dtype)

    in_specs = [_row_spec(a.shape, ax, tr) for a, ax, tr in rows] + [_full_spec(b.shape) for b in bcast]
    out_specs = [_row_spec(s, ax % len(s), tr) for s, _, ax, tr in outs]
    res = pl.pallas_call(
        body, name=name, out_shape=[jax.ShapeDtypeStruct(s, dt) for s, dt, _, _ in outs],
        grid=(steps,), in_specs=in_specs, out_specs=out_specs,
        compiler_params=_params(("parallel",)),
    )(*[a for a, _, _ in rows], *bcast)
    return res


def _rows_vjp(name, fn, drows, dbc, arows, cts, primal=False, grad_dtypes=None):
    entries = list(drows) + list(arows) + list(cts)
    steps = _steps(entries)
    ndr, ndb, nar, nct = len(drows), len(dbc), len(arows), len(cts)
    gdt = list(grad_dtypes) if grad_dtypes is not None else [F32] * ndr

    def body(*refs):
        p = 0
        dr = [r[...] for r in refs[p:p + ndr]]; p += ndr
        db = [r[...] for r in refs[p:p + ndb]]; p += ndb
        ar = [r[...] for r in refs[p:p + nar]]; p += nar
        ct = [r[...] for r in refs[p:p + nct]]; p += nct
        g_rows = refs[p:p + ndr]; p += ndr
        g_bc = refs[p:p + ndb]; p += ndb
        prim_refs = refs[p:]

        def f(*d):
            return _as_tuple(fn(*d, *ar))

        outs, pullback = jax.vjp(f, *dr, *db)
        grads = pullback(tuple(c.astype(o.dtype) for c, o in zip(ct, outs)))
        for k in range(ndr):
            g_rows[k][...] = grads[k].astype(g_rows[k].dtype)
        if ndb:
            @pl.when(pl.program_id(0) == 0)
            def _():
                for r in g_bc:
                    r[...] = jnp.zeros_like(r)
            for k in range(ndb):
                g_bc[k][...] += grads[ndr + k]
        for r, o in zip(prim_refs, outs):
            r[...] = o.astype(r.dtype)

    in_specs = ([_row_spec(a.shape, ax, tr) for a, ax, tr in drows] + [_full_spec(b.shape) for b in dbc]
                + [_row_spec(a.shape, ax, tr) for a, ax, tr in arows]
                + [_row_spec(a.shape, ax, tr) for a, ax, tr in cts])
    out_shape = ([jax.ShapeDtypeStruct(a.shape, dt) for (a, _, _), dt in zip(drows, gdt)]
                 + [jax.ShapeDtypeStruct(b.shape, F32) for b in dbc])
    out_specs = ([_row_spec(a.shape, ax, tr) for a, ax, tr in drows] + [_full_spec(b.shape) for b in dbc])
    if primal:
        out_shape += [jax.ShapeDtypeStruct(a.shape, F32) for a, _, _ in cts]
        out_specs += [_row_spec(a.shape, ax, tr) for a, ax, tr in cts]
    return pl.pallas_call(
        body, name=name, out_shape=out_shape, grid=(steps,), in_specs=in_specs, out_specs=out_specs,
        compiler_params=_params(("arbitrary",)),
    )(*[a for a, _, _ in drows], *dbc, *[a for a, _, _ in arows], *[a for a, _, _ in cts])


def _rms_fn(x, g):
    return x * lax.rsqrt(jnp.mean(x * x, axis=-1, keepdims=True) + EPS) * g


def _rms_res_fn(x, g):
    return _rms_fn(x, g), x


def _rope_fn(q1, q2, k1, k2, cos, sin):
    return q1 * cos - q2 * sin, q2 * cos + q1 * sin, k1 * cos - k2 * sin, k2 * cos + k1 * sin


def _s5_act_fn(ymm, u, d):
    return jax.nn.gelu(ymm + d * u)


def _glu_fn(z1, z2, b1, b2):
    return (z1 + b1) * jax.nn.sigmoid(z2 + b2)


def _outnorm_fn(ym, ys, yd, gm, gs, gd):
    return jnp.concatenate([_rms_fn(ym, gm), _rms_fn(ys, gs), _rms_fn(yd, gd)], axis=-1)


def _swiglu_fn(a, b):
    return jax.nn.silu(a) * b


def _loss_fn(x, g, tgt):
    err = _rms_fn(x, g) - tgt
    return 0.5 * jnp.mean(err * err, axis=-1, keepdims=True)


def _dil_mix_fn(o0, o1, o2, l0, l1, l2):
    m = jnp.maximum(jnp.maximum(l0, l1), l2)
    e0, e1, e2 = jnp.exp(l0 - m), jnp.exp(l1 - m), jnp.exp(l2 - m)
    s = e0 + e1 + e2
    return (e0 / s) * o0 + (e1 / s) * o1 + (e2 / s) * o2


def _s5_disc_fn(a_re, a_im, ldt, b_r, b_i):
    lr = jnp.minimum(a_re.reshape(1, SSM_N), -1e-4)
    li = a_im.reshape(1, SSM_N)
    dt = jnp.exp(ldt.reshape(1, 1))
    e = jnp.exp(lr * dt)
    ar = e * jnp.cos(li * dt)
    ai = e * jnp.sin(li * dt)
    nr, ni = ar - 1.0, ai
    den = lr * lr + li * li
    cr = (nr * lr + ni * li) / den
    ci = (ni * lr - nr * li) / den
    return ar.reshape(1, 1, SSM_N), ai.reshape(1, 1, SSM_N), cr * b_r - ci * b_i, cr * b_i + ci * b_r


def _adamw_fn(w, g, m, v):
    m = ADAM_B1 * m + (1.0 - ADAM_B1) * g
    v = ADAM_B2 * v + (1.0 - ADAM_B2) * jnp.square(g)
    m_hat = m / (1.0 - ADAM_B1 ** ADAM_STEP)
    v_hat = v / (1.0 - ADAM_B2 ** ADAM_STEP)
    delta = -ADAM_LR * (m_hat / (jnp.sqrt(v_hat) + ADAM_EPS) + ADAM_WD * w)
    return delta, m, v


def _rms_fwd(name, x, g):
    (h,) = _rows_fwd(name, _rms_fn, [_row(x)], [g], [(x.shape, BF16, -2, _row(x)[2])])
    return h


def _rms_bwd(name, x, g, dh, dres=None):
    if dres is None:
        dx, dg = _rows_vjp(name, _rms_fn, [_row(x)], [g], [], [_row(dh)])
    else:
        dx, dg = _rows_vjp(name, _rms_res_fn, [_row(x)], [g], [], [_row(dh), _row(dres)])
    return dx, dg


MLA_TQ = 256


def _mla_fwd(q, k, v):
    tq = MLA_TQ

    def body(q_ref, k_ref, v_ref, o_ref, lse_ref):
        i = pl.program_id(1)
        s = _dot(q_ref[0], k_ref[0], NT) * MLA_SCALE
        row = i * tq + lax.broadcasted_iota(jnp.int32, (tq, T), 0)
        col = lax.broadcasted_iota(jnp.int32, (tq, T), 1)
        s = jnp.where(row >= col, s, -jnp.inf)
        m = jnp.max(s, axis=-1, keepdims=True)
        p = jnp.exp(s - m)
        l = jnp.sum(p, axis=-1, keepdims=True)
        o_ref[...] = _dot((p / l).astype(BF16), v_ref[0])
        lse_ref[0] = m + jnp.log(l)

    return pl.pallas_call(
        body, name="mla_fwd",
        out_shape=[jax.ShapeDtypeStruct((T, H_MLA * VDIM), F32), jax.ShapeDtypeStruct((H_MLA, T, 1), F32)],
        grid=(H_MLA, T // tq),
        in_specs=[pl.BlockSpec((1, tq, QK), lambda h, i: (h, i, 0)),
                  pl.BlockSpec((1, T, QK), lambda h, i: (h, 0, 0)),
                  pl.BlockSpec((1, T, VDIM), lambda h, i: (h, 0, 0))],
        out_specs=[pl.BlockSpec((tq, VDIM), lambda h, i: (i, h)),
                   pl.BlockSpec((1, tq, 1), lambda h, i: (h, i, 0))],
        compiler_params=_params(("parallel", "parallel")),
    )(q, k, v)


def _mla_bwd(q, k, v, do, lse):
    tq = MLA_TQ

    def body(q_ref, k_ref, v_ref, do_ref, lse_ref, dq_ref, dk_ref, dv_ref, dkpe_ref):
        h, i = pl.program_id(0), pl.program_id(1)

        @pl.when(i == 0)
        def _():
            dk_ref[...] = jnp.zeros_like(dk_ref)
            dv_ref[...] = jnp.zeros_like(dv_ref)

        @pl.when((i == 0) & (h == 0))
        def _():
            dkpe_ref[...] = jnp.zeros_like(dkpe_ref)

        q, k, v = q_ref[0], k_ref[0], v_ref[0]
        do = do_ref[...].astype(BF16)
        s = _dot(q, k, NT) * MLA_SCALE
        row = i * tq + lax.broadcasted_iota(jnp.int32, (tq, T), 0)
        col = lax.broadcasted_iota(jnp.int32, (tq, T), 1)
        p = jnp.where(row >= col, jnp.exp(s - lse_ref[0]), 0.0)
        dp = _dot(do, v, NT)
        delta = jnp.sum(dp * p, axis=-1, keepdims=True)
        ds = (p * (dp - delta) * MLA_SCALE).astype(BF16)
        dq_ref[0] = _dot(ds, k)
        dk = _dot(ds, q, TN)
        dk_ref[0] += dk
        dkpe_ref[...] += dk[:, NOPE:]
        dv_ref[0] += _dot(p.astype(BF16), do, TN)

    return pl.pallas_call(
        body, name="mla_bwd",
        out_shape=[jax.ShapeDtypeStruct((H_MLA, T, QK), F32), jax.ShapeDtypeStruct((H_MLA, T, QK), F32),
                   jax.ShapeDtypeStruct((H_MLA, T, VDIM), F32), jax.ShapeDtypeStruct((T, ROPE), F32)],
        grid=(H_MLA, T // tq),
        in_specs=[pl.BlockSpec((1, tq, QK), lambda h, i: (h, i, 0)),
                  pl.BlockSpec((1, T, QK), lambda h, i: (h, 0, 0)),
                  pl.BlockSpec((1, T, VDIM), lambda h, i: (h, 0, 0)),
                  pl.BlockSpec((tq, VDIM), lambda h, i: (i, h)),
                  pl.BlockSpec((1, tq, 1), lambda h, i: (h, i, 0))],
        out_specs=[pl.BlockSpec((1, tq, QK), lambda h, i: (h, i, 0)),
                   pl.BlockSpec((1, T, QK), lambda h, i: (h, 0, 0)),
                   pl.BlockSpec((1, T, VDIM), lambda h, i: (h, 0, 0)),
                   pl.BlockSpec((T, ROPE), lambda h, i: (0, 0))],
        compiler_params=_params(("arbitrary", "arbitrary")),
    )(q, k, v, do, lse)


NBLK = T // BLK


def _band_masks():
    r = lax.broadcasted_iota(jnp.int32, (BLK, BLK), 0)
    j = lax.broadcasted_iota(jnp.int32, (BLK, BLK), 1)
    return j <= r, j >= r


def _seq_start(p, i):
    per_seq = lax.shift_right_logical(jnp.int32(NBLK), 2 * p)
    return lax.rem(i, per_seq) == 0


def _band_fwd(q, k, v):
    def body(q_ref, kp_ref, kc_ref, vp_ref, vc_ref, o_ref, lse_ref):
        p, i = pl.program_id(0), pl.program_id(1)
        has_prev = jnp.logical_not(_seq_start(p, i))
        m_cur, m_prev = _band_masks()
        m_prev = m_prev & has_prev
        for h in range(DIL_H):
            qh = q_ref[0, h]
            s_c = jnp.where(m_cur, _dot(qh, kc_ref[0, h], NT) * DIL_SCALE, -jnp.inf)
            s_p = jnp.where(m_prev, _dot(qh, kp_ref[0, h], NT) * DIL_SCALE, -jnp.inf)
            m = jnp.maximum(jnp.max(s_c, axis=-1, keepdims=True), jnp.max(s_p, axis=-1, keepdims=True))
            e_c, e_p = jnp.exp(s_c - m), jnp.exp(s_p - m)
            l = jnp.sum(e_c, axis=-1, keepdims=True) + jnp.sum(e_p, axis=-1, keepdims=True)
            o_ref[0, h] = (_dot((e_p / l).astype(BF16), vp_ref[0, h]) + _dot((e_c / l).astype(BF16), vc_ref[0, h]))
            lse_ref[0, h] = m + jnp.log(l)

    blk = lambda w: (1, DIL_H, BLK, w)
    cur = lambda p, i: (p, 0, i, 0)
    prev = lambda p, i: (p, 0, jnp.maximum(i - 1, 0), 0)
    return pl.pallas_call(
        body, name="band_fwd",
        out_shape=[jax.ShapeDtypeStruct((3, DIL_H, T, DIL_D), F32), jax.ShapeDtypeStruct((3, DIL_H, T, 1), F32)],
        grid=(3, NBLK),
        in_specs=[pl.BlockSpec(blk(DIL_D), cur), pl.BlockSpec(blk(DIL_D), prev), pl.BlockSpec(blk(DIL_D), cur),
                  pl.BlockSpec(blk(DIL_D), prev), pl.BlockSpec(blk(DIL_D), cur)],
        out_specs=[pl.BlockSpec(blk(DIL_D), cur), pl.BlockSpec(blk(1), cur)],
        compiler_params=_params(("parallel", "parallel")),
    )(q, k, k, v, v)


def _band_bwd(q, k, v, o, lse, do, dlse):
    def body(qc_ref, qn_ref, kp_ref, kc_ref, vp_ref, vc_ref, oc_ref, on_ref, lc_ref, ln_ref,
             doc_ref, don_ref, dlc_ref, dln_ref, dq_ref, dk_ref, dv_ref):
        p, i = pl.program_id(0), pl.program_id(1)
        has_prev = jnp.logical_not(_seq_start(p, i))
        has_next = jnp.logical_not(_seq_start(p, i + 1)) & (i + 1 < NBLK)
        m_cur, m_prev = _band_masks()

        def probs(qh, kh, lse_h, mask):
            return jnp.where(mask, jnp.exp(_dot(qh, kh, NT) * DIL_SCALE - lse_h), 0.0)

        def dscore(pr, do_h, vh, shift):
            return (pr * (_dot(do_h, vh, NT) + shift) * DIL_SCALE).astype(BF16)

        for h in range(DIL_H):
            qc, qn, kp, kc, vp, vc = qc_ref[0, h], qn_ref[0, h], kp_ref[0, h], kc_ref[0, h], vp_ref[0, h], vc_ref[0, h]
            doc, don = doc_ref[0, h], don_ref[0, h]
            sh_c = dlc_ref[0, h] - jnp.sum(doc * oc_ref[0, h], axis=-1, keepdims=True)
            sh_n = dln_ref[0, h] - jnp.sum(don * on_ref[0, h], axis=-1, keepdims=True)
            doc, don = doc.astype(BF16), don.astype(BF16)
            p_cc = probs(qc, kc, lc_ref[0, h], m_cur)
            p_cp = probs(qc, kp, lc_ref[0, h], m_prev & has_prev)
            p_nc = probs(qn, kc, ln_ref[0, h], m_prev & has_next)
            ds_cc = dscore(p_cc, doc, vc, sh_c)
            ds_cp = dscore(p_cp, doc, vp, sh_c)
            ds_nc = dscore(p_nc, don, vc, sh_n)
            dq_ref[0, h] = _dot(ds_cc, kc) + _dot(ds_cp, kp)
            dk_ref[0, h] = _dot(ds_cc, qc, TN) + _dot(ds_nc, qn, TN)
            dv_ref[0, h] = _dot(p_cc.astype(BF16), doc, TN) + _dot(p_nc.astype(BF16), don, TN)

    blk = lambda w: (1, DIL_H, BLK, w)
    cur = lambda p, i: (p, 0, i, 0)
    prev = lambda p, i: (p, 0, jnp.maximum(i - 1, 0), 0)
    nxt = lambda p, i: (p, 0, jnp.minimum(i + 1, NBLK - 1), 0)
    w, one = pl.BlockSpec(blk(DIL_D), cur), pl.BlockSpec(blk(1), cur)
    wn, onen = pl.BlockSpec(blk(DIL_D), nxt), pl.BlockSpec(blk(1), nxt)
    wp = pl.BlockSpec(blk(DIL_D), prev)
    return pl.pallas_call(
        body, name="band_bwd",
        out_shape=[jax.ShapeDtypeStruct((3, DIL_H, T, DIL_D), F32)] * 3,
        grid=(3, NBLK),
        in_specs=[w, wn, wp, w, wp, w, w, wn, one, onen, w, wn, one, onen],
        out_specs=[w, w, w],
        compiler_params=_params(("parallel", "parallel")),
    )(q, q, k, k, v, v, o, o, lse, lse, do, do, dlse, dlse)


SCAN_TC = 256


def _scan_fwd(bu, ar, ai):
    tc, S = SCAN_TC, SSM_S

    def body(bu_ref, ar_ref, ai_ref, h_ref, cr_ref, ci_ref):
        @pl.when(pl.program_id(0) == 0)
        def _():
            cr_ref[...] = jnp.zeros_like(cr_ref)
            ci_ref[...] = jnp.zeros_like(ci_ref)

        a_r, a_i = ar_ref[...], ai_ref[...]

        def step(j, carry):
            hr, hi = carry
            for r in range(8):
                t = pl.multiple_of(j * 8, 8) + r
                br = bu_ref[pl.ds(t, 1), pl.ds(0, S)]
                bi = bu_ref[pl.ds(t, 1), pl.ds(S, S)]
                hr, hi = a_r * hr - a_i * hi + br, a_r * hi + a_i * hr + bi
                h_ref[pl.ds(t, 1), pl.ds(0, S)] = hr
                h_ref[pl.ds(t, 1), pl.ds(S, S)] = hi
            return hr, hi

        hr, hi = lax.fori_loop(0, tc // 8, step, (cr_ref[...], ci_ref[...]))
        cr_ref[...] = hr
        ci_ref[...] = hi

    return pl.pallas_call(
        body, name="s5_scan_fwd", out_shape=jax.ShapeDtypeStruct((T, 2 * S), F32),
        grid=(T // tc,),
        in_specs=[pl.BlockSpec((tc, 2 * S), lambda i: (i, 0)), _full_spec((1, S)), _full_spec((1, S))],
        out_specs=pl.BlockSpec((tc, 2 * S), lambda i: (i, 0)),
        scratch_shapes=[pltpu.VMEM((1, S), F32), pltpu.VMEM((1, S), F32)],
        compiler_params=_params(("arbitrary",)),
    )(bu, ar, ai)


def _scan_bwd(dh, h, ar, ai):
    tc, S = SCAN_TC, SSM_S
    nc = T // tc

    def body(dh_ref, h_ref, hp_ref, ar_ref, ai_ref, g_ref, dar_ref, dai_ref, cr_ref, ci_ref):
        i = pl.program_id(0)

        @pl.when(i == 0)
        def _():
            cr_ref[...] = jnp.zeros_like(cr_ref)
            ci_ref[...] = jnp.zeros_like(ci_ref)
            dar_ref[...] = jnp.zeros_like(dar_ref)
            dai_ref[...] = jnp.zeros_like(dai_ref)

        a_r, a_i = ar_ref[...], ai_ref[...]
        first_chunk = (i == nc - 1)
        edge = jnp.where(first_chunk, 0.0, 1.0)
        hpr = hp_ref[pl.ds(7, 1), pl.ds(0, S)] * edge
        hpi = hp_ref[pl.ds(7, 1), pl.ds(S, S)] * edge

        def step(jj, carry):
            gr, gi, dar, dai = carry
            j = tc // 8 - 1 - jj
            for r in range(7, -1, -1):
                t = pl.multiple_of(j * 8, 8) + r
                tp = jnp.maximum(t - 1, 0)
                inside = t > 0
                pr = jnp.where(inside, h_ref[pl.ds(tp, 1), pl.ds(0, S)], hpr)
                pi = jnp.where(inside, h_ref[pl.ds(tp, 1), pl.ds(S, S)], hpi)
                gr, gi = (dh_ref[pl.ds(t, 1), pl.ds(0, S)] + a_r * gr + a_i * gi,
                          dh_ref[pl.ds(t, 1), pl.ds(S, S)] + a_r * gi - a_i * gr)
                g_ref[pl.ds(t, 1), pl.ds(0, S)] = gr
                g_ref[pl.ds(t, 1), pl.ds(S, S)] = gi
                dar = dar + gr * pr + gi * pi
                dai = dai + gi * pr - gr * pi
            return gr, gi, dar, dai

        zero = jnp.zeros((1, S), F32)
        gr, gi, dar, dai = lax.fori_loop(0, tc // 8, step, (cr_ref[...], ci_ref[...], zero, zero))
        cr_ref[...] = gr
        ci_ref[...] = gi
        dar_ref[...] += dar
        dai_ref[...] += dai

    rev = lambda i: (nc - 1 - i, 0)
    before = lambda i: (jnp.maximum((nc - 1 - i) * (tc // 8) - 1, 0), 0)
    return pl.pallas_call(
        body, name="s5_scan_bwd",
        out_shape=[jax.ShapeDtypeStruct((T, 2 * S), F32), jax.ShapeDtypeStruct((1, S), F32),
                   jax.ShapeDtypeStruct((1, S), F32)],
        grid=(nc,),
        in_specs=[pl.BlockSpec((tc, 2 * S), rev), pl.BlockSpec((tc, 2 * S), rev), pl.BlockSpec((8, 2 * S), before),
                  _full_spec((1, S)), _full_spec((1, S))],
        out_specs=[pl.BlockSpec((tc, 2 * S), rev), _full_spec((1, S)), _full_spec((1, S))],
        scratch_shapes=[pltpu.VMEM((1, S), F32), pltpu.VMEM((1, S), F32)],
        compiler_params=_params(("arbitrary",)),
    )(dh, h, h, ar, ai)


def _sum_rows(name, x):
    def body(x_ref, o_ref):
        o_ref[...] = jnp.sum(x_ref[...], axis=0, keepdims=True)

    return pl.pallas_call(body, name=name, out_shape=jax.ShapeDtypeStruct((1, 1), F32),
                          in_specs=[_full_spec(x.shape)], out_specs=_full_spec((1, 1)), grid=(1,))(x)


ANY = pl.BlockSpec(memory_space=pl.ANY)


def _place():
    return lax.axis_index("x"), lax.axis_index("y"), lax.axis_index("c")


def _all_gather(name, shards):
    n = len(shards)

    def body(*refs):
        xs, outs = refs[:n], refs[n:2 * n]
        send_sems, recv_sems, local_sems = refs[2 * n:]
        x, y, c = _place()
        me, sibling = (x, y, c), (x, y, 1 - c)
        chips = [(1 - x, y), (x, 1 - y), (1 - x, 1 - y)]

        def slot(t, px, py, pc):
            return outs[t].at[4 * px + 2 * py + pc]

        def copy(t, k, block, to, src=None):
            return pltpu.make_async_remote_copy(
                src_ref=slot(t, *block) if src is None else src, dst_ref=slot(t, *block),
                send_sem=send_sems.at[7 * t + k], recv_sem=recv_sems.at[7 * t + k], device_id=to, device_id_type=MESH)

        mine = [pltpu.make_async_copy(xs[t], slot(t, *me), local_sems.at[t]) for t in range(n)]
        sent = []
        for t in range(n):
            mine[t].start()
            sent.append(copy(t, 0, me, sibling, src=xs[t]))
            sent += [copy(t, 1 + j, me, (*chip, c), src=xs[t]) for j, chip in enumerate(chips)]
        for cp in sent:
            cp.start()
        for j, chip in enumerate(chips):
            for t in range(n):
                copy(t, 1 + j, (*chip, c), me).wait_recv()
                fwd = copy(t, 4 + j, (*chip, c), sibling)
                fwd.start()
                sent.append(fwd)
        for t in range(n):
            copy(t, 0, sibling, me).wait_recv()
            for j, chip in enumerate(chips):
                copy(t, 4 + j, (*chip, 1 - c), me).wait_recv()
        for cp in sent:
            cp.wait_send()
        for cp in mine:
            cp.wait()

    return pl.pallas_call(
        body, name=name, out_shape=[jax.ShapeDtypeStruct((N_DEV,) + s.shape, s.dtype) for s in shards],
        in_specs=[ANY] * n, out_specs=[ANY] * n,
        scratch_shapes=[pltpu.SemaphoreType.DMA((7 * n,)), pltpu.SemaphoreType.DMA((7 * n,)),
                        pltpu.SemaphoreType.DMA((n,))],
    )(*shards)


HBM = pl.BlockSpec(memory_space=pltpu.HBM)
SEM = pl.BlockSpec(memory_space=pltpu.SEMAPHORE)
DATAFLOW = pltpu.SideEffectType.DATAFLOW_SIDE_EFFECTING


def _hbm(a):
    return pltpu.with_memory_space_constraint(a, pltpu.HBM)


def _split_start(name, srcs, lands, ncopies, plan, after):
    ns, nl = len(srcs), len(lands)

    def body(*refs):
        send_sems, recv_sems = refs[ns + nl + 1], refs[ns + nl + 2]
        token = refs[-1]
        for k, (src, dst, peer, _) in enumerate(plan(refs[:ns], refs[ns:ns + nl])):
            pltpu.make_async_remote_copy(src_ref=src, dst_ref=dst, send_sem=send_sems.at[k], recv_sem=recv_sems.at[k],
                                         device_id=peer, device_id_type=MESH).start()
        token[...] = jnp.zeros_like(token)

    out = pl.pallas_call(
        body, name=name,
        out_shape=(pltpu.SemaphoreType.DMA((ncopies,)), pltpu.SemaphoreType.DMA((ncopies,)),
                   *[pltpu.HBM(a.shape, a.dtype) for a in srcs], *[pltpu.HBM(a.shape, a.dtype) for a in lands],
                   jax.ShapeDtypeStruct((8, LANES), F32)),
        in_specs=[HBM] * (ns + nl) + [ANY],
        out_specs=(SEM, SEM, *[HBM] * (ns + nl), pl.BlockSpec(memory_space=pltpu.VMEM)),
        input_output_aliases={i: 2 + i for i in range(ns + nl)},
        compiler_params=pltpu.CompilerParams(has_side_effects=DATAFLOW),
    )(*[_hbm(a) for a in srcs], *[_hbm(a) for a in lands], after)
    return out[0], out[1], list(out[2:2 + ns]), list(out[2 + ns:2 + ns + nl]), out[-1]


def _split_wait(name, handle, ncopies, plan, after):
    send_sems, recv_sems, srcs, lands, _ = handle
    ns, nl = len(srcs), len(lands)

    def body(*refs):
        s_sems, r_sems = refs[ns + nl], refs[ns + nl + 1]
        for k, (src, dst, peer, mine) in enumerate(plan(refs[:ns], refs[ns:ns + nl])):
            pltpu.make_async_remote_copy(src_ref=src, dst_ref=dst, send_sem=s_sems.at[k], recv_sem=r_sems.at[k],
                                         device_id=peer, device_id_type=MESH).wait_send()
            pltpu.make_async_remote_copy(src_ref=src, dst_ref=mine, send_sem=s_sems.at[k], recv_sem=r_sems.at[k],
                                         device_id=peer, device_id_type=MESH).wait_recv()

    out = pl.pallas_call(
        body, name=name,
        out_shape=(*[pltpu.HBM(a.shape, a.dtype) for a in srcs], *[pltpu.HBM(a.shape, a.dtype) for a in lands]),
        in_specs=[HBM] * (ns + nl) + [SEM, SEM, ANY],
        out_specs=tuple([HBM] * (ns + nl)),
        input_output_aliases={i: i for i in range(ns + nl)},
        compiler_params=pltpu.CompilerParams(has_side_effects=DATAFLOW),
    )(*srcs, *lands, send_sems, recv_sems, after)
    return list(out[:ns]), list(out[ns:])


def _slot(px, py, pc):
    return 4 * px + 2 * py + pc


def _gather_plan(xs, lands):
    x, y, c = _place()
    peers = [(x, y, 1 - c), (1 - x, y, c), (x, 1 - y, c), (1 - x, 1 - y, c)]
    return [(xs[t], lands[t].at[_slot(x, y, c)], peer, lands[t].at[_slot(*peer)])
            for t in range(len(xs)) for peer in peers]


def _gather_start(name, shards, after):
    lands = [lax.empty((N_DEV,) + s.shape, s.dtype) for s in shards]
    return _split_start(name, shards, lands, 4 * len(shards), _gather_plan, after)


def _gather_finish(name, handle, after):
    n = len(handle[2])
    shards, lands = _split_wait(name + "_wait", handle, 4 * n, _gather_plan, after)

    def body(*refs):
        xs, outs = refs[:n], refs[2 * n:3 * n]
        send_sems, recv_sems, local_sems = refs[3 * n:]
        x, y, c = _place()
        chips = [(1 - x, y), (x, 1 - y), (1 - x, 1 - y)]
        mine = [pltpu.make_async_copy(xs[t], outs[t].at[_slot(x, y, c)], local_sems.at[t]) for t in range(n)]
        fwd = [pltpu.make_async_remote_copy(
            src_ref=outs[t].at[_slot(px, py, c)], dst_ref=outs[t].at[_slot(px, py, c)],
            send_sem=send_sems.at[3 * t + j], recv_sem=recv_sems.at[3 * t + j],
            device_id=(x, y, 1 - c), device_id_type=MESH) for t in range(n) for j, (px, py) in enumerate(chips)]
        for cp in mine + fwd:
            cp.start()
        for t in range(n):
            for j, (px, py) in enumerate(chips):
                pltpu.make_async_remote_copy(
                    src_ref=outs[t].at[_slot(px, py, c)], dst_ref=outs[t].at[_slot(px, py, 1 - c)],
                    send_sem=send_sems.at[3 * t + j], recv_sem=recv_sems.at[3 * t + j],
                    device_id=(x, y, 1 - c), device_id_type=MESH).wait_recv()
        for cp in fwd:
            cp.wait_send()
        for cp in mine:
            cp.wait()

    return pl.pallas_call(
        body, name=name + "_forward", out_shape=[jax.ShapeDtypeStruct(a.shape, a.dtype) for a in lands],
        in_specs=[ANY] * (2 * n), out_specs=[ANY] * n,
        input_output_aliases={n + t: t for t in range(n)},
        scratch_shapes=[pltpu.SemaphoreType.DMA((3 * n,)), pltpu.SemaphoreType.DMA((3 * n,)),
                        pltpu.SemaphoreType.DMA((n,))],
    )(*shards, *lands)


def _sibling_plan(ps, gots):
    x, y, c = _place()
    return [(ps[t].at[j, 1 - c], gots[t].at[j], (x, y, 1 - c), gots[t].at[j]) for t in range(len(ps)) for j in range(4)]


def _chips_plan(ss, gots):
    x, y, c = _place()
    chips = [(1 - x, y), (x, 1 - y), (1 - x, 1 - y)]
    return [(ss[t].at[2 * px + py], gots[t].at[k], (px, py, c), gots[t].at[k])
            for t in range(len(ss)) for k, (px, py) in enumerate(chips)]


def _sum_tile(R, C):
    return _tile(R, max(8, min(1024, ROW_BLOCK_BYTES // (C * 4))), 8)


def _pair_sum(name, p4, got, core):
    _, _, R, C = p4.shape
    tr = _sum_tile(R, C)

    def body(core_ref, p_ref, g_ref, o_ref):
        o_ref[...] = p_ref[:, 0] + g_ref[...]

    return pl.pallas_call(
        body, name=name, out_shape=jax.ShapeDtypeStruct((4, R, C), F32),
        grid_spec=pltpu.PrefetchScalarGridSpec(
            num_scalar_prefetch=1, grid=(4, R // tr),
            in_specs=[pl.BlockSpec((1, 1, tr, C), lambda j, i, core: (j, core[0], i, 0)),
                      pl.BlockSpec((1, tr, C), lambda j, i, core: (j, i, 0))],
            out_specs=pl.BlockSpec((1, tr, C), lambda j, i, core: (j, i, 0))),
        compiler_params=_params(("parallel", "parallel")),
    )(core, p4, got)


def _sum_devices(name, g8):
    _, R, C = g8.shape
    tr = _tile(R, SMALL_ROW_ALIGN, 8)

    def body(g_ref, o_ref):
        acc = g_ref[0]
        for d in range(1, N_DEV):
            acc = acc + g_ref[d]
        o_ref[...] = acc

    return pl.pallas_call(
        body, name=name, out_shape=jax.ShapeDtypeStruct((R, C), F32), grid=(R // tr,),
        in_specs=[pl.BlockSpec((N_DEV, tr, C), lambda i: (0, i, 0))], out_specs=pl.BlockSpec((tr, C), lambda i: (i, 0)),
        compiler_params=_params(("parallel",)),
    )(g8)


def _adamw_shard(name, layer, w, m, v, s4, got, chip, bufs):
    _, R, C = w.shape
    tr = _sum_tile(R, C) // 2 if _sum_tile(R, C) % 16 == 0 else _sum_tile(R, C)

    def body(chip_ref, w_ref, m_ref, v_ref, s_ref, g_ref, b0, b1, b2, b3, og, od, om, ov):
        g = ((s_ref[0] + g_ref[0]) + g_ref[1]) + g_ref[2]
        d, nm, nv = _adamw_fn(w_ref[0], g, m_ref[0], v_ref[0])
        og[0], od[0], om[0], ov[0] = g, d, nm, nv

    lay = pl.BlockSpec((1, tr, C), lambda i, chip: (layer, i, 0))
    if bufs is None:
        bufs = [lax.empty(w.shape, F32) for _ in range(4)]
    return pl.pallas_call(
        body, name=name, out_shape=[jax.ShapeDtypeStruct(w.shape, F32)] * 4,
        grid_spec=pltpu.PrefetchScalarGridSpec(
            num_scalar_prefetch=1, grid=(R // tr,),
            in_specs=[lay, lay, lay, pl.BlockSpec((1, tr, C), lambda i, chip: (chip[0], i, 0)),
                      pl.BlockSpec((3, tr, C), lambda i, chip: (0, i, 0)), ANY, ANY, ANY, ANY],
            out_specs=[lay] * 4),
        input_output_aliases={6: 0, 7: 1, 8: 2, 9: 3},
        compiler_params=_params(("parallel",)),
    )(chip, w, m, v, s4, got, *bufs)


def _adamw(name, wt, g, m, v):
    shape = wt.shape
    two = (lambda a: a.reshape(1, -1)) if wt.ndim == 1 else (lambda a: a.reshape(-1, shape[-1]))
    w2, g2, m2, v2 = two(wt), two(g), two(m), two(v)
    tr = _row(w2, None, 0)[2]
    outs = [(w2.shape, F32, 0, tr)] * 3
    d, nm, nv = _rows_fwd(name, _adamw_fn, [_row(a, tr, 0) for a in (w2, g2, m2, v2)], [], outs)
    return d.reshape(shape), nm.reshape(shape), nv.reshape(shape)


def _block_diag(v):
    eye = jnp.eye(SSM_G, dtype=v.dtype)
    return (eye[:, None, :, None] * v[:, :, None, :]).reshape(SSM_G * SSM_P, SSM_G * SSM_N)


def _diag_blocks(m):
    m4 = m.reshape(SSM_G, SSM_P, SSM_G, SSM_N)
    eye = jnp.eye(SSM_G, dtype=m.dtype)
    return (m4 * eye[:, None, :, None]).sum(axis=2)


def _fold(a, dil):
    if dil == 1:
        return a
    return a.reshape((T // dil, dil) + a.shape[1:]).swapaxes(0, 1).reshape(a.shape)


def _unfold(a, dil):
    if dil == 1:
        return a
    return a.reshape((dil, T // dil) + a.shape[1:]).swapaxes(0, 1).reshape(a.shape)


DILS = (1, 4, 16)


def _fold3(a):
    return jnp.stack([_fold(a, d) for d in DILS]).transpose(0, 2, 1, 3)


def _unfold3(a):
    return jnp.stack([_unfold(a[p].swapaxes(0, 1), d).swapaxes(0, 1) for p, d in enumerate(DILS)])


def _refold3(a):
    return jnp.stack([_fold(a[p].swapaxes(0, 1), d).swapaxes(0, 1) for p, d in enumerate(DILS)])


def _rope_tables():
    half = ROPE // 2
    inv_freq = ROPE_THETA ** (-jnp.arange(half, dtype=F32) / half)
    ang = jnp.arange(T).astype(F32)[:, None] * inv_freq[None, :]
    return jnp.cos(ang), jnp.sin(ang)


def _from_blocks(a8):
    return a8.transpose(1, 0, 2).reshape(a8.shape[1], -1)


def _to_blocks(a):
    return a.reshape(a.shape[0], N_DEV, -1).transpose(1, 0, 2)


HALF = ROPE // 2


def _rope_rows(q1, q2, k1, k2, cos, sin):
    return [_row(q1, 256, 1), _row(q2, 256, 1), _row(k1, 256, 0), _row(k2, 256, 0)], [_row(cos, 256, 0), _row(sin, 256, 0)]


def _behind(a, tok):
    return a if tok is None else a + tok


def _mixer_fwd(x, w, sp, cos, sin):
    s = {}
    s['x'] = x
    h = _rms_fwd("rms_mix", x, sp['g_mix'])
    proj = _from_blocks(_mm("mm_in", h, w['w_in'], 'nn', bb='c', ob='c'))
    offs = np.cumsum((0,) + IN_SPLITS)
    c_q, c_kv, k_rope, u, qd, kd, vd = [proj[:, offs[i]:offs[i + 1]] for i in range(7)]
    s.update(h=h, c_q=c_q, c_kv=c_kv, u=u)

    cqn = _rms_fwd("rms_q", c_q, sp['g_q'])
    ckvn = _rms_fwd("rms_kv", c_kv, sp['g_kv'])
    q8 = _mm("mm_uq", cqn, w['w_uq'], 'nn', bb='c', ob='c')
    kv8 = _mm("mm_ukv", ckvn, w['w_ukv'], 'nn', bb='c', ob='c')
    rows, aux = _rope_rows(q8[:, :, NOPE:NOPE + HALF], q8[:, :, NOPE + HALF:], k_rope[:, :HALF], k_rope[:, HALF:], cos, sin)
    oq1, oq2, ok1, ok2 = _rows_fwd("rope", _rope_fn, rows + aux, [],
                                   [((H_MLA, T, HALF), F32, 1, 256)] * 2 + [((T, HALF), F32, 0, 256)] * 2)
    k_pe = jnp.concatenate([ok1, ok2], axis=-1)
    qh = jnp.concatenate([q8[:, :, :NOPE], oq1, oq2], axis=-1).astype(BF16)
    kh = jnp.concatenate([kv8[:, :, :NOPE], jnp.broadcast_to(k_pe[None], (H_MLA, T, ROPE))], axis=-1).astype(BF16)
    vh = kv8[:, :, NOPE:].astype(BF16)
    y_mla, lse_mla = _mla_fwd(qh, kh, vh)
    s.update(cqn=cqn, ckvn=ckvn, qh=qh, kh=kh, vh=vh, lse_mla=lse_mla)

    a3 = lambda n: sp[n].reshape(SSM_G, 1, SSM_N)
    b2 = lambda n: sp[n].transpose(0, 2, 1).reshape(SSM_G * SSM_P, SSM_N)
    disc_rows = [_row(a3('a_re'), 1, 0), _row(a3('a_im'), 1, 0), _row(sp['log_dt'].reshape(SSM_G, 1, 1), 1, 0),
                 _row(b2('b_re'), SSM_P, 0), _row(b2('b_im'), SSM_P, 0)]
    abr, abi, bbr, bbi = _rows_fwd(
        "s5_disc", _s5_disc_fn, disc_rows, [],
        [((SSM_G, 1, SSM_N), F32, 0, 1), ((SSM_G, 1, SSM_N), F32, 0, 1),
         ((SSM_G * SSM_P, SSM_N), F32, 0, SSM_P), ((SSM_G * SSM_P, SSM_N), F32, 0, SSM_P)])
    ar, ai = abr.reshape(1, SSM_S), abi.reshape(1, SSM_S)
    b_mat = jnp.concatenate([_block_diag(bbr.reshape(SSM_G, SSM_P, SSM_N)),
                             _block_diag(bbi.reshape(SSM_G, SSM_P, SSM_N))], axis=1)
    c_mat = jnp.concatenate([_block_diag(sp['c_re']), _block_diag(-sp['c_im'])], axis=1)
    bu = _mm("mm_s5_b", u, b_mat, 'nn', prec=HI)
    hst = _scan_fwd(bu, ar, ai)
    ymm = _mm("mm_s5_c", hst, c_mat, 'nt', prec=HI)
    d_row = sp['d_skip'].reshape(1, SSM_W)
    (yg,) = _rows_fwd("s5_act", _s5_act_fn, [_row(ymm), _row(u)], [d_row], [((T, SSM_W), BF16, -2, _row(u)[2])])
    z = _mm("mm_glu", yg, w['w_glu'], 'nn', bb='c')
    glu_rows = [_row(z[:, :SSM_W]), _row(z[:, SSM_W:])]
    glu_b = [sp['b_glu'][:SSM_W].reshape(1, -1), sp['b_glu'][SSM_W:].reshape(1, -1)]
    (y_ssm,) = _rows_fwd("s5_glu", _glu_fn, glu_rows, glu_b, [((T, SSM_W), F32, -2, glu_rows[0][2])])
    s.update(disc_rows=disc_rows, ar=ar, ai=ai, b_mat=b_mat, c_mat=c_mat, hst=hst, ymm=ymm, d_row=d_row, yg=yg,
             glu_rows=glu_rows, glu_b=glu_b)

    f3 = lambda a: _fold3(a.reshape(T, DIL_H, DIL_D)).astype(BF16)
    qf, kf, vf = f3(qd), f3(kd), f3(vd)
    o_f, lse_f = _band_fwd(qf, kf, vf)
    o_n, lse_n = _unfold3(o_f), _unfold3(lse_f)
    mix_rows = [_row(o_n[p], 256, 1) for p in range(3)] + [_row(lse_n[p], 256, 1) for p in range(3)]
    (y_dh,) = _rows_fwd("dil_mix", _dil_mix_fn, mix_rows, [], [((DIL_H, T, DIL_D), F32, 1, 256)])
    y_dil = y_dh.transpose(1, 0, 2).reshape(T, DIL_W)
    s.update(qf=qf, kf=kf, vf=vf, o_f=o_f, lse_f=lse_f, mix_rows=mix_rows)

    gm, gs, gd = sp['g_out_mla'].reshape(1, -1), sp['g_out_ssm'].reshape(1, -1), sp['g_out_dil'].reshape(1, -1)
    on_rows = [_row(y_mla), _row(y_ssm), _row(y_dil)]
    (ycat,) = _rows_fwd("out_norm", _outnorm_fn, on_rows, [gm, gs, gd], [((T, D), BF16, -2, on_rows[0][2])])
    x1_ = _mm("mm_o", ycat, w['w_o'], 'nn', bb='r', res=x)
    s.update(on_rows=on_rows, on_g=[gm, gs, gd], ycat=ycat, x1=x1_)
    return x1_, s


def _ffn_fwd(x1_, w, sp, s):
    h2 = _rms_fwd("rms_ffn", x1_, sp['g_ffn'])
    ga = _mm("mm_gate", h2, w['w_gate'], 'nn', bb='c', ob='c')
    gb = _mm("mm_up", h2, w['w_up'], 'nn', bb='c', ob='c')
    ffn_rows = [_row(ga, None, 1), _row(gb, None, 1)]
    (zf,) = _rows_fwd("swiglu", _swiglu_fn, ffn_rows, [], [(ga.shape, BF16, 1, ffn_rows[0][2])])
    x2_ = _mm("mm_down", zf, w['w_down'], 'nn', ab='c', bb='r', res=x1_)
    s.update(h2=h2, ffn_rows=ffn_rows, zf=zf)
    return x2_


def _b16(a):
    return a.astype(BF16)


def _ffn_bwd(dx2, s, w, sp, tok=None):
    gw, gs_ = {}, {}
    b16 = _b16
    dx2b = b16(_behind(dx2, tok))
    dzf = _mm("mm_down_dx", dx2b, w['w_down'], 'nt', bb='r', ob='c')
    gw['w_down'] = _mm("mm_down_dw", s['zf'], dx2b, 'tn', ab='c', ob='r')
    tr = s['ffn_rows'][0][2]
    dga, dgb = _rows_vjp("swiglu_bwd", _swiglu_fn, s['ffn_rows'], [], [], [_row(dzf, tr, 1)], grad_dtypes=[BF16, BF16])
    gw['w_gate'] = _mm("mm_gate_dw", s['h2'], dga, 'tn', bb='c', ob='c')
    gw['w_up'] = _mm("mm_up_dw", s['h2'], dgb, 'tn', bb='c', ob='c')
    dh2 = _mm("mm_up_dx", dgb, w['w_up'], 'nt', ab='c', bb='c',
              res=_mm("mm_gate_dx", dga, w['w_gate'], 'nt', ab='c', bb='c'))
    dx1, gs_['g_ffn'] = _rms_bwd("rms_ffn_bwd", s['x1'], sp['g_ffn'], dh2, dx2)
    return dx1, gw, gs_


def _mixer_bwd_out(dx1, s, w, sp, tok=None):
    gw, gs_ = {}, {}
    b16 = _b16
    dx1b = b16(_behind(dx1, tok))
    dycat = _mm("mm_o_dx", dx1b, w['w_o'], 'nt', bb='r')
    gw['w_o'] = _mm("mm_o_dw", s['ycat'], dx1b, 'tn', ob='r')
    dy_mla, dy_ssm, dy_dil, gs_['g_out_mla'], gs_['g_out_ssm'], gs_['g_out_dil'] = _rows_vjp(
        "out_norm_bwd", _outnorm_fn, s['on_rows'], s['on_g'], [], [_row(dycat)])

    do_h = dy_dil.reshape(T, DIL_H, DIL_D).transpose(1, 0, 2)
    dmix = _rows_vjp("dil_mix_bwd", _dil_mix_fn, s['mix_rows'], [], [], [_row(do_h, 256, 1)])
    do_n, dlse_n = jnp.stack(dmix[:3]), jnp.stack(dmix[3:])
    dqf, dkf, dvf = _band_bwd(s['qf'], s['kf'], s['vf'], s['o_f'], s['lse_f'], _refold3(do_n), _refold3(dlse_n))
    back = lambda a: _unfold3(a).sum(axis=0).transpose(1, 0, 2).reshape(T, DIL_W)
    dqd, dkd, dvd = back(dqf), back(dkf), back(dvf)

    dz1, dz2, db1, db2 = _rows_vjp("s5_glu_bwd", _glu_fn, s['glu_rows'], s['glu_b'], [], [_row(dy_ssm)])
    gs_['b_glu'] = jnp.concatenate([db1, db2], axis=1)
    dzb = b16(jnp.concatenate([dz1, dz2], axis=1))
    dyg = _mm("mm_glu_dx", dzb, w['w_glu'], 'nt', bb='c')
    gw['w_glu'] = _mm("mm_glu_dw", s['yg'], dzb, 'tn', ob='c')
    dymm, du_act, dd = _rows_vjp("s5_act_bwd", _s5_act_fn, [_row(s['ymm']), _row(s['u'])], [s['d_row']], [], [_row(dyg)])
    gs_['d_skip'] = dd
    dhst = _mm("mm_s5_c_dx", dymm, s['c_mat'], 'nn', prec=HI)
    dc_mat = _mm("mm_s5_c_dw", dymm, s['hst'], 'tn', prec=HI)
    g, dar, dai = _scan_bwd(dhst, s['hst'], s['ar'], s['ai'])
    du = _mm("mm_s5_b_dx", g, s['b_mat'], 'nt', prec=HI, res=du_act)
    db_mat = _mm("mm_s5_b_dw", s['u'], g, 'tn', prec=HI)
    gs_['c_re'] = _diag_blocks(dc_mat[:, :SSM_S])
    gs_['c_im'] = -_diag_blocks(dc_mat[:, SSM_S:])
    dbbr = _diag_blocks(db_mat[:, :SSM_S]).reshape(SSM_G * SSM_P, SSM_N)
    dbbi = _diag_blocks(db_mat[:, SSM_S:]).reshape(SSM_G * SSM_P, SSM_N)
    disc_cts = [_row(dar.reshape(SSM_G, 1, SSM_N), 1, 0), _row(dai.reshape(SSM_G, 1, SSM_N), 1, 0),
                _row(dbbr, SSM_P, 0), _row(dbbi, SSM_P, 0)]
    da_re, da_im, dldt, db_r, db_i = _rows_vjp("s5_disc_bwd", _s5_disc_fn, s['disc_rows'], [], [], disc_cts)
    gs_['a_re'], gs_['a_im'], gs_['log_dt'] = da_re, da_im, dldt
    unb = lambda a: a.reshape(SSM_G, SSM_P, SSM_N).transpose(0, 2, 1)
    gs_['b_re'], gs_['b_im'] = unb(db_r), unb(db_i)
    return (dy_mla, du, dqd, dkd, dvd), gw, gs_


def _mixer_bwd_in(cts, dx1, s, w, sp, cos, sin, tok=None):
    gw, gs_ = {}, {}
    b16 = _b16
    dy_mla, du, dqd, dkd, dvd = cts
    dqh, dkh, dvh, dkpe = _mla_bwd(s['qh'], s['kh'], s['vh'], dy_mla, _behind(s['lse_mla'], tok))
    zq, zk = jnp.zeros((H_MLA, T, HALF), F32), jnp.zeros((T, HALF), F32)
    rows, aux = _rope_rows(zq, zq, zk, zk, cos, sin)
    cts, _ = _rope_rows(dqh[:, :, NOPE:NOPE + HALF], dqh[:, :, NOPE + HALF:], dkpe[:, :HALF], dkpe[:, HALF:], cos, sin)
    dq1, dq2, dk1, dk2 = _rows_vjp("rope_bwd", _rope_fn, rows, [], aux, cts)
    dq8 = b16(jnp.concatenate([dqh[:, :, :NOPE], dq1, dq2], axis=-1))
    dkv8 = b16(jnp.concatenate([dkh[:, :, :NOPE], dvh], axis=-1))
    dk_rope = jnp.concatenate([dk1, dk2], axis=-1)
    dcqn = _mm("mm_uq_dx", dq8, w['w_uq'], 'nt', ab='c', bb='c')
    gw['w_uq'] = _mm("mm_uq_dw", s['cqn'], dq8, 'tn', bb='c', ob='c')
    dckvn = _mm("mm_ukv_dx", dkv8, w['w_ukv'], 'nt', ab='c', bb='c')
    gw['w_ukv'] = _mm("mm_ukv_dw", s['ckvn'], dkv8, 'tn', bb='c', ob='c')
    dc_q, gs_['g_q'] = _rms_bwd("rms_q_bwd", s['c_q'], sp['g_q'], dcqn)
    dc_kv, gs_['g_kv'] = _rms_bwd("rms_kv_bwd", s['c_kv'], sp['g_kv'], dckvn)

    dproj = _to_blocks(b16(jnp.concatenate([dc_q, dc_kv, dk_rope, du, dqd, dkd, dvd], axis=1)))
    dh = _mm("mm_in_dx", dproj, w['w_in'], 'nt', ab='c', bb='c')
    gw['w_in'] = _mm("mm_in_dw", s['h'], dproj, 'tn', bb='c', ob='c')
    dx, gs_['g_mix'] = _rms_bwd("rms_mix_bwd", s['x'], sp['g_mix'], dh, dx1)
    return dx, gw, gs_


def kernel(x, g_mix, w_in, g_q, w_uq, g_kv, w_ukv, a_re, a_im, b_re, b_im, c_re, c_im, d_skip, log_dt, w_glu, b_glu, g_out_mla, g_out_ssm, g_out_dil, w_o, g_ffn, w_gate, w_up, w_down, g_final, loss_target, m_g_mix, m_w_in, m_g_q, m_w_uq, m_g_kv, m_w_ukv, m_a_re, m_a_im, m_b_re, m_b_im, m_c_re, m_c_im, m_d_skip, m_log_dt, m_w_glu, m_b_glu, m_g_out_mla, m_g_out_ssm, m_g_out_dil, m_w_o, m_g_ffn, m_w_gate, m_w_up, m_w_down, m_g_final, v_g_mix, v_w_in, v_g_q, v_w_uq, v_g_kv, v_w_ukv, v_a_re, v_a_im, v_b_re, v_b_im, v_c_re, v_c_im, v_d_skip, v_log_dt, v_w_glu, v_b_glu, v_g_out_mla, v_g_out_ssm, v_g_out_dil, v_w_o, v_g_ffn, v_w_gate, v_w_up, v_w_down, v_g_final):
    W = dict(zip(PARAMS, (g_mix, w_in, g_q, w_uq, g_kv, w_ukv, a_re, a_im, b_re, b_im, c_re, c_im, d_skip, log_dt,
                          w_glu, b_glu, g_out_mla, g_out_ssm, g_out_dil, w_o, g_ffn, w_gate, w_up, w_down, g_final)))
    M = dict(zip(PARAMS, (m_g_mix, m_w_in, m_g_q, m_w_uq, m_g_kv, m_w_ukv, m_a_re, m_a_im, m_b_re, m_b_im, m_c_re,
                          m_c_im, m_d_skip, m_log_dt, m_w_glu, m_b_glu, m_g_out_mla, m_g_out_ssm, m_g_out_dil, m_w_o,
                          m_g_ffn, m_w_gate, m_w_up, m_w_down, m_g_final)))
    V = dict(zip(PARAMS, (v_g_mix, v_w_in, v_g_q, v_w_uq, v_g_kv, v_w_ukv, v_a_re, v_a_im, v_b_re, v_b_im, v_c_re,
                          v_c_im, v_d_skip, v_log_dt, v_w_glu, v_b_glu, v_g_out_mla, v_g_out_ssm, v_g_out_dil, v_w_o,
                          v_g_ffn, v_w_gate, v_w_up, v_w_down, v_g_final)))
    cx, cy, cc = _place()
    core = cc.astype(jnp.int32).reshape(1)
    chip = (2 * cx + cy).astype(jnp.int32).reshape(1)
    cos, sin = _rope_tables()
    small = [{n: W[n][l] for n in SMALL} for l in range(DEPTH)]
    for sp in small:
        for n in ('g_mix', 'g_q', 'g_kv', 'g_ffn'):
            sp[n] = sp[n].reshape(1, -1)

    def tok_of(tokens):
        return sum(t[0, 0] for t in tokens) if tokens else None

    def gather_start(l, group, names, after):
        return _gather_start(f"gather_{group}_start_{l}", [W[n][l].astype(BF16) for n in names], after)

    h_mix = gather_start(0, "mix", MIXER_W, jnp.zeros((8, LANES), F32))
    h_ffn = gather_start(0, "ffn", FFN_W, h_mix[4])
    xa = x[0]
    saved, full = [], []
    for l in range(DEPTH):
        wm = dict(zip(MIXER_W, _gather_finish(f"gather_mix_{l}", h_mix, xa)))
        sp = dict(small[l])
        h_ffn_now = h_ffn
        if l + 1 < DEPTH:
            h_mix = gather_start(l + 1, "mix", MIXER_W, wm['w_in'])
            h_ffn = gather_start(l + 1, "ffn", FFN_W, h_mix[4])
            sp['g_mix'] = _behind(sp['g_mix'], tok_of([h_mix[4], h_ffn[4]]))
        x1, s = _mixer_fwd(xa, wm, sp, cos, sin)
        wf = dict(zip(FFN_W, _gather_finish(f"gather_ffn_{l}", h_ffn_now, x1)))
        xa = _ffn_fwd(x1, wf, small[l], s)
        saved.append(s)
        full.append({**wm, **wf})
    gf = g_final.reshape(1, D)
    ones = jnp.ones((T, 1), F32)
    dxa, dgf, loss_rows = _rows_vjp("loss", _loss_fn, [_row(xa)], [gf], [_row(loss_target[0])], [_row(ones)],
                                    primal=True)
    loss = lax.psum(_sum_rows("loss_sum", loss_rows)[0, 0], ("x", "y", "c"))

    bufs = {n: None for n in BIG}
    pending = []

    def advance(dep):
        tokens = []
        for g in pending:
            names, tag = g['names'], g['tag']
            if g['stage'] == 0:
                p4 = [a.reshape((4, 2) + a.shape[1:]) for a in g['gw']]
                gots = [lax.empty((4,) + a.shape[1:], F32) for a in g['gw']]
                g['h'] = _split_start("rs_sibling_start_" + tag, p4, gots, 4 * len(p4), _sibling_plan, dep)
                tokens.append(g['h'][4])
            elif g['stage'] == 1:
                p4, gots = _split_wait("rs_sibling_wait_" + tag, g['h'], 4 * len(names), _sibling_plan, dep)
                s4 = [_pair_sum("rs_pair_sum_" + n, p, q, core) for n, p, q in zip(names, p4, gots)]
                gots = [lax.empty((3,) + a.shape[1:], F32) for a in s4]
                g['h'] = _split_start("rs_chips_start_" + tag, s4, gots, 3 * len(s4), _chips_plan, dep)
                tokens.append(g['h'][4])
            elif g['stage'] == 3:
                s4, gots = _split_wait("rs_chips_wait_" + tag, g['h'], 3 * len(names), _chips_plan, dep)
                for n, s4n, got in zip(names, s4, gots):
                    bufs[n] = _adamw_shard("adamw_" + n, g['layer'], W[n], M[n], V[n], s4n, got, chip, bufs[n])
            g['stage'] += 1
        pending[:] = [g for g in pending if g['stage'] < 4]
        return tokens

    def group(names, l, gw, kind):
        return dict(names=names, layer=l, gw=[gw[n] for n in names], stage=0, tag=f"{kind}_{l}")

    g_small = [None] * DEPTH
    tokens = []
    for l in reversed(range(DEPTH)):
        dx1, gw_f, gs_f = _ffn_bwd(dxa, saved[l], full[l], small[l], tok_of(tokens))
        pending.append(group(FFN_W, l, gw_f, "ffn"))
        tokens = advance(dx1)
        cts, gw_o, gs_o = _mixer_bwd_out(dx1, saved[l], full[l], small[l], tok_of(tokens))
        tokens = advance(cts[0])
        dxa, gw_i, gs_i = _mixer_bwd_in(cts, dx1, saved[l], full[l], small[l], cos, sin, tok_of(tokens))
        pending.append(group(MIXER_W, l, {**gw_o, **gw_i}, "mix"))
        tokens = advance(dxa)
        g_small[l] = {**gs_f, **gs_o, **gs_i}

    flat = [g_small[l][n].reshape(-1) for l in range(DEPTH) for n in SMALL] + [dgf.reshape(-1)]
    n_small = sum(int(f.shape[0]) for f in flat)
    rows = -(-n_small // (PACK_C * SMALL_ROW_ALIGN)) * SMALL_ROW_ALIGN
    flat = jnp.concatenate(flat + [jnp.zeros((rows * PACK_C - n_small,), F32)]).reshape(rows, PACK_C)
    (gathered,) = _all_gather("gather_small", [flat])
    tot = _sum_devices("small_sum", gathered).reshape(-1)
    grads, off = {}, 0
    per_layer = {n: [] for n in SMALL}
    for l in range(DEPTH):
        for n, shp in SMALL.items():
            k = int(np.prod(shp))
            per_layer[n].append(tot[off:off + k].reshape(shp))
            off += k
    for n in SMALL:
        grads[n] = jnp.stack(per_layer[n])
    grads['g_final'] = tot[off:off + D]

    delta, new_m, new_v = {}, {}, {}
    advance(tot)
    for n in PARAMS:
        if n not in BIG:
            delta[n], new_m[n], new_v[n] = _adamw("adamw_" + n, W[n], grads[n], M[n], V[n])
    while pending:
        advance(delta['g_final'])
    for n in BIG:
        grads[n], delta[n], new_m[n], new_v[n] = bufs[n]
    return (loss, dxa[None], *[grads[n] for n in PARAMS], *[delta[n] for n in PARAMS],
            *[new_m[n] for n in PARAMS], *[new_v[n] for n in PARAMS])
```

```python
import jax
import jax.numpy as jnp
import numpy as np
from jax import lax
from jax.experimental import pallas as pl
from jax.experimental.pallas import tpu as pltpu

F32 = jnp.float32
BF16 = jnp.bfloat16

T = 2048
D = 2048
DEPTH = 4
N_DEV = 8
H_MLA, NOPE, ROPE, VDIM = 8, 128, 64, 128
QK = NOPE + ROPE
Q_LORA, KV_LORA = 512, 256
SSM_W, SSM_G, SSM_P, SSM_N = 512, 32, 16, 64
SSM_S = SSM_G * SSM_N
DIL_W, DIL_H, DIL_D = 512, 8, 64
BLK = 128
IN_SPLITS = (Q_LORA, KV_LORA, ROPE, SSM_W, DIL_W, DIL_W, DIL_W)
IN_W = sum(IN_SPLITS)
D_FF = 5632
EPS = 1e-6
ROPE_THETA = 10000.0
MLA_SCALE = QK ** -0.5
DIL_SCALE = DIL_D ** -0.5

ADAM_LR, ADAM_B1, ADAM_B2, ADAM_EPS, ADAM_WD, ADAM_STEP = 0.001, 0.9, 0.999, 1e-08, 0.01, 10

VMEM_LIMIT_V7X = 52 * 1024 * 1024
LANES = 128
PACK_C = 1024
ROW_BLOCK_BYTES = 2 * 1024 * 1024
MM_TM, MM_TN, MM_TK = 512, 512, 2048

NT = (((1,), (1,)), ((), ()))
TN = (((0,), (0,)), ((), ()))
HI = lax.Precision.HIGHEST
MESH = pl.DeviceIdType.MESH

PARAMS = ['g_mix', 'w_in', 'g_q', 'w_uq', 'g_kv', 'w_ukv', 'a_re', 'a_im', 'b_re', 'b_im', 'c_re', 'c_im',
          'd_skip', 'log_dt', 'w_glu', 'b_glu', 'g_out_mla', 'g_out_ssm', 'g_out_dil', 'w_o', 'g_ffn',
          'w_gate', 'w_up', 'w_down', 'g_final']
BIG = {'w_in': 'c', 'w_uq': 'c', 'w_ukv': 'c', 'w_glu': 'c', 'w_o': 'r', 'w_gate': 'c', 'w_up': 'c', 'w_down': 'r'}
MIXER_W = ['w_in', 'w_uq', 'w_ukv', 'w_glu', 'w_o']
FFN_W = ['w_gate', 'w_up', 'w_down']
SMALL = {'g_mix': (D,), 'g_q': (Q_LORA,), 'g_kv': (KV_LORA,), 'a_re': (SSM_G, SSM_N), 'a_im': (SSM_G, SSM_N),
         'b_re': (SSM_G, SSM_N, SSM_P), 'b_im': (SSM_G, SSM_N, SSM_P), 'c_re': (SSM_G, SSM_P, SSM_N),
         'c_im': (SSM_G, SSM_P, SSM_N), 'd_skip': (SSM_G, SSM_P), 'log_dt': (SSM_G,), 'b_glu': (2 * SSM_W,),
         'g_out_mla': (H_MLA * VDIM,), 'g_out_ssm': (SSM_W,), 'g_out_dil': (DIL_W,), 'g_ffn': (D,)}
SMALL_ROW_ALIGN = 64


def _tile(dim, target, align=LANES):
    best = None
    for t in range(align, min(dim, target) + 1, align):
        if dim % t == 0:
            best = t
    return best if best is not None else dim


def _params(sem=None):
    return pltpu.CompilerParams(dimension_semantics=sem, vmem_limit_bytes=VMEM_LIMIT_V7X)


def _dot(a, b, dims=None, prec=None):
    if dims is None:
        return jnp.dot(a, b, preferred_element_type=F32, precision=prec)
    return lax.dot_general(a, b, dims, preferred_element_type=F32, precision=prec)


def _mm_spec(shape, blk, t_r, t_c, rc):
    if blk is None:
        return pl.BlockSpec((t_r, t_c), rc)
    _, R, C = shape
    if blk == 'r':
        per = R // t_r
        return pl.BlockSpec((1, t_r, t_c), lambda i, j, k: (rc(i, j, k)[0] // per, rc(i, j, k)[0] % per, rc(i, j, k)[1]))
    per = C // t_c
    return pl.BlockSpec((1, t_r, t_c), lambda i, j, k: (rc(i, j, k)[1] // per, rc(i, j, k)[0], rc(i, j, k)[1] % per))


def _logical(shape, blk):
    if blk is None:
        return tuple(shape)
    G, R, C = shape
    return (G * R, C) if blk == 'r' else (R, G * C)


def _mm(name, a, b, mode, ab=None, bb=None, ob=None, res=None, prec=None):
    la, lb = _logical(a.shape, ab), _logical(b.shape, bb)
    am, ak = (0, 1) if mode != 'tn' else (1, 0)
    bk, bn = (0, 1) if mode != 'nt' else (1, 0)
    M, K, N = la[am], la[ak], lb[bn]
    assert lb[bk] == K, (name, a.shape, b.shape, mode)
    if ob is None:
        out_shape = (M, N)
    elif ob == 'r':
        G = N_DEV
        out_shape = (G, M // G, N)
    else:
        G = N_DEV
        out_shape = (G, M, N // G)
    em = min(a.shape[-2:][am], out_shape[-2])
    en = min(b.shape[-2:][bn], out_shape[-1])
    ek = min(a.shape[-2:][ak], b.shape[-2:][bk])
    tn = _tile(en, MM_TN)
    tm = _tile(em, MM_TM)
    tk = _tile(ek, MM_TK)
    nk = K // tk
    dims = {'nn': None, 'nt': NT, 'tn': TN}[mode]

    def val(ref):
        return ref[...] if len(ref.shape) == 2 else ref[0]

    def put(o_ref, r):
        if len(o_ref.shape) == 2:
            o_ref[...] = r
        else:
            o_ref[0] = r

    def body(*refs):
        if res is None:
            a_ref, b_ref, o_ref = refs[:3]
            r_ref = None
        else:
            a_ref, b_ref, r_ref, o_ref = refs[:4]
        part = _dot(val(a_ref), val(b_ref), dims, prec)
        if nk == 1:
            put(o_ref, part if r_ref is None else part + val(r_ref))
            return
        acc_ref = refs[-1]
        k = pl.program_id(2)

        @pl.when(k == 0)
        def _():
            acc_ref[...] = part

        @pl.when((k > 0) & (k < nk - 1))
        def _():
            acc_ref[...] += part

        @pl.when(k == nk - 1)
        def _():
            r = acc_ref[...] + part
            put(o_ref, r if r_ref is None else r + val(r_ref))

    if mode == 'tn':
        a_spec = _mm_spec(a.shape, ab, tk, tm, lambda i, j, k: (k, i))
    else:
        a_spec = _mm_spec(a.shape, ab, tm, tk, lambda i, j, k: (i, k))
    if mode == 'nt':
        b_spec = _mm_spec(b.shape, bb, tn, tk, lambda i, j, k: (j, k))
    else:
        b_spec = _mm_spec(b.shape, bb, tk, tn, lambda i, j, k: (k, j))
    o_spec = _mm_spec(out_shape, ob, tm, tn, lambda i, j, k: (i, j))
    in_specs = [a_spec, b_spec] + ([o_spec] if res is not None else [])
    args = (a, b) + ((res,) if res is not None else ())
    return pl.pallas_call(
        body, name=name, out_shape=jax.ShapeDtypeStruct(out_shape, F32),
        grid=(M // tm, N // tn, nk), in_specs=in_specs, out_specs=o_spec,
        scratch_shapes=[pltpu.VMEM((tm, tn), F32)] if nk > 1 else [],
        compiler_params=_params(("parallel", "parallel", "arbitrary")),
    )(*args)


def _row(a, tr=None, axis=-2):
    axis = axis % a.ndim
    n = a.shape[axis]
    if tr is None:
        row_bytes = a.size // n * 4
        tr = _tile(n, max(8, min(256, ROW_BLOCK_BYTES // row_bytes)), 8)
    return (a, axis, tr)


def _row_spec(shape, axis, tr):
    nd = len(shape)
    blk = tuple(tr if d == axis else s for d, s in enumerate(shape))
    return pl.BlockSpec(blk, lambda i: tuple(i if d == axis else 0 for d in range(nd)))


def _full_spec(shape):
    nd = len(shape)
    return pl.BlockSpec(tuple(shape), lambda i: (0,) * nd)


def _steps(entries):
    ns = {a.shape[ax] // tr for a, ax, tr in entries}
    assert len(ns) == 1, [(a.shape, ax, tr) for a, ax, tr in entries]
    return ns.pop()


def _as_tuple(r):
    return tuple(r) if isinstance(r, (tuple, list)) else (r,)


def _rows_fwd(name, fn, rows, bcast, outs):
    steps = _steps(rows)
    nr, nb = len(rows), len(bcast)

    def body(*refs):
        vals = [r[...] for r in refs[:nr + nb]]
        res = _as_tuple(fn(*vals))
        for o_ref, r in zip(refs[nr + nb:], res):
            o_ref[...] = r.astype(o_ref.dtype)

    in_specs = [_row_spec(a.shape, ax, tr) for a, ax, tr in rows] + [_full_spec(b.shape) for b in bcast]
    out_specs = [_row_spec(s, ax % len(s), tr) for s, _, ax, tr in outs]
    res = pl.pallas_call(
        body, name=name, out_shape=[jax.ShapeDtypeStruct(s, dt) for s, dt, _, _ in outs],
        grid=(steps,), in_specs=in_specs, out_specs=out_specs,
        compiler_params=_params(("parallel",)),
    )(*[a for a, _, _ in rows], *bcast)
    return res


def _rows_vjp(name, fn, drows, dbc, arows, cts, primal=False, grad_dtypes=None):
    entries = list(drows) + list(arows) + list(cts)
    steps = _steps(entries)
    ndr, ndb, nar, nct = len(drows), len(dbc), len(arows), len(cts)
    gdt = list(grad_dtypes) if grad_dtypes is not None else [F32] * ndr

    def body(*refs):
        p = 0
        dr = [r[...] for r in refs[p:p + ndr]]; p += ndr
        db = [r[...] for r in refs[p:p + ndb]]; p += ndb
        ar = [r[...] for r in refs[p:p + nar]]; p += nar
        ct = [r[...] for r in refs[p:p + nct]]; p += nct
        g_rows = refs[p:p + ndr]; p += ndr
        g_bc = refs[p:p + ndb]; p += ndb
        prim_refs = refs[p:]

        def f(*d):
            return _as_tuple(fn(*d, *ar))

        outs, pullback = jax.vjp(f, *dr, *db)
        grads = pullback(tuple(c.astype(o.dtype) for c, o in zip(ct, outs)))
        for k in range(ndr):
            g_rows[k][...] = grads[k].astype(g_rows[k].dtype)
        if ndb:
            @pl.when(pl.program_id(0) == 0)
            def _():
                for r in g_bc:
                    r[...] = jnp.zeros_like(r)
            for k in range(ndb):
                g_bc[k][...] += grads[ndr + k]
        for r, o in zip(prim_refs, outs):
            r[...] = o.astype(r.dtype)

    in_specs = ([_row_spec(a.shape, ax, tr) for a, ax, tr in drows] + [_full_spec(b.shape) for b in dbc]
                + [_row_spec(a.shape, ax, tr) for a, ax, tr in arows]
                + [_row_spec(a.shape, ax, tr) for a, ax, tr in cts])
    out_shape = ([jax.ShapeDtypeStruct(a.shape, dt) for (a, _, _), dt in zip(drows, gdt)]
                 + [jax.ShapeDtypeStruct(b.shape, F32) for b in dbc])
    out_specs = ([_row_spec(a.shape, ax, tr) for a, ax, tr in drows] + [_full_spec(b.shape) for b in dbc])
    if primal:
        out_shape += [jax.ShapeDtypeStruct(a.shape, F32) for a, _, _ in cts]
        out_specs += [_row_spec(a.shape, ax, tr) for a, ax, tr in cts]
    return pl.pallas_call(
        body, name=name, out_shape=out_shape, grid=(steps,), in_specs=in_specs, out_specs=out_specs,
        compiler_params=_params(("arbitrary",)),
    )(*[a for a, _, _ in drows], *dbc, *[a for a, _, _ in arows], *[a for a, _, _ in cts])


def _rms_fn(x, g):
    return x * lax.rsqrt(jnp.mean(x * x, axis=-1, keepdims=True) + EPS) * g


def _rms_res_fn(x, g):
    return _rms_fn(x, g), x


def _rope_fn(q1, q2, k1, k2, cos, sin):
    return q1 * cos - q2 * sin, q2 * cos + q1 * sin, k1 * cos - k2 * sin, k2 * cos + k1 * sin


def _s5_act_fn(ymm, u, d):
    return jax.nn.gelu(ymm + d * u)


def _glu_fn(z1, z2, b1, b2):
    return (z1 + b1) * jax.nn.sigmoid(z2 + b2)


def _outnorm_fn(ym, ys, yd, gm, gs, gd):
    return jnp.concatenate([_rms_fn(ym, gm), _rms_fn(ys, gs), _rms_fn(yd, gd)], axis=-1)


def _swiglu_fn(a, b):
    return jax.nn.silu(a) * b


def _loss_fn(x, g, tgt):
    err = _rms_fn(x, g) - tgt
    return 0.5 * jnp.mean(err * err, axis=-1, keepdims=True)


def _dil_mix_fn(o0, o1, o2, l0, l1, l2):
    m = jnp.maximum(jnp.maximum(l0, l1), l2)
    e0, e1, e2 = jnp.exp(l0 - m), jnp.exp(l1 - m), jnp.exp(l2 - m)
    s = e0 + e1 + e2
    return (e0 / s) * o0 + (e1 / s) * o1 + (e2 / s) * o2


def _s5_disc_fn(a_re, a_im, ldt, b_r, b_i):
    lr = jnp.minimum(a_re.reshape(1, SSM_N), -1e-4)
    li = a_im.reshape(1, SSM_N)
    dt = jnp.exp(ldt.reshape(1, 1))
    e = jnp.exp(lr * dt)
    ar = e * jnp.cos(li * dt)
    ai = e * jnp.sin(li * dt)
    nr, ni = ar - 1.0, ai
    den = lr * lr + li * li
    cr = (nr * lr + ni * li) / den
    ci = (ni * lr - nr * li) / den
    return ar.reshape(1, 1, SSM_N), ai.reshape(1, 1, SSM_N), cr * b_r - ci * b_i, cr * b_i + ci * b_r


def _adamw_fn(w, g, m, v):
    m = ADAM_B1 * m + (1.0 - ADAM_B1) * g
    v = ADAM_B2 * v + (1.0 - ADAM_B2) * jnp.square(g)
    m_hat = m / (1.0 - ADAM_B1 ** ADAM_STEP)
    v_hat = v / (1.0 - ADAM_B2 ** ADAM_STEP)
    delta = -ADAM_LR * (m_hat / (jnp.sqrt(v_hat) + ADAM_EPS) + ADAM_WD * w)
    return delta, m, v


def _rms_fwd(name, x, g):
    (h,) = _rows_fwd(name, _rms_fn, [_row(x)], [g], [(x.shape, BF16, -2, _row(x)[2])])
    return h


def _rms_bwd(name, x, g, dh, dres=None):
    if dres is None:
        dx, dg = _rows_vjp(name, _rms_fn, [_row(x)], [g], [], [_row(dh)])
    else:
        dx, dg = _rows_vjp(name, _rms_res_fn, [_row(x)], [g], [], [_row(dh), _row(dres)])
    return dx, dg


MLA_TQ = 256


def _mla_fwd(q, k, v):
    tq = MLA_TQ

    def body(q_ref, k_ref, v_ref, o_ref, lse_ref):
        i = pl.program_id(1)
        s = _dot(q_ref[0], k_ref[0], NT) * MLA_SCALE
        row = i * tq + lax.broadcasted_iota(jnp.int32, (tq, T), 0)
        col = lax.broadcasted_iota(jnp.int32, (tq, T), 1)
        s = jnp.where(row >= col, s, -jnp.inf)
        m = jnp.max(s, axis=-1, keepdims=True)
        p = jnp.exp(s - m)
        l = jnp.sum(p, axis=-1, keepdims=True)
        o_ref[...] = _dot((p / l).astype(BF16), v_ref[0])
        lse_ref[0] = m + jnp.log(l)

    return pl.pallas_call(
        body, name="mla_fwd",
        out_shape=[jax.ShapeDtypeStruct((T, H_MLA * VDIM), F32), jax.ShapeDtypeStruct((H_MLA, T, 1), F32)],
        grid=(H_MLA, T // tq),
        in_specs=[pl.BlockSpec((1, tq, QK), lambda h, i: (h, i, 0)),
                  pl.BlockSpec((1, T, QK), lambda h, i: (h, 0, 0)),
                  pl.BlockSpec((1, T, VDIM), lambda h, i: (h, 0, 0))],
        out_specs=[pl.BlockSpec((tq, VDIM), lambda h, i: (i, h)),
                   pl.BlockSpec((1, tq, 1), lambda h, i: (h, i, 0))],
        compiler_params=_params(("parallel", "parallel")),
    )(q, k, v)


def _mla_bwd(q, k, v, do, lse):
    tq = MLA_TQ

    def body(q_ref, k_ref, v_ref, do_ref, lse_ref, dq_ref, dk_ref, dv_ref, dkpe_ref):
        h, i = pl.program_id(0), pl.program_id(1)

        @pl.when(i == 0)
        def _():
            dk_ref[...] = jnp.zeros_like(dk_ref)
            dv_ref[...] = jnp.zeros_like(dv_ref)

        @pl.when((i == 0) & (h == 0))
        def _():
            dkpe_ref[...] = jnp.zeros_like(dkpe_ref)

        q, k, v = q_ref[0], k_ref[0], v_ref[0]
        do = do_ref[...].astype(BF16)
        s = _dot(q, k, NT) * MLA_SCALE
        row = i * tq + lax.broadcasted_iota(jnp.int32, (tq, T), 0)
        col = lax.broadcasted_iota(jnp.int32, (tq, T), 1)
        p = jnp.where(row >= col, jnp.exp(s - lse_ref[0]), 0.0)
        dp = _dot(do, v, NT)
        delta = jnp.sum(dp * p, axis=-1, keepdims=True)
        ds = (p * (dp - delta) * MLA_SCALE).astype(BF16)
        dq_ref[0] = _dot(ds, k)
        dk = _dot(ds, q, TN)
        dk_ref[0] += dk
        dkpe_ref[...] += dk[:, NOPE:]
        dv_ref[0] += _dot(p.astype(BF16), do, TN)

    return pl.pallas_call(
        body, name="mla_bwd",
        out_shape=[jax.ShapeDtypeStruct((H_MLA, T, QK), F32), jax.ShapeDtypeStruct((H_MLA, T, QK), F32),
                   jax.ShapeDtypeStruct((H_MLA, T, VDIM), F32), jax.ShapeDtypeStruct((T, ROPE), F32)],
        grid=(H_MLA, T // tq),
        in_specs=[pl.BlockSpec((1, tq, QK), lambda h, i: (h, i, 0)),
                  pl.BlockSpec((1, T, QK), lambda h, i: (h, 0, 0)),
                  pl.BlockSpec((1, T, VDIM), lambda h, i: (h, 0, 0)),
                  pl.BlockSpec((tq, VDIM), lambda h, i: (i, h)),
                  pl.BlockSpec((1, tq, 1), lambda h, i: (h, i, 0))],
        out_specs=[pl.BlockSpec((1, tq, QK), lambda h, i: (h, i, 0)),
                   pl.BlockSpec((1, T, QK), lambda h, i: (h, 0, 0)),
                   pl.BlockSpec((1, T, VDIM), lambda h, i: (h, 0, 0)),
                   pl.BlockSpec((T, ROPE), lambda h, i: (0, 0))],
        compiler_params=_params(("arbitrary", "arbitrary")),
    )(q, k, v, do, lse)


NBLK = T // BLK


def _band_masks():
    r = lax.broadcasted_iota(jnp.int32, (BLK, BLK), 0)
    j = lax.broadcasted_iota(jnp.int32, (BLK, BLK), 1)
    return j <= r, j >= r


def _seq_start(p, i):
    per_seq = lax.shift_right_logical(jnp.int32(NBLK), 2 * p)
    return lax.rem(i, per_seq) == 0


def _band_fwd(q, k, v):
    def body(q_ref, kp_ref, kc_ref, vp_ref, vc_ref, o_ref, lse_ref):
        p, i = pl.program_id(0), pl.program_id(1)
        has_prev = jnp.logical_not(_seq_start(p, i))
        m_cur, m_prev = _band_masks()
        m_prev = m_prev & has_prev
        for h in range(DIL_H):
            qh = q_ref[0, h]
            s_c = jnp.where(m_cur, _dot(qh, kc_ref[0, h], NT) * DIL_SCALE, -jnp.inf)
            s_p = jnp.where(m_prev, _dot(qh, kp_ref[0, h], NT) * DIL_SCALE, -jnp.inf)
            m = jnp.maximum(jnp.max(s_c, axis=-1, keepdims=True), jnp.max(s_p, axis=-1, keepdims=True))
            e_c, e_p = jnp.exp(s_c - m), jnp.exp(s_p - m)
            l = jnp.sum(e_c, axis=-1, keepdims=True) + jnp.sum(e_p, axis=-1, keepdims=True)
            o_ref[0, h] = (_dot((e_p / l).astype(BF16), vp_ref[0, h]) + _dot((e_c / l).astype(BF16), vc_ref[0, h]))
            lse_ref[0, h] = m + jnp.log(l)

    blk = lambda w: (1, DIL_H, BLK, w)
    cur = lambda p, i: (p, 0, i, 0)
    prev = lambda p, i: (p, 0, jnp.maximum(i - 1, 0), 0)
    return pl.pallas_call(
        body, name="band_fwd",
        out_shape=[jax.ShapeDtypeStruct((3, DIL_H, T, DIL_D), F32), jax.ShapeDtypeStruct((3, DIL_H, T, 1), F32)],
        grid=(3, NBLK),
        in_specs=[pl.BlockSpec(blk(DIL_D), cur), pl.BlockSpec(blk(DIL_D), prev), pl.BlockSpec(blk(DIL_D), cur),
                  pl.BlockSpec(blk(DIL_D), prev), pl.BlockSpec(blk(DIL_D), cur)],
        out_specs=[pl.BlockSpec(blk(DIL_D), cur), pl.BlockSpec(blk(1), cur)],
        compiler_params=_params(("parallel", "parallel")),
    )(q, k, k, v, v)


def _band_bwd(q, k, v, o, lse, do, dlse):
    def body(qc_ref, qn_ref, kp_ref, kc_ref, vp_ref, vc_ref, oc_ref, on_ref, lc_ref, ln_ref,
             doc_ref, don_ref, dlc_ref, dln_ref, dq_ref, dk_ref, dv_ref):
        p, i = pl.program_id(0), pl.program_id(1)
        has_prev = jnp.logical_not(_seq_start(p, i))
        has_next = jnp.logical_not(_seq_start(p, i + 1)) & (i + 1 < NBLK)
        m_cur, m_prev = _band_masks()

        def probs(qh, kh, lse_h, mask):
            return jnp.where(mask, jnp.exp(_dot(qh, kh, NT) * DIL_SCALE - lse_h), 0.0)

        def dscore(pr, do_h, vh, shift):
            return (pr * (_dot(do_h, vh, NT) + shift) * DIL_SCALE).astype(BF16)

        for h in range(DIL_H):
            qc, qn, kp, kc, vp, vc = qc_ref[0, h], qn_ref[0, h], kp_ref[0, h], kc_ref[0, h], vp_ref[0, h], vc_ref[0, h]
            doc, don = doc_ref[0, h], don_ref[0, h]
            sh_c = dlc_ref[0, h] - jnp.sum(doc * oc_ref[0, h], axis=-1, keepdims=True)
            sh_n = dln_ref[0, h] - jnp.sum(don * on_ref[0, h], axis=-1, keepdims=True)
            doc, don = doc.astype(BF16), don.astype(BF16)
            p_cc = probs(qc, kc, lc_ref[0, h], m_cur)
            p_cp = probs(qc, kp, lc_ref[0, h], m_prev & has_prev)
            p_nc = probs(qn, kc, ln_ref[0, h], m_prev & has_next)
            ds_cc = dscore(p_cc, doc, vc, sh_c)
            ds_cp = dscore(p_cp, doc, vp, sh_c)
            ds_nc = dscore(p_nc, don, vc, sh_n)
            dq_ref[0, h] = _dot(ds_cc, kc) + _dot(ds_cp, kp)
            dk_ref[0, h] = _dot(ds_cc, qc, TN) + _dot(ds_nc, qn, TN)
            dv_ref[0, h] = _dot(p_cc.astype(BF16), doc, TN) + _dot(p_nc.astype(BF16), don, TN)

    blk = lambda w: (1, DIL_H, BLK, w)
    cur = lambda p, i: (p, 0, i, 0)
    prev = lambda p, i: (p, 0, jnp.maximum(i - 1, 0), 0)
    nxt = lambda p, i: (p, 0, jnp.minimum(i + 1, NBLK - 1), 0)
    w, one = pl.BlockSpec(blk(DIL_D), cur), pl.BlockSpec(blk(1), cur)
    wn, onen = pl.BlockSpec(blk(DIL_D), nxt), pl.BlockSpec(blk(1), nxt)
    wp = pl.BlockSpec(blk(DIL_D), prev)
    return pl.pallas_call(
        body, name="band_bwd",
        out_shape=[jax.ShapeDtypeStruct((3, DIL_H, T, DIL_D), F32)] * 3,
        grid=(3, NBLK),
        in_specs=[w, wn, wp, w, wp, w, w, wn, one, onen, w, wn, one, onen],
        out_specs=[w, w, w],
        compiler_params=_params(("parallel", "parallel")),
    )(q, q, k, k, v, v, o, o, lse, lse, do, do, dlse, dlse)


SCAN_TC = 256


def _scan_fwd(bu, ar, ai):
    tc, S = SCAN_TC, SSM_S

    def body(bu_ref, ar_ref, ai_ref, h_ref, h16_ref, cr_ref, ci_ref):
        @pl.when(pl.program_id(0) == 0)
        def _():
            cr_ref[...] = jnp.zeros_like(cr_ref)
            ci_ref[...] = jnp.zeros_like(ci_ref)

        a_r, a_i = ar_ref[...], ai_ref[...]

        def step(j, carry):
            hr, hi = carry
            for r in range(8):
                t = pl.multiple_of(j * 8, 8) + r
                br = bu_ref[pl.ds(t, 1), pl.ds(0, S)]
                bi = bu_ref[pl.ds(t, 1), pl.ds(S, S)]
                hr, hi = a_r * hr - a_i * hi + br, a_r * hi + a_i * hr + bi
                h_ref[pl.ds(t, 1), pl.ds(0, S)] = hr
                h_ref[pl.ds(t, 1), pl.ds(S, S)] = hi
            return hr, hi

        hr, hi = lax.fori_loop(0, tc // 8, step, (cr_ref[...], ci_ref[...]))
        cr_ref[...] = hr
        ci_ref[...] = hi
        h16_ref[...] = h_ref[...].astype(BF16)

    return pl.pallas_call(
        body, name="s5_scan_fwd",
        out_shape=[jax.ShapeDtypeStruct((T, 2 * S), F32), jax.ShapeDtypeStruct((T, 2 * S), BF16)],
        grid=(T // tc,),
        in_specs=[pl.BlockSpec((tc, 2 * S), lambda i: (i, 0)), _full_spec((1, S)), _full_spec((1, S))],
        out_specs=[pl.BlockSpec((tc, 2 * S), lambda i: (i, 0)), pl.BlockSpec((tc, 2 * S), lambda i: (i, 0))],
        scratch_shapes=[pltpu.VMEM((1, S), F32), pltpu.VMEM((1, S), F32)],
        compiler_params=_params(("arbitrary",)),
    )(bu, ar, ai)


def _scan_bwd(dh, h, ar, ai):
    tc, S = SCAN_TC, SSM_S
    nc = T // tc

    def body(dh_ref, h_ref, hp_ref, ar_ref, ai_ref, g16_ref, dar_ref, dai_ref, cr_ref, ci_ref, g_ref):
        i = pl.program_id(0)

        @pl.when(i == 0)
        def _():
            cr_ref[...] = jnp.zeros_like(cr_ref)
            ci_ref[...] = jnp.zeros_like(ci_ref)
            dar_ref[...] = jnp.zeros_like(dar_ref)
            dai_ref[...] = jnp.zeros_like(dai_ref)

        a_r, a_i = ar_ref[...], ai_ref[...]
        first_chunk = (i == nc - 1)
        edge = jnp.where(first_chunk, 0.0, 1.0)
        hpr = hp_ref[pl.ds(7, 1), pl.ds(0, S)] * edge
        hpi = hp_ref[pl.ds(7, 1), pl.ds(S, S)] * edge

        def step(jj, carry):
            gr, gi, dar, dai = carry
            j = tc // 8 - 1 - jj
            for r in range(7, -1, -1):
                t = pl.multiple_of(j * 8, 8) + r
                tp = jnp.maximum(t - 1, 0)
                inside = t > 0
                pr = jnp.where(inside, h_ref[pl.ds(tp, 1), pl.ds(0, S)], hpr)
                pi = jnp.where(inside, h_ref[pl.ds(tp, 1), pl.ds(S, S)], hpi)
                gr, gi = (dh_ref[pl.ds(t, 1), pl.ds(0, S)] + a_r * gr + a_i * gi,
                          dh_ref[pl.ds(t, 1), pl.ds(S, S)] + a_r * gi - a_i * gr)
                g_ref[pl.ds(t, 1), pl.ds(0, S)] = gr
                g_ref[pl.ds(t, 1), pl.ds(S, S)] = gi
                dar = dar + gr * pr + gi * pi
                dai = dai + gi * pr - gr * pi
            return gr, gi, dar, dai

        zero = jnp.zeros((1, S), F32)
        gr, gi, dar, dai = lax.fori_loop(0, tc // 8, step, (cr_ref[...], ci_ref[...], zero, zero))
        cr_ref[...] = gr
        ci_ref[...] = gi
        dar_ref[...] += dar
        dai_ref[...] += dai
        g16_ref[...] = g_ref[...].astype(BF16)

    rev = lambda i: (nc - 1 - i, 0)
    before = lambda i: (jnp.maximum((nc - 1 - i) * (tc // 8) - 1, 0), 0)
    return pl.pallas_call(
        body, name="s5_scan_bwd",
        out_shape=[jax.ShapeDtypeStruct((T, 2 * S), BF16), jax.ShapeDtypeStruct((1, S), F32),
                   jax.ShapeDtypeStruct((1, S), F32)],
        grid=(nc,),
        in_specs=[pl.BlockSpec((tc, 2 * S), rev), pl.BlockSpec((tc, 2 * S), rev), pl.BlockSpec((8, 2 * S), before),
                  _full_spec((1, S)), _full_spec((1, S))],
        out_specs=[pl.BlockSpec((tc, 2 * S), rev), _full_spec((1, S)), _full_spec((1, S))],
        scratch_shapes=[pltpu.VMEM((1, S), F32), pltpu.VMEM((1, S), F32), pltpu.VMEM((tc, 2 * S), F32)],
        compiler_params=_params(("arbitrary",)),
    )(dh, h, h, ar, ai)


def _sum_rows(name, x):
    def body(x_ref, o_ref):
        o_ref[...] = jnp.sum(x_ref[...], axis=0, keepdims=True)

    return pl.pallas_call(body, name=name, out_shape=jax.ShapeDtypeStruct((1, 1), F32),
                          in_specs=[_full_spec(x.shape)], out_specs=_full_spec((1, 1)), grid=(1,))(x)


ANY = pl.BlockSpec(memory_space=pl.ANY)


def _place():
    return lax.axis_index("x"), lax.axis_index("y"), lax.axis_index("c")


def _all_gather(name, shards):
    n = len(shards)

    def body(*refs):
        xs, outs = refs[:n], refs[n:2 * n]
        send_sems, recv_sems, local_sems = refs[2 * n:]
        x, y, c = _place()
        me, sibling = (x, y, c), (x, y, 1 - c)
        chips = [(1 - x, y), (x, 1 - y), (1 - x, 1 - y)]

        def slot(t, px, py, pc):
            return outs[t].at[4 * px + 2 * py + pc]

        def copy(t, k, block, to, src=None):
            return pltpu.make_async_remote_copy(
                src_ref=slot(t, *block) if src is None else src, dst_ref=slot(t, *block),
                send_sem=send_sems.at[7 * t + k], recv_sem=recv_sems.at[7 * t + k], device_id=to, device_id_type=MESH)

        mine = [pltpu.make_async_copy(xs[t], slot(t, *me), local_sems.at[t]) for t in range(n)]
        sent = []
        for t in range(n):
            mine[t].start()
            sent.append(copy(t, 0, me, sibling, src=xs[t]))
            sent += [copy(t, 1 + j, me, (*chip, c), src=xs[t]) for j, chip in enumerate(chips)]
        for cp in sent:
            cp.start()
        for j, chip in enumerate(chips):
            for t in range(n):
                copy(t, 1 + j, (*chip, c), me).wait_recv()
                fwd = copy(t, 4 + j, (*chip, c), sibling)
                fwd.start()
                sent.append(fwd)
        for t in range(n):
            copy(t, 0, sibling, me).wait_recv()
            for j, chip in enumerate(chips):
                copy(t, 4 + j, (*chip, 1 - c), me).wait_recv()
        for cp in sent:
            cp.wait_send()
        for cp in mine:
            cp.wait()

    return pl.pallas_call(
        body, name=name, out_shape=[jax.ShapeDtypeStruct((N_DEV,) + s.shape, s.dtype) for s in shards],
        in_specs=[ANY] * n, out_specs=[ANY] * n,
        scratch_shapes=[pltpu.SemaphoreType.DMA((7 * n,)), pltpu.SemaphoreType.DMA((7 * n,)),
                        pltpu.SemaphoreType.DMA((n,))],
    )(*shards)


HBM = pl.BlockSpec(memory_space=pltpu.HBM)
SEM = pl.BlockSpec(memory_space=pltpu.SEMAPHORE)
DATAFLOW = pltpu.SideEffectType.DATAFLOW_SIDE_EFFECTING


def _hbm(a):
    return pltpu.with_memory_space_constraint(a, pltpu.HBM)


def _split_start(name, srcs, lands, ncopies, plan, after):
    ns, nl = len(srcs), len(lands)

    def body(*refs):
        send_sems, recv_sems = refs[ns + nl + 1], refs[ns + nl + 2]
        token = refs[-1]
        for k, (src, dst, peer, _) in enumerate(plan(refs[:ns], refs[ns:ns + nl])):
            pltpu.make_async_remote_copy(src_ref=src, dst_ref=dst, send_sem=send_sems.at[k], recv_sem=recv_sems.at[k],
                                         device_id=peer, device_id_type=MESH).start()
        token[...] = jnp.zeros_like(token)

    out = pl.pallas_call(
        body, name=name,
        out_shape=(pltpu.SemaphoreType.DMA((ncopies,)), pltpu.SemaphoreType.DMA((ncopies,)),
                   *[pltpu.HBM(a.shape, a.dtype) for a in srcs], *[pltpu.HBM(a.shape, a.dtype) for a in lands],
                   jax.ShapeDtypeStruct((8, LANES), F32)),
        in_specs=[HBM] * (ns + nl) + [ANY],
        out_specs=(SEM, SEM, *[HBM] * (ns + nl), pl.BlockSpec(memory_space=pltpu.VMEM)),
        input_output_aliases={i: 2 + i for i in range(ns + nl)},
        compiler_params=pltpu.CompilerParams(has_side_effects=DATAFLOW),
    )(*[_hbm(a) for a in srcs], *[_hbm(a) for a in lands], after)
    return out[0], out[1], list(out[2:2 + ns]), list(out[2 + ns:2 + ns + nl]), out[-1]


def _split_wait(name, handle, ncopies, plan, after):
    send_sems, recv_sems, srcs, lands, _ = handle
    ns, nl = len(srcs), len(lands)

    def body(*refs):
        s_sems, r_sems = refs[ns + nl], refs[ns + nl + 1]
        for k, (src, dst, peer, mine) in enumerate(plan(refs[:ns], refs[ns:ns + nl])):
            pltpu.make_async_remote_copy(src_ref=src, dst_ref=dst, send_sem=s_sems.at[k], recv_sem=r_sems.at[k],
                                         device_id=peer, device_id_type=MESH).wait_send()
            pltpu.make_async_remote_copy(src_ref=src, dst_ref=mine, send_sem=s_sems.at[k], recv_sem=r_sems.at[k],
                                         device_id=peer, device_id_type=MESH).wait_recv()

    out = pl.pallas_call(
        body, name=name,
        out_shape=(*[pltpu.HBM(a.shape, a.dtype) for a in srcs], *[pltpu.HBM(a.shape, a.dtype) for a in lands]),
        in_specs=[HBM] * (ns + nl) + [SEM, SEM, ANY],
        out_specs=tuple([HBM] * (ns + nl)),
        input_output_aliases={i: i for i in range(ns + nl)},
        compiler_params=pltpu.CompilerParams(has_side_effects=DATAFLOW),
    )(*srcs, *lands, send_sems, recv_sems, after)
    return list(out[:ns]), list(out[ns:])


def _slot(px, py, pc):
    return 4 * px + 2 * py + pc


def _gather_plan(xs, lands):
    x, y, c = _place()
    peers = [(x, y, 1 - c), (1 - x, y, c), (x, 1 - y, c), (1 - x, 1 - y, c)]
    return [(xs[t], lands[t].at[_slot(x, y, c)], peer, lands[t].at[_slot(*peer)])
            for t in range(len(xs)) for peer in peers]


def _gather_start(name, shards, after):
    lands = [lax.empty((N_DEV,) + s.shape, s.dtype) for s in shards]
    return _split_start(name, shards, lands, 4 * len(shards), _gather_plan, after)


def _pass_on_plan(_, lands):
    x, y, c = _place()
    blocks = [((1 - x, y, c), (1 - x, y, 1 - c)), ((x, 1 - y, c), (x, 1 - y, 1 - c)),
              ((1 - x, 1 - y, c), (1 - x, 1 - y, 1 - c)), ((x, y, 1 - c), (x, y, c))]
    return [(lands[t].at[_slot(*out)], lands[t].at[_slot(*out)], (x, y, 1 - c), lands[t].at[_slot(*back)])
            for t in range(len(lands)) for out, back in blocks]


def _gather_pass_on(name, handle, after):
    n = len(handle[2])
    _, lands = _split_wait(name + "_wait", handle, 4 * n, _gather_plan, after)
    return _split_start(name + "_pass_on", [], lands, 4 * n, _pass_on_plan, after)


def _gather_finish(name, handle, after):
    n = len(handle[3])
    _, lands = _split_wait(name + "_done", handle, 4 * n, _pass_on_plan, after)
    return lands


def _sibling_plan(ps, gots):
    x, y, c = _place()
    return [(ps[t].at[j, 1 - c], gots[t].at[j], (x, y, 1 - c), gots[t].at[j]) for t in range(len(ps)) for j in range(4)]


def _chips_plan(ss, gots):
    x, y, c = _place()
    chips = [(1 - x, y), (x, 1 - y), (1 - x, 1 - y)]
    return [(ss[t].at[2 * px + py], gots[t].at[k], (px, py, c), gots[t].at[k])
            for t in range(len(ss)) for k, (px, py) in enumerate(chips)]


def _sum_tile(R, C):
    return _tile(R, max(8, min(1024, ROW_BLOCK_BYTES // (C * 4))), 8)


def _pair_sum(name, p4, got, core):
    _, _, R, C = p4.shape
    tr = _sum_tile(R, C)

    def body(core_ref, p_ref, g_ref, o_ref):
        o_ref[...] = p_ref[:, 0] + g_ref[...]

    return pl.pallas_call(
        body, name=name, out_shape=jax.ShapeDtypeStruct((4, R, C), F32),
        grid_spec=pltpu.PrefetchScalarGridSpec(
            num_scalar_prefetch=1, grid=(4, R // tr),
            in_specs=[pl.BlockSpec((1, 1, tr, C), lambda j, i, core: (j, core[0], i, 0)),
                      pl.BlockSpec((1, tr, C), lambda j, i, core: (j, i, 0))],
            out_specs=pl.BlockSpec((1, tr, C), lambda j, i, core: (j, i, 0))),
        compiler_params=_params(("parallel", "parallel")),
    )(core, p4, got)


def _sum_devices(name, g8):
    _, R, C = g8.shape
    tr = _tile(R, SMALL_ROW_ALIGN, 8)

    def body(g_ref, o_ref):
        acc = g_ref[0]
        for d in range(1, N_DEV):
            acc = acc + g_ref[d]
        o_ref[...] = acc

    return pl.pallas_call(
        body, name=name, out_shape=jax.ShapeDtypeStruct((R, C), F32), grid=(R // tr,),
        in_specs=[pl.BlockSpec((N_DEV, tr, C), lambda i: (0, i, 0))], out_specs=pl.BlockSpec((tr, C), lambda i: (i, 0)),
        compiler_params=_params(("parallel",)),
    )(g8)


def _adamw_shard(name, layer, w, m, v, s4, got, chip, bufs):
    _, R, C = w.shape
    tr = _sum_tile(R, C) // 2 if _sum_tile(R, C) % 16 == 0 else _sum_tile(R, C)

    def body(chip_ref, w_ref, m_ref, v_ref, s_ref, g_ref, b0, b1, b2, b3, og, od, om, ov):
        g = ((s_ref[0] + g_ref[0]) + g_ref[1]) + g_ref[2]
        d, nm, nv = _adamw_fn(w_ref[0], g, m_ref[0], v_ref[0])
        og[0], od[0], om[0], ov[0] = g, d, nm, nv

    lay = pl.BlockSpec((1, tr, C), lambda i, chip: (layer, i, 0))
    if bufs is None:
        bufs = [lax.empty(w.shape, F32) for _ in range(4)]
    return pl.pallas_call(
        body, name=name, out_shape=[jax.ShapeDtypeStruct(w.shape, F32)] * 4,
        grid_spec=pltpu.PrefetchScalarGridSpec(
            num_scalar_prefetch=1, grid=(R // tr,),
            in_specs=[lay, lay, lay, pl.BlockSpec((1, tr, C), lambda i, chip: (chip[0], i, 0)),
                      pl.BlockSpec((3, tr, C), lambda i, chip: (0, i, 0)), ANY, ANY, ANY, ANY],
            out_specs=[lay] * 4),
        input_output_aliases={6: 0, 7: 1, 8: 2, 9: 3},
        compiler_params=_params(("parallel",)),
    )(chip, w, m, v, s4, got, *bufs)


def _adamw(name, wt, g, m, v):
    shape = wt.shape
    two = (lambda a: a.reshape(1, -1)) if wt.ndim == 1 else (lambda a: a.reshape(-1, shape[-1]))
    w2, g2, m2, v2 = two(wt), two(g), two(m), two(v)
    tr = _row(w2, None, 0)[2]
    outs = [(w2.shape, F32, 0, tr)] * 3
    d, nm, nv = _rows_fwd(name, _adamw_fn, [_row(a, tr, 0) for a in (w2, g2, m2, v2)], [], outs)
    return d.reshape(shape), nm.reshape(shape), nv.reshape(shape)


def _block_diag(v):
    eye = jnp.eye(SSM_G, dtype=v.dtype)
    return (eye[:, None, :, None] * v[:, :, None, :]).reshape(SSM_G * SSM_P, SSM_G * SSM_N)


def _diag_blocks(m):
    m4 = m.reshape(SSM_G, SSM_P, SSM_G, SSM_N)
    eye = jnp.eye(SSM_G, dtype=m.dtype)
    return (m4 * eye[:, None, :, None]).sum(axis=2)


def _fold(a, dil):
    if dil == 1:
        return a
    return a.reshape((T // dil, dil) + a.shape[1:]).swapaxes(0, 1).reshape(a.shape)


def _unfold(a, dil):
    if dil == 1:
        return a
    return a.reshape((dil, T // dil) + a.shape[1:]).swapaxes(0, 1).reshape(a.shape)


DILS = (1, 4, 16)


def _fold3(a):
    return jnp.stack([_fold(a, d) for d in DILS]).transpose(0, 2, 1, 3)


def _unfold3(a):
    return jnp.stack([_unfold(a[p].swapaxes(0, 1), d).swapaxes(0, 1) for p, d in enumerate(DILS)])


def _refold3(a):
    return jnp.stack([_fold(a[p].swapaxes(0, 1), d).swapaxes(0, 1) for p, d in enumerate(DILS)])


def _rope_tables():
    half = ROPE // 2
    inv_freq = ROPE_THETA ** (-jnp.arange(half, dtype=F32) / half)
    ang = jnp.arange(T).astype(F32)[:, None] * inv_freq[None, :]
    return jnp.cos(ang), jnp.sin(ang)


def _from_blocks(a8):
    return a8.transpose(1, 0, 2).reshape(a8.shape[1], -1)


def _to_blocks(a):
    return a.reshape(a.shape[0], N_DEV, -1).transpose(1, 0, 2)


HALF = ROPE // 2


def _rope_rows(q1, q2, k1, k2, cos, sin):
    return [_row(q1, 256, 1), _row(q2, 256, 1), _row(k1, 256, 0), _row(k2, 256, 0)], [_row(cos, 256, 0), _row(sin, 256, 0)]


def _behind(a, tok):
    return a if tok is None else a + tok


def _mixer_fwd_in(x, w, sp, cos, sin):
    s = {}
    s['x'] = x
    h = _rms_fwd("rms_mix", x, sp['g_mix'])
    proj = _from_blocks(_mm("mm_in", h, w['w_in'], 'nn', bb='c', ob='c'))
    offs = np.cumsum((0,) + IN_SPLITS)
    c_q, c_kv, k_rope, u, qd, kd, vd = [proj[:, offs[i]:offs[i + 1]] for i in range(7)]
    s.update(h=h, c_q=c_q, c_kv=c_kv, u=u)

    cqn = _rms_fwd("rms_q", c_q, sp['g_q'])
    ckvn = _rms_fwd("rms_kv", c_kv, sp['g_kv'])
    q8 = _mm("mm_uq", cqn, w['w_uq'], 'nn', bb='c', ob='c')
    kv8 = _mm("mm_ukv", ckvn, w['w_ukv'], 'nn', bb='c', ob='c')
    rows, aux = _rope_rows(q8[:, :, NOPE:NOPE + HALF], q8[:, :, NOPE + HALF:], k_rope[:, :HALF], k_rope[:, HALF:], cos, sin)
    oq1, oq2, ok1, ok2 = _rows_fwd("rope", _rope_fn, rows + aux, [],
                                   [((H_MLA, T, HALF), F32, 1, 256)] * 2 + [((T, HALF), F32, 0, 256)] * 2)
    k_pe = jnp.concatenate([ok1, ok2], axis=-1)
    qh = jnp.concatenate([q8[:, :, :NOPE], oq1, oq2], axis=-1).astype(BF16)
    kh = jnp.concatenate([kv8[:, :, :NOPE], jnp.broadcast_to(k_pe[None], (H_MLA, T, ROPE))], axis=-1).astype(BF16)
    vh = kv8[:, :, NOPE:].astype(BF16)
    y_mla, lse_mla = _mla_fwd(qh, kh, vh)
    s.update(cqn=cqn, ckvn=ckvn, qh=qh, kh=kh, vh=vh, lse_mla=lse_mla, y_mla=y_mla, qd=qd, kd=kd, vd=vd)
    return s


def _mixer_fwd_out(s, w, sp, tok=None):
    x, u, y_mla, qd, kd, vd = s['x'], s['u'], s['y_mla'], s['qd'], s['kd'], s['vd']
    a3 = lambda n: sp[n].reshape(SSM_G, 1, SSM_N)
    b2 = lambda n: sp[n].transpose(0, 2, 1).reshape(SSM_G * SSM_P, SSM_N)
    disc_rows = [_row(a3('a_re'), 1, 0), _row(a3('a_im'), 1, 0), _row(sp['log_dt'].reshape(SSM_G, 1, 1), 1, 0),
                 _row(b2('b_re'), SSM_P, 0), _row(b2('b_im'), SSM_P, 0)]
    abr, abi, bbr, bbi = _rows_fwd(
        "s5_disc", _s5_disc_fn, disc_rows, [],
        [((SSM_G, 1, SSM_N), F32, 0, 1), ((SSM_G, 1, SSM_N), F32, 0, 1),
         ((SSM_G * SSM_P, SSM_N), F32, 0, SSM_P), ((SSM_G * SSM_P, SSM_N), F32, 0, SSM_P)])
    ar, ai = abr.reshape(1, SSM_S), abi.reshape(1, SSM_S)
    b_mat = jnp.concatenate([_block_diag(bbr.reshape(SSM_G, SSM_P, SSM_N)),
                             _block_diag(bbi.reshape(SSM_G, SSM_P, SSM_N))], axis=1)
    c_mat = jnp.concatenate([_block_diag(sp['c_re']), _block_diag(-sp['c_im'])], axis=1)
    u16, b_mat, c_mat = _behind(u, tok).astype(BF16), b_mat.astype(BF16), c_mat.astype(BF16)
    bu = _mm("mm_s5_b", u16, b_mat, 'nn')
    hst, hst16 = _scan_fwd(bu, ar, ai)
    ymm = _mm("mm_s5_c", hst16, c_mat, 'nt')
    d_row = sp['d_skip'].reshape(1, SSM_W)
    (yg,) = _rows_fwd("s5_act", _s5_act_fn, [_row(ymm), _row(u)], [d_row], [((T, SSM_W), BF16, -2, _row(u)[2])])
    z = _mm("mm_glu", yg, w['w_glu'], 'nn', bb='c')
    glu_rows = [_row(z[:, :SSM_W]), _row(z[:, SSM_W:])]
    glu_b = [sp['b_glu'][:SSM_W].reshape(1, -1), sp['b_glu'][SSM_W:].reshape(1, -1)]
    (y_ssm,) = _rows_fwd("s5_glu", _glu_fn, glu_rows, glu_b, [((T, SSM_W), F32, -2, glu_rows[0][2])])
    s.update(disc_rows=disc_rows, ar=ar, ai=ai, b_mat=b_mat, c_mat=c_mat, hst=hst, hst16=hst16, u16=u16, ymm=ymm,
             d_row=d_row, yg=yg,
             glu_rows=glu_rows, glu_b=glu_b)

    f3 = lambda a: _fold3(a.reshape(T, DIL_H, DIL_D)).astype(BF16)
    qf, kf, vf = f3(qd), f3(kd), f3(vd)
    o_f, lse_f = _band_fwd(qf, kf, vf)
    o_n, lse_n = _unfold3(o_f), _unfold3(lse_f)
    mix_rows = [_row(o_n[p], 256, 1) for p in range(3)] + [_row(lse_n[p], 256, 1) for p in range(3)]
    (y_dh,) = _rows_fwd("dil_mix", _dil_mix_fn, mix_rows, [], [((DIL_H, T, DIL_D), F32, 1, 256)])
    y_dil = y_dh.transpose(1, 0, 2).reshape(T, DIL_W)
    s.update(qf=qf, kf=kf, vf=vf, o_f=o_f, lse_f=lse_f, mix_rows=mix_rows)

    gm, gs, gd = sp['g_out_mla'].reshape(1, -1), sp['g_out_ssm'].reshape(1, -1), sp['g_out_dil'].reshape(1, -1)
    on_rows = [_row(y_mla), _row(y_ssm), _row(y_dil)]
    (ycat,) = _rows_fwd("out_norm", _outnorm_fn, on_rows, [gm, gs, gd], [((T, D), BF16, -2, on_rows[0][2])])
    x1_ = _mm("mm_o", ycat, w['w_o'], 'nn', bb='r', res=x)
    s.update(on_rows=on_rows, on_g=[gm, gs, gd], ycat=ycat, x1=x1_)
    for k in ('y_mla', 'qd', 'kd', 'vd'):
        del s[k]
    return x1_


def _ffn_fwd(x1_, w, sp, s, tok=None):
    h2 = _rms_fwd("rms_ffn", x1_, _behind(sp['g_ffn'], tok))
    ga = _mm("mm_gate", h2, w['w_gate'], 'nn', bb='c', ob='c')
    gb = _mm("mm_up", h2, w['w_up'], 'nn', bb='c', ob='c')
    ffn_rows = [_row(ga, None, 1), _row(gb, None, 1)]
    (zf,) = _rows_fwd("swiglu", _swiglu_fn, ffn_rows, [], [(ga.shape, BF16, 1, ffn_rows[0][2])])
    x2_ = _mm("mm_down", zf, w['w_down'], 'nn', ab='c', bb='r', res=x1_)
    s.update(h2=h2, ffn_rows=ffn_rows, zf=zf)
    return x2_


def _b16(a):
    return a.astype(BF16)


def _ffn_bwd(dx2, s, w, sp, tok=None):
    gw, gs_ = {}, {}
    b16 = _b16
    dx2b = b16(_behind(dx2, tok))
    dzf = _mm("mm_down_dx", dx2b, w['w_down'], 'nt', bb='r', ob='c')
    gw['w_down'] = _mm("mm_down_dw", s['zf'], dx2b, 'tn', ab='c', ob='r')
    tr = s['ffn_rows'][0][2]
    dga, dgb = _rows_vjp("swiglu_bwd", _swiglu_fn, s['ffn_rows'], [], [], [_row(dzf, tr, 1)], grad_dtypes=[BF16, BF16])
    gw['w_gate'] = _mm("mm_gate_dw", s['h2'], dga, 'tn', bb='c', ob='c')
    gw['w_up'] = _mm("mm_up_dw", s['h2'], dgb, 'tn', bb='c', ob='c')
    dh2 = _mm("mm_up_dx", dgb, w['w_up'], 'nt', ab='c', bb='c',
              res=_mm("mm_gate_dx", dga, w['w_gate'], 'nt', ab='c', bb='c'))
    dx1, gs_['g_ffn'] = _rms_bwd("rms_ffn_bwd", s['x1'], sp['g_ffn'], dh2, dx2)
    return dx1, gw, gs_


def _mixer_bwd_out(dx1, s, w, sp, tok=None):
    gw, gs_ = {}, {}
    b16 = _b16
    dx1b = b16(_behind(dx1, tok))
    dycat = _mm("mm_o_dx", dx1b, w['w_o'], 'nt', bb='r')
    gw['w_o'] = _mm("mm_o_dw", s['ycat'], dx1b, 'tn', ob='r')
    dy_mla, dy_ssm, dy_dil, gs_['g_out_mla'], gs_['g_out_ssm'], gs_['g_out_dil'] = _rows_vjp(
        "out_norm_bwd", _outnorm_fn, s['on_rows'], s['on_g'], [], [_row(dycat)])

    do_h = dy_dil.reshape(T, DIL_H, DIL_D).transpose(1, 0, 2)
    dmix = _rows_vjp("dil_mix_bwd", _dil_mix_fn, s['mix_rows'], [], [], [_row(do_h, 256, 1)])
    do_n, dlse_n = jnp.stack(dmix[:3]), jnp.stack(dmix[3:])
    dqf, dkf, dvf = _band_bwd(s['qf'], s['kf'], s['vf'], s['o_f'], s['lse_f'], _refold3(do_n), _refold3(dlse_n))
    back = lambda a: _unfold3(a).sum(axis=0).transpose(1, 0, 2).reshape(T, DIL_W)
    dqd, dkd, dvd = back(dqf), back(dkf), back(dvf)

    dz1, dz2, db1, db2 = _rows_vjp("s5_glu_bwd", _glu_fn, s['glu_rows'], s['glu_b'], [], [_row(dy_ssm)])
    gs_['b_glu'] = jnp.concatenate([db1, db2], axis=1)
    dzb = b16(jnp.concatenate([dz1, dz2], axis=1))
    dyg = _mm("mm_glu_dx", dzb, w['w_glu'], 'nt', bb='c')
    gw['w_glu'] = _mm("mm_glu_dw", s['yg'], dzb, 'tn', ob='c')
    dymm, du_act, dd = _rows_vjp("s5_act_bwd", _s5_act_fn, [_row(s['ymm']), _row(s['u'])], [s['d_row']], [], [_row(dyg)],
                                 grad_dtypes=[BF16, F32])
    gs_['d_skip'] = dd
    dhst = _mm("mm_s5_c_dx", dymm, s['c_mat'], 'nn')
    dc_mat = _mm("mm_s5_c_dw", dymm, s['hst16'], 'tn')
    g, dar, dai = _scan_bwd(dhst, s['hst'], s['ar'], s['ai'])
    du = _mm("mm_s5_b_dx", g, s['b_mat'], 'nt', res=du_act)
    db_mat = _mm("mm_s5_b_dw", s['u16'], g, 'tn')
    gs_['c_re'] = _diag_blocks(dc_mat[:, :SSM_S])
    gs_['c_im'] = -_diag_blocks(dc_mat[:, SSM_S:])
    dbbr = _diag_blocks(db_mat[:, :SSM_S]).reshape(SSM_G * SSM_P, SSM_N)
    dbbi = _diag_blocks(db_mat[:, SSM_S:]).reshape(SSM_G * SSM_P, SSM_N)
    disc_cts = [_row(dar.reshape(SSM_G, 1, SSM_N), 1, 0), _row(dai.reshape(SSM_G, 1, SSM_N), 1, 0),
                _row(dbbr, SSM_P, 0), _row(dbbi, SSM_P, 0)]
    da_re, da_im, dldt, db_r, db_i = _rows_vjp("s5_disc_bwd", _s5_disc_fn, s['disc_rows'], [], [], disc_cts)
    gs_['a_re'], gs_['a_im'], gs_['log_dt'] = da_re, da_im, dldt
    unb = lambda a: a.reshape(SSM_G, SSM_P, SSM_N).transpose(0, 2, 1)
    gs_['b_re'], gs_['b_im'] = unb(db_r), unb(db_i)
    return (dy_mla, du, dqd, dkd, dvd), gw, gs_


def _mixer_bwd_in(cts, dx1, s, w, sp, cos, sin, tok=None):
    gw, gs_ = {}, {}
    b16 = _b16
    dy_mla, du, dqd, dkd, dvd = cts
    dqh, dkh, dvh, dkpe = _mla_bwd(s['qh'], s['kh'], s['vh'], dy_mla, _behind(s['lse_mla'], tok))
    zq, zk = jnp.zeros((H_MLA, T, HALF), F32), jnp.zeros((T, HALF), F32)
    rows, aux = _rope_rows(zq, zq, zk, zk, cos, sin)
    cts, _ = _rope_rows(dqh[:, :, NOPE:NOPE + HALF], dqh[:, :, NOPE + HALF:], dkpe[:, :HALF], dkpe[:, HALF:], cos, sin)
    dq1, dq2, dk1, dk2 = _rows_vjp("rope_bwd", _rope_fn, rows, [], aux, cts)
    dq8 = b16(jnp.concatenate([dqh[:, :, :NOPE], dq1, dq2], axis=-1))
    dkv8 = b16(jnp.concatenate([dkh[:, :, :NOPE], dvh], axis=-1))
    dk_rope = jnp.concatenate([dk1, dk2], axis=-1)
    dcqn = _mm("mm_uq_dx", dq8, w['w_uq'], 'nt', ab='c', bb='c')
    gw['w_uq'] = _mm("mm_uq_dw", s['cqn'], dq8, 'tn', bb='c', ob='c')
    dckvn = _mm("mm_ukv_dx", dkv8, w['w_ukv'], 'nt', ab='c', bb='c')
    gw['w_ukv'] = _mm("mm_ukv_dw", s['ckvn'], dkv8, 'tn', bb='c', ob='c')
    dc_q, gs_['g_q'] = _rms_bwd("rms_q_bwd", s['c_q'], sp['g_q'], dcqn)
    dc_kv, gs_['g_kv'] = _rms_bwd("rms_kv_bwd", s['c_kv'], sp['g_kv'], dckvn)

    dproj = _to_blocks(b16(jnp.concatenate([dc_q, dc_kv, dk_rope, du, dqd, dkd, dvd], axis=1)))
    dh = _mm("mm_in_dx", dproj, w['w_in'], 'nt', ab='c', bb='c')
    gw['w_in'] = _mm("mm_in_dw", s['h'], dproj, 'tn', bb='c', ob='c')
    dx, gs_['g_mix'] = _rms_bwd("rms_mix_bwd", s['x'], sp['g_mix'], dh, dx1)
    return dx, gw, gs_


def kernel(x, g_mix, w_in, g_q, w_uq, g_kv, w_ukv, a_re, a_im, b_re, b_im, c_re, c_im, d_skip, log_dt, w_glu, b_glu, g_out_mla, g_out_ssm, g_out_dil, w_o, g_ffn, w_gate, w_up, w_down, g_final, loss_target, m_g_mix, m_w_in, m_g_q, m_w_uq, m_g_kv, m_w_ukv, m_a_re, m_a_im, m_b_re, m_b_im, m_c_re, m_c_im, m_d_skip, m_log_dt, m_w_glu, m_b_glu, m_g_out_mla, m_g_out_ssm, m_g_out_dil, m_w_o, m_g_ffn, m_w_gate, m_w_up, m_w_down, m_g_final, v_g_mix, v_w_in, v_g_q, v_w_uq, v_g_kv, v_w_ukv, v_a_re, v_a_im, v_b_re, v_b_im, v_c_re, v_c_im, v_d_skip, v_log_dt, v_w_glu, v_b_glu, v_g_out_mla, v_g_out_ssm, v_g_out_dil, v_w_o, v_g_ffn, v_w_gate, v_w_up, v_w_down, v_g_final):
    W = dict(zip(PARAMS, (g_mix, w_in, g_q, w_uq, g_kv, w_ukv, a_re, a_im, b_re, b_im, c_re, c_im, d_skip, log_dt,
                          w_glu, b_glu, g_out_mla, g_out_ssm, g_out_dil, w_o, g_ffn, w_gate, w_up, w_down, g_final)))
    M = dict(zip(PARAMS, (m_g_mix, m_w_in, m_g_q, m_w_uq, m_g_kv, m_w_ukv, m_a_re, m_a_im, m_b_re, m_b_im, m_c_re,
                          m_c_im, m_d_skip, m_log_dt, m_w_glu, m_b_glu, m_g_out_mla, m_g_out_ssm, m_g_out_dil, m_w_o,
                          m_g_ffn, m_w_gate, m_w_up, m_w_down, m_g_final)))
    V = dict(zip(PARAMS, (v_g_mix, v_w_in, v_g_q, v_w_uq, v_g_kv, v_w_ukv, v_a_re, v_a_im, v_b_re, v_b_im, v_c_re,
                          v_c_im, v_d_skip, v_log_dt, v_w_glu, v_b_glu, v_g_out_mla, v_g_out_ssm, v_g_out_dil, v_w_o,
                          v_g_ffn, v_w_gate, v_w_up, v_w_down, v_g_final)))
    cx, cy, cc = _place()
    core = cc.astype(jnp.int32).reshape(1)
    chip = (2 * cx + cy).astype(jnp.int32).reshape(1)
    cos, sin = _rope_tables()
    small = [{n: W[n][l] for n in SMALL} for l in range(DEPTH)]
    for sp in small:
        for n in ('g_mix', 'g_q', 'g_kv', 'g_ffn'):
            sp[n] = sp[n].reshape(1, -1)

    def tok_of(tokens):
        return sum(t[0, 0] for t in tokens) if tokens else None

    def gather_start(l, group, names, after):
        return _gather_start(f"gather_{group}_start_{l}", [W[n][l].astype(BF16) for n in names], after)

    xa = x[0]
    h1_mix = gather_start(0, "mix", MIXER_W, jnp.zeros((8, LANES), F32))
    h1_ffn = gather_start(0, "ffn", FFN_W, h1_mix[4])
    h2_mix = _gather_pass_on("gather_mix_0", h1_mix, xa)
    saved, full = [], []
    tokens = [h2_mix[4]]
    for l in range(DEPTH):
        last = l + 1 == DEPTH
        wm = dict(zip(MIXER_W, _gather_finish(f"gather_mix_{l}", h2_mix, xa)))
        sp = dict(small[l])
        sp['g_mix'] = _behind(sp['g_mix'], tok_of(tokens))
        s = _mixer_fwd_in(xa, wm, sp, cos, sin)
        tokens = []
        if l == 0:
            h2_ffn = _gather_pass_on("gather_ffn_0", h1_ffn, s['y_mla'])
            tokens.append(h2_ffn[4])
        if not last:
            h1_mix = gather_start(l + 1, "mix", MIXER_W, s['y_mla'])
            h1_ffn = gather_start(l + 1, "ffn", FFN_W, h1_mix[4])
            tokens += [h1_mix[4], h1_ffn[4]]
        x1 = _mixer_fwd_out(s, wm, small[l], tok_of(tokens))
        tokens = []
        wf = dict(zip(FFN_W, _gather_finish(f"gather_ffn_{l}", h2_ffn, x1)))
        if not last:
            h2_mix = _gather_pass_on(f"gather_mix_{l + 1}", h1_mix, x1)
            tokens.append(h2_mix[4])
        xa = _ffn_fwd(x1, wf, small[l], s, tok_of(tokens))
        tokens = []
        if not last:
            h2_ffn = _gather_pass_on(f"gather_ffn_{l + 1}", h1_ffn, xa)
            tokens.append(h2_ffn[4])
        saved.append(s)
        full.append({**wm, **wf})
    gf = g_final.reshape(1, D)
    ones = jnp.ones((T, 1), F32)
    dxa, dgf, loss_rows = _rows_vjp("loss", _loss_fn, [_row(xa)], [gf], [_row(loss_target[0])], [_row(ones)],
                                    primal=True)
    loss = lax.psum(_sum_rows("loss_sum", loss_rows)[0, 0], ("x", "y", "c"))

    bufs = {n: None for n in BIG}
    pending = []

    def advance(dep):
        tokens = []
        for g in pending:
            names, tag = g['names'], g['tag']
            if g['stage'] == 0:
                p4 = [a.reshape((4, 2) + a.shape[1:]) for a in g['gw']]
                gots = [lax.empty((4,) + a.shape[1:], F32) for a in g['gw']]
                g['h'] = _split_start("rs_sibling_start_" + tag, p4, gots, 4 * len(p4), _sibling_plan, dep)
                tokens.append(g['h'][4])
            elif g['stage'] == 1:
                p4, gots = _split_wait("rs_sibling_wait_" + tag, g['h'], 4 * len(names), _sibling_plan, dep)
                s4 = [_pair_sum("rs_pair_sum_" + n, p, q, core) for n, p, q in zip(names, p4, gots)]
                gots = [lax.empty((3,) + a.shape[1:], F32) for a in s4]
                g['h'] = _split_start("rs_chips_start_" + tag, s4, gots, 3 * len(s4), _chips_plan, dep)
                tokens.append(g['h'][4])
            elif g['stage'] == 3:
                s4, gots = _split_wait("rs_chips_wait_" + tag, g['h'], 3 * len(names), _chips_plan, dep)
                for n, s4n, got in zip(names, s4, gots):
                    bufs[n] = _adamw_shard("adamw_" + n, g['layer'], W[n], M[n], V[n], s4n, got, chip, bufs[n])
            g['stage'] += 1
        pending[:] = [g for g in pending if g['stage'] < 4]
        return tokens

    def group(names, l, gw, kind):
        return dict(names=names, layer=l, gw=[gw[n] for n in names], stage=0, tag=f"{kind}_{l}")

    g_small = [None] * DEPTH
    tokens = []
    for l in reversed(range(DEPTH)):
        dx1, gw_f, gs_f = _ffn_bwd(dxa, saved[l], full[l], small[l], tok_of(tokens))
        pending.append(group(FFN_W, l, gw_f, "ffn"))
        tokens = advance(dx1)
        cts, gw_o, gs_o = _mixer_bwd_out(dx1, saved[l], full[l], small[l], tok_of(tokens))
        tokens = advance(cts[0])
        dxa, gw_i, gs_i = _mixer_bwd_in(cts, dx1, saved[l], full[l], small[l], cos, sin, tok_of(tokens))
        pending.append(group(MIXER_W, l, {**gw_o, **gw_i}, "mix"))
        tokens = advance(dxa)
        g_small[l] = {**gs_f, **gs_o, **gs_i}

    flat = [g_small[l][n].reshape(-1) for l in range(DEPTH) for n in SMALL] + [dgf.reshape(-1)]
    n_small = sum(int(f.shape[0]) for f in flat)
    rows = -(-n_small // (PACK_C * SMALL_ROW_ALIGN)) * SMALL_ROW_ALIGN
    flat = jnp.concatenate(flat + [jnp.zeros((rows * PACK_C - n_small,), F32)]).reshape(rows, PACK_C)
    (gathered,) = _all_gather("gather_small", [flat])
    tot = _sum_devices("small_sum", gathered).reshape(-1)
    grads, off = {}, 0
    per_layer = {n: [] for n in SMALL}
    for l in range(DEPTH):
        for n, shp in SMALL.items():
            k = int(np.prod(shp))
            per_layer[n].append(tot[off:off + k].reshape(shp))
            off += k
    for n in SMALL:
        grads[n] = jnp.stack(per_layer[n])
    grads['g_final'] = tot[off:off + D]

    delta, new_m, new_v = {}, {}, {}
    advance(tot)
    for n in PARAMS:
        if n not in BIG:
            delta[n], new_m[n], new_v[n] = _adamw("adamw_" + n, W[n], grads[n], M[n], V[n])
    while pending:
        advance(delta['g_final'])
    for n in BIG:
        grads[n], delta[n], new_m[n], new_v[n] = bufs[n]
    return (loss, dxa[None], *[grads[n] for n in PARAMS], *[delta[n] for n in PARAMS],
            *[new_m[n] for n in PARAMS], *[new_v[n] for n in PARAMS])
```

```python
import jax
import jax.numpy as jnp
import numpy as np
from jax import lax
from jax.experimental import pallas as pl
from jax.experimental.pallas import tpu as pltpu

F32 = jnp.float32
BF16 = jnp.bfloat16

T = 2048
D = 2048
DEPTH = 4
N_DEV = 8
H_MLA, NOPE, ROPE, VDIM = 8, 128, 64, 128
QK = NOPE + ROPE
Q_LORA, KV_LORA = 512, 256
SSM_W, SSM_G, SSM_P, SSM_N = 512, 32, 16, 64
SSM_S = SSM_G * SSM_N
DIL_W, DIL_H, DIL_D = 512, 8, 64
BLK = 128
IN_SPLITS = (Q_LORA, KV_LORA, ROPE, SSM_W, DIL_W, DIL_W, DIL_W)
IN_W = sum(IN_SPLITS)
D_FF = 5632
EPS = 1e-6
ROPE_THETA = 10000.0
MLA_SCALE = QK ** -0.5
DIL_SCALE = DIL_D ** -0.5

ADAM_LR, ADAM_B1, ADAM_B2, ADAM_EPS, ADAM_WD, ADAM_STEP = 0.001, 0.9, 0.999, 1e-08, 0.01, 10

VMEM_LIMIT_V7X = 52 * 1024 * 1024
LANES = 128
PACK_C = 1024
ROW_BLOCK_BYTES = 2 * 1024 * 1024
MM_TM, MM_TN, MM_TK = 512, 512, 2048

NT = (((1,), (1,)), ((), ()))
TN = (((0,), (0,)), ((), ()))
H_QK = (((2,), (2,)), ((0,), (0,)))
H_PV = (((2,), (1,)), ((0,), (0,)))
H_TN = (((1,), (1,)), ((0,), (0,)))
HI = lax.Precision.HIGHEST
MESH = pl.DeviceIdType.MESH

PARAMS = ['g_mix', 'w_in', 'g_q', 'w_uq', 'g_kv', 'w_ukv', 'a_re', 'a_im', 'b_re', 'b_im', 'c_re', 'c_im',
          'd_skip', 'log_dt', 'w_glu', 'b_glu', 'g_out_mla', 'g_out_ssm', 'g_out_dil', 'w_o', 'g_ffn',
          'w_gate', 'w_up', 'w_down', 'g_final']
BIG = {'w_in': 't', 'w_uq': 't', 'w_ukv': 'c', 'w_glu': 'c', 'w_o': 'r', 'w_gate': 't', 'w_up': 't', 'w_down': 'r'}
MIXER_W = ['w_in', 'w_uq', 'w_ukv', 'w_glu', 'w_o']
FFN_W = ['w_gate', 'w_up', 'w_down']
OUT_W, IN_W = ['w_o', 'w_glu'], ['w_in', 'w_uq', 'w_ukv']
SMALL = {'g_mix': (D,), 'g_q': (Q_LORA,), 'g_kv': (KV_LORA,), 'a_re': (SSM_G, SSM_N), 'a_im': (SSM_G, SSM_N),
         'b_re': (SSM_G, SSM_N, SSM_P), 'b_im': (SSM_G, SSM_N, SSM_P), 'c_re': (SSM_G, SSM_P, SSM_N),
         'c_im': (SSM_G, SSM_P, SSM_N), 'd_skip': (SSM_G, SSM_P), 'log_dt': (SSM_G,), 'b_glu': (2 * SSM_W,),
         'g_out_mla': (H_MLA * VDIM,), 'g_out_ssm': (SSM_W,), 'g_out_dil': (DIL_W,), 'g_ffn': (D,)}
SMALL_ROW_ALIGN = 64


def _tile(dim, target, align=LANES):
    best = None
    for t in range(align, min(dim, target) + 1, align):
        if dim % t == 0:
            best = t
    return best if best is not None else dim


def _params(sem=None):
    return pltpu.CompilerParams(dimension_semantics=sem, vmem_limit_bytes=VMEM_LIMIT_V7X)


def _dot(a, b, dims=None, prec=None):
    if dims is None:
        return jnp.dot(a, b, preferred_element_type=F32, precision=prec)
    return lax.dot_general(a, b, dims, preferred_element_type=F32, precision=prec)


def _mm_spec(shape, blk, t_r, t_c, rc):
    if blk is None:
        return pl.BlockSpec((t_r, t_c), rc)
    _, R, C = shape
    if blk == 'r':
        per = R // t_r
        return pl.BlockSpec((1, t_r, t_c), lambda i, j, k: (rc(i, j, k)[0] // per, rc(i, j, k)[0] % per, rc(i, j, k)[1]))
    per = C // t_c
    return pl.BlockSpec((1, t_r, t_c), lambda i, j, k: (rc(i, j, k)[1] // per, rc(i, j, k)[0], rc(i, j, k)[1] % per))


def _logical(shape, blk):
    if blk is None:
        return tuple(shape)
    G, R, C = shape
    return (G * R, C) if blk == 'r' else (R, G * C)


def _mm(name, a, b, mode, ab=None, bb=None, ob=None, res=None, prec=None):
    la, lb = _logical(a.shape, ab), _logical(b.shape, bb)
    am, ak = (0, 1) if mode != 'tn' else (1, 0)
    bk, bn = (0, 1) if mode != 'nt' else (1, 0)
    M, K, N = la[am], la[ak], lb[bn]
    assert lb[bk] == K, (name, a.shape, b.shape, mode)
    if ob is None:
        out_shape = (M, N)
    elif ob == 'r':
        G = N_DEV
        out_shape = (G, M // G, N)
    else:
        G = N_DEV
        out_shape = (G, M, N // G)
    em = min(a.shape[-2:][am], out_shape[-2])
    en = min(b.shape[-2:][bn], out_shape[-1])
    ek = min(a.shape[-2:][ak], b.shape[-2:][bk])
    tn = _tile(en, MM_TN)
    tm = _tile(em, MM_TM)
    dims = {'nn': None, 'nt': NT, 'tn': TN}[mode]
    a_kb = mode != 'tn' and ab == 'c'
    b_kb = (mode == 'nn' and bb == 'r') or (mode == 'nt' and bb == 'c')
    blocks = K // ek if (a_kb or b_kb) else 1
    assert blocks == 1 or ((a_kb or ab is None) and (b_kb or bb is None)), (name, ab, bb, mode)
    tk = ek if blocks > 1 else _tile(ek, MM_TK)
    nk = 1 if blocks > 1 else K // tk

    def val(ref):
        return ref[...] if len(ref.shape) == 2 else ref[0]

    def put(o_ref, r):
        if len(o_ref.shape) == 2:
            o_ref[...] = r
        else:
            o_ref[0] = r

    def k_block(ref, d, blocked, lanes):
        if blocked:
            return ref[d]
        return ref[:, d * ek:(d + 1) * ek] if lanes else ref[d * ek:(d + 1) * ek, :]

    def body(*refs):
        if res is None:
            a_ref, b_ref, o_ref = refs[:3]
            r_ref = None
        else:
            a_ref, b_ref, r_ref, o_ref = refs[:4]
        if blocks > 1:
            part = None
            for d in range(blocks):
                p = _dot(k_block(a_ref, d, a_kb, True), k_block(b_ref, d, b_kb, mode == 'nt'), dims, prec)
                part = p if part is None else part + p
        else:
            part = _dot(val(a_ref), val(b_ref), dims, prec)
        if nk == 1:
            put(o_ref, part if r_ref is None else part + val(r_ref))
            return
        acc_ref = refs[-1]
        k = pl.program_id(2)

        @pl.when(k == 0)
        def _():
            acc_ref[...] = part

        @pl.when((k > 0) & (k < nk - 1))
        def _():
            acc_ref[...] += part

        @pl.when(k == nk - 1)
        def _():
            r = acc_ref[...] + part
            put(o_ref, r if r_ref is None else r + val(r_ref))

    if blocks > 1:
        G = blocks
        a_spec = (pl.BlockSpec((G, tm, ek), lambda i, j, k: (0, i, 0)) if a_kb
                  else pl.BlockSpec((tm, K), lambda i, j, k: (i, 0)))
        if b_kb:
            b_spec = (pl.BlockSpec((G, ek, tn), lambda i, j, k: (0, 0, j)) if mode == 'nn'
                      else pl.BlockSpec((G, tn, ek), lambda i, j, k: (0, j, 0)))
        else:
            b_spec = (pl.BlockSpec((K, tn), lambda i, j, k: (0, j)) if mode == 'nn'
                      else pl.BlockSpec((tn, K), lambda i, j, k: (j, 0)))
    else:
        if mode == 'tn':
            a_spec = _mm_spec(a.shape, ab, tk, tm, lambda i, j, k: (k, i))
        else:
            a_spec = _mm_spec(a.shape, ab, tm, tk, lambda i, j, k: (i, k))
        if mode == 'nt':
            b_spec = _mm_spec(b.shape, bb, tn, tk, lambda i, j, k: (j, k))
        else:
            b_spec = _mm_spec(b.shape, bb, tk, tn, lambda i, j, k: (k, j))
    o_spec = _mm_spec(out_shape, ob, tm, tn, lambda i, j, k: (i, j))
    in_specs = [a_spec, b_spec] + ([o_spec] if res is not None else [])
    args = (a, b) + ((res,) if res is not None else ())
    return pl.pallas_call(
        body, name=name, out_shape=jax.ShapeDtypeStruct(out_shape, F32),
        grid=(M // tm, N // tn, nk), in_specs=in_specs, out_specs=o_spec,
        scratch_shapes=[pltpu.VMEM((tm, tn), F32)] if nk > 1 else [],
        compiler_params=_params(("parallel", "parallel", "arbitrary")),
    )(*args)


def _row(a, tr=None, axis=-2):
    axis = axis % a.ndim
    n = a.shape[axis]
    if tr is None:
        row_bytes = a.size // n * 4
        tr = _tile(n, max(8, min(256, ROW_BLOCK_BYTES // row_bytes)), 8)
    return (a, axis, tr)


def _row_spec(shape, axis, tr):
    nd = len(shape)
    blk = tuple(tr if d == axis else s for d, s in enumerate(shape))
    return pl.BlockSpec(blk, lambda i: tuple(i if d == axis else 0 for d in range(nd)))


def _full_spec(shape):
    nd = len(shape)
    return pl.BlockSpec(tuple(shape), lambda i: (0,) * nd)


def _steps(entries):
    ns = {a.shape[ax] // tr for a, ax, tr in entries}
    assert len(ns) == 1, [(a.shape, ax, tr) for a, ax, tr in entries]
    return ns.pop()


def _as_tuple(r):
    return tuple(r) if isinstance(r, (tuple, list)) else (r,)


def _rows_fwd(name, fn, rows, bcast, outs):
    steps = _steps(rows)
    nr, nb = len(rows), len(bcast)

    def body(*refs):
        vals = [r[...] for r in refs[:nr + nb]]
        res = _as_tuple(fn(*vals))
        for o_ref, r in zip(refs[nr + nb:], res):
            o_ref[...] = r.astype(o_ref.dtype)

    in_specs = [_row_spec(a.shape, ax, tr) for a, ax, tr in rows] + [_full_spec(b.shape) for b in bcast]
    out_specs = [_row_spec(s, ax % len(s), tr) for s, _, ax, tr in outs]
    res = pl.pallas_call(
        body, name=name, out_shape=[jax.ShapeDtypeStruct(s, dt) for s, dt, _, _ in outs],
        grid=(steps,), in_specs=in_specs, out_specs=out_specs,
        compiler_params=_params(("parallel",)),
    )(*[a for a, _, _ in rows], *bcast)
    return res


def _rows_vjp(name, fn, drows, dbc, arows, cts, primal=False, grad_dtypes=None):
    entries = list(drows) + list(arows) + list(cts)
    steps = _steps(entries)
    ndr, ndb, nar, nct = len(drows), len(dbc), len(arows), len(cts)
    gdt = list(grad_dtypes) if grad_dtypes is not None else [F32] * ndr

    def body(*refs):
        p = 0
        dr = [r[...] for r in refs[p:p + ndr]]; p += ndr
        db = [r[...] for r in refs[p:p + ndb]]; p += ndb
        ar = [r[...] for r in refs[p:p + nar]]; p += nar
        ct = [r[...] for r in refs[p:p + nct]]; p += nct
        g_rows = refs[p:p + ndr]; p += ndr
        g_bc = refs[p:p + ndb]; p += ndb
        prim_refs = refs[p:]

        def f(*d):
            return _as_tuple(fn(*d, *ar))

        outs, pullback = jax.vjp(f, *dr, *db)
        grads = pullback(tuple(c.astype(o.dtype) for c, o in zip(ct, outs)))
        for k in range(ndr):
            g_rows[k][...] = grads[k].astype(g_rows[k].dtype)
        if ndb:
            @pl.when(pl.program_id(0) == 0)
            def _():
                for r in g_bc:
                    r[...] = jnp.zeros_like(r)
            for k in range(ndb):
                g_bc[k][...] += grads[ndr + k]
        for r, o in zip(prim_refs, outs):
            r[...] = o.astype(r.dtype)

    in_specs = ([_row_spec(a.shape, ax, tr) for a, ax, tr in drows] + [_full_spec(b.shape) for b in dbc]
                + [_row_spec(a.shape, ax, tr) for a, ax, tr in arows]
                + [_row_spec(a.shape, ax, tr) for a, ax, tr in cts])
    out_shape = ([jax.ShapeDtypeStruct(a.shape, dt) for (a, _, _), dt in zip(drows, gdt)]
                 + [jax.ShapeDtypeStruct(b.shape, F32) for b in dbc])
    out_specs = ([_row_spec(a.shape, ax, tr) for a, ax, tr in drows] + [_full_spec(b.shape) for b in dbc])
    if primal:
        out_shape += [jax.ShapeDtypeStruct(a.shape, F32) for a, _, _ in cts]
        out_specs += [_row_spec(a.shape, ax, tr) for a, ax, tr in cts]
    return pl.pallas_call(
        body, name=name, out_shape=out_shape, grid=(steps,), in_specs=in_specs, out_specs=out_specs,
        compiler_params=_params(("arbitrary",)),
    )(*[a for a, _, _ in drows], *dbc, *[a for a, _, _ in arows], *[a for a, _, _ in cts])


def _rms_fn(x, g):
    return x * lax.rsqrt(jnp.mean(x * x, axis=-1, keepdims=True) + EPS) * g


def _rms_res_fn(x, g):
    return _rms_fn(x, g), x


def _rope_fn(q1, q2, k1, k2, cos, sin):
    return q1 * cos - q2 * sin, q2 * cos + q1 * sin, k1 * cos - k2 * sin, k2 * cos + k1 * sin


def _s5_act_fn(ymm, u, d):
    return jax.nn.gelu(ymm + d * u)


def _glu_fn(z1, z2, b1, b2):
    return (z1 + b1) * jax.nn.sigmoid(z2 + b2)


def _outnorm_fn(ym, ys, yd, gm, gs, gd):
    return jnp.concatenate([_rms_fn(ym, gm), _rms_fn(ys, gs), _rms_fn(yd, gd)], axis=-1)


def _swiglu_fn(a, b):
    return jax.nn.silu(a) * b


def _loss_fn(x, g, tgt):
    err = _rms_fn(x, g) - tgt
    return 0.5 * jnp.mean(err * err, axis=-1, keepdims=True)


def _dil_mix_fn(o0, o1, o2, l0, l1, l2):
    m = jnp.maximum(jnp.maximum(l0, l1), l2)
    e0, e1, e2 = jnp.exp(l0 - m), jnp.exp(l1 - m), jnp.exp(l2 - m)
    s = e0 + e1 + e2
    return (e0 / s) * o0 + (e1 / s) * o1 + (e2 / s) * o2


def _s5_disc_fn(a_re, a_im, ldt, b_r, b_i):
    lr = jnp.minimum(a_re.reshape(1, SSM_N), -1e-4)
    li = a_im.reshape(1, SSM_N)
    dt = jnp.exp(ldt.reshape(1, 1))
    e = jnp.exp(lr * dt)
    ar = e * jnp.cos(li * dt)
    ai = e * jnp.sin(li * dt)
    nr, ni = ar - 1.0, ai
    den = lr * lr + li * li
    cr = (nr * lr + ni * li) / den
    ci = (ni * lr - nr * li) / den
    return ar.reshape(1, 1, SSM_N), ai.reshape(1, 1, SSM_N), cr * b_r - ci * b_i, cr * b_i + ci * b_r


def _adamw_fn(w, g, m, v):
    m = ADAM_B1 * m + (1.0 - ADAM_B1) * g
    v = ADAM_B2 * v + (1.0 - ADAM_B2) * jnp.square(g)
    m_hat = m / (1.0 - ADAM_B1 ** ADAM_STEP)
    v_hat = v / (1.0 - ADAM_B2 ** ADAM_STEP)
    delta = -ADAM_LR * (m_hat / (jnp.sqrt(v_hat) + ADAM_EPS) + ADAM_WD * w)
    return delta, m, v


def _rms_fwd(name, x, g):
    (h,) = _rows_fwd(name, _rms_fn, [_row(x)], [g], [(x.shape, BF16, -2, _row(x)[2])])
    return h


def _rms_bwd(name, x, g, dh, dres=None):
    if dres is None:
        dx, dg = _rows_vjp(name, _rms_fn, [_row(x)], [g], [], [_row(dh)])
    else:
        dx, dg = _rows_vjp(name, _rms_res_fn, [_row(x)], [g], [], [_row(dh), _row(dres)])
    return dx, dg


MLA_TQ = 256


def _causal_block(t):
    return lax.broadcasted_iota(jnp.int32, (t, t), 0) >= lax.broadcasted_iota(jnp.int32, (t, t), 1)


def _mla_fwd(q, k, v):
    tq = MLA_TQ

    def body(q_ref, k_ref, v_ref, o_ref, lse_ref):
        i = pl.program_id(1)
        q = q_ref[0]

        def key_block(kb, carry, diagonal):
            m, l, acc = carry
            ks = pl.multiple_of(kb * tq, tq)
            s = _dot(q, k_ref[0, pl.ds(ks, tq), :], NT) * MLA_SCALE
            if diagonal:
                s = jnp.where(_causal_block(tq), s, -jnp.inf)
            m_new = jnp.maximum(m, jnp.max(s, axis=-1, keepdims=True))
            alpha = jnp.exp(m - m_new)
            p = jnp.exp(s - m_new)
            l = alpha * l + jnp.sum(p, axis=-1, keepdims=True)
            acc = alpha * acc + _dot(p.astype(BF16), v_ref[0, pl.ds(ks, tq), :])
            return m_new, l, acc

        init = (jnp.full((tq, 1), -jnp.inf, F32), jnp.zeros((tq, 1), F32), jnp.zeros((tq, VDIM), F32))
        carry = lax.fori_loop(0, i, lambda kb, c: key_block(kb, c, False), init)
        m, l, acc = key_block(i, carry, True)
        o_ref[...] = acc / l
        lse_ref[0] = m + jnp.log(l)

    return pl.pallas_call(
        body, name="mla_fwd",
        out_shape=[jax.ShapeDtypeStruct((T, H_MLA * VDIM), F32), jax.ShapeDtypeStruct((H_MLA, T, 1), F32)],
        grid=(H_MLA, T // tq),
        in_specs=[pl.BlockSpec((1, tq, QK), lambda h, i: (h, i, 0)),
                  pl.BlockSpec((1, T, QK), lambda h, i: (h, 0, 0)),
                  pl.BlockSpec((1, T, VDIM), lambda h, i: (h, 0, 0))],
        out_specs=[pl.BlockSpec((tq, VDIM), lambda h, i: (i, h)),
                   pl.BlockSpec((1, tq, 1), lambda h, i: (h, i, 0))],
        compiler_params=_params(("parallel", "parallel")),
    )(q, k, v)


def _mla_bwd(q, k, v, o, do, lse):
    tq = MLA_TQ

    def body(q_ref, k_ref, v_ref, o_ref, do_ref, lse_ref, dq_ref, dk_ref, dv_ref, dkpe_ref):
        h, i = pl.program_id(0), pl.program_id(1)

        @pl.when(i == 0)
        def _():
            dk_ref[...] = jnp.zeros_like(dk_ref)
            dv_ref[...] = jnp.zeros_like(dv_ref)

        @pl.when((i == 0) & (h == 0))
        def _():
            dkpe_ref[...] = jnp.zeros_like(dkpe_ref)

        q, lse = q_ref[0], lse_ref[0]
        delta = jnp.sum(do_ref[...] * o_ref[...], axis=-1, keepdims=True)
        do = do_ref[...].astype(BF16)

        def key_block(kb, dq, diagonal):
            ks = pl.multiple_of(kb * tq, tq)
            kblk, vblk = k_ref[0, pl.ds(ks, tq), :], v_ref[0, pl.ds(ks, tq), :]
            p = jnp.exp(_dot(q, kblk, NT) * MLA_SCALE - lse)
            if diagonal:
                p = jnp.where(_causal_block(tq), p, 0.0)
            ds = (p * (_dot(do, vblk, NT) - delta) * MLA_SCALE).astype(BF16)
            dk = _dot(ds, q, TN)
            dk_ref[0, pl.ds(ks, tq), :] += dk
            dkpe_ref[pl.ds(ks, tq), :] += dk[:, NOPE:]
            dv_ref[0, pl.ds(ks, tq), :] += _dot(p.astype(BF16), do, TN)
            return dq + _dot(ds, kblk)

        dq = lax.fori_loop(0, i, lambda kb, c: key_block(kb, c, False), jnp.zeros((tq, QK), F32))
        dq_ref[0] = key_block(i, dq, True)

    return pl.pallas_call(
        body, name="mla_bwd",
        out_shape=[jax.ShapeDtypeStruct((H_MLA, T, QK), F32), jax.ShapeDtypeStruct((H_MLA, T, QK), F32),
                   jax.ShapeDtypeStruct((H_MLA, T, VDIM), F32), jax.ShapeDtypeStruct((T, ROPE), F32)],
        grid=(H_MLA, T // tq),
        in_specs=[pl.BlockSpec((1, tq, QK), lambda h, i: (h, i, 0)),
                  pl.BlockSpec((1, T, QK), lambda h, i: (h, 0, 0)),
                  pl.BlockSpec((1, T, VDIM), lambda h, i: (h, 0, 0)),
                  pl.BlockSpec((tq, VDIM), lambda h, i: (i, h)),
                  pl.BlockSpec((tq, VDIM), lambda h, i: (i, h)),
                  pl.BlockSpec((1, tq, 1), lambda h, i: (h, i, 0))],
        out_specs=[pl.BlockSpec((1, tq, QK), lambda h, i: (h, i, 0)),
                   pl.BlockSpec((1, T, QK), lambda h, i: (h, 0, 0)),
                   pl.BlockSpec((1, T, VDIM), lambda h, i: (h, 0, 0)),
                   pl.BlockSpec((T, ROPE), lambda h, i: (0, 0))],
        compiler_params=_params(("arbitrary", "arbitrary")),
    )(q, k, v, o, do, lse)


NBLK = T // BLK


def _band_masks():
    r = lax.broadcasted_iota(jnp.int32, (BLK, BLK), 0)
    j = lax.broadcasted_iota(jnp.int32, (BLK, BLK), 1)
    return j <= r, j >= r


def _seq_start(p, i):
    per_seq = lax.shift_right_logical(jnp.int32(NBLK), 2 * p)
    return lax.rem(i, per_seq) == 0


def _band_fwd(q, k, v):
    def body(q_ref, kp_ref, kc_ref, vp_ref, vc_ref, o_ref, lse_ref):
        p, i = pl.program_id(0), pl.program_id(1)
        has_prev = jnp.logical_not(_seq_start(p, i))
        m_cur, m_prev = _band_masks()
        m_cur, m_prev = m_cur[None], (m_prev & has_prev)[None]
        q = q_ref[0]
        s_c = jnp.where(m_cur, _dot(q, kc_ref[0], H_QK) * DIL_SCALE, -jnp.inf)
        s_p = jnp.where(m_prev, _dot(q, kp_ref[0], H_QK) * DIL_SCALE, -jnp.inf)
        m = jnp.maximum(jnp.max(s_c, axis=-1, keepdims=True), jnp.max(s_p, axis=-1, keepdims=True))
        e_c, e_p = jnp.exp(s_c - m), jnp.exp(s_p - m)
        l = jnp.sum(e_c, axis=-1, keepdims=True) + jnp.sum(e_p, axis=-1, keepdims=True)
        o_ref[0] = _dot((e_p / l).astype(BF16), vp_ref[0], H_PV) + _dot((e_c / l).astype(BF16), vc_ref[0], H_PV)
        lse_ref[0] = m + jnp.log(l)

    blk = lambda w: (1, DIL_H, BLK, w)
    cur = lambda p, i: (p, 0, i, 0)
    prev = lambda p, i: (p, 0, jnp.maximum(i - 1, 0), 0)
    return pl.pallas_call(
        body, name="band_fwd",
        out_shape=[jax.ShapeDtypeStruct((3, DIL_H, T, DIL_D), F32), jax.ShapeDtypeStruct((3, DIL_H, T, 1), F32)],
        grid=(3, NBLK),
        in_specs=[pl.BlockSpec(blk(DIL_D), cur), pl.BlockSpec(blk(DIL_D), prev), pl.BlockSpec(blk(DIL_D), cur),
                  pl.BlockSpec(blk(DIL_D), prev), pl.BlockSpec(blk(DIL_D), cur)],
        out_specs=[pl.BlockSpec(blk(DIL_D), cur), pl.BlockSpec(blk(1), cur)],
        compiler_params=_params(("parallel", "parallel")),
    )(q, k, k, v, v)


def _band_bwd(q, k, v, o, lse, do, dlse):
    def body(qc_ref, qn_ref, kp_ref, kc_ref, vp_ref, vc_ref, oc_ref, on_ref, lc_ref, ln_ref,
             doc_ref, don_ref, dlc_ref, dln_ref, dq_ref, dk_ref, dv_ref):
        p, i = pl.program_id(0), pl.program_id(1)
        has_prev = jnp.logical_not(_seq_start(p, i))
        has_next = jnp.logical_not(_seq_start(p, i + 1)) & (i + 1 < NBLK)
        m_cur, m_prev = _band_masks()

        def probs(q, k, lse, mask):
            return jnp.where(mask[None], jnp.exp(_dot(q, k, H_QK) * DIL_SCALE - lse), 0.0)

        def dscore(pr, do, v, shift):
            return (pr * (_dot(do, v, H_QK) + shift) * DIL_SCALE).astype(BF16)

        qc, qn, kp, kc, vp, vc = qc_ref[0], qn_ref[0], kp_ref[0], kc_ref[0], vp_ref[0], vc_ref[0]
        doc, don = doc_ref[0], don_ref[0]
        sh_c = dlc_ref[0] - jnp.sum(doc * oc_ref[0], axis=-1, keepdims=True)
        sh_n = dln_ref[0] - jnp.sum(don * on_ref[0], axis=-1, keepdims=True)
        doc, don = doc.astype(BF16), don.astype(BF16)
        p_cc = probs(qc, kc, lc_ref[0], m_cur)
        p_cp = probs(qc, kp, lc_ref[0], m_prev & has_prev)
        p_nc = probs(qn, kc, ln_ref[0], m_prev & has_next)
        ds_cc = dscore(p_cc, doc, vc, sh_c)
        ds_cp = dscore(p_cp, doc, vp, sh_c)
        ds_nc = dscore(p_nc, don, vc, sh_n)
        dq_ref[0] = _dot(ds_cc, kc, H_PV) + _dot(ds_cp, kp, H_PV)
        dk_ref[0] = _dot(ds_cc, qc, H_TN) + _dot(ds_nc, qn, H_TN)
        dv_ref[0] = _dot(p_cc.astype(BF16), doc, H_TN) + _dot(p_nc.astype(BF16), don, H_TN)

    blk = lambda w: (1, DIL_H, BLK, w)
    cur = lambda p, i: (p, 0, i, 0)
    prev = lambda p, i: (p, 0, jnp.maximum(i - 1, 0), 0)
    nxt = lambda p, i: (p, 0, jnp.minimum(i + 1, NBLK - 1), 0)
    w, one = pl.BlockSpec(blk(DIL_D), cur), pl.BlockSpec(blk(1), cur)
    wn, onen = pl.BlockSpec(blk(DIL_D), nxt), pl.BlockSpec(blk(1), nxt)
    wp = pl.BlockSpec(blk(DIL_D), prev)
    return pl.pallas_call(
        body, name="band_bwd",
        out_shape=[jax.ShapeDtypeStruct((3, DIL_H, T, DIL_D), F32)] * 3,
        grid=(3, NBLK),
        in_specs=[w, wn, wp, w, wp, w, w, wn, one, onen, w, wn, one, onen],
        out_specs=[w, w, w],
        compiler_params=_params(("parallel", "parallel")),
    )(q, q, k, k, v, v, o, o, lse, lse, do, do, dlse, dlse)


SCAN_TC = 256


def _scan_fwd(bu, ar, ai):
    tc, S = SCAN_TC, SSM_S

    def body(bu_ref, ar_ref, ai_ref, h_ref, h16_ref, cr_ref, ci_ref):
        @pl.when(pl.program_id(0) == 0)
        def _():
            cr_ref[...] = jnp.zeros_like(cr_ref)
            ci_ref[...] = jnp.zeros_like(ci_ref)

        a_r, a_i = ar_ref[...], ai_ref[...]

        def step(j, carry):
            hr, hi = carry
            for r in range(8):
                t = pl.multiple_of(j * 8, 8) + r
                br = bu_ref[pl.ds(t, 1), pl.ds(0, S)]
                bi = bu_ref[pl.ds(t, 1), pl.ds(S, S)]
                hr, hi = a_r * hr - a_i * hi + br, a_r * hi + a_i * hr + bi
                h_ref[pl.ds(t, 1), pl.ds(0, S)] = hr
                h_ref[pl.ds(t, 1), pl.ds(S, S)] = hi
            return hr, hi

        hr, hi = lax.fori_loop(0, tc // 8, step, (cr_ref[...], ci_ref[...]))
        cr_ref[...] = hr
        ci_ref[...] = hi
        h16_ref[...] = h_ref[...].astype(BF16)

    return pl.pallas_call(
        body, name="s5_scan_fwd",
        out_shape=[jax.ShapeDtypeStruct((T, 2 * S), F32), jax.ShapeDtypeStruct((T, 2 * S), BF16)],
        grid=(T // tc,),
        in_specs=[pl.BlockSpec((tc, 2 * S), lambda i: (i, 0)), _full_spec((1, S)), _full_spec((1, S))],
        out_specs=[pl.BlockSpec((tc, 2 * S), lambda i: (i, 0)), pl.BlockSpec((tc, 2 * S), lambda i: (i, 0))],
        scratch_shapes=[pltpu.VMEM((1, S), F32), pltpu.VMEM((1, S), F32)],
        compiler_params=_params(("arbitrary",)),
    )(bu, ar, ai)


def _scan_bwd(dh, h, ar, ai):
    tc, S = SCAN_TC, SSM_S
    nc = T // tc

    def body(dh_ref, h_ref, hp_ref, ar_ref, ai_ref, g16_ref, dar_ref, dai_ref, cr_ref, ci_ref, g_ref):
        i = pl.program_id(0)

        @pl.when(i == 0)
        def _():
            cr_ref[...] = jnp.zeros_like(cr_ref)
            ci_ref[...] = jnp.zeros_like(ci_ref)
            dar_ref[...] = jnp.zeros_like(dar_ref)
            dai_ref[...] = jnp.zeros_like(dai_ref)

        a_r, a_i = ar_ref[...], ai_ref[...]
        first_chunk = (i == nc - 1)
        edge = jnp.where(first_chunk, 0.0, 1.0)
        hpr = hp_ref[pl.ds(7, 1), pl.ds(0, S)] * edge
        hpi = hp_ref[pl.ds(7, 1), pl.ds(S, S)] * edge

        def step(jj, carry):
            gr, gi, dar, dai = carry
            j = tc // 8 - 1 - jj
            for r in range(7, -1, -1):
                t = pl.multiple_of(j * 8, 8) + r
                tp = jnp.maximum(t - 1, 0)
                inside = t > 0
                pr = jnp.where(inside, h_ref[pl.ds(tp, 1), pl.ds(0, S)], hpr)
                pi = jnp.where(inside, h_ref[pl.ds(tp, 1), pl.ds(S, S)], hpi)
                gr, gi = (dh_ref[pl.ds(t, 1), pl.ds(0, S)] + a_r * gr + a_i * gi,
                          dh_ref[pl.ds(t, 1), pl.ds(S, S)] + a_r * gi - a_i * gr)
                g_ref[pl.ds(t, 1), pl.ds(0, S)] = gr
                g_ref[pl.ds(t, 1), pl.ds(S, S)] = gi
                dar = dar + gr * pr + gi * pi
                dai = dai + gi * pr - gr * pi
            return gr, gi, dar, dai

        zero = jnp.zeros((1, S), F32)
        gr, gi, dar, dai = lax.fori_loop(0, tc // 8, step, (cr_ref[...], ci_ref[...], zero, zero))
        cr_ref[...] = gr
        ci_ref[...] = gi
        dar_ref[...] += dar
        dai_ref[...] += dai
        g16_ref[...] = g_ref[...].astype(BF16)

    rev = lambda i: (nc - 1 - i, 0)
    before = lambda i: (jnp.maximum((nc - 1 - i) * (tc // 8) - 1, 0), 0)
    return pl.pallas_call(
        body, name="s5_scan_bwd",
        out_shape=[jax.ShapeDtypeStruct((T, 2 * S), BF16), jax.ShapeDtypeStruct((1, S), F32),
                   jax.ShapeDtypeStruct((1, S), F32)],
        grid=(nc,),
        in_specs=[pl.BlockSpec((tc, 2 * S), rev), pl.BlockSpec((tc, 2 * S), rev), pl.BlockSpec((8, 2 * S), before),
                  _full_spec((1, S)), _full_spec((1, S))],
        out_specs=[pl.BlockSpec((tc, 2 * S), rev), _full_spec((1, S)), _full_spec((1, S))],
        scratch_shapes=[pltpu.VMEM((1, S), F32), pltpu.VMEM((1, S), F32), pltpu.VMEM((tc, 2 * S), F32)],
        compiler_params=_params(("arbitrary",)),
    )(dh, h, h, ar, ai)


def _sum_rows(name, x):
    def body(x_ref, o_ref):
        o_ref[...] = jnp.sum(x_ref[...], axis=0, keepdims=True)

    return pl.pallas_call(body, name=name, out_shape=jax.ShapeDtypeStruct((1, 1), F32),
                          in_specs=[_full_spec(x.shape)], out_specs=_full_spec((1, 1)), grid=(1,))(x)


ANY = pl.BlockSpec(memory_space=pl.ANY)


def _place():
    return lax.axis_index("x"), lax.axis_index("y"), lax.axis_index("c")


def _all_gather(name, shards):
    n = len(shards)

    def body(*refs):
        xs, outs = refs[:n], refs[n:2 * n]
        send_sems, recv_sems, local_sems = refs[2 * n:]
        x, y, c = _place()
        me, sibling = (x, y, c), (x, y, 1 - c)
        chips = [(1 - x, y), (x, 1 - y), (1 - x, 1 - y)]

        def slot(t, px, py, pc):
            return outs[t].at[4 * px + 2 * py + pc]

        def copy(t, k, block, to, src=None):
            return pltpu.make_async_remote_copy(
                src_ref=slot(t, *block) if src is None else src, dst_ref=slot(t, *block),
                send_sem=send_sems.at[7 * t + k], recv_sem=recv_sems.at[7 * t + k], device_id=to, device_id_type=MESH)

        mine = [pltpu.make_async_copy(xs[t], slot(t, *me), local_sems.at[t]) for t in range(n)]
        sent = []
        for t in range(n):
            mine[t].start()
            sent.append(copy(t, 0, me, sibling, src=xs[t]))
            sent += [copy(t, 1 + j, me, (*chip, c), src=xs[t]) for j, chip in enumerate(chips)]
        for cp in sent:
            cp.start()
        for j, chip in enumerate(chips):
            for t in range(n):
                copy(t, 1 + j, (*chip, c), me).wait_recv()
                fwd = copy(t, 4 + j, (*chip, c), sibling)
                fwd.start()
                sent.append(fwd)
        for t in range(n):
            copy(t, 0, sibling, me).wait_recv()
            for j, chip in enumerate(chips):
                copy(t, 4 + j, (*chip, 1 - c), me).wait_recv()
        for cp in sent:
            cp.wait_send()
        for cp in mine:
            cp.wait()

    return pl.pallas_call(
        body, name=name, out_shape=[jax.ShapeDtypeStruct((N_DEV,) + s.shape, s.dtype) for s in shards],
        in_specs=[ANY] * n, out_specs=[ANY] * n,
        scratch_shapes=[pltpu.SemaphoreType.DMA((7 * n,)), pltpu.SemaphoreType.DMA((7 * n,)),
                        pltpu.SemaphoreType.DMA((n,))],
    )(*shards)


HBM = pl.BlockSpec(memory_space=pltpu.HBM)
SEM = pl.BlockSpec(memory_space=pltpu.SEMAPHORE)
DATAFLOW = pltpu.SideEffectType.DATAFLOW_SIDE_EFFECTING


def _hbm(a):
    return pltpu.with_memory_space_constraint(a, pltpu.HBM)


def _split_start(name, srcs, lands, ncopies, plan, after):
    ns, nl = len(srcs), len(lands)

    def body(*refs):
        send_sems, recv_sems = refs[ns + nl + 1], refs[ns + nl + 2]
        token = refs[-1]
        for k, (src, dst, peer, _) in enumerate(plan(refs[:ns], refs[ns:ns + nl])):
            pltpu.make_async_remote_copy(src_ref=src, dst_ref=dst, send_sem=send_sems.at[k], recv_sem=recv_sems.at[k],
                                         device_id=peer, device_id_type=MESH).start()
        token[...] = jnp.zeros_like(token)

    out = pl.pallas_call(
        body, name=name,
        out_shape=(pltpu.SemaphoreType.DMA((ncopies,)), pltpu.SemaphoreType.DMA((ncopies,)),
                   *[pltpu.HBM(a.shape, a.dtype) for a in srcs], *[pltpu.HBM(a.shape, a.dtype) for a in lands],
                   jax.ShapeDtypeStruct((8, LANES), F32)),
        in_specs=[HBM] * (ns + nl) + [ANY],
        out_specs=(SEM, SEM, *[HBM] * (ns + nl), pl.BlockSpec(memory_space=pltpu.VMEM)),
        input_output_aliases={i: 2 + i for i in range(ns + nl)},
        compiler_params=pltpu.CompilerParams(has_side_effects=DATAFLOW),
    )(*[_hbm(a) for a in srcs], *[_hbm(a) for a in lands], after)
    return out[0], out[1], list(out[2:2 + ns]), list(out[2 + ns:2 + ns + nl]), out[-1]


def _split_wait(name, handle, ncopies, plan, after):
    send_sems, recv_sems, srcs, lands, _ = handle
    ns, nl = len(srcs), len(lands)

    def body(*refs):
        s_sems, r_sems = refs[ns + nl], refs[ns + nl + 1]
        for k, (src, dst, peer, mine) in enumerate(plan(refs[:ns], refs[ns:ns + nl])):
            pltpu.make_async_remote_copy(src_ref=src, dst_ref=dst, send_sem=s_sems.at[k], recv_sem=r_sems.at[k],
                                         device_id=peer, device_id_type=MESH).wait_send()
            pltpu.make_async_remote_copy(src_ref=src, dst_ref=mine, send_sem=s_sems.at[k], recv_sem=r_sems.at[k],
                                         device_id=peer, device_id_type=MESH).wait_recv()

    out = pl.pallas_call(
        body, name=name,
        out_shape=(*[pltpu.HBM(a.shape, a.dtype) for a in srcs], *[pltpu.HBM(a.shape, a.dtype) for a in lands]),
        in_specs=[HBM] * (ns + nl) + [SEM, SEM, ANY],
        out_specs=tuple([HBM] * (ns + nl)),
        input_output_aliases={i: i for i in range(ns + nl)},
        compiler_params=pltpu.CompilerParams(has_side_effects=DATAFLOW),
    )(*srcs, *lands, send_sems, recv_sems, after)
    return list(out[:ns]), list(out[ns:])


def _slot(px, py, pc):
    return 4 * px + 2 * py + pc


def _gather_plan(xs, lands):
    x, y, c = _place()
    peers = [(x, y, 1 - c), (1 - x, y, c), (x, 1 - y, c), (1 - x, 1 - y, c)]
    return [(xs[t], lands[t].at[_slot(x, y, c)], peer, lands[t].at[_slot(*peer)])
            for t in range(len(xs)) for peer in peers]


def _gather_start(name, shards, after):
    lands = [lax.empty((N_DEV,) + s.shape, s.dtype) for s in shards]
    return _split_start(name, shards, lands, 4 * len(shards), _gather_plan, after)


def _pass_on_plan(_, lands):
    x, y, c = _place()
    blocks = [((1 - x, y, c), (1 - x, y, 1 - c)), ((x, 1 - y, c), (x, 1 - y, 1 - c)),
              ((1 - x, 1 - y, c), (1 - x, 1 - y, 1 - c)), ((x, y, 1 - c), (x, y, c))]
    return [(lands[t].at[_slot(*out)], lands[t].at[_slot(*out)], (x, y, 1 - c), lands[t].at[_slot(*back)])
            for t in range(len(lands)) for out, back in blocks]


def _gather_pass_on(name, handle, after):
    n = len(handle[2])
    _, lands = _split_wait(name + "_wait", handle, 4 * n, _gather_plan, after)
    return _split_start(name + "_pass_on", [], lands, 4 * n, _pass_on_plan, after)


def _gather_finish(name, handle, after):
    n = len(handle[3])
    _, lands = _split_wait(name + "_done", handle, 4 * n, _pass_on_plan, after)
    return lands


def _sibling_plan(ps, gots):
    x, y, c = _place()
    return [(ps[t].at[j, 1 - c], gots[t].at[j], (x, y, 1 - c), gots[t].at[j]) for t in range(len(ps)) for j in range(4)]


def _chips_plan(ss, gots):
    x, y, c = _place()
    chips = [(1 - x, y), (x, 1 - y), (1 - x, 1 - y)]
    return [(ss[t].at[2 * px + py], gots[t].at[k], (px, py, c), gots[t].at[k])
            for t in range(len(ss)) for k, (px, py) in enumerate(chips)]


def _sum_tile(R, C):
    return _tile(R, max(8, min(1024, ROW_BLOCK_BYTES // (C * 4))), 8)


def _pair_sum(name, p4, got, core):
    _, _, R, C = p4.shape
    tr = _sum_tile(R, C)

    def body(core_ref, p_ref, g_ref, o_ref):
        o_ref[...] = p_ref[:, 0] + g_ref[...]

    return pl.pallas_call(
        body, name=name, out_shape=jax.ShapeDtypeStruct((4, R, C), F32),
        grid_spec=pltpu.PrefetchScalarGridSpec(
            num_scalar_prefetch=1, grid=(4, R // tr),
            in_specs=[pl.BlockSpec((1, 1, tr, C), lambda j, i, core: (j, core[0], i, 0)),
                      pl.BlockSpec((1, tr, C), lambda j, i, core: (j, i, 0))],
            out_specs=pl.BlockSpec((1, tr, C), lambda j, i, core: (j, i, 0))),
        compiler_params=_params(("parallel", "parallel")),
    )(core, p4, got)


def _sum_devices(name, g8):
    _, R, C = g8.shape
    tr = _tile(R, SMALL_ROW_ALIGN, 8)

    def body(g_ref, o_ref):
        acc = g_ref[0]
        for d in range(1, N_DEV):
            acc = acc + g_ref[d]
        o_ref[...] = acc

    return pl.pallas_call(
        body, name=name, out_shape=jax.ShapeDtypeStruct((R, C), F32), grid=(R // tr,),
        in_specs=[pl.BlockSpec((N_DEV, tr, C), lambda i: (0, i, 0))], out_specs=pl.BlockSpec((tr, C), lambda i: (i, 0)),
        compiler_params=_params(("parallel",)),
    )(g8)


def _adamw_shard(name, layer, w, m, v, s4, got, chip, bufs):
    _, R, C = w.shape
    tr = _sum_tile(R, C) // 2 if _sum_tile(R, C) % 16 == 0 else _sum_tile(R, C)

    def body(chip_ref, w_ref, m_ref, v_ref, s_ref, g_ref, b0, b1, b2, b3, og, od, om, ov):
        g = ((s_ref[0] + g_ref[0]) + g_ref[1]) + g_ref[2]
        d, nm, nv = _adamw_fn(w_ref[0], g, m_ref[0], v_ref[0])
        og[0], od[0], om[0], ov[0] = g, d, nm, nv

    lay = pl.BlockSpec((1, tr, C), lambda i, chip: (layer, i, 0))
    if bufs is None:
        bufs = [lax.empty(w.shape, F32) for _ in range(4)]
    return pl.pallas_call(
        body, name=name, out_shape=[jax.ShapeDtypeStruct(w.shape, F32)] * 4,
        grid_spec=pltpu.PrefetchScalarGridSpec(
            num_scalar_prefetch=1, grid=(R // tr,),
            in_specs=[lay, lay, lay, pl.BlockSpec((1, tr, C), lambda i, chip: (chip[0], i, 0)),
                      pl.BlockSpec((3, tr, C), lambda i, chip: (0, i, 0)), ANY, ANY, ANY, ANY],
            out_specs=[lay] * 4),
        input_output_aliases={6: 0, 7: 1, 8: 2, 9: 3},
        compiler_params=_params(("parallel",)),
    )(chip, w, m, v, s4, got, *bufs)


def _adamw(name, wt, g, m, v):
    shape = wt.shape
    two = (lambda a: a.reshape(1, -1)) if wt.ndim == 1 else (lambda a: a.reshape(-1, shape[-1]))
    w2, g2, m2, v2 = two(wt), two(g), two(m), two(v)
    tr = _row(w2, None, 0)[2]
    outs = [(w2.shape, F32, 0, tr)] * 3
    d, nm, nv = _rows_fwd(name, _adamw_fn, [_row(a, tr, 0) for a in (w2, g2, m2, v2)], [], outs)
    return d.reshape(shape), nm.reshape(shape), nv.reshape(shape)


def _block_diag(v):
    eye = jnp.eye(SSM_G, dtype=v.dtype)
    return (eye[:, None, :, None] * v[:, :, None, :]).reshape(SSM_G * SSM_P, SSM_G * SSM_N)


def _diag_blocks(m):
    m4 = m.reshape(SSM_G, SSM_P, SSM_G, SSM_N)
    eye = jnp.eye(SSM_G, dtype=m.dtype)
    return (m4 * eye[:, None, :, None]).sum(axis=2)


def _fold(a, dil):
    if dil == 1:
        return a
    return a.reshape((T // dil, dil) + a.shape[1:]).swapaxes(0, 1).reshape(a.shape)


def _unfold(a, dil):
    if dil == 1:
        return a
    return a.reshape((dil, T // dil) + a.shape[1:]).swapaxes(0, 1).reshape(a.shape)


DILS = (1, 4, 16)


def _fold3(a):
    return jnp.stack([_fold(a, d) for d in DILS]).transpose(0, 2, 1, 3)


def _unfold3(a):
    return jnp.stack([_unfold(a[p].swapaxes(0, 1), d).swapaxes(0, 1) for p, d in enumerate(DILS)])


def _refold3(a):
    return jnp.stack([_fold(a[p].swapaxes(0, 1), d).swapaxes(0, 1) for p, d in enumerate(DILS)])


def _rope_tables():
    half = ROPE // 2
    inv_freq = ROPE_THETA ** (-jnp.arange(half, dtype=F32) / half)
    ang = jnp.arange(T).astype(F32)[:, None] * inv_freq[None, :]
    return jnp.cos(ang), jnp.sin(ang)


def _from_blocks(a8):
    return a8.transpose(1, 0, 2).reshape(a8.shape[1], -1)


def _to_blocks(a):
    return a.reshape(a.shape[0], N_DEV, -1).transpose(1, 0, 2)


HALF = ROPE // 2


def _rope_rows(q1, q2, k1, k2, cos, sin):
    return [_row(q1, 256, 1), _row(q2, 256, 1), _row(k1, 256, 0), _row(k2, 256, 0)], [_row(cos, 256, 0), _row(sin, 256, 0)]


def _behind(a, tok):
    return a if tok is None else a + tok


def _mixer_fwd_in(x, w, sp, cos, sin):
    s = {}
    s['x'] = x
    h = _rms_fwd("rms_mix", x, sp['g_mix'])
    proj = _from_blocks(_mm("mm_in", h, w['w_in'], 'nt', bb='r', ob='c'))
    offs = np.cumsum((0,) + IN_SPLITS)
    c_q, c_kv, k_rope, u, qd, kd, vd = [proj[:, offs[i]:offs[i + 1]] for i in range(7)]
    s.update(h=h, c_q=c_q, c_kv=c_kv, u=u)

    cqn = _rms_fwd("rms_q", c_q, sp['g_q'])
    ckvn = _rms_fwd("rms_kv", c_kv, sp['g_kv'])
    q8 = _mm("mm_uq", cqn, w['w_uq'], 'nt', bb='r', ob='c')
    kv8 = _mm("mm_ukv", ckvn, w['w_ukv'], 'nn', bb='c', ob='c')
    rows, aux = _rope_rows(q8[:, :, NOPE:NOPE + HALF], q8[:, :, NOPE + HALF:], k_rope[:, :HALF], k_rope[:, HALF:], cos, sin)
    oq1, oq2, ok1, ok2 = _rows_fwd("rope", _rope_fn, rows + aux, [],
                                   [((H_MLA, T, HALF), F32, 1, 256)] * 2 + [((T, HALF), F32, 0, 256)] * 2)
    k_pe = jnp.concatenate([ok1, ok2], axis=-1)
    qh = jnp.concatenate([q8[:, :, :NOPE], oq1, oq2], axis=-1).astype(BF16)
    kh = jnp.concatenate([kv8[:, :, :NOPE], jnp.broadcast_to(k_pe[None], (H_MLA, T, ROPE))], axis=-1).astype(BF16)
    vh = kv8[:, :, NOPE:].astype(BF16)
    y_mla, lse_mla = _mla_fwd(qh, kh, vh)
    s.update(cqn=cqn, ckvn=ckvn, qh=qh, kh=kh, vh=vh, lse_mla=lse_mla, y_mla=y_mla, qd=qd, kd=kd, vd=vd)
    return s


def _mixer_fwd_out(s, w, sp, tok=None):
    x, u, y_mla, qd, kd, vd = s['x'], s['u'], s['y_mla'], s['qd'], s['kd'], s['vd']
    a3 = lambda n: sp[n].reshape(SSM_G, 1, SSM_N)
    b2 = lambda n: sp[n].transpose(0, 2, 1).reshape(SSM_G * SSM_P, SSM_N)
    disc_rows = [_row(a3('a_re'), 1, 0), _row(a3('a_im'), 1, 0), _row(sp['log_dt'].reshape(SSM_G, 1, 1), 1, 0),
                 _row(b2('b_re'), SSM_P, 0), _row(b2('b_im'), SSM_P, 0)]
    abr, abi, bbr, bbi = _rows_fwd(
        "s5_disc", _s5_disc_fn, disc_rows, [],
        [((SSM_G, 1, SSM_N), F32, 0, 1), ((SSM_G, 1, SSM_N), F32, 0, 1),
         ((SSM_G * SSM_P, SSM_N), F32, 0, SSM_P), ((SSM_G * SSM_P, SSM_N), F32, 0, SSM_P)])
    ar, ai = abr.reshape(1, SSM_S), abi.reshape(1, SSM_S)
    b_mat = jnp.concatenate([_block_diag(bbr.reshape(SSM_G, SSM_P, SSM_N)),
                             _block_diag(bbi.reshape(SSM_G, SSM_P, SSM_N))], axis=1)
    c_mat = jnp.concatenate([_block_diag(sp['c_re']), _block_diag(-sp['c_im'])], axis=1)
    u16, b_mat, c_mat = _behind(u, tok).astype(BF16), b_mat.astype(BF16), c_mat.astype(BF16)
    bu = _mm("mm_s5_b", u16, b_mat, 'nn')
    hst, hst16 = _scan_fwd(bu, ar, ai)
    ymm = _mm("mm_s5_c", hst16, c_mat, 'nt')
    d_row = sp['d_skip'].reshape(1, SSM_W)
    (yg,) = _rows_fwd("s5_act", _s5_act_fn, [_row(ymm), _row(u)], [d_row], [((T, SSM_W), BF16, -2, _row(u)[2])])
    z = _mm("mm_glu", yg, w['w_glu'], 'nn', bb='c')
    glu_rows = [_row(z[:, :SSM_W]), _row(z[:, SSM_W:])]
    glu_b = [sp['b_glu'][:SSM_W].reshape(1, -1), sp['b_glu'][SSM_W:].reshape(1, -1)]
    (y_ssm,) = _rows_fwd("s5_glu", _glu_fn, glu_rows, glu_b, [((T, SSM_W), F32, -2, glu_rows[0][2])])
    s.update(disc_rows=disc_rows, ar=ar, ai=ai, b_mat=b_mat, c_mat=c_mat, hst=hst, hst16=hst16, u16=u16, ymm=ymm,
             d_row=d_row, yg=yg,
             glu_rows=glu_rows, glu_b=glu_b)

    f3 = lambda a: _fold3(a.reshape(T, DIL_H, DIL_D)).astype(BF16)
    qf, kf, vf = f3(qd), f3(kd), f3(vd)
    o_f, lse_f = _band_fwd(qf, kf, vf)
    o_n, lse_n = _unfold3(o_f), _unfold3(lse_f)
    mix_rows = [_row(o_n[p], 256, 1) for p in range(3)] + [_row(lse_n[p], 256, 1) for p in range(3)]
    (y_dh,) = _rows_fwd("dil_mix", _dil_mix_fn, mix_rows, [], [((DIL_H, T, DIL_D), F32, 1, 256)])
    y_dil = y_dh.transpose(1, 0, 2).reshape(T, DIL_W)
    s.update(qf=qf, kf=kf, vf=vf, o_f=o_f, lse_f=lse_f, mix_rows=mix_rows)

    gm, gs, gd = sp['g_out_mla'].reshape(1, -1), sp['g_out_ssm'].reshape(1, -1), sp['g_out_dil'].reshape(1, -1)
    on_rows = [_row(y_mla), _row(y_ssm), _row(y_dil)]
    (ycat,) = _rows_fwd("out_norm", _outnorm_fn, on_rows, [gm, gs, gd], [((T, D), BF16, -2, on_rows[0][2])])
    x1_ = _mm("mm_o", ycat, w['w_o'], 'nn', bb='r', res=x)
    s.update(on_rows=on_rows, on_g=[gm, gs, gd], ycat=ycat, x1=x1_)
    for k in ('qd', 'kd', 'vd'):
        del s[k]
    return x1_


def _ffn_fwd(x1_, w, sp, s, tok=None):
    h2 = _rms_fwd("rms_ffn", x1_, _behind(sp['g_ffn'], tok))
    ga = _mm("mm_gate", h2, w['w_gate'], 'nt', bb='r', ob='c')
    gb = _mm("mm_up", h2, w['w_up'], 'nt', bb='r', ob='c')
    ffn_rows = [_row(ga, None, 1), _row(gb, None, 1)]
    (zf,) = _rows_fwd("swiglu", _swiglu_fn, ffn_rows, [], [(ga.shape, BF16, 1, ffn_rows[0][2])])
    x2_ = _mm("mm_down", zf, w['w_down'], 'nn', ab='c', bb='r', res=x1_)
    s.update(h2=h2, ffn_rows=ffn_rows, zf=zf)
    return x2_


def _b16(a):
    return a.astype(BF16)


def _ffn_bwd(dx2, s, w, sp, tok=None):
    gw, gs_ = {}, {}
    b16 = _b16
    dx2b = b16(_behind(dx2, tok))
    dzf = _mm("mm_down_dx", dx2b, w['w_down'], 'nt', bb='r', ob='c')
    gw['w_down'] = _mm("mm_down_dw", s['zf'], dx2b, 'tn', ab='c', ob='r')
    tr = s['ffn_rows'][0][2]
    dga, dgb = _rows_vjp("swiglu_bwd", _swiglu_fn, s['ffn_rows'], [], [], [_row(dzf, tr, 1)], grad_dtypes=[BF16, BF16])
    gw['w_gate'] = _mm("mm_gate_dw", dga, s['h2'], 'tn', ab='c', ob='r')
    gw['w_up'] = _mm("mm_up_dw", dgb, s['h2'], 'tn', ab='c', ob='r')
    dh2 = _mm("mm_up_dx", dgb, w['w_up'], 'nn', ab='c', bb='r',
              res=_mm("mm_gate_dx", dga, w['w_gate'], 'nn', ab='c', bb='r'))
    dx1, gs_['g_ffn'] = _rms_bwd("rms_ffn_bwd", s['x1'], sp['g_ffn'], dh2, dx2)
    return dx1, gw, gs_


def _mixer_bwd_out(dx1, s, w, sp, tok=None):
    gw, gs_ = {}, {}
    b16 = _b16
    dx1b = b16(_behind(dx1, tok))
    dycat = _mm("mm_o_dx", dx1b, w['w_o'], 'nt', bb='r')
    gw['w_o'] = _mm("mm_o_dw", s['ycat'], dx1b, 'tn', ob='r')
    dy_mla, dy_ssm, dy_dil, gs_['g_out_mla'], gs_['g_out_ssm'], gs_['g_out_dil'] = _rows_vjp(
        "out_norm_bwd", _outnorm_fn, s['on_rows'], s['on_g'], [], [_row(dycat)])

    do_h = dy_dil.reshape(T, DIL_H, DIL_D).transpose(1, 0, 2)
    dmix = _rows_vjp("dil_mix_bwd", _dil_mix_fn, s['mix_rows'], [], [], [_row(do_h, 256, 1)])
    do_n, dlse_n = jnp.stack(dmix[:3]), jnp.stack(dmix[3:])
    dqf, dkf, dvf = _band_bwd(s['qf'], s['kf'], s['vf'], s['o_f'], s['lse_f'], _refold3(do_n), _refold3(dlse_n))
    back = lambda a: _unfold3(a).sum(axis=0).transpose(1, 0, 2).reshape(T, DIL_W)
    dqd, dkd, dvd = back(dqf), back(dkf), back(dvf)

    dz1, dz2, db1, db2 = _rows_vjp("s5_glu_bwd", _glu_fn, s['glu_rows'], s['glu_b'], [], [_row(dy_ssm)])
    gs_['b_glu'] = jnp.concatenate([db1, db2], axis=1)
    dzb = b16(jnp.concatenate([dz1, dz2], axis=1))
    dyg = _mm("mm_glu_dx", dzb, w['w_glu'], 'nt', bb='c')
    gw['w_glu'] = _mm("mm_glu_dw", s['yg'], dzb, 'tn', ob='c')
    dymm, du_act, dd = _rows_vjp("s5_act_bwd", _s5_act_fn, [_row(s['ymm']), _row(s['u'])], [s['d_row']], [], [_row(dyg)],
                                 grad_dtypes=[BF16, F32])
    gs_['d_skip'] = dd
    dhst = _mm("mm_s5_c_dx", dymm, s['c_mat'], 'nn')
    dc_mat = _mm("mm_s5_c_dw", dymm, s['hst16'], 'tn')
    g, dar, dai = _scan_bwd(dhst, s['hst'], s['ar'], s['ai'])
    du = _mm("mm_s5_b_dx", g, s['b_mat'], 'nt', res=du_act)
    db_mat = _mm("mm_s5_b_dw", s['u16'], g, 'tn')
    gs_['c_re'] = _diag_blocks(dc_mat[:, :SSM_S])
    gs_['c_im'] = -_diag_blocks(dc_mat[:, SSM_S:])
    dbbr = _diag_blocks(db_mat[:, :SSM_S]).reshape(SSM_G * SSM_P, SSM_N)
    dbbi = _diag_blocks(db_mat[:, SSM_S:]).reshape(SSM_G * SSM_P, SSM_N)
    disc_cts = [_row(dar.reshape(SSM_G, 1, SSM_N), 1, 0), _row(dai.reshape(SSM_G, 1, SSM_N), 1, 0),
                _row(dbbr, SSM_P, 0), _row(dbbi, SSM_P, 0)]
    da_re, da_im, dldt, db_r, db_i = _rows_vjp("s5_disc_bwd", _s5_disc_fn, s['disc_rows'], [], [], disc_cts)
    gs_['a_re'], gs_['a_im'], gs_['log_dt'] = da_re, da_im, dldt
    unb = lambda a: a.reshape(SSM_G, SSM_P, SSM_N).transpose(0, 2, 1)
    gs_['b_re'], gs_['b_im'] = unb(db_r), unb(db_i)
    return (dy_mla, du, dqd, dkd, dvd), gw, gs_


def _mixer_bwd_in(cts, dx1, s, w, sp, cos, sin, tok=None):
    gw, gs_ = {}, {}
    b16 = _b16
    dy_mla, du, dqd, dkd, dvd = cts
    dqh, dkh, dvh, dkpe = _mla_bwd(s['qh'], s['kh'], s['vh'], s['y_mla'], dy_mla, _behind(s['lse_mla'], tok))
    zq, zk = jnp.zeros((H_MLA, T, HALF), F32), jnp.zeros((T, HALF), F32)
    rows, aux = _rope_rows(zq, zq, zk, zk, cos, sin)
    cts, _ = _rope_rows(dqh[:, :, NOPE:NOPE + HALF], dqh[:, :, NOPE + HALF:], dkpe[:, :HALF], dkpe[:, HALF:], cos, sin)
    dq1, dq2, dk1, dk2 = _rows_vjp("rope_bwd", _rope_fn, rows, [], aux, cts)
    dq8 = b16(jnp.concatenate([dqh[:, :, :NOPE], dq1, dq2], axis=-1))
    dkv8 = b16(jnp.concatenate([dkh[:, :, :NOPE], dvh], axis=-1))
    dk_rope = jnp.concatenate([dk1, dk2], axis=-1)
    dcqn = _mm("mm_uq_dx", dq8, w['w_uq'], 'nn', ab='c', bb='r')
    gw['w_uq'] = _mm("mm_uq_dw", dq8, s['cqn'], 'tn', ab='c', ob='r')
    dckvn = _mm("mm_ukv_dx", dkv8, w['w_ukv'], 'nt', ab='c', bb='c')
    gw['w_ukv'] = _mm("mm_ukv_dw", s['ckvn'], dkv8, 'tn', bb='c', ob='c')
    dc_q, gs_['g_q'] = _rms_bwd("rms_q_bwd", s['c_q'], sp['g_q'], dcqn)
    dc_kv, gs_['g_kv'] = _rms_bwd("rms_kv_bwd", s['c_kv'], sp['g_kv'], dckvn)

    dproj = _to_blocks(b16(jnp.concatenate([dc_q, dc_kv, dk_rope, du, dqd, dkd, dvd], axis=1)))
    dh = _mm("mm_in_dx", dproj, w['w_in'], 'nn', ab='c', bb='r')
    gw['w_in'] = _mm("mm_in_dw", dproj, s['h'], 'tn', ab='c', ob='r')
    dx, gs_['g_mix'] = _rms_bwd("rms_mix_bwd", s['x'], sp['g_mix'], dh, dx1)
    return dx, gw, gs_


def kernel(x, g_mix, w_in, g_q, w_uq, g_kv, w_ukv, a_re, a_im, b_re, b_im, c_re, c_im, d_skip, log_dt, w_glu, b_glu, g_out_mla, g_out_ssm, g_out_dil, w_o, g_ffn, w_gate, w_up, w_down, g_final, loss_target, m_g_mix, m_w_in, m_g_q, m_w_uq, m_g_kv, m_w_ukv, m_a_re, m_a_im, m_b_re, m_b_im, m_c_re, m_c_im, m_d_skip, m_log_dt, m_w_glu, m_b_glu, m_g_out_mla, m_g_out_ssm, m_g_out_dil, m_w_o, m_g_ffn, m_w_gate, m_w_up, m_w_down, m_g_final, v_g_mix, v_w_in, v_g_q, v_w_uq, v_g_kv, v_w_ukv, v_a_re, v_a_im, v_b_re, v_b_im, v_c_re, v_c_im, v_d_skip, v_log_dt, v_w_glu, v_b_glu, v_g_out_mla, v_g_out_ssm, v_g_out_dil, v_w_o, v_g_ffn, v_w_gate, v_w_up, v_w_down, v_g_final):
    W = dict(zip(PARAMS, (g_mix, w_in, g_q, w_uq, g_kv, w_ukv, a_re, a_im, b_re, b_im, c_re, c_im, d_skip, log_dt,
                          w_glu, b_glu, g_out_mla, g_out_ssm, g_out_dil, w_o, g_ffn, w_gate, w_up, w_down, g_final)))
    M = dict(zip(PARAMS, (m_g_mix, m_w_in, m_g_q, m_w_uq, m_g_kv, m_w_ukv, m_a_re, m_a_im, m_b_re, m_b_im, m_c_re,
                          m_c_im, m_d_skip, m_log_dt, m_w_glu, m_b_glu, m_g_out_mla, m_g_out_ssm, m_g_out_dil, m_w_o,
                          m_g_ffn, m_w_gate, m_w_up, m_w_down, m_g_final)))
    V = dict(zip(PARAMS, (v_g_mix, v_w_in, v_g_q, v_w_uq, v_g_kv, v_w_ukv, v_a_re, v_a_im, v_b_re, v_b_im, v_c_re,
                          v_c_im, v_d_skip, v_log_dt, v_w_glu, v_b_glu, v_g_out_mla, v_g_out_ssm, v_g_out_dil, v_w_o,
                          v_g_ffn, v_w_gate, v_w_up, v_w_down, v_g_final)))
    cx, cy, cc = _place()
    core = cc.astype(jnp.int32).reshape(1)
    chip = (2 * cx + cy).astype(jnp.int32).reshape(1)
    cos, sin = _rope_tables()
    small = [{n: W[n][l] for n in SMALL} for l in range(DEPTH)]
    for sp in small:
        for n in ('g_mix', 'g_q', 'g_kv', 'g_ffn'):
            sp[n] = sp[n].reshape(1, -1)

    def tok_of(tokens):
        return sum(t[0, 0] for t in tokens) if tokens else None

    def shard_view(a, n):
        return a.swapaxes(1, 2) if BIG[n] == 't' else a

    def gather_start(l, group, names, after):
        return _gather_start(f"gather_{group}_start_{l}", [shard_view(W[n], n)[l].astype(BF16) for n in names], after)

    xa = x[0]
    h1_mix = gather_start(0, "mix", MIXER_W, jnp.zeros((8, LANES), F32))
    h1_ffn = gather_start(0, "ffn", FFN_W, h1_mix[4])
    h2_mix = _gather_pass_on("gather_mix_0", h1_mix, xa)
    saved, full = [], []
    tokens = [h2_mix[4]]
    for l in range(DEPTH):
        last = l + 1 == DEPTH
        wm = dict(zip(MIXER_W, _gather_finish(f"gather_mix_{l}", h2_mix, xa)))
        sp = dict(small[l])
        sp['g_mix'] = _behind(sp['g_mix'], tok_of(tokens))
        s = _mixer_fwd_in(xa, wm, sp, cos, sin)
        tokens = []
        if l == 0:
            h2_ffn = _gather_pass_on("gather_ffn_0", h1_ffn, s['y_mla'])
            tokens.append(h2_ffn[4])
        if not last:
            h1_mix = gather_start(l + 1, "mix", MIXER_W, s['y_mla'])
            h1_ffn = gather_start(l + 1, "ffn", FFN_W, h1_mix[4])
            tokens += [h1_mix[4], h1_ffn[4]]
        x1 = _mixer_fwd_out(s, wm, small[l], tok_of(tokens))
        tokens = []
        wf = dict(zip(FFN_W, _gather_finish(f"gather_ffn_{l}", h2_ffn, x1)))
        if not last:
            h2_mix = _gather_pass_on(f"gather_mix_{l + 1}", h1_mix, x1)
            tokens.append(h2_mix[4])
        xa = _ffn_fwd(x1, wf, small[l], s, tok_of(tokens))
        tokens = []
        if not last:
            h2_ffn = _gather_pass_on(f"gather_ffn_{l + 1}", h1_ffn, xa)
            tokens.append(h2_ffn[4])
        saved.append(s)
        full.append({**wm, **wf})
    gf = g_final.reshape(1, D)
    ones = jnp.ones((T, 1), F32)
    dxa, dgf, loss_rows = _rows_vjp("loss", _loss_fn, [_row(xa)], [gf], [_row(loss_target[0])], [_row(ones)],
                                    primal=True)
    loss = lax.psum(_sum_rows("loss_sum", loss_rows)[0, 0], ("x", "y", "c"))

    bufs = {n: None for n in BIG}
    pending = []

    def advance(dep):
        tokens = []
        for g in pending:
            names, tag = g['names'], g['tag']
            if g['stage'] == 0:
                p4 = [a.reshape((4, 2) + a.shape[1:]) for a in g['gw']]
                gots = [lax.empty((4,) + a.shape[1:], F32) for a in g['gw']]
                g['h'] = _split_start("rs_sibling_start_" + tag, p4, gots, 4 * len(p4), _sibling_plan, dep)
                tokens.append(g['h'][4])
            elif g['stage'] == 1:
                p4, gots = _split_wait("rs_sibling_wait_" + tag, g['h'], 4 * len(names), _sibling_plan, dep)
                s4 = [_pair_sum("rs_pair_sum_" + n, p, q, core) for n, p, q in zip(names, p4, gots)]
                gots = [lax.empty((3,) + a.shape[1:], F32) for a in s4]
                g['h'] = _split_start("rs_chips_start_" + tag, s4, gots, 3 * len(s4), _chips_plan, dep)
                tokens.append(g['h'][4])
            elif g['stage'] == 3:
                s4, gots = _split_wait("rs_chips_wait_" + tag, g['h'], 3 * len(names), _chips_plan, dep)
                for n, s4n, got in zip(names, s4, gots):
                    bufs[n] = _adamw_shard("adamw_" + n, g['layer'], shard_view(W[n], n), shard_view(M[n], n),
                                           shard_view(V[n], n), s4n, got, chip, bufs[n])
            g['stage'] += 1
        pending[:] = [g for g in pending if g['stage'] < 4]
        return tokens

    def group(names, l, gw, kind):
        return dict(names=names, layer=l, gw=[gw[n] for n in names], stage=0, tag=f"{kind}_{l}")

    g_small = [None] * DEPTH
    tokens = []
    for l in reversed(range(DEPTH)):
        dx1, gw_f, gs_f = _ffn_bwd(dxa, saved[l], full[l], small[l], tok_of(tokens))
        pending.append(group(FFN_W, l, gw_f, "ffn"))
        tokens = advance(dx1)
        cts, gw_o, gs_o = _mixer_bwd_out(dx1, saved[l], full[l], small[l], tok_of(tokens))
        pending.append(group(OUT_W, l, gw_o, "out"))
        tokens = advance(cts[0])
        dxa, gw_i, gs_i = _mixer_bwd_in(cts, dx1, saved[l], full[l], small[l], cos, sin, tok_of(tokens))
        pending.append(group(IN_W, l, gw_i, "in"))
        tokens = advance(dxa)
        g_small[l] = {**gs_f, **gs_o, **gs_i}

    flat = [g_small[l][n].reshape(-1) for l in range(DEPTH) for n in SMALL] + [dgf.reshape(-1)]
    n_small = sum(int(f.shape[0]) for f in flat)
    rows = -(-n_small // (PACK_C * SMALL_ROW_ALIGN)) * SMALL_ROW_ALIGN
    flat = jnp.concatenate(flat + [jnp.zeros((rows * PACK_C - n_small,), F32)]).reshape(rows, PACK_C)
    (gathered,) = _all_gather("gather_small", [flat])
    tot = _sum_devices("small_sum", gathered).reshape(-1)
    grads, off = {}, 0
    per_layer = {n: [] for n in SMALL}
    for l in range(DEPTH):
        for n, shp in SMALL.items():
            k = int(np.prod(shp))
            per_layer[n].append(tot[off:off + k].reshape(shp))
            off += k
    for n in SMALL:
        grads[n] = jnp.stack(per_layer[n])
    grads['g_final'] = tot[off:off + D]

    delta, new_m, new_v = {}, {}, {}
    advance(tot)
    for n in PARAMS:
        if n not in BIG:
            delta[n], new_m[n], new_v[n] = _adamw("adamw_" + n, W[n], grads[n], M[n], V[n])
    while pending:
        advance(delta['g_final'])
    for n in BIG:
        grads[n], delta[n], new_m[n], new_v[n] = [shard_view(b, n) for b in bufs[n]]
    return (loss, dxa[None], *[grads[n] for n in PARAMS], *[delta[n] for n in PARAMS],
            *[new_m[n] for n in PARAMS], *[new_v[n] for n in PARAMS])
```

```python
import jax
import jax.numpy as jnp
import numpy as np
from jax import lax
from jax.experimental import pallas as pl
from jax.experimental.pallas import tpu as pltpu

F32 = jnp.float32
BF16 = jnp.bfloat16

T = 2048
D = 2048
DEPTH = 4
N_DEV = 8
H_MLA, NOPE, ROPE, VDIM = 8, 128, 64, 128
QK = NOPE + ROPE
Q_LORA, KV_LORA = 512, 256
SSM_W, SSM_G, SSM_P, SSM_N = 512, 32, 16, 64
SSM_S = SSM_G * SSM_N
DIL_W, DIL_H, DIL_D = 512, 8, 64
BLK = 128
IN_SPLITS = (Q_LORA, KV_LORA, ROPE, SSM_W, DIL_W, DIL_W, DIL_W)
IN_W = sum(IN_SPLITS)
D_FF = 5632
EPS = 1e-6
ROPE_THETA = 10000.0
MLA_SCALE = QK ** -0.5
DIL_SCALE = DIL_D ** -0.5

ADAM_LR, ADAM_B1, ADAM_B2, ADAM_EPS, ADAM_WD, ADAM_STEP = 0.001, 0.9, 0.999, 1e-08, 0.01, 10

VMEM_LIMIT_V7X = 52 * 1024 * 1024
LANES = 128
PACK_C = 1024
ROW_BLOCK_BYTES = 2 * 1024 * 1024
MM_TM, MM_TN, MM_TK = 1024, 1024, 2048
MM_TB = 512

NT = (((1,), (1,)), ((), ()))
TN = (((0,), (0,)), ((), ()))
H_QK = (((2,), (2,)), ((0,), (0,)))
H_PV = (((2,), (1,)), ((0,), (0,)))
H_TN = (((1,), (1,)), ((0,), (0,)))
HI = lax.Precision.HIGHEST
MESH = pl.DeviceIdType.MESH

PARAMS = ['g_mix', 'w_in', 'g_q', 'w_uq', 'g_kv', 'w_ukv', 'a_re', 'a_im', 'b_re', 'b_im', 'c_re', 'c_im',
          'd_skip', 'log_dt', 'w_glu', 'b_glu', 'g_out_mla', 'g_out_ssm', 'g_out_dil', 'w_o', 'g_ffn',
          'w_gate', 'w_up', 'w_down', 'g_final']
BIG = {'w_in': 't', 'w_uq': 't', 'w_ukv': 'c', 'w_glu': 'c', 'w_o': 'r', 'w_gate': 't', 'w_up': 't', 'w_down': 'r'}
MIXER_W = ['w_in', 'w_uq', 'w_ukv', 'w_glu', 'w_o']
FFN_W = ['w_gate', 'w_up', 'w_down']
OUT_W, IN_W = ['w_o', 'w_glu'], ['w_in', 'w_uq', 'w_ukv']
SMALL = {'g_mix': (D,), 'g_q': (Q_LORA,), 'g_kv': (KV_LORA,), 'a_re': (SSM_G, SSM_N), 'a_im': (SSM_G, SSM_N),
         'b_re': (SSM_G, SSM_N, SSM_P), 'b_im': (SSM_G, SSM_N, SSM_P), 'c_re': (SSM_G, SSM_P, SSM_N),
         'c_im': (SSM_G, SSM_P, SSM_N), 'd_skip': (SSM_G, SSM_P), 'log_dt': (SSM_G,), 'b_glu': (2 * SSM_W,),
         'g_out_mla': (H_MLA * VDIM,), 'g_out_ssm': (SSM_W,), 'g_out_dil': (DIL_W,), 'g_ffn': (D,)}
SMALL_ROW_ALIGN = 64


def _tile(dim, target, align=LANES):
    best = None
    for t in range(align, min(dim, target) + 1, align):
        if dim % t == 0:
            best = t
    return best if best is not None else dim


def _params(sem=None):
    return pltpu.CompilerParams(dimension_semantics=sem, vmem_limit_bytes=VMEM_LIMIT_V7X)


def _dot(a, b, dims=None, prec=None):
    if dims is None:
        return jnp.dot(a, b, preferred_element_type=F32, precision=prec)
    return lax.dot_general(a, b, dims, preferred_element_type=F32, precision=prec)


def _mm_spec(shape, blk, t_r, t_c, rc):
    if blk is None:
        return pl.BlockSpec((t_r, t_c), rc)
    _, R, C = shape
    if blk == 'r':
        per = R // t_r
        return pl.BlockSpec((1, t_r, t_c), lambda i, j, k: (rc(i, j, k)[0] // per, rc(i, j, k)[0] % per, rc(i, j, k)[1]))
    per = C // t_c
    return pl.BlockSpec((1, t_r, t_c), lambda i, j, k: (rc(i, j, k)[1] // per, rc(i, j, k)[0], rc(i, j, k)[1] % per))


def _logical(shape, blk):
    if blk is None:
        return tuple(shape)
    G, R, C = shape
    return (G * R, C) if blk == 'r' else (R, G * C)


def _mm(name, a, b, mode, ab=None, bb=None, ob=None, res=None, prec=None):
    la, lb = _logical(a.shape, ab), _logical(b.shape, bb)
    am, ak = (0, 1) if mode != 'tn' else (1, 0)
    bk, bn = (0, 1) if mode != 'nt' else (1, 0)
    M, K, N = la[am], la[ak], lb[bn]
    assert lb[bk] == K, (name, a.shape, b.shape, mode)
    if ob is None:
        out_shape = (M, N)
    elif ob == 'r':
        G = N_DEV
        out_shape = (G, M // G, N)
    else:
        G = N_DEV
        out_shape = (G, M, N // G)
    em = min(a.shape[-2:][am], out_shape[-2])
    en = min(b.shape[-2:][bn], out_shape[-1])
    ek = min(a.shape[-2:][ak], b.shape[-2:][bk])
    dims = {'nn': None, 'nt': NT, 'tn': TN}[mode]
    a_kb = mode != 'tn' and ab == 'c'
    b_kb = (mode == 'nn' and bb == 'r') or (mode == 'nt' and bb == 'c')
    blocks = K // ek if (a_kb or b_kb) else 1
    tn = _tile(en, MM_TB if blocks > 1 else MM_TN)
    tm = _tile(em, MM_TB if blocks > 1 else MM_TM)
    assert blocks == 1 or ((a_kb or ab is None) and (b_kb or bb is None)), (name, ab, bb, mode)
    tk = ek if blocks > 1 else _tile(ek, MM_TK)
    nk = 1 if blocks > 1 else K // tk

    def val(ref):
        return ref[...] if len(ref.shape) == 2 else ref[0]

    def put(o_ref, r):
        if len(o_ref.shape) == 2:
            o_ref[...] = r
        else:
            o_ref[0] = r

    def k_block(ref, d, blocked, lanes):
        if blocked:
            return ref[d]
        return ref[:, d * ek:(d + 1) * ek] if lanes else ref[d * ek:(d + 1) * ek, :]

    def body(*refs):
        if res is None:
            a_ref, b_ref, o_ref = refs[:3]
            r_ref = None
        else:
            a_ref, b_ref, r_ref, o_ref = refs[:4]
        if blocks > 1:
            part = None
            for d in range(blocks):
                p = _dot(k_block(a_ref, d, a_kb, True), k_block(b_ref, d, b_kb, mode == 'nt'), dims, prec)
                part = p if part is None else part + p
        else:
            part = _dot(val(a_ref), val(b_ref), dims, prec)
        if nk == 1:
            put(o_ref, part if r_ref is None else part + val(r_ref))
            return
        acc_ref = refs[-1]
        k = pl.program_id(2)

        @pl.when(k == 0)
        def _():
            acc_ref[...] = part

        @pl.when((k > 0) & (k < nk - 1))
        def _():
            acc_ref[...] += part

        @pl.when(k == nk - 1)
        def _():
            r = acc_ref[...] + part
            put(o_ref, r if r_ref is None else r + val(r_ref))

    if blocks > 1:
        G = blocks
        a_spec = (pl.BlockSpec((G, tm, ek), lambda i, j, k: (0, i, 0)) if a_kb
                  else pl.BlockSpec((tm, K), lambda i, j, k: (i, 0)))
        if b_kb:
            b_spec = (pl.BlockSpec((G, ek, tn), lambda i, j, k: (0, 0, j)) if mode == 'nn'
                      else pl.BlockSpec((G, tn, ek), lambda i, j, k: (0, j, 0)))
        else:
            b_spec = (pl.BlockSpec((K, tn), lambda i, j, k: (0, j)) if mode == 'nn'
                      else pl.BlockSpec((tn, K), lambda i, j, k: (j, 0)))
    else:
        if mode == 'tn':
            a_spec = _mm_spec(a.shape, ab, tk, tm, lambda i, j, k: (k, i))
        else:
            a_spec = _mm_spec(a.shape, ab, tm, tk, lambda i, j, k: (i, k))
        if mode == 'nt':
            b_spec = _mm_spec(b.shape, bb, tn, tk, lambda i, j, k: (j, k))
        else:
            b_spec = _mm_spec(b.shape, bb, tk, tn, lambda i, j, k: (k, j))
    o_spec = _mm_spec(out_shape, ob, tm, tn, lambda i, j, k: (i, j))
    in_specs = [a_spec, b_spec] + ([o_spec] if res is not None else [])
    args = (a, b) + ((res,) if res is not None else ())
    return pl.pallas_call(
        body, name=name, out_shape=jax.ShapeDtypeStruct(out_shape, F32),
        grid=(M // tm, N // tn, nk), in_specs=in_specs, out_specs=o_spec,
        scratch_shapes=[pltpu.VMEM((tm, tn), F32)] if nk > 1 else [],
        compiler_params=_params(("parallel", "parallel", "arbitrary")),
    )(*args)


def _row(a, tr=None, axis=-2):
    axis = axis % a.ndim
    n = a.shape[axis]
    if tr is None:
        row_bytes = a.size // n * 4
        tr = _tile(n, max(8, min(256, ROW_BLOCK_BYTES // row_bytes)), 8)
    return (a, axis, tr)


def _row_spec(shape, axis, tr):
    nd = len(shape)
    blk = tuple(tr if d == axis else s for d, s in enumerate(shape))
    return pl.BlockSpec(blk, lambda i: tuple(i if d == axis else 0 for d in range(nd)))


def _full_spec(shape):
    nd = len(shape)
    return pl.BlockSpec(tuple(shape), lambda i: (0,) * nd)


def _steps(entries):
    ns = {a.shape[ax] // tr for a, ax, tr in entries}
    assert len(ns) == 1, [(a.shape, ax, tr) for a, ax, tr in entries]
    return ns.pop()


def _as_tuple(r):
    return tuple(r) if isinstance(r, (tuple, list)) else (r,)


def _rows_fwd(name, fn, rows, bcast, outs):
    steps = _steps(rows)
    nr, nb = len(rows), len(bcast)

    def body(*refs):
        vals = [r[...] for r in refs[:nr + nb]]
        res = _as_tuple(fn(*vals))
        for o_ref, r in zip(refs[nr + nb:], res):
            o_ref[...] = r.astype(o_ref.dtype)

    in_specs = [_row_spec(a.shape, ax, tr) for a, ax, tr in rows] + [_full_spec(b.shape) for b in bcast]
    out_specs = [_row_spec(s, ax % len(s), tr) for s, _, ax, tr in outs]
    res = pl.pallas_call(
        body, name=name, out_shape=[jax.ShapeDtypeStruct(s, dt) for s, dt, _, _ in outs],
        grid=(steps,), in_specs=in_specs, out_specs=out_specs,
        compiler_params=_params(("parallel",)),
    )(*[a for a, _, _ in rows], *bcast)
    return res


def _rows_vjp(name, fn, drows, dbc, arows, cts, primal=False, grad_dtypes=None):
    entries = list(drows) + list(arows) + list(cts)
    steps = _steps(entries)
    ndr, ndb, nar, nct = len(drows), len(dbc), len(arows), len(cts)
    gdt = list(grad_dtypes) if grad_dtypes is not None else [F32] * ndr

    def body(*refs):
        p = 0
        dr = [r[...] for r in refs[p:p + ndr]]; p += ndr
        db = [r[...] for r in refs[p:p + ndb]]; p += ndb
        ar = [r[...] for r in refs[p:p + nar]]; p += nar
        ct = [r[...] for r in refs[p:p + nct]]; p += nct
        g_rows = refs[p:p + ndr]; p += ndr
        g_bc = refs[p:p + ndb]; p += ndb
        prim_refs = refs[p:]

        def f(*d):
            return _as_tuple(fn(*d, *ar))

        outs, pullback = jax.vjp(f, *dr, *db)
        grads = pullback(tuple(c.astype(o.dtype) for c, o in zip(ct, outs)))
        for k in range(ndr):
            g_rows[k][...] = grads[k].astype(g_rows[k].dtype)
        if ndb:
            @pl.when(pl.program_id(0) == 0)
            def _():
                for r in g_bc:
                    r[...] = jnp.zeros_like(r)
            for k in range(ndb):
                g_bc[k][...] += grads[ndr + k]
        for r, o in zip(prim_refs, outs):
            r[...] = o.astype(r.dtype)

    in_specs = ([_row_spec(a.shape, ax, tr) for a, ax, tr in drows] + [_full_spec(b.shape) for b in dbc]
                + [_row_spec(a.shape, ax, tr) for a, ax, tr in arows]
                + [_row_spec(a.shape, ax, tr) for a, ax, tr in cts])
    out_shape = ([jax.ShapeDtypeStruct(a.shape, dt) for (a, _, _), dt in zip(drows, gdt)]
                 + [jax.ShapeDtypeStruct(b.shape, F32) for b in dbc])
    out_specs = ([_row_spec(a.shape, ax, tr) for a, ax, tr in drows] + [_full_spec(b.shape) for b in dbc])
    if primal:
        out_shape += [jax.ShapeDtypeStruct(a.shape, F32) for a, _, _ in cts]
        out_specs += [_row_spec(a.shape, ax, tr) for a, ax, tr in cts]
    return pl.pallas_call(
        body, name=name, out_shape=out_shape, grid=(steps,), in_specs=in_specs, out_specs=out_specs,
        compiler_params=_params(("arbitrary",)),
    )(*[a for a, _, _ in drows], *dbc, *[a for a, _, _ in arows], *[a for a, _, _ in cts])


def _rms_fn(x, g):
    return x * lax.rsqrt(jnp.mean(x * x, axis=-1, keepdims=True) + EPS) * g


def _rms_res_fn(x, g):
    return _rms_fn(x, g), x


def _rope_fn(q1, q2, k1, k2, cos, sin):
    return q1 * cos - q2 * sin, q2 * cos + q1 * sin, k1 * cos - k2 * sin, k2 * cos + k1 * sin


def _s5_act_fn(ymm, u, d):
    return jax.nn.gelu(ymm + d * u)


def _glu_fn(z1, z2, b1, b2):
    return (z1 + b1) * jax.nn.sigmoid(z2 + b2)


def _outnorm_fn(ym, ys, yd, gm, gs, gd):
    return jnp.concatenate([_rms_fn(ym, gm), _rms_fn(ys, gs), _rms_fn(yd, gd)], axis=-1)


def _swiglu_fn(a, b):
    return jax.nn.silu(a) * b


def _loss_fn(x, g, tgt):
    err = _rms_fn(x, g) - tgt
    return 0.5 * jnp.mean(err * err, axis=-1, keepdims=True)


def _dil_mix_fn(o0, o1, o2, l0, l1, l2):
    m = jnp.maximum(jnp.maximum(l0, l1), l2)
    e0, e1, e2 = jnp.exp(l0 - m), jnp.exp(l1 - m), jnp.exp(l2 - m)
    s = e0 + e1 + e2
    return (e0 / s) * o0 + (e1 / s) * o1 + (e2 / s) * o2


def _s5_disc_fn(a_re, a_im, ldt, b_r, b_i):
    lr = jnp.minimum(a_re.reshape(1, SSM_N), -1e-4)
    li = a_im.reshape(1, SSM_N)
    dt = jnp.exp(ldt.reshape(1, 1))
    e = jnp.exp(lr * dt)
    ar = e * jnp.cos(li * dt)
    ai = e * jnp.sin(li * dt)
    nr, ni = ar - 1.0, ai
    den = lr * lr + li * li
    cr = (nr * lr + ni * li) / den
    ci = (ni * lr - nr * li) / den
    return ar.reshape(1, 1, SSM_N), ai.reshape(1, 1, SSM_N), cr * b_r - ci * b_i, cr * b_i + ci * b_r


def _adamw_fn(w, g, m, v):
    m = ADAM_B1 * m + (1.0 - ADAM_B1) * g
    v = ADAM_B2 * v + (1.0 - ADAM_B2) * jnp.square(g)
    m_hat = m / (1.0 - ADAM_B1 ** ADAM_STEP)
    v_hat = v / (1.0 - ADAM_B2 ** ADAM_STEP)
    delta = -ADAM_LR * (m_hat / (jnp.sqrt(v_hat) + ADAM_EPS) + ADAM_WD * w)
    return delta, m, v


def _rms_fwd(name, x, g):
    (h,) = _rows_fwd(name, _rms_fn, [_row(x)], [g], [(x.shape, BF16, -2, _row(x)[2])])
    return h


def _rms_bwd(name, x, g, dh, dres=None):
    if dres is None:
        dx, dg = _rows_vjp(name, _rms_fn, [_row(x)], [g], [], [_row(dh)])
    else:
        dx, dg = _rows_vjp(name, _rms_res_fn, [_row(x)], [g], [], [_row(dh), _row(dres)])
    return dx, dg


MLA_TQ = 256
MLA_EXT = 512


def _mla_fwd(q, k, v):
    tq = MLA_TQ

    def body(q_ref, k_ref, v_ref, o_ref, lse_ref):
        i = pl.program_id(1)
        q = q_ref[0]

        def rows_below(ext):
            s = _dot(q, k_ref[0, :ext, :], NT) * MLA_SCALE
            row = i * tq + lax.broadcasted_iota(jnp.int32, (tq, ext), 0)
            col = lax.broadcasted_iota(jnp.int32, (tq, ext), 1)
            s = jnp.where(row >= col, s, -jnp.inf)
            m = jnp.max(s, axis=-1, keepdims=True)
            p = jnp.exp(s - m)
            l = jnp.sum(p, axis=-1, keepdims=True)
            o_ref[...] = _dot((p / l).astype(BF16), v_ref[0, :ext, :])
            lse_ref[0] = m + jnp.log(l)

        for g in range(T // MLA_EXT):
            pl.when(i // (MLA_EXT // tq) == g)(lambda g=g: rows_below((g + 1) * MLA_EXT))

    return pl.pallas_call(
        body, name="mla_fwd",
        out_shape=[jax.ShapeDtypeStruct((T, H_MLA * VDIM), F32), jax.ShapeDtypeStruct((H_MLA, T, 1), F32)],
        grid=(H_MLA, T // tq),
        in_specs=[pl.BlockSpec((1, tq, QK), lambda h, i: (h, i, 0)),
                  pl.BlockSpec((1, T, QK), lambda h, i: (h, 0, 0)),
                  pl.BlockSpec((1, T, VDIM), lambda h, i: (h, 0, 0))],
        out_specs=[pl.BlockSpec((tq, VDIM), lambda h, i: (i, h)),
                   pl.BlockSpec((1, tq, 1), lambda h, i: (h, i, 0))],
        compiler_params=_params(("parallel", "parallel")),
    )(q, k, v)


def _mla_bwd(q, k, v, o, do, lse):
    tq = MLA_TQ

    def body(q_ref, k_ref, v_ref, o_ref, do_ref, lse_ref, dq_ref, dk_ref, dv_ref, dkpe_ref):
        h, i = pl.program_id(0), pl.program_id(1)

        @pl.when(i == 0)
        def _():
            dk_ref[...] = jnp.zeros_like(dk_ref)
            dv_ref[...] = jnp.zeros_like(dv_ref)

        @pl.when((i == 0) & (h == 0))
        def _():
            dkpe_ref[...] = jnp.zeros_like(dkpe_ref)

        q, lse = q_ref[0], lse_ref[0]
        delta = jnp.sum(do_ref[...] * o_ref[...], axis=-1, keepdims=True)
        do = do_ref[...].astype(BF16)

        def rows_below(ext):
            k, v = k_ref[0, :ext, :], v_ref[0, :ext, :]
            s = _dot(q, k, NT) * MLA_SCALE
            row = i * tq + lax.broadcasted_iota(jnp.int32, (tq, ext), 0)
            col = lax.broadcasted_iota(jnp.int32, (tq, ext), 1)
            p = jnp.where(row >= col, jnp.exp(s - lse), 0.0)
            ds = (p * (_dot(do, v, NT) - delta) * MLA_SCALE).astype(BF16)
            dq_ref[0] = _dot(ds, k)
            dk = _dot(ds, q, TN)
            dk_ref[0, :ext, :] += dk
            dkpe_ref[:ext, :] += dk[:, NOPE:]
            dv_ref[0, :ext, :] += _dot(p.astype(BF16), do, TN)

        for g in range(T // MLA_EXT):
            pl.when(i // (MLA_EXT // tq) == g)(lambda g=g: rows_below((g + 1) * MLA_EXT))

    return pl.pallas_call(
        body, name="mla_bwd",
        out_shape=[jax.ShapeDtypeStruct((H_MLA, T, QK), F32), jax.ShapeDtypeStruct((H_MLA, T, QK), F32),
                   jax.ShapeDtypeStruct((H_MLA, T, VDIM), F32), jax.ShapeDtypeStruct((T, ROPE), F32)],
        grid=(H_MLA, T // tq),
        in_specs=[pl.BlockSpec((1, tq, QK), lambda h, i: (h, i, 0)),
                  pl.BlockSpec((1, T, QK), lambda h, i: (h, 0, 0)),
                  pl.BlockSpec((1, T, VDIM), lambda h, i: (h, 0, 0)),
                  pl.BlockSpec((tq, VDIM), lambda h, i: (i, h)),
                  pl.BlockSpec((tq, VDIM), lambda h, i: (i, h)),
                  pl.BlockSpec((1, tq, 1), lambda h, i: (h, i, 0))],
        out_specs=[pl.BlockSpec((1, tq, QK), lambda h, i: (h, i, 0)),
                   pl.BlockSpec((1, T, QK), lambda h, i: (h, 0, 0)),
                   pl.BlockSpec((1, T, VDIM), lambda h, i: (h, 0, 0)),
                   pl.BlockSpec((T, ROPE), lambda h, i: (0, 0))],
        compiler_params=_params(("arbitrary", "arbitrary")),
    )(q, k, v, o, do, lse)


NBLK = T // BLK


def _band_masks():
    r = lax.broadcasted_iota(jnp.int32, (BLK, BLK), 0)
    j = lax.broadcasted_iota(jnp.int32, (BLK, BLK), 1)
    return j <= r, j >= r


def _seq_start(p, i):
    per_seq = lax.shift_right_logical(jnp.int32(NBLK), 2 * p)
    return lax.rem(i, per_seq) == 0


def _band_fwd(q, k, v):
    def body(q_ref, kp_ref, kc_ref, vp_ref, vc_ref, o_ref, lse_ref):
        p, i = pl.program_id(0), pl.program_id(1)
        has_prev = jnp.logical_not(_seq_start(p, i))
        m_cur, m_prev = _band_masks()
        m_cur, m_prev = m_cur[None], (m_prev & has_prev)[None]
        q = q_ref[0]
        s_c = jnp.where(m_cur, _dot(q, kc_ref[0], H_QK) * DIL_SCALE, -jnp.inf)
        s_p = jnp.where(m_prev, _dot(q, kp_ref[0], H_QK) * DIL_SCALE, -jnp.inf)
        m = jnp.maximum(jnp.max(s_c, axis=-1, keepdims=True), jnp.max(s_p, axis=-1, keepdims=True))
        e_c, e_p = jnp.exp(s_c - m), jnp.exp(s_p - m)
        l = jnp.sum(e_c, axis=-1, keepdims=True) + jnp.sum(e_p, axis=-1, keepdims=True)
        o_ref[0] = _dot((e_p / l).astype(BF16), vp_ref[0], H_PV) + _dot((e_c / l).astype(BF16), vc_ref[0], H_PV)
        lse_ref[0] = m + jnp.log(l)

    blk = lambda w: (1, DIL_H, BLK, w)
    cur = lambda p, i: (p, 0, i, 0)
    prev = lambda p, i: (p, 0, jnp.maximum(i - 1, 0), 0)
    return pl.pallas_call(
        body, name="band_fwd",
        out_shape=[jax.ShapeDtypeStruct((3, DIL_H, T, DIL_D), F32), jax.ShapeDtypeStruct((3, DIL_H, T, 1), F32)],
        grid=(3, NBLK),
        in_specs=[pl.BlockSpec(blk(DIL_D), cur), pl.BlockSpec(blk(DIL_D), prev), pl.BlockSpec(blk(DIL_D), cur),
                  pl.BlockSpec(blk(DIL_D), prev), pl.BlockSpec(blk(DIL_D), cur)],
        out_specs=[pl.BlockSpec(blk(DIL_D), cur), pl.BlockSpec(blk(1), cur)],
        compiler_params=_params(("parallel", "parallel")),
    )(q, k, k, v, v)


def _band_bwd(q, k, v, o, lse, do, dlse):
    def body(qc_ref, qn_ref, kp_ref, kc_ref, vp_ref, vc_ref, oc_ref, on_ref, lc_ref, ln_ref,
             doc_ref, don_ref, dlc_ref, dln_ref, dq_ref, dk_ref, dv_ref):
        p, i = pl.program_id(0), pl.program_id(1)
        has_prev = jnp.logical_not(_seq_start(p, i))
        has_next = jnp.logical_not(_seq_start(p, i + 1)) & (i + 1 < NBLK)
        m_cur, m_prev = _band_masks()

        def probs(q, k, lse, mask):
            return jnp.where(mask[None], jnp.exp(_dot(q, k, H_QK) * DIL_SCALE - lse), 0.0)

        def dscore(pr, do, v, shift):
            return (pr * (_dot(do, v, H_QK) + shift) * DIL_SCALE).astype(BF16)

        qc, qn, kp, kc, vp, vc = qc_ref[0], qn_ref[0], kp_ref[0], kc_ref[0], vp_ref[0], vc_ref[0]
        doc, don = doc_ref[0], don_ref[0]
        sh_c = dlc_ref[0] - jnp.sum(doc * oc_ref[0], axis=-1, keepdims=True)
        sh_n = dln_ref[0] - jnp.sum(don * on_ref[0], axis=-1, keepdims=True)
        doc, don = doc.astype(BF16), don.astype(BF16)
        p_cc = probs(qc, kc, lc_ref[0], m_cur)
        p_cp = probs(qc, kp, lc_ref[0], m_prev & has_prev)
        p_nc = probs(qn, kc, ln_ref[0], m_prev & has_next)
        ds_cc = dscore(p_cc, doc, vc, sh_c)
        ds_cp = dscore(p_cp, doc, vp, sh_c)
        ds_nc = dscore(p_nc, don, vc, sh_n)
        dq_ref[0] = _dot(ds_cc, kc, H_PV) + _dot(ds_cp, kp, H_PV)
        dk_ref[0] = _dot(ds_cc, qc, H_TN) + _dot(ds_nc, qn, H_TN)
        dv_ref[0] = _dot(p_cc.astype(BF16), doc, H_TN) + _dot(p_nc.astype(BF16), don, H_TN)

    blk = lambda w: (1, DIL_H, BLK, w)
    cur = lambda p, i: (p, 0, i, 0)
    prev = lambda p, i: (p, 0, jnp.maximum(i - 1, 0), 0)
    nxt = lambda p, i: (p, 0, jnp.minimum(i + 1, NBLK - 1), 0)
    w, one = pl.BlockSpec(blk(DIL_D), cur), pl.BlockSpec(blk(1), cur)
    wn, onen = pl.BlockSpec(blk(DIL_D), nxt), pl.BlockSpec(blk(1), nxt)
    wp = pl.BlockSpec(blk(DIL_D), prev)
    return pl.pallas_call(
        body, name="band_bwd",
        out_shape=[jax.ShapeDtypeStruct((3, DIL_H, T, DIL_D), F32)] * 3,
        grid=(3, NBLK),
        in_specs=[w, wn, wp, w, wp, w, w, wn, one, onen, w, wn, one, onen],
        out_specs=[w, w, w],
        compiler_params=_params(("parallel", "parallel")),
    )(q, q, k, k, v, v, o, o, lse, lse, do, do, dlse, dlse)


SCAN_TC = 256


def _scan_fwd(bu, ar, ai):
    tc, S = SCAN_TC, SSM_S

    def body(bu_ref, ar_ref, ai_ref, h_ref, h16_ref, cr_ref, ci_ref):
        @pl.when(pl.program_id(0) == 0)
        def _():
            cr_ref[...] = jnp.zeros_like(cr_ref)
            ci_ref[...] = jnp.zeros_like(ci_ref)

        a_r, a_i = ar_ref[...], ai_ref[...]

        def step(j, carry):
            hr, hi = carry
            for r in range(8):
                t = pl.multiple_of(j * 8, 8) + r
                br = bu_ref[pl.ds(t, 1), pl.ds(0, S)]
                bi = bu_ref[pl.ds(t, 1), pl.ds(S, S)]
                hr, hi = a_r * hr - a_i * hi + br, a_r * hi + a_i * hr + bi
                h_ref[pl.ds(t, 1), pl.ds(0, S)] = hr
                h_ref[pl.ds(t, 1), pl.ds(S, S)] = hi
            return hr, hi

        hr, hi = lax.fori_loop(0, tc // 8, step, (cr_ref[...], ci_ref[...]))
        cr_ref[...] = hr
        ci_ref[...] = hi
        h16_ref[...] = h_ref[...].astype(BF16)

    return pl.pallas_call(
        body, name="s5_scan_fwd",
        out_shape=[jax.ShapeDtypeStruct((T, 2 * S), F32), jax.ShapeDtypeStruct((T, 2 * S), BF16)],
        grid=(T // tc,),
        in_specs=[pl.BlockSpec((tc, 2 * S), lambda i: (i, 0)), _full_spec((1, S)), _full_spec((1, S))],
        out_specs=[pl.BlockSpec((tc, 2 * S), lambda i: (i, 0)), pl.BlockSpec((tc, 2 * S), lambda i: (i, 0))],
        scratch_shapes=[pltpu.VMEM((1, S), F32), pltpu.VMEM((1, S), F32)],
        compiler_params=_params(("arbitrary",)),
    )(bu, ar, ai)


def _scan_bwd(dh, h, ar, ai):
    tc, S = SCAN_TC, SSM_S
    nc = T // tc

    def body(dh_ref, h_ref, hp_ref, ar_ref, ai_ref, g16_ref, dar_ref, dai_ref, cr_ref, ci_ref, g_ref):
        i = pl.program_id(0)

        @pl.when(i == 0)
        def _():
            cr_ref[...] = jnp.zeros_like(cr_ref)
            ci_ref[...] = jnp.zeros_like(ci_ref)
            dar_ref[...] = jnp.zeros_like(dar_ref)
            dai_ref[...] = jnp.zeros_like(dai_ref)

        a_r, a_i = ar_ref[...], ai_ref[...]
        first_chunk = (i == nc - 1)
        edge = jnp.where(first_chunk, 0.0, 1.0)
        hpr = hp_ref[pl.ds(7, 1), pl.ds(0, S)] * edge
        hpi = hp_ref[pl.ds(7, 1), pl.ds(S, S)] * edge

        def step(jj, carry):
            gr, gi, dar, dai = carry
            j = tc // 8 - 1 - jj
            for r in range(7, -1, -1):
                t = pl.multiple_of(j * 8, 8) + r
                tp = jnp.maximum(t - 1, 0)
                inside = t > 0
                pr = jnp.where(inside, h_ref[pl.ds(tp, 1), pl.ds(0, S)], hpr)
                pi = jnp.where(inside, h_ref[pl.ds(tp, 1), pl.ds(S, S)], hpi)
                gr, gi = (dh_ref[pl.ds(t, 1), pl.ds(0, S)] + a_r * gr + a_i * gi,
                          dh_ref[pl.ds(t, 1), pl.ds(S, S)] + a_r * gi - a_i * gr)
                g_ref[pl.ds(t, 1), pl.ds(0, S)] = gr
                g_ref[pl.ds(t, 1), pl.ds(S, S)] = gi
                dar = dar + gr * pr + gi * pi
                dai = dai + gi * pr - gr * pi
            return gr, gi, dar, dai

        zero = jnp.zeros((1, S), F32)
        gr, gi, dar, dai = lax.fori_loop(0, tc // 8, step, (cr_ref[...], ci_ref[...], zero, zero))
        cr_ref[...] = gr
        ci_ref[...] = gi
        dar_ref[...] += dar
        dai_ref[...] += dai
        g16_ref[...] = g_ref[...].astype(BF16)

    rev = lambda i: (nc - 1 - i, 0)
    before = lambda i: (jnp.maximum((nc - 1 - i) * (tc // 8) - 1, 0), 0)
    return pl.pallas_call(
        body, name="s5_scan_bwd",
        out_shape=[jax.ShapeDtypeStruct((T, 2 * S), BF16), jax.ShapeDtypeStruct((1, S), F32),
                   jax.ShapeDtypeStruct((1, S), F32)],
        grid=(nc,),
        in_specs=[pl.BlockSpec((tc, 2 * S), rev), pl.BlockSpec((tc, 2 * S), rev), pl.BlockSpec((8, 2 * S), before),
                  _full_spec((1, S)), _full_spec((1, S))],
        out_specs=[pl.BlockSpec((tc, 2 * S), rev), _full_spec((1, S)), _full_spec((1, S))],
        scratch_shapes=[pltpu.VMEM((1, S), F32), pltpu.VMEM((1, S), F32), pltpu.VMEM((tc, 2 * S), F32)],
        compiler_params=_params(("arbitrary",)),
    )(dh, h, h, ar, ai)


def _sum_rows(name, x):
    def body(x_ref, o_ref):
        o_ref[...] = jnp.sum(x_ref[...], axis=0, keepdims=True)

    return pl.pallas_call(body, name=name, out_shape=jax.ShapeDtypeStruct((1, 1), F32),
                          in_specs=[_full_spec(x.shape)], out_specs=_full_spec((1, 1)), grid=(1,))(x)


ANY = pl.BlockSpec(memory_space=pl.ANY)


def _place():
    return lax.axis_index("x"), lax.axis_index("y"), lax.axis_index("c")


def _all_gather(name, shards):
    n = len(shards)

    def body(*refs):
        xs, outs = refs[:n], refs[n:2 * n]
        send_sems, recv_sems, local_sems = refs[2 * n:]
        x, y, c = _place()
        me, sibling = (x, y, c), (x, y, 1 - c)
        chips = [(1 - x, y), (x, 1 - y), (1 - x, 1 - y)]

        def slot(t, px, py, pc):
            return outs[t].at[4 * px + 2 * py + pc]

        def copy(t, k, block, to, src=None):
            return pltpu.make_async_remote_copy(
                src_ref=slot(t, *block) if src is None else src, dst_ref=slot(t, *block),
                send_sem=send_sems.at[7 * t + k], recv_sem=recv_sems.at[7 * t + k], device_id=to, device_id_type=MESH)

        mine = [pltpu.make_async_copy(xs[t], slot(t, *me), local_sems.at[t]) for t in range(n)]
        sent = []
        for t in range(n):
            mine[t].start()
            sent.append(copy(t, 0, me, sibling, src=xs[t]))
            sent += [copy(t, 1 + j, me, (*chip, c), src=xs[t]) for j, chip in enumerate(chips)]
        for cp in sent:
            cp.start()
        for j, chip in enumerate(chips):
            for t in range(n):
                copy(t, 1 + j, (*chip, c), me).wait_recv()
                fwd = copy(t, 4 + j, (*chip, c), sibling)
                fwd.start()
                sent.append(fwd)
        for t in range(n):
            copy(t, 0, sibling, me).wait_recv()
            for j, chip in enumerate(chips):
                copy(t, 4 + j, (*chip, 1 - c), me).wait_recv()
        for cp in sent:
            cp.wait_send()
        for cp in mine:
            cp.wait()

    return pl.pallas_call(
        body, name=name, out_shape=[jax.ShapeDtypeStruct((N_DEV,) + s.shape, s.dtype) for s in shards],
        in_specs=[ANY] * n, out_specs=[ANY] * n,
        scratch_shapes=[pltpu.SemaphoreType.DMA((7 * n,)), pltpu.SemaphoreType.DMA((7 * n,)),
                        pltpu.SemaphoreType.DMA((n,))],
    )(*shards)


HBM = pl.BlockSpec(memory_space=pltpu.HBM)
SEM = pl.BlockSpec(memory_space=pltpu.SEMAPHORE)
DATAFLOW = pltpu.SideEffectType.DATAFLOW_SIDE_EFFECTING


def _hbm(a):
    return pltpu.with_memory_space_constraint(a, pltpu.HBM)


def _split_start(name, srcs, lands, ncopies, plan, after):
    ns, nl = len(srcs), len(lands)

    def body(*refs):
        send_sems, recv_sems = refs[ns + nl + 1], refs[ns + nl + 2]
        token = refs[-1]
        for k, (src, dst, peer, _) in enumerate(plan(refs[:ns], refs[ns:ns + nl])):
            pltpu.make_async_remote_copy(src_ref=src, dst_ref=dst, send_sem=send_sems.at[k], recv_sem=recv_sems.at[k],
                                         device_id=peer, device_id_type=MESH).start()
        token[...] = jnp.zeros_like(token)

    out = pl.pallas_call(
        body, name=name,
        out_shape=(pltpu.SemaphoreType.DMA((ncopies,)), pltpu.SemaphoreType.DMA((ncopies,)),
                   *[pltpu.HBM(a.shape, a.dtype) for a in srcs], *[pltpu.HBM(a.shape, a.dtype) for a in lands],
                   jax.ShapeDtypeStruct((8, LANES), F32)),
        in_specs=[HBM] * (ns + nl) + [ANY],
        out_specs=(SEM, SEM, *[HBM] * (ns + nl), pl.BlockSpec(memory_space=pltpu.VMEM)),
        input_output_aliases={i: 2 + i for i in range(ns + nl)},
        compiler_params=pltpu.CompilerParams(has_side_effects=DATAFLOW),
    )(*[_hbm(a) for a in srcs], *[_hbm(a) for a in lands], after)
    return out[0], out[1], list(out[2:2 + ns]), list(out[2 + ns:2 + ns + nl]), out[-1]


def _split_wait(name, handle, ncopies, plan, after):
    send_sems, recv_sems, srcs, lands, _ = handle
    ns, nl = len(srcs), len(lands)

    def body(*refs):
        s_sems, r_sems = refs[ns + nl], refs[ns + nl + 1]
        for k, (src, dst, peer, mine) in enumerate(plan(refs[:ns], refs[ns:ns + nl])):
            pltpu.make_async_remote_copy(src_ref=src, dst_ref=dst, send_sem=s_sems.at[k], recv_sem=r_sems.at[k],
                                         device_id=peer, device_id_type=MESH).wait_send()
            pltpu.make_async_remote_copy(src_ref=src, dst_ref=mine, send_sem=s_sems.at[k], recv_sem=r_sems.at[k],
                                         device_id=peer, device_id_type=MESH).wait_recv()

    out = pl.pallas_call(
        body, name=name,
        out_shape=(*[pltpu.HBM(a.shape, a.dtype) for a in srcs], *[pltpu.HBM(a.shape, a.dtype) for a in lands]),
        in_specs=[HBM] * (ns + nl) + [SEM, SEM, ANY],
        out_specs=tuple([HBM] * (ns + nl)),
        input_output_aliases={i: i for i in range(ns + nl)},
        compiler_params=pltpu.CompilerParams(has_side_effects=DATAFLOW),
    )(*srcs, *lands, send_sems, recv_sems, after)
    return list(out[:ns]), list(out[ns:])


def _slot(px, py, pc):
    return 4 * px + 2 * py + pc


def _gather_plan(xs, lands):
    x, y, c = _place()
    peers = [(x, y, 1 - c), (1 - x, y, c), (x, 1 - y, c), (1 - x, 1 - y, c)]
    return [(xs[t], lands[t].at[_slot(x, y, c)], peer, lands[t].at[_slot(*peer)])
            for t in range(len(xs)) for peer in peers]


def _gather_start(name, shards, after):
    lands = [lax.empty((N_DEV,) + s.shape, s.dtype) for s in shards]
    return _split_start(name, shards, lands, 4 * len(shards), _gather_plan, after)


def _pass_on_plan(_, lands):
    x, y, c = _place()
    blocks = [((1 - x, y, c), (1 - x, y, 1 - c)), ((x, 1 - y, c), (x, 1 - y, 1 - c)),
              ((1 - x, 1 - y, c), (1 - x, 1 - y, 1 - c)), ((x, y, 1 - c), (x, y, c))]
    return [(lands[t].at[_slot(*out)], lands[t].at[_slot(*out)], (x, y, 1 - c), lands[t].at[_slot(*back)])
            for t in range(len(lands)) for out, back in blocks]


def _gather_pass_on(name, handle, after):
    n = len(handle[2])
    _, lands = _split_wait(name + "_wait", handle, 4 * n, _gather_plan, after)
    return _split_start(name + "_pass_on", [], lands, 4 * n, _pass_on_plan, after)


def _gather_finish(name, handle, after):
    n = len(handle[3])
    _, lands = _split_wait(name + "_done", handle, 4 * n, _pass_on_plan, after)
    return lands


def _sibling_plan(ps, gots):
    x, y, c = _place()
    return [(ps[t].at[j, 1 - c], gots[t].at[j], (x, y, 1 - c), gots[t].at[j]) for t in range(len(ps)) for j in range(4)]


def _chips_plan(ss, gots):
    x, y, c = _place()
    chips = [(1 - x, y), (x, 1 - y), (1 - x, 1 - y)]
    return [(ss[t].at[2 * px + py], gots[t].at[k], (px, py, c), gots[t].at[k])
            for t in range(len(ss)) for k, (px, py) in enumerate(chips)]


def _sum_tile(R, C):
    return _tile(R, max(8, min(1024, ROW_BLOCK_BYTES // (C * 4))), 8)


def _pair_sum(name, p4, got, core):
    _, _, R, C = p4.shape
    tr = _sum_tile(R, C)

    def body(core_ref, p_ref, g_ref, o_ref):
        o_ref[...] = p_ref[:, 0] + g_ref[...]

    return pl.pallas_call(
        body, name=name, out_shape=jax.ShapeDtypeStruct((4, R, C), F32),
        grid_spec=pltpu.PrefetchScalarGridSpec(
            num_scalar_prefetch=1, grid=(4, R // tr),
            in_specs=[pl.BlockSpec((1, 1, tr, C), lambda j, i, core: (j, core[0], i, 0)),
                      pl.BlockSpec((1, tr, C), lambda j, i, core: (j, i, 0))],
            out_specs=pl.BlockSpec((1, tr, C), lambda j, i, core: (j, i, 0))),
        compiler_params=_params(("parallel", "parallel")),
    )(core, p4, got)


def _sum_devices(name, g8):
    _, R, C = g8.shape
    tr = _tile(R, SMALL_ROW_ALIGN, 8)

    def body(g_ref, o_ref):
        acc = g_ref[0]
        for d in range(1, N_DEV):
            acc = acc + g_ref[d]
        o_ref[...] = acc

    return pl.pallas_call(
        body, name=name, out_shape=jax.ShapeDtypeStruct((R, C), F32), grid=(R // tr,),
        in_specs=[pl.BlockSpec((N_DEV, tr, C), lambda i: (0, i, 0))], out_specs=pl.BlockSpec((tr, C), lambda i: (i, 0)),
        compiler_params=_params(("parallel",)),
    )(g8)


def _adamw_shard(name, layer, w, m, v, s4, got, chip, bufs):
    _, R, C = w.shape
    tr = _sum_tile(R, C) // 2 if _sum_tile(R, C) % 16 == 0 else _sum_tile(R, C)

    def body(chip_ref, w_ref, m_ref, v_ref, s_ref, g_ref, b0, b1, b2, b3, og, od, om, ov):
        g = ((s_ref[0] + g_ref[0]) + g_ref[1]) + g_ref[2]
        d, nm, nv = _adamw_fn(w_ref[0], g, m_ref[0], v_ref[0])
        og[0], od[0], om[0], ov[0] = g, d, nm, nv

    lay = pl.BlockSpec((1, tr, C), lambda i, chip: (layer, i, 0))
    if bufs is None:
        bufs = [lax.empty(w.shape, F32) for _ in range(4)]
    return pl.pallas_call(
        body, name=name, out_shape=[jax.ShapeDtypeStruct(w.shape, F32)] * 4,
        grid_spec=pltpu.PrefetchScalarGridSpec(
            num_scalar_prefetch=1, grid=(R // tr,),
            in_specs=[lay, lay, lay, pl.BlockSpec((1, tr, C), lambda i, chip: (chip[0], i, 0)),
                      pl.BlockSpec((3, tr, C), lambda i, chip: (0, i, 0)), ANY, ANY, ANY, ANY],
            out_specs=[lay] * 4),
        input_output_aliases={6: 0, 7: 1, 8: 2, 9: 3},
        compiler_params=_params(("parallel",)),
    )(chip, w, m, v, s4, got, *bufs)


def _adamw(name, wt, g, m, v):
    shape = wt.shape
    two = (lambda a: a.reshape(1, -1)) if wt.ndim == 1 else (lambda a: a.reshape(-1, shape[-1]))
    w2, g2, m2, v2 = two(wt), two(g), two(m), two(v)
    tr = _row(w2, None, 0)[2]
    outs = [(w2.shape, F32, 0, tr)] * 3
    d, nm, nv = _rows_fwd(name, _adamw_fn, [_row(a, tr, 0) for a in (w2, g2, m2, v2)], [], outs)
    return d.reshape(shape), nm.reshape(shape), nv.reshape(shape)


def _lane_tiling():
    return (jnp.arange(SSM_N)[:, None] == (jnp.arange(SSM_S) % SSM_N)[None, :]).astype(BF16)


def _own_block():
    r = lax.broadcasted_iota(jnp.int32, (SSM_G * SSM_P, SSM_S), 0) // SSM_P
    c = lax.broadcasted_iota(jnp.int32, (SSM_G * SSM_P, SSM_S), 1) // SSM_N
    return r == c


def _bd_build(name, v_re, v_im, sign, tiling):
    def body(r_ref, i_ref, t_ref, o_ref):
        own = _own_block()
        o_ref[:, :SSM_S] = jnp.where(own, _dot(r_ref[...].astype(BF16), t_ref[...]), 0.0).astype(BF16)
        o_ref[:, SSM_S:] = jnp.where(own, sign * _dot(i_ref[...].astype(BF16), t_ref[...]), 0.0).astype(BF16)

    rows = SSM_G * SSM_P
    return pl.pallas_call(
        body, name=name, out_shape=jax.ShapeDtypeStruct((rows, 2 * SSM_S), BF16), grid=(1,),
        in_specs=[_full_spec((rows, SSM_N))] * 2 + [_full_spec((SSM_N, SSM_S))], out_specs=_full_spec((rows, 2 * SSM_S)),
        compiler_params=_params(("arbitrary",)),
    )(v_re, v_im, tiling)


def _bd_extract(name, m, sign, tiling):
    def body(m_ref, t_ref, r_ref, i_ref):
        own, t = _own_block(), t_ref[...]

        def pick(x):
            x = jnp.where(own, x, 0.0)
            hi = x.astype(BF16)
            rest = x - hi.astype(F32)
            mid = rest.astype(BF16)
            lo = (rest - mid.astype(F32)).astype(BF16)
            return _dot(hi, t, NT) + _dot(mid, t, NT) + _dot(lo, t, NT)

        r_ref[...] = pick(m_ref[:, :SSM_S])
        i_ref[...] = sign * pick(m_ref[:, SSM_S:])

    rows = SSM_G * SSM_P
    return pl.pallas_call(
        body, name=name, out_shape=[jax.ShapeDtypeStruct((rows, SSM_N), F32)] * 2, grid=(1,),
        in_specs=[_full_spec((rows, 2 * SSM_S)), _full_spec((SSM_N, SSM_S))], out_specs=[_full_spec((rows, SSM_N))] * 2,
        compiler_params=_params(("arbitrary",)),
    )(m, tiling)


def _fold(a, dil):
    if dil == 1:
        return a
    return a.reshape((T // dil, dil) + a.shape[1:]).swapaxes(0, 1).reshape(a.shape)


def _unfold(a, dil):
    if dil == 1:
        return a
    return a.reshape((dil, T // dil) + a.shape[1:]).swapaxes(0, 1).reshape(a.shape)


DILS = (1, 4, 16)


def _fold3(a):
    return jnp.stack([_fold(a, d) for d in DILS]).transpose(0, 2, 1, 3)


def _unfold3(a):
    return jnp.stack([_unfold(a[p].swapaxes(0, 1), d).swapaxes(0, 1) for p, d in enumerate(DILS)])


def _refold3(a):
    return jnp.stack([_fold(a[p].swapaxes(0, 1), d).swapaxes(0, 1) for p, d in enumerate(DILS)])


def _rope_tables():
    half = ROPE // 2
    inv_freq = ROPE_THETA ** (-jnp.arange(half, dtype=F32) / half)
    ang = jnp.arange(T).astype(F32)[:, None] * inv_freq[None, :]
    return jnp.cos(ang), jnp.sin(ang)


def _from_blocks(a8):
    return a8.transpose(1, 0, 2).reshape(a8.shape[1], -1)


def _to_blocks(a):
    return a.reshape(a.shape[0], N_DEV, -1).transpose(1, 0, 2)


HALF = ROPE // 2


def _rope_rows(q1, q2, k1, k2, cos, sin):
    return [_row(q1, 256, 1), _row(q2, 256, 1), _row(k1, 256, 0), _row(k2, 256, 0)], [_row(cos, 256, 0), _row(sin, 256, 0)]


def _behind(a, tok):
    return a if tok is None else a + tok


def _mixer_fwd_in(x, w, sp, cos, sin):
    s = {}
    s['x'] = x
    h = _rms_fwd("rms_mix", x, sp['g_mix'])
    proj = _from_blocks(_mm("mm_in", h, w['w_in'], 'nt', bb='r', ob='c'))
    offs = np.cumsum((0,) + IN_SPLITS)
    c_q, c_kv, k_rope, u, qd, kd, vd = [proj[:, offs[i]:offs[i + 1]] for i in range(7)]
    s.update(h=h, c_q=c_q, c_kv=c_kv, u=u)

    cqn = _rms_fwd("rms_q", c_q, sp['g_q'])
    ckvn = _rms_fwd("rms_kv", c_kv, sp['g_kv'])
    q8 = _mm("mm_uq", cqn, w['w_uq'], 'nt', bb='r', ob='c')
    kv8 = _mm("mm_ukv", ckvn, w['w_ukv'], 'nn', bb='c', ob='c')
    rows, aux = _rope_rows(q8[:, :, NOPE:NOPE + HALF], q8[:, :, NOPE + HALF:], k_rope[:, :HALF], k_rope[:, HALF:], cos, sin)
    oq1, oq2, ok1, ok2 = _rows_fwd("rope", _rope_fn, rows + aux, [],
                                   [((H_MLA, T, HALF), F32, 1, 256)] * 2 + [((T, HALF), F32, 0, 256)] * 2)
    k_pe = jnp.concatenate([ok1, ok2], axis=-1)
    qh = jnp.concatenate([q8[:, :, :NOPE], oq1, oq2], axis=-1).astype(BF16)
    kh = jnp.concatenate([kv8[:, :, :NOPE], jnp.broadcast_to(k_pe[None], (H_MLA, T, ROPE))], axis=-1).astype(BF16)
    vh = kv8[:, :, NOPE:].astype(BF16)
    y_mla, lse_mla = _mla_fwd(qh, kh, vh)
    s.update(cqn=cqn, ckvn=ckvn, qh=qh, kh=kh, vh=vh, lse_mla=lse_mla, y_mla=y_mla, qd=qd, kd=kd, vd=vd)
    return s


def _mixer_fwd_out(s, w, sp, tok=None):
    x, u, y_mla, qd, kd, vd = s['x'], s['u'], s['y_mla'], s['qd'], s['kd'], s['vd']
    a3 = lambda n: sp[n].reshape(SSM_G, 1, SSM_N)
    b2 = lambda n: sp[n].transpose(0, 2, 1).reshape(SSM_G * SSM_P, SSM_N)
    disc_rows = [_row(a3('a_re'), 1, 0), _row(a3('a_im'), 1, 0), _row(sp['log_dt'].reshape(SSM_G, 1, 1), 1, 0),
                 _row(b2('b_re'), SSM_P, 0), _row(b2('b_im'), SSM_P, 0)]
    abr, abi, bbr, bbi = _rows_fwd(
        "s5_disc", _s5_disc_fn, disc_rows, [],
        [((SSM_G, 1, SSM_N), F32, 0, 1), ((SSM_G, 1, SSM_N), F32, 0, 1),
         ((SSM_G * SSM_P, SSM_N), F32, 0, SSM_P), ((SSM_G * SSM_P, SSM_N), F32, 0, SSM_P)])
    ar, ai = abr.reshape(1, SSM_S), abi.reshape(1, SSM_S)
    tiling = _lane_tiling()
    b_mat = _bd_build("s5_b_matrix", bbr, bbi, 1.0, tiling)
    c2 = lambda n: sp[n].reshape(SSM_G * SSM_P, SSM_N)
    c_mat = _bd_build("s5_c_matrix", c2('c_re'), c2('c_im'), -1.0, tiling)
    u16 = _behind(u, tok).astype(BF16)
    bu = _mm("mm_s5_b", u16, b_mat, 'nn')
    hst, hst16 = _scan_fwd(bu, ar, ai)
    ymm = _mm("mm_s5_c", hst16, c_mat, 'nt')
    d_row = sp['d_skip'].reshape(1, SSM_W)
    (yg,) = _rows_fwd("s5_act", _s5_act_fn, [_row(ymm), _row(u)], [d_row], [((T, SSM_W), BF16, -2, _row(u)[2])])
    z = _mm("mm_glu", yg, w['w_glu'], 'nn', bb='c')
    glu_rows = [_row(z[:, :SSM_W]), _row(z[:, SSM_W:])]
    glu_b = [sp['b_glu'][:SSM_W].reshape(1, -1), sp['b_glu'][SSM_W:].reshape(1, -1)]
    (y_ssm,) = _rows_fwd("s5_glu", _glu_fn, glu_rows, glu_b, [((T, SSM_W), F32, -2, glu_rows[0][2])])
    s.update(disc_rows=disc_rows, ar=ar, ai=ai, b_mat=b_mat, c_mat=c_mat, hst=hst, hst16=hst16, u16=u16, ymm=ymm,
             d_row=d_row, yg=yg,
             glu_rows=glu_rows, glu_b=glu_b)

    f3 = lambda a: _fold3(a.reshape(T, DIL_H, DIL_D)).astype(BF16)
    qf, kf, vf = f3(qd), f3(kd), f3(vd)
    o_f, lse_f = _band_fwd(qf, kf, vf)
    o_n, lse_n = _unfold3(o_f), _unfold3(lse_f)
    mix_rows = [_row(o_n[p], 256, 1) for p in range(3)] + [_row(lse_n[p], 256, 1) for p in range(3)]
    (y_dh,) = _rows_fwd("dil_mix", _dil_mix_fn, mix_rows, [], [((DIL_H, T, DIL_D), F32, 1, 256)])
    y_dil = y_dh.transpose(1, 0, 2).reshape(T, DIL_W)
    s.update(qf=qf, kf=kf, vf=vf, o_f=o_f, lse_f=lse_f, mix_rows=mix_rows)

    gm, gs, gd = sp['g_out_mla'].reshape(1, -1), sp['g_out_ssm'].reshape(1, -1), sp['g_out_dil'].reshape(1, -1)
    on_rows = [_row(y_mla), _row(y_ssm), _row(y_dil)]
    (ycat,) = _rows_fwd("out_norm", _outnorm_fn, on_rows, [gm, gs, gd], [((T, D), BF16, -2, on_rows[0][2])])
    x1_ = _mm("mm_o", ycat, w['w_o'], 'nn', bb='r', res=x)
    s.update(on_rows=on_rows, on_g=[gm, gs, gd], ycat=ycat, x1=x1_)
    for k in ('qd', 'kd', 'vd'):
        del s[k]
    return x1_


def _ffn_fwd(x1_, w, sp, s, tok=None):
    h2 = _rms_fwd("rms_ffn", x1_, _behind(sp['g_ffn'], tok))
    ga = _mm("mm_gate", h2, w['w_gate'], 'nt', bb='r', ob='c')
    gb = _mm("mm_up", h2, w['w_up'], 'nt', bb='r', ob='c')
    ffn_rows = [_row(ga, None, 1), _row(gb, None, 1)]
    (zf,) = _rows_fwd("swiglu", _swiglu_fn, ffn_rows, [], [(ga.shape, BF16, 1, ffn_rows[0][2])])
    x2_ = _mm("mm_down", zf, w['w_down'], 'nn', ab='c', bb='r', res=x1_)
    s.update(h2=h2, ffn_rows=ffn_rows, zf=zf)
    return x2_


def _b16(a):
    return a.astype(BF16)


def _ffn_bwd(dx2, s, w, sp, tok=None):
    gw, gs_ = {}, {}
    b16 = _b16
    dx2b = b16(_behind(dx2, tok))
    dzf = _mm("mm_down_dx", dx2b, w['w_down'], 'nt', bb='r', ob='c')
    gw['w_down'] = _mm("mm_down_dw", s['zf'], dx2b, 'tn', ab='c', ob='r')
    tr = s['ffn_rows'][0][2]
    dga, dgb = _rows_vjp("swiglu_bwd", _swiglu_fn, s['ffn_rows'], [], [], [_row(dzf, tr, 1)], grad_dtypes=[BF16, BF16])
    gw['w_gate'] = _mm("mm_gate_dw", dga, s['h2'], 'tn', ab='c', ob='r')
    gw['w_up'] = _mm("mm_up_dw", dgb, s['h2'], 'tn', ab='c', ob='r')
    dh2 = _mm("mm_up_dx", dgb, w['w_up'], 'nn', ab='c', bb='r',
              res=_mm("mm_gate_dx", dga, w['w_gate'], 'nn', ab='c', bb='r'))
    dx1, gs_['g_ffn'] = _rms_bwd("rms_ffn_bwd", s['x1'], sp['g_ffn'], dh2, dx2)
    return dx1, gw, gs_


def _mixer_bwd_out(dx1, s, w, sp, tok=None):
    gw, gs_ = {}, {}
    b16 = _b16
    dx1b = b16(_behind(dx1, tok))
    dycat = _mm("mm_o_dx", dx1b, w['w_o'], 'nt', bb='r')
    gw['w_o'] = _mm("mm_o_dw", s['ycat'], dx1b, 'tn', ob='r')
    dy_mla, dy_ssm, dy_dil, gs_['g_out_mla'], gs_['g_out_ssm'], gs_['g_out_dil'] = _rows_vjp(
        "out_norm_bwd", _outnorm_fn, s['on_rows'], s['on_g'], [], [_row(dycat)])

    do_h = dy_dil.reshape(T, DIL_H, DIL_D).transpose(1, 0, 2)
    dmix = _rows_vjp("dil_mix_bwd", _dil_mix_fn, s['mix_rows'], [], [], [_row(do_h, 256, 1)])
    do_n, dlse_n = jnp.stack(dmix[:3]), jnp.stack(dmix[3:])
    dqf, dkf, dvf = _band_bwd(s['qf'], s['kf'], s['vf'], s['o_f'], s['lse_f'], _refold3(do_n), _refold3(dlse_n))
    back = lambda a: _unfold3(a).sum(axis=0).transpose(1, 0, 2).reshape(T, DIL_W)
    dqd, dkd, dvd = back(dqf), back(dkf), back(dvf)

    dz1, dz2, db1, db2 = _rows_vjp("s5_glu_bwd", _glu_fn, s['glu_rows'], s['glu_b'], [], [_row(dy_ssm)])
    gs_['b_glu'] = jnp.concatenate([db1, db2], axis=1)
    dzb = b16(jnp.concatenate([dz1, dz2], axis=1))
    dyg = _mm("mm_glu_dx", dzb, w['w_glu'], 'nt', bb='c')
    gw['w_glu'] = _mm("mm_glu_dw", s['yg'], dzb, 'tn', ob='c')
    dymm, du_act, dd = _rows_vjp("s5_act_bwd", _s5_act_fn, [_row(s['ymm']), _row(s['u'])], [s['d_row']], [], [_row(dyg)],
                                 grad_dtypes=[BF16, F32])
    gs_['d_skip'] = dd
    dhst = _mm("mm_s5_c_dx", dymm, s['c_mat'], 'nn')
    dc_mat = _mm("mm_s5_c_dw", dymm, s['hst16'], 'tn')
    g, dar, dai = _scan_bwd(dhst, s['hst'], s['ar'], s['ai'])
    du = _mm("mm_s5_b_dx", g, s['b_mat'], 'nt', res=du_act)
    db_mat = _mm("mm_s5_b_dw", s['u16'], g, 'tn')
    tiling = _lane_tiling()
    gs_['c_re'], gs_['c_im'] = _bd_extract("s5_c_blocks", dc_mat, -1.0, tiling)
    dbbr, dbbi = _bd_extract("s5_b_blocks", db_mat, 1.0, tiling)
    disc_cts = [_row(dar.reshape(SSM_G, 1, SSM_N), 1, 0), _row(dai.reshape(SSM_G, 1, SSM_N), 1, 0),
                _row(dbbr, SSM_P, 0), _row(dbbi, SSM_P, 0)]
    da_re, da_im, dldt, db_r, db_i = _rows_vjp("s5_disc_bwd", _s5_disc_fn, s['disc_rows'], [], [], disc_cts)
    gs_['a_re'], gs_['a_im'], gs_['log_dt'] = da_re, da_im, dldt
    unb = lambda a: a.reshape(SSM_G, SSM_P, SSM_N).transpose(0, 2, 1)
    gs_['b_re'], gs_['b_im'] = unb(db_r), unb(db_i)
    return (dy_mla, du, dqd, dkd, dvd), gw, gs_


def _mixer_bwd_in(cts, dx1, s, w, sp, cos, sin, tok=None):
    gw, gs_ = {}, {}
    b16 = _b16
    dy_mla, du, dqd, dkd, dvd = cts
    dqh, dkh, dvh, dkpe = _mla_bwd(s['qh'], s['kh'], s['vh'], s['y_mla'], dy_mla, _behind(s['lse_mla'], tok))
    zq, zk = jnp.zeros((H_MLA, T, HALF), F32), jnp.zeros((T, HALF), F32)
    rows, aux = _rope_rows(zq, zq, zk, zk, cos, sin)
    cts, _ = _rope_rows(dqh[:, :, NOPE:NOPE + HALF], dqh[:, :, NOPE + HALF:], dkpe[:, :HALF], dkpe[:, HALF:], cos, sin)
    dq1, dq2, dk1, dk2 = _rows_vjp("rope_bwd", _rope_fn, rows, [], aux, cts)
    dq8 = b16(jnp.concatenate([dqh[:, :, :NOPE], dq1, dq2], axis=-1))
    dkv8 = b16(jnp.concatenate([dkh[:, :, :NOPE], dvh], axis=-1))
    dk_rope = jnp.concatenate([dk1, dk2], axis=-1)
    dcqn = _mm("mm_uq_dx", dq8, w['w_uq'], 'nn', ab='c', bb='r')
    gw['w_uq'] = _mm("mm_uq_dw", dq8, s['cqn'], 'tn', ab='c', ob='r')
    dckvn = _mm("mm_ukv_dx", dkv8, w['w_ukv'], 'nt', ab='c', bb='c')
    gw['w_ukv'] = _mm("mm_ukv_dw", s['ckvn'], dkv8, 'tn', bb='c', ob='c')
    dc_q, gs_['g_q'] = _rms_bwd("rms_q_bwd", s['c_q'], sp['g_q'], dcqn)
    dc_kv, gs_['g_kv'] = _rms_bwd("rms_kv_bwd", s['c_kv'], sp['g_kv'], dckvn)

    dproj = _to_blocks(b16(jnp.concatenate([dc_q, dc_kv, dk_rope, du, dqd, dkd, dvd], axis=1)))
    dh = _mm("mm_in_dx", dproj, w['w_in'], 'nn', ab='c', bb='r')
    gw['w_in'] = _mm("mm_in_dw", dproj, s['h'], 'tn', ab='c', ob='r')
    dx, gs_['g_mix'] = _rms_bwd("rms_mix_bwd", s['x'], sp['g_mix'], dh, dx1)
    return dx, gw, gs_


def kernel(x, g_mix, w_in, g_q, w_uq, g_kv, w_ukv, a_re, a_im, b_re, b_im, c_re, c_im, d_skip, log_dt, w_glu, b_glu, g_out_mla, g_out_ssm, g_out_dil, w_o, g_ffn, w_gate, w_up, w_down, g_final, loss_target, m_g_mix, m_w_in, m_g_q, m_w_uq, m_g_kv, m_w_ukv, m_a_re, m_a_im, m_b_re, m_b_im, m_c_re, m_c_im, m_d_skip, m_log_dt, m_w_glu, m_b_glu, m_g_out_mla, m_g_out_ssm, m_g_out_dil, m_w_o, m_g_ffn, m_w_gate, m_w_up, m_w_down, m_g_final, v_g_mix, v_w_in, v_g_q, v_w_uq, v_g_kv, v_w_ukv, v_a_re, v_a_im, v_b_re, v_b_im, v_c_re, v_c_im, v_d_skip, v_log_dt, v_w_glu, v_b_glu, v_g_out_mla, v_g_out_ssm, v_g_out_dil, v_w_o, v_g_ffn, v_w_gate, v_w_up, v_w_down, v_g_final):
    W = dict(zip(PARAMS, (g_mix, w_in, g_q, w_uq, g_kv, w_ukv, a_re, a_im, b_re, b_im, c_re, c_im, d_skip, log_dt,
                          w_glu, b_glu, g_out_mla, g_out_ssm, g_out_dil, w_o, g_ffn, w_gate, w_up, w_down, g_final)))
    M = dict(zip(PARAMS, (m_g_mix, m_w_in, m_g_q, m_w_uq, m_g_kv, m_w_ukv, m_a_re, m_a_im, m_b_re, m_b_im, m_c_re,
                          m_c_im, m_d_skip, m_log_dt, m_w_glu, m_b_glu, m_g_out_mla, m_g_out_ssm, m_g_out_dil, m_w_o,
                          m_g_ffn, m_w_gate, m_w_up, m_w_down, m_g_final)))
    V = dict(zip(PARAMS, (v_g_mix, v_w_in, v_g_q, v_w_uq, v_g_kv, v_w_ukv, v_a_re, v_a_im, v_b_re, v_b_im, v_c_re,
                          v_c_im, v_d_skip, v_log_dt, v_w_glu, v_b_glu, v_g_out_mla, v_g_out_ssm, v_g_out_dil, v_w_o,
                          v_g_ffn, v_w_gate, v_w_up, v_w_down, v_g_final)))
    cx, cy, cc = _place()
    core = cc.astype(jnp.int32).reshape(1)
    chip = (2 * cx + cy).astype(jnp.int32).reshape(1)
    cos, sin = _rope_tables()
    small = [{n: W[n][l] for n in SMALL} for l in range(DEPTH)]
    for sp in small:
        for n in ('g_mix', 'g_q', 'g_kv', 'g_ffn'):
            sp[n] = sp[n].reshape(1, -1)

    def tok_of(tokens):
        return sum(t[0, 0] for t in tokens) if tokens else None

    def shard_view(a, n):
        return a.swapaxes(1, 2) if BIG[n] == 't' else a

    def gather_start(l, group, names, after):
        return _gather_start(f"gather_{group}_start_{l}", [shard_view(W[n], n)[l].astype(BF16) for n in names], after)

    xa = x[0]
    h1_mix = gather_start(0, "mix", MIXER_W, jnp.zeros((8, LANES), F32))
    h1_ffn = gather_start(0, "ffn", FFN_W, h1_mix[4])
    h2_mix = _gather_pass_on("gather_mix_0", h1_mix, xa)
    saved, full = [], []
    tokens = [h2_mix[4]]
    for l in range(DEPTH):
        last = l + 1 == DEPTH
        wm = dict(zip(MIXER_W, _gather_finish(f"gather_mix_{l}", h2_mix, xa)))
        sp = dict(small[l])
        sp['g_mix'] = _behind(sp['g_mix'], tok_of(tokens))
        s = _mixer_fwd_in(xa, wm, sp, cos, sin)
        tokens = []
        if l == 0:
            h2_ffn = _gather_pass_on("gather_ffn_0", h1_ffn, s['y_mla'])
            tokens.append(h2_ffn[4])
        if not last:
            h1_mix = gather_start(l + 1, "mix", MIXER_W, s['y_mla'])
            h1_ffn = gather_start(l + 1, "ffn", FFN_W, h1_mix[4])
            tokens += [h1_mix[4], h1_ffn[4]]
        x1 = _mixer_fwd_out(s, wm, small[l], tok_of(tokens))
        tokens = []
        wf = dict(zip(FFN_W, _gather_finish(f"gather_ffn_{l}", h2_ffn, x1)))
        if not last:
            h2_mix = _gather_pass_on(f"gather_mix_{l + 1}", h1_mix, x1)
            tokens.append(h2_mix[4])
        xa = _ffn_fwd(x1, wf, small[l], s, tok_of(tokens))
        tokens = []
        if not last:
            h2_ffn = _gather_pass_on(f"gather_ffn_{l + 1}", h1_ffn, xa)
            tokens.append(h2_ffn[4])
        saved.append(s)
        full.append({**wm, **wf})
    gf = g_final.reshape(1, D)
    ones = jnp.ones((T, 1), F32)
    dxa, dgf, loss_rows = _rows_vjp("loss", _loss_fn, [_row(xa)], [gf], [_row(loss_target[0])], [_row(ones)],
                                    primal=True)
    loss = lax.psum(_sum_rows("loss_sum", loss_rows)[0, 0], ("x", "y", "c"))

    bufs = {n: None for n in BIG}
    pending = []

    def advance(dep):
        tokens = []
        for g in pending:
            names, tag = g['names'], g['tag']
            if g['stage'] == 0:
                p4 = [a.reshape((4, 2) + a.shape[1:]) for a in g['gw']]
                gots = [lax.empty((4,) + a.shape[1:], F32) for a in g['gw']]
                g['h'] = _split_start("rs_sibling_start_" + tag, p4, gots, 4 * len(p4), _sibling_plan, dep)
                tokens.append(g['h'][4])
            elif g['stage'] == 1:
                p4, gots = _split_wait("rs_sibling_wait_" + tag, g['h'], 4 * len(names), _sibling_plan, dep)
                s4 = [_pair_sum("rs_pair_sum_" + n, p, q, core) for n, p, q in zip(names, p4, gots)]
                gots = [lax.empty((3,) + a.shape[1:], F32) for a in s4]
                g['h'] = _split_start("rs_chips_start_" + tag, s4, gots, 3 * len(s4), _chips_plan, dep)
                tokens.append(g['h'][4])
            elif g['stage'] == 3:
                s4, gots = _split_wait("rs_chips_wait_" + tag, g['h'], 3 * len(names), _chips_plan, dep)
                for n, s4n, got in zip(names, s4, gots):
                    bufs[n] = _adamw_shard("adamw_" + n, g['layer'], shard_view(W[n], n), shard_view(M[n], n),
                                           shard_view(V[n], n), s4n, got, chip, bufs[n])
            g['stage'] += 1
        pending[:] = [g for g in pending if g['stage'] < 4]
        return tokens

    def group(names, l, gw, kind):
        return dict(names=names, layer=l, gw=[gw[n] for n in names], stage=0, tag=f"{kind}_{l}")

    g_small = [None] * DEPTH
    tokens = []
    for l in reversed(range(DEPTH)):
        dx1, gw_f, gs_f = _ffn_bwd(dxa, saved[l], full[l], small[l], tok_of(tokens))
        pending.append(group(FFN_W, l, gw_f, "ffn"))
        tokens = advance(dx1)
        cts, gw_o, gs_o = _mixer_bwd_out(dx1, saved[l], full[l], small[l], tok_of(tokens))
        pending.append(group(OUT_W, l, gw_o, "out"))
        tokens = advance(cts[0])
        dxa, gw_i, gs_i = _mixer_bwd_in(cts, dx1, saved[l], full[l], small[l], cos, sin, tok_of(tokens))
        pending.append(group(IN_W, l, gw_i, "in"))
        tokens = advance(dxa)
        g_small[l] = {**gs_f, **gs_o, **gs_i}

    flat = [g_small[l][n].reshape(-1) for l in range(DEPTH) for n in SMALL] + [dgf.reshape(-1)]
    n_small = sum(int(f.shape[0]) for f in flat)
    rows = -(-n_small // (PACK_C * SMALL_ROW_ALIGN)) * SMALL_ROW_ALIGN
    flat = jnp.concatenate(flat + [jnp.zeros((rows * PACK_C - n_small,), F32)]).reshape(rows, PACK_C)
    (gathered,) = _all_gather("gather_small", [flat])
    tot = _sum_devices("small_sum", gathered).reshape(-1)
    grads, off = {}, 0
    per_layer = {n: [] for n in SMALL}
    for l in range(DEPTH):
        for n, shp in SMALL.items():
            k = int(np.prod(shp))
            per_layer[n].append(tot[off:off + k].reshape(shp))
            off += k
    for n in SMALL:
        grads[n] = jnp.stack(per_layer[n])
    grads['g_final'] = tot[off:off + D]

    delta, new_m, new_v = {}, {}, {}
    advance(tot)
    for n in PARAMS:
        if n not in BIG:
            delta[n], new_m[n], new_v[n] = _adamw("adamw_" + n, W[n], grads[n], M[n], V[n])
    while pending:
        advance(delta['g_final'])
    for n in BIG:
        grads[n], delta[n], new_m[n], new_v[n] = [shard_view(b, n) for b in bufs[n]]
    return (loss, dxa[None], *[grads[n] for n in PARAMS], *[delta[n] for n in PARAMS],
            *[new_m[n] for n in PARAMS], *[new_v[n] for n in PARAMS])
```

```python
import jax
import jax.numpy as jnp
import numpy as np
from jax import lax
from jax.experimental import pallas as pl
from jax.experimental.pallas import tpu as pltpu

F32 = jnp.float32
BF16 = jnp.bfloat16

T = 2048
D = 2048
DEPTH = 4
N_DEV = 8
H_MLA, NOPE, ROPE, VDIM = 8, 128, 64, 128
QK = NOPE + ROPE
Q_LORA, KV_LORA = 512, 256
SSM_W, SSM_G, SSM_P, SSM_N = 512, 32, 16, 64
SSM_S = SSM_G * SSM_N
DIL_W, DIL_H, DIL_D = 512, 8, 64
BLK = 128
IN_SPLITS = (Q_LORA, KV_LORA, ROPE, SSM_W, DIL_W, DIL_W, DIL_W)
IN_W = sum(IN_SPLITS)
D_FF = 5632
EPS = 1e-6
ROPE_THETA = 10000.0
MLA_SCALE = QK ** -0.5
DIL_SCALE = DIL_D ** -0.5

ADAM_LR, ADAM_B1, ADAM_B2, ADAM_EPS, ADAM_WD, ADAM_STEP = 0.001, 0.9, 0.999, 1e-08, 0.01, 10

VMEM_LIMIT_V7X = 52 * 1024 * 1024
LANES = 128
PACK_C = 1024
ROW_BLOCK_BYTES = 2 * 1024 * 1024
MM_TM, MM_TN, MM_TK = 1024, 1024, 2048
MM_TB = 512

NT = (((1,), (1,)), ((), ()))
TN = (((0,), (0,)), ((), ()))
H_QK = (((2,), (2,)), ((0,), (0,)))
H_PV = (((2,), (1,)), ((0,), (0,)))
H_TN = (((1,), (1,)), ((0,), (0,)))
HI = lax.Precision.HIGHEST
MESH = pl.DeviceIdType.MESH

PARAMS = ['g_mix', 'w_in', 'g_q', 'w_uq', 'g_kv', 'w_ukv', 'a_re', 'a_im', 'b_re', 'b_im', 'c_re', 'c_im',
          'd_skip', 'log_dt', 'w_glu', 'b_glu', 'g_out_mla', 'g_out_ssm', 'g_out_dil', 'w_o', 'g_ffn',
          'w_gate', 'w_up', 'w_down', 'g_final']
BIG = {'w_in': 't', 'w_uq': 't', 'w_ukv': 'c', 'w_glu': 'c', 'w_o': 'r', 'w_gate': 't', 'w_up': 't', 'w_down': 'r'}
MIXER_W = ['w_in', 'w_uq', 'w_ukv', 'w_glu', 'w_o']
FFN_W = ['w_gate', 'w_up', 'w_down']
OUT_W, IN_W = ['w_o', 'w_glu'], ['w_in', 'w_uq', 'w_ukv']
SMALL = {'g_mix': (D,), 'g_q': (Q_LORA,), 'g_kv': (KV_LORA,), 'a_re': (SSM_G, SSM_N), 'a_im': (SSM_G, SSM_N),
         'b_re': (SSM_G, SSM_N, SSM_P), 'b_im': (SSM_G, SSM_N, SSM_P), 'c_re': (SSM_G, SSM_P, SSM_N),
         'c_im': (SSM_G, SSM_P, SSM_N), 'd_skip': (SSM_G, SSM_P), 'log_dt': (SSM_G,), 'b_glu': (2 * SSM_W,),
         'g_out_mla': (H_MLA * VDIM,), 'g_out_ssm': (SSM_W,), 'g_out_dil': (DIL_W,), 'g_ffn': (D,)}
SMALL_ROW_ALIGN = 64


def _tile(dim, target, align=LANES):
    best = None
    for t in range(align, min(dim, target) + 1, align):
        if dim % t == 0:
            best = t
    return best if best is not None else dim


def _params(sem=None):
    return pltpu.CompilerParams(dimension_semantics=sem, vmem_limit_bytes=VMEM_LIMIT_V7X)


def _dot(a, b, dims=None, prec=None):
    if dims is None:
        return jnp.dot(a, b, preferred_element_type=F32, precision=prec)
    return lax.dot_general(a, b, dims, preferred_element_type=F32, precision=prec)


def _mm_spec(shape, blk, t_r, t_c, rc):
    if blk is None:
        return pl.BlockSpec((t_r, t_c), rc)
    _, R, C = shape
    if blk == 'r':
        per = R // t_r
        return pl.BlockSpec((1, t_r, t_c), lambda i, j, k: (rc(i, j, k)[0] // per, rc(i, j, k)[0] % per, rc(i, j, k)[1]))
    per = C // t_c
    return pl.BlockSpec((1, t_r, t_c), lambda i, j, k: (rc(i, j, k)[1] // per, rc(i, j, k)[0], rc(i, j, k)[1] % per))


def _logical(shape, blk):
    if blk is None:
        return tuple(shape)
    G, R, C = shape
    return (G * R, C) if blk == 'r' else (R, G * C)


def _mm(name, a, b, mode, ab=None, bb=None, ob=None, res=None, prec=None):
    la, lb = _logical(a.shape, ab), _logical(b.shape, bb)
    am, ak = (0, 1) if mode != 'tn' else (1, 0)
    bk, bn = (0, 1) if mode != 'nt' else (1, 0)
    M, K, N = la[am], la[ak], lb[bn]
    assert lb[bk] == K, (name, a.shape, b.shape, mode)
    if ob is None:
        out_shape = (M, N)
    elif ob == 'r':
        G = N_DEV
        out_shape = (G, M // G, N)
    else:
        G = N_DEV
        out_shape = (G, M, N // G)
    em = min(a.shape[-2:][am], out_shape[-2])
    en = min(b.shape[-2:][bn], out_shape[-1])
    ek = min(a.shape[-2:][ak], b.shape[-2:][bk])
    dims = {'nn': None, 'nt': NT, 'tn': TN}[mode]
    a_kb = mode != 'tn' and ab == 'c'
    b_kb = (mode == 'nn' and bb == 'r') or (mode == 'nt' and bb == 'c')
    blocks = K // ek if (a_kb or b_kb) else 1
    tn = _tile(en, MM_TB if blocks > 1 else MM_TN)
    tm = _tile(em, MM_TB if blocks > 1 else MM_TM)
    assert blocks == 1 or ((a_kb or ab is None) and (b_kb or bb is None)), (name, ab, bb, mode)
    tk = ek if blocks > 1 else _tile(ek, MM_TK)
    nk = 1 if blocks > 1 else K // tk

    def val(ref):
        return ref[...] if len(ref.shape) == 2 else ref[0]

    def put(o_ref, r):
        if len(o_ref.shape) == 2:
            o_ref[...] = r
        else:
            o_ref[0] = r

    def k_block(ref, d, blocked, lanes):
        if blocked:
            return ref[d]
        return ref[:, d * ek:(d + 1) * ek] if lanes else ref[d * ek:(d + 1) * ek, :]

    def body(*refs):
        if res is None:
            a_ref, b_ref, o_ref = refs[:3]
            r_ref = None
        else:
            a_ref, b_ref, r_ref, o_ref = refs[:4]
        if blocks > 1:
            part = None
            for d in range(blocks):
                p = _dot(k_block(a_ref, d, a_kb, True), k_block(b_ref, d, b_kb, mode == 'nt'), dims, prec)
                part = p if part is None else part + p
        else:
            part = _dot(val(a_ref), val(b_ref), dims, prec)
        if nk == 1:
            put(o_ref, part if r_ref is None else part + val(r_ref))
            return
        acc_ref = refs[-1]
        k = pl.program_id(2)

        @pl.when(k == 0)
        def _():
            acc_ref[...] = part

        @pl.when((k > 0) & (k < nk - 1))
        def _():
            acc_ref[...] += part

        @pl.when(k == nk - 1)
        def _():
            r = acc_ref[...] + part
            put(o_ref, r if r_ref is None else r + val(r_ref))

    if blocks > 1:
        G = blocks
        a_spec = (pl.BlockSpec((G, tm, ek), lambda i, j, k: (0, i, 0)) if a_kb
                  else pl.BlockSpec((tm, K), lambda i, j, k: (i, 0)))
        if b_kb:
            b_spec = (pl.BlockSpec((G, ek, tn), lambda i, j, k: (0, 0, j)) if mode == 'nn'
                      else pl.BlockSpec((G, tn, ek), lambda i, j, k: (0, j, 0)))
        else:
            b_spec = (pl.BlockSpec((K, tn), lambda i, j, k: (0, j)) if mode == 'nn'
                      else pl.BlockSpec((tn, K), lambda i, j, k: (j, 0)))
    else:
        if mode == 'tn':
            a_spec = _mm_spec(a.shape, ab, tk, tm, lambda i, j, k: (k, i))
        else:
            a_spec = _mm_spec(a.shape, ab, tm, tk, lambda i, j, k: (i, k))
        if mode == 'nt':
            b_spec = _mm_spec(b.shape, bb, tn, tk, lambda i, j, k: (j, k))
        else:
            b_spec = _mm_spec(b.shape, bb, tk, tn, lambda i, j, k: (k, j))
    o_spec = _mm_spec(out_shape, ob, tm, tn, lambda i, j, k: (i, j))
    in_specs = [a_spec, b_spec] + ([o_spec] if res is not None else [])
    args = (a, b) + ((res,) if res is not None else ())
    return pl.pallas_call(
        body, name=name, out_shape=jax.ShapeDtypeStruct(out_shape, F32),
        grid=(M // tm, N // tn, nk), in_specs=in_specs, out_specs=o_spec,
        scratch_shapes=[pltpu.VMEM((tm, tn), F32)] if nk > 1 else [],
        compiler_params=_params(("parallel", "parallel", "arbitrary")),
    )(*args)


def _row(a, tr=None, axis=-2):
    axis = axis % a.ndim
    n = a.shape[axis]
    if tr is None:
        row_bytes = a.size // n * 4
        tr = _tile(n, max(8, min(256, ROW_BLOCK_BYTES // row_bytes)), 8)
    return (a, axis, tr)


def _row_spec(shape, axis, tr):
    nd = len(shape)
    blk = tuple(tr if d == axis else s for d, s in enumerate(shape))
    return pl.BlockSpec(blk, lambda i: tuple(i if d == axis else 0 for d in range(nd)))


def _full_spec(shape):
    nd = len(shape)
    return pl.BlockSpec(tuple(shape), lambda i: (0,) * nd)


def _steps(entries):
    ns = {a.shape[ax] // tr for a, ax, tr in entries}
    assert len(ns) == 1, [(a.shape, ax, tr) for a, ax, tr in entries]
    return ns.pop()


def _as_tuple(r):
    return tuple(r) if isinstance(r, (tuple, list)) else (r,)


def _rows_fwd(name, fn, rows, bcast, outs):
    steps = _steps(rows)
    nr, nb = len(rows), len(bcast)

    def body(*refs):
        vals = [r[...] for r in refs[:nr + nb]]
        res = _as_tuple(fn(*vals))
        for o_ref, r in zip(refs[nr + nb:], res):
            o_ref[...] = r.astype(o_ref.dtype)

    in_specs = [_row_spec(a.shape, ax, tr) for a, ax, tr in rows] + [_full_spec(b.shape) for b in bcast]
    out_specs = [_row_spec(s, ax % len(s), tr) for s, _, ax, tr in outs]
    res = pl.pallas_call(
        body, name=name, out_shape=[jax.ShapeDtypeStruct(s, dt) for s, dt, _, _ in outs],
        grid=(steps,), in_specs=in_specs, out_specs=out_specs,
        compiler_params=_params(("parallel",)),
    )(*[a for a, _, _ in rows], *bcast)
    return res


def _rows_vjp(name, fn, drows, dbc, arows, cts, primal=False, grad_dtypes=None):
    entries = list(drows) + list(arows) + list(cts)
    steps = _steps(entries)
    ndr, ndb, nar, nct = len(drows), len(dbc), len(arows), len(cts)
    gdt = list(grad_dtypes) if grad_dtypes is not None else [F32] * ndr

    def body(*refs):
        p = 0
        dr = [r[...] for r in refs[p:p + ndr]]; p += ndr
        db = [r[...] for r in refs[p:p + ndb]]; p += ndb
        ar = [r[...] for r in refs[p:p + nar]]; p += nar
        ct = [r[...] for r in refs[p:p + nct]]; p += nct
        g_rows = refs[p:p + ndr]; p += ndr
        g_bc = refs[p:p + ndb]; p += ndb
        prim_refs = refs[p:]

        def f(*d):
            return _as_tuple(fn(*d, *ar))

        outs, pullback = jax.vjp(f, *dr, *db)
        grads = pullback(tuple(c.astype(o.dtype) for c, o in zip(ct, outs)))
        for k in range(ndr):
            g_rows[k][...] = grads[k].astype(g_rows[k].dtype)
        if ndb:
            @pl.when(pl.program_id(0) == 0)
            def _():
                for r in g_bc:
                    r[...] = jnp.zeros_like(r)
            for k in range(ndb):
                g_bc[k][...] += grads[ndr + k]
        for r, o in zip(prim_refs, outs):
            r[...] = o.astype(r.dtype)

    in_specs = ([_row_spec(a.shape, ax, tr) for a, ax, tr in drows] + [_full_spec(b.shape) for b in dbc]
                + [_row_spec(a.shape, ax, tr) for a, ax, tr in arows]
                + [_row_spec(a.shape, ax, tr) for a, ax, tr in cts])
    out_shape = ([jax.ShapeDtypeStruct(a.shape, dt) for (a, _, _), dt in zip(drows, gdt)]
                 + [jax.ShapeDtypeStruct(b.shape, F32) for b in dbc])
    out_specs = ([_row_spec(a.shape, ax, tr) for a, ax, tr in drows] + [_full_spec(b.shape) for b in dbc])
    if primal:
        out_shape += [jax.ShapeDtypeStruct(a.shape, F32) for a, _, _ in cts]
        out_specs += [_row_spec(a.shape, ax, tr) for a, ax, tr in cts]
    return pl.pallas_call(
        body, name=name, out_shape=out_shape, grid=(steps,), in_specs=in_specs, out_specs=out_specs,
        compiler_params=_params(("arbitrary",)),
    )(*[a for a, _, _ in drows], *dbc, *[a for a, _, _ in arows], *[a for a, _, _ in cts])


def _rms_fn(x, g):
    return x * lax.rsqrt(jnp.mean(x * x, axis=-1, keepdims=True) + EPS) * g


def _rms_res_fn(x, g):
    return _rms_fn(x, g), x


def _rope_fn(q1, q2, k1, k2, cos, sin):
    return q1 * cos - q2 * sin, q2 * cos + q1 * sin, k1 * cos - k2 * sin, k2 * cos + k1 * sin


def _s5_act_fn(ymm, u, d):
    return jax.nn.gelu(ymm + d * u)


def _glu_fn(z1, z2, b1, b2):
    return (z1 + b1) * jax.nn.sigmoid(z2 + b2)


def _outnorm_fn(ym, ys, yd, gm, gs, gd):
    return jnp.concatenate([_rms_fn(ym, gm), _rms_fn(ys, gs), _rms_fn(yd, gd)], axis=-1)


def _swiglu_fn(a, b):
    return jax.nn.silu(a) * b


def _loss_fn(x, g, tgt):
    err = _rms_fn(x, g) - tgt
    return 0.5 * jnp.mean(err * err, axis=-1, keepdims=True)


def _dil_mix_fn(o0, o1, o2, l0, l1, l2):
    m = jnp.maximum(jnp.maximum(l0, l1), l2)
    e0, e1, e2 = jnp.exp(l0 - m), jnp.exp(l1 - m), jnp.exp(l2 - m)
    s = e0 + e1 + e2
    return (e0 / s) * o0 + (e1 / s) * o1 + (e2 / s) * o2


def _s5_disc_fn(a_re, a_im, ldt, b_r, b_i):
    lr = jnp.minimum(a_re.reshape(1, SSM_N), -1e-4)
    li = a_im.reshape(1, SSM_N)
    dt = jnp.exp(ldt.reshape(1, 1))
    e = jnp.exp(lr * dt)
    ar = e * jnp.cos(li * dt)
    ai = e * jnp.sin(li * dt)
    nr, ni = ar - 1.0, ai
    den = lr * lr + li * li
    cr = (nr * lr + ni * li) / den
    ci = (ni * lr - nr * li) / den
    return ar.reshape(1, 1, SSM_N), ai.reshape(1, 1, SSM_N), cr * b_r - ci * b_i, cr * b_i + ci * b_r


def _adamw_fn(w, g, m, v):
    m = ADAM_B1 * m + (1.0 - ADAM_B1) * g
    v = ADAM_B2 * v + (1.0 - ADAM_B2) * jnp.square(g)
    m_hat = m / (1.0 - ADAM_B1 ** ADAM_STEP)
    v_hat = v / (1.0 - ADAM_B2 ** ADAM_STEP)
    delta = -ADAM_LR * (m_hat / (jnp.sqrt(v_hat) + ADAM_EPS) + ADAM_WD * w)
    return delta, m, v


def _rms_fwd(name, x, g):
    (h,) = _rows_fwd(name, _rms_fn, [_row(x)], [g], [(x.shape, BF16, -2, _row(x)[2])])
    return h


def _rms_bwd(name, x, g, dh, dres=None):
    if dres is None:
        dx, dg = _rows_vjp(name, _rms_fn, [_row(x)], [g], [], [_row(dh)])
    else:
        dx, dg = _rows_vjp(name, _rms_res_fn, [_row(x)], [g], [], [_row(dh), _row(dres)])
    return dx, dg


MLA_TQ = 256
MLA_EXT = 512


def _mla_fwd(q, k, v):
    tq = MLA_TQ

    def body(q_ref, k_ref, v_ref, o_ref, lse_ref):
        i = pl.program_id(1)
        q = q_ref[0]

        def rows_below(ext):
            s = _dot(q, k_ref[0, :ext, :], NT) * MLA_SCALE
            row = i * tq + lax.broadcasted_iota(jnp.int32, (tq, ext), 0)
            col = lax.broadcasted_iota(jnp.int32, (tq, ext), 1)
            s = jnp.where(row >= col, s, -jnp.inf)
            m = jnp.max(s, axis=-1, keepdims=True)
            p = jnp.exp(s - m)
            l = jnp.sum(p, axis=-1, keepdims=True)
            o_ref[...] = _dot((p / l).astype(BF16), v_ref[0, :ext, :])
            lse_ref[0] = m + jnp.log(l)

        for g in range(T // MLA_EXT):
            pl.when(i // (MLA_EXT // tq) == g)(lambda g=g: rows_below((g + 1) * MLA_EXT))

    return pl.pallas_call(
        body, name="mla_fwd",
        out_shape=[jax.ShapeDtypeStruct((T, H_MLA * VDIM), F32), jax.ShapeDtypeStruct((H_MLA, T, 1), F32)],
        grid=(H_MLA, T // tq),
        in_specs=[pl.BlockSpec((1, tq, QK), lambda h, i: (h, i, 0)),
                  pl.BlockSpec((1, T, QK), lambda h, i: (h, 0, 0)),
                  pl.BlockSpec((1, T, VDIM), lambda h, i: (h, 0, 0))],
        out_specs=[pl.BlockSpec((tq, VDIM), lambda h, i: (i, h)),
                   pl.BlockSpec((1, tq, 1), lambda h, i: (h, i, 0))],
        compiler_params=_params(("parallel", "parallel")),
    )(q, k, v)


def _mla_bwd(q, k, v, o, do, lse):
    tq = MLA_TQ

    def body(q_ref, k_ref, v_ref, o_ref, do_ref, lse_ref, dq_ref, dk_ref, dv_ref, dkpe_ref):
        h, i = pl.program_id(0), pl.program_id(1)

        @pl.when(i == 0)
        def _():
            dk_ref[...] = jnp.zeros_like(dk_ref)
            dv_ref[...] = jnp.zeros_like(dv_ref)

        @pl.when((i == 0) & (h == 0))
        def _():
            dkpe_ref[...] = jnp.zeros_like(dkpe_ref)

        q, lse = q_ref[0], lse_ref[0]
        delta = jnp.sum(do_ref[...] * o_ref[...], axis=-1, keepdims=True)
        do = do_ref[...].astype(BF16)

        def rows_below(ext):
            k, v = k_ref[0, :ext, :], v_ref[0, :ext, :]
            s = _dot(q, k, NT) * MLA_SCALE
            row = i * tq + lax.broadcasted_iota(jnp.int32, (tq, ext), 0)
            col = lax.broadcasted_iota(jnp.int32, (tq, ext), 1)
            p = jnp.where(row >= col, jnp.exp(s - lse), 0.0)
            ds = (p * (_dot(do, v, NT) - delta) * MLA_SCALE).astype(BF16)
            dq_ref[0] = _dot(ds, k)
            dk = _dot(ds, q, TN)
            dk_ref[0, :ext, :] += dk
            dkpe_ref[:ext, :] += dk[:, NOPE:]
            dv_ref[0, :ext, :] += _dot(p.astype(BF16), do, TN)

        for g in range(T // MLA_EXT):
            pl.when(i // (MLA_EXT // tq) == g)(lambda g=g: rows_below((g + 1) * MLA_EXT))

    return pl.pallas_call(
        body, name="mla_bwd",
        out_shape=[jax.ShapeDtypeStruct((H_MLA, T, QK), F32), jax.ShapeDtypeStruct((H_MLA, T, QK), F32),
                   jax.ShapeDtypeStruct((H_MLA, T, VDIM), F32), jax.ShapeDtypeStruct((T, ROPE), F32)],
        grid=(H_MLA, T // tq),
        in_specs=[pl.BlockSpec((1, tq, QK), lambda h, i: (h, i, 0)),
                  pl.BlockSpec((1, T, QK), lambda h, i: (h, 0, 0)),
                  pl.BlockSpec((1, T, VDIM), lambda h, i: (h, 0, 0)),
                  pl.BlockSpec((tq, VDIM), lambda h, i: (i, h)),
                  pl.BlockSpec((tq, VDIM), lambda h, i: (i, h)),
                  pl.BlockSpec((1, tq, 1), lambda h, i: (h, i, 0))],
        out_specs=[pl.BlockSpec((1, tq, QK), lambda h, i: (h, i, 0)),
                   pl.BlockSpec((1, T, QK), lambda h, i: (h, 0, 0)),
                   pl.BlockSpec((1, T, VDIM), lambda h, i: (h, 0, 0)),
                   pl.BlockSpec((T, ROPE), lambda h, i: (0, 0))],
        compiler_params=_params(("arbitrary", "arbitrary")),
    )(q, k, v, o, do, lse)


NBLK = T // BLK


HROWS = DIL_H * BLK


def _band_masks():
    r = lax.broadcasted_iota(jnp.int32, (HROWS, BLK), 0) & (BLK - 1)
    j = lax.broadcasted_iota(jnp.int32, (HROWS, BLK), 1)
    return j <= r, j >= r


def _head_lanes():
    lane_head = lax.broadcasted_iota(jnp.int32, (1, DIL_W), 1) // DIL_D
    return [lane_head == h for h in range(DIL_H)]


def _stack_heads(x, lanes):
    return jnp.concatenate([jnp.where(m, x, jnp.zeros_like(x)) for m in lanes], axis=0)


def _merge_heads(xs, lanes):
    out = None
    for h, m in enumerate(lanes):
        part = jnp.where(m, xs[h * BLK:(h + 1) * BLK], 0.0)
        out = part if out is None else out + part
    return out


def _per_head(x, lanes):
    return jnp.concatenate([jnp.sum(jnp.where(m, x, 0.0), axis=-1, keepdims=True) for m in lanes], axis=0)


def _seq_start(p, i):
    per_seq = lax.shift_right_logical(jnp.int32(NBLK), 2 * p)
    return lax.rem(i, per_seq) == 0


def _band_fwd(q, k, v):
    def body(q_ref, kp_ref, kc_ref, vp_ref, vc_ref, o_ref, lse_ref):
        p, i = pl.program_id(0), pl.program_id(1)
        has_prev = jnp.logical_not(_seq_start(p, i))
        m_cur, m_prev = _band_masks()
        lanes = _head_lanes()
        qs = _stack_heads(q_ref[0], lanes)
        s_c = jnp.where(m_cur, _dot(qs, kc_ref[0], NT) * DIL_SCALE, -jnp.inf)
        s_p = jnp.where(m_prev & has_prev, _dot(qs, kp_ref[0], NT) * DIL_SCALE, -jnp.inf)
        m = jnp.maximum(jnp.max(s_c, axis=-1, keepdims=True), jnp.max(s_p, axis=-1, keepdims=True))
        e_c, e_p = jnp.exp(s_c - m), jnp.exp(s_p - m)
        l = jnp.sum(e_c, axis=-1, keepdims=True) + jnp.sum(e_p, axis=-1, keepdims=True)
        os = _dot((e_p / l).astype(BF16), vp_ref[0]) + _dot((e_c / l).astype(BF16), vc_ref[0])
        o_ref[0] = _merge_heads(os, lanes)
        lse_ref[0] = _merge_heads(m + jnp.log(l), lanes)

    blk = (1, BLK, DIL_W)
    cur = lambda p, i: (p, i, 0)
    prev = lambda p, i: (p, jnp.maximum(i - 1, 0), 0)
    return pl.pallas_call(
        body, name="band_fwd",
        out_shape=[jax.ShapeDtypeStruct((3, T, DIL_W), F32)] * 2,
        grid=(3, NBLK),
        in_specs=[pl.BlockSpec(blk, cur), pl.BlockSpec(blk, prev), pl.BlockSpec(blk, cur),
                  pl.BlockSpec(blk, prev), pl.BlockSpec(blk, cur)],
        out_specs=[pl.BlockSpec(blk, cur), pl.BlockSpec(blk, cur)],
        compiler_params=_params(("parallel", "parallel")),
    )(q, k, k, v, v)


def _band_bwd(q, k, v, o, lse, do, dlse):
    def body(qc_ref, qn_ref, kp_ref, kc_ref, vp_ref, vc_ref, oc_ref, on_ref, lc_ref, ln_ref,
             doc_ref, don_ref, dlc_ref, dln_ref, dq_ref, dk_ref, dv_ref):
        p, i = pl.program_id(0), pl.program_id(1)
        has_prev = jnp.logical_not(_seq_start(p, i))
        has_next = jnp.logical_not(_seq_start(p, i + 1)) & (i + 1 < NBLK)
        m_cur, m_prev = _band_masks()
        lanes = _head_lanes()

        def probs(qs, k, lse, mask):
            return jnp.where(mask, jnp.exp(_dot(qs, k, NT) * DIL_SCALE - lse), 0.0)

        def dscore(pr, dos, v, shift):
            return (pr * (_dot(dos, v, NT) + shift) * DIL_SCALE).astype(BF16)

        kp, kc, vp, vc = kp_ref[0], kc_ref[0], vp_ref[0], vc_ref[0]
        qc, qn = _stack_heads(qc_ref[0], lanes), _stack_heads(qn_ref[0], lanes)
        doc, don = doc_ref[0], don_ref[0]
        lse_c = _per_head(lc_ref[0], lanes) * (1.0 / DIL_D)
        lse_n = _per_head(ln_ref[0], lanes) * (1.0 / DIL_D)
        sh_c = _per_head(dlc_ref[0] - doc * oc_ref[0], lanes)
        sh_n = _per_head(dln_ref[0] - don * on_ref[0], lanes)
        doc, don = _stack_heads(doc.astype(BF16), lanes), _stack_heads(don.astype(BF16), lanes)
        p_cc = probs(qc, kc, lse_c, m_cur)
        p_cp = probs(qc, kp, lse_c, m_prev & has_prev)
        p_nc = probs(qn, kc, lse_n, m_prev & has_next)
        ds_cc = dscore(p_cc, doc, vc, sh_c)
        ds_cp = dscore(p_cp, doc, vp, sh_c)
        ds_nc = dscore(p_nc, don, vc, sh_n)
        dq_ref[0] = _merge_heads(_dot(ds_cc, kc) + _dot(ds_cp, kp), lanes)
        dk_ref[0] = _dot(ds_cc, qc, TN) + _dot(ds_nc, qn, TN)
        dv_ref[0] = _dot(p_cc.astype(BF16), doc, TN) + _dot(p_nc.astype(BF16), don, TN)

    blk = (1, BLK, DIL_W)
    cur = lambda p, i: (p, i, 0)
    prev = lambda p, i: (p, jnp.maximum(i - 1, 0), 0)
    nxt = lambda p, i: (p, jnp.minimum(i + 1, NBLK - 1), 0)
    w, wn, wp = pl.BlockSpec(blk, cur), pl.BlockSpec(blk, nxt), pl.BlockSpec(blk, prev)
    return pl.pallas_call(
        body, name="band_bwd",
        out_shape=[jax.ShapeDtypeStruct((3, T, DIL_W), F32)] * 3,
        grid=(3, NBLK),
        in_specs=[w, wn, wp, w, wp, w, w, wn, w, wn, w, wn, w, wn],
        out_specs=[w, w, w],
        compiler_params=_params(("parallel", "parallel")),
    )(q, q, k, k, v, v, o, o, lse, lse, do, do, dlse, dlse)


SCAN_TC = 256


def _scan_fwd(bu, ar, ai):
    tc, S = SCAN_TC, SSM_S

    def body(bu_ref, ar_ref, ai_ref, h_ref, h16_ref, cr_ref, ci_ref):
        @pl.when(pl.program_id(0) == 0)
        def _():
            cr_ref[...] = jnp.zeros_like(cr_ref)
            ci_ref[...] = jnp.zeros_like(ci_ref)

        a_r, a_i = ar_ref[...], ai_ref[...]

        def step(j, carry):
            hr, hi = carry
            for r in range(8):
                t = pl.multiple_of(j * 8, 8) + r
                br = bu_ref[pl.ds(t, 1), pl.ds(0, S)]
                bi = bu_ref[pl.ds(t, 1), pl.ds(S, S)]
                hr, hi = a_r * hr - a_i * hi + br, a_r * hi + a_i * hr + bi
                h_ref[pl.ds(t, 1), pl.ds(0, S)] = hr
                h_ref[pl.ds(t, 1), pl.ds(S, S)] = hi
            return hr, hi

        hr, hi = lax.fori_loop(0, tc // 8, step, (cr_ref[...], ci_ref[...]))
        cr_ref[...] = hr
        ci_ref[...] = hi
        h16_ref[...] = h_ref[...].astype(BF16)

    return pl.pallas_call(
        body, name="s5_scan_fwd",
        out_shape=[jax.ShapeDtypeStruct((T, 2 * S), F32), jax.ShapeDtypeStruct((T, 2 * S), BF16)],
        grid=(T // tc,),
        in_specs=[pl.BlockSpec((tc, 2 * S), lambda i: (i, 0)), _full_spec((1, S)), _full_spec((1, S))],
        out_specs=[pl.BlockSpec((tc, 2 * S), lambda i: (i, 0)), pl.BlockSpec((tc, 2 * S), lambda i: (i, 0))],
        scratch_shapes=[pltpu.VMEM((1, S), F32), pltpu.VMEM((1, S), F32)],
        compiler_params=_params(("arbitrary",)),
    )(bu, ar, ai)


def _scan_bwd(dh, h, ar, ai):
    tc, S = SCAN_TC, SSM_S
    nc = T // tc

    def body(dh_ref, h_ref, hp_ref, ar_ref, ai_ref, g16_ref, dar_ref, dai_ref, cr_ref, ci_ref, g_ref):
        i = pl.program_id(0)

        @pl.when(i == 0)
        def _():
            cr_ref[...] = jnp.zeros_like(cr_ref)
            ci_ref[...] = jnp.zeros_like(ci_ref)
            dar_ref[...] = jnp.zeros_like(dar_ref)
            dai_ref[...] = jnp.zeros_like(dai_ref)

        a_r, a_i = ar_ref[...], ai_ref[...]
        first_chunk = (i == nc - 1)
        edge = jnp.where(first_chunk, 0.0, 1.0)
        hpr = hp_ref[pl.ds(7, 1), pl.ds(0, S)] * edge
        hpi = hp_ref[pl.ds(7, 1), pl.ds(S, S)] * edge

        def step(jj, carry):
            gr, gi, dar, dai = carry
            j = tc // 8 - 1 - jj
            for r in range(7, -1, -1):
                t = pl.multiple_of(j * 8, 8) + r
                tp = jnp.maximum(t - 1, 0)
                inside = t > 0
                pr = jnp.where(inside, h_ref[pl.ds(tp, 1), pl.ds(0, S)], hpr)
                pi = jnp.where(inside, h_ref[pl.ds(tp, 1), pl.ds(S, S)], hpi)
                gr, gi = (dh_ref[pl.ds(t, 1), pl.ds(0, S)] + a_r * gr + a_i * gi,
                          dh_ref[pl.ds(t, 1), pl.ds(S, S)] + a_r * gi - a_i * gr)
                g_ref[pl.ds(t, 1), pl.ds(0, S)] = gr
                g_ref[pl.ds(t, 1), pl.ds(S, S)] = gi
                dar = dar + gr * pr + gi * pi
                dai = dai + gi * pr - gr * pi
            return gr, gi, dar, dai

        zero = jnp.zeros((1, S), F32)
        gr, gi, dar, dai = lax.fori_loop(0, tc // 8, step, (cr_ref[...], ci_ref[...], zero, zero))
        cr_ref[...] = gr
        ci_ref[...] = gi
        dar_ref[...] += dar
        dai_ref[...] += dai
        g16_ref[...] = g_ref[...].astype(BF16)

    rev = lambda i: (nc - 1 - i, 0)
    before = lambda i: (jnp.maximum((nc - 1 - i) * (tc // 8) - 1, 0), 0)
    return pl.pallas_call(
        body, name="s5_scan_bwd",
        out_shape=[jax.ShapeDtypeStruct((T, 2 * S), BF16), jax.ShapeDtypeStruct((1, S), F32),
                   jax.ShapeDtypeStruct((1, S), F32)],
        grid=(nc,),
        in_specs=[pl.BlockSpec((tc, 2 * S), rev), pl.BlockSpec((tc, 2 * S), rev), pl.BlockSpec((8, 2 * S), before),
                  _full_spec((1, S)), _full_spec((1, S))],
        out_specs=[pl.BlockSpec((tc, 2 * S), rev), _full_spec((1, S)), _full_spec((1, S))],
        scratch_shapes=[pltpu.VMEM((1, S), F32), pltpu.VMEM((1, S), F32), pltpu.VMEM((tc, 2 * S), F32)],
        compiler_params=_params(("arbitrary",)),
    )(dh, h, h, ar, ai)


def _sum_rows(name, x):
    def body(x_ref, o_ref):
        o_ref[...] = jnp.sum(x_ref[...], axis=0, keepdims=True)

    return pl.pallas_call(body, name=name, out_shape=jax.ShapeDtypeStruct((1, 1), F32),
                          in_specs=[_full_spec(x.shape)], out_specs=_full_spec((1, 1)), grid=(1,))(x)


ANY = pl.BlockSpec(memory_space=pl.ANY)


def _place():
    return lax.axis_index("x"), lax.axis_index("y"), lax.axis_index("c")


def _all_gather(name, shards):
    n = len(shards)

    def body(*refs):
        xs, outs = refs[:n], refs[n:2 * n]
        send_sems, recv_sems, local_sems = refs[2 * n:]
        x, y, c = _place()
        me, sibling = (x, y, c), (x, y, 1 - c)
        chips = [(1 - x, y), (x, 1 - y), (1 - x, 1 - y)]

        def slot(t, px, py, pc):
            return outs[t].at[4 * px + 2 * py + pc]

        def copy(t, k, block, to, src=None):
            return pltpu.make_async_remote_copy(
                src_ref=slot(t, *block) if src is None else src, dst_ref=slot(t, *block),
                send_sem=send_sems.at[7 * t + k], recv_sem=recv_sems.at[7 * t + k], device_id=to, device_id_type=MESH)

        mine = [pltpu.make_async_copy(xs[t], slot(t, *me), local_sems.at[t]) for t in range(n)]
        sent = []
        for t in range(n):
            mine[t].start()
            sent.append(copy(t, 0, me, sibling, src=xs[t]))
            sent += [copy(t, 1 + j, me, (*chip, c), src=xs[t]) for j, chip in enumerate(chips)]
        for cp in sent:
            cp.start()
        for j, chip in enumerate(chips):
            for t in range(n):
                copy(t, 1 + j, (*chip, c), me).wait_recv()
                fwd = copy(t, 4 + j, (*chip, c), sibling)
                fwd.start()
                sent.append(fwd)
        for t in range(n):
            copy(t, 0, sibling, me).wait_recv()
            for j, chip in enumerate(chips):
                copy(t, 4 + j, (*chip, 1 - c), me).wait_recv()
        for cp in sent:
            cp.wait_send()
        for cp in mine:
            cp.wait()

    return pl.pallas_call(
        body, name=name, out_shape=[jax.ShapeDtypeStruct((N_DEV,) + s.shape, s.dtype) for s in shards],
        in_specs=[ANY] * n, out_specs=[ANY] * n,
        scratch_shapes=[pltpu.SemaphoreType.DMA((7 * n,)), pltpu.SemaphoreType.DMA((7 * n,)),
                        pltpu.SemaphoreType.DMA((n,))],
    )(*shards)


HBM = pl.BlockSpec(memory_space=pltpu.HBM)
SEM = pl.BlockSpec(memory_space=pltpu.SEMAPHORE)
DATAFLOW = pltpu.SideEffectType.DATAFLOW_SIDE_EFFECTING


def _hbm(a):
    return pltpu.with_memory_space_constraint(a, pltpu.HBM)


def _split_start(name, srcs, lands, ncopies, plan, after):
    ns, nl = len(srcs), len(lands)

    def body(*refs):
        send_sems, recv_sems = refs[ns + nl + 1], refs[ns + nl + 2]
        token = refs[-1]
        for k, (src, dst, peer, _) in enumerate(plan(refs[:ns], refs[ns:ns + nl])):
            pltpu.make_async_remote_copy(src_ref=src, dst_ref=dst, send_sem=send_sems.at[k], recv_sem=recv_sems.at[k],
                                         device_id=peer, device_id_type=MESH).start()
        token[...] = jnp.zeros_like(token)

    out = pl.pallas_call(
        body, name=name,
        out_shape=(pltpu.SemaphoreType.DMA((ncopies,)), pltpu.SemaphoreType.DMA((ncopies,)),
                   *[pltpu.HBM(a.shape, a.dtype) for a in srcs], *[pltpu.HBM(a.shape, a.dtype) for a in lands],
                   jax.ShapeDtypeStruct((8, LANES), F32)),
        in_specs=[HBM] * (ns + nl) + [ANY],
        out_specs=(SEM, SEM, *[HBM] * (ns + nl), pl.BlockSpec(memory_space=pltpu.VMEM)),
        input_output_aliases={i: 2 + i for i in range(ns + nl)},
        compiler_params=pltpu.CompilerParams(has_side_effects=DATAFLOW),
    )(*[_hbm(a) for a in srcs], *[_hbm(a) for a in lands], after)
    return out[0], out[1], list(out[2:2 + ns]), list(out[2 + ns:2 + ns + nl]), out[-1]


def _split_wait(name, handle, ncopies, plan, after):
    send_sems, recv_sems, srcs, lands, _ = handle
    ns, nl = len(srcs), len(lands)

    def body(*refs):
        s_sems, r_sems = refs[ns + nl], refs[ns + nl + 1]
        for k, (src, dst, peer, mine) in enumerate(plan(refs[:ns], refs[ns:ns + nl])):
            pltpu.make_async_remote_copy(src_ref=src, dst_ref=dst, send_sem=s_sems.at[k], recv_sem=r_sems.at[k],
                                         device_id=peer, device_id_type=MESH).wait_send()
            pltpu.make_async_remote_copy(src_ref=src, dst_ref=mine, send_sem=s_sems.at[k], recv_sem=r_sems.at[k],
                                         device_id=peer, device_id_type=MESH).wait_recv()

    out = pl.pallas_call(
        body, name=name,
        out_shape=(*[pltpu.HBM(a.shape, a.dtype) for a in srcs], *[pltpu.HBM(a.shape, a.dtype) for a in lands]),
        in_specs=[HBM] * (ns + nl) + [SEM, SEM, ANY],
        out_specs=tuple([HBM] * (ns + nl)),
        input_output_aliases={i: i for i in range(ns + nl)},
        compiler_params=pltpu.CompilerParams(has_side_effects=DATAFLOW),
    )(*srcs, *lands, send_sems, recv_sems, after)
    return list(out[:ns]), list(out[ns:])


def _slot(px, py, pc):
    return 4 * px + 2 * py + pc


def _gather_plan(xs, lands):
    x, y, c = _place()
    peers = [(x, y, 1 - c), (1 - x, y, c), (x, 1 - y, c), (1 - x, 1 - y, c)]
    return [(xs[t], lands[t].at[_slot(x, y, c)], peer, lands[t].at[_slot(*peer)])
            for t in range(len(xs)) for peer in peers]


def _gather_start(name, shards, after):
    lands = [lax.empty((N_DEV,) + s.shape, s.dtype) for s in shards]
    return _split_start(name, shards, lands, 4 * len(shards), _gather_plan, after)


def _pass_on_plan(_, lands):
    x, y, c = _place()
    blocks = [((1 - x, y, c), (1 - x, y, 1 - c)), ((x, 1 - y, c), (x, 1 - y, 1 - c)),
              ((1 - x, 1 - y, c), (1 - x, 1 - y, 1 - c)), ((x, y, 1 - c), (x, y, c))]
    return [(lands[t].at[_slot(*out)], lands[t].at[_slot(*out)], (x, y, 1 - c), lands[t].at[_slot(*back)])
            for t in range(len(lands)) for out, back in blocks]


def _gather_pass_on(name, handle, after):
    n = len(handle[2])
    _, lands = _split_wait(name + "_wait", handle, 4 * n, _gather_plan, after)
    return _split_start(name + "_pass_on", [], lands, 4 * n, _pass_on_plan, after)


def _gather_finish(name, handle, after):
    n = len(handle[3])
    _, lands = _split_wait(name + "_done", handle, 4 * n, _pass_on_plan, after)
    return lands


def _sibling_plan(ps, gots):
    x, y, c = _place()
    return [(ps[t].at[j, 1 - c], gots[t].at[j], (x, y, 1 - c), gots[t].at[j]) for t in range(len(ps)) for j in range(4)]


def _chips_plan(ss, gots):
    x, y, c = _place()
    chips = [(1 - x, y), (x, 1 - y), (1 - x, 1 - y)]
    return [(ss[t].at[2 * px + py], gots[t].at[k], (px, py, c), gots[t].at[k])
            for t in range(len(ss)) for k, (px, py) in enumerate(chips)]


def _sum_tile(R, C):
    return _tile(R, max(8, min(1024, ROW_BLOCK_BYTES // (C * 4))), 8)


def _pair_sum(name, p4, got, core):
    _, _, R, C = p4.shape
    tr = _sum_tile(R, C)

    def body(core_ref, p_ref, g_ref, o_ref):
        o_ref[...] = p_ref[:, 0] + g_ref[...]

    return pl.pallas_call(
        body, name=name, out_shape=jax.ShapeDtypeStruct((4, R, C), F32),
        grid_spec=pltpu.PrefetchScalarGridSpec(
            num_scalar_prefetch=1, grid=(4, R // tr),
            in_specs=[pl.BlockSpec((1, 1, tr, C), lambda j, i, core: (j, core[0], i, 0)),
                      pl.BlockSpec((1, tr, C), lambda j, i, core: (j, i, 0))],
            out_specs=pl.BlockSpec((1, tr, C), lambda j, i, core: (j, i, 0))),
        compiler_params=_params(("parallel", "parallel")),
    )(core, p4, got)


def _sum_devices(name, g8):
    _, R, C = g8.shape
    tr = _tile(R, SMALL_ROW_ALIGN, 8)

    def body(g_ref, o_ref):
        acc = g_ref[0]
        for d in range(1, N_DEV):
            acc = acc + g_ref[d]
        o_ref[...] = acc

    return pl.pallas_call(
        body, name=name, out_shape=jax.ShapeDtypeStruct((R, C), F32), grid=(R // tr,),
        in_specs=[pl.BlockSpec((N_DEV, tr, C), lambda i: (0, i, 0))], out_specs=pl.BlockSpec((tr, C), lambda i: (i, 0)),
        compiler_params=_params(("parallel",)),
    )(g8)


def _adamw_shard(name, layer, w, m, v, s4, got, chip, bufs):
    _, R, C = w.shape
    tr = _sum_tile(R, C) // 2 if _sum_tile(R, C) % 16 == 0 else _sum_tile(R, C)

    def body(chip_ref, w_ref, m_ref, v_ref, s_ref, g_ref, b0, b1, b2, b3, og, od, om, ov):
        g = ((s_ref[0] + g_ref[0]) + g_ref[1]) + g_ref[2]
        d, nm, nv = _adamw_fn(w_ref[0], g, m_ref[0], v_ref[0])
        og[0], od[0], om[0], ov[0] = g, d, nm, nv

    lay = pl.BlockSpec((1, tr, C), lambda i, chip: (layer, i, 0))
    if bufs is None:
        bufs = [lax.empty(w.shape, F32) for _ in range(4)]
    return pl.pallas_call(
        body, name=name, out_shape=[jax.ShapeDtypeStruct(w.shape, F32)] * 4,
        grid_spec=pltpu.PrefetchScalarGridSpec(
            num_scalar_prefetch=1, grid=(R // tr,),
            in_specs=[lay, lay, lay, pl.BlockSpec((1, tr, C), lambda i, chip: (chip[0], i, 0)),
                      pl.BlockSpec((3, tr, C), lambda i, chip: (0, i, 0)), ANY, ANY, ANY, ANY],
            out_specs=[lay] * 4),
        input_output_aliases={6: 0, 7: 1, 8: 2, 9: 3},
        compiler_params=_params(("parallel",)),
    )(chip, w, m, v, s4, got, *bufs)


def _adamw(name, wt, g, m, v):
    shape = wt.shape
    two = (lambda a: a.reshape(1, -1)) if wt.ndim == 1 else (lambda a: a.reshape(-1, shape[-1]))
    w2, g2, m2, v2 = two(wt), two(g), two(m), two(v)
    tr = _row(w2, None, 0)[2]
    outs = [(w2.shape, F32, 0, tr)] * 3
    d, nm, nv = _rows_fwd(name, _adamw_fn, [_row(a, tr, 0) for a in (w2, g2, m2, v2)], [], outs)
    return d.reshape(shape), nm.reshape(shape), nv.reshape(shape)


def _lane_tiling():
    return (jnp.arange(SSM_N)[:, None] == (jnp.arange(SSM_S) % SSM_N)[None, :]).astype(BF16)


def _own_block():
    r = lax.broadcasted_iota(jnp.int32, (SSM_G * SSM_P, SSM_S), 0) // SSM_P
    c = lax.broadcasted_iota(jnp.int32, (SSM_G * SSM_P, SSM_S), 1) // SSM_N
    return r == c


def _bd_build(name, v_re, v_im, sign, tiling):
    def body(r_ref, i_ref, t_ref, o_ref):
        own = _own_block()
        o_ref[:, :SSM_S] = jnp.where(own, _dot(r_ref[...].astype(BF16), t_ref[...]), 0.0).astype(BF16)
        o_ref[:, SSM_S:] = jnp.where(own, sign * _dot(i_ref[...].astype(BF16), t_ref[...]), 0.0).astype(BF16)

    rows = SSM_G * SSM_P
    return pl.pallas_call(
        body, name=name, out_shape=jax.ShapeDtypeStruct((rows, 2 * SSM_S), BF16), grid=(1,),
        in_specs=[_full_spec((rows, SSM_N))] * 2 + [_full_spec((SSM_N, SSM_S))], out_specs=_full_spec((rows, 2 * SSM_S)),
        compiler_params=_params(("arbitrary",)),
    )(v_re, v_im, tiling)


def _bd_extract(name, m, sign, tiling):
    def body(m_ref, t_ref, r_ref, i_ref):
        own, t = _own_block(), t_ref[...]

        def pick(x):
            x = jnp.where(own, x, 0.0)
            hi = x.astype(BF16)
            rest = x - hi.astype(F32)
            mid = rest.astype(BF16)
            lo = (rest - mid.astype(F32)).astype(BF16)
            return _dot(hi, t, NT) + _dot(mid, t, NT) + _dot(lo, t, NT)

        r_ref[...] = pick(m_ref[:, :SSM_S])
        i_ref[...] = sign * pick(m_ref[:, SSM_S:])

    rows = SSM_G * SSM_P
    return pl.pallas_call(
        body, name=name, out_shape=[jax.ShapeDtypeStruct((rows, SSM_N), F32)] * 2, grid=(1,),
        in_specs=[_full_spec((rows, 2 * SSM_S)), _full_spec((SSM_N, SSM_S))], out_specs=[_full_spec((rows, SSM_N))] * 2,
        compiler_params=_params(("arbitrary",)),
    )(m, tiling)


def _fold(a, dil):
    if dil == 1:
        return a
    return a.reshape((T // dil, dil) + a.shape[1:]).swapaxes(0, 1).reshape(a.shape)


def _unfold(a, dil):
    if dil == 1:
        return a
    return a.reshape((dil, T // dil) + a.shape[1:]).swapaxes(0, 1).reshape(a.shape)


DILS = (1, 4, 16)


def _fold3(parts):
    parts = [parts] * 3 if not isinstance(parts, (list, tuple)) else parts
    return jnp.stack([_fold(a, d) for a, d in zip(parts, DILS)])


def _unfold3(a):
    return [_unfold(a[p], d) for p, d in enumerate(DILS)]


def _rope_tables():
    half = ROPE // 2
    inv_freq = ROPE_THETA ** (-jnp.arange(half, dtype=F32) / half)
    ang = jnp.arange(T).astype(F32)[:, None] * inv_freq[None, :]
    return jnp.cos(ang), jnp.sin(ang)


def _from_blocks(a8):
    return a8.transpose(1, 0, 2).reshape(a8.shape[1], -1)


def _to_blocks(a):
    return a.reshape(a.shape[0], N_DEV, -1).transpose(1, 0, 2)


HALF = ROPE // 2


def _rope_rows(q1, q2, k1, k2, cos, sin):
    return [_row(q1, 256, 1), _row(q2, 256, 1), _row(k1, 256, 0), _row(k2, 256, 0)], [_row(cos, 256, 0), _row(sin, 256, 0)]


def _behind(a, tok):
    return a if tok is None else a + tok


def _mixer_fwd_in(x, w, sp, cos, sin):
    s = {}
    s['x'] = x
    h = _rms_fwd("rms_mix", x, sp['g_mix'])
    proj = _from_blocks(_mm("mm_in", h, w['w_in'], 'nt', bb='r', ob='c'))
    offs = np.cumsum((0,) + IN_SPLITS)
    c_q, c_kv, k_rope, u, qd, kd, vd = [proj[:, offs[i]:offs[i + 1]] for i in range(7)]
    s.update(h=h, c_q=c_q, c_kv=c_kv, u=u)

    cqn = _rms_fwd("rms_q", c_q, sp['g_q'])
    ckvn = _rms_fwd("rms_kv", c_kv, sp['g_kv'])
    q8 = _mm("mm_uq", cqn, w['w_uq'], 'nt', bb='r', ob='c')
    kv8 = _mm("mm_ukv", ckvn, w['w_ukv'], 'nn', bb='c', ob='c')
    rows, aux = _rope_rows(q8[:, :, NOPE:NOPE + HALF], q8[:, :, NOPE + HALF:], k_rope[:, :HALF], k_rope[:, HALF:], cos, sin)
    oq1, oq2, ok1, ok2 = _rows_fwd("rope", _rope_fn, rows + aux, [],
                                   [((H_MLA, T, HALF), F32, 1, 256)] * 2 + [((T, HALF), F32, 0, 256)] * 2)
    k_pe = jnp.concatenate([ok1, ok2], axis=-1)
    qh = jnp.concatenate([q8[:, :, :NOPE], oq1, oq2], axis=-1).astype(BF16)
    kh = jnp.concatenate([kv8[:, :, :NOPE], jnp.broadcast_to(k_pe[None], (H_MLA, T, ROPE))], axis=-1).astype(BF16)
    vh = kv8[:, :, NOPE:].astype(BF16)
    y_mla, lse_mla = _mla_fwd(qh, kh, vh)
    s.update(cqn=cqn, ckvn=ckvn, qh=qh, kh=kh, vh=vh, lse_mla=lse_mla, y_mla=y_mla, qd=qd, kd=kd, vd=vd)
    return s


def _mixer_fwd_out(s, w, sp, tok=None):
    x, u, y_mla, qd, kd, vd = s['x'], s['u'], s['y_mla'], s['qd'], s['kd'], s['vd']
    a3 = lambda n: sp[n].reshape(SSM_G, 1, SSM_N)
    b2 = lambda n: sp[n].transpose(0, 2, 1).reshape(SSM_G * SSM_P, SSM_N)
    disc_rows = [_row(a3('a_re'), 1, 0), _row(a3('a_im'), 1, 0), _row(sp['log_dt'].reshape(SSM_G, 1, 1), 1, 0),
                 _row(b2('b_re'), SSM_P, 0), _row(b2('b_im'), SSM_P, 0)]
    abr, abi, bbr, bbi = _rows_fwd(
        "s5_disc", _s5_disc_fn, disc_rows, [],
        [((SSM_G, 1, SSM_N), F32, 0, 1), ((SSM_G, 1, SSM_N), F32, 0, 1),
         ((SSM_G * SSM_P, SSM_N), F32, 0, SSM_P), ((SSM_G * SSM_P, SSM_N), F32, 0, SSM_P)])
    ar, ai = abr.reshape(1, SSM_S), abi.reshape(1, SSM_S)
    tiling = _lane_tiling()
    b_mat = _bd_build("s5_b_matrix", bbr, bbi, 1.0, tiling)
    c2 = lambda n: sp[n].reshape(SSM_G * SSM_P, SSM_N)
    c_mat = _bd_build("s5_c_matrix", c2('c_re'), c2('c_im'), -1.0, tiling)
    u16 = _behind(u, tok).astype(BF16)
    bu = _mm("mm_s5_b", u16, b_mat, 'nn')
    hst, hst16 = _scan_fwd(bu, ar, ai)
    ymm = _mm("mm_s5_c", hst16, c_mat, 'nt')
    d_row = sp['d_skip'].reshape(1, SSM_W)
    (yg,) = _rows_fwd("s5_act", _s5_act_fn, [_row(ymm), _row(u)], [d_row], [((T, SSM_W), BF16, -2, _row(u)[2])])
    z = _mm("mm_glu", yg, w['w_glu'], 'nn', bb='c')
    glu_rows = [_row(z[:, :SSM_W]), _row(z[:, SSM_W:])]
    glu_b = [sp['b_glu'][:SSM_W].reshape(1, -1), sp['b_glu'][SSM_W:].reshape(1, -1)]
    (y_ssm,) = _rows_fwd("s5_glu", _glu_fn, glu_rows, glu_b, [((T, SSM_W), F32, -2, glu_rows[0][2])])
    s.update(disc_rows=disc_rows, ar=ar, ai=ai, b_mat=b_mat, c_mat=c_mat, hst=hst, hst16=hst16, u16=u16, ymm=ymm,
             d_row=d_row, yg=yg,
             glu_rows=glu_rows, glu_b=glu_b)

    qf, kf, vf = [_fold3(a).astype(BF16) for a in (qd, kd, vd)]
    o_f, lse_f = _band_fwd(qf, kf, vf)
    mix_rows = [_row(a) for a in _unfold3(o_f) + _unfold3(lse_f)]
    (y_dil,) = _rows_fwd("dil_mix", _dil_mix_fn, mix_rows, [], [((T, DIL_W), F32, -2, mix_rows[0][2])])
    s.update(qf=qf, kf=kf, vf=vf, o_f=o_f, lse_f=lse_f, mix_rows=mix_rows)

    gm, gs, gd = sp['g_out_mla'].reshape(1, -1), sp['g_out_ssm'].reshape(1, -1), sp['g_out_dil'].reshape(1, -1)
    on_rows = [_row(y_mla), _row(y_ssm), _row(y_dil)]
    (ycat,) = _rows_fwd("out_norm", _outnorm_fn, on_rows, [gm, gs, gd], [((T, D), BF16, -2, on_rows[0][2])])
    x1_ = _mm("mm_o", ycat, w['w_o'], 'nn', bb='r', res=x)
    s.update(on_rows=on_rows, on_g=[gm, gs, gd], ycat=ycat, x1=x1_)
    for k in ('qd', 'kd', 'vd'):
        del s[k]
    return x1_


def _ffn_fwd(x1_, w, sp, s, tok=None):
    h2 = _rms_fwd("rms_ffn", x1_, _behind(sp['g_ffn'], tok))
    ga = _mm("mm_gate", h2, w['w_gate'], 'nt', bb='r', ob='c')
    gb = _mm("mm_up", h2, w['w_up'], 'nt', bb='r', ob='c')
    ffn_rows = [_row(ga, None, 1), _row(gb, None, 1)]
    (zf,) = _rows_fwd("swiglu", _swiglu_fn, ffn_rows, [], [(ga.shape, BF16, 1, ffn_rows[0][2])])
    x2_ = _mm("mm_down", zf, w['w_down'], 'nn', ab='c', bb='r', res=x1_)
    s.update(h2=h2, ffn_rows=ffn_rows, zf=zf)
    return x2_


def _b16(a):
    return a.astype(BF16)


def _ffn_bwd(dx2, s, w, sp, tok=None):
    gw, gs_ = {}, {}
    b16 = _b16
    dx2b = b16(_behind(dx2, tok))
    dzf = _mm("mm_down_dx", dx2b, w['w_down'], 'nt', bb='r', ob='c')
    gw['w_down'] = _mm("mm_down_dw", s['zf'], dx2b, 'tn', ab='c', ob='r')
    tr = s['ffn_rows'][0][2]
    dga, dgb = _rows_vjp("swiglu_bwd", _swiglu_fn, s['ffn_rows'], [], [], [_row(dzf, tr, 1)], grad_dtypes=[BF16, BF16])
    gw['w_gate'] = _mm("mm_gate_dw", dga, s['h2'], 'tn', ab='c', ob='r')
    gw['w_up'] = _mm("mm_up_dw", dgb, s['h2'], 'tn', ab='c', ob='r')
    dh2 = _mm("mm_up_dx", dgb, w['w_up'], 'nn', ab='c', bb='r',
              res=_mm("mm_gate_dx", dga, w['w_gate'], 'nn', ab='c', bb='r'))
    dx1, gs_['g_ffn'] = _rms_bwd("rms_ffn_bwd", s['x1'], sp['g_ffn'], dh2, dx2)
    return dx1, gw, gs_


def _mixer_bwd_out(dx1, s, w, sp, tok=None):
    gw, gs_ = {}, {}
    b16 = _b16
    dx1b = b16(_behind(dx1, tok))
    dycat = _mm("mm_o_dx", dx1b, w['w_o'], 'nt', bb='r')
    gw['w_o'] = _mm("mm_o_dw", s['ycat'], dx1b, 'tn', ob='r')
    dy_mla, dy_ssm, dy_dil, gs_['g_out_mla'], gs_['g_out_ssm'], gs_['g_out_dil'] = _rows_vjp(
        "out_norm_bwd", _outnorm_fn, s['on_rows'], s['on_g'], [], [_row(dycat)])

    dmix = _rows_vjp("dil_mix_bwd", _dil_mix_fn, s['mix_rows'], [], [], [_row(dy_dil)])
    dqf, dkf, dvf = _band_bwd(s['qf'], s['kf'], s['vf'], s['o_f'], s['lse_f'], _fold3(dmix[:3]), _fold3(dmix[3:]))
    back = lambda a: sum(_unfold3(a))
    dqd, dkd, dvd = back(dqf), back(dkf), back(dvf)

    dz1, dz2, db1, db2 = _rows_vjp("s5_glu_bwd", _glu_fn, s['glu_rows'], s['glu_b'], [], [_row(dy_ssm)])
    gs_['b_glu'] = jnp.concatenate([db1, db2], axis=1)
    dzb = b16(jnp.concatenate([dz1, dz2], axis=1))
    dyg = _mm("mm_glu_dx", dzb, w['w_glu'], 'nt', bb='c')
    gw['w_glu'] = _mm("mm_glu_dw", s['yg'], dzb, 'tn', ob='c')
    dymm, du_act, dd = _rows_vjp("s5_act_bwd", _s5_act_fn, [_row(s['ymm']), _row(s['u'])], [s['d_row']], [], [_row(dyg)],
                                 grad_dtypes=[BF16, F32])
    gs_['d_skip'] = dd
    dhst = _mm("mm_s5_c_dx", dymm, s['c_mat'], 'nn')
    dc_mat = _mm("mm_s5_c_dw", dymm, s['hst16'], 'tn')
    g, dar, dai = _scan_bwd(dhst, s['hst'], s['ar'], s['ai'])
    du = _mm("mm_s5_b_dx", g, s['b_mat'], 'nt', res=du_act)
    db_mat = _mm("mm_s5_b_dw", s['u16'], g, 'tn')
    tiling = _lane_tiling()
    gs_['c_re'], gs_['c_im'] = _bd_extract("s5_c_blocks", dc_mat, -1.0, tiling)
    dbbr, dbbi = _bd_extract("s5_b_blocks", db_mat, 1.0, tiling)
    disc_cts = [_row(dar.reshape(SSM_G, 1, SSM_N), 1, 0), _row(dai.reshape(SSM_G, 1, SSM_N), 1, 0),
                _row(dbbr, SSM_P, 0), _row(dbbi, SSM_P, 0)]
    da_re, da_im, dldt, db_r, db_i = _rows_vjp("s5_disc_bwd", _s5_disc_fn, s['disc_rows'], [], [], disc_cts)
    gs_['a_re'], gs_['a_im'], gs_['log_dt'] = da_re, da_im, dldt
    unb = lambda a: a.reshape(SSM_G, SSM_P, SSM_N).transpose(0, 2, 1)
    gs_['b_re'], gs_['b_im'] = unb(db_r), unb(db_i)
    return (dy_mla, du, dqd, dkd, dvd), gw, gs_


def _mixer_bwd_in(cts, dx1, s, w, sp, cos, sin, tok=None):
    gw, gs_ = {}, {}
    b16 = _b16
    dy_mla, du, dqd, dkd, dvd = cts
    dqh, dkh, dvh, dkpe = _mla_bwd(s['qh'], s['kh'], s['vh'], s['y_mla'], dy_mla, _behind(s['lse_mla'], tok))
    zq, zk = jnp.zeros((H_MLA, T, HALF), F32), jnp.zeros((T, HALF), F32)
    rows, aux = _rope_rows(zq, zq, zk, zk, cos, sin)
    cts, _ = _rope_rows(dqh[:, :, NOPE:NOPE + HALF], dqh[:, :, NOPE + HALF:], dkpe[:, :HALF], dkpe[:, HALF:], cos, sin)
    dq1, dq2, dk1, dk2 = _rows_vjp("rope_bwd", _rope_fn, rows, [], aux, cts)
    dq8 = b16(jnp.concatenate([dqh[:, :, :NOPE], dq1, dq2], axis=-1))
    dkv8 = b16(jnp.concatenate([dkh[:, :, :NOPE], dvh], axis=-1))
    dk_rope = jnp.concatenate([dk1, dk2], axis=-1)
    dcqn = _mm("mm_uq_dx", dq8, w['w_uq'], 'nn', ab='c', bb='r')
    gw['w_uq'] = _mm("mm_uq_dw", dq8, s['cqn'], 'tn', ab='c', ob='r')
    dckvn = _mm("mm_ukv_dx", dkv8, w['w_ukv'], 'nt', ab='c', bb='c')
    gw['w_ukv'] = _mm("mm_ukv_dw", s['ckvn'], dkv8, 'tn', bb='c', ob='c')
    dc_q, gs_['g_q'] = _rms_bwd("rms_q_bwd", s['c_q'], sp['g_q'], dcqn)
    dc_kv, gs_['g_kv'] = _rms_bwd("rms_kv_bwd", s['c_kv'], sp['g_kv'], dckvn)

    dproj = _to_blocks(b16(jnp.concatenate([dc_q, dc_kv, dk_rope, du, dqd, dkd, dvd], axis=1)))
    dh = _mm("mm_in_dx", dproj, w['w_in'], 'nn', ab='c', bb='r')
    gw['w_in'] = _mm("mm_in_dw", dproj, s['h'], 'tn', ab='c', ob='r')
    dx, gs_['g_mix'] = _rms_bwd("rms_mix_bwd", s['x'], sp['g_mix'], dh, dx1)
    return dx, gw, gs_


def kernel(x, g_mix, w_in, g_q, w_uq, g_kv, w_ukv, a_re, a_im, b_re, b_im, c_re, c_im, d_skip, log_dt, w_glu, b_glu, g_out_mla, g_out_ssm, g_out_dil, w_o, g_ffn, w_gate, w_up, w_down, g_final, loss_target, m_g_mix, m_w_in, m_g_q, m_w_uq, m_g_kv, m_w_ukv, m_a_re, m_a_im, m_b_re, m_b_im, m_c_re, m_c_im, m_d_skip, m_log_dt, m_w_glu, m_b_glu, m_g_out_mla, m_g_out_ssm, m_g_out_dil, m_w_o, m_g_ffn, m_w_gate, m_w_up, m_w_down, m_g_final, v_g_mix, v_w_in, v_g_q, v_w_uq, v_g_kv, v_w_ukv, v_a_re, v_a_im, v_b_re, v_b_im, v_c_re, v_c_im, v_d_skip, v_log_dt, v_w_glu, v_b_glu, v_g_out_mla, v_g_out_ssm, v_g_out_dil, v_w_o, v_g_ffn, v_w_gate, v_w_up, v_w_down, v_g_final):
    W = dict(zip(PARAMS, (g_mix, w_in, g_q, w_uq, g_kv, w_ukv, a_re, a_im, b_re, b_im, c_re, c_im, d_skip, log_dt,
                          w_glu, b_glu, g_out_mla, g_out_ssm, g_out_dil, w_o, g_ffn, w_gate, w_up, w_down, g_final)))
    M = dict(zip(PARAMS, (m_g_mix, m_w_in, m_g_q, m_w_uq, m_g_kv, m_w_ukv, m_a_re, m_a_im, m_b_re, m_b_im, m_c_re,
                          m_c_im, m_d_skip, m_log_dt, m_w_glu, m_b_glu, m_g_out_mla, m_g_out_ssm, m_g_out_dil, m_w_o,
                          m_g_ffn, m_w_gate, m_w_up, m_w_down, m_g_final)))
    V = dict(zip(PARAMS, (v_g_mix, v_w_in, v_g_q, v_w_uq, v_g_kv, v_w_ukv, v_a_re, v_a_im, v_b_re, v_b_im, v_c_re,
                          v_c_im, v_d_skip, v_log_dt, v_w_glu, v_b_glu, v_g_out_mla, v_g_out_ssm, v_g_out_dil, v_w_o,
                          v_g_ffn, v_w_gate, v_w_up, v_w_down, v_g_final)))
    cx, cy, cc = _place()
    core = cc.astype(jnp.int32).reshape(1)
    chip = (2 * cx + cy).astype(jnp.int32).reshape(1)
    cos, sin = _rope_tables()
    small = [{n: W[n][l] for n in SMALL} for l in range(DEPTH)]
    for sp in small:
        for n in ('g_mix', 'g_q', 'g_kv', 'g_ffn'):
            sp[n] = sp[n].reshape(1, -1)

    def tok_of(tokens):
        return sum(t[0, 0] for t in tokens) if tokens else None

    def shard_view(a, n):
        return a.swapaxes(1, 2) if BIG[n] == 't' else a

    def gather_start(l, group, names, after):
        return _gather_start(f"gather_{group}_start_{l}", [shard_view(W[n], n)[l].astype(BF16) for n in names], after)

    xa = x[0]
    h1_mix = gather_start(0, "mix", MIXER_W, jnp.zeros((8, LANES), F32))
    h1_ffn = gather_start(0, "ffn", FFN_W, h1_mix[4])
    h2_mix = _gather_pass_on("gather_mix_0", h1_mix, xa)
    saved, full = [], []
    tokens = [h2_mix[4]]
    for l in range(DEPTH):
        last = l + 1 == DEPTH
        wm = dict(zip(MIXER_W, _gather_finish(f"gather_mix_{l}", h2_mix, xa)))
        sp = dict(small[l])
        sp['g_mix'] = _behind(sp['g_mix'], tok_of(tokens))
        s = _mixer_fwd_in(xa, wm, sp, cos, sin)
        tokens = []
        if l == 0:
            h2_ffn = _gather_pass_on("gather_ffn_0", h1_ffn, s['y_mla'])
            tokens.append(h2_ffn[4])
        if not last:
            h1_mix = gather_start(l + 1, "mix", MIXER_W, s['y_mla'])
            h1_ffn = gather_start(l + 1, "ffn", FFN_W, h1_mix[4])
            tokens += [h1_mix[4], h1_ffn[4]]
        x1 = _mixer_fwd_out(s, wm, small[l], tok_of(tokens))
        tokens = []
        wf = dict(zip(FFN_W, _gather_finish(f"gather_ffn_{l}", h2_ffn, x1)))
        if not last:
            h2_mix = _gather_pass_on(f"gather_mix_{l + 1}", h1_mix, x1)
            tokens.append(h2_mix[4])
        xa = _ffn_fwd(x1, wf, small[l], s, tok_of(tokens))
        tokens = []
        if not last:
            h2_ffn = _gather_pass_on(f"gather_ffn_{l + 1}", h1_ffn, xa)
            tokens.append(h2_ffn[4])
        saved.append(s)
        full.append({**wm, **wf})
    gf = g_final.reshape(1, D)
    ones = jnp.ones((T, 1), F32)
    dxa, dgf, loss_rows = _rows_vjp("loss", _loss_fn, [_row(xa)], [gf], [_row(loss_target[0])], [_row(ones)],
                                    primal=True)
    loss = lax.psum(_sum_rows("loss_sum", loss_rows)[0, 0], ("x", "y", "c"))

    bufs = {n: None for n in BIG}
    pending = []

    def advance(dep):
        tokens = []
        for g in pending:
            names, tag = g['names'], g['tag']
            if g['stage'] == 0:
                p4 = [a.reshape((4, 2) + a.shape[1:]) for a in g['gw']]
                gots = [lax.empty((4,) + a.shape[1:], F32) for a in g['gw']]
                g['h'] = _split_start("rs_sibling_start_" + tag, p4, gots, 4 * len(p4), _sibling_plan, dep)
                tokens.append(g['h'][4])
            elif g['stage'] == 1:
                p4, gots = _split_wait("rs_sibling_wait_" + tag, g['h'], 4 * len(names), _sibling_plan, dep)
                s4 = [_pair_sum("rs_pair_sum_" + n, p, q, core) for n, p, q in zip(names, p4, gots)]
                gots = [lax.empty((3,) + a.shape[1:], F32) for a in s4]
                g['h'] = _split_start("rs_chips_start_" + tag, s4, gots, 3 * len(s4), _chips_plan, dep)
                tokens.append(g['h'][4])
            elif g['stage'] == 3:
                s4, gots = _split_wait("rs_chips_wait_" + tag, g['h'], 3 * len(names), _chips_plan, dep)
                for n, s4n, got in zip(names, s4, gots):
                    bufs[n] = _adamw_shard("adamw_" + n, g['layer'], shard_view(W[n], n), shard_view(M[n], n),
                                           shard_view(V[n], n), s4n, got, chip, bufs[n])
            g['stage'] += 1
        pending[:] = [g for g in pending if g['stage'] < 4]
        return tokens

    def group(names, l, gw, kind):
        return dict(names=names, layer=l, gw=[gw[n] for n in names], stage=0, tag=f"{kind}_{l}")

    g_small = [None] * DEPTH
    tokens = []
    for l in reversed(range(DEPTH)):
        dx1, gw_f, gs_f = _ffn_bwd(dxa, saved[l], full[l], small[l], tok_of(tokens))
        pending.append(group(FFN_W, l, gw_f, "ffn"))
        tokens = advance(dx1)
        cts, gw_o, gs_o = _mixer_bwd_out(dx1, saved[l], full[l], small[l], tok_of(tokens))
        pending.append(group(OUT_W, l, gw_o, "out"))
        tokens = advance(cts[0])
        dxa, gw_i, gs_i = _mixer_bwd_in(cts, dx1, saved[l], full[l], small[l], cos, sin, tok_of(tokens))
        pending.append(group(IN_W, l, gw_i, "in"))
        tokens = advance(dxa)
        g_small[l] = {**gs_f, **gs_o, **gs_i}

    flat = [g_small[l][n].reshape(-1) for l in range(DEPTH) for n in SMALL] + [dgf.reshape(-1)]
    n_small = sum(int(f.shape[0]) for f in flat)
    rows = -(-n_small // (PACK_C * SMALL_ROW_ALIGN)) * SMALL_ROW_ALIGN
    flat = jnp.concatenate(flat + [jnp.zeros((rows * PACK_C - n_small,), F32)]).reshape(rows, PACK_C)
    (gathered,) = _all_gather("gather_small", [flat])
    tot = _sum_devices("small_sum", gathered).reshape(-1)
    grads, off = {}, 0
    per_layer = {n: [] for n in SMALL}
    for l in range(DEPTH):
        for n, shp in SMALL.items():
            k = int(np.prod(shp))
            per_layer[n].append(tot[off:off + k].reshape(shp))
            off += k
    for n in SMALL:
        grads[n] = jnp.stack(per_layer[n])
    grads['g_final'] = tot[off:off + D]

    delta, new_m, new_v = {}, {}, {}
    advance(tot)
    for n in PARAMS:
        if n not in BIG:
            delta[n], new_m[n], new_v[n] = _adamw("adamw_" + n, W[n], grads[n], M[n], V[n])
    while pending:
        advance(delta['g_final'])
    for n in BIG:
        grads[n], delta[n], new_m[n], new_v[n] = [shard_view(b, n) for b in bufs[n]]
    return (loss, dxa[None], *[grads[n] for n in PARAMS], *[delta[n] for n in PARAMS],
            *[new_m[n] for n in PARAMS], *[new_v[n] for n in PARAMS])
```

```python
import jax
import jax.numpy as jnp
import numpy as np
from jax import lax
from jax.experimental import pallas as pl
from jax.experimental.pallas import tpu as pltpu

F32 = jnp.float32
BF16 = jnp.bfloat16

T = 2048
D = 2048
DEPTH = 4
N_DEV = 8
H_MLA, NOPE, ROPE, VDIM = 8, 128, 64, 128
QK = NOPE + ROPE
Q_LORA, KV_LORA = 512, 256
SSM_W, SSM_G, SSM_P, SSM_N = 512, 32, 16, 64
SSM_S = SSM_G * SSM_N
DIL_W, DIL_H, DIL_D = 512, 8, 64
BLK = 128
IN_SPLITS = (Q_LORA, KV_LORA, ROPE, SSM_W, DIL_W, DIL_W, DIL_W)
IN_W = sum(IN_SPLITS)
D_FF = 5632
EPS = 1e-6
ROPE_THETA = 10000.0
MLA_SCALE = QK ** -0.5
DIL_SCALE = DIL_D ** -0.5

ADAM_LR, ADAM_B1, ADAM_B2, ADAM_EPS, ADAM_WD, ADAM_STEP = 0.001, 0.9, 0.999, 1e-08, 0.01, 10

VMEM_LIMIT_V7X = 52 * 1024 * 1024
LANES = 128
PACK_C = 1024
ROW_BLOCK_BYTES = 2 * 1024 * 1024
MM_TM, MM_TN, MM_TK = 1024, 1024, 2048
MM_TB = 512

NT = (((1,), (1,)), ((), ()))
TN = (((0,), (0,)), ((), ()))
H_QK = (((2,), (2,)), ((0,), (0,)))
H_PV = (((2,), (1,)), ((0,), (0,)))
H_TN = (((1,), (1,)), ((0,), (0,)))
HI = lax.Precision.HIGHEST
MESH = pl.DeviceIdType.MESH

PARAMS = ['g_mix', 'w_in', 'g_q', 'w_uq', 'g_kv', 'w_ukv', 'a_re', 'a_im', 'b_re', 'b_im', 'c_re', 'c_im',
          'd_skip', 'log_dt', 'w_glu', 'b_glu', 'g_out_mla', 'g_out_ssm', 'g_out_dil', 'w_o', 'g_ffn',
          'w_gate', 'w_up', 'w_down', 'g_final']
BIG = {'w_in': 't', 'w_uq': 't', 'w_ukv': 'c', 'w_glu': 'c', 'w_o': 'r', 'w_gate': 't', 'w_up': 't', 'w_down': 'r'}
MIXER_W = ['w_in', 'w_uq', 'w_ukv', 'w_glu', 'w_o']
FFN_W = ['w_gate', 'w_up', 'w_down']
OUT_W, IN_W = ['w_o', 'w_glu'], ['w_in', 'w_uq', 'w_ukv']
SMALL = {'g_mix': (D,), 'g_q': (Q_LORA,), 'g_kv': (KV_LORA,), 'a_re': (SSM_G, SSM_N), 'a_im': (SSM_G, SSM_N),
         'b_re': (SSM_G, SSM_N, SSM_P), 'b_im': (SSM_G, SSM_N, SSM_P), 'c_re': (SSM_G, SSM_P, SSM_N),
         'c_im': (SSM_G, SSM_P, SSM_N), 'd_skip': (SSM_G, SSM_P), 'log_dt': (SSM_G,), 'b_glu': (2 * SSM_W,),
         'g_out_mla': (H_MLA * VDIM,), 'g_out_ssm': (SSM_W,), 'g_out_dil': (DIL_W,), 'g_ffn': (D,)}
SMALL_ROW_ALIGN = 64


def _tile(dim, target, align=LANES):
    best = None
    for t in range(align, min(dim, target) + 1, align):
        if dim % t == 0:
            best = t
    return best if best is not None else dim


def _params(sem=None):
    return pltpu.CompilerParams(dimension_semantics=sem, vmem_limit_bytes=VMEM_LIMIT_V7X)


def _dot(a, b, dims=None, prec=None):
    if dims is None:
        return jnp.dot(a, b, preferred_element_type=F32, precision=prec)
    return lax.dot_general(a, b, dims, preferred_element_type=F32, precision=prec)


def _mm_spec(shape, blk, t_r, t_c, rc):
    if blk is None:
        return pl.BlockSpec((t_r, t_c), rc)
    _, R, C = shape
    if blk == 'r':
        per = R // t_r
        return pl.BlockSpec((1, t_r, t_c), lambda i, j, k: (rc(i, j, k)[0] // per, rc(i, j, k)[0] % per, rc(i, j, k)[1]))
    per = C // t_c
    return pl.BlockSpec((1, t_r, t_c), lambda i, j, k: (rc(i, j, k)[1] // per, rc(i, j, k)[0], rc(i, j, k)[1] % per))


def _logical(shape, blk):
    if blk is None:
        return tuple(shape)
    G, R, C = shape
    return (G * R, C) if blk == 'r' else (R, G * C)


def _mm(name, a, b, mode, ab=None, bb=None, ob=None, res=None, prec=None):
    la, lb = _logical(a.shape, ab), _logical(b.shape, bb)
    am, ak = (0, 1) if mode != 'tn' else (1, 0)
    bk, bn = (0, 1) if mode != 'nt' else (1, 0)
    M, K, N = la[am], la[ak], lb[bn]
    assert lb[bk] == K, (name, a.shape, b.shape, mode)
    if ob is None:
        out_shape = (M, N)
    elif ob == 'r':
        G = N_DEV
        out_shape = (G, M // G, N)
    else:
        G = N_DEV
        out_shape = (G, M, N // G)
    em = min(a.shape[-2:][am], out_shape[-2])
    en = min(b.shape[-2:][bn], out_shape[-1])
    ek = min(a.shape[-2:][ak], b.shape[-2:][bk])
    dims = {'nn': None, 'nt': NT, 'tn': TN}[mode]
    a_kb = mode != 'tn' and ab == 'c'
    b_kb = (mode == 'nn' and bb == 'r') or (mode == 'nt' and bb == 'c')
    blocks = K // ek if (a_kb or b_kb) else 1
    tn = _tile(en, MM_TB if blocks > 1 else MM_TN)
    tm = _tile(em, MM_TB if blocks > 1 else (2 * MM_TM if K <= MM_TB else MM_TM))
    assert blocks == 1 or ((a_kb or ab is None) and (b_kb or bb is None)), (name, ab, bb, mode)
    tk = ek if blocks > 1 else _tile(ek, MM_TK)
    nk = 1 if blocks > 1 else K // tk

    def val(ref):
        return ref[...] if len(ref.shape) == 2 else ref[0]

    def put(o_ref, r):
        if len(o_ref.shape) == 2:
            o_ref[...] = r
        else:
            o_ref[0] = r

    def k_block(ref, d, blocked, lanes):
        if blocked:
            return ref[d]
        return ref[:, d * ek:(d + 1) * ek] if lanes else ref[d * ek:(d + 1) * ek, :]

    def body(*refs):
        if res is None:
            a_ref, b_ref, o_ref = refs[:3]
            r_ref = None
        else:
            a_ref, b_ref, r_ref, o_ref = refs[:4]
        if blocks > 1:
            part = None
            for d in range(blocks):
                p = _dot(k_block(a_ref, d, a_kb, True), k_block(b_ref, d, b_kb, mode == 'nt'), dims, prec)
                part = p if part is None else part + p
        else:
            part = _dot(val(a_ref), val(b_ref), dims, prec)
        if nk == 1:
            put(o_ref, part if r_ref is None else part + val(r_ref))
            return
        acc_ref = refs[-1]
        k = pl.program_id(2)

        @pl.when(k == 0)
        def _():
            acc_ref[...] = part

        @pl.when((k > 0) & (k < nk - 1))
        def _():
            acc_ref[...] += part

        @pl.when(k == nk - 1)
        def _():
            r = acc_ref[...] + part
            put(o_ref, r if r_ref is None else r + val(r_ref))

    if blocks > 1:
        G = blocks
        a_spec = (pl.BlockSpec((G, tm, ek), lambda i, j, k: (0, i, 0)) if a_kb
                  else pl.BlockSpec((tm, K), lambda i, j, k: (i, 0)))
        if b_kb:
            b_spec = (pl.BlockSpec((G, ek, tn), lambda i, j, k: (0, 0, j)) if mode == 'nn'
                      else pl.BlockSpec((G, tn, ek), lambda i, j, k: (0, j, 0)))
        else:
            b_spec = (pl.BlockSpec((K, tn), lambda i, j, k: (0, j)) if mode == 'nn'
                      else pl.BlockSpec((tn, K), lambda i, j, k: (j, 0)))
    else:
        if mode == 'tn':
            a_spec = _mm_spec(a.shape, ab, tk, tm, lambda i, j, k: (k, i))
        else:
            a_spec = _mm_spec(a.shape, ab, tm, tk, lambda i, j, k: (i, k))
        if mode == 'nt':
            b_spec = _mm_spec(b.shape, bb, tn, tk, lambda i, j, k: (j, k))
        else:
            b_spec = _mm_spec(b.shape, bb, tk, tn, lambda i, j, k: (k, j))
    o_spec = _mm_spec(out_shape, ob, tm, tn, lambda i, j, k: (i, j))
    in_specs = [a_spec, b_spec] + ([o_spec] if res is not None else [])
    args = (a, b) + ((res,) if res is not None else ())
    return pl.pallas_call(
        body, name=name, out_shape=jax.ShapeDtypeStruct(out_shape, F32),
        grid=(M // tm, N // tn, nk), in_specs=in_specs, out_specs=o_spec,
        scratch_shapes=[pltpu.VMEM((tm, tn), F32)] if nk > 1 else [],
        compiler_params=_params(("parallel", "parallel", "arbitrary")),
    )(*args)


def _row(a, tr=None, axis=-2):
    axis = axis % a.ndim
    n = a.shape[axis]
    if tr is None:
        row_bytes = a.size // n * 4
        tr = _tile(n, max(8, min(256, ROW_BLOCK_BYTES // row_bytes)), 8)
    return (a, axis, tr)


def _row_spec(shape, axis, tr):
    nd = len(shape)
    blk = tuple(tr if d == axis else s for d, s in enumerate(shape))
    return pl.BlockSpec(blk, lambda i: tuple(i if d == axis else 0 for d in range(nd)))


def _full_spec(shape):
    nd = len(shape)
    return pl.BlockSpec(tuple(shape), lambda i: (0,) * nd)


def _steps(entries):
    ns = {a.shape[ax] // tr for a, ax, tr in entries}
    assert len(ns) == 1, [(a.shape, ax, tr) for a, ax, tr in entries]
    return ns.pop()


def _as_tuple(r):
    return tuple(r) if isinstance(r, (tuple, list)) else (r,)


def _rows_fwd(name, fn, rows, bcast, outs):
    steps = _steps(rows)
    nr, nb = len(rows), len(bcast)

    def body(*refs):
        vals = [r[...] for r in refs[:nr + nb]]
        res = _as_tuple(fn(*vals))
        for o_ref, r in zip(refs[nr + nb:], res):
            o_ref[...] = r.astype(o_ref.dtype)

    in_specs = [_row_spec(a.shape, ax, tr) for a, ax, tr in rows] + [_full_spec(b.shape) for b in bcast]
    out_specs = [_row_spec(s, ax % len(s), tr) for s, _, ax, tr in outs]
    res = pl.pallas_call(
        body, name=name, out_shape=[jax.ShapeDtypeStruct(s, dt) for s, dt, _, _ in outs],
        grid=(steps,), in_specs=in_specs, out_specs=out_specs,
        compiler_params=_params(("parallel",)),
    )(*[a for a, _, _ in rows], *bcast)
    return res


def _rows_vjp(name, fn, drows, dbc, arows, cts, primal=False, grad_dtypes=None):
    entries = list(drows) + list(arows) + list(cts)
    steps = _steps(entries)
    ndr, ndb, nar, nct = len(drows), len(dbc), len(arows), len(cts)
    gdt = list(grad_dtypes) if grad_dtypes is not None else [F32] * ndr

    def body(*refs):
        p = 0
        dr = [r[...] for r in refs[p:p + ndr]]; p += ndr
        db = [r[...] for r in refs[p:p + ndb]]; p += ndb
        ar = [r[...] for r in refs[p:p + nar]]; p += nar
        ct = [r[...] for r in refs[p:p + nct]]; p += nct
        g_rows = refs[p:p + ndr]; p += ndr
        g_bc = refs[p:p + ndb]; p += ndb
        prim_refs = refs[p:]

        def f(*d):
            return _as_tuple(fn(*d, *ar))

        outs, pullback = jax.vjp(f, *dr, *db)
        grads = pullback(tuple(c.astype(o.dtype) for c, o in zip(ct, outs)))
        for k in range(ndr):
            g_rows[k][...] = grads[k].astype(g_rows[k].dtype)
        if ndb:
            @pl.when(pl.program_id(0) == 0)
            def _():
                for r in g_bc:
                    r[...] = jnp.zeros_like(r)
            for k in range(ndb):
                g_bc[k][...] += grads[ndr + k]
        for r, o in zip(prim_refs, outs):
            r[...] = o.astype(r.dtype)

    in_specs = ([_row_spec(a.shape, ax, tr) for a, ax, tr in drows] + [_full_spec(b.shape) for b in dbc]
                + [_row_spec(a.shape, ax, tr) for a, ax, tr in arows]
                + [_row_spec(a.shape, ax, tr) for a, ax, tr in cts])
    out_shape = ([jax.ShapeDtypeStruct(a.shape, dt) for (a, _, _), dt in zip(drows, gdt)]
                 + [jax.ShapeDtypeStruct(b.shape, F32) for b in dbc])
    out_specs = ([_row_spec(a.shape, ax, tr) for a, ax, tr in drows] + [_full_spec(b.shape) for b in dbc])
    if primal:
        out_shape += [jax.ShapeDtypeStruct(a.shape, F32) for a, _, _ in cts]
        out_specs += [_row_spec(a.shape, ax, tr) for a, ax, tr in cts]
    return pl.pallas_call(
        body, name=name, out_shape=out_shape, grid=(steps,), in_specs=in_specs, out_specs=out_specs,
        compiler_params=_params(("arbitrary",)),
    )(*[a for a, _, _ in drows], *dbc, *[a for a, _, _ in arows], *[a for a, _, _ in cts])


def _rms_fn(x, g):
    return x * lax.rsqrt(jnp.mean(x * x, axis=-1, keepdims=True) + EPS) * g


def _rms_res_fn(x, g):
    return _rms_fn(x, g), x


def _s5_act_fn(ymm, u, d):
    return jax.nn.gelu(ymm + d * u)


def _glu_fn(z1, z2, b1, b2):
    return (z1 + b1) * jax.nn.sigmoid(z2 + b2)


def _outnorm_fn(ym, ys, yd, gm, gs, gd):
    return jnp.concatenate([_rms_fn(ym, gm), _rms_fn(ys, gs), _rms_fn(yd, gd)], axis=-1)


def _swiglu_fn(a, b):
    return jax.nn.silu(a) * b


def _loss_fn(x, g, tgt):
    err = _rms_fn(x, g) - tgt
    return 0.5 * jnp.mean(err * err, axis=-1, keepdims=True)


def _dil_mix_fn(o0, o1, o2, l0, l1, l2):
    m = jnp.maximum(jnp.maximum(l0, l1), l2)
    e0, e1, e2 = jnp.exp(l0 - m), jnp.exp(l1 - m), jnp.exp(l2 - m)
    s = e0 + e1 + e2
    return (e0 / s) * o0 + (e1 / s) * o1 + (e2 / s) * o2


def _s5_disc_fn(a_re, a_im, ldt, b_r, b_i):
    lr = jnp.minimum(a_re.reshape(1, SSM_N), -1e-4)
    li = a_im.reshape(1, SSM_N)
    dt = jnp.exp(ldt.reshape(1, 1))
    e = jnp.exp(lr * dt)
    ar = e * jnp.cos(li * dt)
    ai = e * jnp.sin(li * dt)
    nr, ni = ar - 1.0, ai
    den = lr * lr + li * li
    cr = (nr * lr + ni * li) / den
    ci = (ni * lr - nr * li) / den
    return ar.reshape(1, 1, SSM_N), ai.reshape(1, 1, SSM_N), cr * b_r - ci * b_i, cr * b_i + ci * b_r


def _adamw_fn(w, g, m, v):
    m = ADAM_B1 * m + (1.0 - ADAM_B1) * g
    v = ADAM_B2 * v + (1.0 - ADAM_B2) * jnp.square(g)
    m_hat = m / (1.0 - ADAM_B1 ** ADAM_STEP)
    v_hat = v / (1.0 - ADAM_B2 ** ADAM_STEP)
    delta = -ADAM_LR * (m_hat / (jnp.sqrt(v_hat) + ADAM_EPS) + ADAM_WD * w)
    return delta, m, v


def _rms_fwd(name, x, g):
    (h,) = _rows_fwd(name, _rms_fn, [_row(x)], [g], [(x.shape, BF16, -2, _row(x)[2])])
    return h


def _rms_bwd(name, x, g, dh, dres=None):
    if dres is None:
        dx, dg = _rows_vjp(name, _rms_fn, [_row(x)], [g], [], [_row(dh)])
    else:
        dx, dg = _rows_vjp(name, _rms_res_fn, [_row(x)], [g], [], [_row(dh), _row(dres)])
    return dx, dg


MLA_TQ = 256
MLA_EXT = 512


def _mla_fwd(q, k, v):
    tq = MLA_TQ

    def body(q_ref, k_ref, v_ref, o_ref, lse_ref):
        i = pl.program_id(1)
        q = q_ref[0]

        def rows_below(ext):
            s = _dot(q, k_ref[0, :ext, :], NT) * MLA_SCALE
            row = i * tq + lax.broadcasted_iota(jnp.int32, (tq, ext), 0)
            col = lax.broadcasted_iota(jnp.int32, (tq, ext), 1)
            s = jnp.where(row >= col, s, -jnp.inf)
            m = jnp.max(s, axis=-1, keepdims=True)
            p = jnp.exp(s - m)
            l = jnp.sum(p, axis=-1, keepdims=True)
            o_ref[...] = _dot((p / l).astype(BF16), v_ref[0, :ext, :])
            lse_ref[0] = m + jnp.log(l)

        for g in range(T // MLA_EXT):
            pl.when(i // (MLA_EXT // tq) == g)(lambda g=g: rows_below((g + 1) * MLA_EXT))

    return pl.pallas_call(
        body, name="mla_fwd",
        out_shape=[jax.ShapeDtypeStruct((T, H_MLA * VDIM), F32), jax.ShapeDtypeStruct((H_MLA, T, 1), F32)],
        grid=(H_MLA, T // tq),
        in_specs=[pl.BlockSpec((1, tq, QK), lambda h, i: (h, i, 0)),
                  pl.BlockSpec((1, T, QK), lambda h, i: (h, 0, 0)),
                  pl.BlockSpec((1, T, VDIM), lambda h, i: (h, 0, 0))],
        out_specs=[pl.BlockSpec((tq, VDIM), lambda h, i: (i, h)),
                   pl.BlockSpec((1, tq, 1), lambda h, i: (h, i, 0))],
        compiler_params=_params(("parallel", "parallel")),
    )(q, k, v)


def _mla_bwd(q, k, v, o, do, lse):
    tq = MLA_TQ

    def body(q_ref, k_ref, v_ref, o_ref, do_ref, lse_ref, dq_ref, dk_ref, dv_ref):
        i = pl.program_id(1)

        @pl.when(i == 0)
        def _():
            dk_ref[...] = jnp.zeros_like(dk_ref)
            dv_ref[...] = jnp.zeros_like(dv_ref)

        q, lse = q_ref[0], lse_ref[0]
        delta = jnp.sum(do_ref[...] * o_ref[...], axis=-1, keepdims=True)
        do = do_ref[...].astype(BF16)

        def rows_below(ext):
            k, v = k_ref[0, :ext, :], v_ref[0, :ext, :]
            s = _dot(q, k, NT) * MLA_SCALE
            row = i * tq + lax.broadcasted_iota(jnp.int32, (tq, ext), 0)
            col = lax.broadcasted_iota(jnp.int32, (tq, ext), 1)
            p = jnp.where(row >= col, jnp.exp(s - lse), 0.0)
            ds = (p * (_dot(do, v, NT) - delta) * MLA_SCALE).astype(BF16)
            dq_ref[0] = _dot(ds, k)
            dk_ref[0, :ext, :] += _dot(ds, q, TN)
            dv_ref[0, :ext, :] += _dot(p.astype(BF16), do, TN)

        for g in range(T // MLA_EXT):
            pl.when(i // (MLA_EXT // tq) == g)(lambda g=g: rows_below((g + 1) * MLA_EXT))

    return pl.pallas_call(
        body, name="mla_bwd",
        out_shape=[jax.ShapeDtypeStruct((H_MLA, T, QK), F32), jax.ShapeDtypeStruct((H_MLA, T, QK), F32),
                   jax.ShapeDtypeStruct((H_MLA, T, VDIM), F32)],
        grid=(H_MLA, T // tq),
        in_specs=[pl.BlockSpec((1, tq, QK), lambda h, i: (h, i, 0)),
                  pl.BlockSpec((1, T, QK), lambda h, i: (h, 0, 0)),
                  pl.BlockSpec((1, T, VDIM), lambda h, i: (h, 0, 0)),
                  pl.BlockSpec((tq, VDIM), lambda h, i: (i, h)),
                  pl.BlockSpec((tq, VDIM), lambda h, i: (i, h)),
                  pl.BlockSpec((1, tq, 1), lambda h, i: (h, i, 0))],
        out_specs=[pl.BlockSpec((1, tq, QK), lambda h, i: (h, i, 0)),
                   pl.BlockSpec((1, T, QK), lambda h, i: (h, 0, 0)),
                   pl.BlockSpec((1, T, VDIM), lambda h, i: (h, 0, 0))],
        compiler_params=_params(("parallel", "arbitrary")),
    )(q, k, v, o, do, lse)


NBLK = T // BLK


BAND_GL = 256
BAND_GH = BAND_GL // DIL_D
BAND_ROWS = BAND_GH * BLK
BAND_GROUPS = DIL_W // BAND_GL


def _band_masks():
    r = lax.broadcasted_iota(jnp.int32, (BAND_ROWS, BLK), 0) & (BLK - 1)
    j = lax.broadcasted_iota(jnp.int32, (BAND_ROWS, BLK), 1)
    return j <= r, j >= r


def _head_lanes():
    lane_head = lax.broadcasted_iota(jnp.int32, (1, BAND_GL), 1) // DIL_D
    return [lane_head == h for h in range(BAND_GH)]


def _stack_heads(x, lanes):
    return jnp.concatenate([jnp.where(m, x, jnp.zeros_like(x)) for m in lanes], axis=0)


def _merge_heads(xs, lanes):
    out = None
    for h, m in enumerate(lanes):
        part = jnp.where(m, xs[h * BLK:(h + 1) * BLK], 0.0)
        out = part if out is None else out + part
    return out


def _per_head(x, lanes):
    return jnp.concatenate([jnp.sum(jnp.where(m, x, 0.0), axis=-1, keepdims=True) for m in lanes], axis=0)


def _lane_group(ref, g):
    return ref[0, :, g * BAND_GL:(g + 1) * BAND_GL]


def _seq_start(p, i):
    per_seq = lax.shift_right_logical(jnp.int32(NBLK), 2 * p)
    return lax.rem(i, per_seq) == 0


def _band_fwd(q, k, v):
    def body(q_ref, kp_ref, kc_ref, vp_ref, vc_ref, o_ref, lse_ref):
        p, i = pl.program_id(0), pl.program_id(1)
        has_prev = jnp.logical_not(_seq_start(p, i))
        m_cur, m_prev = _band_masks()
        m_prev = m_prev & has_prev
        lanes = _head_lanes()
        for g in range(BAND_GROUPS):
            qs = _stack_heads(_lane_group(q_ref, g), lanes)
            s_c = jnp.where(m_cur, _dot(qs, _lane_group(kc_ref, g), NT) * DIL_SCALE, -jnp.inf)
            s_p = jnp.where(m_prev, _dot(qs, _lane_group(kp_ref, g), NT) * DIL_SCALE, -jnp.inf)
            m = jnp.maximum(jnp.max(s_c, axis=-1, keepdims=True), jnp.max(s_p, axis=-1, keepdims=True))
            e_c, e_p = jnp.exp(s_c - m), jnp.exp(s_p - m)
            l = jnp.sum(e_c, axis=-1, keepdims=True) + jnp.sum(e_p, axis=-1, keepdims=True)
            os = (_dot((e_p / l).astype(BF16), _lane_group(vp_ref, g))
                  + _dot((e_c / l).astype(BF16), _lane_group(vc_ref, g)))
            cols = slice(g * BAND_GL, (g + 1) * BAND_GL)
            o_ref[0, :, cols] = _merge_heads(os, lanes)
            lse_ref[0, :, cols] = _merge_heads(m + jnp.log(l), lanes)

    blk = (1, BLK, DIL_W)
    cur = lambda p, i: (p, i, 0)
    prev = lambda p, i: (p, jnp.maximum(i - 1, 0), 0)
    return pl.pallas_call(
        body, name="band_fwd",
        out_shape=[jax.ShapeDtypeStruct((3, T, DIL_W), F32)] * 2,
        grid=(3, NBLK),
        in_specs=[pl.BlockSpec(blk, cur), pl.BlockSpec(blk, prev), pl.BlockSpec(blk, cur),
                  pl.BlockSpec(blk, prev), pl.BlockSpec(blk, cur)],
        out_specs=[pl.BlockSpec(blk, cur), pl.BlockSpec(blk, cur)],
        compiler_params=_params(("parallel", "parallel")),
    )(q, k, k, v, v)


def _band_bwd(q, k, v, o, lse, do, dlse):
    def body(qc_ref, qn_ref, kp_ref, kc_ref, vp_ref, vc_ref, oc_ref, on_ref, lc_ref, ln_ref,
             doc_ref, don_ref, dlc_ref, dln_ref, dq_ref, dk_ref, dv_ref):
        p, i = pl.program_id(0), pl.program_id(1)
        has_prev = jnp.logical_not(_seq_start(p, i))
        has_next = jnp.logical_not(_seq_start(p, i + 1)) & (i + 1 < NBLK)
        m_cur, m_prev = _band_masks()
        lanes = _head_lanes()

        def probs(qs, k, lse, mask):
            return jnp.where(mask, jnp.exp(_dot(qs, k, NT) * DIL_SCALE - lse), 0.0)

        def dscore(pr, dos, v, shift):
            return (pr * (_dot(dos, v, NT) + shift) * DIL_SCALE).astype(BF16)

        for g in range(BAND_GROUPS):
            grp = lambda ref: _lane_group(ref, g)
            kp, kc, vp, vc = grp(kp_ref), grp(kc_ref), grp(vp_ref), grp(vc_ref)
            qc, qn = _stack_heads(grp(qc_ref), lanes), _stack_heads(grp(qn_ref), lanes)
            doc, don = grp(doc_ref), grp(don_ref)
            lse_c = _per_head(grp(lc_ref), lanes) * (1.0 / DIL_D)
            lse_n = _per_head(grp(ln_ref), lanes) * (1.0 / DIL_D)
            sh_c = _per_head(grp(dlc_ref) - doc * grp(oc_ref), lanes)
            sh_n = _per_head(grp(dln_ref) - don * grp(on_ref), lanes)
            doc, don = _stack_heads(doc.astype(BF16), lanes), _stack_heads(don.astype(BF16), lanes)
            p_cc = probs(qc, kc, lse_c, m_cur)
            p_cp = probs(qc, kp, lse_c, m_prev & has_prev)
            p_nc = probs(qn, kc, lse_n, m_prev & has_next)
            ds_cc = dscore(p_cc, doc, vc, sh_c)
            ds_cp = dscore(p_cp, doc, vp, sh_c)
            ds_nc = dscore(p_nc, don, vc, sh_n)
            cols = slice(g * BAND_GL, (g + 1) * BAND_GL)
            dq_ref[0, :, cols] = _merge_heads(_dot(ds_cc, kc) + _dot(ds_cp, kp), lanes)
            dk_ref[0, :, cols] = _dot(ds_cc, qc, TN) + _dot(ds_nc, qn, TN)
            dv_ref[0, :, cols] = _dot(p_cc.astype(BF16), doc, TN) + _dot(p_nc.astype(BF16), don, TN)

    blk = (1, BLK, DIL_W)
    cur = lambda p, i: (p, i, 0)
    prev = lambda p, i: (p, jnp.maximum(i - 1, 0), 0)
    nxt = lambda p, i: (p, jnp.minimum(i + 1, NBLK - 1), 0)
    w, wn, wp = pl.BlockSpec(blk, cur), pl.BlockSpec(blk, nxt), pl.BlockSpec(blk, prev)
    return pl.pallas_call(
        body, name="band_bwd",
        out_shape=[jax.ShapeDtypeStruct((3, T, DIL_W), F32)] * 3,
        grid=(3, NBLK),
        in_specs=[w, wn, wp, w, wp, w, w, wn, w, wn, w, wn, w, wn],
        out_specs=[w, w, w],
        compiler_params=_params(("parallel", "parallel")),
    )(q, q, k, k, v, v, o, o, lse, lse, do, do, dlse, dlse)


SCAN_TC = 256


def _scan_fwd(bu, ar, ai):
    tc, S = SCAN_TC, SSM_S

    def body(bu_ref, ar_ref, ai_ref, h_ref, h16_ref, cr_ref, ci_ref):
        @pl.when(pl.program_id(0) == 0)
        def _():
            cr_ref[...] = jnp.zeros_like(cr_ref)
            ci_ref[...] = jnp.zeros_like(ci_ref)

        a_r, a_i = ar_ref[...], ai_ref[...]

        def step(j, carry):
            hr, hi = carry
            for r in range(8):
                t = pl.multiple_of(j * 8, 8) + r
                br = bu_ref[pl.ds(t, 1), pl.ds(0, S)]
                bi = bu_ref[pl.ds(t, 1), pl.ds(S, S)]
                hr, hi = a_r * hr - a_i * hi + br, a_r * hi + a_i * hr + bi
                h_ref[pl.ds(t, 1), pl.ds(0, S)] = hr
                h_ref[pl.ds(t, 1), pl.ds(S, S)] = hi
            return hr, hi

        hr, hi = lax.fori_loop(0, tc // 8, step, (cr_ref[...], ci_ref[...]))
        cr_ref[...] = hr
        ci_ref[...] = hi
        h16_ref[...] = h_ref[...].astype(BF16)

    return pl.pallas_call(
        body, name="s5_scan_fwd",
        out_shape=[jax.ShapeDtypeStruct((T, 2 * S), F32), jax.ShapeDtypeStruct((T, 2 * S), BF16)],
        grid=(T // tc,),
        in_specs=[pl.BlockSpec((tc, 2 * S), lambda i: (i, 0)), _full_spec((1, S)), _full_spec((1, S))],
        out_specs=[pl.BlockSpec((tc, 2 * S), lambda i: (i, 0)), pl.BlockSpec((tc, 2 * S), lambda i: (i, 0))],
        scratch_shapes=[pltpu.VMEM((1, S), F32), pltpu.VMEM((1, S), F32)],
        compiler_params=_params(("arbitrary",)),
    )(bu, ar, ai)


def _scan_bwd(dh, h, ar, ai):
    tc, S = SCAN_TC, SSM_S
    nc = T // tc

    def body(dh_ref, h_ref, hp_ref, ar_ref, ai_ref, g16_ref, dar_ref, dai_ref, cr_ref, ci_ref, g_ref):
        i = pl.program_id(0)

        @pl.when(i == 0)
        def _():
            cr_ref[...] = jnp.zeros_like(cr_ref)
            ci_ref[...] = jnp.zeros_like(ci_ref)
            dar_ref[...] = jnp.zeros_like(dar_ref)
            dai_ref[...] = jnp.zeros_like(dai_ref)

        a_r, a_i = ar_ref[...], ai_ref[...]
        first_chunk = (i == nc - 1)
        edge = jnp.where(first_chunk, 0.0, 1.0)
        hpr = hp_ref[pl.ds(7, 1), pl.ds(0, S)] * edge
        hpi = hp_ref[pl.ds(7, 1), pl.ds(S, S)] * edge

        def step(jj, carry):
            gr, gi, dar, dai = carry
            j = tc // 8 - 1 - jj
            for r in range(7, -1, -1):
                t = pl.multiple_of(j * 8, 8) + r
                tp = jnp.maximum(t - 1, 0)
                inside = t > 0
                pr = jnp.where(inside, h_ref[pl.ds(tp, 1), pl.ds(0, S)], hpr)
                pi = jnp.where(inside, h_ref[pl.ds(tp, 1), pl.ds(S, S)], hpi)
                gr, gi = (dh_ref[pl.ds(t, 1), pl.ds(0, S)] + a_r * gr + a_i * gi,
                          dh_ref[pl.ds(t, 1), pl.ds(S, S)] + a_r * gi - a_i * gr)
                g_ref[pl.ds(t, 1), pl.ds(0, S)] = gr
                g_ref[pl.ds(t, 1), pl.ds(S, S)] = gi
                dar = dar + gr * pr + gi * pi
                dai = dai + gi * pr - gr * pi
            return gr, gi, dar, dai

        zero = jnp.zeros((1, S), F32)
        gr, gi, dar, dai = lax.fori_loop(0, tc // 8, step, (cr_ref[...], ci_ref[...], zero, zero))
        cr_ref[...] = gr
        ci_ref[...] = gi
        dar_ref[...] += dar
        dai_ref[...] += dai
        g16_ref[...] = g_ref[...].astype(BF16)

    rev = lambda i: (nc - 1 - i, 0)
    before = lambda i: (jnp.maximum((nc - 1 - i) * (tc // 8) - 1, 0), 0)
    return pl.pallas_call(
        body, name="s5_scan_bwd",
        out_shape=[jax.ShapeDtypeStruct((T, 2 * S), BF16), jax.ShapeDtypeStruct((1, S), F32),
                   jax.ShapeDtypeStruct((1, S), F32)],
        grid=(nc,),
        in_specs=[pl.BlockSpec((tc, 2 * S), rev), pl.BlockSpec((tc, 2 * S), rev), pl.BlockSpec((8, 2 * S), before),
                  _full_spec((1, S)), _full_spec((1, S))],
        out_specs=[pl.BlockSpec((tc, 2 * S), rev), _full_spec((1, S)), _full_spec((1, S))],
        scratch_shapes=[pltpu.VMEM((1, S), F32), pltpu.VMEM((1, S), F32), pltpu.VMEM((tc, 2 * S), F32)],
        compiler_params=_params(("arbitrary",)),
    )(dh, h, h, ar, ai)


def _sum_rows(name, x):
    def body(x_ref, o_ref):
        o_ref[...] = jnp.sum(x_ref[...], axis=0, keepdims=True)

    return pl.pallas_call(body, name=name, out_shape=jax.ShapeDtypeStruct((1, 1), F32),
                          in_specs=[_full_spec(x.shape)], out_specs=_full_spec((1, 1)), grid=(1,))(x)


ANY = pl.BlockSpec(memory_space=pl.ANY)


def _place():
    return lax.axis_index("x"), lax.axis_index("y"), lax.axis_index("c")


def _all_gather(name, shards):
    n = len(shards)

    def body(*refs):
        xs, outs = refs[:n], refs[n:2 * n]
        send_sems, recv_sems, local_sems = refs[2 * n:]
        x, y, c = _place()
        me, sibling = (x, y, c), (x, y, 1 - c)
        chips = [(1 - x, y), (x, 1 - y), (1 - x, 1 - y)]

        def slot(t, px, py, pc):
            return outs[t].at[4 * px + 2 * py + pc]

        def copy(t, k, block, to, src=None):
            return pltpu.make_async_remote_copy(
                src_ref=slot(t, *block) if src is None else src, dst_ref=slot(t, *block),
                send_sem=send_sems.at[7 * t + k], recv_sem=recv_sems.at[7 * t + k], device_id=to, device_id_type=MESH)

        mine = [pltpu.make_async_copy(xs[t], slot(t, *me), local_sems.at[t]) for t in range(n)]
        sent = []
        for t in range(n):
            mine[t].start()
            sent.append(copy(t, 0, me, sibling, src=xs[t]))
            sent += [copy(t, 1 + j, me, (*chip, c), src=xs[t]) for j, chip in enumerate(chips)]
        for cp in sent:
            cp.start()
        for j, chip in enumerate(chips):
            for t in range(n):
                copy(t, 1 + j, (*chip, c), me).wait_recv()
                fwd = copy(t, 4 + j, (*chip, c), sibling)
                fwd.start()
                sent.append(fwd)
        for t in range(n):
            copy(t, 0, sibling, me).wait_recv()
            for j, chip in enumerate(chips):
                copy(t, 4 + j, (*chip, 1 - c), me).wait_recv()
        for cp in sent:
            cp.wait_send()
        for cp in mine:
            cp.wait()

    return pl.pallas_call(
        body, name=name, out_shape=[jax.ShapeDtypeStruct((N_DEV,) + s.shape, s.dtype) for s in shards],
        in_specs=[ANY] * n, out_specs=[ANY] * n,
        scratch_shapes=[pltpu.SemaphoreType.DMA((7 * n,)), pltpu.SemaphoreType.DMA((7 * n,)),
                        pltpu.SemaphoreType.DMA((n,))],
    )(*shards)


HBM = pl.BlockSpec(memory_space=pltpu.HBM)
SEM = pl.BlockSpec(memory_space=pltpu.SEMAPHORE)
DATAFLOW = pltpu.SideEffectType.DATAFLOW_SIDE_EFFECTING


def _hbm(a):
    return pltpu.with_memory_space_constraint(a, pltpu.HBM)


def _split_start(name, srcs, lands, ncopies, plan, after):
    ns, nl = len(srcs), len(lands)

    def body(*refs):
        send_sems, recv_sems = refs[ns + nl + 1], refs[ns + nl + 2]
        token = refs[-1]
        for k, (src, dst, peer, _) in enumerate(plan(refs[:ns], refs[ns:ns + nl])):
            pltpu.make_async_remote_copy(src_ref=src, dst_ref=dst, send_sem=send_sems.at[k], recv_sem=recv_sems.at[k],
                                         device_id=peer, device_id_type=MESH).start()
        token[...] = jnp.zeros_like(token)

    out = pl.pallas_call(
        body, name=name,
        out_shape=(pltpu.SemaphoreType.DMA((ncopies,)), pltpu.SemaphoreType.DMA((ncopies,)),
                   *[pltpu.HBM(a.shape, a.dtype) for a in srcs], *[pltpu.HBM(a.shape, a.dtype) for a in lands],
                   jax.ShapeDtypeStruct((8, LANES), F32)),
        in_specs=[HBM] * (ns + nl) + [ANY],
        out_specs=(SEM, SEM, *[HBM] * (ns + nl), pl.BlockSpec(memory_space=pltpu.VMEM)),
        input_output_aliases={i: 2 + i for i in range(ns + nl)},
        compiler_params=pltpu.CompilerParams(has_side_effects=DATAFLOW),
    )(*[_hbm(a) for a in srcs], *[_hbm(a) for a in lands], after)
    return out[0], out[1], list(out[2:2 + ns]), list(out[2 + ns:2 + ns + nl]), out[-1]


def _split_wait(name, handle, ncopies, plan, after):
    send_sems, recv_sems, srcs, lands, _ = handle
    ns, nl = len(srcs), len(lands)

    def body(*refs):
        s_sems, r_sems = refs[ns + nl], refs[ns + nl + 1]
        for k, (src, dst, peer, mine) in enumerate(plan(refs[:ns], refs[ns:ns + nl])):
            pltpu.make_async_remote_copy(src_ref=src, dst_ref=dst, send_sem=s_sems.at[k], recv_sem=r_sems.at[k],
                                         device_id=peer, device_id_type=MESH).wait_send()
            pltpu.make_async_remote_copy(src_ref=src, dst_ref=mine, send_sem=s_sems.at[k], recv_sem=r_sems.at[k],
                                         device_id=peer, device_id_type=MESH).wait_recv()

    out = pl.pallas_call(
        body, name=name,
        out_shape=(*[pltpu.HBM(a.shape, a.dtype) for a in srcs], *[pltpu.HBM(a.shape, a.dtype) for a in lands]),
        in_specs=[HBM] * (ns + nl) + [SEM, SEM, ANY],
        out_specs=tuple([HBM] * (ns + nl)),
        input_output_aliases={i: i for i in range(ns + nl)},
        compiler_params=pltpu.CompilerParams(has_side_effects=DATAFLOW),
    )(*srcs, *lands, send_sems, recv_sems, after)
    return list(out[:ns]), list(out[ns:])


def _slot(px, py, pc):
    return 4 * px + 2 * py + pc


def _gather_plan(xs, lands):
    x, y, c = _place()
    peers = [(x, y, 1 - c), (1 - x, y, c), (x, 1 - y, c), (1 - x, 1 - y, c)]
    return [(xs[t], lands[t].at[_slot(x, y, c)], peer, lands[t].at[_slot(*peer)])
            for t in range(len(xs)) for peer in peers]


def _gather_start(name, shards, after):
    lands = [lax.empty((N_DEV,) + s.shape, s.dtype) for s in shards]
    return _split_start(name, shards, lands, 4 * len(shards), _gather_plan, after)


def _pass_on_plan(_, lands):
    x, y, c = _place()
    blocks = [((1 - x, y, c), (1 - x, y, 1 - c)), ((x, 1 - y, c), (x, 1 - y, 1 - c)),
              ((1 - x, 1 - y, c), (1 - x, 1 - y, 1 - c)), ((x, y, 1 - c), (x, y, c))]
    return [(lands[t].at[_slot(*out)], lands[t].at[_slot(*out)], (x, y, 1 - c), lands[t].at[_slot(*back)])
            for t in range(len(lands)) for out, back in blocks]


def _gather_pass_on(name, handle, after):
    n = len(handle[2])
    _, lands = _split_wait(name + "_wait", handle, 4 * n, _gather_plan, after)
    return _split_start(name + "_pass_on", [], lands, 4 * n, _pass_on_plan, after)


def _gather_finish(name, handle, after):
    n = len(handle[3])
    _, lands = _split_wait(name + "_done", handle, 4 * n, _pass_on_plan, after)
    return lands


def _sibling_plan(ps, gots):
    x, y, c = _place()
    return [(ps[t].at[j, 1 - c], gots[t].at[j], (x, y, 1 - c), gots[t].at[j]) for t in range(len(ps)) for j in range(4)]


def _chips_plan(ss, gots):
    x, y, c = _place()
    chips = [(1 - x, y), (x, 1 - y), (1 - x, 1 - y)]
    return [(ss[t].at[2 * px + py], gots[t].at[k], (px, py, c), gots[t].at[k])
            for t in range(len(ss)) for k, (px, py) in enumerate(chips)]


def _sum_tile(R, C):
    return _tile(R, max(8, min(1024, ROW_BLOCK_BYTES // (C * 4))), 8)


def _pair_sum(name, p4, got, core):
    _, _, R, C = p4.shape
    tr = _sum_tile(R, C)

    def body(core_ref, p_ref, g_ref, o_ref):
        o_ref[...] = p_ref[:, 0] + g_ref[...]

    return pl.pallas_call(
        body, name=name, out_shape=jax.ShapeDtypeStruct((4, R, C), F32),
        grid_spec=pltpu.PrefetchScalarGridSpec(
            num_scalar_prefetch=1, grid=(4, R // tr),
            in_specs=[pl.BlockSpec((1, 1, tr, C), lambda j, i, core: (j, core[0], i, 0)),
                      pl.BlockSpec((1, tr, C), lambda j, i, core: (j, i, 0))],
            out_specs=pl.BlockSpec((1, tr, C), lambda j, i, core: (j, i, 0))),
        compiler_params=_params(("parallel", "parallel")),
    )(core, p4, got)


def _sum_devices(name, g8):
    _, R, C = g8.shape
    tr = _tile(R, SMALL_ROW_ALIGN, 8)

    def body(g_ref, o_ref):
        acc = g_ref[0]
        for d in range(1, N_DEV):
            acc = acc + g_ref[d]
        o_ref[...] = acc

    return pl.pallas_call(
        body, name=name, out_shape=jax.ShapeDtypeStruct((R, C), F32), grid=(R // tr,),
        in_specs=[pl.BlockSpec((N_DEV, tr, C), lambda i: (0, i, 0))], out_specs=pl.BlockSpec((tr, C), lambda i: (i, 0)),
        compiler_params=_params(("parallel",)),
    )(g8)


def _adamw_shard(name, layer, w, m, v, s4, got, chip, bufs):
    _, R, C = w.shape
    tr = _sum_tile(R, C) // 2 if _sum_tile(R, C) % 16 == 0 else _sum_tile(R, C)

    def body(chip_ref, w_ref, m_ref, v_ref, s_ref, g_ref, b0, b1, b2, b3, og, od, om, ov):
        g = ((s_ref[0] + g_ref[0]) + g_ref[1]) + g_ref[2]
        d, nm, nv = _adamw_fn(w_ref[0], g, m_ref[0], v_ref[0])
        og[0], od[0], om[0], ov[0] = g, d, nm, nv

    lay = pl.BlockSpec((1, tr, C), lambda i, chip: (layer, i, 0))
    if bufs is None:
        bufs = [lax.empty(w.shape, F32) for _ in range(4)]
    return pl.pallas_call(
        body, name=name, out_shape=[jax.ShapeDtypeStruct(w.shape, F32)] * 4,
        grid_spec=pltpu.PrefetchScalarGridSpec(
            num_scalar_prefetch=1, grid=(R // tr,),
            in_specs=[lay, lay, lay, pl.BlockSpec((1, tr, C), lambda i, chip: (chip[0], i, 0)),
                      pl.BlockSpec((3, tr, C), lambda i, chip: (0, i, 0)), ANY, ANY, ANY, ANY],
            out_specs=[lay] * 4),
        input_output_aliases={6: 0, 7: 1, 8: 2, 9: 3},
        compiler_params=_params(("parallel",)),
    )(chip, w, m, v, s4, got, *bufs)


def _adamw(name, wt, g, m, v):
    shape = wt.shape
    two = (lambda a: a.reshape(1, -1)) if wt.ndim == 1 else (lambda a: a.reshape(-1, shape[-1]))
    w2, g2, m2, v2 = two(wt), two(g), two(m), two(v)
    tr = _row(w2, None, 0)[2]
    outs = [(w2.shape, F32, 0, tr)] * 3
    d, nm, nv = _rows_fwd(name, _adamw_fn, [_row(a, tr, 0) for a in (w2, g2, m2, v2)], [], outs)
    return d.reshape(shape), nm.reshape(shape), nv.reshape(shape)


def _lane_tiling():
    return (jnp.arange(SSM_N)[:, None] == (jnp.arange(SSM_S) % SSM_N)[None, :]).astype(BF16)


def _own_block():
    r = lax.broadcasted_iota(jnp.int32, (SSM_G * SSM_P, SSM_S), 0) // SSM_P
    c = lax.broadcasted_iota(jnp.int32, (SSM_G * SSM_P, SSM_S), 1) // SSM_N
    return r == c


def _bd_build(name, v_re, v_im, sign, tiling):
    def body(r_ref, i_ref, t_ref, o_ref):
        own = _own_block()
        o_ref[:, :SSM_S] = jnp.where(own, _dot(r_ref[...].astype(BF16), t_ref[...]), 0.0).astype(BF16)
        o_ref[:, SSM_S:] = jnp.where(own, sign * _dot(i_ref[...].astype(BF16), t_ref[...]), 0.0).astype(BF16)

    rows = SSM_G * SSM_P
    return pl.pallas_call(
        body, name=name, out_shape=jax.ShapeDtypeStruct((rows, 2 * SSM_S), BF16), grid=(1,),
        in_specs=[_full_spec((rows, SSM_N))] * 2 + [_full_spec((SSM_N, SSM_S))], out_specs=_full_spec((rows, 2 * SSM_S)),
        compiler_params=_params(("arbitrary",)),
    )(v_re, v_im, tiling)


def _bd_extract(name, m, sign, tiling):
    def body(m_ref, t_ref, r_ref, i_ref):
        own, t = _own_block(), t_ref[...]

        def pick(x):
            x = jnp.where(own, x, 0.0)
            hi = x.astype(BF16)
            rest = x - hi.astype(F32)
            mid = rest.astype(BF16)
            lo = (rest - mid.astype(F32)).astype(BF16)
            return _dot(hi, t, NT) + _dot(mid, t, NT) + _dot(lo, t, NT)

        r_ref[...] = pick(m_ref[:, :SSM_S])
        i_ref[...] = sign * pick(m_ref[:, SSM_S:])

    rows = SSM_G * SSM_P
    return pl.pallas_call(
        body, name=name, out_shape=[jax.ShapeDtypeStruct((rows, SSM_N), F32)] * 2, grid=(1,),
        in_specs=[_full_spec((rows, 2 * SSM_S)), _full_spec((SSM_N, SSM_S))], out_specs=[_full_spec((rows, SSM_N))] * 2,
        compiler_params=_params(("arbitrary",)),
    )(m, tiling)


def _fold(a, dil):
    if dil == 1:
        return a
    return a.reshape((T // dil, dil) + a.shape[1:]).swapaxes(0, 1).reshape(a.shape)


def _unfold(a, dil):
    if dil == 1:
        return a
    return a.reshape((dil, T // dil) + a.shape[1:]).swapaxes(0, 1).reshape(a.shape)


DILS = (1, 4, 16)


def _fold3(parts):
    parts = [parts] * 3 if not isinstance(parts, (list, tuple)) else parts
    return jnp.stack([_fold(a, d) for a, d in zip(parts, DILS)])


def _unfold3(a):
    return [_unfold(a[p], d) for p, d in enumerate(DILS)]


def _rope_tables():
    half = ROPE // 2
    inv_freq = ROPE_THETA ** (-jnp.arange(half, dtype=F32) / half)
    ang = jnp.arange(T).astype(F32)[:, None] * inv_freq[None, :]
    i, j = jnp.arange(ROPE)[:, None], jnp.arange(ROPE)[None, :]
    rot = jnp.where(i == j + half, -1.0, jnp.where(i + half == j, 1.0, 0.0)).astype(F32)
    return jnp.tile(jnp.cos(ang), (1, 2)), jnp.tile(jnp.sin(ang), (1, 2)), rot


def _rot_half(x, rot):
    return _dot(x.reshape(-1, ROPE), rot, prec=HI).reshape(x.shape)


def _mla_pack_fn(q, kv, k_rope, cos, sin, rot):
    rope = lambda x: x * cos + _rot_half(x, rot) * sin
    q_out = jnp.concatenate([q[:, :, :NOPE], rope(q[:, :, NOPE:])], axis=-1)
    k_pe = jnp.broadcast_to(rope(k_rope)[None], (H_MLA,) + k_rope.shape)
    return q_out, jnp.concatenate([kv[:, :, :NOPE], k_pe], axis=-1), kv[:, :, NOPE:]


def _mla_unpack_fn(dq, dk, dv, cos, sin, rot):
    unrope = lambda g: g * cos - _rot_half(g * sin, rot)
    dq_out = jnp.concatenate([dq[:, :, :NOPE], unrope(dq[:, :, NOPE:])], axis=-1)
    dk_rope = unrope(jnp.sum(dk[:, :, NOPE:], axis=0))
    return dq_out, jnp.concatenate([dk[:, :, :NOPE], dv], axis=-1), dk_rope


def _from_blocks(a8):
    return a8.transpose(1, 0, 2).reshape(a8.shape[1], -1)


def _to_blocks(a):
    return a.reshape(a.shape[0], N_DEV, -1).transpose(1, 0, 2)


def _behind(a, tok):
    return a if tok is None else a + tok


def _mixer_fwd_in(x, w, sp, rope):
    s = {}
    s['x'] = x
    h = _rms_fwd("rms_mix", x, sp['g_mix'])
    proj = _from_blocks(_mm("mm_in", h, w['w_in'], 'nt', bb='r', ob='c'))
    offs = np.cumsum((0,) + IN_SPLITS)
    c_q, c_kv, k_rope, u, qd, kd, vd = [proj[:, offs[i]:offs[i + 1]] for i in range(7)]
    s.update(h=h, c_q=c_q, c_kv=c_kv, u=u)

    cqn = _rms_fwd("rms_q", c_q, sp['g_q'])
    ckvn = _rms_fwd("rms_kv", c_kv, sp['g_kv'])
    q8 = _mm("mm_uq", cqn, w['w_uq'], 'nt', bb='r', ob='c')
    kv8 = _mm("mm_ukv", ckvn, w['w_ukv'], 'nn', bb='c', ob='c')
    cos, sin, rot = rope
    tr = MLA_TQ
    qh, kh, vh = _rows_fwd(
        "mla_pack", _mla_pack_fn,
        [_row(q8, tr, 1), _row(kv8, tr, 1), _row(k_rope, tr, 0), _row(cos, tr, 0), _row(sin, tr, 0)], [rot],
        [((H_MLA, T, QK), BF16, 1, tr), ((H_MLA, T, QK), BF16, 1, tr), ((H_MLA, T, VDIM), BF16, 1, tr)])
    y_mla, lse_mla = _mla_fwd(qh, kh, vh)
    s.update(cqn=cqn, ckvn=ckvn, qh=qh, kh=kh, vh=vh, lse_mla=lse_mla, y_mla=y_mla, qd=qd, kd=kd, vd=vd)
    return s


def _mixer_fwd_out(s, w, sp, tok=None):
    x, u, y_mla, qd, kd, vd = s['x'], s['u'], s['y_mla'], s['qd'], s['kd'], s['vd']
    a3 = lambda n: sp[n].reshape(SSM_G, 1, SSM_N)
    b2 = lambda n: sp[n].transpose(0, 2, 1).reshape(SSM_G * SSM_P, SSM_N)
    disc_rows = [_row(a3('a_re'), 1, 0), _row(a3('a_im'), 1, 0), _row(sp['log_dt'].reshape(SSM_G, 1, 1), 1, 0),
                 _row(b2('b_re'), SSM_P, 0), _row(b2('b_im'), SSM_P, 0)]
    abr, abi, bbr, bbi = _rows_fwd(
        "s5_disc", _s5_disc_fn, disc_rows, [],
        [((SSM_G, 1, SSM_N), F32, 0, 1), ((SSM_G, 1, SSM_N), F32, 0, 1),
         ((SSM_G * SSM_P, SSM_N), F32, 0, SSM_P), ((SSM_G * SSM_P, SSM_N), F32, 0, SSM_P)])
    ar, ai = abr.reshape(1, SSM_S), abi.reshape(1, SSM_S)
    tiling = _lane_tiling()
    b_mat = _bd_build("s5_b_matrix", bbr, bbi, 1.0, tiling)
    c2 = lambda n: sp[n].reshape(SSM_G * SSM_P, SSM_N)
    c_mat = _bd_build("s5_c_matrix", c2('c_re'), c2('c_im'), -1.0, tiling)
    u16 = _behind(u, tok).astype(BF16)
    bu = _mm("mm_s5_b", u16, b_mat, 'nn')
    hst, hst16 = _scan_fwd(bu, ar, ai)
    ymm = _mm("mm_s5_c", hst16, c_mat, 'nt')
    d_row = sp['d_skip'].reshape(1, SSM_W)
    (yg,) = _rows_fwd("s5_act", _s5_act_fn, [_row(ymm), _row(u)], [d_row], [((T, SSM_W), BF16, -2, _row(u)[2])])
    z = _mm("mm_glu", yg, w['w_glu'], 'nn', bb='c')
    glu_rows = [_row(z[:, :SSM_W]), _row(z[:, SSM_W:])]
    glu_b = [sp['b_glu'][:SSM_W].reshape(1, -1), sp['b_glu'][SSM_W:].reshape(1, -1)]
    (y_ssm,) = _rows_fwd("s5_glu", _glu_fn, glu_rows, glu_b, [((T, SSM_W), F32, -2, glu_rows[0][2])])
    s.update(disc_rows=disc_rows, ar=ar, ai=ai, b_mat=b_mat, c_mat=c_mat, hst=hst, hst16=hst16, u16=u16, ymm=ymm,
             d_row=d_row, yg=yg,
             glu_rows=glu_rows, glu_b=glu_b)

    qf, kf, vf = [_fold3(a).astype(BF16) for a in (qd, kd, vd)]
    o_f, lse_f = _band_fwd(qf, kf, vf)
    mix_rows = [_row(a) for a in _unfold3(o_f) + _unfold3(lse_f)]
    (y_dil,) = _rows_fwd("dil_mix", _dil_mix_fn, mix_rows, [], [((T, DIL_W), F32, -2, mix_rows[0][2])])
    s.update(qf=qf, kf=kf, vf=vf, o_f=o_f, lse_f=lse_f, mix_rows=mix_rows)

    gm, gs, gd = sp['g_out_mla'].reshape(1, -1), sp['g_out_ssm'].reshape(1, -1), sp['g_out_dil'].reshape(1, -1)
    on_rows = [_row(y_mla), _row(y_ssm), _row(y_dil)]
    (ycat,) = _rows_fwd("out_norm", _outnorm_fn, on_rows, [gm, gs, gd], [((T, D), BF16, -2, on_rows[0][2])])
    x1_ = _mm("mm_o", ycat, w['w_o'], 'nn', bb='r', res=x)
    s.update(on_rows=on_rows, on_g=[gm, gs, gd], ycat=ycat, x1=x1_)
    for k in ('qd', 'kd', 'vd'):
        del s[k]
    return x1_


def _ffn_fwd(x1_, w, sp, s, tok=None):
    h2 = _rms_fwd("rms_ffn", x1_, _behind(sp['g_ffn'], tok))
    ga = _mm("mm_gate", h2, w['w_gate'], 'nt', bb='r', ob='c')
    gb = _mm("mm_up", h2, w['w_up'], 'nt', bb='r', ob='c')
    ffn_rows = [_row(ga, None, 1), _row(gb, None, 1)]
    (zf,) = _rows_fwd("swiglu", _swiglu_fn, ffn_rows, [], [(ga.shape, BF16, 1, ffn_rows[0][2])])
    x2_ = _mm("mm_down", zf, w['w_down'], 'nn', ab='c', bb='r', res=x1_)
    s.update(h2=h2, ffn_rows=ffn_rows, zf=zf)
    return x2_


def _b16(a):
    return a.astype(BF16)


def _ffn_bwd(dx2, s, w, sp, tok=None):
    gw, gs_ = {}, {}
    b16 = _b16
    dx2b = b16(_behind(dx2, tok))
    dzf = _mm("mm_down_dx", dx2b, w['w_down'], 'nt', bb='r', ob='c')
    gw['w_down'] = _mm("mm_down_dw", s['zf'], dx2b, 'tn', ab='c', ob='r')
    tr = s['ffn_rows'][0][2]
    dga, dgb = _rows_vjp("swiglu_bwd", _swiglu_fn, s['ffn_rows'], [], [], [_row(dzf, tr, 1)], grad_dtypes=[BF16, BF16])
    gw['w_gate'] = _mm("mm_gate_dw", dga, s['h2'], 'tn', ab='c', ob='r')
    gw['w_up'] = _mm("mm_up_dw", dgb, s['h2'], 'tn', ab='c', ob='r')
    dh2 = _mm("mm_up_dx", dgb, w['w_up'], 'nn', ab='c', bb='r',
              res=_mm("mm_gate_dx", dga, w['w_gate'], 'nn', ab='c', bb='r'))
    dx1, gs_['g_ffn'] = _rms_bwd("rms_ffn_bwd", s['x1'], sp['g_ffn'], dh2, dx2)
    return dx1, gw, gs_


def _mixer_bwd_out(dx1, s, w, sp, tok=None):
    gw, gs_ = {}, {}
    b16 = _b16
    dx1b = b16(_behind(dx1, tok))
    dycat = _mm("mm_o_dx", dx1b, w['w_o'], 'nt', bb='r')
    gw['w_o'] = _mm("mm_o_dw", s['ycat'], dx1b, 'tn', ob='r')
    dy_mla, dy_ssm, dy_dil, gs_['g_out_mla'], gs_['g_out_ssm'], gs_['g_out_dil'] = _rows_vjp(
        "out_norm_bwd", _outnorm_fn, s['on_rows'], s['on_g'], [], [_row(dycat)])

    dmix = _rows_vjp("dil_mix_bwd", _dil_mix_fn, s['mix_rows'], [], [], [_row(dy_dil)])
    dqf, dkf, dvf = _band_bwd(s['qf'], s['kf'], s['vf'], s['o_f'], s['lse_f'], _fold3(dmix[:3]), _fold3(dmix[3:]))
    back = lambda a: sum(_unfold3(a))
    dqd, dkd, dvd = back(dqf), back(dkf), back(dvf)

    dz1, dz2, db1, db2 = _rows_vjp("s5_glu_bwd", _glu_fn, s['glu_rows'], s['glu_b'], [], [_row(dy_ssm)])
    gs_['b_glu'] = jnp.concatenate([db1, db2], axis=1)
    dzb = b16(jnp.concatenate([dz1, dz2], axis=1))
    dyg = _mm("mm_glu_dx", dzb, w['w_glu'], 'nt', bb='c')
    gw['w_glu'] = _mm("mm_glu_dw", s['yg'], dzb, 'tn', ob='c')
    dymm, du_act, dd = _rows_vjp("s5_act_bwd", _s5_act_fn, [_row(s['ymm']), _row(s['u'])], [s['d_row']], [], [_row(dyg)],
                                 grad_dtypes=[BF16, F32])
    gs_['d_skip'] = dd
    dhst = _mm("mm_s5_c_dx", dymm, s['c_mat'], 'nn')
    dc_mat = _mm("mm_s5_c_dw", dymm, s['hst16'], 'tn')
    g, dar, dai = _scan_bwd(dhst, s['hst'], s['ar'], s['ai'])
    du = _mm("mm_s5_b_dx", g, s['b_mat'], 'nt', res=du_act)
    db_mat = _mm("mm_s5_b_dw", s['u16'], g, 'tn')
    tiling = _lane_tiling()
    gs_['c_re'], gs_['c_im'] = _bd_extract("s5_c_blocks", dc_mat, -1.0, tiling)
    dbbr, dbbi = _bd_extract("s5_b_blocks", db_mat, 1.0, tiling)
    disc_cts = [_row(dar.reshape(SSM_G, 1, SSM_N), 1, 0), _row(dai.reshape(SSM_G, 1, SSM_N), 1, 0),
                _row(dbbr, SSM_P, 0), _row(dbbi, SSM_P, 0)]
    da_re, da_im, dldt, db_r, db_i = _rows_vjp("s5_disc_bwd", _s5_disc_fn, s['disc_rows'], [], [], disc_cts)
    gs_['a_re'], gs_['a_im'], gs_['log_dt'] = da_re, da_im, dldt
    unb = lambda a: a.reshape(SSM_G, SSM_P, SSM_N).transpose(0, 2, 1)
    gs_['b_re'], gs_['b_im'] = unb(db_r), unb(db_i)
    return (dy_mla, du, dqd, dkd, dvd), gw, gs_


def _mixer_bwd_in(cts, dx1, s, w, sp, rope, tok=None):
    gw, gs_ = {}, {}
    b16 = _b16
    dy_mla, du, dqd, dkd, dvd = cts
    dqh, dkh, dvh = _mla_bwd(s['qh'], s['kh'], s['vh'], s['y_mla'], dy_mla, _behind(s['lse_mla'], tok))
    cos, sin, rot = rope
    tr = MLA_TQ
    dq8, dkv8, dk_rope = _rows_fwd(
        "mla_unpack", _mla_unpack_fn,
        [_row(dqh, tr, 1), _row(dkh, tr, 1), _row(dvh, tr, 1), _row(cos, tr, 0), _row(sin, tr, 0)], [rot],
        [((H_MLA, T, QK), BF16, 1, tr), ((H_MLA, T, NOPE + VDIM), BF16, 1, tr), ((T, ROPE), F32, 0, tr)])
    dcqn = _mm("mm_uq_dx", dq8, w['w_uq'], 'nn', ab='c', bb='r')
    gw['w_uq'] = _mm("mm_uq_dw", dq8, s['cqn'], 'tn', ab='c', ob='r')
    dckvn = _mm("mm_ukv_dx", dkv8, w['w_ukv'], 'nt', ab='c', bb='c')
    gw['w_ukv'] = _mm("mm_ukv_dw", s['ckvn'], dkv8, 'tn', bb='c', ob='c')
    dc_q, gs_['g_q'] = _rms_bwd("rms_q_bwd", s['c_q'], sp['g_q'], dcqn)
    dc_kv, gs_['g_kv'] = _rms_bwd("rms_kv_bwd", s['c_kv'], sp['g_kv'], dckvn)

    dproj = _to_blocks(b16(jnp.concatenate([dc_q, dc_kv, dk_rope, du, dqd, dkd, dvd], axis=1)))
    dh = _mm("mm_in_dx", dproj, w['w_in'], 'nn', ab='c', bb='r')
    gw['w_in'] = _mm("mm_in_dw", dproj, s['h'], 'tn', ab='c', ob='r')
    dx, gs_['g_mix'] = _rms_bwd("rms_mix_bwd", s['x'], sp['g_mix'], dh, dx1)
    return dx, gw, gs_


def kernel(x, g_mix, w_in, g_q, w_uq, g_kv, w_ukv, a_re, a_im, b_re, b_im, c_re, c_im, d_skip, log_dt, w_glu, b_glu, g_out_mla, g_out_ssm, g_out_dil, w_o, g_ffn, w_gate, w_up, w_down, g_final, loss_target, m_g_mix, m_w_in, m_g_q, m_w_uq, m_g_kv, m_w_ukv, m_a_re, m_a_im, m_b_re, m_b_im, m_c_re, m_c_im, m_d_skip, m_log_dt, m_w_glu, m_b_glu, m_g_out_mla, m_g_out_ssm, m_g_out_dil, m_w_o, m_g_ffn, m_w_gate, m_w_up, m_w_down, m_g_final, v_g_mix, v_w_in, v_g_q, v_w_uq, v_g_kv, v_w_ukv, v_a_re, v_a_im, v_b_re, v_b_im, v_c_re, v_c_im, v_d_skip, v_log_dt, v_w_glu, v_b_glu, v_g_out_mla, v_g_out_ssm, v_g_out_dil, v_w_o, v_g_ffn, v_w_gate, v_w_up, v_w_down, v_g_final):
    W = dict(zip(PARAMS, (g_mix, w_in, g_q, w_uq, g_kv, w_ukv, a_re, a_im, b_re, b_im, c_re, c_im, d_skip, log_dt,
                          w_glu, b_glu, g_out_mla, g_out_ssm, g_out_dil, w_o, g_ffn, w_gate, w_up, w_down, g_final)))
    M = dict(zip(PARAMS, (m_g_mix, m_w_in, m_g_q, m_w_uq, m_g_kv, m_w_ukv, m_a_re, m_a_im, m_b_re, m_b_im, m_c_re,
                          m_c_im, m_d_skip, m_log_dt, m_w_glu, m_b_glu, m_g_out_mla, m_g_out_ssm, m_g_out_dil, m_w_o,
                          m_g_ffn, m_w_gate, m_w_up, m_w_down, m_g_final)))
    V = dict(zip(PARAMS, (v_g_mix, v_w_in, v_g_q, v_w_uq, v_g_kv, v_w_ukv, v_a_re, v_a_im, v_b_re, v_b_im, v_c_re,
                          v_c_im, v_d_skip, v_log_dt, v_w_glu, v_b_glu, v_g_out_mla, v_g_out_ssm, v_g_out_dil, v_w_o,
                          v_g_ffn, v_w_gate, v_w_up, v_w_down, v_g_final)))
    cx, cy, cc = _place()
    core = cc.astype(jnp.int32).reshape(1)
    chip = (2 * cx + cy).astype(jnp.int32).reshape(1)
    rope = _rope_tables()
    small = [{n: W[n][l] for n in SMALL} for l in range(DEPTH)]
    for sp in small:
        for n in ('g_mix', 'g_q', 'g_kv', 'g_ffn'):
            sp[n] = sp[n].reshape(1, -1)

    def tok_of(tokens):
        return sum(t[0, 0] for t in tokens) if tokens else None

    def shard_view(a, n):
        return a.swapaxes(1, 2) if BIG[n] == 't' else a

    def gather_start(l, group, names, after):
        return _gather_start(f"gather_{group}_start_{l}", [shard_view(W[n], n)[l].astype(BF16) for n in names], after)

    xa = x[0]
    h1_mix = gather_start(0, "mix", MIXER_W, jnp.zeros((8, LANES), F32))
    h1_ffn = gather_start(0, "ffn", FFN_W, h1_mix[4])
    h2_mix = _gather_pass_on("gather_mix_0", h1_mix, xa)
    saved, full = [], []
    tokens = [h2_mix[4]]
    for l in range(DEPTH):
        last = l + 1 == DEPTH
        wm = dict(zip(MIXER_W, _gather_finish(f"gather_mix_{l}", h2_mix, xa)))
        sp = dict(small[l])
        sp['g_mix'] = _behind(sp['g_mix'], tok_of(tokens))
        s = _mixer_fwd_in(xa, wm, sp, rope)
        tokens = []
        if l == 0:
            h2_ffn = _gather_pass_on("gather_ffn_0", h1_ffn, s['y_mla'])
            tokens.append(h2_ffn[4])
        if not last:
            h1_mix = gather_start(l + 1, "mix", MIXER_W, s['y_mla'])
            h1_ffn = gather_start(l + 1, "ffn", FFN_W, h1_mix[4])
            tokens += [h1_mix[4], h1_ffn[4]]
        x1 = _mixer_fwd_out(s, wm, small[l], tok_of(tokens))
        tokens = []
        wf = dict(zip(FFN_W, _gather_finish(f"gather_ffn_{l}", h2_ffn, x1)))
        if not last:
            h2_mix = _gather_pass_on(f"gather_mix_{l + 1}", h1_mix, x1)
            tokens.append(h2_mix[4])
        xa = _ffn_fwd(x1, wf, small[l], s, tok_of(tokens))
        tokens = []
        if not last:
            h2_ffn = _gather_pass_on(f"gather_ffn_{l + 1}", h1_ffn, xa)
            tokens.append(h2_ffn[4])
        saved.append(s)
        full.append({**wm, **wf})
    gf = g_final.reshape(1, D)
    ones = jnp.ones((T, 1), F32)
    dxa, dgf, loss_rows = _rows_vjp("loss", _loss_fn, [_row(xa)], [gf], [_row(loss_target[0])], [_row(ones)],
                                    primal=True)
    loss = lax.psum(_sum_rows("loss_sum", loss_rows)[0, 0], ("x", "y", "c"))

    bufs = {n: None for n in BIG}
    pending = []

    def advance(dep):
        tokens = []
        for g in pending:
            names, tag = g['names'], g['tag']
            if g['stage'] == 0:
                p4 = [a.reshape((4, 2) + a.shape[1:]) for a in g['gw']]
                gots = [lax.empty((4,) + a.shape[1:], F32) for a in g['gw']]
                g['h'] = _split_start("rs_sibling_start_" + tag, p4, gots, 4 * len(p4), _sibling_plan, dep)
                tokens.append(g['h'][4])
            elif g['stage'] == 1:
                p4, gots = _split_wait("rs_sibling_wait_" + tag, g['h'], 4 * len(names), _sibling_plan, dep)
                s4 = [_pair_sum("rs_pair_sum_" + n, p, q, core) for n, p, q in zip(names, p4, gots)]
                gots = [lax.empty((3,) + a.shape[1:], F32) for a in s4]
                g['h'] = _split_start("rs_chips_start_" + tag, s4, gots, 3 * len(s4), _chips_plan, dep)
                tokens.append(g['h'][4])
            elif g['stage'] == 3:
                s4, gots = _split_wait("rs_chips_wait_" + tag, g['h'], 3 * len(names), _chips_plan, dep)
                for n, s4n, got in zip(names, s4, gots):
                    bufs[n] = _adamw_shard("adamw_" + n, g['layer'], shard_view(W[n], n), shard_view(M[n], n),
                                           shard_view(V[n], n), s4n, got, chip, bufs[n])
            g['stage'] += 1
        pending[:] = [g for g in pending if g['stage'] < 4]
        return tokens

    def group(names, l, gw, kind):
        return dict(names=names, layer=l, gw=[gw[n] for n in names], stage=0, tag=f"{kind}_{l}")

    g_small = [None] * DEPTH
    tokens = []
    for l in reversed(range(DEPTH)):
        dx1, gw_f, gs_f = _ffn_bwd(dxa, saved[l], full[l], small[l], tok_of(tokens))
        pending.append(group(FFN_W, l, gw_f, "ffn"))
        tokens = advance(dx1)
        cts, gw_o, gs_o = _mixer_bwd_out(dx1, saved[l], full[l], small[l], tok_of(tokens))
        pending.append(group(OUT_W, l, gw_o, "out"))
        tokens = advance(cts[0])
        dxa, gw_i, gs_i = _mixer_bwd_in(cts, dx1, saved[l], full[l], small[l], rope, tok_of(tokens))
        pending.append(group(IN_W, l, gw_i, "in"))
        tokens = advance(dxa)
        g_small[l] = {**gs_f, **gs_o, **gs_i}

    flat = [g_small[l][n].reshape(-1) for l in range(DEPTH) for n in SMALL] + [dgf.reshape(-1)]
    n_small = sum(int(f.shape[0]) for f in flat)
    rows = -(-n_small // (PACK_C * SMALL_ROW_ALIGN)) * SMALL_ROW_ALIGN
    flat = jnp.concatenate(flat + [jnp.zeros((rows * PACK_C - n_small,), F32)]).reshape(rows, PACK_C)
    (gathered,) = _all_gather("gather_small", [flat])
    tot = _sum_devices("small_sum", gathered).reshape(-1)
    grads, off = {}, 0
    per_layer = {n: [] for n in SMALL}
    for l in range(DEPTH):
        for n, shp in SMALL.items():
            k = int(np.prod(shp))
            per_layer[n].append(tot[off:off + k].reshape(shp))
            off += k
    for n in SMALL:
        grads[n] = jnp.stack(per_layer[n])
    grads['g_final'] = tot[off:off + D]

    delta, new_m, new_v = {}, {}, {}
    advance(tot)
    for n in PARAMS:
        if n not in BIG:
            delta[n], new_m[n], new_v[n] = _adamw("adamw_" + n, W[n], grads[n], M[n], V[n])
    while pending:
        advance(delta['g_final'])
    for n in BIG:
        grads[n], delta[n], new_m[n], new_v[n] = [shard_view(b, n) for b in bufs[n]]
    return (loss, dxa[None], *[grads[n] for n in PARAMS], *[delta[n] for n in PARAMS],
            *[new_m[n] for n in PARAMS], *[new_v[n] for n in PARAMS])
```

```python
import jax
import jax.numpy as jnp
import numpy as np
from jax import lax
from jax.experimental import pallas as pl
from jax.experimental.pallas import tpu as pltpu

F32 = jnp.float32
BF16 = jnp.bfloat16

T = 2048
D = 2048
DEPTH = 4
N_DEV = 8
H_MLA, NOPE, ROPE, VDIM = 8, 128, 64, 128
QK = NOPE + ROPE
Q_LORA, KV_LORA = 512, 256
SSM_W, SSM_G, SSM_P, SSM_N = 512, 32, 16, 64
SSM_S = SSM_G * SSM_N
DIL_W, DIL_H, DIL_D = 512, 8, 64
BLK = 128
IN_SPLITS = (Q_LORA, KV_LORA, ROPE, SSM_W, DIL_W, DIL_W, DIL_W)
IN_W = sum(IN_SPLITS)
D_FF = 5632
EPS = 1e-6
ROPE_THETA = 10000.0
MLA_SCALE = QK ** -0.5
DIL_SCALE = DIL_D ** -0.5

ADAM_LR, ADAM_B1, ADAM_B2, ADAM_EPS, ADAM_WD, ADAM_STEP = 0.001, 0.9, 0.999, 1e-08, 0.01, 10

VMEM_LIMIT_V7X = 52 * 1024 * 1024
LANES = 128
PACK_C = 1024
ROW_BLOCK_BYTES = 2 * 1024 * 1024
MM_TM, MM_TN, MM_TK = 1024, 1024, 2048
MM_TB = 512

NT = (((1,), (1,)), ((), ()))
TN = (((0,), (0,)), ((), ()))
H_QK = (((2,), (2,)), ((0,), (0,)))
H_PV = (((2,), (1,)), ((0,), (0,)))
H_TN = (((1,), (1,)), ((0,), (0,)))
HI = lax.Precision.HIGHEST
MESH = pl.DeviceIdType.MESH

PARAMS = ['g_mix', 'w_in', 'g_q', 'w_uq', 'g_kv', 'w_ukv', 'a_re', 'a_im', 'b_re', 'b_im', 'c_re', 'c_im',
          'd_skip', 'log_dt', 'w_glu', 'b_glu', 'g_out_mla', 'g_out_ssm', 'g_out_dil', 'w_o', 'g_ffn',
          'w_gate', 'w_up', 'w_down', 'g_final']
BIG = {'w_in': 't', 'w_uq': 't', 'w_ukv': 'c', 'w_glu': 'c', 'w_o': 'r', 'w_gate': 't', 'w_up': 't', 'w_down': 'r'}
MIXER_W = ['w_in', 'w_uq', 'w_ukv', 'w_glu', 'w_o']
FFN_W = ['w_gate', 'w_up', 'w_down']
OUT_W, IN_W = ['w_o', 'w_glu'], ['w_in', 'w_uq', 'w_ukv']
SMALL = {'g_mix': (D,), 'g_q': (Q_LORA,), 'g_kv': (KV_LORA,), 'a_re': (SSM_G, SSM_N), 'a_im': (SSM_G, SSM_N),
         'b_re': (SSM_G, SSM_N, SSM_P), 'b_im': (SSM_G, SSM_N, SSM_P), 'c_re': (SSM_G, SSM_P, SSM_N),
         'c_im': (SSM_G, SSM_P, SSM_N), 'd_skip': (SSM_G, SSM_P), 'log_dt': (SSM_G,), 'b_glu': (2 * SSM_W,),
         'g_out_mla': (H_MLA * VDIM,), 'g_out_ssm': (SSM_W,), 'g_out_dil': (DIL_W,), 'g_ffn': (D,)}
SMALL_ROW_ALIGN = 64


def _tile(dim, target, align=LANES):
    best = None
    for t in range(align, min(dim, target) + 1, align):
        if dim % t == 0:
            best = t
    return best if best is not None else dim


def _params(sem=None):
    return pltpu.CompilerParams(dimension_semantics=sem, vmem_limit_bytes=VMEM_LIMIT_V7X)


def _dot(a, b, dims=None, prec=None):
    if dims is None:
        return jnp.dot(a, b, preferred_element_type=F32, precision=prec)
    return lax.dot_general(a, b, dims, preferred_element_type=F32, precision=prec)


def _mm_spec(shape, blk, t_r, t_c, rc):
    if blk is None:
        return pl.BlockSpec((t_r, t_c), rc)
    _, R, C = shape
    if blk == 'r':
        per = R // t_r
        return pl.BlockSpec((1, t_r, t_c), lambda i, j, k: (rc(i, j, k)[0] // per, rc(i, j, k)[0] % per, rc(i, j, k)[1]))
    per = C // t_c
    return pl.BlockSpec((1, t_r, t_c), lambda i, j, k: (rc(i, j, k)[1] // per, rc(i, j, k)[0], rc(i, j, k)[1] % per))


def _logical(shape, blk):
    if blk is None:
        return tuple(shape)
    G, R, C = shape
    return (G * R, C) if blk == 'r' else (R, G * C)


def _mm(name, a, b, mode, ab=None, bb=None, ob=None, res=None, prec=None):
    la, lb = _logical(a.shape, ab), _logical(b.shape, bb)
    am, ak = (0, 1) if mode != 'tn' else (1, 0)
    bk, bn = (0, 1) if mode != 'nt' else (1, 0)
    M, K, N = la[am], la[ak], lb[bn]
    assert lb[bk] == K, (name, a.shape, b.shape, mode)
    if ob is None:
        out_shape = (M, N)
    elif ob == 'r':
        G = N_DEV
        out_shape = (G, M // G, N)
    else:
        G = N_DEV
        out_shape = (G, M, N // G)
    em = min(a.shape[-2:][am], out_shape[-2])
    en = min(b.shape[-2:][bn], out_shape[-1])
    ek = min(a.shape[-2:][ak], b.shape[-2:][bk])
    dims = {'nn': None, 'nt': NT, 'tn': TN}[mode]
    a_kb = mode != 'tn' and ab == 'c'
    b_kb = (mode == 'nn' and bb == 'r') or (mode == 'nt' and bb == 'c')
    blocks = K // ek if (a_kb or b_kb) else 1
    tn = _tile(en, MM_TB if blocks > 1 else MM_TN)
    tm = _tile(em, MM_TB if blocks > 1 else (2 * MM_TM if K <= MM_TB else MM_TM))
    assert blocks == 1 or ((a_kb or ab is None) and (b_kb or bb is None)), (name, ab, bb, mode)
    tk = ek if blocks > 1 else _tile(ek, MM_TK)
    nk = 1 if blocks > 1 else K // tk

    def val(ref):
        return ref[...] if len(ref.shape) == 2 else ref[0]

    def put(o_ref, r):
        if len(o_ref.shape) == 2:
            o_ref[...] = r
        else:
            o_ref[0] = r

    def k_block(ref, d, blocked, lanes):
        if blocked:
            return ref[d]
        return ref[:, d * ek:(d + 1) * ek] if lanes else ref[d * ek:(d + 1) * ek, :]

    def body(*refs):
        if res is None:
            a_ref, b_ref, o_ref = refs[:3]
            r_ref = None
        else:
            a_ref, b_ref, r_ref, o_ref = refs[:4]
        if blocks > 1:
            part = None
            for d in range(blocks):
                p = _dot(k_block(a_ref, d, a_kb, True), k_block(b_ref, d, b_kb, mode == 'nt'), dims, prec)
                part = p if part is None else part + p
        else:
            part = _dot(val(a_ref), val(b_ref), dims, prec)
        if nk == 1:
            put(o_ref, part if r_ref is None else part + val(r_ref))
            return
        acc_ref = refs[-1]
        k = pl.program_id(2)

        @pl.when(k == 0)
        def _():
            acc_ref[...] = part

        @pl.when((k > 0) & (k < nk - 1))
        def _():
            acc_ref[...] += part

        @pl.when(k == nk - 1)
        def _():
            r = acc_ref[...] + part
            put(o_ref, r if r_ref is None else r + val(r_ref))

    if blocks > 1:
        G = blocks
        a_spec = (pl.BlockSpec((G, tm, ek), lambda i, j, k: (0, i, 0)) if a_kb
                  else pl.BlockSpec((tm, K), lambda i, j, k: (i, 0)))
        if b_kb:
            b_spec = (pl.BlockSpec((G, ek, tn), lambda i, j, k: (0, 0, j)) if mode == 'nn'
                      else pl.BlockSpec((G, tn, ek), lambda i, j, k: (0, j, 0)))
        else:
            b_spec = (pl.BlockSpec((K, tn), lambda i, j, k: (0, j)) if mode == 'nn'
                      else pl.BlockSpec((tn, K), lambda i, j, k: (j, 0)))
    else:
        if mode == 'tn':
            a_spec = _mm_spec(a.shape, ab, tk, tm, lambda i, j, k: (k, i))
        else:
            a_spec = _mm_spec(a.shape, ab, tm, tk, lambda i, j, k: (i, k))
        if mode == 'nt':
            b_spec = _mm_spec(b.shape, bb, tn, tk, lambda i, j, k: (j, k))
        else:
            b_spec = _mm_spec(b.shape, bb, tk, tn, lambda i, j, k: (k, j))
    o_spec = _mm_spec(out_shape, ob, tm, tn, lambda i, j, k: (i, j))
    in_specs = [a_spec, b_spec] + ([o_spec] if res is not None else [])
    args = (a, b) + ((res,) if res is not None else ())
    return pl.pallas_call(
        body, name=name, out_shape=jax.ShapeDtypeStruct(out_shape, F32),
        grid=(M // tm, N // tn, nk), in_specs=in_specs, out_specs=o_spec,
        scratch_shapes=[pltpu.VMEM((tm, tn), F32)] if nk > 1 else [],
        compiler_params=_params(("parallel", "parallel", "arbitrary")),
    )(*args)


def _row(a, tr=None, axis=-2):
    axis = axis % a.ndim
    n = a.shape[axis]
    if tr is None:
        row_bytes = a.size // n * 4
        tr = _tile(n, max(8, min(256, ROW_BLOCK_BYTES // row_bytes)), 8)
    return (a, axis, tr)


def _row_spec(shape, axis, tr):
    nd = len(shape)
    blk = tuple(tr if d == axis else s for d, s in enumerate(shape))
    return pl.BlockSpec(blk, lambda i: tuple(i if d == axis else 0 for d in range(nd)))


def _full_spec(shape):
    nd = len(shape)
    return pl.BlockSpec(tuple(shape), lambda i: (0,) * nd)


def _steps(entries):
    ns = {a.shape[ax] // tr for a, ax, tr in entries}
    assert len(ns) == 1, [(a.shape, ax, tr) for a, ax, tr in entries]
    return ns.pop()


def _as_tuple(r):
    return tuple(r) if isinstance(r, (tuple, list)) else (r,)


def _rows_fwd(name, fn, rows, bcast, outs):
    steps = _steps(rows)
    nr, nb = len(rows), len(bcast)

    def body(*refs):
        vals = [r[...] for r in refs[:nr + nb]]
        res = _as_tuple(fn(*vals))
        for o_ref, r in zip(refs[nr + nb:], res):
            o_ref[...] = r.astype(o_ref.dtype)

    in_specs = [_row_spec(a.shape, ax, tr) for a, ax, tr in rows] + [_full_spec(b.shape) for b in bcast]
    out_specs = [_row_spec(s, ax % len(s), tr) for s, _, ax, tr in outs]
    res = pl.pallas_call(
        body, name=name, out_shape=[jax.ShapeDtypeStruct(s, dt) for s, dt, _, _ in outs],
        grid=(steps,), in_specs=in_specs, out_specs=out_specs,
        compiler_params=_params(("parallel",)),
    )(*[a for a, _, _ in rows], *bcast)
    return res


def _rows_vjp(name, fn, drows, dbc, arows, cts, primal=False, grad_dtypes=None):
    entries = list(drows) + list(arows) + list(cts)
    steps = _steps(entries)
    ndr, ndb, nar, nct = len(drows), len(dbc), len(arows), len(cts)
    gdt = list(grad_dtypes) if grad_dtypes is not None else [F32] * ndr

    def body(*refs):
        p = 0
        dr = [r[...] for r in refs[p:p + ndr]]; p += ndr
        db = [r[...] for r in refs[p:p + ndb]]; p += ndb
        ar = [r[...] for r in refs[p:p + nar]]; p += nar
        ct = [r[...] for r in refs[p:p + nct]]; p += nct
        g_rows = refs[p:p + ndr]; p += ndr
        g_bc = refs[p:p + ndb]; p += ndb
        prim_refs = refs[p:]

        def f(*d):
            return _as_tuple(fn(*d, *ar))

        outs, pullback = jax.vjp(f, *dr, *db)
        grads = pullback(tuple(c.astype(o.dtype) for c, o in zip(ct, outs)))
        for k in range(ndr):
            g_rows[k][...] = grads[k].astype(g_rows[k].dtype)
        if ndb:
            @pl.when(pl.program_id(0) == 0)
            def _():
                for r in g_bc:
                    r[...] = jnp.zeros_like(r)
            for k in range(ndb):
                g_bc[k][...] += grads[ndr + k]
        for r, o in zip(prim_refs, outs):
            r[...] = o.astype(r.dtype)

    in_specs = ([_row_spec(a.shape, ax, tr) for a, ax, tr in drows] + [_full_spec(b.shape) for b in dbc]
                + [_row_spec(a.shape, ax, tr) for a, ax, tr in arows]
                + [_row_spec(a.shape, ax, tr) for a, ax, tr in cts])
    out_shape = ([jax.ShapeDtypeStruct(a.shape, dt) for (a, _, _), dt in zip(drows, gdt)]
                 + [jax.ShapeDtypeStruct(b.shape, F32) for b in dbc])
    out_specs = ([_row_spec(a.shape, ax, tr) for a, ax, tr in drows] + [_full_spec(b.shape) for b in dbc])
    if primal:
        out_shape += [jax.ShapeDtypeStruct(a.shape, F32) for a, _, _ in cts]
        out_specs += [_row_spec(a.shape, ax, tr) for a, ax, tr in cts]
    return pl.pallas_call(
        body, name=name, out_shape=out_shape, grid=(steps,), in_specs=in_specs, out_specs=out_specs,
        compiler_params=_params(("arbitrary",)),
    )(*[a for a, _, _ in drows], *dbc, *[a for a, _, _ in arows], *[a for a, _, _ in cts])


def _rms_fn(x, g):
    return x * lax.rsqrt(jnp.mean(x * x, axis=-1, keepdims=True) + EPS) * g


def _rms_res_fn(x, g):
    return _rms_fn(x, g), x


def _s5_act_fn(ymm, u, d):
    return jax.nn.gelu(ymm + d * u)


def _glu_fn(z1, z2, b1, b2):
    return (z1 + b1) * jax.nn.sigmoid(z2 + b2)


def _outnorm_fn(ym, ys, yd, gm, gs, gd):
    return jnp.concatenate([_rms_fn(ym, gm), _rms_fn(ys, gs), _rms_fn(yd, gd)], axis=-1)


def _swiglu_fn(a, b):
    return jax.nn.silu(a) * b


def _loss_fn(x, g, tgt):
    err = _rms_fn(x, g) - tgt
    return 0.5 * jnp.mean(err * err, axis=-1, keepdims=True)


def _dil_mix_fn(o0, o1, o2, l0, l1, l2):
    m = jnp.maximum(jnp.maximum(l0, l1), l2)
    e0, e1, e2 = jnp.exp(l0 - m), jnp.exp(l1 - m), jnp.exp(l2 - m)
    s = e0 + e1 + e2
    return (e0 / s) * o0 + (e1 / s) * o1 + (e2 / s) * o2


def _s5_disc_fn(a_re, a_im, ldt, b_r, b_i):
    lr = jnp.minimum(a_re.reshape(1, SSM_N), -1e-4)
    li = a_im.reshape(1, SSM_N)
    dt = jnp.exp(ldt.reshape(1, 1))
    e = jnp.exp(lr * dt)
    ar = e * jnp.cos(li * dt)
    ai = e * jnp.sin(li * dt)
    nr, ni = ar - 1.0, ai
    den = lr * lr + li * li
    cr = (nr * lr + ni * li) / den
    ci = (ni * lr - nr * li) / den
    return ar.reshape(1, 1, SSM_N), ai.reshape(1, 1, SSM_N), cr * b_r - ci * b_i, cr * b_i + ci * b_r


def _adamw_fn(w, g, m, v):
    m = ADAM_B1 * m + (1.0 - ADAM_B1) * g
    v = ADAM_B2 * v + (1.0 - ADAM_B2) * jnp.square(g)
    m_hat = m / (1.0 - ADAM_B1 ** ADAM_STEP)
    v_hat = v / (1.0 - ADAM_B2 ** ADAM_STEP)
    delta = -ADAM_LR * (m_hat / (jnp.sqrt(v_hat) + ADAM_EPS) + ADAM_WD * w)
    return delta, m, v


def _rms_fwd(name, x, g):
    (h,) = _rows_fwd(name, _rms_fn, [_row(x)], [g], [(x.shape, BF16, -2, _row(x)[2])])
    return h


def _rms_bwd(name, x, g, dh, dres=None):
    if dres is None:
        dx, dg = _rows_vjp(name, _rms_fn, [_row(x)], [g], [], [_row(dh)])
    else:
        dx, dg = _rows_vjp(name, _rms_res_fn, [_row(x)], [g], [], [_row(dh), _row(dres)])
    return dx, dg


MLA_TQ = 256
MLA_EXT = 512


def _mla_fwd(q, k, v):
    tq = MLA_TQ

    def body(q_ref, k_ref, v_ref, o_ref, lse_ref):
        i = pl.program_id(1)
        q = q_ref[0]

        def rows_below(ext):
            s = _dot(q, k_ref[0, :ext, :], NT) * MLA_SCALE
            row = i * tq + lax.broadcasted_iota(jnp.int32, (tq, ext), 0)
            col = lax.broadcasted_iota(jnp.int32, (tq, ext), 1)
            s = jnp.where(row >= col, s, -jnp.inf)
            m = jnp.max(s, axis=-1, keepdims=True)
            p = jnp.exp(s - m)
            l = jnp.sum(p, axis=-1, keepdims=True)
            o_ref[...] = _dot((p / l).astype(BF16), v_ref[0, :ext, :])
            lse_ref[0] = m + jnp.log(l)

        for g in range(T // MLA_EXT):
            pl.when(i // (MLA_EXT // tq) == g)(lambda g=g: rows_below((g + 1) * MLA_EXT))

    return pl.pallas_call(
        body, name="mla_fwd",
        out_shape=[jax.ShapeDtypeStruct((T, H_MLA * VDIM), F32), jax.ShapeDtypeStruct((H_MLA, T, 1), F32)],
        grid=(H_MLA, T // tq),
        in_specs=[pl.BlockSpec((1, tq, QK), lambda h, i: (h, i, 0)),
                  pl.BlockSpec((1, T, QK), lambda h, i: (h, 0, 0)),
                  pl.BlockSpec((1, T, VDIM), lambda h, i: (h, 0, 0))],
        out_specs=[pl.BlockSpec((tq, VDIM), lambda h, i: (i, h)),
                   pl.BlockSpec((1, tq, 1), lambda h, i: (h, i, 0))],
        compiler_params=_params(("parallel", "parallel")),
    )(q, k, v)


def _mla_bwd(q, k, v, o, do, lse):
    tq = MLA_TQ

    def body(q_ref, k_ref, v_ref, o_ref, do_ref, lse_ref, dq_ref, dk_ref, dv_ref):
        i = pl.program_id(1)

        @pl.when(i == 0)
        def _():
            dk_ref[...] = jnp.zeros_like(dk_ref)
            dv_ref[...] = jnp.zeros_like(dv_ref)

        q, lse = q_ref[0], lse_ref[0]
        delta = jnp.sum(do_ref[...] * o_ref[...], axis=-1, keepdims=True)
        do = do_ref[...].astype(BF16)

        def rows_below(ext):
            k, v = k_ref[0, :ext, :], v_ref[0, :ext, :]
            s = _dot(q, k, NT) * MLA_SCALE
            row = i * tq + lax.broadcasted_iota(jnp.int32, (tq, ext), 0)
            col = lax.broadcasted_iota(jnp.int32, (tq, ext), 1)
            p = jnp.where(row >= col, jnp.exp(s - lse), 0.0)
            ds = (p * (_dot(do, v, NT) - delta) * MLA_SCALE).astype(BF16)
            dq_ref[0] = _dot(ds, k)
            dk_ref[0, :ext, :] += _dot(ds, q, TN)
            dv_ref[0, :ext, :] += _dot(p.astype(BF16), do, TN)

        for g in range(T // MLA_EXT):
            pl.when(i // (MLA_EXT // tq) == g)(lambda g=g: rows_below((g + 1) * MLA_EXT))

    return pl.pallas_call(
        body, name="mla_bwd",
        out_shape=[jax.ShapeDtypeStruct((H_MLA, T, QK), F32), jax.ShapeDtypeStruct((H_MLA, T, QK), F32),
                   jax.ShapeDtypeStruct((H_MLA, T, VDIM), F32)],
        grid=(H_MLA, T // tq),
        in_specs=[pl.BlockSpec((1, tq, QK), lambda h, i: (h, i, 0)),
                  pl.BlockSpec((1, T, QK), lambda h, i: (h, 0, 0)),
                  pl.BlockSpec((1, T, VDIM), lambda h, i: (h, 0, 0)),
                  pl.BlockSpec((tq, VDIM), lambda h, i: (i, h)),
                  pl.BlockSpec((tq, VDIM), lambda h, i: (i, h)),
                  pl.BlockSpec((1, tq, 1), lambda h, i: (h, i, 0))],
        out_specs=[pl.BlockSpec((1, tq, QK), lambda h, i: (h, i, 0)),
                   pl.BlockSpec((1, T, QK), lambda h, i: (h, 0, 0)),
                   pl.BlockSpec((1, T, VDIM), lambda h, i: (h, 0, 0))],
        compiler_params=_params(("parallel", "arbitrary")),
    )(q, k, v, o, do, lse)


NBLK = T // BLK


BAND_GL = 256
BAND_GH = BAND_GL // DIL_D
BAND_ROWS = BAND_GH * BLK
BAND_GROUPS = DIL_W // BAND_GL


def _band_masks():
    r = lax.broadcasted_iota(jnp.int32, (BAND_ROWS, BLK), 0) & (BLK - 1)
    j = lax.broadcasted_iota(jnp.int32, (BAND_ROWS, BLK), 1)
    return j <= r, j >= r


def _head_lanes():
    lane_head = lax.broadcasted_iota(jnp.int32, (1, BAND_GL), 1) // DIL_D
    return [lane_head == h for h in range(BAND_GH)]


def _stack_heads(x, lanes):
    return jnp.concatenate([jnp.where(m, x, jnp.zeros_like(x)) for m in lanes], axis=0)


def _merge_heads(xs, lanes):
    out = None
    for h, m in enumerate(lanes):
        part = jnp.where(m, xs[h * BLK:(h + 1) * BLK], 0.0)
        out = part if out is None else out + part
    return out


def _per_head(x, lanes):
    return jnp.concatenate([jnp.sum(jnp.where(m, x, 0.0), axis=-1, keepdims=True) for m in lanes], axis=0)


def _lane_group(ref, g):
    return ref[0, :, g * BAND_GL:(g + 1) * BAND_GL]


def _seq_start(p, i):
    per_seq = lax.shift_right_logical(jnp.int32(NBLK), 2 * p)
    return lax.rem(i, per_seq) == 0


def _band_fwd(q, k, v):
    def body(q_ref, kp_ref, kc_ref, vp_ref, vc_ref, o_ref, lse_ref):
        p, i = pl.program_id(0), pl.program_id(1)
        has_prev = jnp.logical_not(_seq_start(p, i))
        m_cur, m_prev = _band_masks()
        m_prev = m_prev & has_prev
        lanes = _head_lanes()
        for g in range(BAND_GROUPS):
            qs = _stack_heads(_lane_group(q_ref, g), lanes)
            s_c = jnp.where(m_cur, _dot(qs, _lane_group(kc_ref, g), NT) * DIL_SCALE, -jnp.inf)
            s_p = jnp.where(m_prev, _dot(qs, _lane_group(kp_ref, g), NT) * DIL_SCALE, -jnp.inf)
            m = jnp.maximum(jnp.max(s_c, axis=-1, keepdims=True), jnp.max(s_p, axis=-1, keepdims=True))
            e_c, e_p = jnp.exp(s_c - m), jnp.exp(s_p - m)
            l = jnp.sum(e_c, axis=-1, keepdims=True) + jnp.sum(e_p, axis=-1, keepdims=True)
            os = (_dot((e_p / l).astype(BF16), _lane_group(vp_ref, g))
                  + _dot((e_c / l).astype(BF16), _lane_group(vc_ref, g)))
            cols = slice(g * BAND_GL, (g + 1) * BAND_GL)
            o_ref[0, :, cols] = _merge_heads(os, lanes)
            lse_ref[0, :, cols] = _merge_heads(m + jnp.log(l), lanes)

    blk = (1, BLK, DIL_W)
    cur = lambda p, i: (p, i, 0)
    prev = lambda p, i: (p, jnp.maximum(i - 1, 0), 0)
    return pl.pallas_call(
        body, name="band_fwd",
        out_shape=[jax.ShapeDtypeStruct((3, T, DIL_W), F32)] * 2,
        grid=(3, NBLK),
        in_specs=[pl.BlockSpec(blk, cur), pl.BlockSpec(blk, prev), pl.BlockSpec(blk, cur),
                  pl.BlockSpec(blk, prev), pl.BlockSpec(blk, cur)],
        out_specs=[pl.BlockSpec(blk, cur), pl.BlockSpec(blk, cur)],
        compiler_params=_params(("parallel", "parallel")),
    )(q, k, k, v, v)


def _band_bwd(q, k, v, o, lse, do, dlse):
    def body(qc_ref, qn_ref, kp_ref, kc_ref, vp_ref, vc_ref, oc_ref, on_ref, lc_ref, ln_ref,
             doc_ref, don_ref, dlc_ref, dln_ref, dq_ref, dk_ref, dv_ref):
        p, i = pl.program_id(0), pl.program_id(1)
        has_prev = jnp.logical_not(_seq_start(p, i))
        has_next = jnp.logical_not(_seq_start(p, i + 1)) & (i + 1 < NBLK)
        m_cur, m_prev = _band_masks()
        lanes = _head_lanes()

        def probs(qs, k, lse, mask):
            return jnp.where(mask, jnp.exp(_dot(qs, k, NT) * DIL_SCALE - lse), 0.0)

        def dscore(pr, dos, v, shift):
            return (pr * (_dot(dos, v, NT) + shift) * DIL_SCALE).astype(BF16)

        for g in range(BAND_GROUPS):
            grp = lambda ref: _lane_group(ref, g)
            kp, kc, vp, vc = grp(kp_ref), grp(kc_ref), grp(vp_ref), grp(vc_ref)
            qc, qn = _stack_heads(grp(qc_ref), lanes), _stack_heads(grp(qn_ref), lanes)
            doc, don = grp(doc_ref), grp(don_ref)
            lse_c = _per_head(grp(lc_ref), lanes) * (1.0 / DIL_D)
            lse_n = _per_head(grp(ln_ref), lanes) * (1.0 / DIL_D)
            sh_c = _per_head(grp(dlc_ref) - doc * grp(oc_ref), lanes)
            sh_n = _per_head(grp(dln_ref) - don * grp(on_ref), lanes)
            doc, don = _stack_heads(doc.astype(BF16), lanes), _stack_heads(don.astype(BF16), lanes)
            p_cc = probs(qc, kc, lse_c, m_cur)
            p_cp = probs(qc, kp, lse_c, m_prev & has_prev)
            p_nc = probs(qn, kc, lse_n, m_prev & has_next)
            ds_cc = dscore(p_cc, doc, vc, sh_c)
            ds_cp = dscore(p_cp, doc, vp, sh_c)
            ds_nc = dscore(p_nc, don, vc, sh_n)
            cols = slice(g * BAND_GL, (g + 1) * BAND_GL)
            dq_ref[0, :, cols] = _merge_heads(_dot(ds_cc, kc) + _dot(ds_cp, kp), lanes)
            dk_ref[0, :, cols] = _dot(ds_cc, qc, TN) + _dot(ds_nc, qn, TN)
            dv_ref[0, :, cols] = _dot(p_cc.astype(BF16), doc, TN) + _dot(p_nc.astype(BF16), don, TN)

    blk = (1, BLK, DIL_W)
    cur = lambda p, i: (p, i, 0)
    prev = lambda p, i: (p, jnp.maximum(i - 1, 0), 0)
    nxt = lambda p, i: (p, jnp.minimum(i + 1, NBLK - 1), 0)
    w, wn, wp = pl.BlockSpec(blk, cur), pl.BlockSpec(blk, nxt), pl.BlockSpec(blk, prev)
    return pl.pallas_call(
        body, name="band_bwd",
        out_shape=[jax.ShapeDtypeStruct((3, T, DIL_W), F32)] * 3,
        grid=(3, NBLK),
        in_specs=[w, wn, wp, w, wp, w, w, wn, w, wn, w, wn, w, wn],
        out_specs=[w, w, w],
        compiler_params=_params(("parallel", "parallel")),
    )(q, q, k, k, v, v, o, o, lse, lse, do, do, dlse, dlse)


SCAN_TC = 256


def _scan_fwd(bu, ar, ai):
    tc, S = SCAN_TC, SSM_S

    def body(bu_ref, ar_ref, ai_ref, h_ref, h16_ref, cr_ref, ci_ref):
        @pl.when(pl.program_id(0) == 0)
        def _():
            cr_ref[...] = jnp.zeros_like(cr_ref)
            ci_ref[...] = jnp.zeros_like(ci_ref)

        a_r, a_i = ar_ref[...], ai_ref[...]

        def step(j, carry):
            hr, hi = carry
            for r in range(8):
                t = pl.multiple_of(j * 8, 8) + r
                br = bu_ref[pl.ds(t, 1), pl.ds(0, S)]
                bi = bu_ref[pl.ds(t, 1), pl.ds(S, S)]
                hr, hi = a_r * hr - a_i * hi + br, a_r * hi + a_i * hr + bi
                h_ref[pl.ds(t, 1), pl.ds(0, S)] = hr
                h_ref[pl.ds(t, 1), pl.ds(S, S)] = hi
            return hr, hi

        hr, hi = lax.fori_loop(0, tc // 8, step, (cr_ref[...], ci_ref[...]))
        cr_ref[...] = hr
        ci_ref[...] = hi
        h16_ref[...] = h_ref[...].astype(BF16)

    return pl.pallas_call(
        body, name="s5_scan_fwd",
        out_shape=[jax.ShapeDtypeStruct((T, 2 * S), F32), jax.ShapeDtypeStruct((T, 2 * S), BF16)],
        grid=(T // tc,),
        in_specs=[pl.BlockSpec((tc, 2 * S), lambda i: (i, 0)), _full_spec((1, S)), _full_spec((1, S))],
        out_specs=[pl.BlockSpec((tc, 2 * S), lambda i: (i, 0)), pl.BlockSpec((tc, 2 * S), lambda i: (i, 0))],
        scratch_shapes=[pltpu.VMEM((1, S), F32), pltpu.VMEM((1, S), F32)],
        compiler_params=_params(("arbitrary",)),
    )(bu, ar, ai)


def _scan_bwd(dh, h, ar, ai):
    tc, S = SCAN_TC, SSM_S
    nc = T // tc

    def body(dh_ref, h_ref, hp_ref, ar_ref, ai_ref, g16_ref, dar_ref, dai_ref, cr_ref, ci_ref, g_ref):
        i = pl.program_id(0)

        @pl.when(i == 0)
        def _():
            cr_ref[...] = jnp.zeros_like(cr_ref)
            ci_ref[...] = jnp.zeros_like(ci_ref)
            dar_ref[...] = jnp.zeros_like(dar_ref)
            dai_ref[...] = jnp.zeros_like(dai_ref)

        a_r, a_i = ar_ref[...], ai_ref[...]
        first_chunk = (i == nc - 1)
        edge = jnp.where(first_chunk, 0.0, 1.0)
        hpr = hp_ref[pl.ds(7, 1), pl.ds(0, S)] * edge
        hpi = hp_ref[pl.ds(7, 1), pl.ds(S, S)] * edge

        def step(jj, carry):
            gr, gi, dar, dai = carry
            j = tc // 8 - 1 - jj
            for r in range(7, -1, -1):
                t = pl.multiple_of(j * 8, 8) + r
                tp = jnp.maximum(t - 1, 0)
                inside = t > 0
                pr = jnp.where(inside, h_ref[pl.ds(tp, 1), pl.ds(0, S)], hpr)
                pi = jnp.where(inside, h_ref[pl.ds(tp, 1), pl.ds(S, S)], hpi)
                gr, gi = (dh_ref[pl.ds(t, 1), pl.ds(0, S)] + a_r * gr + a_i * gi,
                          dh_ref[pl.ds(t, 1), pl.ds(S, S)] + a_r * gi - a_i * gr)
                g_ref[pl.ds(t, 1), pl.ds(0, S)] = gr
                g_ref[pl.ds(t, 1), pl.ds(S, S)] = gi
                dar = dar + gr * pr + gi * pi
                dai = dai + gi * pr - gr * pi
            return gr, gi, dar, dai

        zero = jnp.zeros((1, S), F32)
        gr, gi, dar, dai = lax.fori_loop(0, tc // 8, step, (cr_ref[...], ci_ref[...], zero, zero))
        cr_ref[...] = gr
        ci_ref[...] = gi
        dar_ref[...] += dar
        dai_ref[...] += dai
        g16_ref[...] = g_ref[...].astype(BF16)

    rev = lambda i: (nc - 1 - i, 0)
    before = lambda i: (jnp.maximum((nc - 1 - i) * (tc // 8) - 1, 0), 0)
    return pl.pallas_call(
        body, name="s5_scan_bwd",
        out_shape=[jax.ShapeDtypeStruct((T, 2 * S), BF16), jax.ShapeDtypeStruct((1, S), F32),
                   jax.ShapeDtypeStruct((1, S), F32)],
        grid=(nc,),
        in_specs=[pl.BlockSpec((tc, 2 * S), rev), pl.BlockSpec((tc, 2 * S), rev), pl.BlockSpec((8, 2 * S), before),
                  _full_spec((1, S)), _full_spec((1, S))],
        out_specs=[pl.BlockSpec((tc, 2 * S), rev), _full_spec((1, S)), _full_spec((1, S))],
        scratch_shapes=[pltpu.VMEM((1, S), F32), pltpu.VMEM((1, S), F32), pltpu.VMEM((tc, 2 * S), F32)],
        compiler_params=_params(("arbitrary",)),
    )(dh, h, h, ar, ai)


def _sum_rows(name, x):
    def body(x_ref, o_ref):
        o_ref[...] = jnp.sum(x_ref[...], axis=0, keepdims=True)

    return pl.pallas_call(body, name=name, out_shape=jax.ShapeDtypeStruct((1, 1), F32),
                          in_specs=[_full_spec(x.shape)], out_specs=_full_spec((1, 1)), grid=(1,))(x)


ANY = pl.BlockSpec(memory_space=pl.ANY)


def _place():
    return lax.axis_index("x"), lax.axis_index("y"), lax.axis_index("c")


def _all_gather(name, shards):
    n = len(shards)

    def body(*refs):
        xs, outs = refs[:n], refs[n:2 * n]
        send_sems, recv_sems, local_sems = refs[2 * n:]
        x, y, c = _place()
        me, sibling = (x, y, c), (x, y, 1 - c)
        chips = [(1 - x, y), (x, 1 - y), (1 - x, 1 - y)]

        def slot(t, px, py, pc):
            return outs[t].at[4 * px + 2 * py + pc]

        def copy(t, k, block, to, src=None):
            return pltpu.make_async_remote_copy(
                src_ref=slot(t, *block) if src is None else src, dst_ref=slot(t, *block),
                send_sem=send_sems.at[7 * t + k], recv_sem=recv_sems.at[7 * t + k], device_id=to, device_id_type=MESH)

        mine = [pltpu.make_async_copy(xs[t], slot(t, *me), local_sems.at[t]) for t in range(n)]
        sent = []
        for t in range(n):
            mine[t].start()
            sent.append(copy(t, 0, me, sibling, src=xs[t]))
            sent += [copy(t, 1 + j, me, (*chip, c), src=xs[t]) for j, chip in enumerate(chips)]
        for cp in sent:
            cp.start()
        for j, chip in enumerate(chips):
            for t in range(n):
                copy(t, 1 + j, (*chip, c), me).wait_recv()
                fwd = copy(t, 4 + j, (*chip, c), sibling)
                fwd.start()
                sent.append(fwd)
        for t in range(n):
            copy(t, 0, sibling, me).wait_recv()
            for j, chip in enumerate(chips):
                copy(t, 4 + j, (*chip, 1 - c), me).wait_recv()
        for cp in sent:
            cp.wait_send()
        for cp in mine:
            cp.wait()

    return pl.pallas_call(
        body, name=name, out_shape=[jax.ShapeDtypeStruct((N_DEV,) + s.shape, s.dtype) for s in shards],
        in_specs=[ANY] * n, out_specs=[ANY] * n,
        scratch_shapes=[pltpu.SemaphoreType.DMA((7 * n,)), pltpu.SemaphoreType.DMA((7 * n,)),
                        pltpu.SemaphoreType.DMA((n,))],
    )(*shards)


HBM = pl.BlockSpec(memory_space=pltpu.HBM)
SEM = pl.BlockSpec(memory_space=pltpu.SEMAPHORE)
DATAFLOW = pltpu.SideEffectType.DATAFLOW_SIDE_EFFECTING


def _hbm(a):
    return pltpu.with_memory_space_constraint(a, pltpu.HBM)


def _split_start(name, srcs, lands, ncopies, plan, after):
    ns, nl = len(srcs), len(lands)

    def body(*refs):
        send_sems, recv_sems = refs[ns + nl + 1], refs[ns + nl + 2]
        token = refs[-1]
        for k, (src, dst, peer, _) in enumerate(plan(refs[:ns], refs[ns:ns + nl])):
            pltpu.make_async_remote_copy(src_ref=src, dst_ref=dst, send_sem=send_sems.at[k], recv_sem=recv_sems.at[k],
                                         device_id=peer, device_id_type=MESH).start()
        token[...] = jnp.zeros_like(token)

    out = pl.pallas_call(
        body, name=name,
        out_shape=(pltpu.SemaphoreType.DMA((ncopies,)), pltpu.SemaphoreType.DMA((ncopies,)),
                   *[pltpu.HBM(a.shape, a.dtype) for a in srcs], *[pltpu.HBM(a.shape, a.dtype) for a in lands],
                   jax.ShapeDtypeStruct((8, LANES), F32)),
        in_specs=[HBM] * (ns + nl) + [ANY],
        out_specs=(SEM, SEM, *[HBM] * (ns + nl), pl.BlockSpec(memory_space=pltpu.VMEM)),
        input_output_aliases={i: 2 + i for i in range(ns + nl)},
        compiler_params=pltpu.CompilerParams(has_side_effects=DATAFLOW),
    )(*[_hbm(a) for a in srcs], *[_hbm(a) for a in lands], after)
    return out[0], out[1], list(out[2:2 + ns]), list(out[2 + ns:2 + ns + nl]), out[-1]


def _split_wait(name, handle, ncopies, plan, after):
    send_sems, recv_sems, srcs, lands, _ = handle
    ns, nl = len(srcs), len(lands)

    def body(*refs):
        s_sems, r_sems = refs[ns + nl], refs[ns + nl + 1]
        for k, (src, dst, peer, mine) in enumerate(plan(refs[:ns], refs[ns:ns + nl])):
            pltpu.make_async_remote_copy(src_ref=src, dst_ref=dst, send_sem=s_sems.at[k], recv_sem=r_sems.at[k],
                                         device_id=peer, device_id_type=MESH).wait_send()
            pltpu.make_async_remote_copy(src_ref=src, dst_ref=mine, send_sem=s_sems.at[k], recv_sem=r_sems.at[k],
                                         device_id=peer, device_id_type=MESH).wait_recv()

    out = pl.pallas_call(
        body, name=name,
        out_shape=(*[pltpu.HBM(a.shape, a.dtype) for a in srcs], *[pltpu.HBM(a.shape, a.dtype) for a in lands]),
        in_specs=[HBM] * (ns + nl) + [SEM, SEM, ANY],
        out_specs=tuple([HBM] * (ns + nl)),
        input_output_aliases={i: i for i in range(ns + nl)},
        compiler_params=pltpu.CompilerParams(has_side_effects=DATAFLOW),
    )(*srcs, *lands, send_sems, recv_sems, after)
    return list(out[:ns]), list(out[ns:])


def _slot(px, py, pc):
    return 4 * px + 2 * py + pc


def _gather_plan(xs, lands):
    x, y, c = _place()
    peers = [(x, y, 1 - c), (1 - x, y, c), (x, 1 - y, c), (1 - x, 1 - y, c)]
    return [(xs[t], lands[t].at[_slot(x, y, c)], peer, lands[t].at[_slot(*peer)])
            for t in range(len(xs)) for peer in peers]


def _gather_start(name, shards, after):
    lands = [lax.empty((N_DEV,) + s.shape, s.dtype) for s in shards]
    return _split_start(name, shards, lands, 4 * len(shards), _gather_plan, after)


def _pass_on_plan(_, lands):
    x, y, c = _place()
    blocks = [((1 - x, y, c), (1 - x, y, 1 - c)), ((x, 1 - y, c), (x, 1 - y, 1 - c)),
              ((1 - x, 1 - y, c), (1 - x, 1 - y, 1 - c)), ((x, y, 1 - c), (x, y, c))]
    return [(lands[t].at[_slot(*out)], lands[t].at[_slot(*out)], (x, y, 1 - c), lands[t].at[_slot(*back)])
            for t in range(len(lands)) for out, back in blocks]


def _gather_pass_on(name, handle, after):
    n = len(handle[2])
    _, lands = _split_wait(name + "_wait", handle, 4 * n, _gather_plan, after)
    return _split_start(name + "_pass_on", [], lands, 4 * n, _pass_on_plan, after)


def _gather_finish(name, handle, after):
    n = len(handle[3])
    _, lands = _split_wait(name + "_done", handle, 4 * n, _pass_on_plan, after)
    return lands


def _sibling_plan(ps, gots):
    x, y, c = _place()
    return [(ps[t].at[j, 1 - c], gots[t].at[j], (x, y, 1 - c), gots[t].at[j]) for t in range(len(ps)) for j in range(4)]


def _chips_plan(ss, gots):
    x, y, c = _place()
    chips = [(1 - x, y), (x, 1 - y), (1 - x, 1 - y)]
    return [(ss[t].at[2 * px + py], gots[t].at[k], (px, py, c), gots[t].at[k])
            for t in range(len(ss)) for k, (px, py) in enumerate(chips)]


def _sum_tile(R, C):
    return _tile(R, max(8, min(1024, ROW_BLOCK_BYTES // (C * 4))), 8)


def _pair_sum(name, p4, got, core):
    _, _, R, C = p4.shape
    tr = _sum_tile(R, C)

    def body(core_ref, p_ref, g_ref, o_ref):
        o_ref[...] = p_ref[:, 0] + g_ref[...]

    return pl.pallas_call(
        body, name=name, out_shape=jax.ShapeDtypeStruct((4, R, C), F32),
        grid_spec=pltpu.PrefetchScalarGridSpec(
            num_scalar_prefetch=1, grid=(4, R // tr),
            in_specs=[pl.BlockSpec((1, 1, tr, C), lambda j, i, core: (j, core[0], i, 0)),
                      pl.BlockSpec((1, tr, C), lambda j, i, core: (j, i, 0))],
            out_specs=pl.BlockSpec((1, tr, C), lambda j, i, core: (j, i, 0))),
        compiler_params=_params(("parallel", "parallel")),
    )(core, p4, got)


def _sum_devices(name, g8):
    _, R, C = g8.shape
    tr = _tile(R, SMALL_ROW_ALIGN, 8)

    def body(g_ref, o_ref):
        acc = g_ref[0]
        for d in range(1, N_DEV):
            acc = acc + g_ref[d]
        o_ref[...] = acc

    return pl.pallas_call(
        body, name=name, out_shape=jax.ShapeDtypeStruct((R, C), F32), grid=(R // tr,),
        in_specs=[pl.BlockSpec((N_DEV, tr, C), lambda i: (0, i, 0))], out_specs=pl.BlockSpec((tr, C), lambda i: (i, 0)),
        compiler_params=_params(("parallel",)),
    )(g8)


def _adamw_shard(name, layer, w, m, v, s4, got, chip, bufs):
    _, R, C = w.shape
    tr = _sum_tile(R, C) // 2 if _sum_tile(R, C) % 16 == 0 else _sum_tile(R, C)

    def body(chip_ref, w_ref, m_ref, v_ref, s_ref, g_ref, b0, b1, b2, b3, og, od, om, ov):
        g = ((s_ref[0] + g_ref[0]) + g_ref[1]) + g_ref[2]
        d, nm, nv = _adamw_fn(w_ref[0], g, m_ref[0], v_ref[0])
        og[0], od[0], om[0], ov[0] = g, d, nm, nv

    lay = pl.BlockSpec((1, tr, C), lambda i, chip: (layer, i, 0))
    if bufs is None:
        bufs = [lax.empty(w.shape, F32) for _ in range(4)]
    return pl.pallas_call(
        body, name=name, out_shape=[jax.ShapeDtypeStruct(w.shape, F32)] * 4,
        grid_spec=pltpu.PrefetchScalarGridSpec(
            num_scalar_prefetch=1, grid=(R // tr,),
            in_specs=[lay, lay, lay, pl.BlockSpec((1, tr, C), lambda i, chip: (chip[0], i, 0)),
                      pl.BlockSpec((3, tr, C), lambda i, chip: (0, i, 0)), ANY, ANY, ANY, ANY],
            out_specs=[lay] * 4),
        input_output_aliases={6: 0, 7: 1, 8: 2, 9: 3},
        compiler_params=_params(("parallel",)),
    )(chip, w, m, v, s4, got, *bufs)


def _adamw(name, wt, g, m, v):
    shape = wt.shape
    two = (lambda a: a.reshape(1, -1)) if wt.ndim == 1 else (lambda a: a.reshape(-1, shape[-1]))
    w2, g2, m2, v2 = two(wt), two(g), two(m), two(v)
    tr = _row(w2, None, 0)[2]
    outs = [(w2.shape, F32, 0, tr)] * 3
    d, nm, nv = _rows_fwd(name, _adamw_fn, [_row(a, tr, 0) for a in (w2, g2, m2, v2)], [], outs)
    return d.reshape(shape), nm.reshape(shape), nv.reshape(shape)


def _lane_tiling():
    return (jnp.arange(SSM_N)[:, None] == (jnp.arange(SSM_S) % SSM_N)[None, :]).astype(BF16)


def _own_block():
    r = lax.broadcasted_iota(jnp.int32, (SSM_G * SSM_P, SSM_S), 0) // SSM_P
    c = lax.broadcasted_iota(jnp.int32, (SSM_G * SSM_P, SSM_S), 1) // SSM_N
    return r == c


def _bd_build(name, v_re, v_im, sign, tiling):
    def body(r_ref, i_ref, t_ref, o_ref):
        own = _own_block()
        o_ref[:, :SSM_S] = jnp.where(own, _dot(r_ref[...].astype(BF16), t_ref[...]), 0.0).astype(BF16)
        o_ref[:, SSM_S:] = jnp.where(own, sign * _dot(i_ref[...].astype(BF16), t_ref[...]), 0.0).astype(BF16)

    rows = SSM_G * SSM_P
    return pl.pallas_call(
        body, name=name, out_shape=jax.ShapeDtypeStruct((rows, 2 * SSM_S), BF16), grid=(1,),
        in_specs=[_full_spec((rows, SSM_N))] * 2 + [_full_spec((SSM_N, SSM_S))], out_specs=_full_spec((rows, 2 * SSM_S)),
        compiler_params=_params(("arbitrary",)),
    )(v_re, v_im, tiling)


def _bd_extract(name, m, sign, tiling):
    def body(m_ref, t_ref, r_ref, i_ref):
        own, t = _own_block(), t_ref[...]

        def pick(x):
            x = jnp.where(own, x, 0.0)
            hi = x.astype(BF16)
            rest = x - hi.astype(F32)
            mid = rest.astype(BF16)
            lo = (rest - mid.astype(F32)).astype(BF16)
            return _dot(hi, t, NT) + _dot(mid, t, NT) + _dot(lo, t, NT)

        r_ref[...] = pick(m_ref[:, :SSM_S])
        i_ref[...] = sign * pick(m_ref[:, SSM_S:])

    rows = SSM_G * SSM_P
    return pl.pallas_call(
        body, name=name, out_shape=[jax.ShapeDtypeStruct((rows, SSM_N), F32)] * 2, grid=(1,),
        in_specs=[_full_spec((rows, 2 * SSM_S)), _full_spec((SSM_N, SSM_S))], out_specs=[_full_spec((rows, SSM_N))] * 2,
        compiler_params=_params(("arbitrary",)),
    )(m, tiling)


def _fold(a, dil):
    if dil == 1:
        return a
    return a.reshape((T // dil, dil) + a.shape[1:]).swapaxes(0, 1).reshape(a.shape)


def _unfold(a, dil):
    if dil == 1:
        return a
    return a.reshape((dil, T // dil) + a.shape[1:]).swapaxes(0, 1).reshape(a.shape)


DILS = (1, 4, 16)


def _fold3(parts):
    parts = [parts] * 3 if not isinstance(parts, (list, tuple)) else parts
    return jnp.stack([_fold(a, d) for a, d in zip(parts, DILS)])


def _unfold3(a):
    return [_unfold(a[p], d) for p, d in enumerate(DILS)]


def _rope_tables():
    half = ROPE // 2
    inv_freq = ROPE_THETA ** (-jnp.arange(half, dtype=F32) / half)
    ang = jnp.arange(T).astype(F32)[:, None] * inv_freq[None, :]
    i, j = jnp.arange(ROPE)[:, None], jnp.arange(ROPE)[None, :]
    rot = jnp.where(i == j + half, -1.0, jnp.where(i + half == j, 1.0, 0.0)).astype(F32)
    return jnp.tile(jnp.cos(ang), (1, 2)), jnp.tile(jnp.sin(ang), (1, 2)), rot


def _rot_half(x, rot):
    return _dot(x.reshape(-1, ROPE), rot, prec=HI).reshape(x.shape)


def _mla_pack_fn(q, kv, k_rope, cos, sin, rot):
    rope = lambda x: x * cos + _rot_half(x, rot) * sin
    q_out = jnp.concatenate([q[:, :, :NOPE], rope(q[:, :, NOPE:])], axis=-1)
    k_pe = jnp.broadcast_to(rope(k_rope)[None], (H_MLA,) + k_rope.shape)
    return q_out, jnp.concatenate([kv[:, :, :NOPE], k_pe], axis=-1), kv[:, :, NOPE:]


def _mla_unpack_fn(dq, dk, dv, cos, sin, rot):
    unrope = lambda g: g * cos - _rot_half(g * sin, rot)
    dq_out = jnp.concatenate([dq[:, :, :NOPE], unrope(dq[:, :, NOPE:])], axis=-1)
    dk_rope = unrope(jnp.sum(dk[:, :, NOPE:], axis=0))
    return dq_out, jnp.concatenate([dk[:, :, :NOPE], dv], axis=-1), dk_rope


def _from_blocks(a8):
    return a8.transpose(1, 0, 2).reshape(a8.shape[1], -1)


def _to_blocks(a):
    return a.reshape(a.shape[0], N_DEV, -1).transpose(1, 0, 2)


def _behind(a, tok):
    return a if tok is None else a + tok


def _mixer_fwd_in(x, w, sp, rope):
    s = {}
    s['x'] = x
    h = _rms_fwd("rms_mix", x, sp['g_mix'])
    proj = _from_blocks(_mm("mm_in", h, w['w_in'], 'nt', bb='r', ob='c'))
    offs = np.cumsum((0,) + IN_SPLITS)
    c_q, c_kv, k_rope, u, qd, kd, vd = [proj[:, offs[i]:offs[i + 1]] for i in range(7)]
    s.update(h=h, c_q=c_q, c_kv=c_kv, u=u)

    cqn = _rms_fwd("rms_q", c_q, sp['g_q'])
    ckvn = _rms_fwd("rms_kv", c_kv, sp['g_kv'])
    q8 = _mm("mm_uq", cqn, w['w_uq'], 'nt', bb='r', ob='c')
    kv8 = _mm("mm_ukv", ckvn, w['w_ukv'], 'nn', bb='c', ob='c')
    cos, sin, rot = rope
    tr = MLA_TQ
    qh, kh, vh = _rows_fwd(
        "mla_pack", _mla_pack_fn,
        [_row(q8, tr, 1), _row(kv8, tr, 1), _row(k_rope, tr, 0), _row(cos, tr, 0), _row(sin, tr, 0)], [rot],
        [((H_MLA, T, QK), BF16, 1, tr), ((H_MLA, T, QK), BF16, 1, tr), ((H_MLA, T, VDIM), BF16, 1, tr)])
    y_mla, lse_mla = _mla_fwd(qh, kh, vh)
    s.update(cqn=cqn, ckvn=ckvn, qh=qh, kh=kh, vh=vh, lse_mla=lse_mla, y_mla=y_mla, qd=qd, kd=kd, vd=vd)
    return s


def _mixer_fwd_out(s, w, sp, tok=None, after_ssm=None):
    x, u, y_mla, qd, kd, vd = s['x'], s['u'], s['y_mla'], s['qd'], s['kd'], s['vd']
    a3 = lambda n: sp[n].reshape(SSM_G, 1, SSM_N)
    b2 = lambda n: sp[n].transpose(0, 2, 1).reshape(SSM_G * SSM_P, SSM_N)
    disc_rows = [_row(a3('a_re'), 1, 0), _row(a3('a_im'), 1, 0), _row(sp['log_dt'].reshape(SSM_G, 1, 1), 1, 0),
                 _row(b2('b_re'), SSM_P, 0), _row(b2('b_im'), SSM_P, 0)]
    abr, abi, bbr, bbi = _rows_fwd(
        "s5_disc", _s5_disc_fn, disc_rows, [],
        [((SSM_G, 1, SSM_N), F32, 0, 1), ((SSM_G, 1, SSM_N), F32, 0, 1),
         ((SSM_G * SSM_P, SSM_N), F32, 0, SSM_P), ((SSM_G * SSM_P, SSM_N), F32, 0, SSM_P)])
    ar, ai = abr.reshape(1, SSM_S), abi.reshape(1, SSM_S)
    tiling = _lane_tiling()
    b_mat = _bd_build("s5_b_matrix", bbr, bbi, 1.0, tiling)
    c2 = lambda n: sp[n].reshape(SSM_G * SSM_P, SSM_N)
    c_mat = _bd_build("s5_c_matrix", c2('c_re'), c2('c_im'), -1.0, tiling)
    u16 = _behind(u, tok).astype(BF16)
    bu = _mm("mm_s5_b", u16, b_mat, 'nn')
    hst, hst16 = _scan_fwd(bu, ar, ai)
    ymm = _mm("mm_s5_c", hst16, c_mat, 'nt')
    d_row = sp['d_skip'].reshape(1, SSM_W)
    (yg,) = _rows_fwd("s5_act", _s5_act_fn, [_row(ymm), _row(u)], [d_row], [((T, SSM_W), BF16, -2, _row(u)[2])])
    z = _mm("mm_glu", yg, w['w_glu'], 'nn', bb='c')
    glu_rows = [_row(z[:, :SSM_W]), _row(z[:, SSM_W:])]
    glu_b = [sp['b_glu'][:SSM_W].reshape(1, -1), sp['b_glu'][SSM_W:].reshape(1, -1)]
    (y_ssm,) = _rows_fwd("s5_glu", _glu_fn, glu_rows, glu_b, [((T, SSM_W), F32, -2, glu_rows[0][2])])
    if after_ssm is not None:
        after_ssm(y_ssm)
    s.update(disc_rows=disc_rows, ar=ar, ai=ai, b_mat=b_mat, c_mat=c_mat, hst=hst, hst16=hst16, u16=u16, ymm=ymm,
             d_row=d_row, yg=yg,
             glu_rows=glu_rows, glu_b=glu_b)

    qf, kf, vf = [_fold3(a).astype(BF16) for a in (qd, kd, vd)]
    o_f, lse_f = _band_fwd(qf, kf, vf)
    mix_rows = [_row(a) for a in _unfold3(o_f) + _unfold3(lse_f)]
    (y_dil,) = _rows_fwd("dil_mix", _dil_mix_fn, mix_rows, [], [((T, DIL_W), F32, -2, mix_rows[0][2])])
    s.update(qf=qf, kf=kf, vf=vf, o_f=o_f, lse_f=lse_f, mix_rows=mix_rows)

    gm, gs, gd = sp['g_out_mla'].reshape(1, -1), sp['g_out_ssm'].reshape(1, -1), sp['g_out_dil'].reshape(1, -1)
    on_rows = [_row(y_mla), _row(y_ssm), _row(y_dil)]
    (ycat,) = _rows_fwd("out_norm", _outnorm_fn, on_rows, [gm, gs, gd], [((T, D), BF16, -2, on_rows[0][2])])
    x1_ = _mm("mm_o", ycat, w['w_o'], 'nn', bb='r', res=x)
    s.update(on_rows=on_rows, on_g=[gm, gs, gd], ycat=ycat, x1=x1_)
    for k in ('qd', 'kd', 'vd'):
        del s[k]
    return x1_


def _ffn_fwd(x1_, w, sp, s, tok=None):
    h2 = _rms_fwd("rms_ffn", x1_, _behind(sp['g_ffn'], tok))
    ga = _mm("mm_gate", h2, w['w_gate'], 'nt', bb='r', ob='c')
    gb = _mm("mm_up", h2, w['w_up'], 'nt', bb='r', ob='c')
    ffn_rows = [_row(ga, None, 1), _row(gb, None, 1)]
    (zf,) = _rows_fwd("swiglu", _swiglu_fn, ffn_rows, [], [(ga.shape, BF16, 1, ffn_rows[0][2])])
    x2_ = _mm("mm_down", zf, w['w_down'], 'nn', ab='c', bb='r', res=x1_)
    s.update(h2=h2, ffn_rows=ffn_rows, zf=zf)
    return x2_


def _b16(a):
    return a.astype(BF16)


def _ffn_bwd(dx2, s, w, sp, tok=None):
    gw, gs_ = {}, {}
    b16 = _b16
    dx2b = b16(_behind(dx2, tok))
    dzf = _mm("mm_down_dx", dx2b, w['w_down'], 'nt', bb='r', ob='c')
    gw['w_down'] = _mm("mm_down_dw", s['zf'], dx2b, 'tn', ab='c', ob='r')
    tr = s['ffn_rows'][0][2]
    dga, dgb = _rows_vjp("swiglu_bwd", _swiglu_fn, s['ffn_rows'], [], [], [_row(dzf, tr, 1)], grad_dtypes=[BF16, BF16])
    gw['w_gate'] = _mm("mm_gate_dw", dga, s['h2'], 'tn', ab='c', ob='r')
    gw['w_up'] = _mm("mm_up_dw", dgb, s['h2'], 'tn', ab='c', ob='r')
    dh2 = _mm("mm_up_dx", dgb, w['w_up'], 'nn', ab='c', bb='r',
              res=_mm("mm_gate_dx", dga, w['w_gate'], 'nn', ab='c', bb='r'))
    dx1, gs_['g_ffn'] = _rms_bwd("rms_ffn_bwd", s['x1'], sp['g_ffn'], dh2, dx2)
    return dx1, gw, gs_


def _mixer_bwd_out(dx1, s, w, sp, tok=None):
    gw, gs_ = {}, {}
    b16 = _b16
    dx1b = b16(_behind(dx1, tok))
    dycat = _mm("mm_o_dx", dx1b, w['w_o'], 'nt', bb='r')
    gw['w_o'] = _mm("mm_o_dw", s['ycat'], dx1b, 'tn', ob='r')
    dy_mla, dy_ssm, dy_dil, gs_['g_out_mla'], gs_['g_out_ssm'], gs_['g_out_dil'] = _rows_vjp(
        "out_norm_bwd", _outnorm_fn, s['on_rows'], s['on_g'], [], [_row(dycat)])

    dmix = _rows_vjp("dil_mix_bwd", _dil_mix_fn, s['mix_rows'], [], [], [_row(dy_dil)])
    dqf, dkf, dvf = _band_bwd(s['qf'], s['kf'], s['vf'], s['o_f'], s['lse_f'], _fold3(dmix[:3]), _fold3(dmix[3:]))
    back = lambda a: sum(_unfold3(a))
    dqd, dkd, dvd = back(dqf), back(dkf), back(dvf)

    dz1, dz2, db1, db2 = _rows_vjp("s5_glu_bwd", _glu_fn, s['glu_rows'], s['glu_b'], [], [_row(dy_ssm)])
    gs_['b_glu'] = jnp.concatenate([db1, db2], axis=1)
    dzb = b16(jnp.concatenate([dz1, dz2], axis=1))
    dyg = _mm("mm_glu_dx", dzb, w['w_glu'], 'nt', bb='c')
    gw['w_glu'] = _mm("mm_glu_dw", s['yg'], dzb, 'tn', ob='c')
    dymm, du_act, dd = _rows_vjp("s5_act_bwd", _s5_act_fn, [_row(s['ymm']), _row(s['u'])], [s['d_row']], [], [_row(dyg)],
                                 grad_dtypes=[BF16, F32])
    gs_['d_skip'] = dd
    dhst = _mm("mm_s5_c_dx", dymm, s['c_mat'], 'nn')
    dc_mat = _mm("mm_s5_c_dw", dymm, s['hst16'], 'tn')
    g, dar, dai = _scan_bwd(dhst, s['hst'], s['ar'], s['ai'])
    du = _mm("mm_s5_b_dx", g, s['b_mat'], 'nt', res=du_act)
    db_mat = _mm("mm_s5_b_dw", s['u16'], g, 'tn')
    tiling = _lane_tiling()
    gs_['c_re'], gs_['c_im'] = _bd_extract("s5_c_blocks", dc_mat, -1.0, tiling)
    dbbr, dbbi = _bd_extract("s5_b_blocks", db_mat, 1.0, tiling)
    disc_cts = [_row(dar.reshape(SSM_G, 1, SSM_N), 1, 0), _row(dai.reshape(SSM_G, 1, SSM_N), 1, 0),
                _row(dbbr, SSM_P, 0), _row(dbbi, SSM_P, 0)]
    da_re, da_im, dldt, db_r, db_i = _rows_vjp("s5_disc_bwd", _s5_disc_fn, s['disc_rows'], [], [], disc_cts)
    gs_['a_re'], gs_['a_im'], gs_['log_dt'] = da_re, da_im, dldt
    unb = lambda a: a.reshape(SSM_G, SSM_P, SSM_N).transpose(0, 2, 1)
    gs_['b_re'], gs_['b_im'] = unb(db_r), unb(db_i)
    return (dy_mla, du, dqd, dkd, dvd), gw, gs_


def _mixer_bwd_in(cts, dx1, s, w, sp, rope, tok=None):
    gw, gs_ = {}, {}
    b16 = _b16
    dy_mla, du, dqd, dkd, dvd = cts
    dqh, dkh, dvh = _mla_bwd(s['qh'], s['kh'], s['vh'], s['y_mla'], dy_mla, _behind(s['lse_mla'], tok))
    cos, sin, rot = rope
    tr = MLA_TQ
    dq8, dkv8, dk_rope = _rows_fwd(
        "mla_unpack", _mla_unpack_fn,
        [_row(dqh, tr, 1), _row(dkh, tr, 1), _row(dvh, tr, 1), _row(cos, tr, 0), _row(sin, tr, 0)], [rot],
        [((H_MLA, T, QK), BF16, 1, tr), ((H_MLA, T, NOPE + VDIM), BF16, 1, tr), ((T, ROPE), F32, 0, tr)])
    dcqn = _mm("mm_uq_dx", dq8, w['w_uq'], 'nn', ab='c', bb='r')
    gw['w_uq'] = _mm("mm_uq_dw", dq8, s['cqn'], 'tn', ab='c', ob='r')
    dckvn = _mm("mm_ukv_dx", dkv8, w['w_ukv'], 'nt', ab='c', bb='c')
    gw['w_ukv'] = _mm("mm_ukv_dw", s['ckvn'], dkv8, 'tn', bb='c', ob='c')
    dc_q, gs_['g_q'] = _rms_bwd("rms_q_bwd", s['c_q'], sp['g_q'], dcqn)
    dc_kv, gs_['g_kv'] = _rms_bwd("rms_kv_bwd", s['c_kv'], sp['g_kv'], dckvn)

    dproj = _to_blocks(b16(jnp.concatenate([dc_q, dc_kv, dk_rope, du, dqd, dkd, dvd], axis=1)))
    dh = _mm("mm_in_dx", dproj, w['w_in'], 'nn', ab='c', bb='r')
    gw['w_in'] = _mm("mm_in_dw", dproj, s['h'], 'tn', ab='c', ob='r')
    dx, gs_['g_mix'] = _rms_bwd("rms_mix_bwd", s['x'], sp['g_mix'], dh, dx1)
    return dx, gw, gs_


def kernel(x, g_mix, w_in, g_q, w_uq, g_kv, w_ukv, a_re, a_im, b_re, b_im, c_re, c_im, d_skip, log_dt, w_glu, b_glu, g_out_mla, g_out_ssm, g_out_dil, w_o, g_ffn, w_gate, w_up, w_down, g_final, loss_target, m_g_mix, m_w_in, m_g_q, m_w_uq, m_g_kv, m_w_ukv, m_a_re, m_a_im, m_b_re, m_b_im, m_c_re, m_c_im, m_d_skip, m_log_dt, m_w_glu, m_b_glu, m_g_out_mla, m_g_out_ssm, m_g_out_dil, m_w_o, m_g_ffn, m_w_gate, m_w_up, m_w_down, m_g_final, v_g_mix, v_w_in, v_g_q, v_w_uq, v_g_kv, v_w_ukv, v_a_re, v_a_im, v_b_re, v_b_im, v_c_re, v_c_im, v_d_skip, v_log_dt, v_w_glu, v_b_glu, v_g_out_mla, v_g_out_ssm, v_g_out_dil, v_w_o, v_g_ffn, v_w_gate, v_w_up, v_w_down, v_g_final):
    W = dict(zip(PARAMS, (g_mix, w_in, g_q, w_uq, g_kv, w_ukv, a_re, a_im, b_re, b_im, c_re, c_im, d_skip, log_dt,
                          w_glu, b_glu, g_out_mla, g_out_ssm, g_out_dil, w_o, g_ffn, w_gate, w_up, w_down, g_final)))
    M = dict(zip(PARAMS, (m_g_mix, m_w_in, m_g_q, m_w_uq, m_g_kv, m_w_ukv, m_a_re, m_a_im, m_b_re, m_b_im, m_c_re,
                          m_c_im, m_d_skip, m_log_dt, m_w_glu, m_b_glu, m_g_out_mla, m_g_out_ssm, m_g_out_dil, m_w_o,
                          m_g_ffn, m_w_gate, m_w_up, m_w_down, m_g_final)))
    V = dict(zip(PARAMS, (v_g_mix, v_w_in, v_g_q, v_w_uq, v_g_kv, v_w_ukv, v_a_re, v_a_im, v_b_re, v_b_im, v_c_re,
                          v_c_im, v_d_skip, v_log_dt, v_w_glu, v_b_glu, v_g_out_mla, v_g_out_ssm, v_g_out_dil, v_w_o,
                          v_g_ffn, v_w_gate, v_w_up, v_w_down, v_g_final)))
    cx, cy, cc = _place()
    core = cc.astype(jnp.int32).reshape(1)
    chip = (2 * cx + cy).astype(jnp.int32).reshape(1)
    rope = _rope_tables()
    small = [{n: W[n][l] for n in SMALL} for l in range(DEPTH)]
    for sp in small:
        for n in ('g_mix', 'g_q', 'g_kv', 'g_ffn'):
            sp[n] = sp[n].reshape(1, -1)

    def tok_of(tokens):
        return sum(t[0, 0] for t in tokens) if tokens else None

    def shard_view(a, n):
        return a.swapaxes(1, 2) if BIG[n] == 't' else a

    def gather_start(l, group, names, after):
        return _gather_start(f"gather_{group}_start_{l}", [shard_view(W[n], n)[l].astype(BF16) for n in names], after)

    xa = x[0]
    h1_mix = gather_start(0, "mix", MIXER_W, jnp.zeros((8, LANES), F32))
    h1_ffn = gather_start(0, "ffn", FFN_W, h1_mix[4])
    h2_mix = _gather_pass_on("gather_mix_0", h1_mix, xa)
    saved, full = [], []
    tokens = [h2_mix[4]]
    for l in range(DEPTH):
        last = l + 1 == DEPTH
        wm = dict(zip(MIXER_W, _gather_finish(f"gather_mix_{l}", h2_mix, xa)))
        sp = dict(small[l])
        sp['g_mix'] = _behind(sp['g_mix'], tok_of(tokens))
        s = _mixer_fwd_in(xa, wm, sp, rope)
        tokens = []
        first_ffn = {}
        if l == 0:
            h1_first = h1_ffn
            mid = lambda dep: first_ffn.update(h=_gather_pass_on("gather_ffn_0", h1_first, dep))
        else:
            mid = None
        if not last:
            h1_mix = gather_start(l + 1, "mix", MIXER_W, s['y_mla'])
            h1_ffn = gather_start(l + 1, "ffn", FFN_W, h1_mix[4])
            tokens += [h1_mix[4], h1_ffn[4]]
        x1 = _mixer_fwd_out(s, wm, small[l], tok_of(tokens), mid)
        if l == 0:
            h2_ffn = first_ffn['h']
        tokens = []
        wf = dict(zip(FFN_W, _gather_finish(f"gather_ffn_{l}", h2_ffn, x1)))
        if not last:
            h2_mix = _gather_pass_on(f"gather_mix_{l + 1}", h1_mix, x1)
            tokens.append(h2_mix[4])
        xa = _ffn_fwd(x1, wf, small[l], s, tok_of(tokens))
        tokens = []
        if not last:
            h2_ffn = _gather_pass_on(f"gather_ffn_{l + 1}", h1_ffn, xa)
            tokens.append(h2_ffn[4])
        saved.append(s)
        full.append({**wm, **wf})
    gf = g_final.reshape(1, D)
    ones = jnp.ones((T, 1), F32)
    dxa, dgf, loss_rows = _rows_vjp("loss", _loss_fn, [_row(xa)], [gf], [_row(loss_target[0])], [_row(ones)],
                                    primal=True)
    loss_here = _sum_rows("loss_sum", loss_rows)[0, 0]

    bufs = {n: None for n in BIG}
    pending = []

    def advance(dep):
        tokens = []
        for g in pending:
            names, tag = g['names'], g['tag']
            if g['stage'] == 0:
                p4 = [a.reshape((4, 2) + a.shape[1:]) for a in g['gw']]
                gots = [lax.empty((4,) + a.shape[1:], F32) for a in g['gw']]
                g['h'] = _split_start("rs_sibling_start_" + tag, p4, gots, 4 * len(p4), _sibling_plan, dep)
                tokens.append(g['h'][4])
            elif g['stage'] == 1:
                p4, gots = _split_wait("rs_sibling_wait_" + tag, g['h'], 4 * len(names), _sibling_plan, dep)
                s4 = [_pair_sum("rs_pair_sum_" + n, p, q, core) for n, p, q in zip(names, p4, gots)]
                gots = [lax.empty((3,) + a.shape[1:], F32) for a in s4]
                g['h'] = _split_start("rs_chips_start_" + tag, s4, gots, 3 * len(s4), _chips_plan, dep)
                tokens.append(g['h'][4])
            elif g['stage'] == 3:
                s4, gots = _split_wait("rs_chips_wait_" + tag, g['h'], 3 * len(names), _chips_plan, dep)
                for n, s4n, got in zip(names, s4, gots):
                    bufs[n] = _adamw_shard("adamw_" + n, g['layer'], shard_view(W[n], n), shard_view(M[n], n),
                                           shard_view(V[n], n), s4n, got, chip, bufs[n])
            g['stage'] += 1
        pending[:] = [g for g in pending if g['stage'] < 4]
        return tokens

    def group(names, l, gw, kind):
        return dict(names=names, layer=l, gw=[gw[n] for n in names], stage=0, tag=f"{kind}_{l}")

    g_small = [None] * DEPTH
    tokens = []
    for l in reversed(range(DEPTH)):
        dx1, gw_f, gs_f = _ffn_bwd(dxa, saved[l], full[l], small[l], tok_of(tokens))
        pending.append(group(FFN_W, l, gw_f, "ffn"))
        tokens = advance(dx1)
        cts, gw_o, gs_o = _mixer_bwd_out(dx1, saved[l], full[l], small[l], tok_of(tokens))
        pending.append(group(OUT_W, l, gw_o, "out"))
        tokens = advance(cts[0])
        dxa, gw_i, gs_i = _mixer_bwd_in(cts, dx1, saved[l], full[l], small[l], rope, tok_of(tokens))
        pending.append(group(IN_W, l, gw_i, "in"))
        tokens = advance(dxa)
        g_small[l] = {**gs_f, **gs_o, **gs_i}

    flat = [g_small[l][n].reshape(-1) for l in range(DEPTH) for n in SMALL] + [dgf.reshape(-1)]
    n_small = sum(int(f.shape[0]) for f in flat)
    rows = -(-n_small // (PACK_C * SMALL_ROW_ALIGN)) * SMALL_ROW_ALIGN
    flat = jnp.concatenate(flat + [jnp.zeros((rows * PACK_C - n_small,), F32)]).reshape(rows, PACK_C)
    (gathered,) = _all_gather("gather_small", [flat])
    tokens = advance(gathered)
    tot = _behind(_sum_devices("small_sum", gathered).reshape(-1), tok_of(tokens))
    grads, off = {}, 0
    per_layer = {n: [] for n in SMALL}
    for l in range(DEPTH):
        for n, shp in SMALL.items():
            k = int(np.prod(shp))
            per_layer[n].append(tot[off:off + k].reshape(shp))
            off += k
    for n in SMALL:
        grads[n] = jnp.stack(per_layer[n])
    grads['g_final'] = tot[off:off + D]

    delta, new_m, new_v = {}, {}, {}
    for n in PARAMS:
        if n not in BIG:
            delta[n], new_m[n], new_v[n] = _adamw("adamw_" + n, W[n], grads[n], M[n], V[n])
    while pending:
        advance(delta['g_final'])
    for n in BIG:
        grads[n], delta[n], new_m[n], new_v[n] = [shard_view(b, n) for b in bufs[n]]
    loss = lax.psum(_behind(loss_here, 0.0 * new_v['g_final'][0]), ("x", "y", "c"))
    return (loss, dxa[None], *[grads[n] for n in PARAMS], *[delta[n] for n in PARAMS],
            *[new_m[n] for n in PARAMS], *[new_v[n] for n in PARAMS])
```

```python
import jax
import jax.numpy as jnp
import numpy as np
from jax import lax
from jax.experimental import pallas as pl
from jax.experimental.pallas import tpu as pltpu

F32 = jnp.float32
BF16 = jnp.bfloat16

T = 2048
D = 2048
DEPTH = 4
N_DEV = 8
H_MLA, NOPE, ROPE, VDIM = 8, 128, 64, 128
QK = NOPE + ROPE
Q_LORA, KV_LORA = 512, 256
SSM_W, SSM_G, SSM_P, SSM_N = 512, 32, 16, 64
SSM_S = SSM_G * SSM_N
DIL_W, DIL_H, DIL_D = 512, 8, 64
BLK = 128
IN_SPLITS = (Q_LORA, KV_LORA, ROPE, SSM_W, DIL_W, DIL_W, DIL_W)
IN_W = sum(IN_SPLITS)
D_FF = 5632
EPS = 1e-6
ROPE_THETA = 10000.0
MLA_SCALE = QK ** -0.5
DIL_SCALE = DIL_D ** -0.5

ADAM_LR, ADAM_B1, ADAM_B2, ADAM_EPS, ADAM_WD, ADAM_STEP = 0.001, 0.9, 0.999, 1e-08, 0.01, 10

VMEM_LIMIT_V7X = 52 * 1024 * 1024
LANES = 128
PACK_C = 1024
ROW_BLOCK_BYTES = 2 * 1024 * 1024
MM_TM, MM_TN, MM_TK = 1024, 1024, 2816
MM_TB = 512

NT = (((1,), (1,)), ((), ()))
TN = (((0,), (0,)), ((), ()))
H_QK = (((2,), (2,)), ((0,), (0,)))
H_PV = (((2,), (1,)), ((0,), (0,)))
H_TN = (((1,), (1,)), ((0,), (0,)))
HI = lax.Precision.HIGHEST
MESH = pl.DeviceIdType.MESH

PARAMS = ['g_mix', 'w_in', 'g_q', 'w_uq', 'g_kv', 'w_ukv', 'a_re', 'a_im', 'b_re', 'b_im', 'c_re', 'c_im',
          'd_skip', 'log_dt', 'w_glu', 'b_glu', 'g_out_mla', 'g_out_ssm', 'g_out_dil', 'w_o', 'g_ffn',
          'w_gate', 'w_up', 'w_down', 'g_final']
BIG = {'w_in': 't', 'w_uq': 't', 'w_ukv': 'c', 'w_glu': 'c', 'w_o': 'r', 'w_gate': 't', 'w_up': 't', 'w_down': 'r'}
MIXER_W = ['w_in', 'w_uq', 'w_ukv', 'w_glu', 'w_o']
FFN_W = ['w_gate', 'w_up', 'w_down']
OUT_W, IN_W = ['w_o', 'w_glu'], ['w_in', 'w_uq', 'w_ukv']
SMALL = {'g_mix': (D,), 'g_q': (Q_LORA,), 'g_kv': (KV_LORA,), 'a_re': (SSM_G, SSM_N), 'a_im': (SSM_G, SSM_N),
         'b_re': (SSM_G, SSM_N, SSM_P), 'b_im': (SSM_G, SSM_N, SSM_P), 'c_re': (SSM_G, SSM_P, SSM_N),
         'c_im': (SSM_G, SSM_P, SSM_N), 'd_skip': (SSM_G, SSM_P), 'log_dt': (SSM_G,), 'b_glu': (2 * SSM_W,),
         'g_out_mla': (H_MLA * VDIM,), 'g_out_ssm': (SSM_W,), 'g_out_dil': (DIL_W,), 'g_ffn': (D,)}
SMALL_ROW_ALIGN = 64


def _tile(dim, target, align=LANES):
    best = None
    for t in range(align, min(dim, target) + 1, align):
        if dim % t == 0:
            best = t
    return best if best is not None else dim


def _params(sem=None):
    return pltpu.CompilerParams(dimension_semantics=sem, vmem_limit_bytes=VMEM_LIMIT_V7X)


def _dot(a, b, dims=None, prec=None):
    if dims is None:
        return jnp.dot(a, b, preferred_element_type=F32, precision=prec)
    return lax.dot_general(a, b, dims, preferred_element_type=F32, precision=prec)


def _mm_spec(shape, blk, t_r, t_c, rc):
    if blk is None:
        return pl.BlockSpec((t_r, t_c), rc)
    _, R, C = shape
    if blk == 'r':
        per = R // t_r
        return pl.BlockSpec((1, t_r, t_c), lambda i, j, k: (rc(i, j, k)[0] // per, rc(i, j, k)[0] % per, rc(i, j, k)[1]))
    per = C // t_c
    return pl.BlockSpec((1, t_r, t_c), lambda i, j, k: (rc(i, j, k)[1] // per, rc(i, j, k)[0], rc(i, j, k)[1] % per))


def _logical(shape, blk):
    if blk is None:
        return tuple(shape)
    G, R, C = shape
    return (G * R, C) if blk == 'r' else (R, G * C)


def _mm(name, a, b, mode, ab=None, bb=None, ob=None, res=None, prec=None):
    la, lb = _logical(a.shape, ab), _logical(b.shape, bb)
    am, ak = (0, 1) if mode != 'tn' else (1, 0)
    bk, bn = (0, 1) if mode != 'nt' else (1, 0)
    M, K, N = la[am], la[ak], lb[bn]
    assert lb[bk] == K, (name, a.shape, b.shape, mode)
    if ob is None:
        out_shape = (M, N)
    elif ob == 'r':
        G = N_DEV
        out_shape = (G, M // G, N)
    else:
        G = N_DEV
        out_shape = (G, M, N // G)
    em = min(a.shape[-2:][am], out_shape[-2])
    en = min(b.shape[-2:][bn], out_shape[-1])
    ek = min(a.shape[-2:][ak], b.shape[-2:][bk])
    dims = {'nn': None, 'nt': NT, 'tn': TN}[mode]
    a_kb = mode != 'tn' and ab == 'c'
    b_kb = (mode == 'nn' and bb == 'r') or (mode == 'nt' and bb == 'c')
    blocks = K // ek if (a_kb or b_kb) else 1
    assert blocks == 1 or ((a_kb or ab is None) and (b_kb or bb is None)), (name, ab, bb, mode)
    tk = ek if blocks > 1 else _tile(ek, MM_TK)
    nk = 1 if blocks > 1 else K // tk
    small = blocks > 1 or nk > 1
    tn = _tile(en, MM_TB if small else MM_TN)
    tm = _tile(em, MM_TB if small else (2 * MM_TM if K <= MM_TB else MM_TM))
    if tn > MM_TN:
        tm = _tile(em, MM_TB)
    if tm > 2 * MM_TM:
        tn = _tile(en, MM_TB)

    def val(ref):
        return ref[...] if len(ref.shape) == 2 else ref[0]

    def put(o_ref, r):
        if len(o_ref.shape) == 2:
            o_ref[...] = r
        else:
            o_ref[0] = r

    def k_block(ref, d, blocked, lanes):
        if blocked:
            return ref[d]
        return ref[:, d * ek:(d + 1) * ek] if lanes else ref[d * ek:(d + 1) * ek, :]

    def body(*refs):
        if res is None:
            a_ref, b_ref, o_ref = refs[:3]
            r_ref = None
        else:
            a_ref, b_ref, r_ref, o_ref = refs[:4]
        if blocks > 1:
            part = None
            for d in range(blocks):
                p = _dot(k_block(a_ref, d, a_kb, True), k_block(b_ref, d, b_kb, mode == 'nt'), dims, prec)
                part = p if part is None else part + p
        else:
            part = _dot(val(a_ref), val(b_ref), dims, prec)
        if nk == 1:
            put(o_ref, part if r_ref is None else part + val(r_ref))
            return
        acc_ref = refs[-1]
        k = pl.program_id(2)

        @pl.when(k == 0)
        def _():
            acc_ref[...] = part

        @pl.when((k > 0) & (k < nk - 1))
        def _():
            acc_ref[...] += part

        @pl.when(k == nk - 1)
        def _():
            r = acc_ref[...] + part
            put(o_ref, r if r_ref is None else r + val(r_ref))

    if blocks > 1:
        G = blocks
        a_spec = (pl.BlockSpec((G, tm, ek), lambda i, j, k: (0, i, 0)) if a_kb
                  else pl.BlockSpec((tm, K), lambda i, j, k: (i, 0)))
        if b_kb:
            b_spec = (pl.BlockSpec((G, ek, tn), lambda i, j, k: (0, 0, j)) if mode == 'nn'
                      else pl.BlockSpec((G, tn, ek), lambda i, j, k: (0, j, 0)))
        else:
            b_spec = (pl.BlockSpec((K, tn), lambda i, j, k: (0, j)) if mode == 'nn'
                      else pl.BlockSpec((tn, K), lambda i, j, k: (j, 0)))
    else:
        if mode == 'tn':
            a_spec = _mm_spec(a.shape, ab, tk, tm, lambda i, j, k: (k, i))
        else:
            a_spec = _mm_spec(a.shape, ab, tm, tk, lambda i, j, k: (i, k))
        if mode == 'nt':
            b_spec = _mm_spec(b.shape, bb, tn, tk, lambda i, j, k: (j, k))
        else:
            b_spec = _mm_spec(b.shape, bb, tk, tn, lambda i, j, k: (k, j))
    o_spec = _mm_spec(out_shape, ob, tm, tn, lambda i, j, k: (i, j))
    in_specs = [a_spec, b_spec] + ([o_spec] if res is not None else [])
    args = (a, b) + ((res,) if res is not None else ())
    return pl.pallas_call(
        body, name=name, out_shape=jax.ShapeDtypeStruct(out_shape, F32),
        grid=(M // tm, N // tn, nk), in_specs=in_specs, out_specs=o_spec,
        scratch_shapes=[pltpu.VMEM((tm, tn), F32)] if nk > 1 else [],
        compiler_params=_params(("parallel", "parallel", "arbitrary")),
    )(*args)


def _row(a, tr=None, axis=-2):
    axis = axis % a.ndim
    n = a.shape[axis]
    if tr is None:
        row_bytes = a.size // n * 4
        tr = _tile(n, max(8, min(256, ROW_BLOCK_BYTES // row_bytes)), 8)
    return (a, axis, tr)


def _row_spec(shape, axis, tr):
    nd = len(shape)
    blk = tuple(tr if d == axis else s for d, s in enumerate(shape))
    return pl.BlockSpec(blk, lambda i: tuple(i if d == axis else 0 for d in range(nd)))


def _full_spec(shape):
    nd = len(shape)
    return pl.BlockSpec(tuple(shape), lambda i: (0,) * nd)


def _steps(entries):
    ns = {a.shape[ax] // tr for a, ax, tr in entries}
    assert len(ns) == 1, [(a.shape, ax, tr) for a, ax, tr in entries]
    return ns.pop()


def _as_tuple(r):
    return tuple(r) if isinstance(r, (tuple, list)) else (r,)


def _rows_fwd(name, fn, rows, bcast, outs):
    steps = _steps(rows)
    nr, nb = len(rows), len(bcast)

    def body(*refs):
        vals = [r[...] for r in refs[:nr + nb]]
        res = _as_tuple(fn(*vals))
        for o_ref, r in zip(refs[nr + nb:], res):
            o_ref[...] = r.astype(o_ref.dtype)

    in_specs = [_row_spec(a.shape, ax, tr) for a, ax, tr in rows] + [_full_spec(b.shape) for b in bcast]
    out_specs = [_row_spec(s, ax % len(s), tr) for s, _, ax, tr in outs]
    res = pl.pallas_call(
        body, name=name, out_shape=[jax.ShapeDtypeStruct(s, dt) for s, dt, _, _ in outs],
        grid=(steps,), in_specs=in_specs, out_specs=out_specs,
        compiler_params=_params(("parallel",)),
    )(*[a for a, _, _ in rows], *bcast)
    return res


def _rows_vjp(name, fn, drows, dbc, arows, cts, primal=False, grad_dtypes=None):
    entries = list(drows) + list(arows) + list(cts)
    steps = _steps(entries)
    ndr, ndb, nar, nct = len(drows), len(dbc), len(arows), len(cts)
    gdt = list(grad_dtypes) if grad_dtypes is not None else [F32] * ndr

    def body(*refs):
        p = 0
        dr = [r[...] for r in refs[p:p + ndr]]; p += ndr
        db = [r[...] for r in refs[p:p + ndb]]; p += ndb
        ar = [r[...] for r in refs[p:p + nar]]; p += nar
        ct = [r[...] for r in refs[p:p + nct]]; p += nct
        g_rows = refs[p:p + ndr]; p += ndr
        g_bc = refs[p:p + ndb]; p += ndb
        prim_refs = refs[p:]

        def f(*d):
            return _as_tuple(fn(*d, *ar))

        outs, pullback = jax.vjp(f, *dr, *db)
        grads = pullback(tuple(c.astype(o.dtype) for c, o in zip(ct, outs)))
        for k in range(ndr):
            g_rows[k][...] = grads[k].astype(g_rows[k].dtype)
        if ndb:
            @pl.when(pl.program_id(0) == 0)
            def _():
                for r in g_bc:
                    r[...] = jnp.zeros_like(r)
            for k in range(ndb):
                g_bc[k][...] += grads[ndr + k]
        for r, o in zip(prim_refs, outs):
            r[...] = o.astype(r.dtype)

    in_specs = ([_row_spec(a.shape, ax, tr) for a, ax, tr in drows] + [_full_spec(b.shape) for b in dbc]
                + [_row_spec(a.shape, ax, tr) for a, ax, tr in arows]
                + [_row_spec(a.shape, ax, tr) for a, ax, tr in cts])
    out_shape = ([jax.ShapeDtypeStruct(a.shape, dt) for (a, _, _), dt in zip(drows, gdt)]
                 + [jax.ShapeDtypeStruct(b.shape, F32) for b in dbc])
    out_specs = ([_row_spec(a.shape, ax, tr) for a, ax, tr in drows] + [_full_spec(b.shape) for b in dbc])
    if primal:
        out_shape += [jax.ShapeDtypeStruct(a.shape, F32) for a, _, _ in cts]
        out_specs += [_row_spec(a.shape, ax, tr) for a, ax, tr in cts]
    return pl.pallas_call(
        body, name=name, out_shape=out_shape, grid=(steps,), in_specs=in_specs, out_specs=out_specs,
        compiler_params=_params(("arbitrary",)),
    )(*[a for a, _, _ in drows], *dbc, *[a for a, _, _ in arows], *[a for a, _, _ in cts])


def _rms_fn(x, g):
    return x * lax.rsqrt(jnp.mean(x * x, axis=-1, keepdims=True) + EPS) * g


def _rms_res_fn(x, g):
    return _rms_fn(x, g), x


def _s5_act_fn(ymm, u, d):
    return jax.nn.gelu(ymm + d * u)


def _glu_fn(z1, z2, b1, b2):
    return (z1 + b1) * jax.nn.sigmoid(z2 + b2)


def _outnorm_fn(ym, ys, yd, gm, gs, gd):
    return jnp.concatenate([_rms_fn(ym, gm), _rms_fn(ys, gs), _rms_fn(yd, gd)], axis=-1)


def _swiglu_fn(a, b):
    return jax.nn.silu(a) * b


def _loss_fn(x, g, tgt):
    err = _rms_fn(x, g) - tgt
    return 0.5 * jnp.mean(err * err, axis=-1, keepdims=True)


def _dil_mix_fn(o0, o1, o2, l0, l1, l2):
    m = jnp.maximum(jnp.maximum(l0, l1), l2)
    e0, e1, e2 = jnp.exp(l0 - m), jnp.exp(l1 - m), jnp.exp(l2 - m)
    s = e0 + e1 + e2
    return (e0 / s) * o0 + (e1 / s) * o1 + (e2 / s) * o2


def _s5_disc_fn(a_re, a_im, ldt, b_r, b_i):
    lr = jnp.minimum(a_re.reshape(1, SSM_N), -1e-4)
    li = a_im.reshape(1, SSM_N)
    dt = jnp.exp(ldt.reshape(1, 1))
    e = jnp.exp(lr * dt)
    ar = e * jnp.cos(li * dt)
    ai = e * jnp.sin(li * dt)
    nr, ni = ar - 1.0, ai
    den = lr * lr + li * li
    cr = (nr * lr + ni * li) / den
    ci = (ni * lr - nr * li) / den
    return ar.reshape(1, 1, SSM_N), ai.reshape(1, 1, SSM_N), cr * b_r - ci * b_i, cr * b_i + ci * b_r


def _adamw_fn(w, g, m, v):
    m = ADAM_B1 * m + (1.0 - ADAM_B1) * g
    v = ADAM_B2 * v + (1.0 - ADAM_B2) * jnp.square(g)
    m_hat = m / (1.0 - ADAM_B1 ** ADAM_STEP)
    v_hat = v / (1.0 - ADAM_B2 ** ADAM_STEP)
    delta = -ADAM_LR * (m_hat / (jnp.sqrt(v_hat) + ADAM_EPS) + ADAM_WD * w)
    return delta, m, v


def _rms_fwd(name, x, g):
    (h,) = _rows_fwd(name, _rms_fn, [_row(x)], [g], [(x.shape, BF16, -2, _row(x)[2])])
    return h


def _rms_bwd(name, x, g, dh, dres=None):
    if dres is None:
        dx, dg = _rows_vjp(name, _rms_fn, [_row(x)], [g], [], [_row(dh)])
    else:
        dx, dg = _rows_vjp(name, _rms_res_fn, [_row(x)], [g], [], [_row(dh), _row(dres)])
    return dx, dg


MLA_TQ = 256
MLA_EXT = 512


def _mla_fwd(q, k, v):
    tq = MLA_TQ

    def body(q_ref, k_ref, v_ref, o_ref, lse_ref):
        i = pl.program_id(1)
        q = q_ref[0]

        def rows_below(ext):
            s = _dot(q, k_ref[0, :ext, :], NT) * MLA_SCALE
            row = i * tq + lax.broadcasted_iota(jnp.int32, (tq, ext), 0)
            col = lax.broadcasted_iota(jnp.int32, (tq, ext), 1)
            s = jnp.where(row >= col, s, -jnp.inf)
            m = jnp.max(s, axis=-1, keepdims=True)
            p = jnp.exp(s - m)
            l = jnp.sum(p, axis=-1, keepdims=True)
            o_ref[...] = _dot((p / l).astype(BF16), v_ref[0, :ext, :])
            lse_ref[0] = m + jnp.log(l)

        for g in range(T // MLA_EXT):
            pl.when(i // (MLA_EXT // tq) == g)(lambda g=g: rows_below((g + 1) * MLA_EXT))

    return pl.pallas_call(
        body, name="mla_fwd",
        out_shape=[jax.ShapeDtypeStruct((T, H_MLA * VDIM), F32), jax.ShapeDtypeStruct((H_MLA, T, 1), F32)],
        grid=(H_MLA, T // tq),
        in_specs=[pl.BlockSpec((1, tq, QK), lambda h, i: (h, i, 0)),
                  pl.BlockSpec((1, T, QK), lambda h, i: (h, 0, 0)),
                  pl.BlockSpec((1, T, VDIM), lambda h, i: (h, 0, 0))],
        out_specs=[pl.BlockSpec((tq, VDIM), lambda h, i: (i, h)),
                   pl.BlockSpec((1, tq, 1), lambda h, i: (h, i, 0))],
        compiler_params=_params(("parallel", "parallel")),
    )(q, k, v)


def _mla_bwd(q, k, v, o, do, lse):
    tq = MLA_TQ

    def body(q_ref, k_ref, v_ref, o_ref, do_ref, lse_ref, dq_ref, dk_ref, dv_ref):
        i = pl.program_id(1)

        @pl.when(i == 0)
        def _():
            dk_ref[...] = jnp.zeros_like(dk_ref)
            dv_ref[...] = jnp.zeros_like(dv_ref)

        q, lse = q_ref[0], lse_ref[0]
        delta = jnp.sum(do_ref[...] * o_ref[...], axis=-1, keepdims=True)
        do = do_ref[...].astype(BF16)

        def rows_below(ext):
            k, v = k_ref[0, :ext, :], v_ref[0, :ext, :]
            s = _dot(q, k, NT) * MLA_SCALE
            row = i * tq + lax.broadcasted_iota(jnp.int32, (tq, ext), 0)
            col = lax.broadcasted_iota(jnp.int32, (tq, ext), 1)
            p = jnp.where(row >= col, jnp.exp(s - lse), 0.0)
            ds = (p * (_dot(do, v, NT) - delta) * MLA_SCALE).astype(BF16)
            dq_ref[0] = _dot(ds, k)
            dk_ref[0, :ext, :] += _dot(ds, q, TN)
            dv_ref[0, :ext, :] += _dot(p.astype(BF16), do, TN)

        for g in range(T // MLA_EXT):
            pl.when(i // (MLA_EXT // tq) == g)(lambda g=g: rows_below((g + 1) * MLA_EXT))

    return pl.pallas_call(
        body, name="mla_bwd",
        out_shape=[jax.ShapeDtypeStruct((H_MLA, T, QK), F32), jax.ShapeDtypeStruct((H_MLA, T, QK), F32),
                   jax.ShapeDtypeStruct((H_MLA, T, VDIM), F32)],
        grid=(H_MLA, T // tq),
        in_specs=[pl.BlockSpec((1, tq, QK), lambda h, i: (h, i, 0)),
                  pl.BlockSpec((1, T, QK), lambda h, i: (h, 0, 0)),
                  pl.BlockSpec((1, T, VDIM), lambda h, i: (h, 0, 0)),
                  pl.BlockSpec((tq, VDIM), lambda h, i: (i, h)),
                  pl.BlockSpec((tq, VDIM), lambda h, i: (i, h)),
                  pl.BlockSpec((1, tq, 1), lambda h, i: (h, i, 0))],
        out_specs=[pl.BlockSpec((1, tq, QK), lambda h, i: (h, i, 0)),
                   pl.BlockSpec((1, T, QK), lambda h, i: (h, 0, 0)),
                   pl.BlockSpec((1, T, VDIM), lambda h, i: (h, 0, 0))],
        compiler_params=_params(("parallel", "arbitrary")),
    )(q, k, v, o, do, lse)


NBLK = T // BLK


BAND_GL = 256
BAND_GH = BAND_GL // DIL_D
BAND_ROWS = BAND_GH * BLK
BAND_GROUPS = DIL_W // BAND_GL


def _band_masks():
    r = lax.broadcasted_iota(jnp.int32, (BAND_ROWS, BLK), 0) & (BLK - 1)
    j = lax.broadcasted_iota(jnp.int32, (BAND_ROWS, BLK), 1)
    return j <= r, j >= r


def _head_lanes():
    lane_head = lax.broadcasted_iota(jnp.int32, (1, BAND_GL), 1) // DIL_D
    return [lane_head == h for h in range(BAND_GH)]


def _stack_heads(x, lanes):
    return jnp.concatenate([jnp.where(m, x, jnp.zeros_like(x)) for m in lanes], axis=0)


def _merge_heads(xs, lanes):
    out = None
    for h, m in enumerate(lanes):
        part = jnp.where(m, xs[h * BLK:(h + 1) * BLK], 0.0)
        out = part if out is None else out + part
    return out


def _per_head(x, lanes):
    return jnp.concatenate([jnp.sum(jnp.where(m, x, 0.0), axis=-1, keepdims=True) for m in lanes], axis=0)


def _lane_group(ref, g):
    return ref[0, :, g * BAND_GL:(g + 1) * BAND_GL]


def _seq_start(p, i):
    per_seq = lax.shift_right_logical(jnp.int32(NBLK), 2 * p)
    return lax.rem(i, per_seq) == 0


def _band_fwd(q, k, v):
    def body(q_ref, kp_ref, kc_ref, vp_ref, vc_ref, o_ref, lse_ref):
        p, i = pl.program_id(0), pl.program_id(1)
        has_prev = jnp.logical_not(_seq_start(p, i))
        m_cur, m_prev = _band_masks()
        m_prev = m_prev & has_prev
        lanes = _head_lanes()
        for g in range(BAND_GROUPS):
            qs = _stack_heads(_lane_group(q_ref, g), lanes)
            s_c = jnp.where(m_cur, _dot(qs, _lane_group(kc_ref, g), NT) * DIL_SCALE, -jnp.inf)
            s_p = jnp.where(m_prev, _dot(qs, _lane_group(kp_ref, g), NT) * DIL_SCALE, -jnp.inf)
            m = jnp.maximum(jnp.max(s_c, axis=-1, keepdims=True), jnp.max(s_p, axis=-1, keepdims=True))
            e_c, e_p = jnp.exp(s_c - m), jnp.exp(s_p - m)
            l = jnp.sum(e_c, axis=-1, keepdims=True) + jnp.sum(e_p, axis=-1, keepdims=True)
            os = (_dot((e_p / l).astype(BF16), _lane_group(vp_ref, g))
                  + _dot((e_c / l).astype(BF16), _lane_group(vc_ref, g)))
            cols = slice(g * BAND_GL, (g + 1) * BAND_GL)
            o_ref[0, :, cols] = _merge_heads(os, lanes)
            lse_ref[0, :, cols] = _merge_heads(m + jnp.log(l), lanes)

    blk = (1, BLK, DIL_W)
    cur = lambda p, i: (p, i, 0)
    prev = lambda p, i: (p, jnp.maximum(i - 1, 0), 0)
    return pl.pallas_call(
        body, name="band_fwd",
        out_shape=[jax.ShapeDtypeStruct((3, T, DIL_W), F32)] * 2,
        grid=(3, NBLK),
        in_specs=[pl.BlockSpec(blk, cur), pl.BlockSpec(blk, prev), pl.BlockSpec(blk, cur),
                  pl.BlockSpec(blk, prev), pl.BlockSpec(blk, cur)],
        out_specs=[pl.BlockSpec(blk, cur), pl.BlockSpec(blk, cur)],
        compiler_params=_params(("parallel", "parallel")),
    )(q, k, k, v, v)


def _band_bwd(q, k, v, o, lse, do, dlse):
    def body(qc_ref, qn_ref, kp_ref, kc_ref, vp_ref, vc_ref, oc_ref, on_ref, lc_ref, ln_ref,
             doc_ref, don_ref, dlc_ref, dln_ref, dq_ref, dk_ref, dv_ref):
        p, i = pl.program_id(0), pl.program_id(1)
        has_prev = jnp.logical_not(_seq_start(p, i))
        has_next = jnp.logical_not(_seq_start(p, i + 1)) & (i + 1 < NBLK)
        m_cur, m_prev = _band_masks()
        lanes = _head_lanes()

        def probs(qs, k, lse, mask):
            return jnp.where(mask, jnp.exp(_dot(qs, k, NT) * DIL_SCALE - lse), 0.0)

        def dscore(pr, dos, v, shift):
            return (pr * (_dot(dos, v, NT) + shift) * DIL_SCALE).astype(BF16)

        for g in range(BAND_GROUPS):
            grp = lambda ref: _lane_group(ref, g)
            kp, kc, vp, vc = grp(kp_ref), grp(kc_ref), grp(vp_ref), grp(vc_ref)
            qc, qn = _stack_heads(grp(qc_ref), lanes), _stack_heads(grp(qn_ref), lanes)
            doc, don = grp(doc_ref), grp(don_ref)
            lse_c = _per_head(grp(lc_ref), lanes) * (1.0 / DIL_D)
            lse_n = _per_head(grp(ln_ref), lanes) * (1.0 / DIL_D)
            sh_c = _per_head(grp(dlc_ref) - doc * grp(oc_ref), lanes)
            sh_n = _per_head(grp(dln_ref) - don * grp(on_ref), lanes)
            doc, don = _stack_heads(doc.astype(BF16), lanes), _stack_heads(don.astype(BF16), lanes)
            p_cc = probs(qc, kc, lse_c, m_cur)
            p_cp = probs(qc, kp, lse_c, m_prev & has_prev)
            p_nc = probs(qn, kc, lse_n, m_prev & has_next)
            ds_cc = dscore(p_cc, doc, vc, sh_c)
            ds_cp = dscore(p_cp, doc, vp, sh_c)
            ds_nc = dscore(p_nc, don, vc, sh_n)
            cols = slice(g * BAND_GL, (g + 1) * BAND_GL)
            dq_ref[0, :, cols] = _merge_heads(_dot(ds_cc, kc) + _dot(ds_cp, kp), lanes)
            dk_ref[0, :, cols] = _dot(ds_cc, qc, TN) + _dot(ds_nc, qn, TN)
            dv_ref[0, :, cols] = _dot(p_cc.astype(BF16), doc, TN) + _dot(p_nc.astype(BF16), don, TN)

    blk = (1, BLK, DIL_W)
    cur = lambda p, i: (p, i, 0)
    prev = lambda p, i: (p, jnp.maximum(i - 1, 0), 0)
    nxt = lambda p, i: (p, jnp.minimum(i + 1, NBLK - 1), 0)
    w, wn, wp = pl.BlockSpec(blk, cur), pl.BlockSpec(blk, nxt), pl.BlockSpec(blk, prev)
    return pl.pallas_call(
        body, name="band_bwd",
        out_shape=[jax.ShapeDtypeStruct((3, T, DIL_W), F32)] * 3,
        grid=(3, NBLK),
        in_specs=[w, wn, wp, w, wp, w, w, wn, w, wn, w, wn, w, wn],
        out_specs=[w, w, w],
        compiler_params=_params(("parallel", "parallel")),
    )(q, q, k, k, v, v, o, o, lse, lse, do, do, dlse, dlse)


SCAN_TC = 256


def _scan_fwd(bu, ar, ai):
    tc, S = SCAN_TC, SSM_S

    def body(bu_ref, ar_ref, ai_ref, h_ref, h16_ref, cr_ref, ci_ref):
        @pl.when(pl.program_id(0) == 0)
        def _():
            cr_ref[...] = jnp.zeros_like(cr_ref)
            ci_ref[...] = jnp.zeros_like(ci_ref)

        a_r, a_i = ar_ref[...], ai_ref[...]

        def step(j, carry):
            hr, hi = carry
            for r in range(8):
                t = pl.multiple_of(j * 8, 8) + r
                br = bu_ref[pl.ds(t, 1), pl.ds(0, S)]
                bi = bu_ref[pl.ds(t, 1), pl.ds(S, S)]
                hr, hi = a_r * hr - a_i * hi + br, a_r * hi + a_i * hr + bi
                h_ref[pl.ds(t, 1), pl.ds(0, S)] = hr
                h_ref[pl.ds(t, 1), pl.ds(S, S)] = hi
            return hr, hi

        hr, hi = lax.fori_loop(0, tc // 8, step, (cr_ref[...], ci_ref[...]))
        cr_ref[...] = hr
        ci_ref[...] = hi
        h16_ref[...] = h_ref[...].astype(BF16)

    return pl.pallas_call(
        body, name="s5_scan_fwd",
        out_shape=[jax.ShapeDtypeStruct((T, 2 * S), F32), jax.ShapeDtypeStruct((T, 2 * S), BF16)],
        grid=(T // tc,),
        in_specs=[pl.BlockSpec((tc, 2 * S), lambda i: (i, 0)), _full_spec((1, S)), _full_spec((1, S))],
        out_specs=[pl.BlockSpec((tc, 2 * S), lambda i: (i, 0)), pl.BlockSpec((tc, 2 * S), lambda i: (i, 0))],
        scratch_shapes=[pltpu.VMEM((1, S), F32), pltpu.VMEM((1, S), F32)],
        compiler_params=_params(("arbitrary",)),
    )(bu, ar, ai)


def _scan_bwd(dh, h, ar, ai):
    tc, S = SCAN_TC, SSM_S
    nc = T // tc

    def body(dh_ref, h_ref, hp_ref, ar_ref, ai_ref, g16_ref, dar_ref, dai_ref, cr_ref, ci_ref, g_ref):
        i = pl.program_id(0)

        @pl.when(i == 0)
        def _():
            cr_ref[...] = jnp.zeros_like(cr_ref)
            ci_ref[...] = jnp.zeros_like(ci_ref)
            dar_ref[...] = jnp.zeros_like(dar_ref)
            dai_ref[...] = jnp.zeros_like(dai_ref)

        a_r, a_i = ar_ref[...], ai_ref[...]
        first_chunk = (i == nc - 1)
        edge = jnp.where(first_chunk, 0.0, 1.0)
        hpr = hp_ref[pl.ds(7, 1), pl.ds(0, S)] * edge
        hpi = hp_ref[pl.ds(7, 1), pl.ds(S, S)] * edge

        def step(jj, carry):
            gr, gi, dar, dai = carry
            j = tc // 8 - 1 - jj
            for r in range(7, -1, -1):
                t = pl.multiple_of(j * 8, 8) + r
                tp = jnp.maximum(t - 1, 0)
                inside = t > 0
                pr = jnp.where(inside, h_ref[pl.ds(tp, 1), pl.ds(0, S)], hpr)
                pi = jnp.where(inside, h_ref[pl.ds(tp, 1), pl.ds(S, S)], hpi)
                gr, gi = (dh_ref[pl.ds(t, 1), pl.ds(0, S)] + a_r * gr + a_i * gi,
                          dh_ref[pl.ds(t, 1), pl.ds(S, S)] + a_r * gi - a_i * gr)
                g_ref[pl.ds(t, 1), pl.ds(0, S)] = gr
                g_ref[pl.ds(t, 1), pl.ds(S, S)] = gi
                dar = dar + gr * pr + gi * pi
                dai = dai + gi * pr - gr * pi
            return gr, gi, dar, dai

        zero = jnp.zeros((1, S), F32)
        gr, gi, dar, dai = lax.fori_loop(0, tc // 8, step, (cr_ref[...], ci_ref[...], zero, zero))
        cr_ref[...] = gr
        ci_ref[...] = gi
        dar_ref[...] += dar
        dai_ref[...] += dai
        g16_ref[...] = g_ref[...].astype(BF16)

    rev = lambda i: (nc - 1 - i, 0)
    before = lambda i: (jnp.maximum((nc - 1 - i) * (tc // 8) - 1, 0), 0)
    return pl.pallas_call(
        body, name="s5_scan_bwd",
        out_shape=[jax.ShapeDtypeStruct((T, 2 * S), BF16), jax.ShapeDtypeStruct((1, S), F32),
                   jax.ShapeDtypeStruct((1, S), F32)],
        grid=(nc,),
        in_specs=[pl.BlockSpec((tc, 2 * S), rev), pl.BlockSpec((tc, 2 * S), rev), pl.BlockSpec((8, 2 * S), before),
                  _full_spec((1, S)), _full_spec((1, S))],
        out_specs=[pl.BlockSpec((tc, 2 * S), rev), _full_spec((1, S)), _full_spec((1, S))],
        scratch_shapes=[pltpu.VMEM((1, S), F32), pltpu.VMEM((1, S), F32), pltpu.VMEM((tc, 2 * S), F32)],
        compiler_params=_params(("arbitrary",)),
    )(dh, h, h, ar, ai)


def _sum_rows(name, x):
    def body(x_ref, o_ref):
        o_ref[...] = jnp.sum(x_ref[...], axis=0, keepdims=True)

    return pl.pallas_call(body, name=name, out_shape=jax.ShapeDtypeStruct((1, 1), F32),
                          in_specs=[_full_spec(x.shape)], out_specs=_full_spec((1, 1)), grid=(1,))(x)


ANY = pl.BlockSpec(memory_space=pl.ANY)


def _place():
    return lax.axis_index("x"), lax.axis_index("y"), lax.axis_index("c")


HBM = pl.BlockSpec(memory_space=pltpu.HBM)
SEM = pl.BlockSpec(memory_space=pltpu.SEMAPHORE)
DATAFLOW = pltpu.SideEffectType.DATAFLOW_SIDE_EFFECTING


def _hbm(a):
    return pltpu.with_memory_space_constraint(a, pltpu.HBM)


def _split_start(name, srcs, lands, ncopies, plan, after):
    ns, nl = len(srcs), len(lands)

    def body(*refs):
        send_sems, recv_sems = refs[ns + nl + 1], refs[ns + nl + 2]
        token = refs[-1]
        for k, (src, dst, peer, _) in enumerate(plan(refs[:ns], refs[ns:ns + nl])):
            pltpu.make_async_remote_copy(src_ref=src, dst_ref=dst, send_sem=send_sems.at[k], recv_sem=recv_sems.at[k],
                                         device_id=peer, device_id_type=MESH).start()
        token[...] = jnp.zeros_like(token)

    out = pl.pallas_call(
        body, name=name,
        out_shape=(pltpu.SemaphoreType.DMA((ncopies,)), pltpu.SemaphoreType.DMA((ncopies,)),
                   *[pltpu.HBM(a.shape, a.dtype) for a in srcs], *[pltpu.HBM(a.shape, a.dtype) for a in lands],
                   jax.ShapeDtypeStruct((8, LANES), F32)),
        in_specs=[HBM] * (ns + nl) + [ANY],
        out_specs=(SEM, SEM, *[HBM] * (ns + nl), pl.BlockSpec(memory_space=pltpu.VMEM)),
        input_output_aliases={i: 2 + i for i in range(ns + nl)},
        compiler_params=pltpu.CompilerParams(has_side_effects=DATAFLOW),
    )(*[_hbm(a) for a in srcs], *[_hbm(a) for a in lands], after)
    return out[0], out[1], list(out[2:2 + ns]), list(out[2 + ns:2 + ns + nl]), out[-1]


def _split_wait(name, handle, ncopies, plan, after):
    send_sems, recv_sems, srcs, lands, _ = handle
    ns, nl = len(srcs), len(lands)

    def body(*refs):
        s_sems, r_sems = refs[ns + nl], refs[ns + nl + 1]
        for k, (src, dst, peer, mine) in enumerate(plan(refs[:ns], refs[ns:ns + nl])):
            pltpu.make_async_remote_copy(src_ref=src, dst_ref=dst, send_sem=s_sems.at[k], recv_sem=r_sems.at[k],
                                         device_id=peer, device_id_type=MESH).wait_send()
            pltpu.make_async_remote_copy(src_ref=src, dst_ref=mine, send_sem=s_sems.at[k], recv_sem=r_sems.at[k],
                                         device_id=peer, device_id_type=MESH).wait_recv()

    out = pl.pallas_call(
        body, name=name,
        out_shape=(*[pltpu.HBM(a.shape, a.dtype) for a in srcs], *[pltpu.HBM(a.shape, a.dtype) for a in lands]),
        in_specs=[HBM] * (ns + nl) + [SEM, SEM, ANY],
        out_specs=tuple([HBM] * (ns + nl)),
        input_output_aliases={i: i for i in range(ns + nl)},
        compiler_params=pltpu.CompilerParams(has_side_effects=DATAFLOW),
    )(*srcs, *lands, send_sems, recv_sems, after)
    return list(out[:ns]), list(out[ns:])


def _slot(px, py, pc):
    return 4 * px + 2 * py + pc


def _gather_plan(xs, lands):
    x, y, c = _place()
    peers = [(x, y, 1 - c), (1 - x, y, c), (x, 1 - y, c), (1 - x, 1 - y, c)]
    return [(xs[t], lands[t].at[_slot(x, y, c)], peer, lands[t].at[_slot(*peer)])
            for t in range(len(xs)) for peer in peers]


def _gather_start(name, shards, after):
    lands = [lax.empty((N_DEV,) + s.shape, s.dtype) for s in shards]
    return _split_start(name, shards, lands, 4 * len(shards), _gather_plan, after)


def _pass_on_plan(_, lands):
    x, y, c = _place()
    blocks = [((1 - x, y, c), (1 - x, y, 1 - c)), ((x, 1 - y, c), (x, 1 - y, 1 - c)),
              ((1 - x, 1 - y, c), (1 - x, 1 - y, 1 - c)), ((x, y, 1 - c), (x, y, c))]
    return [(lands[t].at[_slot(*out)], lands[t].at[_slot(*out)], (x, y, 1 - c), lands[t].at[_slot(*back)])
            for t in range(len(lands)) for out, back in blocks]


def _gather_pass_on(name, handle, after):
    n = len(handle[2])
    _, lands = _split_wait(name + "_wait", handle, 4 * n, _gather_plan, after)
    return _split_start(name + "_pass_on", [], lands, 4 * n, _pass_on_plan, after)


def _gather_finish(name, handle, after):
    n = len(handle[3])
    _, lands = _split_wait(name + "_done", handle, 4 * n, _pass_on_plan, after)
    return lands


def _sibling_plan(ps, gots):
    x, y, c = _place()
    return [(ps[t].at[j, 1 - c], gots[t].at[j], (x, y, 1 - c), gots[t].at[j]) for t in range(len(ps)) for j in range(4)]


def _chips_plan(ss, gots):
    x, y, c = _place()
    chips = [(1 - x, y), (x, 1 - y), (1 - x, 1 - y)]
    return [(ss[t].at[2 * px + py], gots[t].at[k], (px, py, c), gots[t].at[k])
            for t in range(len(ss)) for k, (px, py) in enumerate(chips)]


def _sum_tile(R, C):
    return _tile(R, max(8, min(1024, ROW_BLOCK_BYTES // (C * 4))), 8)


def _pair_sum(name, p4, got, core):
    _, _, R, C = p4.shape
    tr = _sum_tile(R, C)

    def body(core_ref, p_ref, g_ref, o_ref):
        o_ref[...] = p_ref[:, 0] + g_ref[...]

    return pl.pallas_call(
        body, name=name, out_shape=jax.ShapeDtypeStruct((4, R, C), F32),
        grid_spec=pltpu.PrefetchScalarGridSpec(
            num_scalar_prefetch=1, grid=(4, R // tr),
            in_specs=[pl.BlockSpec((1, 1, tr, C), lambda j, i, core: (j, core[0], i, 0)),
                      pl.BlockSpec((1, tr, C), lambda j, i, core: (j, i, 0))],
            out_specs=pl.BlockSpec((1, tr, C), lambda j, i, core: (j, i, 0))),
        compiler_params=_params(("parallel", "parallel")),
    )(core, p4, got)


def _sum_devices(name, g8):
    _, R, C = g8.shape
    tr = _tile(R, SMALL_ROW_ALIGN, 8)

    def body(g_ref, o_ref):
        acc = g_ref[0]
        for d in range(1, N_DEV):
            acc = acc + g_ref[d]
        o_ref[...] = acc

    return pl.pallas_call(
        body, name=name, out_shape=jax.ShapeDtypeStruct((R, C), F32), grid=(R // tr,),
        in_specs=[pl.BlockSpec((N_DEV, tr, C), lambda i: (0, i, 0))], out_specs=pl.BlockSpec((tr, C), lambda i: (i, 0)),
        compiler_params=_params(("parallel",)),
    )(g8)


def _adamw_shard(name, layer, w, m, v, s4, got, chip, bufs):
    _, R, C = w.shape
    tr = _sum_tile(R, C) // 2 if _sum_tile(R, C) % 16 == 0 else _sum_tile(R, C)

    def body(chip_ref, w_ref, m_ref, v_ref, s_ref, g_ref, b0, b1, b2, b3, og, od, om, ov):
        g = ((s_ref[0] + g_ref[0]) + g_ref[1]) + g_ref[2]
        d, nm, nv = _adamw_fn(w_ref[0], g, m_ref[0], v_ref[0])
        og[0], od[0], om[0], ov[0] = g, d, nm, nv

    lay = pl.BlockSpec((1, tr, C), lambda i, chip: (layer, i, 0))
    if bufs is None:
        bufs = [lax.empty(w.shape, F32) for _ in range(4)]
    return pl.pallas_call(
        body, name=name, out_shape=[jax.ShapeDtypeStruct(w.shape, F32)] * 4,
        grid_spec=pltpu.PrefetchScalarGridSpec(
            num_scalar_prefetch=1, grid=(R // tr,),
            in_specs=[lay, lay, lay, pl.BlockSpec((1, tr, C), lambda i, chip: (chip[0], i, 0)),
                      pl.BlockSpec((3, tr, C), lambda i, chip: (0, i, 0)), ANY, ANY, ANY, ANY],
            out_specs=[lay] * 4),
        input_output_aliases={6: 0, 7: 1, 8: 2, 9: 3},
        compiler_params=_params(("parallel",)),
    )(chip, w, m, v, s4, got, *bufs)


def _adamw(name, wt, g, m, v):
    shape = wt.shape
    two = (lambda a: a.reshape(1, -1)) if wt.ndim == 1 else (lambda a: a.reshape(-1, shape[-1]))
    w2, g2, m2, v2 = two(wt), two(g), two(m), two(v)
    tr = _row(w2, None, 0)[2]
    outs = [(w2.shape, F32, 0, tr)] * 3
    d, nm, nv = _rows_fwd(name, _adamw_fn, [_row(a, tr, 0) for a in (w2, g2, m2, v2)], [], outs)
    return d.reshape(shape), nm.reshape(shape), nv.reshape(shape)


def _lane_tiling():
    return (jnp.arange(SSM_N)[:, None] == (jnp.arange(SSM_S) % SSM_N)[None, :]).astype(BF16)


def _own_block():
    r = lax.broadcasted_iota(jnp.int32, (SSM_G * SSM_P, SSM_S), 0) // SSM_P
    c = lax.broadcasted_iota(jnp.int32, (SSM_G * SSM_P, SSM_S), 1) // SSM_N
    return r == c


def _bd_build(name, v_re, v_im, sign, tiling):
    def body(r_ref, i_ref, t_ref, o_ref):
        own = _own_block()
        o_ref[:, :SSM_S] = jnp.where(own, _dot(r_ref[...].astype(BF16), t_ref[...]), 0.0).astype(BF16)
        o_ref[:, SSM_S:] = jnp.where(own, sign * _dot(i_ref[...].astype(BF16), t_ref[...]), 0.0).astype(BF16)

    rows = SSM_G * SSM_P
    return pl.pallas_call(
        body, name=name, out_shape=jax.ShapeDtypeStruct((rows, 2 * SSM_S), BF16), grid=(1,),
        in_specs=[_full_spec((rows, SSM_N))] * 2 + [_full_spec((SSM_N, SSM_S))], out_specs=_full_spec((rows, 2 * SSM_S)),
        compiler_params=_params(("arbitrary",)),
    )(v_re, v_im, tiling)


def _bd_extract(name, m, sign, tiling):
    def body(m_ref, t_ref, r_ref, i_ref):
        own, t = _own_block(), t_ref[...]

        def pick(x):
            x = jnp.where(own, x, 0.0)
            hi = x.astype(BF16)
            rest = x - hi.astype(F32)
            mid = rest.astype(BF16)
            lo = (rest - mid.astype(F32)).astype(BF16)
            return _dot(hi, t, NT) + _dot(mid, t, NT) + _dot(lo, t, NT)

        r_ref[...] = pick(m_ref[:, :SSM_S])
        i_ref[...] = sign * pick(m_ref[:, SSM_S:])

    rows = SSM_G * SSM_P
    return pl.pallas_call(
        body, name=name, out_shape=[jax.ShapeDtypeStruct((rows, SSM_N), F32)] * 2, grid=(1,),
        in_specs=[_full_spec((rows, 2 * SSM_S)), _full_spec((SSM_N, SSM_S))], out_specs=[_full_spec((rows, SSM_N))] * 2,
        compiler_params=_params(("arbitrary",)),
    )(m, tiling)


def _fold(a, dil):
    if dil == 1:
        return a
    return a.reshape((T // dil, dil) + a.shape[1:]).swapaxes(0, 1).reshape(a.shape)


def _unfold(a, dil):
    if dil == 1:
        return a
    return a.reshape((dil, T // dil) + a.shape[1:]).swapaxes(0, 1).reshape(a.shape)


DILS = (1, 4, 16)


def _fold3(parts):
    parts = [parts] * 3 if not isinstance(parts, (list, tuple)) else parts
    return jnp.stack([_fold(a, d) for a, d in zip(parts, DILS)])


def _unfold3(a):
    return [_unfold(a[p], d) for p, d in enumerate(DILS)]


def _rope_tables():
    half = ROPE // 2
    inv_freq = ROPE_THETA ** (-jnp.arange(half, dtype=F32) / half)
    ang = jnp.arange(T).astype(F32)[:, None] * inv_freq[None, :]
    i, j = jnp.arange(ROPE)[:, None], jnp.arange(ROPE)[None, :]
    rot = jnp.where(i == j + half, -1.0, jnp.where(i + half == j, 1.0, 0.0)).astype(F32)
    return jnp.tile(jnp.cos(ang), (1, 2)), jnp.tile(jnp.sin(ang), (1, 2)), rot


def _rot_half(x, rot):
    return _dot(x.reshape(-1, ROPE), rot, prec=HI).reshape(x.shape)


def _mla_pack_fn(q, kv, k_rope, cos, sin, rot):
    rope = lambda x: x * cos + _rot_half(x, rot) * sin
    q_out = jnp.concatenate([q[:, :, :NOPE], rope(q[:, :, NOPE:])], axis=-1)
    k_pe = jnp.broadcast_to(rope(k_rope)[None], (H_MLA,) + k_rope.shape)
    return q_out, jnp.concatenate([kv[:, :, :NOPE], k_pe], axis=-1), kv[:, :, NOPE:]


def _mla_unpack_fn(dq, dk, dv, cos, sin, rot):
    unrope = lambda g: g * cos - _rot_half(g * sin, rot)
    dq_out = jnp.concatenate([dq[:, :, :NOPE], unrope(dq[:, :, NOPE:])], axis=-1)
    dk_rope = unrope(jnp.sum(dk[:, :, NOPE:], axis=0))
    return dq_out, jnp.concatenate([dk[:, :, :NOPE], dv], axis=-1), dk_rope


def _flat(w8):
    return w8.reshape(-1, w8.shape[-1])


def _blocks(m):
    return m.reshape(N_DEV, -1, m.shape[-1])


def _behind(a, tok):
    return a if tok is None else a + tok


def _mixer_fwd_in(x, w, sp, rope):
    s = {}
    s['x'] = x
    h = _rms_fwd("rms_mix", x, sp['g_mix'])
    proj = _mm("mm_in", h, _flat(w['w_in']), 'nt')
    offs = np.cumsum((0,) + IN_SPLITS)
    c_q, c_kv, k_rope, u, qd, kd, vd = [proj[:, offs[i]:offs[i + 1]] for i in range(7)]
    s.update(h=h, c_q=c_q, c_kv=c_kv, u=u)

    cqn = _rms_fwd("rms_q", c_q, sp['g_q'])
    ckvn = _rms_fwd("rms_kv", c_kv, sp['g_kv'])
    q8 = _mm("mm_uq", cqn, w['w_uq'], 'nt', bb='r', ob='c')
    kv8 = _mm("mm_ukv", ckvn, w['w_ukv'], 'nn', bb='c', ob='c')
    cos, sin, rot = rope
    tr = MLA_TQ
    qh, kh, vh = _rows_fwd(
        "mla_pack", _mla_pack_fn,
        [_row(q8, tr, 1), _row(kv8, tr, 1), _row(k_rope, tr, 0), _row(cos, tr, 0), _row(sin, tr, 0)], [rot],
        [((H_MLA, T, QK), BF16, 1, tr), ((H_MLA, T, QK), BF16, 1, tr), ((H_MLA, T, VDIM), BF16, 1, tr)])
    y_mla, lse_mla = _mla_fwd(qh, kh, vh)
    s.update(cqn=cqn, ckvn=ckvn, qh=qh, kh=kh, vh=vh, lse_mla=lse_mla, y_mla=y_mla, qd=qd, kd=kd, vd=vd)
    return s


def _mixer_fwd_out(s, w, sp, tok=None, after_ssm=None):
    x, u, y_mla, qd, kd, vd = s['x'], s['u'], s['y_mla'], s['qd'], s['kd'], s['vd']
    a3 = lambda n: sp[n].reshape(SSM_G, 1, SSM_N)
    b2 = lambda n: sp[n].transpose(0, 2, 1).reshape(SSM_G * SSM_P, SSM_N)
    disc_rows = [_row(a3('a_re'), 1, 0), _row(a3('a_im'), 1, 0), _row(sp['log_dt'].reshape(SSM_G, 1, 1), 1, 0),
                 _row(b2('b_re'), SSM_P, 0), _row(b2('b_im'), SSM_P, 0)]
    abr, abi, bbr, bbi = _rows_fwd(
        "s5_disc", _s5_disc_fn, disc_rows, [],
        [((SSM_G, 1, SSM_N), F32, 0, 1), ((SSM_G, 1, SSM_N), F32, 0, 1),
         ((SSM_G * SSM_P, SSM_N), F32, 0, SSM_P), ((SSM_G * SSM_P, SSM_N), F32, 0, SSM_P)])
    ar, ai = abr.reshape(1, SSM_S), abi.reshape(1, SSM_S)
    tiling = _lane_tiling()
    b_mat = _bd_build("s5_b_matrix", bbr, bbi, 1.0, tiling)
    c2 = lambda n: sp[n].reshape(SSM_G * SSM_P, SSM_N)
    c_mat = _bd_build("s5_c_matrix", c2('c_re'), c2('c_im'), -1.0, tiling)
    u16 = _behind(u, tok).astype(BF16)
    bu = _mm("mm_s5_b", u16, b_mat, 'nn')
    hst, hst16 = _scan_fwd(bu, ar, ai)
    ymm = _mm("mm_s5_c", hst16, c_mat, 'nt')
    d_row = sp['d_skip'].reshape(1, SSM_W)
    (yg,) = _rows_fwd("s5_act", _s5_act_fn, [_row(ymm), _row(u)], [d_row], [((T, SSM_W), BF16, -2, _row(u)[2])])
    z = _mm("mm_glu", yg, w['w_glu'], 'nn', bb='c')
    glu_rows = [_row(z[:, :SSM_W]), _row(z[:, SSM_W:])]
    glu_b = [sp['b_glu'][:SSM_W].reshape(1, -1), sp['b_glu'][SSM_W:].reshape(1, -1)]
    (y_ssm,) = _rows_fwd("s5_glu", _glu_fn, glu_rows, glu_b, [((T, SSM_W), F32, -2, glu_rows[0][2])])
    if after_ssm is not None:
        after_ssm(y_ssm)
    s.update(disc_rows=disc_rows, ar=ar, ai=ai, b_mat=b_mat, c_mat=c_mat, hst=hst, hst16=hst16, u16=u16, ymm=ymm,
             d_row=d_row, yg=yg,
             glu_rows=glu_rows, glu_b=glu_b)

    qf, kf, vf = [_fold3(a).astype(BF16) for a in (qd, kd, vd)]
    o_f, lse_f = _band_fwd(qf, kf, vf)
    mix_rows = [_row(a) for a in _unfold3(o_f) + _unfold3(lse_f)]
    (y_dil,) = _rows_fwd("dil_mix", _dil_mix_fn, mix_rows, [], [((T, DIL_W), F32, -2, mix_rows[0][2])])
    s.update(qf=qf, kf=kf, vf=vf, o_f=o_f, lse_f=lse_f, mix_rows=mix_rows)

    gm, gs, gd = sp['g_out_mla'].reshape(1, -1), sp['g_out_ssm'].reshape(1, -1), sp['g_out_dil'].reshape(1, -1)
    on_rows = [_row(y_mla), _row(y_ssm), _row(y_dil)]
    (ycat,) = _rows_fwd("out_norm", _outnorm_fn, on_rows, [gm, gs, gd], [((T, D), BF16, -2, on_rows[0][2])])
    x1_ = _mm("mm_o", ycat, _flat(w['w_o']), 'nn', res=x)
    s.update(on_rows=on_rows, on_g=[gm, gs, gd], ycat=ycat, x1=x1_)
    for k in ('qd', 'kd', 'vd'):
        del s[k]
    return x1_


def _ffn_fwd(x1_, w, sp, s, tok=None):
    h2 = _rms_fwd("rms_ffn", x1_, _behind(sp['g_ffn'], tok))
    ga = _mm("mm_gate", h2, _flat(w['w_gate']), 'nt')
    gb = _mm("mm_up", h2, _flat(w['w_up']), 'nt')
    ffn_rows = [_row(ga), _row(gb)]
    (zf,) = _rows_fwd("swiglu", _swiglu_fn, ffn_rows, [], [(ga.shape, BF16, -2, ffn_rows[0][2])])
    x2_ = _mm("mm_down", zf, _flat(w['w_down']), 'nn', res=x1_)
    s.update(h2=h2, ffn_rows=ffn_rows, zf=zf)
    return x2_


def _b16(a):
    return a.astype(BF16)


def _ffn_bwd(dx2, s, w, sp, tok=None):
    gw, gs_ = {}, {}
    b16 = _b16
    dx2b = b16(_behind(dx2, tok))
    dzf = _mm("mm_down_dx", dx2b, _flat(w['w_down']), 'nt')
    gw['w_down'] = _blocks(_mm("mm_down_dw", s['zf'], dx2b, 'tn'))
    dga, dgb = _rows_vjp("swiglu_bwd", _swiglu_fn, s['ffn_rows'], [], [], [_row(dzf)], grad_dtypes=[BF16, BF16])
    gw['w_gate'] = _blocks(_mm("mm_gate_dw", dga, s['h2'], 'tn'))
    gw['w_up'] = _blocks(_mm("mm_up_dw", dgb, s['h2'], 'tn'))
    dh2 = _mm("mm_up_dx", dgb, _flat(w['w_up']), 'nn', res=_mm("mm_gate_dx", dga, _flat(w['w_gate']), 'nn'))
    dx1, gs_['g_ffn'] = _rms_bwd("rms_ffn_bwd", s['x1'], sp['g_ffn'], dh2, dx2)
    return dx1, gw, gs_


def _mixer_bwd_out(dx1, s, w, sp, tok=None):
    gw, gs_ = {}, {}
    b16 = _b16
    dx1b = b16(_behind(dx1, tok))
    dycat = _mm("mm_o_dx", dx1b, _flat(w['w_o']), 'nt')
    gw['w_o'] = _blocks(_mm("mm_o_dw", s['ycat'], dx1b, 'tn'))
    dy_mla, dy_ssm, dy_dil, gs_['g_out_mla'], gs_['g_out_ssm'], gs_['g_out_dil'] = _rows_vjp(
        "out_norm_bwd", _outnorm_fn, s['on_rows'], s['on_g'], [], [_row(dycat)])

    dmix = _rows_vjp("dil_mix_bwd", _dil_mix_fn, s['mix_rows'], [], [], [_row(dy_dil)])
    dqf, dkf, dvf = _band_bwd(s['qf'], s['kf'], s['vf'], s['o_f'], s['lse_f'], _fold3(dmix[:3]), _fold3(dmix[3:]))
    back = lambda a: sum(_unfold3(a))
    dqd, dkd, dvd = back(dqf), back(dkf), back(dvf)

    dz1, dz2, db1, db2 = _rows_vjp("s5_glu_bwd", _glu_fn, s['glu_rows'], s['glu_b'], [], [_row(dy_ssm)])
    gs_['b_glu'] = jnp.concatenate([db1, db2], axis=1)
    dzb = b16(jnp.concatenate([dz1, dz2], axis=1))
    dyg = _mm("mm_glu_dx", dzb, w['w_glu'], 'nt', bb='c')
    gw['w_glu'] = _mm("mm_glu_dw", s['yg'], dzb, 'tn', ob='c')
    dymm, du_act, dd = _rows_vjp("s5_act_bwd", _s5_act_fn, [_row(s['ymm']), _row(s['u'])], [s['d_row']], [], [_row(dyg)],
                                 grad_dtypes=[BF16, F32])
    gs_['d_skip'] = dd
    dhst = _mm("mm_s5_c_dx", dymm, s['c_mat'], 'nn')
    dc_mat = _mm("mm_s5_c_dw", dymm, s['hst16'], 'tn')
    g, dar, dai = _scan_bwd(dhst, s['hst'], s['ar'], s['ai'])
    du = _mm("mm_s5_b_dx", g, s['b_mat'], 'nt', res=du_act)
    db_mat = _mm("mm_s5_b_dw", s['u16'], g, 'tn')
    tiling = _lane_tiling()
    gs_['c_re'], gs_['c_im'] = _bd_extract("s5_c_blocks", dc_mat, -1.0, tiling)
    dbbr, dbbi = _bd_extract("s5_b_blocks", db_mat, 1.0, tiling)
    disc_cts = [_row(dar.reshape(SSM_G, 1, SSM_N), 1, 0), _row(dai.reshape(SSM_G, 1, SSM_N), 1, 0),
                _row(dbbr, SSM_P, 0), _row(dbbi, SSM_P, 0)]
    da_re, da_im, dldt, db_r, db_i = _rows_vjp("s5_disc_bwd", _s5_disc_fn, s['disc_rows'], [], [], disc_cts)
    gs_['a_re'], gs_['a_im'], gs_['log_dt'] = da_re, da_im, dldt
    unb = lambda a: a.reshape(SSM_G, SSM_P, SSM_N).transpose(0, 2, 1)
    gs_['b_re'], gs_['b_im'] = unb(db_r), unb(db_i)
    return (dy_mla, du, dqd, dkd, dvd), gw, gs_


def _mixer_bwd_in(cts, dx1, s, w, sp, rope, tok=None):
    gw, gs_ = {}, {}
    b16 = _b16
    dy_mla, du, dqd, dkd, dvd = cts
    dqh, dkh, dvh = _mla_bwd(s['qh'], s['kh'], s['vh'], s['y_mla'], dy_mla, _behind(s['lse_mla'], tok))
    cos, sin, rot = rope
    tr = MLA_TQ
    dq8, dkv8, dk_rope = _rows_fwd(
        "mla_unpack", _mla_unpack_fn,
        [_row(dqh, tr, 1), _row(dkh, tr, 1), _row(dvh, tr, 1), _row(cos, tr, 0), _row(sin, tr, 0)], [rot],
        [((H_MLA, T, QK), BF16, 1, tr), ((H_MLA, T, NOPE + VDIM), BF16, 1, tr), ((T, ROPE), F32, 0, tr)])
    dcqn = _mm("mm_uq_dx", dq8, w['w_uq'], 'nn', ab='c', bb='r')
    gw['w_uq'] = _mm("mm_uq_dw", dq8, s['cqn'], 'tn', ab='c', ob='r')
    dckvn = _mm("mm_ukv_dx", dkv8, w['w_ukv'], 'nt', ab='c', bb='c')
    gw['w_ukv'] = _mm("mm_ukv_dw", s['ckvn'], dkv8, 'tn', bb='c', ob='c')
    dc_q, gs_['g_q'] = _rms_bwd("rms_q_bwd", s['c_q'], sp['g_q'], dcqn)
    dc_kv, gs_['g_kv'] = _rms_bwd("rms_kv_bwd", s['c_kv'], sp['g_kv'], dckvn)

    dproj = b16(jnp.concatenate([dc_q, dc_kv, dk_rope, du, dqd, dkd, dvd], axis=1))
    dh = _mm("mm_in_dx", dproj, _flat(w['w_in']), 'nn')
    gw['w_in'] = _blocks(_mm("mm_in_dw", dproj, s['h'], 'tn'))
    dx, gs_['g_mix'] = _rms_bwd("rms_mix_bwd", s['x'], sp['g_mix'], dh, dx1)
    return dx, gw, gs_


def kernel(x, g_mix, w_in, g_q, w_uq, g_kv, w_ukv, a_re, a_im, b_re, b_im, c_re, c_im, d_skip, log_dt, w_glu, b_glu, g_out_mla, g_out_ssm, g_out_dil, w_o, g_ffn, w_gate, w_up, w_down, g_final, loss_target, m_g_mix, m_w_in, m_g_q, m_w_uq, m_g_kv, m_w_ukv, m_a_re, m_a_im, m_b_re, m_b_im, m_c_re, m_c_im, m_d_skip, m_log_dt, m_w_glu, m_b_glu, m_g_out_mla, m_g_out_ssm, m_g_out_dil, m_w_o, m_g_ffn, m_w_gate, m_w_up, m_w_down, m_g_final, v_g_mix, v_w_in, v_g_q, v_w_uq, v_g_kv, v_w_ukv, v_a_re, v_a_im, v_b_re, v_b_im, v_c_re, v_c_im, v_d_skip, v_log_dt, v_w_glu, v_b_glu, v_g_out_mla, v_g_out_ssm, v_g_out_dil, v_w_o, v_g_ffn, v_w_gate, v_w_up, v_w_down, v_g_final):
    W = dict(zip(PARAMS, (g_mix, w_in, g_q, w_uq, g_kv, w_ukv, a_re, a_im, b_re, b_im, c_re, c_im, d_skip, log_dt,
                          w_glu, b_glu, g_out_mla, g_out_ssm, g_out_dil, w_o, g_ffn, w_gate, w_up, w_down, g_final)))
    M = dict(zip(PARAMS, (m_g_mix, m_w_in, m_g_q, m_w_uq, m_g_kv, m_w_ukv, m_a_re, m_a_im, m_b_re, m_b_im, m_c_re,
                          m_c_im, m_d_skip, m_log_dt, m_w_glu, m_b_glu, m_g_out_mla, m_g_out_ssm, m_g_out_dil, m_w_o,
                          m_g_ffn, m_w_gate, m_w_up, m_w_down, m_g_final)))
    V = dict(zip(PARAMS, (v_g_mix, v_w_in, v_g_q, v_w_uq, v_g_kv, v_w_ukv, v_a_re, v_a_im, v_b_re, v_b_im, v_c_re,
                          v_c_im, v_d_skip, v_log_dt, v_w_glu, v_b_glu, v_g_out_mla, v_g_out_ssm, v_g_out_dil, v_w_o,
                          v_g_ffn, v_w_gate, v_w_up, v_w_down, v_g_final)))
    cx, cy, cc = _place()
    core = cc.astype(jnp.int32).reshape(1)
    chip = (2 * cx + cy).astype(jnp.int32).reshape(1)
    rope = _rope_tables()
    small = [{n: W[n][l] for n in SMALL} for l in range(DEPTH)]
    for sp in small:
        for n in ('g_mix', 'g_q', 'g_kv', 'g_ffn'):
            sp[n] = sp[n].reshape(1, -1)

    def tok_of(tokens):
        return sum(t[0, 0] for t in tokens) if tokens else None

    def shard_view(a, n):
        return a.swapaxes(1, 2) if BIG[n] == 't' else a

    def gather_start(l, group, names, after):
        return _gather_start(f"gather_{group}_start_{l}", [shard_view(W[n], n)[l].astype(BF16) for n in names], after)

    xa = x[0]
    h1_mix = gather_start(0, "mix", MIXER_W, jnp.zeros((8, LANES), F32))
    h1_ffn = gather_start(0, "ffn", FFN_W, h1_mix[4])
    h2_mix = _gather_pass_on("gather_mix_0", h1_mix, xa)
    saved, full = [], []
    tokens = [h2_mix[4]]
    for l in range(DEPTH):
        last = l + 1 == DEPTH
        wm = dict(zip(MIXER_W, _gather_finish(f"gather_mix_{l}", h2_mix, xa)))
        sp = dict(small[l])
        sp['g_mix'] = _behind(sp['g_mix'], tok_of(tokens))
        s = _mixer_fwd_in(xa, wm, sp, rope)
        tokens = []
        first_ffn = {}
        if l == 0:
            h1_first = h1_ffn
            mid = lambda dep: first_ffn.update(h=_gather_pass_on("gather_ffn_0", h1_first, dep))
        else:
            mid = None
        if not last:
            h1_mix = gather_start(l + 1, "mix", MIXER_W, s['y_mla'])
            h1_ffn = gather_start(l + 1, "ffn", FFN_W, h1_mix[4])
            tokens += [h1_mix[4], h1_ffn[4]]
        x1 = _mixer_fwd_out(s, wm, small[l], tok_of(tokens), mid)
        if l == 0:
            h2_ffn = first_ffn['h']
        tokens = []
        wf = dict(zip(FFN_W, _gather_finish(f"gather_ffn_{l}", h2_ffn, x1)))
        if not last:
            h2_mix = _gather_pass_on(f"gather_mix_{l + 1}", h1_mix, x1)
            tokens.append(h2_mix[4])
        xa = _ffn_fwd(x1, wf, small[l], s, tok_of(tokens))
        tokens = []
        if not last:
            h2_ffn = _gather_pass_on(f"gather_ffn_{l + 1}", h1_ffn, xa)
            tokens.append(h2_ffn[4])
        saved.append(s)
        full.append({**wm, **wf})
    gf = g_final.reshape(1, D)
    ones = jnp.ones((T, 1), F32)
    dxa, dgf, loss_rows = _rows_vjp("loss", _loss_fn, [_row(xa)], [gf], [_row(loss_target[0])], [_row(ones)],
                                    primal=True)
    loss_here = _sum_rows("loss_sum", loss_rows)[0, 0]

    bufs = {n: None for n in BIG}
    pending = []

    def advance(dep):
        tokens = []
        for g in pending:
            names, tag = g['names'], g['tag']
            if g['stage'] == 0:
                p4 = [a.reshape((4, 2) + a.shape[1:]) for a in g['gw']]
                gots = [lax.empty((4,) + a.shape[1:], F32) for a in g['gw']]
                g['h'] = _split_start("rs_sibling_start_" + tag, p4, gots, 4 * len(p4), _sibling_plan, dep)
                tokens.append(g['h'][4])
            elif g['stage'] == 1:
                p4, gots = _split_wait("rs_sibling_wait_" + tag, g['h'], 4 * len(names), _sibling_plan, dep)
                s4 = [_pair_sum("rs_pair_sum_" + n, p, q, core) for n, p, q in zip(names, p4, gots)]
                gots = [lax.empty((3,) + a.shape[1:], F32) for a in s4]
                g['h'] = _split_start("rs_chips_start_" + tag, s4, gots, 3 * len(s4), _chips_plan, dep)
                tokens.append(g['h'][4])
            elif g['stage'] == 3:
                s4, gots = _split_wait("rs_chips_wait_" + tag, g['h'], 3 * len(names), _chips_plan, dep)
                for n, s4n, got in zip(names, s4, gots):
                    bufs[n] = _adamw_shard("adamw_" + n, g['layer'], shard_view(W[n], n), shard_view(M[n], n),
                                           shard_view(V[n], n), s4n, got, chip, bufs[n])
            g['stage'] += 1
        pending[:] = [g for g in pending if g['stage'] < 4]
        return tokens

    def group(names, l, gw, kind):
        return dict(names=names, layer=l, gw=[gw[n] for n in names], stage=0, tag=f"{kind}_{l}")

    g_small = [None] * DEPTH
    tokens = []
    for l in reversed(range(DEPTH)):
        dx1, gw_f, gs_f = _ffn_bwd(dxa, saved[l], full[l], small[l], tok_of(tokens))
        pending.append(group(FFN_W, l, gw_f, "ffn"))
        tokens = advance(dx1)
        cts, gw_o, gs_o = _mixer_bwd_out(dx1, saved[l], full[l], small[l], tok_of(tokens))
        pending.append(group(OUT_W, l, gw_o, "out"))
        tokens = advance(cts[0])
        dxa, gw_i, gs_i = _mixer_bwd_in(cts, dx1, saved[l], full[l], small[l], rope, tok_of(tokens))
        pending.append(group(IN_W, l, gw_i, "in"))
        tokens = advance(dxa)
        g_small[l] = {**gs_f, **gs_o, **gs_i}

    flat = [g_small[l][n].reshape(-1) for l in range(DEPTH) for n in SMALL] + [dgf.reshape(-1), loss_here.reshape(1)]
    n_small = sum(int(f.shape[0]) for f in flat)
    rows = -(-n_small // (PACK_C * SMALL_ROW_ALIGN)) * SMALL_ROW_ALIGN
    flat = jnp.concatenate(flat + [jnp.zeros((rows * PACK_C - n_small,), F32)]).reshape(rows, PACK_C)
    h_small = _gather_start("gather_small_start", [flat], dxa)
    advance(h_small[4])
    h_small = _gather_pass_on("gather_small", h_small, flat)
    tokens = advance(h_small[4])
    (gathered,) = _gather_finish("gather_small", h_small, flat)
    tot = _behind(_sum_devices("small_sum", gathered).reshape(-1), tok_of(tokens))
    grads, off = {}, 0
    per_layer = {n: [] for n in SMALL}
    for l in range(DEPTH):
        for n, shp in SMALL.items():
            k = int(np.prod(shp))
            per_layer[n].append(tot[off:off + k].reshape(shp))
            off += k
    for n in SMALL:
        grads[n] = jnp.stack(per_layer[n])
    grads['g_final'] = tot[off:off + D]
    loss = tot[off + D]

    delta, new_m, new_v = {}, {}, {}
    for n in PARAMS:
        if n not in BIG:
            delta[n], new_m[n], new_v[n] = _adamw("adamw_" + n, W[n], grads[n], M[n], V[n])
    while pending:
        advance(delta['g_final'])
    for n in BIG:
        grads[n], delta[n], new_m[n], new_v[n] = [shard_view(b, n) for b in bufs[n]]
    return (loss, dxa[None], *[grads[n] for n in PARAMS], *[delta[n] for n in PARAMS],
            *[new_m[n] for n in PARAMS], *[new_v[n] for n in PARAMS])
```

```python
import jax
import jax.numpy as jnp
import numpy as np
from jax import lax
from jax.experimental import pallas as pl
from jax.experimental.pallas import tpu as pltpu

F32 = jnp.float32
BF16 = jnp.bfloat16

T = 2048
D = 2048
DEPTH = 4
N_DEV = 8
H_MLA, NOPE, ROPE, VDIM = 8, 128, 64, 128
QK = NOPE + ROPE
Q_LORA, KV_LORA = 512, 256
SSM_W, SSM_G, SSM_P, SSM_N = 512, 32, 16, 64
SSM_S = SSM_G * SSM_N
DIL_W, DIL_H, DIL_D = 512, 8, 64
BLK = 128
IN_SPLITS = (Q_LORA, KV_LORA, ROPE, SSM_W, DIL_W, DIL_W, DIL_W)
IN_W = sum(IN_SPLITS)
D_FF = 5632
EPS = 1e-6
ROPE_THETA = 10000.0
MLA_SCALE = QK ** -0.5
DIL_SCALE = DIL_D ** -0.5

ADAM_LR, ADAM_B1, ADAM_B2, ADAM_EPS, ADAM_WD, ADAM_STEP = 0.001, 0.9, 0.999, 1e-08, 0.01, 10

VMEM_LIMIT_V7X = 52 * 1024 * 1024
LANES = 128
PACK_C = 1024
ROW_BLOCK_BYTES = 2 * 1024 * 1024
MM_TM, MM_TN, MM_TK = 1408, 1024, 5632
MM_DEEP = IN_W
MM_TB = 512

NT = (((1,), (1,)), ((), ()))
TN = (((0,), (0,)), ((), ()))
H_QK = (((2,), (2,)), ((0,), (0,)))
H_PV = (((2,), (1,)), ((0,), (0,)))
H_TN = (((1,), (1,)), ((0,), (0,)))
HI = lax.Precision.HIGHEST
MESH = pl.DeviceIdType.MESH

PARAMS = ['g_mix', 'w_in', 'g_q', 'w_uq', 'g_kv', 'w_ukv', 'a_re', 'a_im', 'b_re', 'b_im', 'c_re', 'c_im',
          'd_skip', 'log_dt', 'w_glu', 'b_glu', 'g_out_mla', 'g_out_ssm', 'g_out_dil', 'w_o', 'g_ffn',
          'w_gate', 'w_up', 'w_down', 'g_final']
BIG = {'w_in': 't', 'w_uq': 't', 'w_ukv': 'c', 'w_glu': 'c', 'w_o': 'r', 'w_gate': 't', 'w_up': 't', 'w_down': 'r'}
MIXER_W = ['w_in', 'w_uq', 'w_ukv', 'w_glu', 'w_o']
FFN_W = ['w_gate', 'w_up', 'w_down']
OUT_W, IN_W = ['w_o', 'w_glu'], ['w_in', 'w_uq', 'w_ukv']
SMALL = {'g_mix': (D,), 'g_q': (Q_LORA,), 'g_kv': (KV_LORA,), 'a_re': (SSM_G, SSM_N), 'a_im': (SSM_G, SSM_N),
         'b_re': (SSM_G, SSM_N, SSM_P), 'b_im': (SSM_G, SSM_N, SSM_P), 'c_re': (SSM_G, SSM_P, SSM_N),
         'c_im': (SSM_G, SSM_P, SSM_N), 'd_skip': (SSM_G, SSM_P), 'log_dt': (SSM_G,), 'b_glu': (2 * SSM_W,),
         'g_out_mla': (H_MLA * VDIM,), 'g_out_ssm': (SSM_W,), 'g_out_dil': (DIL_W,), 'g_ffn': (D,)}
SMALL_ROW_ALIGN = 64


def _tile(dim, target, align=LANES):
    best = None
    for t in range(align, min(dim, target) + 1, align):
        if dim % t == 0:
            best = t
    return best if best is not None else dim


def _params(sem=None):
    return pltpu.CompilerParams(dimension_semantics=sem, vmem_limit_bytes=VMEM_LIMIT_V7X)


def _dot(a, b, dims=None, prec=None):
    if dims is None:
        return jnp.dot(a, b, preferred_element_type=F32, precision=prec)
    return lax.dot_general(a, b, dims, preferred_element_type=F32, precision=prec)


def _mm_spec(shape, blk, t_r, t_c, rc):
    if blk is None:
        return pl.BlockSpec((t_r, t_c), rc)
    _, R, C = shape
    if blk == 'r':
        per = R // t_r
        return pl.BlockSpec((1, t_r, t_c), lambda i, j, k: (rc(i, j, k)[0] // per, rc(i, j, k)[0] % per, rc(i, j, k)[1]))
    per = C // t_c
    return pl.BlockSpec((1, t_r, t_c), lambda i, j, k: (rc(i, j, k)[1] // per, rc(i, j, k)[0], rc(i, j, k)[1] % per))


def _logical(shape, blk):
    if blk is None:
        return tuple(shape)
    G, R, C = shape
    return (G * R, C) if blk == 'r' else (R, G * C)


def _mm(name, a, b, mode, ab=None, bb=None, ob=None, res=None, prec=None):
    la, lb = _logical(a.shape, ab), _logical(b.shape, bb)
    am, ak = (0, 1) if mode != 'tn' else (1, 0)
    bk, bn = (0, 1) if mode != 'nt' else (1, 0)
    M, K, N = la[am], la[ak], lb[bn]
    assert lb[bk] == K, (name, a.shape, b.shape, mode)
    if ob is None:
        out_shape = (M, N)
    elif ob == 'r':
        G = N_DEV
        out_shape = (G, M // G, N)
    else:
        G = N_DEV
        out_shape = (G, M, N // G)
    em = min(a.shape[-2:][am], out_shape[-2])
    en = min(b.shape[-2:][bn], out_shape[-1])
    ek = min(a.shape[-2:][ak], b.shape[-2:][bk])
    dims = {'nn': None, 'nt': NT, 'tn': TN}[mode]
    a_kb = mode != 'tn' and ab == 'c'
    b_kb = (mode == 'nn' and bb == 'r') or (mode == 'nt' and bb == 'c')
    blocks = K // ek if (a_kb or b_kb) else 1
    assert blocks == 1 or ((a_kb or ab is None) and (b_kb or bb is None)), (name, ab, bb, mode)
    tk = ek if blocks > 1 else _tile(ek, MM_TK)
    nk = 1 if blocks > 1 else K // tk
    small = blocks > 1 or nk > 1 or K > MM_DEEP
    tn = _tile(en, MM_TB if small else MM_TN)
    tm = _tile(em, MM_TB if small else (2 * MM_TM if K <= MM_TB else MM_TM))
    if tn > MM_TN:
        tm = _tile(em, MM_TB)
    if tm > 2 * MM_TM:
        tn = _tile(en, MM_TB)

    def val(ref):
        return ref[...] if len(ref.shape) == 2 else ref[0]

    def put(o_ref, r):
        if len(o_ref.shape) == 2:
            o_ref[...] = r
        else:
            o_ref[0] = r

    def k_block(ref, d, blocked, lanes):
        if blocked:
            return ref[d]
        return ref[:, d * ek:(d + 1) * ek] if lanes else ref[d * ek:(d + 1) * ek, :]

    def body(*refs):
        if res is None:
            a_ref, b_ref, o_ref = refs[:3]
            r_ref = None
        else:
            a_ref, b_ref, r_ref, o_ref = refs[:4]
        if blocks > 1:
            part = None
            for d in range(blocks):
                p = _dot(k_block(a_ref, d, a_kb, True), k_block(b_ref, d, b_kb, mode == 'nt'), dims, prec)
                part = p if part is None else part + p
        else:
            part = _dot(val(a_ref), val(b_ref), dims, prec)
        if nk == 1:
            put(o_ref, part if r_ref is None else part + val(r_ref))
            return
        acc_ref = refs[-1]
        k = pl.program_id(2)

        @pl.when(k == 0)
        def _():
            acc_ref[...] = part

        @pl.when((k > 0) & (k < nk - 1))
        def _():
            acc_ref[...] += part

        @pl.when(k == nk - 1)
        def _():
            r = acc_ref[...] + part
            put(o_ref, r if r_ref is None else r + val(r_ref))

    if blocks > 1:
        G = blocks
        a_spec = (pl.BlockSpec((G, tm, ek), lambda i, j, k: (0, i, 0)) if a_kb
                  else pl.BlockSpec((tm, K), lambda i, j, k: (i, 0)))
        if b_kb:
            b_spec = (pl.BlockSpec((G, ek, tn), lambda i, j, k: (0, 0, j)) if mode == 'nn'
                      else pl.BlockSpec((G, tn, ek), lambda i, j, k: (0, j, 0)))
        else:
            b_spec = (pl.BlockSpec((K, tn), lambda i, j, k: (0, j)) if mode == 'nn'
                      else pl.BlockSpec((tn, K), lambda i, j, k: (j, 0)))
    else:
        if mode == 'tn':
            a_spec = _mm_spec(a.shape, ab, tk, tm, lambda i, j, k: (k, i))
        else:
            a_spec = _mm_spec(a.shape, ab, tm, tk, lambda i, j, k: (i, k))
        if mode == 'nt':
            b_spec = _mm_spec(b.shape, bb, tn, tk, lambda i, j, k: (j, k))
        else:
            b_spec = _mm_spec(b.shape, bb, tk, tn, lambda i, j, k: (k, j))
    o_spec = _mm_spec(out_shape, ob, tm, tn, lambda i, j, k: (i, j))
    in_specs = [a_spec, b_spec] + ([o_spec] if res is not None else [])
    args = (a, b) + ((res,) if res is not None else ())
    return pl.pallas_call(
        body, name=name, out_shape=jax.ShapeDtypeStruct(out_shape, F32),
        grid=(M // tm, N // tn, nk), in_specs=in_specs, out_specs=o_spec,
        scratch_shapes=[pltpu.VMEM((tm, tn), F32)] if nk > 1 else [],
        compiler_params=_params(("parallel", "parallel", "arbitrary")),
    )(*args)


def _row(a, tr=None, axis=-2):
    axis = axis % a.ndim
    n = a.shape[axis]
    if tr is None:
        row_bytes = a.size // n * 4
        tr = _tile(n, max(8, min(256, ROW_BLOCK_BYTES // row_bytes)), 8)
    return (a, axis, tr)


def _row_spec(shape, axis, tr):
    nd = len(shape)
    blk = tuple(tr if d == axis else s for d, s in enumerate(shape))
    return pl.BlockSpec(blk, lambda i: tuple(i if d == axis else 0 for d in range(nd)))


def _full_spec(shape):
    nd = len(shape)
    return pl.BlockSpec(tuple(shape), lambda i: (0,) * nd)


def _steps(entries):
    ns = {a.shape[ax] // tr for a, ax, tr in entries}
    assert len(ns) == 1, [(a.shape, ax, tr) for a, ax, tr in entries]
    return ns.pop()


def _as_tuple(r):
    return tuple(r) if isinstance(r, (tuple, list)) else (r,)


def _rows_fwd(name, fn, rows, bcast, outs):
    steps = _steps(rows)
    nr, nb = len(rows), len(bcast)

    def body(*refs):
        vals = [r[...] for r in refs[:nr + nb]]
        res = _as_tuple(fn(*vals))
        for o_ref, r in zip(refs[nr + nb:], res):
            o_ref[...] = r.astype(o_ref.dtype)

    in_specs = [_row_spec(a.shape, ax, tr) for a, ax, tr in rows] + [_full_spec(b.shape) for b in bcast]
    out_specs = [_row_spec(s, ax % len(s), tr) for s, _, ax, tr in outs]
    res = pl.pallas_call(
        body, name=name, out_shape=[jax.ShapeDtypeStruct(s, dt) for s, dt, _, _ in outs],
        grid=(steps,), in_specs=in_specs, out_specs=out_specs,
        compiler_params=_params(("parallel",)),
    )(*[a for a, _, _ in rows], *bcast)
    return res


def _rows_vjp(name, fn, drows, dbc, arows, cts, primal=False, grad_dtypes=None):
    entries = list(drows) + list(arows) + list(cts)
    steps = _steps(entries)
    ndr, ndb, nar, nct = len(drows), len(dbc), len(arows), len(cts)
    gdt = list(grad_dtypes) if grad_dtypes is not None else [F32] * ndr

    def body(*refs):
        p = 0
        dr = [r[...] for r in refs[p:p + ndr]]; p += ndr
        db = [r[...] for r in refs[p:p + ndb]]; p += ndb
        ar = [r[...] for r in refs[p:p + nar]]; p += nar
        ct = [r[...] for r in refs[p:p + nct]]; p += nct
        g_rows = refs[p:p + ndr]; p += ndr
        g_bc = refs[p:p + ndb]; p += ndb
        prim_refs = refs[p:]

        def f(*d):
            return _as_tuple(fn(*d, *ar))

        outs, pullback = jax.vjp(f, *dr, *db)
        grads = pullback(tuple(c.astype(o.dtype) for c, o in zip(ct, outs)))
        for k in range(ndr):
            g_rows[k][...] = grads[k].astype(g_rows[k].dtype)
        if ndb:
            @pl.when(pl.program_id(0) == 0)
            def _():
                for r in g_bc:
                    r[...] = jnp.zeros_like(r)
            for k in range(ndb):
                g_bc[k][...] += grads[ndr + k]
        for r, o in zip(prim_refs, outs):
            r[...] = o.astype(r.dtype)

    in_specs = ([_row_spec(a.shape, ax, tr) for a, ax, tr in drows] + [_full_spec(b.shape) for b in dbc]
                + [_row_spec(a.shape, ax, tr) for a, ax, tr in arows]
                + [_row_spec(a.shape, ax, tr) for a, ax, tr in cts])
    out_shape = ([jax.ShapeDtypeStruct(a.shape, dt) for (a, _, _), dt in zip(drows, gdt)]
                 + [jax.ShapeDtypeStruct(b.shape, F32) for b in dbc])
    out_specs = ([_row_spec(a.shape, ax, tr) for a, ax, tr in drows] + [_full_spec(b.shape) for b in dbc])
    if primal:
        out_shape += [jax.ShapeDtypeStruct(a.shape, F32) for a, _, _ in cts]
        out_specs += [_row_spec(a.shape, ax, tr) for a, ax, tr in cts]
    return pl.pallas_call(
        body, name=name, out_shape=out_shape, grid=(steps,), in_specs=in_specs, out_specs=out_specs,
        compiler_params=_params(("arbitrary",)),
    )(*[a for a, _, _ in drows], *dbc, *[a for a, _, _ in arows], *[a for a, _, _ in cts])


def _rms_fn(x, g):
    return x * lax.rsqrt(jnp.mean(x * x, axis=-1, keepdims=True) + EPS) * g


def _rms_res_fn(x, g):
    return _rms_fn(x, g), x


def _s5_act_fn(ymm, u, d):
    return jax.nn.gelu(ymm + d * u)


def _glu_fn(z1, z2, b1, b2):
    return (z1 + b1) * jax.nn.sigmoid(z2 + b2)


def _outnorm_fn(ym, ys, yd, gm, gs, gd):
    return jnp.concatenate([_rms_fn(ym, gm), _rms_fn(ys, gs), _rms_fn(yd, gd)], axis=-1)


def _swiglu_fn(a, b):
    return jax.nn.silu(a) * b


def _loss_fn(x, g, tgt):
    err = _rms_fn(x, g) - tgt
    return 0.5 * jnp.mean(err * err, axis=-1, keepdims=True)


def _dil_mix_fn(o0, o1, o2, l0, l1, l2):
    m = jnp.maximum(jnp.maximum(l0, l1), l2)
    e0, e1, e2 = jnp.exp(l0 - m), jnp.exp(l1 - m), jnp.exp(l2 - m)
    s = e0 + e1 + e2
    return (e0 / s) * o0 + (e1 / s) * o1 + (e2 / s) * o2


def _s5_disc_fn(a_re, a_im, ldt, b_r, b_i):
    lr = jnp.minimum(a_re.reshape(1, SSM_N), -1e-4)
    li = a_im.reshape(1, SSM_N)
    dt = jnp.exp(ldt.reshape(1, 1))
    e = jnp.exp(lr * dt)
    ar = e * jnp.cos(li * dt)
    ai = e * jnp.sin(li * dt)
    nr, ni = ar - 1.0, ai
    den = lr * lr + li * li
    cr = (nr * lr + ni * li) / den
    ci = (ni * lr - nr * li) / den
    return ar.reshape(1, 1, SSM_N), ai.reshape(1, 1, SSM_N), cr * b_r - ci * b_i, cr * b_i + ci * b_r


def _adamw_fn(w, g, m, v):
    m = ADAM_B1 * m + (1.0 - ADAM_B1) * g
    v = ADAM_B2 * v + (1.0 - ADAM_B2) * jnp.square(g)
    m_hat = m / (1.0 - ADAM_B1 ** ADAM_STEP)
    v_hat = v / (1.0 - ADAM_B2 ** ADAM_STEP)
    delta = -ADAM_LR * (m_hat / (jnp.sqrt(v_hat) + ADAM_EPS) + ADAM_WD * w)
    return delta, m, v


def _rms_fwd(name, x, g):
    (h,) = _rows_fwd(name, _rms_fn, [_row(x)], [g], [(x.shape, BF16, -2, _row(x)[2])])
    return h


def _rms_bwd(name, x, g, dh, dres=None):
    if dres is None:
        dx, dg = _rows_vjp(name, _rms_fn, [_row(x)], [g], [], [_row(dh)])
    else:
        dx, dg = _rows_vjp(name, _rms_res_fn, [_row(x)], [g], [], [_row(dh), _row(dres)])
    return dx, dg


MLA_TQ = 256
MLA_EXT = 512


def _mla_fwd(q, k, v):
    tq = MLA_TQ

    def body(q_ref, k_ref, v_ref, o_ref, lse_ref):
        i = pl.program_id(1)
        q = q_ref[0]

        def rows_below(ext):
            s = _dot(q, k_ref[0, :ext, :], NT) * MLA_SCALE
            row = i * tq + lax.broadcasted_iota(jnp.int32, (tq, ext), 0)
            col = lax.broadcasted_iota(jnp.int32, (tq, ext), 1)
            s = jnp.where(row >= col, s, -jnp.inf)
            m = jnp.max(s, axis=-1, keepdims=True)
            p = jnp.exp(s - m)
            l = jnp.sum(p, axis=-1, keepdims=True)
            o_ref[...] = _dot((p / l).astype(BF16), v_ref[0, :ext, :])
            lse_ref[0] = m + jnp.log(l)

        for g in range(T // MLA_EXT):
            pl.when(i // (MLA_EXT // tq) == g)(lambda g=g: rows_below((g + 1) * MLA_EXT))

    return pl.pallas_call(
        body, name="mla_fwd",
        out_shape=[jax.ShapeDtypeStruct((T, H_MLA * VDIM), F32), jax.ShapeDtypeStruct((H_MLA, T, 1), F32)],
        grid=(H_MLA, T // tq),
        in_specs=[pl.BlockSpec((1, tq, QK), lambda h, i: (h, i, 0)),
                  pl.BlockSpec((1, T, QK), lambda h, i: (h, 0, 0)),
                  pl.BlockSpec((1, T, VDIM), lambda h, i: (h, 0, 0))],
        out_specs=[pl.BlockSpec((tq, VDIM), lambda h, i: (i, h)),
                   pl.BlockSpec((1, tq, 1), lambda h, i: (h, i, 0))],
        compiler_params=_params(("parallel", "parallel")),
    )(q, k, v)


def _mla_bwd(q, k, v, o, do, lse):
    tq = MLA_TQ

    def body(q_ref, k_ref, v_ref, o_ref, do_ref, lse_ref, dq_ref, dk_ref, dv_ref):
        i = pl.program_id(1)

        @pl.when(i == 0)
        def _():
            dk_ref[...] = jnp.zeros_like(dk_ref)
            dv_ref[...] = jnp.zeros_like(dv_ref)

        q, lse = q_ref[0], lse_ref[0]
        delta = jnp.sum(do_ref[...] * o_ref[...], axis=-1, keepdims=True)
        do = do_ref[...].astype(BF16)

        def rows_below(ext):
            k, v = k_ref[0, :ext, :], v_ref[0, :ext, :]
            s = _dot(q, k, NT) * MLA_SCALE
            row = i * tq + lax.broadcasted_iota(jnp.int32, (tq, ext), 0)
            col = lax.broadcasted_iota(jnp.int32, (tq, ext), 1)
            p = jnp.where(row >= col, jnp.exp(s - lse), 0.0)
            ds = (p * (_dot(do, v, NT) - delta) * MLA_SCALE).astype(BF16)
            dq_ref[0] = _dot(ds, k)
            dk_ref[0, :ext, :] += _dot(ds, q, TN)
            dv_ref[0, :ext, :] += _dot(p.astype(BF16), do, TN)

        for g in range(T // MLA_EXT):
            pl.when(i // (MLA_EXT // tq) == g)(lambda g=g: rows_below((g + 1) * MLA_EXT))

    return pl.pallas_call(
        body, name="mla_bwd",
        out_shape=[jax.ShapeDtypeStruct((H_MLA, T, QK), F32), jax.ShapeDtypeStruct((H_MLA, T, QK), F32),
                   jax.ShapeDtypeStruct((H_MLA, T, VDIM), F32)],
        grid=(H_MLA, T // tq),
        in_specs=[pl.BlockSpec((1, tq, QK), lambda h, i: (h, i, 0)),
                  pl.BlockSpec((1, T, QK), lambda h, i: (h, 0, 0)),
                  pl.BlockSpec((1, T, VDIM), lambda h, i: (h, 0, 0)),
                  pl.BlockSpec((tq, VDIM), lambda h, i: (i, h)),
                  pl.BlockSpec((tq, VDIM), lambda h, i: (i, h)),
                  pl.BlockSpec((1, tq, 1), lambda h, i: (h, i, 0))],
        out_specs=[pl.BlockSpec((1, tq, QK), lambda h, i: (h, i, 0)),
                   pl.BlockSpec((1, T, QK), lambda h, i: (h, 0, 0)),
                   pl.BlockSpec((1, T, VDIM), lambda h, i: (h, 0, 0))],
        compiler_params=_params(("parallel", "arbitrary")),
    )(q, k, v, o, do, lse)


NBLK = T // BLK


BAND_GL = 256
BAND_GH = BAND_GL // DIL_D
BAND_ROWS = BAND_GH * BLK
BAND_GROUPS = DIL_W // BAND_GL


def _band_masks():
    r = lax.broadcasted_iota(jnp.int32, (BAND_ROWS, BLK), 0) & (BLK - 1)
    j = lax.broadcasted_iota(jnp.int32, (BAND_ROWS, BLK), 1)
    return j <= r, j >= r


def _head_lanes():
    lane_head = lax.broadcasted_iota(jnp.int32, (1, BAND_GL), 1) // DIL_D
    return [lane_head == h for h in range(BAND_GH)]


def _stack_heads(x, lanes):
    return jnp.concatenate([jnp.where(m, x, jnp.zeros_like(x)) for m in lanes], axis=0)


def _merge_heads(xs, lanes):
    out = None
    for h, m in enumerate(lanes):
        part = jnp.where(m, xs[h * BLK:(h + 1) * BLK], 0.0)
        out = part if out is None else out + part
    return out


def _per_head(x, lanes):
    return jnp.concatenate([jnp.sum(jnp.where(m, x, 0.0), axis=-1, keepdims=True) for m in lanes], axis=0)


def _lane_group(ref, g):
    return ref[0, :, g * BAND_GL:(g + 1) * BAND_GL]


def _seq_start(p, i):
    per_seq = lax.shift_right_logical(jnp.int32(NBLK), 2 * p)
    return lax.rem(i, per_seq) == 0


def _band_fwd(q, k, v):
    def body(q_ref, kp_ref, kc_ref, vp_ref, vc_ref, o_ref, lse_ref):
        p, i = pl.program_id(0), pl.program_id(1)
        has_prev = jnp.logical_not(_seq_start(p, i))
        m_cur, m_prev = _band_masks()
        m_prev = m_prev & has_prev
        lanes = _head_lanes()
        for g in range(BAND_GROUPS):
            qs = _stack_heads(_lane_group(q_ref, g), lanes)
            s_c = jnp.where(m_cur, _dot(qs, _lane_group(kc_ref, g), NT) * DIL_SCALE, -jnp.inf)
            s_p = jnp.where(m_prev, _dot(qs, _lane_group(kp_ref, g), NT) * DIL_SCALE, -jnp.inf)
            m = jnp.maximum(jnp.max(s_c, axis=-1, keepdims=True), jnp.max(s_p, axis=-1, keepdims=True))
            e_c, e_p = jnp.exp(s_c - m), jnp.exp(s_p - m)
            l = jnp.sum(e_c, axis=-1, keepdims=True) + jnp.sum(e_p, axis=-1, keepdims=True)
            os = (_dot((e_p / l).astype(BF16), _lane_group(vp_ref, g))
                  + _dot((e_c / l).astype(BF16), _lane_group(vc_ref, g)))
            cols = slice(g * BAND_GL, (g + 1) * BAND_GL)
            o_ref[0, :, cols] = _merge_heads(os, lanes)
            lse_ref[0, :, cols] = _merge_heads(m + jnp.log(l), lanes)

    blk = (1, BLK, DIL_W)
    cur = lambda p, i: (p, i, 0)
    prev = lambda p, i: (p, jnp.maximum(i - 1, 0), 0)
    return pl.pallas_call(
        body, name="band_fwd",
        out_shape=[jax.ShapeDtypeStruct((3, T, DIL_W), F32)] * 2,
        grid=(3, NBLK),
        in_specs=[pl.BlockSpec(blk, cur), pl.BlockSpec(blk, prev), pl.BlockSpec(blk, cur),
                  pl.BlockSpec(blk, prev), pl.BlockSpec(blk, cur)],
        out_specs=[pl.BlockSpec(blk, cur), pl.BlockSpec(blk, cur)],
        compiler_params=_params(("parallel", "parallel")),
    )(q, k, k, v, v)


def _band_bwd(q, k, v, o, lse, do, dlse):
    def body(qc_ref, qn_ref, kp_ref, kc_ref, vp_ref, vc_ref, oc_ref, on_ref, lc_ref, ln_ref,
             doc_ref, don_ref, dlc_ref, dln_ref, dq_ref, dk_ref, dv_ref):
        p, i = pl.program_id(0), pl.program_id(1)
        has_prev = jnp.logical_not(_seq_start(p, i))
        has_next = jnp.logical_not(_seq_start(p, i + 1)) & (i + 1 < NBLK)
        m_cur, m_prev = _band_masks()
        lanes = _head_lanes()

        def probs(qs, k, lse, mask):
            return jnp.where(mask, jnp.exp(_dot(qs, k, NT) * DIL_SCALE - lse), 0.0)

        def dscore(pr, dos, v, shift):
            return (pr * (_dot(dos, v, NT) + shift) * DIL_SCALE).astype(BF16)

        for g in range(BAND_GROUPS):
            grp = lambda ref: _lane_group(ref, g)
            kp, kc, vp, vc = grp(kp_ref), grp(kc_ref), grp(vp_ref), grp(vc_ref)
            qc, qn = _stack_heads(grp(qc_ref), lanes), _stack_heads(grp(qn_ref), lanes)
            doc, don = grp(doc_ref), grp(don_ref)
            lse_c = _per_head(grp(lc_ref), lanes) * (1.0 / DIL_D)
            lse_n = _per_head(grp(ln_ref), lanes) * (1.0 / DIL_D)
            sh_c = _per_head(grp(dlc_ref) - doc * grp(oc_ref), lanes)
            sh_n = _per_head(grp(dln_ref) - don * grp(on_ref), lanes)
            doc, don = _stack_heads(doc.astype(BF16), lanes), _stack_heads(don.astype(BF16), lanes)
            p_cc = probs(qc, kc, lse_c, m_cur)
            p_cp = probs(qc, kp, lse_c, m_prev & has_prev)
            p_nc = probs(qn, kc, lse_n, m_prev & has_next)
            ds_cc = dscore(p_cc, doc, vc, sh_c)
            ds_cp = dscore(p_cp, doc, vp, sh_c)
            ds_nc = dscore(p_nc, don, vc, sh_n)
            cols = slice(g * BAND_GL, (g + 1) * BAND_GL)
            dq_ref[0, :, cols] = _merge_heads(_dot(ds_cc, kc) + _dot(ds_cp, kp), lanes)
            dk_ref[0, :, cols] = _dot(ds_cc, qc, TN) + _dot(ds_nc, qn, TN)
            dv_ref[0, :, cols] = _dot(p_cc.astype(BF16), doc, TN) + _dot(p_nc.astype(BF16), don, TN)

    blk = (1, BLK, DIL_W)
    cur = lambda p, i: (p, i, 0)
    prev = lambda p, i: (p, jnp.maximum(i - 1, 0), 0)
    nxt = lambda p, i: (p, jnp.minimum(i + 1, NBLK - 1), 0)
    w, wn, wp = pl.BlockSpec(blk, cur), pl.BlockSpec(blk, nxt), pl.BlockSpec(blk, prev)
    return pl.pallas_call(
        body, name="band_bwd",
        out_shape=[jax.ShapeDtypeStruct((3, T, DIL_W), F32)] * 3,
        grid=(3, NBLK),
        in_specs=[w, wn, wp, w, wp, w, w, wn, w, wn, w, wn, w, wn],
        out_specs=[w, w, w],
        compiler_params=_params(("parallel", "parallel")),
    )(q, q, k, k, v, v, o, o, lse, lse, do, do, dlse, dlse)


SCAN_TC = 256


def _scan_fwd(bu, ar, ai):
    tc, S = SCAN_TC, SSM_S

    def body(bu_ref, ar_ref, ai_ref, h_ref, h16_ref, cr_ref, ci_ref):
        @pl.when(pl.program_id(0) == 0)
        def _():
            cr_ref[...] = jnp.zeros_like(cr_ref)
            ci_ref[...] = jnp.zeros_like(ci_ref)

        a_r, a_i = ar_ref[...], ai_ref[...]

        def step(j, carry):
            hr, hi = carry
            for r in range(8):
                t = pl.multiple_of(j * 8, 8) + r
                br = bu_ref[pl.ds(t, 1), pl.ds(0, S)]
                bi = bu_ref[pl.ds(t, 1), pl.ds(S, S)]
                hr, hi = a_r * hr - a_i * hi + br, a_r * hi + a_i * hr + bi
                h_ref[pl.ds(t, 1), pl.ds(0, S)] = hr
                h_ref[pl.ds(t, 1), pl.ds(S, S)] = hi
            return hr, hi

        hr, hi = lax.fori_loop(0, tc // 8, step, (cr_ref[...], ci_ref[...]))
        cr_ref[...] = hr
        ci_ref[...] = hi
        h16_ref[...] = h_ref[...].astype(BF16)

    return pl.pallas_call(
        body, name="s5_scan_fwd",
        out_shape=[jax.ShapeDtypeStruct((T, 2 * S), F32), jax.ShapeDtypeStruct((T, 2 * S), BF16)],
        grid=(T // tc,),
        in_specs=[pl.BlockSpec((tc, 2 * S), lambda i: (i, 0)), _full_spec((1, S)), _full_spec((1, S))],
        out_specs=[pl.BlockSpec((tc, 2 * S), lambda i: (i, 0)), pl.BlockSpec((tc, 2 * S), lambda i: (i, 0))],
        scratch_shapes=[pltpu.VMEM((1, S), F32), pltpu.VMEM((1, S), F32)],
        compiler_params=_params(("arbitrary",)),
    )(bu, ar, ai)


def _scan_bwd(dh, h, ar, ai):
    tc, S = SCAN_TC, SSM_S
    nc = T // tc

    def body(dh_ref, h_ref, hp_ref, ar_ref, ai_ref, g16_ref, dar_ref, dai_ref, cr_ref, ci_ref, g_ref):
        i = pl.program_id(0)

        @pl.when(i == 0)
        def _():
            cr_ref[...] = jnp.zeros_like(cr_ref)
            ci_ref[...] = jnp.zeros_like(ci_ref)
            dar_ref[...] = jnp.zeros_like(dar_ref)
            dai_ref[...] = jnp.zeros_like(dai_ref)

        a_r, a_i = ar_ref[...], ai_ref[...]
        first_chunk = (i == nc - 1)
        edge = jnp.where(first_chunk, 0.0, 1.0)
        hpr = hp_ref[pl.ds(7, 1), pl.ds(0, S)] * edge
        hpi = hp_ref[pl.ds(7, 1), pl.ds(S, S)] * edge

        def step(jj, carry):
            gr, gi, dar, dai = carry
            j = tc // 8 - 1 - jj
            for r in range(7, -1, -1):
                t = pl.multiple_of(j * 8, 8) + r
                tp = jnp.maximum(t - 1, 0)
                inside = t > 0
                pr = jnp.where(inside, h_ref[pl.ds(tp, 1), pl.ds(0, S)], hpr)
                pi = jnp.where(inside, h_ref[pl.ds(tp, 1), pl.ds(S, S)], hpi)
                gr, gi = (dh_ref[pl.ds(t, 1), pl.ds(0, S)] + a_r * gr + a_i * gi,
                          dh_ref[pl.ds(t, 1), pl.ds(S, S)] + a_r * gi - a_i * gr)
                g_ref[pl.ds(t, 1), pl.ds(0, S)] = gr
                g_ref[pl.ds(t, 1), pl.ds(S, S)] = gi
                dar = dar + gr * pr + gi * pi
                dai = dai + gi * pr - gr * pi
            return gr, gi, dar, dai

        zero = jnp.zeros((1, S), F32)
        gr, gi, dar, dai = lax.fori_loop(0, tc // 8, step, (cr_ref[...], ci_ref[...], zero, zero))
        cr_ref[...] = gr
        ci_ref[...] = gi
        dar_ref[...] += dar
        dai_ref[...] += dai
        g16_ref[...] = g_ref[...].astype(BF16)

    rev = lambda i: (nc - 1 - i, 0)
    before = lambda i: (jnp.maximum((nc - 1 - i) * (tc // 8) - 1, 0), 0)
    return pl.pallas_call(
        body, name="s5_scan_bwd",
        out_shape=[jax.ShapeDtypeStruct((T, 2 * S), BF16), jax.ShapeDtypeStruct((1, S), F32),
                   jax.ShapeDtypeStruct((1, S), F32)],
        grid=(nc,),
        in_specs=[pl.BlockSpec((tc, 2 * S), rev), pl.BlockSpec((tc, 2 * S), rev), pl.BlockSpec((8, 2 * S), before),
                  _full_spec((1, S)), _full_spec((1, S))],
        out_specs=[pl.BlockSpec((tc, 2 * S), rev), _full_spec((1, S)), _full_spec((1, S))],
        scratch_shapes=[pltpu.VMEM((1, S), F32), pltpu.VMEM((1, S), F32), pltpu.VMEM((tc, 2 * S), F32)],
        compiler_params=_params(("arbitrary",)),
    )(dh, h, h, ar, ai)


def _sum_rows(name, x):
    def body(x_ref, o_ref):
        o_ref[...] = jnp.sum(x_ref[...], axis=0, keepdims=True)

    return pl.pallas_call(body, name=name, out_shape=jax.ShapeDtypeStruct((1, 1), F32),
                          in_specs=[_full_spec(x.shape)], out_specs=_full_spec((1, 1)), grid=(1,))(x)


ANY = pl.BlockSpec(memory_space=pl.ANY)


def _place():
    return lax.axis_index("x"), lax.axis_index("y"), lax.axis_index("c")


HBM = pl.BlockSpec(memory_space=pltpu.HBM)
SEM = pl.BlockSpec(memory_space=pltpu.SEMAPHORE)
DATAFLOW = pltpu.SideEffectType.DATAFLOW_SIDE_EFFECTING


def _hbm(a):
    return pltpu.with_memory_space_constraint(a, pltpu.HBM)


def _split_start(name, srcs, lands, ncopies, plan, after):
    ns, nl = len(srcs), len(lands)

    def body(*refs):
        send_sems, recv_sems = refs[ns + nl + 1], refs[ns + nl + 2]
        token = refs[-1]
        for k, (src, dst, peer, _) in enumerate(plan(refs[:ns], refs[ns:ns + nl])):
            pltpu.make_async_remote_copy(src_ref=src, dst_ref=dst, send_sem=send_sems.at[k], recv_sem=recv_sems.at[k],
                                         device_id=peer, device_id_type=MESH).start()
        token[...] = jnp.zeros_like(token)

    out = pl.pallas_call(
        body, name=name,
        out_shape=(pltpu.SemaphoreType.DMA((ncopies,)), pltpu.SemaphoreType.DMA((ncopies,)),
                   *[pltpu.HBM(a.shape, a.dtype) for a in srcs], *[pltpu.HBM(a.shape, a.dtype) for a in lands],
                   jax.ShapeDtypeStruct((8, LANES), F32)),
        in_specs=[HBM] * (ns + nl) + [ANY],
        out_specs=(SEM, SEM, *[HBM] * (ns + nl), pl.BlockSpec(memory_space=pltpu.VMEM)),
        input_output_aliases={i: 2 + i for i in range(ns + nl)},
        compiler_params=pltpu.CompilerParams(has_side_effects=DATAFLOW),
    )(*[_hbm(a) for a in srcs], *[_hbm(a) for a in lands], after)
    return out[0], out[1], list(out[2:2 + ns]), list(out[2 + ns:2 + ns + nl]), out[-1]


def _split_wait(name, handle, ncopies, plan, after):
    send_sems, recv_sems, srcs, lands, _ = handle
    ns, nl = len(srcs), len(lands)

    def body(*refs):
        s_sems, r_sems = refs[ns + nl], refs[ns + nl + 1]
        for k, (src, dst, peer, mine) in enumerate(plan(refs[:ns], refs[ns:ns + nl])):
            pltpu.make_async_remote_copy(src_ref=src, dst_ref=dst, send_sem=s_sems.at[k], recv_sem=r_sems.at[k],
                                         device_id=peer, device_id_type=MESH).wait_send()
            pltpu.make_async_remote_copy(src_ref=src, dst_ref=mine, send_sem=s_sems.at[k], recv_sem=r_sems.at[k],
                                         device_id=peer, device_id_type=MESH).wait_recv()

    out = pl.pallas_call(
        body, name=name,
        out_shape=(*[pltpu.HBM(a.shape, a.dtype) for a in srcs], *[pltpu.HBM(a.shape, a.dtype) for a in lands]),
        in_specs=[HBM] * (ns + nl) + [SEM, SEM, ANY],
        out_specs=tuple([HBM] * (ns + nl)),
        input_output_aliases={i: i for i in range(ns + nl)},
        compiler_params=pltpu.CompilerParams(has_side_effects=DATAFLOW),
    )(*srcs, *lands, send_sems, recv_sems, after)
    return list(out[:ns]), list(out[ns:])


def _slot(px, py, pc):
    return 4 * px + 2 * py + pc


def _gather_plan(xs, lands):
    x, y, c = _place()
    peers = [(x, y, 1 - c), (1 - x, y, c), (x, 1 - y, c), (1 - x, 1 - y, c)]
    return [(xs[t], lands[t].at[_slot(x, y, c)], peer, lands[t].at[_slot(*peer)])
            for t in range(len(xs)) for peer in peers]


def _gather_start(name, shards, after):
    lands = [lax.empty((N_DEV,) + s.shape, s.dtype) for s in shards]
    return _split_start(name, shards, lands, 4 * len(shards), _gather_plan, after)


def _pass_on_plan(_, lands):
    x, y, c = _place()
    blocks = [((1 - x, y, c), (1 - x, y, 1 - c)), ((x, 1 - y, c), (x, 1 - y, 1 - c)),
              ((1 - x, 1 - y, c), (1 - x, 1 - y, 1 - c)), ((x, y, 1 - c), (x, y, c))]
    return [(lands[t].at[_slot(*out)], lands[t].at[_slot(*out)], (x, y, 1 - c), lands[t].at[_slot(*back)])
            for t in range(len(lands)) for out, back in blocks]


def _gather_pass_on(name, handle, after):
    n = len(handle[2])
    _, lands = _split_wait(name + "_wait", handle, 4 * n, _gather_plan, after)
    return _split_start(name + "_pass_on", [], lands, 4 * n, _pass_on_plan, after)


def _gather_finish(name, handle, after):
    n = len(handle[3])
    _, lands = _split_wait(name + "_done", handle, 4 * n, _pass_on_plan, after)
    return lands


def _sibling_plan(ps, gots):
    x, y, c = _place()
    return [(ps[t].at[j, 1 - c], gots[t].at[j], (x, y, 1 - c), gots[t].at[j]) for t in range(len(ps)) for j in range(4)]


def _chips_plan(ss, gots):
    x, y, c = _place()
    chips = [(1 - x, y), (x, 1 - y), (1 - x, 1 - y)]
    return [(ss[t].at[2 * px + py], gots[t].at[k], (px, py, c), gots[t].at[k])
            for t in range(len(ss)) for k, (px, py) in enumerate(chips)]


def _sum_tile(R, C):
    return _tile(R, max(8, min(1024, ROW_BLOCK_BYTES // (C * 4))), 8)


def _pair_sum(name, p4, got, core):
    _, _, R, C = p4.shape
    tr = _sum_tile(R, C)

    def body(core_ref, p_ref, g_ref, o_ref):
        o_ref[...] = p_ref[:, 0] + g_ref[...]

    return pl.pallas_call(
        body, name=name, out_shape=jax.ShapeDtypeStruct((4, R, C), F32),
        grid_spec=pltpu.PrefetchScalarGridSpec(
            num_scalar_prefetch=1, grid=(4, R // tr),
            in_specs=[pl.BlockSpec((1, 1, tr, C), lambda j, i, core: (j, core[0], i, 0)),
                      pl.BlockSpec((1, tr, C), lambda j, i, core: (j, i, 0))],
            out_specs=pl.BlockSpec((1, tr, C), lambda j, i, core: (j, i, 0))),
        compiler_params=_params(("parallel", "parallel")),
    )(core, p4, got)


def _sum_devices(name, g8):
    _, R, C = g8.shape
    tr = _tile(R, SMALL_ROW_ALIGN, 8)

    def body(g_ref, o_ref):
        acc = g_ref[0]
        for d in range(1, N_DEV):
            acc = acc + g_ref[d]
        o_ref[...] = acc

    return pl.pallas_call(
        body, name=name, out_shape=jax.ShapeDtypeStruct((R, C), F32), grid=(R // tr,),
        in_specs=[pl.BlockSpec((N_DEV, tr, C), lambda i: (0, i, 0))], out_specs=pl.BlockSpec((tr, C), lambda i: (i, 0)),
        compiler_params=_params(("parallel",)),
    )(g8)


def _adamw_shard(name, layer, w, m, v, s4, got, chip, bufs):
    _, R, C = w.shape
    tr = _sum_tile(R, C) // 2 if _sum_tile(R, C) % 16 == 0 else _sum_tile(R, C)

    def body(chip_ref, w_ref, m_ref, v_ref, s_ref, g_ref, b0, b1, b2, b3, og, od, om, ov):
        g = ((s_ref[0] + g_ref[0]) + g_ref[1]) + g_ref[2]
        d, nm, nv = _adamw_fn(w_ref[0], g, m_ref[0], v_ref[0])
        og[0], od[0], om[0], ov[0] = g, d, nm, nv

    lay = pl.BlockSpec((1, tr, C), lambda i, chip: (layer, i, 0))
    if bufs is None:
        bufs = [lax.empty(w.shape, F32) for _ in range(4)]
    return pl.pallas_call(
        body, name=name, out_shape=[jax.ShapeDtypeStruct(w.shape, F32)] * 4,
        grid_spec=pltpu.PrefetchScalarGridSpec(
            num_scalar_prefetch=1, grid=(R // tr,),
            in_specs=[lay, lay, lay, pl.BlockSpec((1, tr, C), lambda i, chip: (chip[0], i, 0)),
                      pl.BlockSpec((3, tr, C), lambda i, chip: (0, i, 0)), ANY, ANY, ANY, ANY],
            out_specs=[lay] * 4),
        input_output_aliases={6: 0, 7: 1, 8: 2, 9: 3},
        compiler_params=_params(("parallel",)),
    )(chip, w, m, v, s4, got, *bufs)


def _adamw(name, wt, g, m, v):
    shape = wt.shape
    two = (lambda a: a.reshape(1, -1)) if wt.ndim == 1 else (lambda a: a.reshape(-1, shape[-1]))
    w2, g2, m2, v2 = two(wt), two(g), two(m), two(v)
    tr = _row(w2, None, 0)[2]
    outs = [(w2.shape, F32, 0, tr)] * 3
    d, nm, nv = _rows_fwd(name, _adamw_fn, [_row(a, tr, 0) for a in (w2, g2, m2, v2)], [], outs)
    return d.reshape(shape), nm.reshape(shape), nv.reshape(shape)


def _lane_tiling():
    return (jnp.arange(SSM_N)[:, None] == (jnp.arange(SSM_S) % SSM_N)[None, :]).astype(BF16)


def _own_block():
    r = lax.broadcasted_iota(jnp.int32, (SSM_G * SSM_P, SSM_S), 0) // SSM_P
    c = lax.broadcasted_iota(jnp.int32, (SSM_G * SSM_P, SSM_S), 1) // SSM_N
    return r == c


def _bd_build(name, v_re, v_im, sign, tiling):
    def body(r_ref, i_ref, t_ref, o_ref):
        own = _own_block()
        o_ref[:, :SSM_S] = jnp.where(own, _dot(r_ref[...].astype(BF16), t_ref[...]), 0.0).astype(BF16)
        o_ref[:, SSM_S:] = jnp.where(own, sign * _dot(i_ref[...].astype(BF16), t_ref[...]), 0.0).astype(BF16)

    rows = SSM_G * SSM_P
    return pl.pallas_call(
        body, name=name, out_shape=jax.ShapeDtypeStruct((rows, 2 * SSM_S), BF16), grid=(1,),
        in_specs=[_full_spec((rows, SSM_N))] * 2 + [_full_spec((SSM_N, SSM_S))], out_specs=_full_spec((rows, 2 * SSM_S)),
        compiler_params=_params(("arbitrary",)),
    )(v_re, v_im, tiling)


def _bd_extract(name, m, sign, tiling):
    def body(m_ref, t_ref, r_ref, i_ref):
        own, t = _own_block(), t_ref[...]

        def pick(x):
            x = jnp.where(own, x, 0.0)
            hi = x.astype(BF16)
            rest = x - hi.astype(F32)
            mid = rest.astype(BF16)
            lo = (rest - mid.astype(F32)).astype(BF16)
            return _dot(hi, t, NT) + _dot(mid, t, NT) + _dot(lo, t, NT)

        r_ref[...] = pick(m_ref[:, :SSM_S])
        i_ref[...] = sign * pick(m_ref[:, SSM_S:])

    rows = SSM_G * SSM_P
    return pl.pallas_call(
        body, name=name, out_shape=[jax.ShapeDtypeStruct((rows, SSM_N), F32)] * 2, grid=(1,),
        in_specs=[_full_spec((rows, 2 * SSM_S)), _full_spec((SSM_N, SSM_S))], out_specs=[_full_spec((rows, SSM_N))] * 2,
        compiler_params=_params(("arbitrary",)),
    )(m, tiling)


def _fold(a, dil):
    if dil == 1:
        return a
    return a.reshape((T // dil, dil) + a.shape[1:]).swapaxes(0, 1).reshape(a.shape)


def _unfold(a, dil):
    if dil == 1:
        return a
    return a.reshape((dil, T // dil) + a.shape[1:]).swapaxes(0, 1).reshape(a.shape)


DILS = (1, 4, 16)


def _fold3(parts):
    parts = [parts] * 3 if not isinstance(parts, (list, tuple)) else parts
    return jnp.stack([_fold(a, d) for a, d in zip(parts, DILS)])


def _unfold3(a):
    return [_unfold(a[p], d) for p, d in enumerate(DILS)]


def _rope_tables():
    half = ROPE // 2
    inv_freq = ROPE_THETA ** (-jnp.arange(half, dtype=F32) / half)
    ang = jnp.arange(T).astype(F32)[:, None] * inv_freq[None, :]
    i, j = jnp.arange(ROPE)[:, None], jnp.arange(ROPE)[None, :]
    rot = jnp.where(i == j + half, -1.0, jnp.where(i + half == j, 1.0, 0.0)).astype(F32)
    return jnp.tile(jnp.cos(ang), (1, 2)), jnp.tile(jnp.sin(ang), (1, 2)), rot


def _rot_half(x, rot):
    return _dot(x.reshape(-1, ROPE), rot, prec=HI).reshape(x.shape)


def _mla_pack_fn(q, kv, k_rope, cos, sin, rot):
    rope = lambda x: x * cos + _rot_half(x, rot) * sin
    q_out = jnp.concatenate([q[:, :, :NOPE], rope(q[:, :, NOPE:])], axis=-1)
    k_pe = jnp.broadcast_to(rope(k_rope)[None], (H_MLA,) + k_rope.shape)
    return q_out, jnp.concatenate([kv[:, :, :NOPE], k_pe], axis=-1), kv[:, :, NOPE:]


def _mla_unpack_fn(dq, dk, dv, cos, sin, rot):
    unrope = lambda g: g * cos - _rot_half(g * sin, rot)
    dq_out = jnp.concatenate([dq[:, :, :NOPE], unrope(dq[:, :, NOPE:])], axis=-1)
    dk_rope = unrope(jnp.sum(dk[:, :, NOPE:], axis=0))
    return dq_out, jnp.concatenate([dk[:, :, :NOPE], dv], axis=-1), dk_rope


def _flat(w8):
    return w8.reshape(-1, w8.shape[-1])


def _blocks(m):
    return m.reshape(N_DEV, -1, m.shape[-1])


def _behind(a, tok):
    return a if tok is None else a + tok


def _mixer_fwd_in(x, w, sp, rope):
    s = {}
    s['x'] = x
    h = _rms_fwd("rms_mix", x, sp['g_mix'])
    proj = _mm("mm_in", h, _flat(w['w_in']), 'nt')
    offs = np.cumsum((0,) + IN_SPLITS)
    c_q, c_kv, k_rope, u, qd, kd, vd = [proj[:, offs[i]:offs[i + 1]] for i in range(7)]
    s.update(h=h, c_q=c_q, c_kv=c_kv, u=u)

    cqn = _rms_fwd("rms_q", c_q, sp['g_q'])
    ckvn = _rms_fwd("rms_kv", c_kv, sp['g_kv'])
    q8 = _mm("mm_uq", cqn, w['w_uq'], 'nt', bb='r', ob='c')
    kv8 = _mm("mm_ukv", ckvn, w['w_ukv'], 'nn', bb='c', ob='c')
    cos, sin, rot = rope
    tr = MLA_TQ
    qh, kh, vh = _rows_fwd(
        "mla_pack", _mla_pack_fn,
        [_row(q8, tr, 1), _row(kv8, tr, 1), _row(k_rope, tr, 0), _row(cos, tr, 0), _row(sin, tr, 0)], [rot],
        [((H_MLA, T, QK), BF16, 1, tr), ((H_MLA, T, QK), BF16, 1, tr), ((H_MLA, T, VDIM), BF16, 1, tr)])
    y_mla, lse_mla = _mla_fwd(qh, kh, vh)
    s.update(cqn=cqn, ckvn=ckvn, qh=qh, kh=kh, vh=vh, lse_mla=lse_mla, y_mla=y_mla, qd=qd, kd=kd, vd=vd)
    return s


def _mixer_fwd_out(s, w, sp, tok=None, after_ssm=None):
    x, u, y_mla, qd, kd, vd = s['x'], s['u'], s['y_mla'], s['qd'], s['kd'], s['vd']
    a3 = lambda n: sp[n].reshape(SSM_G, 1, SSM_N)
    b2 = lambda n: sp[n].transpose(0, 2, 1).reshape(SSM_G * SSM_P, SSM_N)
    disc_rows = [_row(a3('a_re'), 1, 0), _row(a3('a_im'), 1, 0), _row(sp['log_dt'].reshape(SSM_G, 1, 1), 1, 0),
                 _row(b2('b_re'), SSM_P, 0), _row(b2('b_im'), SSM_P, 0)]
    abr, abi, bbr, bbi = _rows_fwd(
        "s5_disc", _s5_disc_fn, disc_rows, [],
        [((SSM_G, 1, SSM_N), F32, 0, 1), ((SSM_G, 1, SSM_N), F32, 0, 1),
         ((SSM_G * SSM_P, SSM_N), F32, 0, SSM_P), ((SSM_G * SSM_P, SSM_N), F32, 0, SSM_P)])
    ar, ai = abr.reshape(1, SSM_S), abi.reshape(1, SSM_S)
    tiling = _lane_tiling()
    b_mat = _bd_build("s5_b_matrix", bbr, bbi, 1.0, tiling)
    c2 = lambda n: sp[n].reshape(SSM_G * SSM_P, SSM_N)
    c_mat = _bd_build("s5_c_matrix", c2('c_re'), c2('c_im'), -1.0, tiling)
    u16 = _behind(u, tok).astype(BF16)
    bu = _mm("mm_s5_b", u16, b_mat, 'nn')
    hst, hst16 = _scan_fwd(bu, ar, ai)
    ymm = _mm("mm_s5_c", hst16, c_mat, 'nt')
    d_row = sp['d_skip'].reshape(1, SSM_W)
    (yg,) = _rows_fwd("s5_act", _s5_act_fn, [_row(ymm), _row(u)], [d_row], [((T, SSM_W), BF16, -2, _row(u)[2])])
    z = _mm("mm_glu", yg, w['w_glu'], 'nn', bb='c')
    glu_rows = [_row(z[:, :SSM_W]), _row(z[:, SSM_W:])]
    glu_b = [sp['b_glu'][:SSM_W].reshape(1, -1), sp['b_glu'][SSM_W:].reshape(1, -1)]
    (y_ssm,) = _rows_fwd("s5_glu", _glu_fn, glu_rows, glu_b, [((T, SSM_W), F32, -2, glu_rows[0][2])])
    if after_ssm is not None:
        qd = _behind(qd, after_ssm(y_ssm))
    s.update(disc_rows=disc_rows, ar=ar, ai=ai, b_mat=b_mat, c_mat=c_mat, hst=hst, hst16=hst16, u16=u16, ymm=ymm,
             d_row=d_row, yg=yg,
             glu_rows=glu_rows, glu_b=glu_b)

    qf, kf, vf = [_fold3(a).astype(BF16) for a in (qd, kd, vd)]
    o_f, lse_f = _band_fwd(qf, kf, vf)
    mix_rows = [_row(a) for a in _unfold3(o_f) + _unfold3(lse_f)]
    (y_dil,) = _rows_fwd("dil_mix", _dil_mix_fn, mix_rows, [], [((T, DIL_W), F32, -2, mix_rows[0][2])])
    s.update(qf=qf, kf=kf, vf=vf, o_f=o_f, lse_f=lse_f, mix_rows=mix_rows)

    gm, gs, gd = sp['g_out_mla'].reshape(1, -1), sp['g_out_ssm'].reshape(1, -1), sp['g_out_dil'].reshape(1, -1)
    on_rows = [_row(y_mla), _row(y_ssm), _row(y_dil)]
    (ycat,) = _rows_fwd("out_norm", _outnorm_fn, on_rows, [gm, gs, gd], [((T, D), BF16, -2, on_rows[0][2])])
    x1_ = _mm("mm_o", ycat, _flat(w['w_o']), 'nn', res=x)
    s.update(on_rows=on_rows, on_g=[gm, gs, gd], ycat=ycat, x1=x1_)
    for k in ('qd', 'kd', 'vd'):
        del s[k]
    return x1_


def _ffn_fwd(x1_, w, sp, s, tok=None):
    h2 = _rms_fwd("rms_ffn", x1_, _behind(sp['g_ffn'], tok))
    ga = _mm("mm_gate", h2, _flat(w['w_gate']), 'nt')
    gb = _mm("mm_up", h2, _flat(w['w_up']), 'nt')
    ffn_rows = [_row(ga), _row(gb)]
    (zf,) = _rows_fwd("swiglu", _swiglu_fn, ffn_rows, [], [(ga.shape, BF16, -2, ffn_rows[0][2])])
    x2_ = _mm("mm_down", zf, _flat(w['w_down']), 'nn', res=x1_)
    s.update(h2=h2, ffn_rows=ffn_rows, zf=zf)
    return x2_


def _b16(a):
    return a.astype(BF16)


def _ffn_bwd(dx2, s, w, sp, tok=None):
    gw, gs_ = {}, {}
    b16 = _b16
    dx2b = b16(_behind(dx2, tok))
    dzf = _mm("mm_down_dx", dx2b, _flat(w['w_down']), 'nt')
    gw['w_down'] = _blocks(_mm("mm_down_dw", s['zf'], dx2b, 'tn'))
    dga, dgb = _rows_vjp("swiglu_bwd", _swiglu_fn, s['ffn_rows'], [], [], [_row(dzf)], grad_dtypes=[BF16, BF16])
    gw['w_gate'] = _blocks(_mm("mm_gate_dw", dga, s['h2'], 'tn'))
    gw['w_up'] = _blocks(_mm("mm_up_dw", dgb, s['h2'], 'tn'))
    dh2 = _mm("mm_up_dx", dgb, _flat(w['w_up']), 'nn', res=_mm("mm_gate_dx", dga, _flat(w['w_gate']), 'nn'))
    dx1, gs_['g_ffn'] = _rms_bwd("rms_ffn_bwd", s['x1'], sp['g_ffn'], dh2, dx2)
    return dx1, gw, gs_


def _mixer_bwd_out(dx1, s, w, sp, tok=None):
    gw, gs_ = {}, {}
    b16 = _b16
    dx1b = b16(_behind(dx1, tok))
    dycat = _mm("mm_o_dx", dx1b, _flat(w['w_o']), 'nt')
    gw['w_o'] = _blocks(_mm("mm_o_dw", s['ycat'], dx1b, 'tn'))
    dy_mla, dy_ssm, dy_dil, gs_['g_out_mla'], gs_['g_out_ssm'], gs_['g_out_dil'] = _rows_vjp(
        "out_norm_bwd", _outnorm_fn, s['on_rows'], s['on_g'], [], [_row(dycat)])

    dmix = _rows_vjp("dil_mix_bwd", _dil_mix_fn, s['mix_rows'], [], [], [_row(dy_dil)])
    dqf, dkf, dvf = _band_bwd(s['qf'], s['kf'], s['vf'], s['o_f'], s['lse_f'], _fold3(dmix[:3]), _fold3(dmix[3:]))
    back = lambda a: sum(_unfold3(a))
    dqd, dkd, dvd = back(dqf), back(dkf), back(dvf)

    dz1, dz2, db1, db2 = _rows_vjp("s5_glu_bwd", _glu_fn, s['glu_rows'], s['glu_b'], [], [_row(dy_ssm)])
    gs_['b_glu'] = jnp.concatenate([db1, db2], axis=1)
    dzb = b16(jnp.concatenate([dz1, dz2], axis=1))
    dyg = _mm("mm_glu_dx", dzb, w['w_glu'], 'nt', bb='c')
    gw['w_glu'] = _mm("mm_glu_dw", s['yg'], dzb, 'tn', ob='c')
    dymm, du_act, dd = _rows_vjp("s5_act_bwd", _s5_act_fn, [_row(s['ymm']), _row(s['u'])], [s['d_row']], [], [_row(dyg)],
                                 grad_dtypes=[BF16, F32])
    gs_['d_skip'] = dd
    dhst = _mm("mm_s5_c_dx", dymm, s['c_mat'], 'nn')
    dc_mat = _mm("mm_s5_c_dw", dymm, s['hst16'], 'tn')
    g, dar, dai = _scan_bwd(dhst, s['hst'], s['ar'], s['ai'])
    du = _mm("mm_s5_b_dx", g, s['b_mat'], 'nt', res=du_act)
    db_mat = _mm("mm_s5_b_dw", s['u16'], g, 'tn')
    tiling = _lane_tiling()
    gs_['c_re'], gs_['c_im'] = _bd_extract("s5_c_blocks", dc_mat, -1.0, tiling)
    dbbr, dbbi = _bd_extract("s5_b_blocks", db_mat, 1.0, tiling)
    disc_cts = [_row(dar.reshape(SSM_G, 1, SSM_N), 1, 0), _row(dai.reshape(SSM_G, 1, SSM_N), 1, 0),
                _row(dbbr, SSM_P, 0), _row(dbbi, SSM_P, 0)]
    da_re, da_im, dldt, db_r, db_i = _rows_vjp("s5_disc_bwd", _s5_disc_fn, s['disc_rows'], [], [], disc_cts)
    gs_['a_re'], gs_['a_im'], gs_['log_dt'] = da_re, da_im, dldt
    unb = lambda a: a.reshape(SSM_G, SSM_P, SSM_N).transpose(0, 2, 1)
    gs_['b_re'], gs_['b_im'] = unb(db_r), unb(db_i)
    return (dy_mla, du, dqd, dkd, dvd), gw, gs_


def _mixer_bwd_in(cts, dx1, s, w, sp, rope, tok=None):
    gw, gs_ = {}, {}
    b16 = _b16
    dy_mla, du, dqd, dkd, dvd = cts
    dqh, dkh, dvh = _mla_bwd(s['qh'], s['kh'], s['vh'], s['y_mla'], dy_mla, _behind(s['lse_mla'], tok))
    cos, sin, rot = rope
    tr = MLA_TQ
    dq8, dkv8, dk_rope = _rows_fwd(
        "mla_unpack", _mla_unpack_fn,
        [_row(dqh, tr, 1), _row(dkh, tr, 1), _row(dvh, tr, 1), _row(cos, tr, 0), _row(sin, tr, 0)], [rot],
        [((H_MLA, T, QK), BF16, 1, tr), ((H_MLA, T, NOPE + VDIM), BF16, 1, tr), ((T, ROPE), F32, 0, tr)])
    dcqn = _mm("mm_uq_dx", dq8, w['w_uq'], 'nn', ab='c', bb='r')
    gw['w_uq'] = _mm("mm_uq_dw", dq8, s['cqn'], 'tn', ab='c', ob='r')
    dckvn = _mm("mm_ukv_dx", dkv8, w['w_ukv'], 'nt', ab='c', bb='c')
    gw['w_ukv'] = _mm("mm_ukv_dw", s['ckvn'], dkv8, 'tn', bb='c', ob='c')
    dc_q, gs_['g_q'] = _rms_bwd("rms_q_bwd", s['c_q'], sp['g_q'], dcqn)
    dc_kv, gs_['g_kv'] = _rms_bwd("rms_kv_bwd", s['c_kv'], sp['g_kv'], dckvn)

    dproj = b16(jnp.concatenate([dc_q, dc_kv, dk_rope, du, dqd, dkd, dvd], axis=1))
    dh = _mm("mm_in_dx", dproj, _flat(w['w_in']), 'nn')
    gw['w_in'] = _blocks(_mm("mm_in_dw", dproj, s['h'], 'tn'))
    dx, gs_['g_mix'] = _rms_bwd("rms_mix_bwd", s['x'], sp['g_mix'], dh, dx1)
    return dx, gw, gs_


def kernel(x, g_mix, w_in, g_q, w_uq, g_kv, w_ukv, a_re, a_im, b_re, b_im, c_re, c_im, d_skip, log_dt, w_glu, b_glu, g_out_mla, g_out_ssm, g_out_dil, w_o, g_ffn, w_gate, w_up, w_down, g_final, loss_target, m_g_mix, m_w_in, m_g_q, m_w_uq, m_g_kv, m_w_ukv, m_a_re, m_a_im, m_b_re, m_b_im, m_c_re, m_c_im, m_d_skip, m_log_dt, m_w_glu, m_b_glu, m_g_out_mla, m_g_out_ssm, m_g_out_dil, m_w_o, m_g_ffn, m_w_gate, m_w_up, m_w_down, m_g_final, v_g_mix, v_w_in, v_g_q, v_w_uq, v_g_kv, v_w_ukv, v_a_re, v_a_im, v_b_re, v_b_im, v_c_re, v_c_im, v_d_skip, v_log_dt, v_w_glu, v_b_glu, v_g_out_mla, v_g_out_ssm, v_g_out_dil, v_w_o, v_g_ffn, v_w_gate, v_w_up, v_w_down, v_g_final):
    W = dict(zip(PARAMS, (g_mix, w_in, g_q, w_uq, g_kv, w_ukv, a_re, a_im, b_re, b_im, c_re, c_im, d_skip, log_dt,
                          w_glu, b_glu, g_out_mla, g_out_ssm, g_out_dil, w_o, g_ffn, w_gate, w_up, w_down, g_final)))
    M = dict(zip(PARAMS, (m_g_mix, m_w_in, m_g_q, m_w_uq, m_g_kv, m_w_ukv, m_a_re, m_a_im, m_b_re, m_b_im, m_c_re,
                          m_c_im, m_d_skip, m_log_dt, m_w_glu, m_b_glu, m_g_out_mla, m_g_out_ssm, m_g_out_dil, m_w_o,
                          m_g_ffn, m_w_gate, m_w_up, m_w_down, m_g_final)))
    V = dict(zip(PARAMS, (v_g_mix, v_w_in, v_g_q, v_w_uq, v_g_kv, v_w_ukv, v_a_re, v_a_im, v_b_re, v_b_im, v_c_re,
                          v_c_im, v_d_skip, v_log_dt, v_w_glu, v_b_glu, v_g_out_mla, v_g_out_ssm, v_g_out_dil, v_w_o,
                          v_g_ffn, v_w_gate, v_w_up, v_w_down, v_g_final)))
    cx, cy, cc = _place()
    core = cc.astype(jnp.int32).reshape(1)
    chip = (2 * cx + cy).astype(jnp.int32).reshape(1)
    rope = _rope_tables()
    small = [{n: W[n][l] for n in SMALL} for l in range(DEPTH)]
    for sp in small:
        for n in ('g_mix', 'g_q', 'g_kv', 'g_ffn'):
            sp[n] = sp[n].reshape(1, -1)

    def tok_of(tokens):
        return sum(t[0, 0] for t in tokens) if tokens else None

    def shard_view(a, n):
        return a.swapaxes(1, 2) if BIG[n] == 't' else a

    def gather_start(l, group, names, after):
        return _gather_start(f"gather_{group}_start_{l}", [shard_view(W[n], n)[l].astype(BF16) for n in names], after)

    xa = x[0]
    h1_mix = gather_start(0, "mix", MIXER_W, jnp.zeros((8, LANES), F32))
    h1_ffn = gather_start(0, "ffn", FFN_W, h1_mix[4])
    h2_mix = _gather_pass_on("gather_mix_0", h1_mix, xa)
    saved, full = [], []
    tokens = [h2_mix[4], h1_ffn[4]]
    for l in range(DEPTH):
        last = l + 1 == DEPTH
        wm = dict(zip(MIXER_W, _gather_finish(f"gather_mix_{l}", h2_mix, xa)))
        sp = dict(small[l])
        sp['g_mix'] = _behind(sp['g_mix'], tok_of(tokens))
        s = _mixer_fwd_in(xa, wm, sp, rope)
        tokens = []
        first_ffn = {}
        if l == 0:
            h1_first = h1_ffn
            def mid(dep):
                first_ffn['h'] = _gather_pass_on("gather_ffn_0", h1_first, dep)
                return first_ffn['h'][4][0, 0]
        else:
            mid = None
        if not last:
            h1_mix = gather_start(l + 1, "mix", MIXER_W, s['y_mla'])
            h1_ffn = gather_start(l + 1, "ffn", FFN_W, h1_mix[4])
            tokens += [h1_mix[4], h1_ffn[4]]
        x1 = _mixer_fwd_out(s, wm, small[l], tok_of(tokens), mid)
        if l == 0:
            h2_ffn = first_ffn['h']
        tokens = []
        wf = dict(zip(FFN_W, _gather_finish(f"gather_ffn_{l}", h2_ffn, x1)))
        if not last:
            h2_mix = _gather_pass_on(f"gather_mix_{l + 1}", h1_mix, x1)
            tokens.append(h2_mix[4])
        xa = _ffn_fwd(x1, wf, small[l], s, tok_of(tokens))
        tokens = []
        if not last:
            h2_ffn = _gather_pass_on(f"gather_ffn_{l + 1}", h1_ffn, xa)
            tokens.append(h2_ffn[4])
        saved.append(s)
        full.append({**wm, **wf})
    gf = g_final.reshape(1, D)
    ones = jnp.ones((T, 1), F32)
    dxa, dgf, loss_rows = _rows_vjp("loss", _loss_fn, [_row(xa)], [gf], [_row(loss_target[0])], [_row(ones)],
                                    primal=True)
    loss_here = _sum_rows("loss_sum", loss_rows)[0, 0]

    bufs = {n: None for n in BIG}
    pending = []

    def advance(dep):
        tokens = []
        for g in pending:
            names, tag = g['names'], g['tag']
            if g['stage'] == 0:
                p4 = [a.reshape((4, 2) + a.shape[1:]) for a in g['gw']]
                gots = [lax.empty((4,) + a.shape[1:], F32) for a in g['gw']]
                g['h'] = _split_start("rs_sibling_start_" + tag, p4, gots, 4 * len(p4), _sibling_plan, dep)
                tokens.append(g['h'][4])
            elif g['stage'] == 1:
                p4, gots = _split_wait("rs_sibling_wait_" + tag, g['h'], 4 * len(names), _sibling_plan, dep)
                s4 = [_pair_sum("rs_pair_sum_" + n, p, q, core) for n, p, q in zip(names, p4, gots)]
                gots = [lax.empty((3,) + a.shape[1:], F32) for a in s4]
                g['h'] = _split_start("rs_chips_start_" + tag, s4, gots, 3 * len(s4), _chips_plan, dep)
                tokens.append(g['h'][4])
            elif g['stage'] == 3:
                s4, gots = _split_wait("rs_chips_wait_" + tag, g['h'], 3 * len(names), _chips_plan, dep)
                for n, s4n, got in zip(names, s4, gots):
                    bufs[n] = _adamw_shard("adamw_" + n, g['layer'], shard_view(W[n], n), shard_view(M[n], n),
                                           shard_view(V[n], n), s4n, got, chip, bufs[n])
            g['stage'] += 1
        pending[:] = [g for g in pending if g['stage'] < 4]
        return tokens

    def group(names, l, gw, kind):
        return dict(names=names, layer=l, gw=[gw[n] for n in names], stage=0, tag=f"{kind}_{l}")

    g_small = [None] * DEPTH
    tokens = []
    for l in reversed(range(DEPTH)):
        dx1, gw_f, gs_f = _ffn_bwd(dxa, saved[l], full[l], small[l], tok_of(tokens))
        pending.append(group(FFN_W, l, gw_f, "ffn"))
        tokens = advance(dx1)
        cts, gw_o, gs_o = _mixer_bwd_out(dx1, saved[l], full[l], small[l], tok_of(tokens))
        pending.append(group(OUT_W, l, gw_o, "out"))
        tokens = advance(cts[0])
        dxa, gw_i, gs_i = _mixer_bwd_in(cts, dx1, saved[l], full[l], small[l], rope, tok_of(tokens))
        pending.append(group(IN_W, l, gw_i, "in"))
        tokens = advance(dxa)
        g_small[l] = {**gs_f, **gs_o, **gs_i}

    flat = [g_small[l][n].reshape(-1) for l in range(DEPTH) for n in SMALL] + [dgf.reshape(-1), loss_here.reshape(1)]
    n_small = sum(int(f.shape[0]) for f in flat)
    rows = -(-n_small // (PACK_C * SMALL_ROW_ALIGN)) * SMALL_ROW_ALIGN
    flat = jnp.concatenate(flat + [jnp.zeros((rows * PACK_C - n_small,), F32)]).reshape(rows, PACK_C)
    h_small = _gather_start("gather_small_start", [flat], dxa)
    advance(h_small[4])
    h_small = _gather_pass_on("gather_small", h_small, flat)
    tokens = advance(h_small[4])
    (gathered,) = _gather_finish("gather_small", h_small, flat)
    tot = _behind(_sum_devices("small_sum", gathered).reshape(-1), tok_of(tokens))
    grads, off = {}, 0
    per_layer = {n: [] for n in SMALL}
    for l in range(DEPTH):
        for n, shp in SMALL.items():
            k = int(np.prod(shp))
            per_layer[n].append(tot[off:off + k].reshape(shp))
            off += k
    for n in SMALL:
        grads[n] = jnp.stack(per_layer[n])
    grads['g_final'] = tot[off:off + D]
    loss = tot[off + D]

    delta, new_m, new_v = {}, {}, {}
    for n in PARAMS:
        if n not in BIG:
            delta[n], new_m[n], new_v[n] = _adamw("adamw_" + n, W[n], grads[n], M[n], V[n])
    while pending:
        advance(delta['g_final'])
    for n in BIG:
        grads[n], delta[n], new_m[n], new_v[n] = [shard_view(b, n) for b in bufs[n]]
    return (loss, dxa[None], *[grads[n] for n in PARAMS], *[delta[n] for n in PARAMS],
            *[new_m[n] for n in PARAMS], *[new_v[n] for n in PARAMS])
```

```python
import jax
import jax.numpy as jnp
import numpy as np
from jax import lax
from jax.experimental import pallas as pl
from jax.experimental.pallas import tpu as pltpu

F32 = jnp.float32
BF16 = jnp.bfloat16

T = 2048
D = 2048
DEPTH = 4
N_DEV = 8
H_MLA, NOPE, ROPE, VDIM = 8, 128, 64, 128
QK = NOPE + ROPE
Q_LORA, KV_LORA = 512, 256
SSM_W, SSM_G, SSM_P, SSM_N = 512, 32, 16, 64
SSM_S = SSM_G * SSM_N
DIL_W, DIL_H, DIL_D = 512, 8, 64
BLK = 128
IN_SPLITS = (Q_LORA, KV_LORA, ROPE, SSM_W, DIL_W, DIL_W, DIL_W)
IN_W = sum(IN_SPLITS)
D_FF = 5632
EPS = 1e-6
ROPE_THETA = 10000.0
MLA_SCALE = QK ** -0.5
DIL_SCALE = DIL_D ** -0.5

ADAM_LR, ADAM_B1, ADAM_B2, ADAM_EPS, ADAM_WD, ADAM_STEP = 0.001, 0.9, 0.999, 1e-08, 0.01, 10

VMEM_LIMIT_V7X = 52 * 1024 * 1024
LANES = 128
PACK_C = 1024
ROW_BLOCK_BYTES = 2 * 1024 * 1024
MM_TM, MM_TN, MM_TK = 1408, 1024, 5632
MM_DEEP = IN_W
MM_TB = 512

NT = (((1,), (1,)), ((), ()))
TN = (((0,), (0,)), ((), ()))
H_QK = (((2,), (2,)), ((0,), (0,)))
H_PV = (((2,), (1,)), ((0,), (0,)))
H_TN = (((1,), (1,)), ((0,), (0,)))
HI = lax.Precision.HIGHEST
MESH = pl.DeviceIdType.MESH

PARAMS = ['g_mix', 'w_in', 'g_q', 'w_uq', 'g_kv', 'w_ukv', 'a_re', 'a_im', 'b_re', 'b_im', 'c_re', 'c_im',
          'd_skip', 'log_dt', 'w_glu', 'b_glu', 'g_out_mla', 'g_out_ssm', 'g_out_dil', 'w_o', 'g_ffn',
          'w_gate', 'w_up', 'w_down', 'g_final']
BIG = {'w_in': 't', 'w_uq': 't', 'w_ukv': 'c', 'w_glu': 'c', 'w_o': 'r', 'w_gate': 't', 'w_up': 't', 'w_down': 'r'}
MIXER_W = ['w_in', 'w_uq', 'w_ukv', 'w_glu', 'w_o']
FFN_W = ['w_gate', 'w_up', 'w_down']
OUT_W, IN_W = ['w_o', 'w_glu'], ['w_in', 'w_uq', 'w_ukv']
SMALL = {'g_mix': (D,), 'g_q': (Q_LORA,), 'g_kv': (KV_LORA,), 'a_re': (SSM_G, SSM_N), 'a_im': (SSM_G, SSM_N),
         'b_re': (SSM_G, SSM_N, SSM_P), 'b_im': (SSM_G, SSM_N, SSM_P), 'c_re': (SSM_G, SSM_P, SSM_N),
         'c_im': (SSM_G, SSM_P, SSM_N), 'd_skip': (SSM_G, SSM_P), 'log_dt': (SSM_G,), 'b_glu': (2 * SSM_W,),
         'g_out_mla': (H_MLA * VDIM,), 'g_out_ssm': (SSM_W,), 'g_out_dil': (DIL_W,), 'g_ffn': (D,)}
SMALL_ROW_ALIGN = 64


def _tile(dim, target, align=LANES):
    best = None
    for t in range(align, min(dim, target) + 1, align):
        if dim % t == 0:
            best = t
    return best if best is not None else dim


def _params(sem=None):
    return pltpu.CompilerParams(dimension_semantics=sem, vmem_limit_bytes=VMEM_LIMIT_V7X)


def _dot(a, b, dims=None, prec=None):
    if dims is None:
        return jnp.dot(a, b, preferred_element_type=F32, precision=prec)
    return lax.dot_general(a, b, dims, preferred_element_type=F32, precision=prec)


def _mm_spec(shape, blk, t_r, t_c, rc):
    if blk is None:
        return pl.BlockSpec((t_r, t_c), rc)
    _, R, C = shape
    if blk == 'r':
        per = R // t_r
        return pl.BlockSpec((1, t_r, t_c), lambda i, j, k: (rc(i, j, k)[0] // per, rc(i, j, k)[0] % per, rc(i, j, k)[1]))
    per = C // t_c
    return pl.BlockSpec((1, t_r, t_c), lambda i, j, k: (rc(i, j, k)[1] // per, rc(i, j, k)[0], rc(i, j, k)[1] % per))


def _logical(shape, blk):
    if blk is None:
        return tuple(shape)
    G, R, C = shape
    return (G * R, C) if blk == 'r' else (R, G * C)


def _mm(name, a, b, mode, ab=None, bb=None, ob=None, res=None, prec=None):
    la, lb = _logical(a.shape, ab), _logical(b.shape, bb)
    am, ak = (0, 1) if mode != 'tn' else (1, 0)
    bk, bn = (0, 1) if mode != 'nt' else (1, 0)
    M, K, N = la[am], la[ak], lb[bn]
    assert lb[bk] == K, (name, a.shape, b.shape, mode)
    if ob is None:
        out_shape = (M, N)
    elif ob == 'r':
        G = N_DEV
        out_shape = (G, M // G, N)
    else:
        G = N_DEV
        out_shape = (G, M, N // G)
    em = min(a.shape[-2:][am], out_shape[-2])
    en = min(b.shape[-2:][bn], out_shape[-1])
    ek = min(a.shape[-2:][ak], b.shape[-2:][bk])
    dims = {'nn': None, 'nt': NT, 'tn': TN}[mode]
    a_kb = mode != 'tn' and ab == 'c'
    b_kb = (mode == 'nn' and bb == 'r') or (mode == 'nt' and bb == 'c')
    blocks = K // ek if (a_kb or b_kb) else 1
    assert blocks == 1 or ((a_kb or ab is None) and (b_kb or bb is None)), (name, ab, bb, mode)
    tk = ek if blocks > 1 else _tile(ek, MM_TK)
    nk = 1 if blocks > 1 else K // tk
    small = blocks > 1 or nk > 1 or K > MM_DEEP
    tn = _tile(en, MM_TB if small else MM_TN)
    tm = _tile(em, MM_TB if small else (2 * MM_TM if K <= MM_TB else MM_TM))
    if tn > MM_TN:
        tm = _tile(em, MM_TB)
    if tm > 2 * MM_TM:
        tn = _tile(en, MM_TB)

    def val(ref):
        return ref[...] if len(ref.shape) == 2 else ref[0]

    def put(o_ref, r):
        if len(o_ref.shape) == 2:
            o_ref[...] = r
        else:
            o_ref[0] = r

    def k_block(ref, d, blocked, lanes):
        if blocked:
            return ref[d]
        return ref[:, d * ek:(d + 1) * ek] if lanes else ref[d * ek:(d + 1) * ek, :]

    def body(*refs):
        if res is None:
            a_ref, b_ref, o_ref = refs[:3]
            r_ref = None
        else:
            a_ref, b_ref, r_ref, o_ref = refs[:4]
        if blocks > 1:
            part = None
            for d in range(blocks):
                p = _dot(k_block(a_ref, d, a_kb, True), k_block(b_ref, d, b_kb, mode == 'nt'), dims, prec)
                part = p if part is None else part + p
        else:
            part = _dot(val(a_ref), val(b_ref), dims, prec)
        if nk == 1:
            put(o_ref, part if r_ref is None else part + val(r_ref))
            return
        acc_ref = refs[-1]
        k = pl.program_id(2)

        @pl.when(k == 0)
        def _():
            acc_ref[...] = part

        @pl.when((k > 0) & (k < nk - 1))
        def _():
            acc_ref[...] += part

        @pl.when(k == nk - 1)
        def _():
            r = acc_ref[...] + part
            put(o_ref, r if r_ref is None else r + val(r_ref))

    if blocks > 1:
        G = blocks
        a_spec = (pl.BlockSpec((G, tm, ek), lambda i, j, k: (0, i, 0)) if a_kb
                  else pl.BlockSpec((tm, K), lambda i, j, k: (i, 0)))
        if b_kb:
            b_spec = (pl.BlockSpec((G, ek, tn), lambda i, j, k: (0, 0, j)) if mode == 'nn'
                      else pl.BlockSpec((G, tn, ek), lambda i, j, k: (0, j, 0)))
        else:
            b_spec = (pl.BlockSpec((K, tn), lambda i, j, k: (0, j)) if mode == 'nn'
                      else pl.BlockSpec((tn, K), lambda i, j, k: (j, 0)))
    else:
        if mode == 'tn':
            a_spec = _mm_spec(a.shape, ab, tk, tm, lambda i, j, k: (k, i))
        else:
            a_spec = _mm_spec(a.shape, ab, tm, tk, lambda i, j, k: (i, k))
        if mode == 'nt':
            b_spec = _mm_spec(b.shape, bb, tn, tk, lambda i, j, k: (j, k))
        else:
            b_spec = _mm_spec(b.shape, bb, tk, tn, lambda i, j, k: (k, j))
    o_spec = _mm_spec(out_shape, ob, tm, tn, lambda i, j, k: (i, j))
    in_specs = [a_spec, b_spec] + ([o_spec] if res is not None else [])
    args = (a, b) + ((res,) if res is not None else ())
    return pl.pallas_call(
        body, name=name, out_shape=jax.ShapeDtypeStruct(out_shape, F32),
        grid=(M // tm, N // tn, nk), in_specs=in_specs, out_specs=o_spec,
        scratch_shapes=[pltpu.VMEM((tm, tn), F32)] if nk > 1 else [],
        compiler_params=_params(("parallel", "parallel", "arbitrary")),
    )(*args)


def _row(a, tr=None, axis=-2):
    axis = axis % a.ndim
    n = a.shape[axis]
    if tr is None:
        row_bytes = a.size // n * 4
        tr = _tile(n, max(8, min(256, ROW_BLOCK_BYTES // row_bytes)), 8)
    return (a, axis, tr)


def _row_spec(shape, axis, tr):
    nd = len(shape)
    blk = tuple(tr if d == axis else s for d, s in enumerate(shape))
    return pl.BlockSpec(blk, lambda i: tuple(i if d == axis else 0 for d in range(nd)))


def _full_spec(shape):
    nd = len(shape)
    return pl.BlockSpec(tuple(shape), lambda i: (0,) * nd)


def _steps(entries):
    ns = {a.shape[ax] // tr for a, ax, tr in entries}
    assert len(ns) == 1, [(a.shape, ax, tr) for a, ax, tr in entries]
    return ns.pop()


def _as_tuple(r):
    return tuple(r) if isinstance(r, (tuple, list)) else (r,)


def _rows_fwd(name, fn, rows, bcast, outs):
    steps = _steps(rows)
    nr, nb = len(rows), len(bcast)

    def body(*refs):
        vals = [r[...] for r in refs[:nr + nb]]
        res = _as_tuple(fn(*vals))
        for o_ref, r in zip(refs[nr + nb:], res):
            o_ref[...] = r.astype(o_ref.dtype)

    in_specs = [_row_spec(a.shape, ax, tr) for a, ax, tr in rows] + [_full_spec(b.shape) for b in bcast]
    out_specs = [_row_spec(s, ax % len(s), tr) for s, _, ax, tr in outs]
    res = pl.pallas_call(
        body, name=name, out_shape=[jax.ShapeDtypeStruct(s, dt) for s, dt, _, _ in outs],
        grid=(steps,), in_specs=in_specs, out_specs=out_specs,
        compiler_params=_params(("parallel",)),
    )(*[a for a, _, _ in rows], *bcast)
    return res


def _rows_vjp(name, fn, drows, dbc, arows, cts, primal=False, grad_dtypes=None):
    entries = list(drows) + list(arows) + list(cts)
    steps = _steps(entries)
    ndr, ndb, nar, nct = len(drows), len(dbc), len(arows), len(cts)
    gdt = list(grad_dtypes) if grad_dtypes is not None else [F32] * ndr

    def body(*refs):
        p = 0
        dr = [r[...] for r in refs[p:p + ndr]]; p += ndr
        db = [r[...] for r in refs[p:p + ndb]]; p += ndb
        ar = [r[...] for r in refs[p:p + nar]]; p += nar
        ct = [r[...] for r in refs[p:p + nct]]; p += nct
        g_rows = refs[p:p + ndr]; p += ndr
        g_bc = refs[p:p + ndb]; p += ndb
        prim_refs = refs[p:]

        def f(*d):
            return _as_tuple(fn(*d, *ar))

        outs, pullback = jax.vjp(f, *dr, *db)
        grads = pullback(tuple(c.astype(o.dtype) for c, o in zip(ct, outs)))
        for k in range(ndr):
            g_rows[k][...] = grads[k].astype(g_rows[k].dtype)
        if ndb:
            @pl.when(pl.program_id(0) == 0)
            def _():
                for r in g_bc:
                    r[...] = jnp.zeros_like(r)
            for k in range(ndb):
                g_bc[k][...] += grads[ndr + k]
        for r, o in zip(prim_refs, outs):
            r[...] = o.astype(r.dtype)

    in_specs = ([_row_spec(a.shape, ax, tr) for a, ax, tr in drows] + [_full_spec(b.shape) for b in dbc]
                + [_row_spec(a.shape, ax, tr) for a, ax, tr in arows]
                + [_row_spec(a.shape, ax, tr) for a, ax, tr in cts])
    out_shape = ([jax.ShapeDtypeStruct(a.shape, dt) for (a, _, _), dt in zip(drows, gdt)]
                 + [jax.ShapeDtypeStruct(b.shape, F32) for b in dbc])
    out_specs = ([_row_spec(a.shape, ax, tr) for a, ax, tr in drows] + [_full_spec(b.shape) for b in dbc])
    if primal:
        out_shape += [jax.ShapeDtypeStruct(a.shape, F32) for a, _, _ in cts]
        out_specs += [_row_spec(a.shape, ax, tr) for a, ax, tr in cts]
    return pl.pallas_call(
        body, name=name, out_shape=out_shape, grid=(steps,), in_specs=in_specs, out_specs=out_specs,
        compiler_params=_params(("arbitrary",)),
    )(*[a for a, _, _ in drows], *dbc, *[a for a, _, _ in arows], *[a for a, _, _ in cts])


def _rms_fn(x, g):
    return x * lax.rsqrt(jnp.mean(x * x, axis=-1, keepdims=True) + EPS) * g


def _rms_res_fn(x, g):
    return _rms_fn(x, g), x


def _s5_act_fn(ymm, u, d):
    return jax.nn.gelu(ymm + d * u)


def _glu_fn(z1, z2, b1, b2):
    return (z1 + b1) * jax.nn.sigmoid(z2 + b2)


def _outnorm_fn(ym, ys, yd, gm, gs, gd):
    return jnp.concatenate([_rms_fn(ym, gm), _rms_fn(ys, gs), _rms_fn(yd, gd)], axis=-1)


def _swiglu_fn(a, b):
    return jax.nn.silu(a) * b


def _loss_fn(x, g, tgt):
    err = _rms_fn(x, g) - tgt
    return 0.5 * jnp.mean(err * err, axis=-1, keepdims=True)


def _dil_mix_fn(o0, o1, o2, l0, l1, l2):
    m = jnp.maximum(jnp.maximum(l0, l1), l2)
    e0, e1, e2 = jnp.exp(l0 - m), jnp.exp(l1 - m), jnp.exp(l2 - m)
    s = e0 + e1 + e2
    return (e0 / s) * o0 + (e1 / s) * o1 + (e2 / s) * o2


def _s5_disc_fn(a_re, a_im, ldt, b_r, b_i):
    lr = jnp.minimum(a_re.reshape(1, SSM_N), -1e-4)
    li = a_im.reshape(1, SSM_N)
    dt = jnp.exp(ldt.reshape(1, 1))
    e = jnp.exp(lr * dt)
    ar = e * jnp.cos(li * dt)
    ai = e * jnp.sin(li * dt)
    nr, ni = ar - 1.0, ai
    den = lr * lr + li * li
    cr = (nr * lr + ni * li) / den
    ci = (ni * lr - nr * li) / den
    return ar.reshape(1, 1, SSM_N), ai.reshape(1, 1, SSM_N), cr * b_r - ci * b_i, cr * b_i + ci * b_r


def _adamw_fn(w, g, m, v):
    m = ADAM_B1 * m + (1.0 - ADAM_B1) * g
    v = ADAM_B2 * v + (1.0 - ADAM_B2) * jnp.square(g)
    m_hat = m / (1.0 - ADAM_B1 ** ADAM_STEP)
    v_hat = v / (1.0 - ADAM_B2 ** ADAM_STEP)
    delta = -ADAM_LR * (m_hat / (jnp.sqrt(v_hat) + ADAM_EPS) + ADAM_WD * w)
    return delta, m, v


def _rms_fwd(name, x, g):
    (h,) = _rows_fwd(name, _rms_fn, [_row(x)], [g], [(x.shape, BF16, -2, _row(x)[2])])
    return h


def _rms_bwd(name, x, g, dh, dres=None):
    if dres is None:
        dx, dg = _rows_vjp(name, _rms_fn, [_row(x)], [g], [], [_row(dh)])
    else:
        dx, dg = _rows_vjp(name, _rms_res_fn, [_row(x)], [g], [], [_row(dh), _row(dres)])
    return dx, dg


MLA_TQ = 256
MLA_EXT = 512


def _mla_fwd(q, k, v):
    tq = MLA_TQ

    def body(q_ref, k_ref, v_ref, o_ref, lse_ref):
        i = pl.program_id(1)
        q = q_ref[0]

        def rows_below(ext):
            s = _dot(q, k_ref[0, :ext, :], NT) * MLA_SCALE
            row = i * tq + lax.broadcasted_iota(jnp.int32, (tq, ext), 0)
            col = lax.broadcasted_iota(jnp.int32, (tq, ext), 1)
            s = jnp.where(row >= col, s, -jnp.inf)
            m = jnp.max(s, axis=-1, keepdims=True)
            p = jnp.exp(s - m)
            l = jnp.sum(p, axis=-1, keepdims=True)
            o_ref[...] = _dot((p / l).astype(BF16), v_ref[0, :ext, :])
            lse_ref[0] = m + jnp.log(l)

        for g in range(T // MLA_EXT):
            pl.when(i // (MLA_EXT // tq) == g)(lambda g=g: rows_below((g + 1) * MLA_EXT))

    return pl.pallas_call(
        body, name="mla_fwd",
        out_shape=[jax.ShapeDtypeStruct((T, H_MLA * VDIM), F32), jax.ShapeDtypeStruct((H_MLA, T, 1), F32)],
        grid=(H_MLA, T // tq),
        in_specs=[pl.BlockSpec((1, tq, QK), lambda h, i: (h, i, 0)),
                  pl.BlockSpec((1, T, QK), lambda h, i: (h, 0, 0)),
                  pl.BlockSpec((1, T, VDIM), lambda h, i: (h, 0, 0))],
        out_specs=[pl.BlockSpec((tq, VDIM), lambda h, i: (i, h)),
                   pl.BlockSpec((1, tq, 1), lambda h, i: (h, i, 0))],
        compiler_params=_params(("parallel", "parallel")),
    )(q, k, v)


def _mla_bwd(q, k, v, o, do, lse):
    tq = MLA_TQ

    def body(q_ref, k_ref, v_ref, o_ref, do_ref, lse_ref, dq_ref, dk_ref, dv_ref):
        i = pl.program_id(1)

        @pl.when(i == 0)
        def _():
            dk_ref[...] = jnp.zeros_like(dk_ref)
            dv_ref[...] = jnp.zeros_like(dv_ref)

        q, lse = q_ref[0], lse_ref[0]
        delta = jnp.sum(do_ref[...] * o_ref[...], axis=-1, keepdims=True)
        do = do_ref[...].astype(BF16)

        def rows_below(ext):
            k, v = k_ref[0, :ext, :], v_ref[0, :ext, :]
            s = _dot(q, k, NT) * MLA_SCALE
            row = i * tq + lax.broadcasted_iota(jnp.int32, (tq, ext), 0)
            col = lax.broadcasted_iota(jnp.int32, (tq, ext), 1)
            p = jnp.where(row >= col, jnp.exp(s - lse), 0.0)
            ds = (p * (_dot(do, v, NT) - delta) * MLA_SCALE).astype(BF16)
            dq_ref[0] = _dot(ds, k)
            dk_ref[0, :ext, :] += _dot(ds, q, TN)
            dv_ref[0, :ext, :] += _dot(p.astype(BF16), do, TN)

        for g in range(T // MLA_EXT):
            pl.when(i // (MLA_EXT // tq) == g)(lambda g=g: rows_below((g + 1) * MLA_EXT))

    return pl.pallas_call(
        body, name="mla_bwd",
        out_shape=[jax.ShapeDtypeStruct((H_MLA, T, QK), F32), jax.ShapeDtypeStruct((H_MLA, T, QK), F32),
                   jax.ShapeDtypeStruct((H_MLA, T, VDIM), F32)],
        grid=(H_MLA, T // tq),
        in_specs=[pl.BlockSpec((1, tq, QK), lambda h, i: (h, i, 0)),
                  pl.BlockSpec((1, T, QK), lambda h, i: (h, 0, 0)),
                  pl.BlockSpec((1, T, VDIM), lambda h, i: (h, 0, 0)),
                  pl.BlockSpec((tq, VDIM), lambda h, i: (i, h)),
                  pl.BlockSpec((tq, VDIM), lambda h, i: (i, h)),
                  pl.BlockSpec((1, tq, 1), lambda h, i: (h, i, 0))],
        out_specs=[pl.BlockSpec((1, tq, QK), lambda h, i: (h, i, 0)),
                   pl.BlockSpec((1, T, QK), lambda h, i: (h, 0, 0)),
                   pl.BlockSpec((1, T, VDIM), lambda h, i: (h, 0, 0))],
        compiler_params=_params(("parallel", "arbitrary")),
    )(q, k, v, o, do, lse)


NBLK = T // BLK


BAND_GL = 256
BAND_GH = BAND_GL // DIL_D
BAND_ROWS = BAND_GH * BLK
BAND_GROUPS = DIL_W // BAND_GL


def _band_masks():
    r = lax.broadcasted_iota(jnp.int32, (BAND_ROWS, BLK), 0) & (BLK - 1)
    j = lax.broadcasted_iota(jnp.int32, (BAND_ROWS, BLK), 1)
    return j <= r, j >= r


def _head_lanes():
    lane_head = lax.broadcasted_iota(jnp.int32, (1, BAND_GL), 1) // DIL_D
    return [lane_head == h for h in range(BAND_GH)]


def _stack_heads(x, lanes):
    return jnp.concatenate([jnp.where(m, x, jnp.zeros_like(x)) for m in lanes], axis=0)


def _merge_heads(xs, lanes):
    out = None
    for h, m in enumerate(lanes):
        part = jnp.where(m, xs[h * BLK:(h + 1) * BLK], 0.0)
        out = part if out is None else out + part
    return out


def _per_head(x, lanes):
    return jnp.concatenate([jnp.sum(jnp.where(m, x, 0.0), axis=-1, keepdims=True) for m in lanes], axis=0)


def _lane_group(ref, g):
    return ref[0, :, g * BAND_GL:(g + 1) * BAND_GL]


def _seq_start(p, i):
    per_seq = lax.shift_right_logical(jnp.int32(NBLK), 2 * p)
    return lax.rem(i, per_seq) == 0


def _band_fwd(q, k, v):
    def body(q_ref, kp_ref, kc_ref, vp_ref, vc_ref, o_ref, lse_ref):
        p, i = pl.program_id(0), pl.program_id(1)
        has_prev = jnp.logical_not(_seq_start(p, i))
        m_cur, m_prev = _band_masks()
        m_prev = m_prev & has_prev
        lanes = _head_lanes()
        for g in range(BAND_GROUPS):
            qs = _stack_heads(_lane_group(q_ref, g), lanes)
            s_c = jnp.where(m_cur, _dot(qs, _lane_group(kc_ref, g), NT) * DIL_SCALE, -jnp.inf)
            s_p = jnp.where(m_prev, _dot(qs, _lane_group(kp_ref, g), NT) * DIL_SCALE, -jnp.inf)
            m = jnp.maximum(jnp.max(s_c, axis=-1, keepdims=True), jnp.max(s_p, axis=-1, keepdims=True))
            e_c, e_p = jnp.exp(s_c - m), jnp.exp(s_p - m)
            l = jnp.sum(e_c, axis=-1, keepdims=True) + jnp.sum(e_p, axis=-1, keepdims=True)
            os = (_dot((e_p / l).astype(BF16), _lane_group(vp_ref, g))
                  + _dot((e_c / l).astype(BF16), _lane_group(vc_ref, g)))
            cols = slice(g * BAND_GL, (g + 1) * BAND_GL)
            o_ref[0, :, cols] = _merge_heads(os, lanes)
            lse_ref[0, :, cols] = _merge_heads(m + jnp.log(l), lanes)

    blk = (1, BLK, DIL_W)
    cur = lambda p, i: (p, i, 0)
    prev = lambda p, i: (p, jnp.maximum(i - 1, 0), 0)
    return pl.pallas_call(
        body, name="band_fwd",
        out_shape=[jax.ShapeDtypeStruct((3, T, DIL_W), F32)] * 2,
        grid=(3, NBLK),
        in_specs=[pl.BlockSpec(blk, cur), pl.BlockSpec(blk, prev), pl.BlockSpec(blk, cur),
                  pl.BlockSpec(blk, prev), pl.BlockSpec(blk, cur)],
        out_specs=[pl.BlockSpec(blk, cur), pl.BlockSpec(blk, cur)],
        compiler_params=_params(("parallel", "parallel")),
    )(q, k, k, v, v)


def _band_bwd(q, k, v, o, lse, do, dlse):
    def body(qc_ref, qn_ref, kp_ref, kc_ref, vp_ref, vc_ref, oc_ref, on_ref, lc_ref, ln_ref,
             doc_ref, don_ref, dlc_ref, dln_ref, dq_ref, dk_ref, dv_ref):
        p, i = pl.program_id(0), pl.program_id(1)
        has_prev = jnp.logical_not(_seq_start(p, i))
        has_next = jnp.logical_not(_seq_start(p, i + 1)) & (i + 1 < NBLK)
        m_cur, m_prev = _band_masks()
        lanes = _head_lanes()

        def probs(qs, k, lse, mask):
            return jnp.where(mask, jnp.exp(_dot(qs, k, NT) * DIL_SCALE - lse), 0.0)

        def dscore(pr, dos, v, shift):
            return (pr * (_dot(dos, v, NT) + shift) * DIL_SCALE).astype(BF16)

        for g in range(BAND_GROUPS):
            grp = lambda ref: _lane_group(ref, g)
            kp, kc, vp, vc = grp(kp_ref), grp(kc_ref), grp(vp_ref), grp(vc_ref)
            qc, qn = _stack_heads(grp(qc_ref), lanes), _stack_heads(grp(qn_ref), lanes)
            doc, don = grp(doc_ref), grp(don_ref)
            lse_c = _per_head(grp(lc_ref), lanes) * (1.0 / DIL_D)
            lse_n = _per_head(grp(ln_ref), lanes) * (1.0 / DIL_D)
            sh_c = _per_head(grp(dlc_ref) - doc * grp(oc_ref), lanes)
            sh_n = _per_head(grp(dln_ref) - don * grp(on_ref), lanes)
            doc, don = _stack_heads(doc.astype(BF16), lanes), _stack_heads(don.astype(BF16), lanes)
            p_cc = probs(qc, kc, lse_c, m_cur)
            p_cp = probs(qc, kp, lse_c, m_prev & has_prev)
            p_nc = probs(qn, kc, lse_n, m_prev & has_next)
            ds_cc = dscore(p_cc, doc, vc, sh_c)
            ds_cp = dscore(p_cp, doc, vp, sh_c)
            ds_nc = dscore(p_nc, don, vc, sh_n)
            cols = slice(g * BAND_GL, (g + 1) * BAND_GL)
            dq_ref[0, :, cols] = _merge_heads(_dot(ds_cc, kc) + _dot(ds_cp, kp), lanes)
            dk_ref[0, :, cols] = _dot(ds_cc, qc, TN) + _dot(ds_nc, qn, TN)
            dv_ref[0, :, cols] = _dot(p_cc.astype(BF16), doc, TN) + _dot(p_nc.astype(BF16), don, TN)

    blk = (1, BLK, DIL_W)
    cur = lambda p, i: (p, i, 0)
    prev = lambda p, i: (p, jnp.maximum(i - 1, 0), 0)
    nxt = lambda p, i: (p, jnp.minimum(i + 1, NBLK - 1), 0)
    w, wn, wp = pl.BlockSpec(blk, cur), pl.BlockSpec(blk, nxt), pl.BlockSpec(blk, prev)
    return pl.pallas_call(
        body, name="band_bwd",
        out_shape=[jax.ShapeDtypeStruct((3, T, DIL_W), F32)] * 3,
        grid=(3, NBLK),
        in_specs=[w, wn, wp, w, wp, w, w, wn, w, wn, w, wn, w, wn],
        out_specs=[w, w, w],
        compiler_params=_params(("parallel", "parallel")),
    )(q, q, k, k, v, v, o, o, lse, lse, do, do, dlse, dlse)


SCAN_TC = 256


def _scan_fwd(bu, ar, ai):
    tc, S = SCAN_TC, SSM_S

    def body(bu_ref, ar_ref, ai_ref, h_ref, h16_ref, cr_ref, ci_ref):
        @pl.when(pl.program_id(0) == 0)
        def _():
            cr_ref[...] = jnp.zeros_like(cr_ref)
            ci_ref[...] = jnp.zeros_like(ci_ref)

        a_r, a_i = ar_ref[...], ai_ref[...]

        def step(j, carry):
            hr, hi = carry
            for r in range(8):
                t = pl.multiple_of(j * 8, 8) + r
                br = bu_ref[pl.ds(t, 1), pl.ds(0, S)]
                bi = bu_ref[pl.ds(t, 1), pl.ds(S, S)]
                hr, hi = a_r * hr - a_i * hi + br, a_r * hi + a_i * hr + bi
                h_ref[pl.ds(t, 1), pl.ds(0, S)] = hr
                h_ref[pl.ds(t, 1), pl.ds(S, S)] = hi
            return hr, hi

        hr, hi = lax.fori_loop(0, tc // 8, step, (cr_ref[...], ci_ref[...]))
        cr_ref[...] = hr
        ci_ref[...] = hi
        h16_ref[...] = h_ref[...].astype(BF16)

    return pl.pallas_call(
        body, name="s5_scan_fwd",
        out_shape=[jax.ShapeDtypeStruct((T, 2 * S), F32), jax.ShapeDtypeStruct((T, 2 * S), BF16)],
        grid=(T // tc,),
        in_specs=[pl.BlockSpec((tc, 2 * S), lambda i: (i, 0)), _full_spec((1, S)), _full_spec((1, S))],
        out_specs=[pl.BlockSpec((tc, 2 * S), lambda i: (i, 0)), pl.BlockSpec((tc, 2 * S), lambda i: (i, 0))],
        scratch_shapes=[pltpu.VMEM((1, S), F32), pltpu.VMEM((1, S), F32)],
        compiler_params=_params(("arbitrary",)),
    )(bu, ar, ai)


def _scan_bwd(dh, h, ar, ai):
    tc, S = SCAN_TC, SSM_S
    nc = T // tc

    def body(dh_ref, h_ref, hp_ref, ar_ref, ai_ref, g16_ref, dar_ref, dai_ref, cr_ref, ci_ref, g_ref):
        i = pl.program_id(0)

        @pl.when(i == 0)
        def _():
            cr_ref[...] = jnp.zeros_like(cr_ref)
            ci_ref[...] = jnp.zeros_like(ci_ref)
            dar_ref[...] = jnp.zeros_like(dar_ref)
            dai_ref[...] = jnp.zeros_like(dai_ref)

        a_r, a_i = ar_ref[...], ai_ref[...]
        first_chunk = (i == nc - 1)
        edge = jnp.where(first_chunk, 0.0, 1.0)
        hpr = hp_ref[pl.ds(7, 1), pl.ds(0, S)] * edge
        hpi = hp_ref[pl.ds(7, 1), pl.ds(S, S)] * edge

        def step(jj, carry):
            gr, gi, dar, dai = carry
            j = tc // 8 - 1 - jj
            for r in range(7, -1, -1):
                t = pl.multiple_of(j * 8, 8) + r
                tp = jnp.maximum(t - 1, 0)
                inside = t > 0
                pr = jnp.where(inside, h_ref[pl.ds(tp, 1), pl.ds(0, S)], hpr)
                pi = jnp.where(inside, h_ref[pl.ds(tp, 1), pl.ds(S, S)], hpi)
                gr, gi = (dh_ref[pl.ds(t, 1), pl.ds(0, S)] + a_r * gr + a_i * gi,
                          dh_ref[pl.ds(t, 1), pl.ds(S, S)] + a_r * gi - a_i * gr)
                g_ref[pl.ds(t, 1), pl.ds(0, S)] = gr
                g_ref[pl.ds(t, 1), pl.ds(S, S)] = gi
                dar = dar + gr * pr + gi * pi
                dai = dai + gi * pr - gr * pi
            return gr, gi, dar, dai

        zero = jnp.zeros((1, S), F32)
        gr, gi, dar, dai = lax.fori_loop(0, tc // 8, step, (cr_ref[...], ci_ref[...], zero, zero))
        cr_ref[...] = gr
        ci_ref[...] = gi
        dar_ref[...] += dar
        dai_ref[...] += dai
        g16_ref[...] = g_ref[...].astype(BF16)

    rev = lambda i: (nc - 1 - i, 0)
    before = lambda i: (jnp.maximum((nc - 1 - i) * (tc // 8) - 1, 0), 0)
    return pl.pallas_call(
        body, name="s5_scan_bwd",
        out_shape=[jax.ShapeDtypeStruct((T, 2 * S), BF16), jax.ShapeDtypeStruct((1, S), F32),
                   jax.ShapeDtypeStruct((1, S), F32)],
        grid=(nc,),
        in_specs=[pl.BlockSpec((tc, 2 * S), rev), pl.BlockSpec((tc, 2 * S), rev), pl.BlockSpec((8, 2 * S), before),
                  _full_spec((1, S)), _full_spec((1, S))],
        out_specs=[pl.BlockSpec((tc, 2 * S), rev), _full_spec((1, S)), _full_spec((1, S))],
        scratch_shapes=[pltpu.VMEM((1, S), F32), pltpu.VMEM((1, S), F32), pltpu.VMEM((tc, 2 * S), F32)],
        compiler_params=_params(("arbitrary",)),
    )(dh, h, h, ar, ai)


def _sum_rows(name, x):
    def body(x_ref, o_ref):
        o_ref[...] = jnp.sum(x_ref[...], axis=0, keepdims=True)

    return pl.pallas_call(body, name=name, out_shape=jax.ShapeDtypeStruct((1, 1), F32),
                          in_specs=[_full_spec(x.shape)], out_specs=_full_spec((1, 1)), grid=(1,))(x)


ANY = pl.BlockSpec(memory_space=pl.ANY)


def _place():
    return lax.axis_index("x"), lax.axis_index("y"), lax.axis_index("c")


HBM = pl.BlockSpec(memory_space=pltpu.HBM)
SEM = pl.BlockSpec(memory_space=pltpu.SEMAPHORE)
DATAFLOW = pltpu.SideEffectType.DATAFLOW_SIDE_EFFECTING


def _hbm(a):
    return pltpu.with_memory_space_constraint(a, pltpu.HBM)


def _split_start(name, srcs, lands, ncopies, plan, after):
    ns, nl = len(srcs), len(lands)

    def body(*refs):
        send_sems, recv_sems = refs[ns + nl + 1], refs[ns + nl + 2]
        token = refs[-1]
        for k, (src, dst, peer, _) in enumerate(plan(refs[:ns], refs[ns:ns + nl])):
            pltpu.make_async_remote_copy(src_ref=src, dst_ref=dst, send_sem=send_sems.at[k], recv_sem=recv_sems.at[k],
                                         device_id=peer, device_id_type=MESH).start()
        token[...] = jnp.zeros_like(token)

    out = pl.pallas_call(
        body, name=name,
        out_shape=(pltpu.SemaphoreType.DMA((ncopies,)), pltpu.SemaphoreType.DMA((ncopies,)),
                   *[pltpu.HBM(a.shape, a.dtype) for a in srcs], *[pltpu.HBM(a.shape, a.dtype) for a in lands],
                   jax.ShapeDtypeStruct((8, LANES), F32)),
        in_specs=[HBM] * (ns + nl) + [ANY],
        out_specs=(SEM, SEM, *[HBM] * (ns + nl), pl.BlockSpec(memory_space=pltpu.VMEM)),
        input_output_aliases={i: 2 + i for i in range(ns + nl)},
        compiler_params=pltpu.CompilerParams(has_side_effects=DATAFLOW),
    )(*[_hbm(a) for a in srcs], *[_hbm(a) for a in lands], after)
    return out[0], out[1], list(out[2:2 + ns]), list(out[2 + ns:2 + ns + nl]), out[-1]


def _split_wait(name, handle, ncopies, plan, after):
    send_sems, recv_sems, srcs, lands, _ = handle
    ns, nl = len(srcs), len(lands)

    def body(*refs):
        s_sems, r_sems = refs[ns + nl], refs[ns + nl + 1]
        for k, (src, dst, peer, mine) in enumerate(plan(refs[:ns], refs[ns:ns + nl])):
            pltpu.make_async_remote_copy(src_ref=src, dst_ref=dst, send_sem=s_sems.at[k], recv_sem=r_sems.at[k],
                                         device_id=peer, device_id_type=MESH).wait_send()
            pltpu.make_async_remote_copy(src_ref=src, dst_ref=mine, send_sem=s_sems.at[k], recv_sem=r_sems.at[k],
                                         device_id=peer, device_id_type=MESH).wait_recv()

    out = pl.pallas_call(
        body, name=name,
        out_shape=(*[pltpu.HBM(a.shape, a.dtype) for a in srcs], *[pltpu.HBM(a.shape, a.dtype) for a in lands]),
        in_specs=[HBM] * (ns + nl) + [SEM, SEM, ANY],
        out_specs=tuple([HBM] * (ns + nl)),
        input_output_aliases={i: i for i in range(ns + nl)},
        compiler_params=pltpu.CompilerParams(has_side_effects=DATAFLOW),
    )(*srcs, *lands, send_sems, recv_sems, after)
    return list(out[:ns]), list(out[ns:])


def _slot(px, py, pc):
    return 4 * px + 2 * py + pc


def _gather_plan(xs, lands):
    x, y, c = _place()
    peers = [(x, y, 1 - c), (1 - x, y, c), (x, 1 - y, c), (1 - x, 1 - y, c)]
    return [(xs[t], lands[t].at[_slot(x, y, c)], peer, lands[t].at[_slot(*peer)])
            for t in range(len(xs)) for peer in peers]


def _gather_start(name, shards, after):
    lands = [lax.empty((N_DEV,) + s.shape, s.dtype) for s in shards]
    return _split_start(name, shards, lands, 4 * len(shards), _gather_plan, after)


def _pass_on_plan(_, lands):
    x, y, c = _place()
    blocks = [((1 - x, y, c), (1 - x, y, 1 - c)), ((x, 1 - y, c), (x, 1 - y, 1 - c)),
              ((1 - x, 1 - y, c), (1 - x, 1 - y, 1 - c)), ((x, y, 1 - c), (x, y, c))]
    return [(lands[t].at[_slot(*out)], lands[t].at[_slot(*out)], (x, y, 1 - c), lands[t].at[_slot(*back)])
            for t in range(len(lands)) for out, back in blocks]


def _gather_pass_on(name, handle, after):
    n = len(handle[2])
    _, lands = _split_wait(name + "_wait", handle, 4 * n, _gather_plan, after)
    return _split_start(name + "_pass_on", [], lands, 4 * n, _pass_on_plan, after)


def _gather_finish(name, handle, after):
    n = len(handle[3])
    _, lands = _split_wait(name + "_done", handle, 4 * n, _pass_on_plan, after)
    return lands


def _sibling_plan(ps, gots):
    x, y, c = _place()
    return [(ps[t].at[j, 1 - c], gots[t].at[j], (x, y, 1 - c), gots[t].at[j]) for t in range(len(ps)) for j in range(4)]


def _chips_plan(ss, gots):
    x, y, c = _place()
    chips = [(1 - x, y), (x, 1 - y), (1 - x, 1 - y)]
    return [(ss[t].at[2 * px + py], gots[t].at[k], (px, py, c), gots[t].at[k])
            for t in range(len(ss)) for k, (px, py) in enumerate(chips)]


BF16_ROWS = 16


def _shard_tiles(R, C, block_bytes):
    tr = _tile(R, max(BF16_ROWS, min(1024, block_bytes // (C * 4))), BF16_ROWS)
    tc = C if tr * C * 4 <= 2 * block_bytes else _tile(C, max(LANES, block_bytes // (tr * 4)))
    return tr, tc


def _pair_sum(name, p4, got, core):
    _, _, R, C = p4.shape
    tr, tc = _shard_tiles(R, C, ROW_BLOCK_BYTES)

    def body(core_ref, p_ref, g_ref, o_ref):
        o_ref[...] = (p_ref[:, 0] + g_ref[...]).astype(o_ref.dtype)

    return pl.pallas_call(
        body, name=name, out_shape=jax.ShapeDtypeStruct((4, R, C), BF16),
        grid_spec=pltpu.PrefetchScalarGridSpec(
            num_scalar_prefetch=1, grid=(4, R // tr, C // tc),
            in_specs=[pl.BlockSpec((1, 1, tr, tc), lambda j, i, k, core: (j, core[0], i, k)),
                      pl.BlockSpec((1, tr, tc), lambda j, i, k, core: (j, i, k))],
            out_specs=pl.BlockSpec((1, tr, tc), lambda j, i, k, core: (j, i, k))),
        compiler_params=_params(("parallel", "parallel", "parallel")),
    )(core, p4, got)


def _sum_devices(name, g8):
    _, R, C = g8.shape
    tr = _tile(R, SMALL_ROW_ALIGN, 8)

    def body(g_ref, o_ref):
        acc = g_ref[0]
        for d in range(1, N_DEV):
            acc = acc + g_ref[d]
        o_ref[...] = acc

    return pl.pallas_call(
        body, name=name, out_shape=jax.ShapeDtypeStruct((R, C), F32), grid=(R // tr,),
        in_specs=[pl.BlockSpec((N_DEV, tr, C), lambda i: (0, i, 0))], out_specs=pl.BlockSpec((tr, C), lambda i: (i, 0)),
        compiler_params=_params(("parallel",)),
    )(g8)


def _adamw_shard(name, layer, w, m, v, s4, got, chip, bufs):
    _, R, C = w.shape
    tr, tc = _shard_tiles(R, C, ROW_BLOCK_BYTES // 2)

    def body(chip_ref, w_ref, m_ref, v_ref, s_ref, g_ref, b0, b1, b2, b3, og, od, om, ov):
        f = lambda a: a.astype(F32)
        g = ((f(s_ref[0]) + f(g_ref[0])) + f(g_ref[1])) + f(g_ref[2])
        d, nm, nv = _adamw_fn(w_ref[0], g, m_ref[0], v_ref[0])
        og[0], od[0], om[0], ov[0] = g, d, nm, nv

    lay = pl.BlockSpec((1, tr, tc), lambda i, k, chip: (layer, i, k))
    if bufs is None:
        bufs = [lax.empty(w.shape, F32) for _ in range(4)]
    return pl.pallas_call(
        body, name=name, out_shape=[jax.ShapeDtypeStruct(w.shape, F32)] * 4,
        grid_spec=pltpu.PrefetchScalarGridSpec(
            num_scalar_prefetch=1, grid=(R // tr, C // tc),
            in_specs=[lay, lay, lay, pl.BlockSpec((1, tr, tc), lambda i, k, chip: (chip[0], i, k)),
                      pl.BlockSpec((3, tr, tc), lambda i, k, chip: (0, i, k)), ANY, ANY, ANY, ANY],
            out_specs=[lay] * 4),
        input_output_aliases={6: 0, 7: 1, 8: 2, 9: 3},
        compiler_params=_params(("parallel", "parallel")),
    )(chip, w, m, v, s4, got, *bufs)


def _adamw(name, wt, g, m, v):
    shape = wt.shape
    two = (lambda a: a.reshape(1, -1)) if wt.ndim == 1 else (lambda a: a.reshape(-1, shape[-1]))
    w2, g2, m2, v2 = two(wt), two(g), two(m), two(v)
    tr = _row(w2, None, 0)[2]
    outs = [(w2.shape, F32, 0, tr)] * 3
    d, nm, nv = _rows_fwd(name, _adamw_fn, [_row(a, tr, 0) for a in (w2, g2, m2, v2)], [], outs)
    return d.reshape(shape), nm.reshape(shape), nv.reshape(shape)


def _lane_tiling():
    return (jnp.arange(SSM_N)[:, None] == (jnp.arange(SSM_S) % SSM_N)[None, :]).astype(BF16)


def _own_block():
    r = lax.broadcasted_iota(jnp.int32, (SSM_G * SSM_P, SSM_S), 0) // SSM_P
    c = lax.broadcasted_iota(jnp.int32, (SSM_G * SSM_P, SSM_S), 1) // SSM_N
    return r == c


def _bd_build(name, v_re, v_im, sign, tiling):
    def body(r_ref, i_ref, t_ref, o_ref):
        own = _own_block()
        o_ref[:, :SSM_S] = jnp.where(own, _dot(r_ref[...].astype(BF16), t_ref[...]), 0.0).astype(BF16)
        o_ref[:, SSM_S:] = jnp.where(own, sign * _dot(i_ref[...].astype(BF16), t_ref[...]), 0.0).astype(BF16)

    rows = SSM_G * SSM_P
    return pl.pallas_call(
        body, name=name, out_shape=jax.ShapeDtypeStruct((rows, 2 * SSM_S), BF16), grid=(1,),
        in_specs=[_full_spec((rows, SSM_N))] * 2 + [_full_spec((SSM_N, SSM_S))], out_specs=_full_spec((rows, 2 * SSM_S)),
        compiler_params=_params(("arbitrary",)),
    )(v_re, v_im, tiling)


def _bd_extract(name, m, sign, tiling):
    def body(m_ref, t_ref, r_ref, i_ref):
        own, t = _own_block(), t_ref[...]

        def pick(x):
            x = jnp.where(own, x, 0.0)
            hi = x.astype(BF16)
            rest = x - hi.astype(F32)
            mid = rest.astype(BF16)
            lo = (rest - mid.astype(F32)).astype(BF16)
            return _dot(hi, t, NT) + _dot(mid, t, NT) + _dot(lo, t, NT)

        r_ref[...] = pick(m_ref[:, :SSM_S])
        i_ref[...] = sign * pick(m_ref[:, SSM_S:])

    rows = SSM_G * SSM_P
    return pl.pallas_call(
        body, name=name, out_shape=[jax.ShapeDtypeStruct((rows, SSM_N), F32)] * 2, grid=(1,),
        in_specs=[_full_spec((rows, 2 * SSM_S)), _full_spec((SSM_N, SSM_S))], out_specs=[_full_spec((rows, SSM_N))] * 2,
        compiler_params=_params(("arbitrary",)),
    )(m, tiling)


def _fold(a, dil):
    if dil == 1:
        return a
    return a.reshape((T // dil, dil) + a.shape[1:]).swapaxes(0, 1).reshape(a.shape)


def _unfold(a, dil):
    if dil == 1:
        return a
    return a.reshape((dil, T // dil) + a.shape[1:]).swapaxes(0, 1).reshape(a.shape)


DILS = (1, 4, 16)


def _fold3(parts):
    parts = [parts] * 3 if not isinstance(parts, (list, tuple)) else parts
    return jnp.stack([_fold(a, d) for a, d in zip(parts, DILS)])


def _unfold3(a):
    return [_unfold(a[p], d) for p, d in enumerate(DILS)]


def _rope_tables():
    half = ROPE // 2
    inv_freq = ROPE_THETA ** (-jnp.arange(half, dtype=F32) / half)
    ang = jnp.arange(T).astype(F32)[:, None] * inv_freq[None, :]
    i, j = jnp.arange(ROPE)[:, None], jnp.arange(ROPE)[None, :]
    rot = jnp.where(i == j + half, -1.0, jnp.where(i + half == j, 1.0, 0.0)).astype(F32)
    return jnp.tile(jnp.cos(ang), (1, 2)), jnp.tile(jnp.sin(ang), (1, 2)), rot


def _rot_half(x, rot):
    return _dot(x.reshape(-1, ROPE), rot, prec=HI).reshape(x.shape)


def _mla_pack_fn(q, kv, k_rope, cos, sin, rot):
    rope = lambda x: x * cos + _rot_half(x, rot) * sin
    q_out = jnp.concatenate([q[:, :, :NOPE], rope(q[:, :, NOPE:])], axis=-1)
    k_pe = jnp.broadcast_to(rope(k_rope)[None], (H_MLA,) + k_rope.shape)
    return q_out, jnp.concatenate([kv[:, :, :NOPE], k_pe], axis=-1), kv[:, :, NOPE:]


def _mla_unpack_fn(dq, dk, dv, cos, sin, rot):
    unrope = lambda g: g * cos - _rot_half(g * sin, rot)
    dq_out = jnp.concatenate([dq[:, :, :NOPE], unrope(dq[:, :, NOPE:])], axis=-1)
    dk_rope = unrope(jnp.sum(dk[:, :, NOPE:], axis=0))
    return dq_out, jnp.concatenate([dk[:, :, :NOPE], dv], axis=-1), dk_rope


def _flat(w8):
    return w8.reshape(-1, w8.shape[-1])


def _blocks(m):
    return m.reshape(N_DEV, -1, m.shape[-1])


def _behind(a, tok):
    return a if tok is None else a + tok


def _mixer_fwd_in(x, w, sp, rope):
    s = {}
    s['x'] = x
    h = _rms_fwd("rms_mix", x, sp['g_mix'])
    proj = _mm("mm_in", h, _flat(w['w_in']), 'nt')
    offs = np.cumsum((0,) + IN_SPLITS)
    c_q, c_kv, k_rope, u, qd, kd, vd = [proj[:, offs[i]:offs[i + 1]] for i in range(7)]
    s.update(h=h, c_q=c_q, c_kv=c_kv, u=u)

    cqn = _rms_fwd("rms_q", c_q, sp['g_q'])
    ckvn = _rms_fwd("rms_kv", c_kv, sp['g_kv'])
    q8 = _mm("mm_uq", cqn, w['w_uq'], 'nt', bb='r', ob='c')
    kv8 = _mm("mm_ukv", ckvn, w['w_ukv'], 'nn', bb='c', ob='c')
    cos, sin, rot = rope
    tr = MLA_TQ
    qh, kh, vh = _rows_fwd(
        "mla_pack", _mla_pack_fn,
        [_row(q8, tr, 1), _row(kv8, tr, 1), _row(k_rope, tr, 0), _row(cos, tr, 0), _row(sin, tr, 0)], [rot],
        [((H_MLA, T, QK), BF16, 1, tr), ((H_MLA, T, QK), BF16, 1, tr), ((H_MLA, T, VDIM), BF16, 1, tr)])
    y_mla, lse_mla = _mla_fwd(qh, kh, vh)
    s.update(cqn=cqn, ckvn=ckvn, qh=qh, kh=kh, vh=vh, lse_mla=lse_mla, y_mla=y_mla, qd=qd, kd=kd, vd=vd)
    return s


def _mixer_fwd_out(s, w, sp, tok=None, after_ssm=None):
    x, u, y_mla, qd, kd, vd = s['x'], s['u'], s['y_mla'], s['qd'], s['kd'], s['vd']
    a3 = lambda n: sp[n].reshape(SSM_G, 1, SSM_N)
    b2 = lambda n: sp[n].transpose(0, 2, 1).reshape(SSM_G * SSM_P, SSM_N)
    disc_rows = [_row(a3('a_re'), 1, 0), _row(a3('a_im'), 1, 0), _row(sp['log_dt'].reshape(SSM_G, 1, 1), 1, 0),
                 _row(b2('b_re'), SSM_P, 0), _row(b2('b_im'), SSM_P, 0)]
    abr, abi, bbr, bbi = _rows_fwd(
        "s5_disc", _s5_disc_fn, disc_rows, [],
        [((SSM_G, 1, SSM_N), F32, 0, 1), ((SSM_G, 1, SSM_N), F32, 0, 1),
         ((SSM_G * SSM_P, SSM_N), F32, 0, SSM_P), ((SSM_G * SSM_P, SSM_N), F32, 0, SSM_P)])
    ar, ai = abr.reshape(1, SSM_S), abi.reshape(1, SSM_S)
    tiling = _lane_tiling()
    b_mat = _bd_build("s5_b_matrix", bbr, bbi, 1.0, tiling)
    c2 = lambda n: sp[n].reshape(SSM_G * SSM_P, SSM_N)
    c_mat = _bd_build("s5_c_matrix", c2('c_re'), c2('c_im'), -1.0, tiling)
    u16 = _behind(u, tok).astype(BF16)
    bu = _mm("mm_s5_b", u16, b_mat, 'nn')
    hst, hst16 = _scan_fwd(bu, ar, ai)
    ymm = _mm("mm_s5_c", hst16, c_mat, 'nt')
    d_row = sp['d_skip'].reshape(1, SSM_W)
    (yg,) = _rows_fwd("s5_act", _s5_act_fn, [_row(ymm), _row(u)], [d_row], [((T, SSM_W), BF16, -2, _row(u)[2])])
    z = _mm("mm_glu", yg, w['w_glu'], 'nn', bb='c')
    glu_rows = [_row(z[:, :SSM_W]), _row(z[:, SSM_W:])]
    glu_b = [sp['b_glu'][:SSM_W].reshape(1, -1), sp['b_glu'][SSM_W:].reshape(1, -1)]
    (y_ssm,) = _rows_fwd("s5_glu", _glu_fn, glu_rows, glu_b, [((T, SSM_W), F32, -2, glu_rows[0][2])])
    if after_ssm is not None:
        qd = _behind(qd, after_ssm(y_ssm))
    s.update(disc_rows=disc_rows, ar=ar, ai=ai, b_mat=b_mat, c_mat=c_mat, hst=hst, hst16=hst16, u16=u16, ymm=ymm,
             d_row=d_row, yg=yg,
             glu_rows=glu_rows, glu_b=glu_b)

    qf, kf, vf = [_fold3(a).astype(BF16) for a in (qd, kd, vd)]
    o_f, lse_f = _band_fwd(qf, kf, vf)
    mix_rows = [_row(a) for a in _unfold3(o_f) + _unfold3(lse_f)]
    (y_dil,) = _rows_fwd("dil_mix", _dil_mix_fn, mix_rows, [], [((T, DIL_W), F32, -2, mix_rows[0][2])])
    s.update(qf=qf, kf=kf, vf=vf, o_f=o_f, lse_f=lse_f, mix_rows=mix_rows)

    gm, gs, gd = sp['g_out_mla'].reshape(1, -1), sp['g_out_ssm'].reshape(1, -1), sp['g_out_dil'].reshape(1, -1)
    on_rows = [_row(y_mla), _row(y_ssm), _row(y_dil)]
    (ycat,) = _rows_fwd("out_norm", _outnorm_fn, on_rows, [gm, gs, gd], [((T, D), BF16, -2, on_rows[0][2])])
    x1_ = _mm("mm_o", ycat, _flat(w['w_o']), 'nn', res=x)
    s.update(on_rows=on_rows, on_g=[gm, gs, gd], ycat=ycat, x1=x1_)
    for k in ('qd', 'kd', 'vd'):
        del s[k]
    return x1_


def _ffn_fwd(x1_, w, sp, s, tok=None):
    h2 = _rms_fwd("rms_ffn", x1_, _behind(sp['g_ffn'], tok))
    ga = _mm("mm_gate", h2, _flat(w['w_gate']), 'nt')
    gb = _mm("mm_up", h2, _flat(w['w_up']), 'nt')
    ffn_rows = [_row(ga), _row(gb)]
    (zf,) = _rows_fwd("swiglu", _swiglu_fn, ffn_rows, [], [(ga.shape, BF16, -2, ffn_rows[0][2])])
    x2_ = _mm("mm_down", zf, _flat(w['w_down']), 'nn', res=x1_)
    s.update(h2=h2, ffn_rows=ffn_rows, zf=zf)
    return x2_


def _b16(a):
    return a.astype(BF16)


def _ffn_bwd(dx2, s, w, sp, tok=None):
    gw, gs_ = {}, {}
    b16 = _b16
    dx2b = b16(_behind(dx2, tok))
    dzf = _mm("mm_down_dx", dx2b, _flat(w['w_down']), 'nt')
    gw['w_down'] = _blocks(_mm("mm_down_dw", s['zf'], dx2b, 'tn'))
    dga, dgb = _rows_vjp("swiglu_bwd", _swiglu_fn, s['ffn_rows'], [], [], [_row(dzf)], grad_dtypes=[BF16, BF16])
    gw['w_gate'] = _blocks(_mm("mm_gate_dw", dga, s['h2'], 'tn'))
    gw['w_up'] = _blocks(_mm("mm_up_dw", dgb, s['h2'], 'tn'))
    dh2 = _mm("mm_up_dx", dgb, _flat(w['w_up']), 'nn', res=_mm("mm_gate_dx", dga, _flat(w['w_gate']), 'nn'))
    dx1, gs_['g_ffn'] = _rms_bwd("rms_ffn_bwd", s['x1'], sp['g_ffn'], dh2, dx2)
    return dx1, gw, gs_


def _mixer_bwd_out(dx1, s, w, sp, tok=None):
    gw, gs_ = {}, {}
    b16 = _b16
    dx1b = b16(_behind(dx1, tok))
    dycat = _mm("mm_o_dx", dx1b, _flat(w['w_o']), 'nt')
    gw['w_o'] = _blocks(_mm("mm_o_dw", s['ycat'], dx1b, 'tn'))
    dy_mla, dy_ssm, dy_dil, gs_['g_out_mla'], gs_['g_out_ssm'], gs_['g_out_dil'] = _rows_vjp(
        "out_norm_bwd", _outnorm_fn, s['on_rows'], s['on_g'], [], [_row(dycat)])

    dmix = _rows_vjp("dil_mix_bwd", _dil_mix_fn, s['mix_rows'], [], [], [_row(dy_dil)])
    dqf, dkf, dvf = _band_bwd(s['qf'], s['kf'], s['vf'], s['o_f'], s['lse_f'], _fold3(dmix[:3]), _fold3(dmix[3:]))
    back = lambda a: sum(_unfold3(a))
    dqd, dkd, dvd = back(dqf), back(dkf), back(dvf)

    dz1, dz2, db1, db2 = _rows_vjp("s5_glu_bwd", _glu_fn, s['glu_rows'], s['glu_b'], [], [_row(dy_ssm)])
    gs_['b_glu'] = jnp.concatenate([db1, db2], axis=1)
    dzb = b16(jnp.concatenate([dz1, dz2], axis=1))
    dyg = _mm("mm_glu_dx", dzb, w['w_glu'], 'nt', bb='c')
    gw['w_glu'] = _mm("mm_glu_dw", s['yg'], dzb, 'tn', ob='c')
    dymm, du_act, dd = _rows_vjp("s5_act_bwd", _s5_act_fn, [_row(s['ymm']), _row(s['u'])], [s['d_row']], [], [_row(dyg)],
                                 grad_dtypes=[BF16, F32])
    gs_['d_skip'] = dd
    dhst = _mm("mm_s5_c_dx", dymm, s['c_mat'], 'nn')
    dc_mat = _mm("mm_s5_c_dw", dymm, s['hst16'], 'tn')
    g, dar, dai = _scan_bwd(dhst, s['hst'], s['ar'], s['ai'])
    du = _mm("mm_s5_b_dx", g, s['b_mat'], 'nt', res=du_act)
    db_mat = _mm("mm_s5_b_dw", s['u16'], g, 'tn')
    tiling = _lane_tiling()
    gs_['c_re'], gs_['c_im'] = _bd_extract("s5_c_blocks", dc_mat, -1.0, tiling)
    dbbr, dbbi = _bd_extract("s5_b_blocks", db_mat, 1.0, tiling)
    disc_cts = [_row(dar.reshape(SSM_G, 1, SSM_N), 1, 0), _row(dai.reshape(SSM_G, 1, SSM_N), 1, 0),
                _row(dbbr, SSM_P, 0), _row(dbbi, SSM_P, 0)]
    da_re, da_im, dldt, db_r, db_i = _rows_vjp("s5_disc_bwd", _s5_disc_fn, s['disc_rows'], [], [], disc_cts)
    gs_['a_re'], gs_['a_im'], gs_['log_dt'] = da_re, da_im, dldt
    unb = lambda a: a.reshape(SSM_G, SSM_P, SSM_N).transpose(0, 2, 1)
    gs_['b_re'], gs_['b_im'] = unb(db_r), unb(db_i)
    return (dy_mla, du, dqd, dkd, dvd), gw, gs_


def _mixer_bwd_in(cts, dx1, s, w, sp, rope, tok=None):
    gw, gs_ = {}, {}
    b16 = _b16
    dy_mla, du, dqd, dkd, dvd = cts
    dqh, dkh, dvh = _mla_bwd(s['qh'], s['kh'], s['vh'], s['y_mla'], dy_mla, _behind(s['lse_mla'], tok))
    cos, sin, rot = rope
    tr = MLA_TQ
    dq8, dkv8, dk_rope = _rows_fwd(
        "mla_unpack", _mla_unpack_fn,
        [_row(dqh, tr, 1), _row(dkh, tr, 1), _row(dvh, tr, 1), _row(cos, tr, 0), _row(sin, tr, 0)], [rot],
        [((H_MLA, T, QK), BF16, 1, tr), ((H_MLA, T, NOPE + VDIM), BF16, 1, tr), ((T, ROPE), F32, 0, tr)])
    dcqn = _mm("mm_uq_dx", dq8, w['w_uq'], 'nn', ab='c', bb='r')
    gw['w_uq'] = _mm("mm_uq_dw", dq8, s['cqn'], 'tn', ab='c', ob='r')
    dckvn = _mm("mm_ukv_dx", dkv8, w['w_ukv'], 'nt', ab='c', bb='c')
    gw['w_ukv'] = _mm("mm_ukv_dw", s['ckvn'], dkv8, 'tn', bb='c', ob='c')
    dc_q, gs_['g_q'] = _rms_bwd("rms_q_bwd", s['c_q'], sp['g_q'], dcqn)
    dc_kv, gs_['g_kv'] = _rms_bwd("rms_kv_bwd", s['c_kv'], sp['g_kv'], dckvn)

    dproj = b16(jnp.concatenate([dc_q, dc_kv, dk_rope, du, dqd, dkd, dvd], axis=1))
    dh = _mm("mm_in_dx", dproj, _flat(w['w_in']), 'nn')
    gw['w_in'] = _blocks(_mm("mm_in_dw", dproj, s['h'], 'tn'))
    dx, gs_['g_mix'] = _rms_bwd("rms_mix_bwd", s['x'], sp['g_mix'], dh, dx1)
    return dx, gw, gs_


def kernel(x, g_mix, w_in, g_q, w_uq, g_kv, w_ukv, a_re, a_im, b_re, b_im, c_re, c_im, d_skip, log_dt, w_glu, b_glu, g_out_mla, g_out_ssm, g_out_dil, w_o, g_ffn, w_gate, w_up, w_down, g_final, loss_target, m_g_mix, m_w_in, m_g_q, m_w_uq, m_g_kv, m_w_ukv, m_a_re, m_a_im, m_b_re, m_b_im, m_c_re, m_c_im, m_d_skip, m_log_dt, m_w_glu, m_b_glu, m_g_out_mla, m_g_out_ssm, m_g_out_dil, m_w_o, m_g_ffn, m_w_gate, m_w_up, m_w_down, m_g_final, v_g_mix, v_w_in, v_g_q, v_w_uq, v_g_kv, v_w_ukv, v_a_re, v_a_im, v_b_re, v_b_im, v_c_re, v_c_im, v_d_skip, v_log_dt, v_w_glu, v_b_glu, v_g_out_mla, v_g_out_ssm, v_g_out_dil, v_w_o, v_g_ffn, v_w_gate, v_w_up, v_w_down, v_g_final):
    W = dict(zip(PARAMS, (g_mix, w_in, g_q, w_uq, g_kv, w_ukv, a_re, a_im, b_re, b_im, c_re, c_im, d_skip, log_dt,
                          w_glu, b_glu, g_out_mla, g_out_ssm, g_out_dil, w_o, g_ffn, w_gate, w_up, w_down, g_final)))
    M = dict(zip(PARAMS, (m_g_mix, m_w_in, m_g_q, m_w_uq, m_g_kv, m_w_ukv, m_a_re, m_a_im, m_b_re, m_b_im, m_c_re,
                          m_c_im, m_d_skip, m_log_dt, m_w_glu, m_b_glu, m_g_out_mla, m_g_out_ssm, m_g_out_dil, m_w_o,
                          m_g_ffn, m_w_gate, m_w_up, m_w_down, m_g_final)))
    V = dict(zip(PARAMS, (v_g_mix, v_w_in, v_g_q, v_w_uq, v_g_kv, v_w_ukv, v_a_re, v_a_im, v_b_re, v_b_im, v_c_re,
                          v_c_im, v_d_skip, v_log_dt, v_w_glu, v_b_glu, v_g_out_mla, v_g_out_ssm, v_g_out_dil, v_w_o,
                          v_g_ffn, v_w_gate, v_w_up, v_w_down, v_g_final)))
    cx, cy, cc = _place()
    core = cc.astype(jnp.int32).reshape(1)
    chip = (2 * cx + cy).astype(jnp.int32).reshape(1)
    rope = _rope_tables()
    small = [{n: W[n][l] for n in SMALL} for l in range(DEPTH)]
    for sp in small:
        for n in ('g_mix', 'g_q', 'g_kv', 'g_ffn'):
            sp[n] = sp[n].reshape(1, -1)

    def tok_of(tokens):
        return sum(t[0, 0] for t in tokens) if tokens else None

    def shard_view(a, n):
        return a.swapaxes(1, 2) if BIG[n] == 't' else a

    def gather_start(l, group, names, after):
        return _gather_start(f"gather_{group}_start_{l}", [shard_view(W[n], n)[l].astype(BF16) for n in names], after)

    xa = x[0]
    h1_mix = gather_start(0, "mix", MIXER_W, jnp.zeros((8, LANES), F32))
    h1_ffn = gather_start(0, "ffn", FFN_W, h1_mix[4])
    h2_mix = _gather_pass_on("gather_mix_0", h1_mix, xa)
    saved, full = [], []
    tokens = [h2_mix[4], h1_ffn[4]]
    for l in range(DEPTH):
        last = l + 1 == DEPTH
        wm = dict(zip(MIXER_W, _gather_finish(f"gather_mix_{l}", h2_mix, xa)))
        sp = dict(small[l])
        sp['g_mix'] = _behind(sp['g_mix'], tok_of(tokens))
        s = _mixer_fwd_in(xa, wm, sp, rope)
        tokens = []
        first_ffn = {}
        if l == 0:
            h1_first = h1_ffn
            def mid(dep):
                first_ffn['h'] = _gather_pass_on("gather_ffn_0", h1_first, dep)
                return first_ffn['h'][4][0, 0]
        else:
            mid = None
        if not last:
            h1_mix = gather_start(l + 1, "mix", MIXER_W, s['y_mla'])
            h1_ffn = gather_start(l + 1, "ffn", FFN_W, h1_mix[4])
            tokens += [h1_mix[4], h1_ffn[4]]
        x1 = _mixer_fwd_out(s, wm, small[l], tok_of(tokens), mid)
        if l == 0:
            h2_ffn = first_ffn['h']
        tokens = []
        wf = dict(zip(FFN_W, _gather_finish(f"gather_ffn_{l}", h2_ffn, x1)))
        if not last:
            h2_mix = _gather_pass_on(f"gather_mix_{l + 1}", h1_mix, x1)
            tokens.append(h2_mix[4])
        xa = _ffn_fwd(x1, wf, small[l], s, tok_of(tokens))
        tokens = []
        if not last:
            h2_ffn = _gather_pass_on(f"gather_ffn_{l + 1}", h1_ffn, xa)
            tokens.append(h2_ffn[4])
        saved.append(s)
        full.append({**wm, **wf})
    gf = g_final.reshape(1, D)
    ones = jnp.ones((T, 1), F32)
    dxa, dgf, loss_rows = _rows_vjp("loss", _loss_fn, [_row(xa)], [gf], [_row(loss_target[0])], [_row(ones)],
                                    primal=True)
    loss_here = _sum_rows("loss_sum", loss_rows)[0, 0]

    bufs = {n: None for n in BIG}
    pending = []

    def advance(dep):
        tokens = []
        for g in pending:
            names, tag = g['names'], g['tag']
            if g['stage'] == 0:
                p4 = [a.reshape((4, 2) + a.shape[1:]) for a in g['gw']]
                gots = [lax.empty((4,) + a.shape[1:], F32) for a in g['gw']]
                g['h'] = _split_start("rs_sibling_start_" + tag, p4, gots, 4 * len(p4), _sibling_plan, dep)
                tokens.append(g['h'][4])
            elif g['stage'] == 1:
                p4, gots = _split_wait("rs_sibling_wait_" + tag, g['h'], 4 * len(names), _sibling_plan, dep)
                s4 = [_pair_sum("rs_pair_sum_" + n, p, q, core) for n, p, q in zip(names, p4, gots)]
                gots = [lax.empty((3,) + a.shape[1:], a.dtype) for a in s4]
                g['h'] = _split_start("rs_chips_start_" + tag, s4, gots, 3 * len(s4), _chips_plan, dep)
                tokens.append(g['h'][4])
            elif g['stage'] == 3:
                s4, gots = _split_wait("rs_chips_wait_" + tag, g['h'], 3 * len(names), _chips_plan, dep)
                for n, s4n, got in zip(names, s4, gots):
                    bufs[n] = _adamw_shard("adamw_" + n, g['layer'], shard_view(W[n], n), shard_view(M[n], n),
                                           shard_view(V[n], n), s4n, got, chip, bufs[n])
            g['stage'] += 1
        pending[:] = [g for g in pending if g['stage'] < 4]
        return tokens

    def group(names, l, gw, kind):
        return dict(names=names, layer=l, gw=[gw[n] for n in names], stage=0, tag=f"{kind}_{l}")

    g_small = [None] * DEPTH
    tokens = []
    for l in reversed(range(DEPTH)):
        dx1, gw_f, gs_f = _ffn_bwd(dxa, saved[l], full[l], small[l], tok_of(tokens))
        pending.append(group(FFN_W, l, gw_f, "ffn"))
        tokens = advance(dx1)
        cts, gw_o, gs_o = _mixer_bwd_out(dx1, saved[l], full[l], small[l], tok_of(tokens))
        pending.append(group(OUT_W, l, gw_o, "out"))
        tokens = advance(cts[0])
        dxa, gw_i, gs_i = _mixer_bwd_in(cts, dx1, saved[l], full[l], small[l], rope, tok_of(tokens))
        pending.append(group(IN_W, l, gw_i, "in"))
        tokens = advance(dxa)
        g_small[l] = {**gs_f, **gs_o, **gs_i}

    flat = [g_small[l][n].reshape(-1) for l in range(DEPTH) for n in SMALL] + [dgf.reshape(-1), loss_here.reshape(1)]
    n_small = sum(int(f.shape[0]) for f in flat)
    rows = -(-n_small // (PACK_C * SMALL_ROW_ALIGN)) * SMALL_ROW_ALIGN
    flat = jnp.concatenate(flat + [jnp.zeros((rows * PACK_C - n_small,), F32)]).reshape(rows, PACK_C)
    h_small = _gather_start("gather_small_start", [flat], dxa)
    advance(h_small[4])
    h_small = _gather_pass_on("gather_small", h_small, flat)
    tokens = advance(h_small[4])
    (gathered,) = _gather_finish("gather_small", h_small, flat)
    tot = _behind(_sum_devices("small_sum", gathered).reshape(-1), tok_of(tokens))
    grads, off = {}, 0
    per_layer = {n: [] for n in SMALL}
    for l in range(DEPTH):
        for n, shp in SMALL.items():
            k = int(np.prod(shp))
            per_layer[n].append(tot[off:off + k].reshape(shp))
            off += k
    for n in SMALL:
        grads[n] = jnp.stack(per_layer[n])
    grads['g_final'] = tot[off:off + D]
    loss = tot[off + D]

    delta, new_m, new_v = {}, {}, {}
    for n in PARAMS:
        if n not in BIG:
            delta[n], new_m[n], new_v[n] = _adamw("adamw_" + n, W[n], grads[n], M[n], V[n])
    while pending:
        advance(delta['g_final'])
    for n in BIG:
        grads[n], delta[n], new_m[n], new_v[n] = [shard_view(b, n) for b in bufs[n]]
    return (loss, dxa[None], *[grads[n] for n in PARAMS], *[delta[n] for n in PARAMS],
            *[new_m[n] for n in PARAMS], *[new_v[n] for n in PARAMS])
```

```python
import jax
import jax.numpy as jnp
import numpy as np
from jax import lax
from jax.experimental import pallas as pl
from jax.experimental.pallas import tpu as pltpu

F32 = jnp.float32
BF16 = jnp.bfloat16

T = 2048
D = 2048
DEPTH = 4
N_DEV = 8
H_MLA, NOPE, ROPE, VDIM = 8, 128, 64, 128
QK = NOPE + ROPE
Q_LORA, KV_LORA = 512, 256
SSM_W, SSM_G, SSM_P, SSM_N = 512, 32, 16, 64
SSM_S = SSM_G * SSM_N
DIL_W, DIL_H, DIL_D = 512, 8, 64
BLK = 128
IN_SPLITS = (Q_LORA, KV_LORA, ROPE, SSM_W, DIL_W, DIL_W, DIL_W)
IN_W = sum(IN_SPLITS)
D_FF = 5632
EPS = 1e-6
ROPE_THETA = 10000.0
MLA_SCALE = QK ** -0.5
DIL_SCALE = DIL_D ** -0.5

ADAM_LR, ADAM_B1, ADAM_B2, ADAM_EPS, ADAM_WD, ADAM_STEP = 0.001, 0.9, 0.999, 1e-08, 0.01, 10

VMEM_LIMIT_V7X = 52 * 1024 * 1024
LANES = 128
PACK_C = 1024
ROW_BLOCK_BYTES = 2 * 1024 * 1024
MM_TM, MM_TN, MM_TK = 1408, 1024, 5632
MM_DEEP = IN_W
MM_TB = 512

NT = (((1,), (1,)), ((), ()))
TN = (((0,), (0,)), ((), ()))
H_QK = (((2,), (2,)), ((0,), (0,)))
H_PV = (((2,), (1,)), ((0,), (0,)))
H_TN = (((1,), (1,)), ((0,), (0,)))
HI = lax.Precision.HIGHEST
MESH = pl.DeviceIdType.MESH

PARAMS = ['g_mix', 'w_in', 'g_q', 'w_uq', 'g_kv', 'w_ukv', 'a_re', 'a_im', 'b_re', 'b_im', 'c_re', 'c_im',
          'd_skip', 'log_dt', 'w_glu', 'b_glu', 'g_out_mla', 'g_out_ssm', 'g_out_dil', 'w_o', 'g_ffn',
          'w_gate', 'w_up', 'w_down', 'g_final']
BIG = {'w_in': 't', 'w_uq': 't', 'w_ukv': 'c', 'w_glu': 'c', 'w_o': 'r', 'w_gate': 't', 'w_up': 't', 'w_down': 'r'}
MIXER_W = ['w_in', 'w_uq', 'w_ukv', 'w_glu', 'w_o']
FFN_W = ['w_gate', 'w_up', 'w_down']
OUT_W, IN_W = ['w_o', 'w_glu'], ['w_in', 'w_uq', 'w_ukv']
SMALL = {'g_mix': (D,), 'g_q': (Q_LORA,), 'g_kv': (KV_LORA,), 'a_re': (SSM_G, SSM_N), 'a_im': (SSM_G, SSM_N),
         'b_re': (SSM_G, SSM_N, SSM_P), 'b_im': (SSM_G, SSM_N, SSM_P), 'c_re': (SSM_G, SSM_P, SSM_N),
         'c_im': (SSM_G, SSM_P, SSM_N), 'd_skip': (SSM_G, SSM_P), 'log_dt': (SSM_G,), 'b_glu': (2 * SSM_W,),
         'g_out_mla': (H_MLA * VDIM,), 'g_out_ssm': (SSM_W,), 'g_out_dil': (DIL_W,), 'g_ffn': (D,)}
SMALL_ROW_ALIGN = 64


def _tile(dim, target, align=LANES):
    best = None
    for t in range(align, min(dim, target) + 1, align):
        if dim % t == 0:
            best = t
    return best if best is not None else dim


def _params(sem=None):
    return pltpu.CompilerParams(dimension_semantics=sem, vmem_limit_bytes=VMEM_LIMIT_V7X)


def _dot(a, b, dims=None, prec=None):
    if dims is None:
        return jnp.dot(a, b, preferred_element_type=F32, precision=prec)
    return lax.dot_general(a, b, dims, preferred_element_type=F32, precision=prec)


def _mm_spec(shape, blk, t_r, t_c, rc):
    if blk is None:
        return pl.BlockSpec((t_r, t_c), rc)
    _, R, C = shape
    if blk == 'r':
        per = R // t_r
        return pl.BlockSpec((1, t_r, t_c), lambda i, j, k: (rc(i, j, k)[0] // per, rc(i, j, k)[0] % per, rc(i, j, k)[1]))
    per = C // t_c
    return pl.BlockSpec((1, t_r, t_c), lambda i, j, k: (rc(i, j, k)[1] // per, rc(i, j, k)[0], rc(i, j, k)[1] % per))


def _logical(shape, blk):
    if blk is None:
        return tuple(shape)
    G, R, C = shape
    return (G * R, C) if blk == 'r' else (R, G * C)


def _mm(name, a, b, mode, ab=None, bb=None, ob=None, res=None, prec=None, epilogue=None):
    la, lb = _logical(a.shape, ab), _logical(b.shape, bb)
    am, ak = (0, 1) if mode != 'tn' else (1, 0)
    bk, bn = (0, 1) if mode != 'nt' else (1, 0)
    M, K, N = la[am], la[ak], lb[bn]
    assert lb[bk] == K, (name, a.shape, b.shape, mode)
    if ob is None:
        out_shape = (M, N)
    elif ob == 'r':
        G = N_DEV
        out_shape = (G, M // G, N)
    else:
        G = N_DEV
        out_shape = (G, M, N // G)
    em = min(a.shape[-2:][am], out_shape[-2])
    en = min(b.shape[-2:][bn], out_shape[-1])
    ek = min(a.shape[-2:][ak], b.shape[-2:][bk])
    dims = {'nn': None, 'nt': NT, 'tn': TN}[mode]
    a_kb = mode != 'tn' and ab == 'c'
    b_kb = (mode == 'nn' and bb == 'r') or (mode == 'nt' and bb == 'c')
    blocks = K // ek if (a_kb or b_kb) else 1
    assert blocks == 1 or ((a_kb or ab is None) and (b_kb or bb is None)), (name, ab, bb, mode)
    tk = ek if blocks > 1 else _tile(ek, MM_TK)
    nk = 1 if blocks > 1 else K // tk
    small = blocks > 1 or nk > 1 or K > MM_DEEP
    tn = _tile(en, MM_TB if small else MM_TN)
    tm = _tile(em, MM_TB if small else (2 * MM_TM if K <= MM_TB else MM_TM))
    if tn > MM_TN:
        tm = _tile(em, MM_TB)
    if tm > 2 * MM_TM:
        tn = _tile(en, MM_TB)

    def val(ref):
        return ref[...] if len(ref.shape) == 2 else ref[0]

    def put(o_ref, r):
        if len(o_ref.shape) == 2:
            o_ref[...] = r
        else:
            o_ref[0] = r

    def k_block(ref, d, blocked, lanes):
        if blocked:
            return ref[d]
        return ref[:, d * ek:(d + 1) * ek] if lanes else ref[d * ek:(d + 1) * ek, :]

    if epilogue is not None:
        epi_fn, extras, out_dtypes = epilogue
        assert nk == 1 and blocks == 1 and ob is None and res is None, name

    def body(*refs):
        if epilogue is not None:
            a_ref, b_ref = refs[:2]
            extra_refs, out_refs = refs[2:2 + len(extras)], refs[2 + len(extras):]
            outs = epi_fn(_dot(val(a_ref), val(b_ref), dims, prec), *[e[...] for e in extra_refs])
            for o_ref, o in zip(out_refs, outs):
                o_ref[...] = o.astype(o_ref.dtype)
            return
        if res is None:
            a_ref, b_ref, o_ref = refs[:3]
            r_ref = None
        else:
            a_ref, b_ref, r_ref, o_ref = refs[:4]
        if blocks > 1:
            part = None
            for d in range(blocks):
                p = _dot(k_block(a_ref, d, a_kb, True), k_block(b_ref, d, b_kb, mode == 'nt'), dims, prec)
                part = p if part is None else part + p
        else:
            part = _dot(val(a_ref), val(b_ref), dims, prec)
        if nk == 1:
            put(o_ref, part if r_ref is None else part + val(r_ref))
            return
        acc_ref = refs[-1]
        k = pl.program_id(2)

        @pl.when(k == 0)
        def _():
            acc_ref[...] = part

        @pl.when((k > 0) & (k < nk - 1))
        def _():
            acc_ref[...] += part

        @pl.when(k == nk - 1)
        def _():
            r = acc_ref[...] + part
            put(o_ref, r if r_ref is None else r + val(r_ref))

    if blocks > 1:
        G = blocks
        a_spec = (pl.BlockSpec((G, tm, ek), lambda i, j, k: (0, i, 0)) if a_kb
                  else pl.BlockSpec((tm, K), lambda i, j, k: (i, 0)))
        if b_kb:
            b_spec = (pl.BlockSpec((G, ek, tn), lambda i, j, k: (0, 0, j)) if mode == 'nn'
                      else pl.BlockSpec((G, tn, ek), lambda i, j, k: (0, j, 0)))
        else:
            b_spec = (pl.BlockSpec((K, tn), lambda i, j, k: (0, j)) if mode == 'nn'
                      else pl.BlockSpec((tn, K), lambda i, j, k: (j, 0)))
    else:
        if mode == 'tn':
            a_spec = _mm_spec(a.shape, ab, tk, tm, lambda i, j, k: (k, i))
        else:
            a_spec = _mm_spec(a.shape, ab, tm, tk, lambda i, j, k: (i, k))
        if mode == 'nt':
            b_spec = _mm_spec(b.shape, bb, tn, tk, lambda i, j, k: (j, k))
        else:
            b_spec = _mm_spec(b.shape, bb, tk, tn, lambda i, j, k: (k, j))
    o_spec = _mm_spec(out_shape, ob, tm, tn, lambda i, j, k: (i, j))
    if epilogue is not None:
        return pl.pallas_call(
            body, name=name, out_shape=[jax.ShapeDtypeStruct(out_shape, dt) for dt in out_dtypes],
            grid=(M // tm, N // tn, nk), in_specs=[a_spec, b_spec] + [o_spec] * len(extras),
            out_specs=[o_spec] * len(out_dtypes),
            compiler_params=_params(("parallel", "parallel", "arbitrary")),
        )(a, b, *extras)
    in_specs = [a_spec, b_spec] + ([o_spec] if res is not None else [])
    args = (a, b) + ((res,) if res is not None else ())
    return pl.pallas_call(
        body, name=name, out_shape=jax.ShapeDtypeStruct(out_shape, F32),
        grid=(M // tm, N // tn, nk), in_specs=in_specs, out_specs=o_spec,
        scratch_shapes=[pltpu.VMEM((tm, tn), F32)] if nk > 1 else [],
        compiler_params=_params(("parallel", "parallel", "arbitrary")),
    )(*args)


def _row(a, tr=None, axis=-2):
    axis = axis % a.ndim
    n = a.shape[axis]
    if tr is None:
        row_bytes = a.size // n * 4
        tr = _tile(n, max(8, min(256, ROW_BLOCK_BYTES // row_bytes)), 8)
    return (a, axis, tr)


def _row_spec(shape, axis, tr):
    nd = len(shape)
    blk = tuple(tr if d == axis else s for d, s in enumerate(shape))
    return pl.BlockSpec(blk, lambda i: tuple(i if d == axis else 0 for d in range(nd)))


def _full_spec(shape):
    nd = len(shape)
    return pl.BlockSpec(tuple(shape), lambda i: (0,) * nd)


def _steps(entries):
    ns = {a.shape[ax] // tr for a, ax, tr in entries}
    assert len(ns) == 1, [(a.shape, ax, tr) for a, ax, tr in entries]
    return ns.pop()


def _as_tuple(r):
    return tuple(r) if isinstance(r, (tuple, list)) else (r,)


def _rows_fwd(name, fn, rows, bcast, outs):
    steps = _steps(rows)
    nr, nb = len(rows), len(bcast)

    def body(*refs):
        vals = [r[...] for r in refs[:nr + nb]]
        res = _as_tuple(fn(*vals))
        for o_ref, r in zip(refs[nr + nb:], res):
            o_ref[...] = r.astype(o_ref.dtype)

    in_specs = [_row_spec(a.shape, ax, tr) for a, ax, tr in rows] + [_full_spec(b.shape) for b in bcast]
    out_specs = [_row_spec(s, ax % len(s), tr) for s, _, ax, tr in outs]
    res = pl.pallas_call(
        body, name=name, out_shape=[jax.ShapeDtypeStruct(s, dt) for s, dt, _, _ in outs],
        grid=(steps,), in_specs=in_specs, out_specs=out_specs,
        compiler_params=_params(("parallel",)),
    )(*[a for a, _, _ in rows], *bcast)
    return res


def _rows_vjp(name, fn, drows, dbc, arows, cts, primal=False, grad_dtypes=None):
    entries = list(drows) + list(arows) + list(cts)
    steps = _steps(entries)
    ndr, ndb, nar, nct = len(drows), len(dbc), len(arows), len(cts)
    gdt = list(grad_dtypes) if grad_dtypes is not None else [F32] * ndr

    def body(*refs):
        p = 0
        dr = [r[...] for r in refs[p:p + ndr]]; p += ndr
        db = [r[...] for r in refs[p:p + ndb]]; p += ndb
        ar = [r[...] for r in refs[p:p + nar]]; p += nar
        ct = [r[...] for r in refs[p:p + nct]]; p += nct
        g_rows = refs[p:p + ndr]; p += ndr
        g_bc = refs[p:p + ndb]; p += ndb
        prim_refs = refs[p:]

        def f(*d):
            return _as_tuple(fn(*d, *ar))

        outs, pullback = jax.vjp(f, *dr, *db)
        grads = pullback(tuple(c.astype(o.dtype) for c, o in zip(ct, outs)))
        for k in range(ndr):
            g_rows[k][...] = grads[k].astype(g_rows[k].dtype)
        if ndb:
            @pl.when(pl.program_id(0) == 0)
            def _():
                for r in g_bc:
                    r[...] = jnp.zeros_like(r)
            for k in range(ndb):
                g_bc[k][...] += grads[ndr + k]
        for r, o in zip(prim_refs, outs):
            r[...] = o.astype(r.dtype)

    in_specs = ([_row_spec(a.shape, ax, tr) for a, ax, tr in drows] + [_full_spec(b.shape) for b in dbc]
                + [_row_spec(a.shape, ax, tr) for a, ax, tr in arows]
                + [_row_spec(a.shape, ax, tr) for a, ax, tr in cts])
    out_shape = ([jax.ShapeDtypeStruct(a.shape, dt) for (a, _, _), dt in zip(drows, gdt)]
                 + [jax.ShapeDtypeStruct(b.shape, F32) for b in dbc])
    out_specs = ([_row_spec(a.shape, ax, tr) for a, ax, tr in drows] + [_full_spec(b.shape) for b in dbc])
    if primal:
        out_shape += [jax.ShapeDtypeStruct(a.shape, F32) for a, _, _ in cts]
        out_specs += [_row_spec(a.shape, ax, tr) for a, ax, tr in cts]
    return pl.pallas_call(
        body, name=name, out_shape=out_shape, grid=(steps,), in_specs=in_specs, out_specs=out_specs,
        compiler_params=_params(("arbitrary",)),
    )(*[a for a, _, _ in drows], *dbc, *[a for a, _, _ in arows], *[a for a, _, _ in cts])


def _rms_fn(x, g):
    return x * lax.rsqrt(jnp.mean(x * x, axis=-1, keepdims=True) + EPS) * g


def _rms_res_fn(x, g):
    return _rms_fn(x, g), x


def _s5_act_fn(ymm, u, d):
    return jax.nn.gelu(ymm + d * u)


def _glu_fn(z1, z2, b1, b2):
    return (z1 + b1) * jax.nn.sigmoid(z2 + b2)


def _outnorm_fn(ym, ys, yd, gm, gs, gd):
    return jnp.concatenate([_rms_fn(ym, gm), _rms_fn(ys, gs), _rms_fn(yd, gd)], axis=-1)


def _swiglu_fn(a, b):
    return jax.nn.silu(a) * b


def _loss_fn(x, g, tgt):
    err = _rms_fn(x, g) - tgt
    return 0.5 * jnp.mean(err * err, axis=-1, keepdims=True)


def _dil_mix_fn(o0, o1, o2, l0, l1, l2):
    m = jnp.maximum(jnp.maximum(l0, l1), l2)
    e0, e1, e2 = jnp.exp(l0 - m), jnp.exp(l1 - m), jnp.exp(l2 - m)
    s = e0 + e1 + e2
    return (e0 / s) * o0 + (e1 / s) * o1 + (e2 / s) * o2


def _s5_disc_fn(a_re, a_im, ldt, b_r, b_i):
    lr = jnp.minimum(a_re.reshape(1, SSM_N), -1e-4)
    li = a_im.reshape(1, SSM_N)
    dt = jnp.exp(ldt.reshape(1, 1))
    e = jnp.exp(lr * dt)
    ar = e * jnp.cos(li * dt)
    ai = e * jnp.sin(li * dt)
    nr, ni = ar - 1.0, ai
    den = lr * lr + li * li
    cr = (nr * lr + ni * li) / den
    ci = (ni * lr - nr * li) / den
    return ar.reshape(1, 1, SSM_N), ai.reshape(1, 1, SSM_N), cr * b_r - ci * b_i, cr * b_i + ci * b_r


def _adamw_fn(w, g, m, v):
    m = ADAM_B1 * m + (1.0 - ADAM_B1) * g
    v = ADAM_B2 * v + (1.0 - ADAM_B2) * jnp.square(g)
    m_hat = m / (1.0 - ADAM_B1 ** ADAM_STEP)
    v_hat = v / (1.0 - ADAM_B2 ** ADAM_STEP)
    delta = -ADAM_LR * (m_hat / (jnp.sqrt(v_hat) + ADAM_EPS) + ADAM_WD * w)
    return delta, m, v


def _rms_fwd(name, x, g):
    (h,) = _rows_fwd(name, _rms_fn, [_row(x)], [g], [(x.shape, BF16, -2, _row(x)[2])])
    return h


def _rms_bwd(name, x, g, dh, dres=None):
    if dres is None:
        dx, dg = _rows_vjp(name, _rms_fn, [_row(x)], [g], [], [_row(dh)])
    else:
        dx, dg = _rows_vjp(name, _rms_res_fn, [_row(x)], [g], [], [_row(dh), _row(dres)])
    return dx, dg


MLA_TQ = 256
MLA_EXT = 512


def _mla_fwd(q, k, v):
    tq = MLA_TQ

    def body(q_ref, k_ref, v_ref, o_ref, lse_ref):
        i = pl.program_id(1)
        q = q_ref[0]

        def rows_below(ext):
            s = _dot(q, k_ref[0, :ext, :], NT) * MLA_SCALE
            row = i * tq + lax.broadcasted_iota(jnp.int32, (tq, ext), 0)
            col = lax.broadcasted_iota(jnp.int32, (tq, ext), 1)
            s = jnp.where(row >= col, s, -jnp.inf)
            m = jnp.max(s, axis=-1, keepdims=True)
            p = jnp.exp(s - m)
            l = jnp.sum(p, axis=-1, keepdims=True)
            o_ref[...] = _dot((p / l).astype(BF16), v_ref[0, :ext, :])
            lse_ref[0] = m + jnp.log(l)

        for g in range(T // MLA_EXT):
            pl.when(i // (MLA_EXT // tq) == g)(lambda g=g: rows_below((g + 1) * MLA_EXT))

    return pl.pallas_call(
        body, name="mla_fwd",
        out_shape=[jax.ShapeDtypeStruct((T, H_MLA * VDIM), F32), jax.ShapeDtypeStruct((H_MLA, T, 1), F32)],
        grid=(H_MLA, T // tq),
        in_specs=[pl.BlockSpec((1, tq, QK), lambda h, i: (h, i, 0)),
                  pl.BlockSpec((1, T, QK), lambda h, i: (h, 0, 0)),
                  pl.BlockSpec((1, T, VDIM), lambda h, i: (h, 0, 0))],
        out_specs=[pl.BlockSpec((tq, VDIM), lambda h, i: (i, h)),
                   pl.BlockSpec((1, tq, 1), lambda h, i: (h, i, 0))],
        compiler_params=_params(("parallel", "parallel")),
    )(q, k, v)


def _mla_bwd(q, k, v, o, do, lse):
    tq = MLA_TQ

    def body(q_ref, k_ref, v_ref, o_ref, do_ref, lse_ref, dq_ref, dk_ref, dv_ref):
        i = pl.program_id(1)

        @pl.when(i == 0)
        def _():
            dk_ref[...] = jnp.zeros_like(dk_ref)
            dv_ref[...] = jnp.zeros_like(dv_ref)

        q, lse = q_ref[0], lse_ref[0]
        delta = jnp.sum(do_ref[...] * o_ref[...], axis=-1, keepdims=True)
        do = do_ref[...].astype(BF16)

        def rows_below(ext):
            k, v = k_ref[0, :ext, :], v_ref[0, :ext, :]
            s = _dot(q, k, NT) * MLA_SCALE
            row = i * tq + lax.broadcasted_iota(jnp.int32, (tq, ext), 0)
            col = lax.broadcasted_iota(jnp.int32, (tq, ext), 1)
            p = jnp.where(row >= col, jnp.exp(s - lse), 0.0)
            ds = (p * (_dot(do, v, NT) - delta) * MLA_SCALE).astype(BF16)
            dq_ref[0] = _dot(ds, k)
            dk_ref[0, :ext, :] += _dot(ds, q, TN)
            dv_ref[0, :ext, :] += _dot(p.astype(BF16), do, TN)

        for g in range(T // MLA_EXT):
            pl.when(i // (MLA_EXT // tq) == g)(lambda g=g: rows_below((g + 1) * MLA_EXT))

    return pl.pallas_call(
        body, name="mla_bwd",
        out_shape=[jax.ShapeDtypeStruct((H_MLA, T, QK), F32), jax.ShapeDtypeStruct((H_MLA, T, QK), F32),
                   jax.ShapeDtypeStruct((H_MLA, T, VDIM), F32)],
        grid=(H_MLA, T // tq),
        in_specs=[pl.BlockSpec((1, tq, QK), lambda h, i: (h, i, 0)),
                  pl.BlockSpec((1, T, QK), lambda h, i: (h, 0, 0)),
                  pl.BlockSpec((1, T, VDIM), lambda h, i: (h, 0, 0)),
                  pl.BlockSpec((tq, VDIM), lambda h, i: (i, h)),
                  pl.BlockSpec((tq, VDIM), lambda h, i: (i, h)),
                  pl.BlockSpec((1, tq, 1), lambda h, i: (h, i, 0))],
        out_specs=[pl.BlockSpec((1, tq, QK), lambda h, i: (h, i, 0)),
                   pl.BlockSpec((1, T, QK), lambda h, i: (h, 0, 0)),
                   pl.BlockSpec((1, T, VDIM), lambda h, i: (h, 0, 0))],
        compiler_params=_params(("parallel", "arbitrary")),
    )(q, k, v, o, do, lse)


NBLK = T // BLK


BAND_GL = 256
BAND_GH = BAND_GL // DIL_D
BAND_ROWS = BAND_GH * BLK
BAND_GROUPS = DIL_W // BAND_GL


def _band_masks():
    r = lax.broadcasted_iota(jnp.int32, (BAND_ROWS, BLK), 0) & (BLK - 1)
    j = lax.broadcasted_iota(jnp.int32, (BAND_ROWS, BLK), 1)
    return j <= r, j >= r


def _head_lanes():
    lane_head = lax.broadcasted_iota(jnp.int32, (1, BAND_GL), 1) // DIL_D
    return [lane_head == h for h in range(BAND_GH)]


def _stack_heads(x, lanes):
    return jnp.concatenate([jnp.where(m, x, jnp.zeros_like(x)) for m in lanes], axis=0)


def _merge_heads(xs, lanes):
    out = None
    for h, m in enumerate(lanes):
        part = jnp.where(m, xs[h * BLK:(h + 1) * BLK], 0.0)
        out = part if out is None else out + part
    return out


def _per_head(x, lanes):
    return jnp.concatenate([jnp.sum(jnp.where(m, x, 0.0), axis=-1, keepdims=True) for m in lanes], axis=0)


def _lane_group(ref, g):
    return ref[0, :, g * BAND_GL:(g + 1) * BAND_GL]


def _seq_start(p, i):
    per_seq = lax.shift_right_logical(jnp.int32(NBLK), 2 * p)
    return lax.rem(i, per_seq) == 0


def _band_fwd(q, k, v):
    def body(q_ref, kp_ref, kc_ref, vp_ref, vc_ref, o_ref, lse_ref):
        p, i = pl.program_id(0), pl.program_id(1)
        has_prev = jnp.logical_not(_seq_start(p, i))
        m_cur, m_prev = _band_masks()
        m_prev = m_prev & has_prev
        lanes = _head_lanes()
        for g in range(BAND_GROUPS):
            qs = _stack_heads(_lane_group(q_ref, g), lanes)
            s_c = jnp.where(m_cur, _dot(qs, _lane_group(kc_ref, g), NT) * DIL_SCALE, -jnp.inf)
            s_p = jnp.where(m_prev, _dot(qs, _lane_group(kp_ref, g), NT) * DIL_SCALE, -jnp.inf)
            m = jnp.maximum(jnp.max(s_c, axis=-1, keepdims=True), jnp.max(s_p, axis=-1, keepdims=True))
            e_c, e_p = jnp.exp(s_c - m), jnp.exp(s_p - m)
            l = jnp.sum(e_c, axis=-1, keepdims=True) + jnp.sum(e_p, axis=-1, keepdims=True)
            os = (_dot((e_p / l).astype(BF16), _lane_group(vp_ref, g))
                  + _dot((e_c / l).astype(BF16), _lane_group(vc_ref, g)))
            cols = slice(g * BAND_GL, (g + 1) * BAND_GL)
            o_ref[0, :, cols] = _merge_heads(os, lanes)
            lse_ref[0, :, cols] = _merge_heads(m + jnp.log(l), lanes)

    blk = (1, BLK, DIL_W)
    cur = lambda p, i: (p, i, 0)
    prev = lambda p, i: (p, jnp.maximum(i - 1, 0), 0)
    return pl.pallas_call(
        body, name="band_fwd",
        out_shape=[jax.ShapeDtypeStruct((3, T, DIL_W), F32)] * 2,
        grid=(3, NBLK),
        in_specs=[pl.BlockSpec(blk, cur), pl.BlockSpec(blk, prev), pl.BlockSpec(blk, cur),
                  pl.BlockSpec(blk, prev), pl.BlockSpec(blk, cur)],
        out_specs=[pl.BlockSpec(blk, cur), pl.BlockSpec(blk, cur)],
        compiler_params=_params(("parallel", "parallel")),
    )(q, k, k, v, v)


def _band_bwd(q, k, v, o, lse, do, dlse):
    def body(qc_ref, qn_ref, kp_ref, kc_ref, vp_ref, vc_ref, oc_ref, on_ref, lc_ref, ln_ref,
             doc_ref, don_ref, dlc_ref, dln_ref, dq_ref, dk_ref, dv_ref):
        p, i = pl.program_id(0), pl.program_id(1)
        has_prev = jnp.logical_not(_seq_start(p, i))
        has_next = jnp.logical_not(_seq_start(p, i + 1)) & (i + 1 < NBLK)
        m_cur, m_prev = _band_masks()
        lanes = _head_lanes()

        def probs(qs, k, lse, mask):
            return jnp.where(mask, jnp.exp(_dot(qs, k, NT) * DIL_SCALE - lse), 0.0)

        def dscore(pr, dos, v, shift):
            return (pr * (_dot(dos, v, NT) + shift) * DIL_SCALE).astype(BF16)

        for g in range(BAND_GROUPS):
            grp = lambda ref: _lane_group(ref, g)
            kp, kc, vp, vc = grp(kp_ref), grp(kc_ref), grp(vp_ref), grp(vc_ref)
            qc, qn = _stack_heads(grp(qc_ref), lanes), _stack_heads(grp(qn_ref), lanes)
            doc, don = grp(doc_ref), grp(don_ref)
            lse_c = _per_head(grp(lc_ref), lanes) * (1.0 / DIL_D)
            lse_n = _per_head(grp(ln_ref), lanes) * (1.0 / DIL_D)
            sh_c = _per_head(grp(dlc_ref) - doc * grp(oc_ref), lanes)
            sh_n = _per_head(grp(dln_ref) - don * grp(on_ref), lanes)
            doc, don = _stack_heads(doc.astype(BF16), lanes), _stack_heads(don.astype(BF16), lanes)
            p_cc = probs(qc, kc, lse_c, m_cur)
            p_cp = probs(qc, kp, lse_c, m_prev & has_prev)
            p_nc = probs(qn, kc, lse_n, m_prev & has_next)
            ds_cc = dscore(p_cc, doc, vc, sh_c)
            ds_cp = dscore(p_cp, doc, vp, sh_c)
            ds_nc = dscore(p_nc, don, vc, sh_n)
            cols = slice(g * BAND_GL, (g + 1) * BAND_GL)
            dq_ref[0, :, cols] = _merge_heads(_dot(ds_cc, kc) + _dot(ds_cp, kp), lanes)
            dk_ref[0, :, cols] = _dot(ds_cc, qc, TN) + _dot(ds_nc, qn, TN)
            dv_ref[0, :, cols] = _dot(p_cc.astype(BF16), doc, TN) + _dot(p_nc.astype(BF16), don, TN)

    blk = (1, BLK, DIL_W)
    cur = lambda p, i: (p, i, 0)
    prev = lambda p, i: (p, jnp.maximum(i - 1, 0), 0)
    nxt = lambda p, i: (p, jnp.minimum(i + 1, NBLK - 1), 0)
    w, wn, wp = pl.BlockSpec(blk, cur), pl.BlockSpec(blk, nxt), pl.BlockSpec(blk, prev)
    return pl.pallas_call(
        body, name="band_bwd",
        out_shape=[jax.ShapeDtypeStruct((3, T, DIL_W), F32)] * 3,
        grid=(3, NBLK),
        in_specs=[w, wn, wp, w, wp, w, w, wn, w, wn, w, wn, w, wn],
        out_specs=[w, w, w],
        compiler_params=_params(("parallel", "parallel")),
    )(q, q, k, k, v, v, o, o, lse, lse, do, do, dlse, dlse)


SCAN_TC = 256


def _scan_fwd(bu, ar, ai):
    tc, S = SCAN_TC, SSM_S

    def body(bu_ref, ar_ref, ai_ref, h_ref, h16_ref, cr_ref, ci_ref):
        @pl.when(pl.program_id(0) == 0)
        def _():
            cr_ref[...] = jnp.zeros_like(cr_ref)
            ci_ref[...] = jnp.zeros_like(ci_ref)

        a_r, a_i = ar_ref[...], ai_ref[...]

        def step(j, carry):
            hr, hi = carry
            for r in range(8):
                t = pl.multiple_of(j * 8, 8) + r
                br = bu_ref[pl.ds(t, 1), pl.ds(0, S)]
                bi = bu_ref[pl.ds(t, 1), pl.ds(S, S)]
                hr, hi = a_r * hr - a_i * hi + br, a_r * hi + a_i * hr + bi
                h_ref[pl.ds(t, 1), pl.ds(0, S)] = hr
                h_ref[pl.ds(t, 1), pl.ds(S, S)] = hi
            return hr, hi

        hr, hi = lax.fori_loop(0, tc // 8, step, (cr_ref[...], ci_ref[...]))
        cr_ref[...] = hr
        ci_ref[...] = hi
        h16_ref[...] = h_ref[...].astype(BF16)

    return pl.pallas_call(
        body, name="s5_scan_fwd",
        out_shape=[jax.ShapeDtypeStruct((T, 2 * S), F32), jax.ShapeDtypeStruct((T, 2 * S), BF16)],
        grid=(T // tc,),
        in_specs=[pl.BlockSpec((tc, 2 * S), lambda i: (i, 0)), _full_spec((1, S)), _full_spec((1, S))],
        out_specs=[pl.BlockSpec((tc, 2 * S), lambda i: (i, 0)), pl.BlockSpec((tc, 2 * S), lambda i: (i, 0))],
        scratch_shapes=[pltpu.VMEM((1, S), F32), pltpu.VMEM((1, S), F32)],
        compiler_params=_params(("arbitrary",)),
    )(bu, ar, ai)


def _scan_bwd(dh, h, ar, ai):
    tc, S = SCAN_TC, SSM_S
    nc = T // tc

    def body(dh_ref, h_ref, hp_ref, ar_ref, ai_ref, g16_ref, dar_ref, dai_ref, cr_ref, ci_ref, g_ref):
        i = pl.program_id(0)

        @pl.when(i == 0)
        def _():
            cr_ref[...] = jnp.zeros_like(cr_ref)
            ci_ref[...] = jnp.zeros_like(ci_ref)
            dar_ref[...] = jnp.zeros_like(dar_ref)
            dai_ref[...] = jnp.zeros_like(dai_ref)

        a_r, a_i = ar_ref[...], ai_ref[...]
        first_chunk = (i == nc - 1)
        edge = jnp.where(first_chunk, 0.0, 1.0)
        hpr = hp_ref[pl.ds(7, 1), pl.ds(0, S)] * edge
        hpi = hp_ref[pl.ds(7, 1), pl.ds(S, S)] * edge

        def step(jj, carry):
            gr, gi, dar, dai = carry
            j = tc // 8 - 1 - jj
            for r in range(7, -1, -1):
                t = pl.multiple_of(j * 8, 8) + r
                tp = jnp.maximum(t - 1, 0)
                inside = t > 0
                pr = jnp.where(inside, h_ref[pl.ds(tp, 1), pl.ds(0, S)], hpr)
                pi = jnp.where(inside, h_ref[pl.ds(tp, 1), pl.ds(S, S)], hpi)
                gr, gi = (dh_ref[pl.ds(t, 1), pl.ds(0, S)] + a_r * gr + a_i * gi,
                          dh_ref[pl.ds(t, 1), pl.ds(S, S)] + a_r * gi - a_i * gr)
                g_ref[pl.ds(t, 1), pl.ds(0, S)] = gr
                g_ref[pl.ds(t, 1), pl.ds(S, S)] = gi
                dar = dar + gr * pr + gi * pi
                dai = dai + gi * pr - gr * pi
            return gr, gi, dar, dai

        zero = jnp.zeros((1, S), F32)
        gr, gi, dar, dai = lax.fori_loop(0, tc // 8, step, (cr_ref[...], ci_ref[...], zero, zero))
        cr_ref[...] = gr
        ci_ref[...] = gi
        dar_ref[...] += dar
        dai_ref[...] += dai
        g16_ref[...] = g_ref[...].astype(BF16)

    rev = lambda i: (nc - 1 - i, 0)
    before = lambda i: (jnp.maximum((nc - 1 - i) * (tc // 8) - 1, 0), 0)
    return pl.pallas_call(
        body, name="s5_scan_bwd",
        out_shape=[jax.ShapeDtypeStruct((T, 2 * S), BF16), jax.ShapeDtypeStruct((1, S), F32),
                   jax.ShapeDtypeStruct((1, S), F32)],
        grid=(nc,),
        in_specs=[pl.BlockSpec((tc, 2 * S), rev), pl.BlockSpec((tc, 2 * S), rev), pl.BlockSpec((8, 2 * S), before),
                  _full_spec((1, S)), _full_spec((1, S))],
        out_specs=[pl.BlockSpec((tc, 2 * S), rev), _full_spec((1, S)), _full_spec((1, S))],
        scratch_shapes=[pltpu.VMEM((1, S), F32), pltpu.VMEM((1, S), F32), pltpu.VMEM((tc, 2 * S), F32)],
        compiler_params=_params(("arbitrary",)),
    )(dh, h, h, ar, ai)


def _sum_rows(name, x):
    def body(x_ref, o_ref):
        o_ref[...] = jnp.sum(x_ref[...], axis=0, keepdims=True)

    return pl.pallas_call(body, name=name, out_shape=jax.ShapeDtypeStruct((1, 1), F32),
                          in_specs=[_full_spec(x.shape)], out_specs=_full_spec((1, 1)), grid=(1,))(x)


ANY = pl.BlockSpec(memory_space=pl.ANY)


def _place():
    return lax.axis_index("x"), lax.axis_index("y"), lax.axis_index("c")


HBM = pl.BlockSpec(memory_space=pltpu.HBM)
SEM = pl.BlockSpec(memory_space=pltpu.SEMAPHORE)
DATAFLOW = pltpu.SideEffectType.DATAFLOW_SIDE_EFFECTING


def _hbm(a):
    return pltpu.with_memory_space_constraint(a, pltpu.HBM)


def _split_start(name, srcs, lands, ncopies, plan, after):
    ns, nl = len(srcs), len(lands)

    def body(*refs):
        send_sems, recv_sems = refs[ns + nl + 1], refs[ns + nl + 2]
        token = refs[-1]
        for k, (src, dst, peer, _) in enumerate(plan(refs[:ns], refs[ns:ns + nl])):
            pltpu.make_async_remote_copy(src_ref=src, dst_ref=dst, send_sem=send_sems.at[k], recv_sem=recv_sems.at[k],
                                         device_id=peer, device_id_type=MESH).start()
        token[...] = jnp.zeros_like(token)

    out = pl.pallas_call(
        body, name=name,
        out_shape=(pltpu.SemaphoreType.DMA((ncopies,)), pltpu.SemaphoreType.DMA((ncopies,)),
                   *[pltpu.HBM(a.shape, a.dtype) for a in srcs], *[pltpu.HBM(a.shape, a.dtype) for a in lands],
                   jax.ShapeDtypeStruct((8, LANES), F32)),
        in_specs=[HBM] * (ns + nl) + [ANY],
        out_specs=(SEM, SEM, *[HBM] * (ns + nl), pl.BlockSpec(memory_space=pltpu.VMEM)),
        input_output_aliases={i: 2 + i for i in range(ns + nl)},
        compiler_params=pltpu.CompilerParams(has_side_effects=DATAFLOW),
    )(*[_hbm(a) for a in srcs], *[_hbm(a) for a in lands], after)
    return out[0], out[1], list(out[2:2 + ns]), list(out[2 + ns:2 + ns + nl]), out[-1]


def _split_wait(name, handle, ncopies, plan, after):
    send_sems, recv_sems, srcs, lands, _ = handle
    ns, nl = len(srcs), len(lands)

    def body(*refs):
        s_sems, r_sems = refs[ns + nl], refs[ns + nl + 1]
        for k, (src, dst, peer, mine) in enumerate(plan(refs[:ns], refs[ns:ns + nl])):
            pltpu.make_async_remote_copy(src_ref=src, dst_ref=dst, send_sem=s_sems.at[k], recv_sem=r_sems.at[k],
                                         device_id=peer, device_id_type=MESH).wait_send()
            pltpu.make_async_remote_copy(src_ref=src, dst_ref=mine, send_sem=s_sems.at[k], recv_sem=r_sems.at[k],
                                         device_id=peer, device_id_type=MESH).wait_recv()

    out = pl.pallas_call(
        body, name=name,
        out_shape=(*[pltpu.HBM(a.shape, a.dtype) for a in srcs], *[pltpu.HBM(a.shape, a.dtype) for a in lands]),
        in_specs=[HBM] * (ns + nl) + [SEM, SEM, ANY],
        out_specs=tuple([HBM] * (ns + nl)),
        input_output_aliases={i: i for i in range(ns + nl)},
        compiler_params=pltpu.CompilerParams(has_side_effects=DATAFLOW),
    )(*srcs, *lands, send_sems, recv_sems, after)
    return list(out[:ns]), list(out[ns:])


def _slot(px, py, pc):
    return 4 * px + 2 * py + pc


def _gather_plan(xs, lands):
    x, y, c = _place()
    peers = [(x, y, 1 - c), (1 - x, y, c), (x, 1 - y, c), (1 - x, 1 - y, c)]
    return [(xs[t], lands[t].at[_slot(x, y, c)], peer, lands[t].at[_slot(*peer)])
            for t in range(len(xs)) for peer in peers]


def _gather_start(name, shards, after):
    lands = [lax.empty((N_DEV,) + s.shape, s.dtype) for s in shards]
    return _split_start(name, shards, lands, 4 * len(shards), _gather_plan, after)


def _pass_on_plan(_, lands):
    x, y, c = _place()
    blocks = [((1 - x, y, c), (1 - x, y, 1 - c)), ((x, 1 - y, c), (x, 1 - y, 1 - c)),
              ((1 - x, 1 - y, c), (1 - x, 1 - y, 1 - c)), ((x, y, 1 - c), (x, y, c))]
    return [(lands[t].at[_slot(*out)], lands[t].at[_slot(*out)], (x, y, 1 - c), lands[t].at[_slot(*back)])
            for t in range(len(lands)) for out, back in blocks]


def _gather_pass_on(name, handle, after):
    n = len(handle[2])
    _, lands = _split_wait(name + "_wait", handle, 4 * n, _gather_plan, after)
    return _split_start(name + "_pass_on", [], lands, 4 * n, _pass_on_plan, after)


def _gather_finish(name, handle, after):
    n = len(handle[3])
    _, lands = _split_wait(name + "_done", handle, 4 * n, _pass_on_plan, after)
    return lands


def _sibling_plan(ps, gots):
    x, y, c = _place()
    return [(ps[t].at[j, 1 - c], gots[t].at[j], (x, y, 1 - c), gots[t].at[j]) for t in range(len(ps)) for j in range(4)]


def _chips_plan(ss, gots):
    x, y, c = _place()
    chips = [(1 - x, y), (x, 1 - y), (1 - x, 1 - y)]
    return [(ss[t].at[2 * px + py], gots[t].at[k], (px, py, c), gots[t].at[k])
            for t in range(len(ss)) for k, (px, py) in enumerate(chips)]


BF16_ROWS = 16


def _shard_tiles(R, C, block_bytes):
    tr = _tile(R, max(BF16_ROWS, min(1024, block_bytes // (C * 4))), BF16_ROWS)
    tc = C if tr * C * 4 <= 2 * block_bytes else _tile(C, max(LANES, block_bytes // (tr * 4)))
    return tr, tc


def _pair_sum(name, p4, got, core):
    _, _, R, C = p4.shape
    tr, tc = _shard_tiles(R, C, ROW_BLOCK_BYTES)

    def body(core_ref, p_ref, g_ref, o_ref):
        o_ref[...] = (p_ref[:, 0] + g_ref[...]).astype(o_ref.dtype)

    return pl.pallas_call(
        body, name=name, out_shape=jax.ShapeDtypeStruct((4, R, C), BF16),
        grid_spec=pltpu.PrefetchScalarGridSpec(
            num_scalar_prefetch=1, grid=(4, R // tr, C // tc),
            in_specs=[pl.BlockSpec((1, 1, tr, tc), lambda j, i, k, core: (j, core[0], i, k)),
                      pl.BlockSpec((1, tr, tc), lambda j, i, k, core: (j, i, k))],
            out_specs=pl.BlockSpec((1, tr, tc), lambda j, i, k, core: (j, i, k))),
        compiler_params=_params(("parallel", "parallel", "parallel")),
    )(core, p4, got)


def _sum_devices(name, g8):
    _, R, C = g8.shape
    tr = _tile(R, SMALL_ROW_ALIGN, 8)

    def body(g_ref, o_ref):
        acc = g_ref[0]
        for d in range(1, N_DEV):
            acc = acc + g_ref[d]
        o_ref[...] = acc

    return pl.pallas_call(
        body, name=name, out_shape=jax.ShapeDtypeStruct((R, C), F32), grid=(R // tr,),
        in_specs=[pl.BlockSpec((N_DEV, tr, C), lambda i: (0, i, 0))], out_specs=pl.BlockSpec((tr, C), lambda i: (i, 0)),
        compiler_params=_params(("parallel",)),
    )(g8)


def _adamw_shard(name, layer, w, m, v, s4, got, chip, bufs):
    _, R, C = w.shape
    tr, tc = _shard_tiles(R, C, ROW_BLOCK_BYTES // 2)

    def body(chip_ref, w_ref, m_ref, v_ref, s_ref, g_ref, b0, b1, b2, b3, og, od, om, ov):
        f = lambda a: a.astype(F32)
        g = ((f(s_ref[0]) + f(g_ref[0])) + f(g_ref[1])) + f(g_ref[2])
        d, nm, nv = _adamw_fn(w_ref[0], g, m_ref[0], v_ref[0])
        og[0], od[0], om[0], ov[0] = g, d, nm, nv

    lay = pl.BlockSpec((1, tr, tc), lambda i, k, chip: (layer, i, k))
    if bufs is None:
        bufs = [lax.empty(w.shape, F32) for _ in range(4)]
    return pl.pallas_call(
        body, name=name, out_shape=[jax.ShapeDtypeStruct(w.shape, F32)] * 4,
        grid_spec=pltpu.PrefetchScalarGridSpec(
            num_scalar_prefetch=1, grid=(R // tr, C // tc),
            in_specs=[lay, lay, lay, pl.BlockSpec((1, tr, tc), lambda i, k, chip: (chip[0], i, k)),
                      pl.BlockSpec((3, tr, tc), lambda i, k, chip: (0, i, k)), ANY, ANY, ANY, ANY],
            out_specs=[lay] * 4),
        input_output_aliases={6: 0, 7: 1, 8: 2, 9: 3},
        compiler_params=_params(("parallel", "parallel")),
    )(chip, w, m, v, s4, got, *bufs)


def _adamw(name, wt, g, m, v):
    shape = wt.shape
    two = (lambda a: a.reshape(1, -1)) if wt.ndim == 1 else (lambda a: a.reshape(-1, shape[-1]))
    w2, g2, m2, v2 = two(wt), two(g), two(m), two(v)
    tr = _row(w2, None, 0)[2]
    outs = [(w2.shape, F32, 0, tr)] * 3
    d, nm, nv = _rows_fwd(name, _adamw_fn, [_row(a, tr, 0) for a in (w2, g2, m2, v2)], [], outs)
    return d.reshape(shape), nm.reshape(shape), nv.reshape(shape)


def _lane_tiling():
    return (jnp.arange(SSM_N)[:, None] == (jnp.arange(SSM_S) % SSM_N)[None, :]).astype(BF16)


def _own_block():
    r = lax.broadcasted_iota(jnp.int32, (SSM_G * SSM_P, SSM_S), 0) // SSM_P
    c = lax.broadcasted_iota(jnp.int32, (SSM_G * SSM_P, SSM_S), 1) // SSM_N
    return r == c


def _bd_build(name, v_re, v_im, sign, tiling):
    def body(r_ref, i_ref, t_ref, o_ref):
        own = _own_block()
        o_ref[:, :SSM_S] = jnp.where(own, _dot(r_ref[...].astype(BF16), t_ref[...]), 0.0).astype(BF16)
        o_ref[:, SSM_S:] = jnp.where(own, sign * _dot(i_ref[...].astype(BF16), t_ref[...]), 0.0).astype(BF16)

    rows = SSM_G * SSM_P
    return pl.pallas_call(
        body, name=name, out_shape=jax.ShapeDtypeStruct((rows, 2 * SSM_S), BF16), grid=(1,),
        in_specs=[_full_spec((rows, SSM_N))] * 2 + [_full_spec((SSM_N, SSM_S))], out_specs=_full_spec((rows, 2 * SSM_S)),
        compiler_params=_params(("arbitrary",)),
    )(v_re, v_im, tiling)


def _bd_extract(name, m, sign, tiling):
    def body(m_ref, t_ref, r_ref, i_ref):
        own, t = _own_block(), t_ref[...]

        def pick(x):
            x = jnp.where(own, x, 0.0)
            hi = x.astype(BF16)
            rest = x - hi.astype(F32)
            mid = rest.astype(BF16)
            lo = (rest - mid.astype(F32)).astype(BF16)
            return _dot(hi, t, NT) + _dot(mid, t, NT) + _dot(lo, t, NT)

        r_ref[...] = pick(m_ref[:, :SSM_S])
        i_ref[...] = sign * pick(m_ref[:, SSM_S:])

    rows = SSM_G * SSM_P
    return pl.pallas_call(
        body, name=name, out_shape=[jax.ShapeDtypeStruct((rows, SSM_N), F32)] * 2, grid=(1,),
        in_specs=[_full_spec((rows, 2 * SSM_S)), _full_spec((SSM_N, SSM_S))], out_specs=[_full_spec((rows, SSM_N))] * 2,
        compiler_params=_params(("arbitrary",)),
    )(m, tiling)


def _fold(a, dil):
    if dil == 1:
        return a
    return a.reshape((T // dil, dil) + a.shape[1:]).swapaxes(0, 1).reshape(a.shape)


def _unfold(a, dil):
    if dil == 1:
        return a
    return a.reshape((dil, T // dil) + a.shape[1:]).swapaxes(0, 1).reshape(a.shape)


DILS = (1, 4, 16)


def _fold3(parts):
    parts = [parts] * 3 if not isinstance(parts, (list, tuple)) else parts
    return jnp.stack([_fold(a, d) for a, d in zip(parts, DILS)])


def _unfold3(a):
    return [_unfold(a[p], d) for p, d in enumerate(DILS)]


def _rope_tables():
    half = ROPE // 2
    inv_freq = ROPE_THETA ** (-jnp.arange(half, dtype=F32) / half)
    ang = jnp.arange(T).astype(F32)[:, None] * inv_freq[None, :]
    i, j = jnp.arange(ROPE)[:, None], jnp.arange(ROPE)[None, :]
    rot = jnp.where(i == j + half, -1.0, jnp.where(i + half == j, 1.0, 0.0)).astype(F32)
    return jnp.tile(jnp.cos(ang), (1, 2)), jnp.tile(jnp.sin(ang), (1, 2)), rot


def _rot_half(x, rot):
    return _dot(x.reshape(-1, ROPE), rot, prec=HI).reshape(x.shape)


def _mla_pack_fn(q, kv, k_rope, cos, sin, rot):
    rope = lambda x: x * cos + _rot_half(x, rot) * sin
    q_out = jnp.concatenate([q[:, :, :NOPE], rope(q[:, :, NOPE:])], axis=-1)
    k_pe = jnp.broadcast_to(rope(k_rope)[None], (H_MLA,) + k_rope.shape)
    return q_out, jnp.concatenate([kv[:, :, :NOPE], k_pe], axis=-1), kv[:, :, NOPE:]


def _mla_unpack_fn(dq, dk, dv, cos, sin, rot):
    unrope = lambda g: g * cos - _rot_half(g * sin, rot)
    dq_out = jnp.concatenate([dq[:, :, :NOPE], unrope(dq[:, :, NOPE:])], axis=-1)
    dk_rope = unrope(jnp.sum(dk[:, :, NOPE:], axis=0))
    return dq_out, jnp.concatenate([dk[:, :, :NOPE], dv], axis=-1), dk_rope


def _flat(w8):
    return w8.reshape(-1, w8.shape[-1])


def _blocks(m):
    return m.reshape(N_DEV, -1, m.shape[-1])


def _behind(a, tok):
    return a if tok is None else a + tok


def _mixer_fwd_in(x, w, sp, rope):
    s = {}
    s['x'] = x
    h = _rms_fwd("rms_mix", x, sp['g_mix'])
    proj = _mm("mm_in", h, _flat(w['w_in']), 'nt')
    offs = np.cumsum((0,) + IN_SPLITS)
    c_q, c_kv, k_rope, u, qd, kd, vd = [proj[:, offs[i]:offs[i + 1]] for i in range(7)]
    s.update(h=h, c_q=c_q, c_kv=c_kv, u=u)

    cqn = _rms_fwd("rms_q", c_q, sp['g_q'])
    ckvn = _rms_fwd("rms_kv", c_kv, sp['g_kv'])
    q8 = _mm("mm_uq", cqn, w['w_uq'], 'nt', bb='r', ob='c')
    kv8 = _mm("mm_ukv", ckvn, w['w_ukv'], 'nn', bb='c', ob='c')
    cos, sin, rot = rope
    tr = MLA_TQ
    qh, kh, vh = _rows_fwd(
        "mla_pack", _mla_pack_fn,
        [_row(q8, tr, 1), _row(kv8, tr, 1), _row(k_rope, tr, 0), _row(cos, tr, 0), _row(sin, tr, 0)], [rot],
        [((H_MLA, T, QK), BF16, 1, tr), ((H_MLA, T, QK), BF16, 1, tr), ((H_MLA, T, VDIM), BF16, 1, tr)])
    y_mla, lse_mla = _mla_fwd(qh, kh, vh)
    s.update(cqn=cqn, ckvn=ckvn, qh=qh, kh=kh, vh=vh, lse_mla=lse_mla, y_mla=y_mla, qd=qd, kd=kd, vd=vd)
    return s


def _mixer_fwd_out(s, w, sp, tok=None, after_ssm=None):
    x, u, y_mla, qd, kd, vd = s['x'], s['u'], s['y_mla'], s['qd'], s['kd'], s['vd']
    a3 = lambda n: sp[n].reshape(SSM_G, 1, SSM_N)
    b2 = lambda n: sp[n].transpose(0, 2, 1).reshape(SSM_G * SSM_P, SSM_N)
    disc_rows = [_row(a3('a_re'), 1, 0), _row(a3('a_im'), 1, 0), _row(sp['log_dt'].reshape(SSM_G, 1, 1), 1, 0),
                 _row(b2('b_re'), SSM_P, 0), _row(b2('b_im'), SSM_P, 0)]
    abr, abi, bbr, bbi = _rows_fwd(
        "s5_disc", _s5_disc_fn, disc_rows, [],
        [((SSM_G, 1, SSM_N), F32, 0, 1), ((SSM_G, 1, SSM_N), F32, 0, 1),
         ((SSM_G * SSM_P, SSM_N), F32, 0, SSM_P), ((SSM_G * SSM_P, SSM_N), F32, 0, SSM_P)])
    ar, ai = abr.reshape(1, SSM_S), abi.reshape(1, SSM_S)
    tiling = _lane_tiling()
    b_mat = _bd_build("s5_b_matrix", bbr, bbi, 1.0, tiling)
    c2 = lambda n: sp[n].reshape(SSM_G * SSM_P, SSM_N)
    c_mat = _bd_build("s5_c_matrix", c2('c_re'), c2('c_im'), -1.0, tiling)
    u16 = _behind(u, tok).astype(BF16)
    bu = _mm("mm_s5_b", u16, b_mat, 'nn')
    hst, hst16 = _scan_fwd(bu, ar, ai)
    ymm = _mm("mm_s5_c", hst16, c_mat, 'nt')
    d_row = sp['d_skip'].reshape(1, SSM_W)
    (yg,) = _rows_fwd("s5_act", _s5_act_fn, [_row(ymm), _row(u)], [d_row], [((T, SSM_W), BF16, -2, _row(u)[2])])
    z = _mm("mm_glu", yg, w['w_glu'], 'nn', bb='c')
    glu_rows = [_row(z[:, :SSM_W]), _row(z[:, SSM_W:])]
    glu_b = [sp['b_glu'][:SSM_W].reshape(1, -1), sp['b_glu'][SSM_W:].reshape(1, -1)]
    (y_ssm,) = _rows_fwd("s5_glu", _glu_fn, glu_rows, glu_b, [((T, SSM_W), F32, -2, glu_rows[0][2])])
    if after_ssm is not None:
        qd = _behind(qd, after_ssm(y_ssm))
    s.update(disc_rows=disc_rows, ar=ar, ai=ai, b_mat=b_mat, c_mat=c_mat, hst=hst, hst16=hst16, u16=u16, ymm=ymm,
             d_row=d_row, yg=yg,
             glu_rows=glu_rows, glu_b=glu_b)

    qf, kf, vf = [_fold3(a).astype(BF16) for a in (qd, kd, vd)]
    o_f, lse_f = _band_fwd(qf, kf, vf)
    mix_rows = [_row(a) for a in _unfold3(o_f) + _unfold3(lse_f)]
    (y_dil,) = _rows_fwd("dil_mix", _dil_mix_fn, mix_rows, [], [((T, DIL_W), F32, -2, mix_rows[0][2])])
    s.update(qf=qf, kf=kf, vf=vf, o_f=o_f, lse_f=lse_f, mix_rows=mix_rows)

    gm, gs, gd = sp['g_out_mla'].reshape(1, -1), sp['g_out_ssm'].reshape(1, -1), sp['g_out_dil'].reshape(1, -1)
    on_rows = [_row(y_mla), _row(y_ssm), _row(y_dil)]
    (ycat,) = _rows_fwd("out_norm", _outnorm_fn, on_rows, [gm, gs, gd], [((T, D), BF16, -2, on_rows[0][2])])
    x1_ = _mm("mm_o", ycat, _flat(w['w_o']), 'nn', res=x)
    s.update(on_rows=on_rows, on_g=[gm, gs, gd], ycat=ycat, x1=x1_)
    for k in ('qd', 'kd', 'vd'):
        del s[k]
    return x1_


def _ffn_fwd(x1_, w, sp, s, tok=None):
    h2 = _rms_fwd("rms_ffn", x1_, _behind(sp['g_ffn'], tok))
    ga = _mm("mm_gate", h2, _flat(w['w_gate']), 'nt')
    gb, zf = _mm("mm_up", h2, _flat(w['w_up']), 'nt',
                 epilogue=(lambda up, gate: (up, _swiglu_fn(gate, up)), [ga], [F32, BF16]))
    x2_ = _mm("mm_down", zf, _flat(w['w_down']), 'nn', res=x1_)
    s.update(h2=h2, ga=ga, gb=gb, zf=zf)
    return x2_


def _b16(a):
    return a.astype(BF16)


def _ffn_bwd(dx2, s, w, sp, tok=None):
    gw, gs_ = {}, {}
    b16 = _b16
    dx2b = b16(_behind(dx2, tok))
    def gate_pullback(dz, gate, up):
        return jax.vjp(_swiglu_fn, gate, up)[1](dz)

    dga, dgb = _mm("mm_down_dx", dx2b, _flat(w['w_down']), 'nt', epilogue=(gate_pullback, [s['ga'], s['gb']], [BF16, BF16]))
    gw['w_down'] = _blocks(_mm("mm_down_dw", s['zf'], dx2b, 'tn'))
    gw['w_gate'] = _blocks(_mm("mm_gate_dw", dga, s['h2'], 'tn'))
    gw['w_up'] = _blocks(_mm("mm_up_dw", dgb, s['h2'], 'tn'))
    dh2 = _mm("mm_up_dx", dgb, _flat(w['w_up']), 'nn', res=_mm("mm_gate_dx", dga, _flat(w['w_gate']), 'nn'))
    dx1, gs_['g_ffn'] = _rms_bwd("rms_ffn_bwd", s['x1'], sp['g_ffn'], dh2, dx2)
    return dx1, gw, gs_


def _mixer_bwd_out(dx1, s, w, sp, tok=None):
    gw, gs_ = {}, {}
    b16 = _b16
    dx1b = b16(_behind(dx1, tok))
    dycat = _mm("mm_o_dx", dx1b, _flat(w['w_o']), 'nt')
    gw['w_o'] = _blocks(_mm("mm_o_dw", s['ycat'], dx1b, 'tn'))
    dy_mla, dy_ssm, dy_dil, gs_['g_out_mla'], gs_['g_out_ssm'], gs_['g_out_dil'] = _rows_vjp(
        "out_norm_bwd", _outnorm_fn, s['on_rows'], s['on_g'], [], [_row(dycat)])

    dmix = _rows_vjp("dil_mix_bwd", _dil_mix_fn, s['mix_rows'], [], [], [_row(dy_dil)])
    dqf, dkf, dvf = _band_bwd(s['qf'], s['kf'], s['vf'], s['o_f'], s['lse_f'], _fold3(dmix[:3]), _fold3(dmix[3:]))
    back = lambda a: sum(_unfold3(a))
    dqd, dkd, dvd = back(dqf), back(dkf), back(dvf)

    dz1, dz2, db1, db2 = _rows_vjp("s5_glu_bwd", _glu_fn, s['glu_rows'], s['glu_b'], [], [_row(dy_ssm)])
    gs_['b_glu'] = jnp.concatenate([db1, db2], axis=1)
    dzb = b16(jnp.concatenate([dz1, dz2], axis=1))
    dyg = _mm("mm_glu_dx", dzb, w['w_glu'], 'nt', bb='c')
    gw['w_glu'] = _mm("mm_glu_dw", s['yg'], dzb, 'tn', ob='c')
    dymm, du_act, dd = _rows_vjp("s5_act_bwd", _s5_act_fn, [_row(s['ymm']), _row(s['u'])], [s['d_row']], [], [_row(dyg)],
                                 grad_dtypes=[BF16, F32])
    gs_['d_skip'] = dd
    dhst = _mm("mm_s5_c_dx", dymm, s['c_mat'], 'nn')
    dc_mat = _mm("mm_s5_c_dw", dymm, s['hst16'], 'tn')
    g, dar, dai = _scan_bwd(dhst, s['hst'], s['ar'], s['ai'])
    du = _mm("mm_s5_b_dx", g, s['b_mat'], 'nt', res=du_act)
    db_mat = _mm("mm_s5_b_dw", s['u16'], g, 'tn')
    tiling = _lane_tiling()
    gs_['c_re'], gs_['c_im'] = _bd_extract("s5_c_blocks", dc_mat, -1.0, tiling)
    dbbr, dbbi = _bd_extract("s5_b_blocks", db_mat, 1.0, tiling)
    disc_cts = [_row(dar.reshape(SSM_G, 1, SSM_N), 1, 0), _row(dai.reshape(SSM_G, 1, SSM_N), 1, 0),
                _row(dbbr, SSM_P, 0), _row(dbbi, SSM_P, 0)]
    da_re, da_im, dldt, db_r, db_i = _rows_vjp("s5_disc_bwd", _s5_disc_fn, s['disc_rows'], [], [], disc_cts)
    gs_['a_re'], gs_['a_im'], gs_['log_dt'] = da_re, da_im, dldt
    unb = lambda a: a.reshape(SSM_G, SSM_P, SSM_N).transpose(0, 2, 1)
    gs_['b_re'], gs_['b_im'] = unb(db_r), unb(db_i)
    return (dy_mla, du, dqd, dkd, dvd), gw, gs_


def _mixer_bwd_in(cts, dx1, s, w, sp, rope, tok=None):
    gw, gs_ = {}, {}
    b16 = _b16
    dy_mla, du, dqd, dkd, dvd = cts
    dqh, dkh, dvh = _mla_bwd(s['qh'], s['kh'], s['vh'], s['y_mla'], dy_mla, _behind(s['lse_mla'], tok))
    cos, sin, rot = rope
    tr = MLA_TQ
    dq8, dkv8, dk_rope = _rows_fwd(
        "mla_unpack", _mla_unpack_fn,
        [_row(dqh, tr, 1), _row(dkh, tr, 1), _row(dvh, tr, 1), _row(cos, tr, 0), _row(sin, tr, 0)], [rot],
        [((H_MLA, T, QK), BF16, 1, tr), ((H_MLA, T, NOPE + VDIM), BF16, 1, tr), ((T, ROPE), F32, 0, tr)])
    dcqn = _mm("mm_uq_dx", dq8, w['w_uq'], 'nn', ab='c', bb='r')
    gw['w_uq'] = _mm("mm_uq_dw", dq8, s['cqn'], 'tn', ab='c', ob='r')
    dckvn = _mm("mm_ukv_dx", dkv8, w['w_ukv'], 'nt', ab='c', bb='c')
    gw['w_ukv'] = _mm("mm_ukv_dw", s['ckvn'], dkv8, 'tn', bb='c', ob='c')
    dc_q, gs_['g_q'] = _rms_bwd("rms_q_bwd", s['c_q'], sp['g_q'], dcqn)
    dc_kv, gs_['g_kv'] = _rms_bwd("rms_kv_bwd", s['c_kv'], sp['g_kv'], dckvn)

    dproj = b16(jnp.concatenate([dc_q, dc_kv, dk_rope, du, dqd, dkd, dvd], axis=1))
    dh = _mm("mm_in_dx", dproj, _flat(w['w_in']), 'nn')
    gw['w_in'] = _blocks(_mm("mm_in_dw", dproj, s['h'], 'tn'))
    dx, gs_['g_mix'] = _rms_bwd("rms_mix_bwd", s['x'], sp['g_mix'], dh, dx1)
    return dx, gw, gs_


def kernel(x, g_mix, w_in, g_q, w_uq, g_kv, w_ukv, a_re, a_im, b_re, b_im, c_re, c_im, d_skip, log_dt, w_glu, b_glu, g_out_mla, g_out_ssm, g_out_dil, w_o, g_ffn, w_gate, w_up, w_down, g_final, loss_target, m_g_mix, m_w_in, m_g_q, m_w_uq, m_g_kv, m_w_ukv, m_a_re, m_a_im, m_b_re, m_b_im, m_c_re, m_c_im, m_d_skip, m_log_dt, m_w_glu, m_b_glu, m_g_out_mla, m_g_out_ssm, m_g_out_dil, m_w_o, m_g_ffn, m_w_gate, m_w_up, m_w_down, m_g_final, v_g_mix, v_w_in, v_g_q, v_w_uq, v_g_kv, v_w_ukv, v_a_re, v_a_im, v_b_re, v_b_im, v_c_re, v_c_im, v_d_skip, v_log_dt, v_w_glu, v_b_glu, v_g_out_mla, v_g_out_ssm, v_g_out_dil, v_w_o, v_g_ffn, v_w_gate, v_w_up, v_w_down, v_g_final):
    W = dict(zip(PARAMS, (g_mix, w_in, g_q, w_uq, g_kv, w_ukv, a_re, a_im, b_re, b_im, c_re, c_im, d_skip, log_dt,
                          w_glu, b_glu, g_out_mla, g_out_ssm, g_out_dil, w_o, g_ffn, w_gate, w_up, w_down, g_final)))
    M = dict(zip(PARAMS, (m_g_mix, m_w_in, m_g_q, m_w_uq, m_g_kv, m_w_ukv, m_a_re, m_a_im, m_b_re, m_b_im, m_c_re,
                          m_c_im, m_d_skip, m_log_dt, m_w_glu, m_b_glu, m_g_out_mla, m_g_out_ssm, m_g_out_dil, m_w_o,
                          m_g_ffn, m_w_gate, m_w_up, m_w_down, m_g_final)))
    V = dict(zip(PARAMS, (v_g_mix, v_w_in, v_g_q, v_w_uq, v_g_kv, v_w_ukv, v_a_re, v_a_im, v_b_re, v_b_im, v_c_re,
                          v_c_im, v_d_skip, v_log_dt, v_w_glu, v_b_glu, v_g_out_mla, v_g_out_ssm, v_g_out_dil, v_w_o,
                          v_g_ffn, v_w_gate, v_w_up, v_w_down, v_g_final)))
    cx, cy, cc = _place()
    core = cc.astype(jnp.int32).reshape(1)
    chip = (2 * cx + cy).astype(jnp.int32).reshape(1)
    rope = _rope_tables()
    small = [{n: W[n][l] for n in SMALL} for l in range(DEPTH)]
    for sp in small:
        for n in ('g_mix', 'g_q', 'g_kv', 'g_ffn'):
            sp[n] = sp[n].reshape(1, -1)

    def tok_of(tokens):
        return sum(t[0, 0] for t in tokens) if tokens else None

    def shard_view(a, n):
        return a.swapaxes(1, 2) if BIG[n] == 't' else a

    def gather_start(l, group, names, after):
        return _gather_start(f"gather_{group}_start_{l}", [shard_view(W[n], n)[l].astype(BF16) for n in names], after)

    xa = x[0]
    h1_mix = gather_start(0, "mix", MIXER_W, jnp.zeros((8, LANES), F32))
    h1_ffn = gather_start(0, "ffn", FFN_W, h1_mix[4])
    h2_mix = _gather_pass_on("gather_mix_0", h1_mix, xa)
    saved, full = [], []
    tokens = [h2_mix[4], h1_ffn[4]]
    for l in range(DEPTH):
        last = l + 1 == DEPTH
        wm = dict(zip(MIXER_W, _gather_finish(f"gather_mix_{l}", h2_mix, xa)))
        sp = dict(small[l])
        sp['g_mix'] = _behind(sp['g_mix'], tok_of(tokens))
        s = _mixer_fwd_in(xa, wm, sp, rope)
        tokens = []
        first_ffn = {}
        if l == 0:
            h1_first = h1_ffn
            def mid(dep):
                first_ffn['h'] = _gather_pass_on("gather_ffn_0", h1_first, dep)
                return first_ffn['h'][4][0, 0]
        else:
            mid = None
        if not last:
            h1_mix = gather_start(l + 1, "mix", MIXER_W, s['y_mla'])
            h1_ffn = gather_start(l + 1, "ffn", FFN_W, h1_mix[4])
            tokens += [h1_mix[4], h1_ffn[4]]
        x1 = _mixer_fwd_out(s, wm, small[l], tok_of(tokens), mid)
        if l == 0:
            h2_ffn = first_ffn['h']
        tokens = []
        wf = dict(zip(FFN_W, _gather_finish(f"gather_ffn_{l}", h2_ffn, x1)))
        if not last:
            h2_mix = _gather_pass_on(f"gather_mix_{l + 1}", h1_mix, x1)
            tokens.append(h2_mix[4])
        xa = _ffn_fwd(x1, wf, small[l], s, tok_of(tokens))
        tokens = []
        if not last:
            h2_ffn = _gather_pass_on(f"gather_ffn_{l + 1}", h1_ffn, xa)
            tokens.append(h2_ffn[4])
        saved.append(s)
        full.append({**wm, **wf})
    gf = g_final.reshape(1, D)
    ones = jnp.ones((T, 1), F32)
    dxa, dgf, loss_rows = _rows_vjp("loss", _loss_fn, [_row(xa)], [gf], [_row(loss_target[0])], [_row(ones)],
                                    primal=True)
    loss_here = _sum_rows("loss_sum", loss_rows)[0, 0]

    bufs = {n: None for n in BIG}
    pending = []

    def advance(dep):
        tokens = []
        for g in pending:
            names, tag = g['names'], g['tag']
            step = g['steps'].pop(0)
            if step == 'sibling':
                p4 = [a.reshape((4, 2) + a.shape[1:]) for a in g['gw']]
                gots = [lax.empty((4,) + a.shape[1:], F32) for a in g['gw']]
                g['h'] = _split_start("rs_sibling_start_" + tag, p4, gots, 4 * len(p4), _sibling_plan, dep)
                tokens.append(g['h'][4])
            elif step == 'chips':
                p4, gots = _split_wait("rs_sibling_wait_" + tag, g['h'], 4 * len(names), _sibling_plan, dep)
                s4 = [_pair_sum("rs_pair_sum_" + n, p, q, core) for n, p, q in zip(names, p4, gots)]
                gots = [lax.empty((3,) + a.shape[1:], a.dtype) for a in s4]
                g['h'] = _split_start("rs_chips_start_" + tag, s4, gots, 3 * len(s4), _chips_plan, dep)
                tokens.append(g['h'][4])
            elif step == 'update':
                s4, gots = _split_wait("rs_chips_wait_" + tag, g['h'], 3 * len(names), _chips_plan, dep)
                for n, s4n, got in zip(names, s4, gots):
                    bufs[n] = _adamw_shard("adamw_" + n, g['layer'], shard_view(W[n], n), shard_view(M[n], n),
                                           shard_view(V[n], n), s4n, got, chip, bufs[n])
        pending[:] = [g for g in pending if g['steps']]
        return tokens

    def group(names, l, gw, kind):
        steps = ['sibling'] + (['rest'] if kind == "ffn" and l > 0 else []) + ['chips', 'rest', 'update']
        return dict(names=names, layer=l, gw=[gw[n] for n in names], steps=steps, tag=f"{kind}_{l}")

    g_small = [None] * DEPTH
    tokens = []
    for l in reversed(range(DEPTH)):
        dx1, gw_f, gs_f = _ffn_bwd(dxa, saved[l], full[l], small[l], tok_of(tokens))
        pending.append(group(FFN_W, l, gw_f, "ffn"))
        tokens = advance(dx1)
        cts, gw_o, gs_o = _mixer_bwd_out(dx1, saved[l], full[l], small[l], tok_of(tokens))
        pending.append(group(OUT_W, l, gw_o, "out"))
        tokens = advance(cts[0])
        dxa, gw_i, gs_i = _mixer_bwd_in(cts, dx1, saved[l], full[l], small[l], rope, tok_of(tokens))
        pending.append(group(IN_W, l, gw_i, "in"))
        tokens = advance(dxa)
        g_small[l] = {**gs_f, **gs_o, **gs_i}

    flat = [g_small[l][n].reshape(-1) for l in range(DEPTH) for n in SMALL] + [dgf.reshape(-1), loss_here.reshape(1)]
    n_small = sum(int(f.shape[0]) for f in flat)
    rows = -(-n_small // (PACK_C * SMALL_ROW_ALIGN)) * SMALL_ROW_ALIGN
    flat = jnp.concatenate(flat + [jnp.zeros((rows * PACK_C - n_small,), F32)]).reshape(rows, PACK_C)
    h_small = _gather_start("gather_small_start", [flat], dxa)
    advance(h_small[4])
    h_small = _gather_pass_on("gather_small", h_small, flat)
    tokens = advance(h_small[4])
    (gathered,) = _gather_finish("gather_small", h_small, flat)
    tot = _behind(_sum_devices("small_sum", gathered).reshape(-1), tok_of(tokens))
    grads, off = {}, 0
    per_layer = {n: [] for n in SMALL}
    for l in range(DEPTH):
        for n, shp in SMALL.items():
            k = int(np.prod(shp))
            per_layer[n].append(tot[off:off + k].reshape(shp))
            off += k
    for n in SMALL:
        grads[n] = jnp.stack(per_layer[n])
    grads['g_final'] = tot[off:off + D]
    loss = tot[off + D]

    delta, new_m, new_v = {}, {}, {}
    for n in PARAMS:
        if n not in BIG:
            delta[n], new_m[n], new_v[n] = _adamw("adamw_" + n, W[n], grads[n], M[n], V[n])
    while pending:
        advance(delta['g_final'])
    for n in BIG:
        grads[n], delta[n], new_m[n], new_v[n] = [shard_view(b, n) for b in bufs[n]]
    return (loss, dxa[None], *[grads[n] for n in PARAMS], *[delta[n] for n in PARAMS],
            *[new_m[n] for n in PARAMS], *[new_v[n] for n in PARAMS])
```

```python
import jax
import jax.numpy as jnp
import numpy as np
from jax import lax
from jax.experimental import pallas as pl
from jax.experimental.pallas import tpu as pltpu

F32 = jnp.float32
BF16 = jnp.bfloat16

T = 2048
D = 2048
DEPTH = 4
N_DEV = 8
H_MLA, NOPE, ROPE, VDIM = 8, 128, 64, 128
QK = NOPE + ROPE
Q_LORA, KV_LORA = 512, 256
SSM_W, SSM_G, SSM_P, SSM_N = 512, 32, 16, 64
SSM_S = SSM_G * SSM_N
DIL_W, DIL_H, DIL_D = 512, 8, 64
BLK = 128
IN_SPLITS = (Q_LORA, KV_LORA, ROPE, SSM_W, DIL_W, DIL_W, DIL_W)
IN_W = sum(IN_SPLITS)
D_FF = 5632
EPS = 1e-6
ROPE_THETA = 10000.0
MLA_SCALE = QK ** -0.5
DIL_SCALE = DIL_D ** -0.5

ADAM_LR, ADAM_B1, ADAM_B2, ADAM_EPS, ADAM_WD, ADAM_STEP = 0.001, 0.9, 0.999, 1e-08, 0.01, 10

VMEM_LIMIT_V7X = 52 * 1024 * 1024
LANES = 128
PACK_C = 1024
ROW_BLOCK_BYTES = 2 * 1024 * 1024
MM_TM, MM_TN, MM_TK = 1408, 1024, 5632
MM_DEEP = IN_W
MM_TB = 512

NT = (((1,), (1,)), ((), ()))
TN = (((0,), (0,)), ((), ()))
HI = lax.Precision.HIGHEST
MESH = pl.DeviceIdType.MESH

PARAMS = ['g_mix', 'w_in', 'g_q', 'w_uq', 'g_kv', 'w_ukv', 'a_re', 'a_im', 'b_re', 'b_im', 'c_re', 'c_im',
          'd_skip', 'log_dt', 'w_glu', 'b_glu', 'g_out_mla', 'g_out_ssm', 'g_out_dil', 'w_o', 'g_ffn',
          'w_gate', 'w_up', 'w_down', 'g_final']
BIG = {'w_in': 't', 'w_uq': 't', 'w_ukv': 'c', 'w_glu': 'c', 'w_o': 'r', 'w_gate': 't', 'w_up': 't', 'w_down': 'r'}
MIXER_W = ['w_in', 'w_uq', 'w_ukv', 'w_glu', 'w_o']
FFN_W = ['w_gate', 'w_up', 'w_down']
OUT_W, IN_W = ['w_o', 'w_glu'], ['w_in', 'w_uq', 'w_ukv']
SMALL = {'g_mix': (D,), 'g_q': (Q_LORA,), 'g_kv': (KV_LORA,), 'a_re': (SSM_G, SSM_N), 'a_im': (SSM_G, SSM_N),
         'b_re': (SSM_G, SSM_N, SSM_P), 'b_im': (SSM_G, SSM_N, SSM_P), 'c_re': (SSM_G, SSM_P, SSM_N),
         'c_im': (SSM_G, SSM_P, SSM_N), 'd_skip': (SSM_G, SSM_P), 'log_dt': (SSM_G,), 'b_glu': (2 * SSM_W,),
         'g_out_mla': (H_MLA * VDIM,), 'g_out_ssm': (SSM_W,), 'g_out_dil': (DIL_W,), 'g_ffn': (D,)}
SMALL_ROW_ALIGN = 64


def _tile(dim, target, align=LANES):
    best = None
    for t in range(align, min(dim, target) + 1, align):
        if dim % t == 0:
            best = t
    return best if best is not None else dim


def _params(sem=None):
    return pltpu.CompilerParams(dimension_semantics=sem, vmem_limit_bytes=VMEM_LIMIT_V7X)


def _dot(a, b, dims=None, prec=None):
    if dims is None:
        return jnp.dot(a, b, preferred_element_type=F32, precision=prec)
    return lax.dot_general(a, b, dims, preferred_element_type=F32, precision=prec)


def _mm_spec(shape, blk, t_r, t_c, rc):
    if blk is None:
        return pl.BlockSpec((t_r, t_c), rc)
    _, R, C = shape
    if blk == 'r':
        per = R // t_r
        return pl.BlockSpec((1, t_r, t_c), lambda i, j, k: (rc(i, j, k)[0] // per, rc(i, j, k)[0] % per, rc(i, j, k)[1]))
    per = C // t_c
    return pl.BlockSpec((1, t_r, t_c), lambda i, j, k: (rc(i, j, k)[1] // per, rc(i, j, k)[0], rc(i, j, k)[1] % per))


def _logical(shape, blk):
    if blk is None:
        return tuple(shape)
    G, R, C = shape
    return (G * R, C) if blk == 'r' else (R, G * C)


def _mm(name, a, b, mode, ab=None, bb=None, ob=None, res=None, prec=None, epilogue=None):
    la, lb = _logical(a.shape, ab), _logical(b.shape, bb)
    am, ak = (0, 1) if mode != 'tn' else (1, 0)
    bk, bn = (0, 1) if mode != 'nt' else (1, 0)
    M, K, N = la[am], la[ak], lb[bn]
    assert lb[bk] == K, (name, a.shape, b.shape, mode)
    if ob is None:
        out_shape = (M, N)
    elif ob == 'r':
        G = N_DEV
        out_shape = (G, M // G, N)
    else:
        G = N_DEV
        out_shape = (G, M, N // G)
    em = min(a.shape[-2:][am], out_shape[-2])
    en = min(b.shape[-2:][bn], out_shape[-1])
    ek = min(a.shape[-2:][ak], b.shape[-2:][bk])
    dims = {'nn': None, 'nt': NT, 'tn': TN}[mode]
    a_kb = mode != 'tn' and ab == 'c'
    b_kb = (mode == 'nn' and bb == 'r') or (mode == 'nt' and bb == 'c')
    blocks = K // ek if (a_kb or b_kb) else 1
    assert blocks == 1 or ((a_kb or ab is None) and (b_kb or bb is None)), (name, ab, bb, mode)
    tk = ek if blocks > 1 else _tile(ek, MM_TK)
    nk = 1 if blocks > 1 else K // tk
    small = blocks > 1 or nk > 1 or K > MM_DEEP
    tn = _tile(en, MM_TB if small else MM_TN)
    tm = _tile(em, MM_TB if small else (2 * MM_TM if K <= MM_TB else MM_TM))
    if tn > MM_TN:
        tm = _tile(em, MM_TB)
    if tm > 2 * MM_TM:
        tn = _tile(en, MM_TB)

    def val(ref):
        return ref[...] if len(ref.shape) == 2 else ref[0]

    def put(o_ref, r):
        if len(o_ref.shape) == 2:
            o_ref[...] = r
        else:
            o_ref[0] = r

    def k_block(ref, d, blocked, lanes):
        if blocked:
            return ref[d]
        return ref[:, d * ek:(d + 1) * ek] if lanes else ref[d * ek:(d + 1) * ek, :]

    if epilogue is not None:
        epi_fn, extras, out_dtypes = epilogue
        assert nk == 1 and blocks == 1 and ob is None and res is None, name

    def body(*refs):
        if epilogue is not None:
            a_ref, b_ref = refs[:2]
            extra_refs, out_refs = refs[2:2 + len(extras)], refs[2 + len(extras):]
            outs = epi_fn(_dot(val(a_ref), val(b_ref), dims, prec), *[e[...] for e in extra_refs])
            for o_ref, o in zip(out_refs, outs):
                o_ref[...] = o.astype(o_ref.dtype)
            return
        if res is None:
            a_ref, b_ref, o_ref = refs[:3]
            r_ref = None
        else:
            a_ref, b_ref, r_ref, o_ref = refs[:4]
        if blocks > 1:
            part = None
            for d in range(blocks):
                p = _dot(k_block(a_ref, d, a_kb, True), k_block(b_ref, d, b_kb, mode == 'nt'), dims, prec)
                part = p if part is None else part + p
        else:
            part = _dot(val(a_ref), val(b_ref), dims, prec)
        if nk == 1:
            put(o_ref, part if r_ref is None else part + val(r_ref))
            return
        acc_ref = refs[-1]
        k = pl.program_id(2)

        @pl.when(k == 0)
        def _():
            acc_ref[...] = part

        @pl.when((k > 0) & (k < nk - 1))
        def _():
            acc_ref[...] += part

        @pl.when(k == nk - 1)
        def _():
            r = acc_ref[...] + part
            put(o_ref, r if r_ref is None else r + val(r_ref))

    if blocks > 1:
        G = blocks
        a_spec = (pl.BlockSpec((G, tm, ek), lambda i, j, k: (0, i, 0)) if a_kb
                  else pl.BlockSpec((tm, K), lambda i, j, k: (i, 0)))
        if b_kb:
            b_spec = (pl.BlockSpec((G, ek, tn), lambda i, j, k: (0, 0, j)) if mode == 'nn'
                      else pl.BlockSpec((G, tn, ek), lambda i, j, k: (0, j, 0)))
        else:
            b_spec = (pl.BlockSpec((K, tn), lambda i, j, k: (0, j)) if mode == 'nn'
                      else pl.BlockSpec((tn, K), lambda i, j, k: (j, 0)))
    else:
        if mode == 'tn':
            a_spec = _mm_spec(a.shape, ab, tk, tm, lambda i, j, k: (k, i))
        else:
            a_spec = _mm_spec(a.shape, ab, tm, tk, lambda i, j, k: (i, k))
        if mode == 'nt':
            b_spec = _mm_spec(b.shape, bb, tn, tk, lambda i, j, k: (j, k))
        else:
            b_spec = _mm_spec(b.shape, bb, tk, tn, lambda i, j, k: (k, j))
    o_spec = _mm_spec(out_shape, ob, tm, tn, lambda i, j, k: (i, j))
    if epilogue is not None:
        return pl.pallas_call(
            body, name=name, out_shape=[jax.ShapeDtypeStruct(out_shape, dt) for dt in out_dtypes],
            grid=(M // tm, N // tn, nk), in_specs=[a_spec, b_spec] + [o_spec] * len(extras),
            out_specs=[o_spec] * len(out_dtypes),
            compiler_params=_params(("parallel", "parallel", "arbitrary")),
        )(a, b, *extras)
    in_specs = [a_spec, b_spec] + ([o_spec] if res is not None else [])
    args = (a, b) + ((res,) if res is not None else ())
    return pl.pallas_call(
        body, name=name, out_shape=jax.ShapeDtypeStruct(out_shape, F32),
        grid=(M // tm, N // tn, nk), in_specs=in_specs, out_specs=o_spec,
        scratch_shapes=[pltpu.VMEM((tm, tn), F32)] if nk > 1 else [],
        compiler_params=_params(("parallel", "parallel", "arbitrary")),
    )(*args)


def _row(a, tr=None, axis=-2):
    axis = axis % a.ndim
    n = a.shape[axis]
    if tr is None:
        row_bytes = a.size // n * 4
        tr = _tile(n, max(8, min(256, ROW_BLOCK_BYTES // row_bytes)), 8)
    return (a, axis, tr)


def _row_spec(shape, axis, tr):
    nd = len(shape)
    blk = tuple(tr if d == axis else s for d, s in enumerate(shape))
    return pl.BlockSpec(blk, lambda i: tuple(i if d == axis else 0 for d in range(nd)))


def _full_spec(shape):
    nd = len(shape)
    return pl.BlockSpec(tuple(shape), lambda i: (0,) * nd)


def _steps(entries):
    ns = {a.shape[ax] // tr for a, ax, tr in entries}
    assert len(ns) == 1, [(a.shape, ax, tr) for a, ax, tr in entries]
    return ns.pop()


def _as_tuple(r):
    return tuple(r) if isinstance(r, (tuple, list)) else (r,)


def _rows_fwd(name, fn, rows, bcast, outs):
    steps = _steps(rows)
    nr, nb = len(rows), len(bcast)

    def body(*refs):
        vals = [r[...] for r in refs[:nr + nb]]
        res = _as_tuple(fn(*vals))
        for o_ref, r in zip(refs[nr + nb:], res):
            o_ref[...] = r.astype(o_ref.dtype)

    in_specs = [_row_spec(a.shape, ax, tr) for a, ax, tr in rows] + [_full_spec(b.shape) for b in bcast]
    out_specs = [_row_spec(s, ax % len(s), tr) for s, _, ax, tr in outs]
    res = pl.pallas_call(
        body, name=name, out_shape=[jax.ShapeDtypeStruct(s, dt) for s, dt, _, _ in outs],
        grid=(steps,), in_specs=in_specs, out_specs=out_specs,
        compiler_params=_params(("parallel",)),
    )(*[a for a, _, _ in rows], *bcast)
    return res


def _rows_vjp(name, fn, drows, dbc, arows, cts, primal=False, grad_dtypes=None):
    entries = list(drows) + list(arows) + list(cts)
    steps = _steps(entries)
    ndr, ndb, nar, nct = len(drows), len(dbc), len(arows), len(cts)
    gdt = list(grad_dtypes) if grad_dtypes is not None else [F32] * ndr

    def body(*refs):
        p = 0
        dr = [r[...] for r in refs[p:p + ndr]]; p += ndr
        db = [r[...] for r in refs[p:p + ndb]]; p += ndb
        ar = [r[...] for r in refs[p:p + nar]]; p += nar
        ct = [r[...] for r in refs[p:p + nct]]; p += nct
        g_rows = refs[p:p + ndr]; p += ndr
        g_bc = refs[p:p + ndb]; p += ndb
        prim_refs = refs[p:]

        def f(*d):
            return _as_tuple(fn(*d, *ar))

        outs, pullback = jax.vjp(f, *dr, *db)
        grads = pullback(tuple(c.astype(o.dtype) for c, o in zip(ct, outs)))
        for k in range(ndr):
            g_rows[k][...] = grads[k].astype(g_rows[k].dtype)
        if ndb:
            @pl.when(pl.program_id(0) == 0)
            def _():
                for r in g_bc:
                    r[...] = jnp.zeros_like(r)
            for k in range(ndb):
                g_bc[k][...] += grads[ndr + k]
        for r, o in zip(prim_refs, outs):
            r[...] = o.astype(r.dtype)

    in_specs = ([_row_spec(a.shape, ax, tr) for a, ax, tr in drows] + [_full_spec(b.shape) for b in dbc]
                + [_row_spec(a.shape, ax, tr) for a, ax, tr in arows]
                + [_row_spec(a.shape, ax, tr) for a, ax, tr in cts])
    out_shape = ([jax.ShapeDtypeStruct(a.shape, dt) for (a, _, _), dt in zip(drows, gdt)]
                 + [jax.ShapeDtypeStruct(b.shape, F32) for b in dbc])
    out_specs = ([_row_spec(a.shape, ax, tr) for a, ax, tr in drows] + [_full_spec(b.shape) for b in dbc])
    if primal:
        out_shape += [jax.ShapeDtypeStruct(a.shape, F32) for a, _, _ in cts]
        out_specs += [_row_spec(a.shape, ax, tr) for a, ax, tr in cts]
    return pl.pallas_call(
        body, name=name, out_shape=out_shape, grid=(steps,), in_specs=in_specs, out_specs=out_specs,
        compiler_params=_params(("arbitrary",)),
    )(*[a for a, _, _ in drows], *dbc, *[a for a, _, _ in arows], *[a for a, _, _ in cts])


def _rms_fn(x, g):
    return x * lax.rsqrt(jnp.mean(x * x, axis=-1, keepdims=True) + EPS) * g


def _rms_res_fn(x, g):
    return _rms_fn(x, g), x


def _s5_act_fn(ymm, u, d):
    return jax.nn.gelu(ymm + d * u)


def _glu_fn(z1, z2, b1, b2):
    return (z1 + b1) * jax.nn.sigmoid(z2 + b2)


def _outnorm_fn(ym, ys, yd, gm, gs, gd):
    return jnp.concatenate([_rms_fn(ym, gm), _rms_fn(ys, gs), _rms_fn(yd, gd)], axis=-1)


def _swiglu_fn(a, b):
    return jax.nn.silu(a) * b


def _loss_fn(x, g, tgt):
    err = _rms_fn(x, g) - tgt
    return 0.5 * jnp.mean(err * err, axis=-1, keepdims=True)


def _dil_mix_fn(o0, o1, o2, l0, l1, l2):
    m = jnp.maximum(jnp.maximum(l0, l1), l2)
    e0, e1, e2 = jnp.exp(l0 - m), jnp.exp(l1 - m), jnp.exp(l2 - m)
    s = e0 + e1 + e2
    return (e0 / s) * o0 + (e1 / s) * o1 + (e2 / s) * o2


def _s5_disc_fn(a_re, a_im, ldt, b_r, b_i):
    lr = jnp.minimum(a_re.reshape(1, SSM_N), -1e-4)
    li = a_im.reshape(1, SSM_N)
    dt = jnp.exp(ldt.reshape(1, 1))
    e = jnp.exp(lr * dt)
    ar = e * jnp.cos(li * dt)
    ai = e * jnp.sin(li * dt)
    nr, ni = ar - 1.0, ai
    den = lr * lr + li * li
    cr = (nr * lr + ni * li) / den
    ci = (ni * lr - nr * li) / den
    return ar.reshape(1, 1, SSM_N), ai.reshape(1, 1, SSM_N), cr * b_r - ci * b_i, cr * b_i + ci * b_r


def _adamw_fn(w, g, m, v):
    m = ADAM_B1 * m + (1.0 - ADAM_B1) * g
    v = ADAM_B2 * v + (1.0 - ADAM_B2) * jnp.square(g)
    m_hat = m / (1.0 - ADAM_B1 ** ADAM_STEP)
    v_hat = v / (1.0 - ADAM_B2 ** ADAM_STEP)
    delta = -ADAM_LR * (m_hat / (jnp.sqrt(v_hat) + ADAM_EPS) + ADAM_WD * w)
    return delta, m, v


def _rms_fwd(name, x, g):
    (h,) = _rows_fwd(name, _rms_fn, [_row(x)], [g], [(x.shape, BF16, -2, _row(x)[2])])
    return h


def _rms_bwd(name, x, g, dh, dres=None):
    if dres is None:
        dx, dg = _rows_vjp(name, _rms_fn, [_row(x)], [g], [], [_row(dh)])
    else:
        dx, dg = _rows_vjp(name, _rms_res_fn, [_row(x)], [g], [], [_row(dh), _row(dres)])
    return dx, dg


MLA_TQ = 256
MLA_EXT = 512


def _mla_fwd(q, k, v):
    tq = MLA_TQ

    def body(q_ref, k_ref, v_ref, o_ref, lse_ref):
        i = pl.program_id(1)
        q = q_ref[0]

        def rows_below(ext):
            s = _dot(q, k_ref[0, :ext, :], NT) * MLA_SCALE
            row = i * tq + lax.broadcasted_iota(jnp.int32, (tq, ext), 0)
            col = lax.broadcasted_iota(jnp.int32, (tq, ext), 1)
            s = jnp.where(row >= col, s, -jnp.inf)
            m = jnp.max(s, axis=-1, keepdims=True)
            p = jnp.exp(s - m)
            l = jnp.sum(p, axis=-1, keepdims=True)
            o_ref[...] = _dot((p / l).astype(BF16), v_ref[0, :ext, :])
            lse_ref[0] = m + jnp.log(l)

        for g in range(T // MLA_EXT):
            pl.when(i // (MLA_EXT // tq) == g)(lambda g=g: rows_below((g + 1) * MLA_EXT))

    return pl.pallas_call(
        body, name="mla_fwd",
        out_shape=[jax.ShapeDtypeStruct((T, H_MLA * VDIM), F32), jax.ShapeDtypeStruct((H_MLA, T, 1), F32)],
        grid=(H_MLA, T // tq),
        in_specs=[pl.BlockSpec((1, tq, QK), lambda h, i: (h, i, 0)),
                  pl.BlockSpec((1, T, QK), lambda h, i: (h, 0, 0)),
                  pl.BlockSpec((1, T, VDIM), lambda h, i: (h, 0, 0))],
        out_specs=[pl.BlockSpec((tq, VDIM), lambda h, i: (i, h)),
                   pl.BlockSpec((1, tq, 1), lambda h, i: (h, i, 0))],
        compiler_params=_params(("parallel", "parallel")),
    )(q, k, v)


def _mla_bwd(q, k, v, o, do, lse):
    tq = MLA_TQ

    def body(q_ref, k_ref, v_ref, o_ref, do_ref, lse_ref, dq_ref, dk_ref, dv_ref):
        i = pl.program_id(1)

        @pl.when(i == 0)
        def _():
            dk_ref[...] = jnp.zeros_like(dk_ref)
            dv_ref[...] = jnp.zeros_like(dv_ref)

        q, lse = q_ref[0], lse_ref[0]
        delta = jnp.sum(do_ref[...] * o_ref[...], axis=-1, keepdims=True)
        do = do_ref[...].astype(BF16)

        def rows_below(ext):
            k, v = k_ref[0, :ext, :], v_ref[0, :ext, :]
            s = _dot(q, k, NT) * MLA_SCALE
            row = i * tq + lax.broadcasted_iota(jnp.int32, (tq, ext), 0)
            col = lax.broadcasted_iota(jnp.int32, (tq, ext), 1)
            p = jnp.where(row >= col, jnp.exp(s - lse), 0.0)
            ds = (p * (_dot(do, v, NT) - delta) * MLA_SCALE).astype(BF16)
            dq_ref[0] = _dot(ds, k)
            dk_ref[0, :ext, :] += _dot(ds, q, TN)
            dv_ref[0, :ext, :] += _dot(p.astype(BF16), do, TN)

        for g in range(T // MLA_EXT):
            pl.when(i // (MLA_EXT // tq) == g)(lambda g=g: rows_below((g + 1) * MLA_EXT))

    return pl.pallas_call(
        body, name="mla_bwd",
        out_shape=[jax.ShapeDtypeStruct((H_MLA, T, QK), F32), jax.ShapeDtypeStruct((H_MLA, T, QK), F32),
                   jax.ShapeDtypeStruct((H_MLA, T, VDIM), F32)],
        grid=(H_MLA, T // tq),
        in_specs=[pl.BlockSpec((1, tq, QK), lambda h, i: (h, i, 0)),
                  pl.BlockSpec((1, T, QK), lambda h, i: (h, 0, 0)),
                  pl.BlockSpec((1, T, VDIM), lambda h, i: (h, 0, 0)),
                  pl.BlockSpec((tq, VDIM), lambda h, i: (i, h)),
                  pl.BlockSpec((tq, VDIM), lambda h, i: (i, h)),
                  pl.BlockSpec((1, tq, 1), lambda h, i: (h, i, 0))],
        out_specs=[pl.BlockSpec((1, tq, QK), lambda h, i: (h, i, 0)),
                   pl.BlockSpec((1, T, QK), lambda h, i: (h, 0, 0)),
                   pl.BlockSpec((1, T, VDIM), lambda h, i: (h, 0, 0))],
        compiler_params=_params(("parallel", "arbitrary")),
    )(q, k, v, o, do, lse)


NBLK = T // BLK


BAND_GL = 256
BAND_GH = BAND_GL // DIL_D
BAND_ROWS = BAND_GH * BLK
BAND_GROUPS = DIL_W // BAND_GL


def _band_masks():
    r = lax.broadcasted_iota(jnp.int32, (BAND_ROWS, BLK), 0) & (BLK - 1)
    j = lax.broadcasted_iota(jnp.int32, (BAND_ROWS, BLK), 1)
    return j <= r, j >= r


def _head_lanes():
    lane_head = lax.broadcasted_iota(jnp.int32, (1, BAND_GL), 1) // DIL_D
    return [lane_head == h for h in range(BAND_GH)]


def _stack_heads(x, lanes):
    return jnp.concatenate([jnp.where(m, x, jnp.zeros_like(x)) for m in lanes], axis=0)


def _merge_heads(xs, lanes):
    out = None
    for h, m in enumerate(lanes):
        part = jnp.where(m, xs[h * BLK:(h + 1) * BLK], 0.0)
        out = part if out is None else out + part
    return out


def _per_head(x, lanes):
    return jnp.concatenate([jnp.sum(jnp.where(m, x, 0.0), axis=-1, keepdims=True) for m in lanes], axis=0)


def _lane_group(ref, g):
    return ref[0, :, g * BAND_GL:(g + 1) * BAND_GL]


def _seq_start(p, i):
    per_seq = lax.shift_right_logical(jnp.int32(NBLK), 2 * p)
    return lax.rem(i, per_seq) == 0


def _band_fwd(q, k, v):
    def body(q_ref, kp_ref, kc_ref, vp_ref, vc_ref, o_ref, lse_ref):
        p, i = pl.program_id(0), pl.program_id(1)
        has_prev = jnp.logical_not(_seq_start(p, i))
        m_cur, m_prev = _band_masks()
        m_prev = m_prev & has_prev
        lanes = _head_lanes()
        for g in range(BAND_GROUPS):
            qs = _stack_heads(_lane_group(q_ref, g), lanes)
            s_c = jnp.where(m_cur, _dot(qs, _lane_group(kc_ref, g), NT) * DIL_SCALE, -jnp.inf)
            s_p = jnp.where(m_prev, _dot(qs, _lane_group(kp_ref, g), NT) * DIL_SCALE, -jnp.inf)
            m = jnp.maximum(jnp.max(s_c, axis=-1, keepdims=True), jnp.max(s_p, axis=-1, keepdims=True))
            e_c, e_p = jnp.exp(s_c - m), jnp.exp(s_p - m)
            l = jnp.sum(e_c, axis=-1, keepdims=True) + jnp.sum(e_p, axis=-1, keepdims=True)
            os = (_dot((e_p / l).astype(BF16), _lane_group(vp_ref, g))
                  + _dot((e_c / l).astype(BF16), _lane_group(vc_ref, g)))
            cols = slice(g * BAND_GL, (g + 1) * BAND_GL)
            o_ref[0, :, cols] = _merge_heads(os, lanes)
            lse_ref[0, :, cols] = _merge_heads(m + jnp.log(l), lanes)

    blk = (1, BLK, DIL_W)
    cur = lambda p, i: (p, i, 0)
    prev = lambda p, i: (p, jnp.maximum(i - 1, 0), 0)
    return pl.pallas_call(
        body, name="band_fwd",
        out_shape=[jax.ShapeDtypeStruct((3, T, DIL_W), F32)] * 2,
        grid=(3, NBLK),
        in_specs=[pl.BlockSpec(blk, cur), pl.BlockSpec(blk, prev), pl.BlockSpec(blk, cur),
                  pl.BlockSpec(blk, prev), pl.BlockSpec(blk, cur)],
        out_specs=[pl.BlockSpec(blk, cur), pl.BlockSpec(blk, cur)],
        compiler_params=_params(("parallel", "parallel")),
    )(q, k, k, v, v)


def _band_bwd(q, k, v, o, lse, do, dlse):
    def body(qc_ref, qn_ref, kp_ref, kc_ref, vp_ref, vc_ref, oc_ref, on_ref, lc_ref, ln_ref,
             doc_ref, don_ref, dlc_ref, dln_ref, dq_ref, dk_ref, dv_ref):
        p, i = pl.program_id(0), pl.program_id(1)
        has_prev = jnp.logical_not(_seq_start(p, i))
        has_next = jnp.logical_not(_seq_start(p, i + 1)) & (i + 1 < NBLK)
        m_cur, m_prev = _band_masks()
        lanes = _head_lanes()

        def probs(qs, k, lse, mask):
            return jnp.where(mask, jnp.exp(_dot(qs, k, NT) * DIL_SCALE - lse), 0.0)

        def dscore(pr, dos, v, shift):
            return (pr * (_dot(dos, v, NT) + shift) * DIL_SCALE).astype(BF16)

        for g in range(BAND_GROUPS):
            grp = lambda ref: _lane_group(ref, g)
            kp, kc, vp, vc = grp(kp_ref), grp(kc_ref), grp(vp_ref), grp(vc_ref)
            qc, qn = _stack_heads(grp(qc_ref), lanes), _stack_heads(grp(qn_ref), lanes)
            doc, don = grp(doc_ref), grp(don_ref)
            lse_c = _per_head(grp(lc_ref), lanes) * (1.0 / DIL_D)
            lse_n = _per_head(grp(ln_ref), lanes) * (1.0 / DIL_D)
            sh_c = _per_head(grp(dlc_ref) - doc * grp(oc_ref), lanes)
            sh_n = _per_head(grp(dln_ref) - don * grp(on_ref), lanes)
            doc, don = _stack_heads(doc.astype(BF16), lanes), _stack_heads(don.astype(BF16), lanes)
            p_cc = probs(qc, kc, lse_c, m_cur)
            p_cp = probs(qc, kp, lse_c, m_prev & has_prev)
            p_nc = probs(qn, kc, lse_n, m_prev & has_next)
            ds_cc = dscore(p_cc, doc, vc, sh_c)
            ds_cp = dscore(p_cp, doc, vp, sh_c)
            ds_nc = dscore(p_nc, don, vc, sh_n)
            cols = slice(g * BAND_GL, (g + 1) * BAND_GL)
            dq_ref[0, :, cols] = _merge_heads(_dot(ds_cc, kc) + _dot(ds_cp, kp), lanes)
            dk_ref[0, :, cols] = _dot(ds_cc, qc, TN) + _dot(ds_nc, qn, TN)
            dv_ref[0, :, cols] = _dot(p_cc.astype(BF16), doc, TN) + _dot(p_nc.astype(BF16), don, TN)

    blk = (1, BLK, DIL_W)
    cur = lambda p, i: (p, i, 0)
    prev = lambda p, i: (p, jnp.maximum(i - 1, 0), 0)
    nxt = lambda p, i: (p, jnp.minimum(i + 1, NBLK - 1), 0)
    w, wn, wp = pl.BlockSpec(blk, cur), pl.BlockSpec(blk, nxt), pl.BlockSpec(blk, prev)
    return pl.pallas_call(
        body, name="band_bwd",
        out_shape=[jax.ShapeDtypeStruct((3, T, DIL_W), F32)] * 3,
        grid=(3, NBLK),
        in_specs=[w, wn, wp, w, wp, w, w, wn, w, wn, w, wn, w, wn],
        out_specs=[w, w, w],
        compiler_params=_params(("parallel", "parallel")),
    )(q, q, k, k, v, v, o, o, lse, lse, do, do, dlse, dlse)


SCAN_TC = 256


def _scan_fwd(bu, ar, ai):
    tc, S = SCAN_TC, SSM_S

    def body(bu_ref, ar_ref, ai_ref, h_ref, h16_ref, cr_ref, ci_ref):
        @pl.when(pl.program_id(0) == 0)
        def _():
            cr_ref[...] = jnp.zeros_like(cr_ref)
            ci_ref[...] = jnp.zeros_like(ci_ref)

        a_r, a_i = ar_ref[...], ai_ref[...]

        def step(j, carry):
            hr, hi = carry
            for r in range(8):
                t = pl.multiple_of(j * 8, 8) + r
                br = bu_ref[pl.ds(t, 1), pl.ds(0, S)]
                bi = bu_ref[pl.ds(t, 1), pl.ds(S, S)]
                hr, hi = a_r * hr - a_i * hi + br, a_r * hi + a_i * hr + bi
                h_ref[pl.ds(t, 1), pl.ds(0, S)] = hr
                h_ref[pl.ds(t, 1), pl.ds(S, S)] = hi
            return hr, hi

        hr, hi = lax.fori_loop(0, tc // 8, step, (cr_ref[...], ci_ref[...]))
        cr_ref[...] = hr
        ci_ref[...] = hi
        h16_ref[...] = h_ref[...].astype(BF16)

    return pl.pallas_call(
        body, name="s5_scan_fwd",
        out_shape=[jax.ShapeDtypeStruct((T, 2 * S), F32), jax.ShapeDtypeStruct((T, 2 * S), BF16)],
        grid=(T // tc,),
        in_specs=[pl.BlockSpec((tc, 2 * S), lambda i: (i, 0)), _full_spec((1, S)), _full_spec((1, S))],
        out_specs=[pl.BlockSpec((tc, 2 * S), lambda i: (i, 0)), pl.BlockSpec((tc, 2 * S), lambda i: (i, 0))],
        scratch_shapes=[pltpu.VMEM((1, S), F32), pltpu.VMEM((1, S), F32)],
        compiler_params=_params(("arbitrary",)),
    )(bu, ar, ai)


def _scan_bwd(dh, h, ar, ai):
    tc, S = SCAN_TC, SSM_S
    nc = T // tc

    def body(dh_ref, h_ref, hp_ref, ar_ref, ai_ref, g16_ref, dar_ref, dai_ref, cr_ref, ci_ref, g_ref):
        i = pl.program_id(0)

        @pl.when(i == 0)
        def _():
            cr_ref[...] = jnp.zeros_like(cr_ref)
            ci_ref[...] = jnp.zeros_like(ci_ref)
            dar_ref[...] = jnp.zeros_like(dar_ref)
            dai_ref[...] = jnp.zeros_like(dai_ref)

        a_r, a_i = ar_ref[...], ai_ref[...]
        first_chunk = (i == nc - 1)
        edge = jnp.where(first_chunk, 0.0, 1.0)
        hpr = hp_ref[pl.ds(7, 1), pl.ds(0, S)] * edge
        hpi = hp_ref[pl.ds(7, 1), pl.ds(S, S)] * edge

        def step(jj, carry):
            gr, gi, dar, dai = carry
            j = tc // 8 - 1 - jj
            for r in range(7, -1, -1):
                t = pl.multiple_of(j * 8, 8) + r
                tp = jnp.maximum(t - 1, 0)
                inside = t > 0
                pr = jnp.where(inside, h_ref[pl.ds(tp, 1), pl.ds(0, S)], hpr)
                pi = jnp.where(inside, h_ref[pl.ds(tp, 1), pl.ds(S, S)], hpi)
                gr, gi = (dh_ref[pl.ds(t, 1), pl.ds(0, S)] + a_r * gr + a_i * gi,
                          dh_ref[pl.ds(t, 1), pl.ds(S, S)] + a_r * gi - a_i * gr)
                g_ref[pl.ds(t, 1), pl.ds(0, S)] = gr
                g_ref[pl.ds(t, 1), pl.ds(S, S)] = gi
                dar = dar + gr * pr + gi * pi
                dai = dai + gi * pr - gr * pi
            return gr, gi, dar, dai

        zero = jnp.zeros((1, S), F32)
        gr, gi, dar, dai = lax.fori_loop(0, tc // 8, step, (cr_ref[...], ci_ref[...], zero, zero))
        cr_ref[...] = gr
        ci_ref[...] = gi
        dar_ref[...] += dar
        dai_ref[...] += dai
        g16_ref[...] = g_ref[...].astype(BF16)

    rev = lambda i: (nc - 1 - i, 0)
    before = lambda i: (jnp.maximum((nc - 1 - i) * (tc // 8) - 1, 0), 0)
    return pl.pallas_call(
        body, name="s5_scan_bwd",
        out_shape=[jax.ShapeDtypeStruct((T, 2 * S), BF16), jax.ShapeDtypeStruct((1, S), F32),
                   jax.ShapeDtypeStruct((1, S), F32)],
        grid=(nc,),
        in_specs=[pl.BlockSpec((tc, 2 * S), rev), pl.BlockSpec((tc, 2 * S), rev), pl.BlockSpec((8, 2 * S), before),
                  _full_spec((1, S)), _full_spec((1, S))],
        out_specs=[pl.BlockSpec((tc, 2 * S), rev), _full_spec((1, S)), _full_spec((1, S))],
        scratch_shapes=[pltpu.VMEM((1, S), F32), pltpu.VMEM((1, S), F32), pltpu.VMEM((tc, 2 * S), F32)],
        compiler_params=_params(("arbitrary",)),
    )(dh, h, h, ar, ai)


def _sum_rows(name, x):
    def body(x_ref, o_ref):
        o_ref[...] = jnp.sum(x_ref[...], axis=0, keepdims=True)

    return pl.pallas_call(body, name=name, out_shape=jax.ShapeDtypeStruct((1, 1), F32),
                          in_specs=[_full_spec(x.shape)], out_specs=_full_spec((1, 1)), grid=(1,))(x)


ANY = pl.BlockSpec(memory_space=pl.ANY)


def _place():
    return lax.axis_index("x"), lax.axis_index("y"), lax.axis_index("c")


HBM = pl.BlockSpec(memory_space=pltpu.HBM)
SEM = pl.BlockSpec(memory_space=pltpu.SEMAPHORE)
DATAFLOW = pltpu.SideEffectType.DATAFLOW_SIDE_EFFECTING


def _hbm(a):
    return pltpu.with_memory_space_constraint(a, pltpu.HBM)


def _split_start(name, srcs, lands, ncopies, plan, after):
    ns, nl = len(srcs), len(lands)

    def body(*refs):
        send_sems, recv_sems = refs[ns + nl + 1], refs[ns + nl + 2]
        token = refs[-1]
        for k, (src, dst, peer, _) in enumerate(plan(refs[:ns], refs[ns:ns + nl])):
            pltpu.make_async_remote_copy(src_ref=src, dst_ref=dst, send_sem=send_sems.at[k], recv_sem=recv_sems.at[k],
                                         device_id=peer, device_id_type=MESH).start()
        token[...] = jnp.zeros_like(token)

    out = pl.pallas_call(
        body, name=name,
        out_shape=(pltpu.SemaphoreType.DMA((ncopies,)), pltpu.SemaphoreType.DMA((ncopies,)),
                   *[pltpu.HBM(a.shape, a.dtype) for a in srcs], *[pltpu.HBM(a.shape, a.dtype) for a in lands],
                   jax.ShapeDtypeStruct((8, LANES), F32)),
        in_specs=[HBM] * (ns + nl) + [ANY],
        out_specs=(SEM, SEM, *[HBM] * (ns + nl), pl.BlockSpec(memory_space=pltpu.VMEM)),
        input_output_aliases={i: 2 + i for i in range(ns + nl)},
        compiler_params=pltpu.CompilerParams(has_side_effects=DATAFLOW),
    )(*[_hbm(a) for a in srcs], *[_hbm(a) for a in lands], after)
    return out[0], out[1], list(out[2:2 + ns]), list(out[2 + ns:2 + ns + nl]), out[-1]


def _split_wait(name, handle, ncopies, plan, after):
    send_sems, recv_sems, srcs, lands, _ = handle
    ns, nl = len(srcs), len(lands)

    def body(*refs):
        s_sems, r_sems = refs[ns + nl], refs[ns + nl + 1]
        for k, (src, dst, peer, mine) in enumerate(plan(refs[:ns], refs[ns:ns + nl])):
            pltpu.make_async_remote_copy(src_ref=src, dst_ref=dst, send_sem=s_sems.at[k], recv_sem=r_sems.at[k],
                                         device_id=peer, device_id_type=MESH).wait_send()
            pltpu.make_async_remote_copy(src_ref=src, dst_ref=mine, send_sem=s_sems.at[k], recv_sem=r_sems.at[k],
                                         device_id=peer, device_id_type=MESH).wait_recv()

    out = pl.pallas_call(
        body, name=name,
        out_shape=(*[pltpu.HBM(a.shape, a.dtype) for a in srcs], *[pltpu.HBM(a.shape, a.dtype) for a in lands]),
        in_specs=[HBM] * (ns + nl) + [SEM, SEM, ANY],
        out_specs=tuple([HBM] * (ns + nl)),
        input_output_aliases={i: i for i in range(ns + nl)},
        compiler_params=pltpu.CompilerParams(has_side_effects=DATAFLOW),
    )(*srcs, *lands, send_sems, recv_sems, after)
    return list(out[:ns]), list(out[ns:])


def _slot(px, py, pc):
    return 4 * px + 2 * py + pc


def _gather_plan(xs, lands):
    x, y, c = _place()
    peers = [(x, y, 1 - c), (1 - x, y, c), (x, 1 - y, c), (1 - x, 1 - y, c)]
    return [(xs[t], lands[t].at[_slot(x, y, c)], peer, lands[t].at[_slot(*peer)])
            for t in range(len(xs)) for peer in peers]


def _gather_start(name, shards, after):
    lands = [lax.empty((N_DEV,) + s.shape, s.dtype) for s in shards]
    return _split_start(name, shards, lands, 4 * len(shards), _gather_plan, after)


def _pass_on_plan(_, lands):
    x, y, c = _place()
    blocks = [((1 - x, y, c), (1 - x, y, 1 - c)), ((x, 1 - y, c), (x, 1 - y, 1 - c)),
              ((1 - x, 1 - y, c), (1 - x, 1 - y, 1 - c)), ((x, y, 1 - c), (x, y, c))]
    return [(lands[t].at[_slot(*out)], lands[t].at[_slot(*out)], (x, y, 1 - c), lands[t].at[_slot(*back)])
            for t in range(len(lands)) for out, back in blocks]


def _gather_pass_on(name, handle, after):
    n = len(handle[2])
    _, lands = _split_wait(name + "_wait", handle, 4 * n, _gather_plan, after)
    return _split_start(name + "_pass_on", [], lands, 4 * n, _pass_on_plan, after)


def _gather_finish(name, handle, after):
    n = len(handle[3])
    _, lands = _split_wait(name + "_done", handle, 4 * n, _pass_on_plan, after)
    return lands


def _sibling_plan(ps, gots):
    x, y, c = _place()
    return [(ps[t].at[j, 1 - c], gots[t].at[j], (x, y, 1 - c), gots[t].at[j]) for t in range(len(ps)) for j in range(4)]


def _chips_plan(ss, gots):
    x, y, c = _place()
    chips = [(1 - x, y), (x, 1 - y), (1 - x, 1 - y)]
    return [(ss[t].at[2 * px + py], gots[t].at[k], (px, py, c), gots[t].at[k])
            for t in range(len(ss)) for k, (px, py) in enumerate(chips)]


BF16_ROWS = 16


def _shard_tiles(R, C, block_bytes):
    tr = _tile(R, max(BF16_ROWS, min(1024, block_bytes // (C * 4))), BF16_ROWS)
    tc = C if tr * C * 4 <= 2 * block_bytes else _tile(C, max(LANES, block_bytes // (tr * 4)))
    return tr, tc


def _pair_sum(name, p4, got, core):
    _, _, R, C = p4.shape
    tr, tc = _shard_tiles(R, C, ROW_BLOCK_BYTES)

    def body(core_ref, p_ref, g_ref, o_ref):
        o_ref[...] = (p_ref[:, 0] + g_ref[...]).astype(o_ref.dtype)

    return pl.pallas_call(
        body, name=name, out_shape=jax.ShapeDtypeStruct((4, R, C), BF16),
        grid_spec=pltpu.PrefetchScalarGridSpec(
            num_scalar_prefetch=1, grid=(4, R // tr, C // tc),
            in_specs=[pl.BlockSpec((1, 1, tr, tc), lambda j, i, k, core: (j, core[0], i, k)),
                      pl.BlockSpec((1, tr, tc), lambda j, i, k, core: (j, i, k))],
            out_specs=pl.BlockSpec((1, tr, tc), lambda j, i, k, core: (j, i, k))),
        compiler_params=_params(("parallel", "parallel", "parallel")),
    )(core, p4, got)


def _sum_devices(name, g8):
    _, R, C = g8.shape
    tr = _tile(R, SMALL_ROW_ALIGN, 8)

    def body(g_ref, o_ref):
        acc = g_ref[0]
        for d in range(1, N_DEV):
            acc = acc + g_ref[d]
        o_ref[...] = acc

    return pl.pallas_call(
        body, name=name, out_shape=jax.ShapeDtypeStruct((R, C), F32), grid=(R // tr,),
        in_specs=[pl.BlockSpec((N_DEV, tr, C), lambda i: (0, i, 0))], out_specs=pl.BlockSpec((tr, C), lambda i: (i, 0)),
        compiler_params=_params(("parallel",)),
    )(g8)


def _adamw_shard(name, layer, w, m, v, s4, got, chip, bufs):
    _, R, C = w.shape
    tr, tc = _shard_tiles(R, C, ROW_BLOCK_BYTES // 2)

    def body(chip_ref, w_ref, m_ref, v_ref, s_ref, g_ref, b0, b1, b2, b3, og, od, om, ov):
        f = lambda a: a.astype(F32)
        g = ((f(s_ref[0]) + f(g_ref[0])) + f(g_ref[1])) + f(g_ref[2])
        d, nm, nv = _adamw_fn(w_ref[0], g, m_ref[0], v_ref[0])
        og[0], od[0], om[0], ov[0] = g, d, nm, nv

    lay = pl.BlockSpec((1, tr, tc), lambda i, k, chip: (layer, i, k))
    if bufs is None:
        bufs = [lax.empty(w.shape, F32) for _ in range(4)]
    return pl.pallas_call(
        body, name=name, out_shape=[jax.ShapeDtypeStruct(w.shape, F32)] * 4,
        grid_spec=pltpu.PrefetchScalarGridSpec(
            num_scalar_prefetch=1, grid=(R // tr, C // tc),
            in_specs=[lay, lay, lay, pl.BlockSpec((1, tr, tc), lambda i, k, chip: (chip[0], i, k)),
                      pl.BlockSpec((3, tr, tc), lambda i, k, chip: (0, i, k)), ANY, ANY, ANY, ANY],
            out_specs=[lay] * 4),
        input_output_aliases={6: 0, 7: 1, 8: 2, 9: 3},
        compiler_params=_params(("parallel", "parallel")),
    )(chip, w, m, v, s4, got, *bufs)


def _adamw(name, wt, g, m, v):
    shape = wt.shape
    two = (lambda a: a.reshape(1, -1)) if wt.ndim == 1 else (lambda a: a.reshape(-1, shape[-1]))
    w2, g2, m2, v2 = two(wt), two(g), two(m), two(v)
    tr = _row(w2, None, 0)[2]
    outs = [(w2.shape, F32, 0, tr)] * 3
    d, nm, nv = _rows_fwd(name, _adamw_fn, [_row(a, tr, 0) for a in (w2, g2, m2, v2)], [], outs)
    return d.reshape(shape), nm.reshape(shape), nv.reshape(shape)


def _lane_tiling():
    return (jnp.arange(SSM_N)[:, None] == (jnp.arange(SSM_S) % SSM_N)[None, :]).astype(BF16)


def _own_block():
    r = lax.broadcasted_iota(jnp.int32, (SSM_G * SSM_P, SSM_S), 0) // SSM_P
    c = lax.broadcasted_iota(jnp.int32, (SSM_G * SSM_P, SSM_S), 1) // SSM_N
    return r == c


def _bd_build(name, v_re, v_im, sign, tiling):
    def body(r_ref, i_ref, t_ref, o_ref):
        own = _own_block()
        o_ref[:, :SSM_S] = jnp.where(own, _dot(r_ref[...].astype(BF16), t_ref[...]), 0.0).astype(BF16)
        o_ref[:, SSM_S:] = jnp.where(own, sign * _dot(i_ref[...].astype(BF16), t_ref[...]), 0.0).astype(BF16)

    rows = SSM_G * SSM_P
    return pl.pallas_call(
        body, name=name, out_shape=jax.ShapeDtypeStruct((rows, 2 * SSM_S), BF16), grid=(1,),
        in_specs=[_full_spec((rows, SSM_N))] * 2 + [_full_spec((SSM_N, SSM_S))], out_specs=_full_spec((rows, 2 * SSM_S)),
        compiler_params=_params(("arbitrary",)),
    )(v_re, v_im, tiling)


def _bd_extract(name, m, sign, tiling):
    def body(m_ref, t_ref, r_ref, i_ref):
        own, t = _own_block(), t_ref[...]

        def pick(x):
            x = jnp.where(own, x, 0.0)
            hi = x.astype(BF16)
            rest = x - hi.astype(F32)
            mid = rest.astype(BF16)
            lo = (rest - mid.astype(F32)).astype(BF16)
            return _dot(hi, t, NT) + _dot(mid, t, NT) + _dot(lo, t, NT)

        r_ref[...] = pick(m_ref[:, :SSM_S])
        i_ref[...] = sign * pick(m_ref[:, SSM_S:])

    rows = SSM_G * SSM_P
    return pl.pallas_call(
        body, name=name, out_shape=[jax.ShapeDtypeStruct((rows, SSM_N), F32)] * 2, grid=(1,),
        in_specs=[_full_spec((rows, 2 * SSM_S)), _full_spec((SSM_N, SSM_S))], out_specs=[_full_spec((rows, SSM_N))] * 2,
        compiler_params=_params(("arbitrary",)),
    )(m, tiling)


def _fold(a, dil):
    if dil == 1:
        return a
    return a.reshape((T // dil, dil) + a.shape[1:]).swapaxes(0, 1).reshape(a.shape)


def _unfold(a, dil):
    if dil == 1:
        return a
    return a.reshape((dil, T // dil) + a.shape[1:]).swapaxes(0, 1).reshape(a.shape)


DILS = (1, 4, 16)


def _fold3(parts):
    parts = [parts] * 3 if not isinstance(parts, (list, tuple)) else parts
    return jnp.stack([_fold(a, d) for a, d in zip(parts, DILS)])


def _unfold3(a):
    return [_unfold(a[p], d) for p, d in enumerate(DILS)]


def _rope_tables():
    half = ROPE // 2
    inv_freq = ROPE_THETA ** (-jnp.arange(half, dtype=F32) / half)
    ang = jnp.arange(T).astype(F32)[:, None] * inv_freq[None, :]
    i, j = jnp.arange(ROPE)[:, None], jnp.arange(ROPE)[None, :]
    rot = jnp.where(i == j + half, -1.0, jnp.where(i + half == j, 1.0, 0.0)).astype(F32)
    return jnp.tile(jnp.cos(ang), (1, 2)), jnp.tile(jnp.sin(ang), (1, 2)), rot


def _rot_half(x, rot):
    return _dot(x.reshape(-1, ROPE), rot, prec=HI).reshape(x.shape)


def _mla_pack_fn(q, kv, k_rope, cos, sin, rot):
    rope = lambda x: x * cos + _rot_half(x, rot) * sin
    q_out = jnp.concatenate([q[:, :, :NOPE], rope(q[:, :, NOPE:])], axis=-1)
    k_pe = jnp.broadcast_to(rope(k_rope)[None], (H_MLA,) + k_rope.shape)
    return q_out, jnp.concatenate([kv[:, :, :NOPE], k_pe], axis=-1), kv[:, :, NOPE:]


def _mla_unpack_fn(dq, dk, dv, cos, sin, rot):
    unrope = lambda g: g * cos - _rot_half(g * sin, rot)
    dq_out = jnp.concatenate([dq[:, :, :NOPE], unrope(dq[:, :, NOPE:])], axis=-1)
    dk_rope = unrope(jnp.sum(dk[:, :, NOPE:], axis=0))
    return dq_out, jnp.concatenate([dk[:, :, :NOPE], dv], axis=-1), dk_rope


def _flat(w8):
    return w8.reshape(-1, w8.shape[-1])


def _blocks(m):
    return m.reshape(N_DEV, -1, m.shape[-1])


def _behind(a, tok):
    return a if tok is None else a + tok


def _mixer_fwd_in(x, w, sp, rope):
    s = {}
    s['x'] = x
    h = _rms_fwd("rms_mix", x, sp['g_mix'])
    proj = _mm("mm_in", h, _flat(w['w_in']), 'nt')
    offs = np.cumsum((0,) + IN_SPLITS)
    c_q, c_kv, k_rope, u, qd, kd, vd = [proj[:, offs[i]:offs[i + 1]] for i in range(7)]
    s.update(h=h, c_q=c_q, c_kv=c_kv, u=u)

    cqn = _rms_fwd("rms_q", c_q, sp['g_q'])
    ckvn = _rms_fwd("rms_kv", c_kv, sp['g_kv'])
    q8 = _mm("mm_uq", cqn, w['w_uq'], 'nt', bb='r', ob='c')
    kv8 = _mm("mm_ukv", ckvn, w['w_ukv'], 'nn', bb='c', ob='c')
    cos, sin, rot = rope
    tr = MLA_TQ
    qh, kh, vh = _rows_fwd(
        "mla_pack", _mla_pack_fn,
        [_row(q8, tr, 1), _row(kv8, tr, 1), _row(k_rope, tr, 0), _row(cos, tr, 0), _row(sin, tr, 0)], [rot],
        [((H_MLA, T, QK), BF16, 1, tr), ((H_MLA, T, QK), BF16, 1, tr), ((H_MLA, T, VDIM), BF16, 1, tr)])
    y_mla, lse_mla = _mla_fwd(qh, kh, vh)
    s.update(cqn=cqn, ckvn=ckvn, qh=qh, kh=kh, vh=vh, lse_mla=lse_mla, y_mla=y_mla, qd=qd, kd=kd, vd=vd)
    return s


def _mixer_fwd_out(s, w, sp, tok=None, after_ssm=None):
    x, u, y_mla, qd, kd, vd = s['x'], s['u'], s['y_mla'], s['qd'], s['kd'], s['vd']
    a3 = lambda n: sp[n].reshape(SSM_G, 1, SSM_N)
    b2 = lambda n: sp[n].transpose(0, 2, 1).reshape(SSM_G * SSM_P, SSM_N)
    disc_rows = [_row(a3('a_re'), 1, 0), _row(a3('a_im'), 1, 0), _row(sp['log_dt'].reshape(SSM_G, 1, 1), 1, 0),
                 _row(b2('b_re'), SSM_P, 0), _row(b2('b_im'), SSM_P, 0)]
    abr, abi, bbr, bbi = _rows_fwd(
        "s5_disc", _s5_disc_fn, disc_rows, [],
        [((SSM_G, 1, SSM_N), F32, 0, 1), ((SSM_G, 1, SSM_N), F32, 0, 1),
         ((SSM_G * SSM_P, SSM_N), F32, 0, SSM_P), ((SSM_G * SSM_P, SSM_N), F32, 0, SSM_P)])
    ar, ai = abr.reshape(1, SSM_S), abi.reshape(1, SSM_S)
    tiling = _lane_tiling()
    b_mat = _bd_build("s5_b_matrix", bbr, bbi, 1.0, tiling)
    c2 = lambda n: sp[n].reshape(SSM_G * SSM_P, SSM_N)
    c_mat = _bd_build("s5_c_matrix", c2('c_re'), c2('c_im'), -1.0, tiling)
    u16 = _behind(u, tok).astype(BF16)
    bu = _mm("mm_s5_b", u16, b_mat, 'nn')
    hst, hst16 = _scan_fwd(bu, ar, ai)
    ymm = _mm("mm_s5_c", hst16, c_mat, 'nt')
    d_row = sp['d_skip'].reshape(1, SSM_W)
    (yg,) = _rows_fwd("s5_act", _s5_act_fn, [_row(ymm), _row(u)], [d_row], [((T, SSM_W), BF16, -2, _row(u)[2])])
    z = _mm("mm_glu", yg, w['w_glu'], 'nn', bb='c')
    glu_rows = [_row(z[:, :SSM_W]), _row(z[:, SSM_W:])]
    glu_b = [sp['b_glu'][:SSM_W].reshape(1, -1), sp['b_glu'][SSM_W:].reshape(1, -1)]
    (y_ssm,) = _rows_fwd("s5_glu", _glu_fn, glu_rows, glu_b, [((T, SSM_W), F32, -2, glu_rows[0][2])])
    if after_ssm is not None:
        qd = _behind(qd, after_ssm(y_ssm))
    s.update(disc_rows=disc_rows, ar=ar, ai=ai, b_mat=b_mat, c_mat=c_mat, hst=hst, hst16=hst16, u16=u16, ymm=ymm,
             d_row=d_row, yg=yg,
             glu_rows=glu_rows, glu_b=glu_b)

    qf, kf, vf = [_fold3(a).astype(BF16) for a in (qd, kd, vd)]
    o_f, lse_f = _band_fwd(qf, kf, vf)
    mix_rows = [_row(a) for a in _unfold3(o_f) + _unfold3(lse_f)]
    (y_dil,) = _rows_fwd("dil_mix", _dil_mix_fn, mix_rows, [], [((T, DIL_W), F32, -2, mix_rows[0][2])])
    s.update(qf=qf, kf=kf, vf=vf, o_f=o_f, lse_f=lse_f, mix_rows=mix_rows)

    gm, gs, gd = sp['g_out_mla'].reshape(1, -1), sp['g_out_ssm'].reshape(1, -1), sp['g_out_dil'].reshape(1, -1)
    on_rows = [_row(y_mla), _row(y_ssm), _row(y_dil)]
    (ycat,) = _rows_fwd("out_norm", _outnorm_fn, on_rows, [gm, gs, gd], [((T, D), BF16, -2, on_rows[0][2])])
    x1_ = _mm("mm_o", ycat, _flat(w['w_o']), 'nn', res=x)
    s.update(on_rows=on_rows, on_g=[gm, gs, gd], ycat=ycat, x1=x1_)
    for k in ('qd', 'kd', 'vd'):
        del s[k]
    return x1_


def _ffn_fwd(x1_, w, sp, s, tok=None):
    h2 = _rms_fwd("rms_ffn", x1_, _behind(sp['g_ffn'], tok))
    ga = _mm("mm_gate", h2, _flat(w['w_gate']), 'nt')
    gb, zf = _mm("mm_up", h2, _flat(w['w_up']), 'nt',
                 epilogue=(lambda up, gate: (up, _swiglu_fn(gate, up)), [ga], [F32, BF16]))
    x2_ = _mm("mm_down", zf, _flat(w['w_down']), 'nn', res=x1_)
    s.update(h2=h2, ga=ga, gb=gb, zf=zf)
    return x2_


def _b16(a):
    return a.astype(BF16)


def _ffn_bwd(dx2, s, w, sp, tok=None):
    gw, gs_ = {}, {}
    b16 = _b16
    dx2b = b16(_behind(dx2, tok))
    def gate_pullback(dz, gate, up):
        return jax.vjp(_swiglu_fn, gate, up)[1](dz)

    dga, dgb = _mm("mm_down_dx", dx2b, _flat(w['w_down']), 'nt', epilogue=(gate_pullback, [s['ga'], s['gb']], [BF16, BF16]))
    gw['w_down'] = _blocks(_mm("mm_down_dw", s['zf'], dx2b, 'tn'))
    gw['w_gate'] = _blocks(_mm("mm_gate_dw", dga, s['h2'], 'tn'))
    gw['w_up'] = _blocks(_mm("mm_up_dw", dgb, s['h2'], 'tn'))
    dh2 = _mm("mm_up_dx", dgb, _flat(w['w_up']), 'nn', res=_mm("mm_gate_dx", dga, _flat(w['w_gate']), 'nn'))
    dx1, gs_['g_ffn'] = _rms_bwd("rms_ffn_bwd", s['x1'], sp['g_ffn'], dh2, dx2)
    return dx1, gw, gs_


def _mixer_bwd_out(dx1, s, w, sp, tok=None):
    gw, gs_ = {}, {}
    b16 = _b16
    dx1b = b16(_behind(dx1, tok))
    dycat = _mm("mm_o_dx", dx1b, _flat(w['w_o']), 'nt')
    gw['w_o'] = _blocks(_mm("mm_o_dw", s['ycat'], dx1b, 'tn'))
    dy_mla, dy_ssm, dy_dil, gs_['g_out_mla'], gs_['g_out_ssm'], gs_['g_out_dil'] = _rows_vjp(
        "out_norm_bwd", _outnorm_fn, s['on_rows'], s['on_g'], [], [_row(dycat)])

    dmix = _rows_vjp("dil_mix_bwd", _dil_mix_fn, s['mix_rows'], [], [], [_row(dy_dil)])
    dqf, dkf, dvf = _band_bwd(s['qf'], s['kf'], s['vf'], s['o_f'], s['lse_f'], _fold3(dmix[:3]), _fold3(dmix[3:]))
    back = lambda a: sum(_unfold3(a))
    dqd, dkd, dvd = back(dqf), back(dkf), back(dvf)

    dz1, dz2, db1, db2 = _rows_vjp("s5_glu_bwd", _glu_fn, s['glu_rows'], s['glu_b'], [], [_row(dy_ssm)])
    gs_['b_glu'] = jnp.concatenate([db1, db2], axis=1)
    dzb = b16(jnp.concatenate([dz1, dz2], axis=1))
    dyg = _mm("mm_glu_dx", dzb, w['w_glu'], 'nt', bb='c')
    gw['w_glu'] = _mm("mm_glu_dw", s['yg'], dzb, 'tn', ob='c')
    dymm, du_act, dd = _rows_vjp("s5_act_bwd", _s5_act_fn, [_row(s['ymm']), _row(s['u'])], [s['d_row']], [], [_row(dyg)],
                                 grad_dtypes=[BF16, F32])
    gs_['d_skip'] = dd
    dhst = _mm("mm_s5_c_dx", dymm, s['c_mat'], 'nn')
    dc_mat = _mm("mm_s5_c_dw", dymm, s['hst16'], 'tn')
    g, dar, dai = _scan_bwd(dhst, s['hst'], s['ar'], s['ai'])
    du = _mm("mm_s5_b_dx", g, s['b_mat'], 'nt', res=du_act)
    db_mat = _mm("mm_s5_b_dw", s['u16'], g, 'tn')
    tiling = _lane_tiling()
    gs_['c_re'], gs_['c_im'] = _bd_extract("s5_c_blocks", dc_mat, -1.0, tiling)
    dbbr, dbbi = _bd_extract("s5_b_blocks", db_mat, 1.0, tiling)
    disc_cts = [_row(dar.reshape(SSM_G, 1, SSM_N), 1, 0), _row(dai.reshape(SSM_G, 1, SSM_N), 1, 0),
                _row(dbbr, SSM_P, 0), _row(dbbi, SSM_P, 0)]
    da_re, da_im, dldt, db_r, db_i = _rows_vjp("s5_disc_bwd", _s5_disc_fn, s['disc_rows'], [], [], disc_cts)
    gs_['a_re'], gs_['a_im'], gs_['log_dt'] = da_re, da_im, dldt
    unb = lambda a: a.reshape(SSM_G, SSM_P, SSM_N).transpose(0, 2, 1)
    gs_['b_re'], gs_['b_im'] = unb(db_r), unb(db_i)
    return (dy_mla, du, dqd, dkd, dvd), gw, gs_


def _mixer_bwd_in(cts, dx1, s, w, sp, rope, tok=None):
    gw, gs_ = {}, {}
    b16 = _b16
    dy_mla, du, dqd, dkd, dvd = cts
    dqh, dkh, dvh = _mla_bwd(s['qh'], s['kh'], s['vh'], s['y_mla'], dy_mla, _behind(s['lse_mla'], tok))
    cos, sin, rot = rope
    tr = MLA_TQ
    dq8, dkv8, dk_rope = _rows_fwd(
        "mla_unpack", _mla_unpack_fn,
        [_row(dqh, tr, 1), _row(dkh, tr, 1), _row(dvh, tr, 1), _row(cos, tr, 0), _row(sin, tr, 0)], [rot],
        [((H_MLA, T, QK), BF16, 1, tr), ((H_MLA, T, NOPE + VDIM), BF16, 1, tr), ((T, ROPE), F32, 0, tr)])
    dcqn = _mm("mm_uq_dx", dq8, w['w_uq'], 'nn', ab='c', bb='r')
    gw['w_uq'] = _mm("mm_uq_dw", dq8, s['cqn'], 'tn', ab='c', ob='r')
    dckvn = _mm("mm_ukv_dx", dkv8, w['w_ukv'], 'nt', ab='c', bb='c')
    gw['w_ukv'] = _mm("mm_ukv_dw", s['ckvn'], dkv8, 'tn', bb='c', ob='c')
    dc_q, gs_['g_q'] = _rms_bwd("rms_q_bwd", s['c_q'], sp['g_q'], dcqn)
    dc_kv, gs_['g_kv'] = _rms_bwd("rms_kv_bwd", s['c_kv'], sp['g_kv'], dckvn)

    dproj = b16(jnp.concatenate([dc_q, dc_kv, dk_rope, du, dqd, dkd, dvd], axis=1))
    dh = _mm("mm_in_dx", dproj, _flat(w['w_in']), 'nn')
    gw['w_in'] = _blocks(_mm("mm_in_dw", dproj, s['h'], 'tn'))
    dx, gs_['g_mix'] = _rms_bwd("rms_mix_bwd", s['x'], sp['g_mix'], dh, dx1)
    return dx, gw, gs_


def kernel(x, g_mix, w_in, g_q, w_uq, g_kv, w_ukv, a_re, a_im, b_re, b_im, c_re, c_im, d_skip, log_dt, w_glu, b_glu, g_out_mla, g_out_ssm, g_out_dil, w_o, g_ffn, w_gate, w_up, w_down, g_final, loss_target, m_g_mix, m_w_in, m_g_q, m_w_uq, m_g_kv, m_w_ukv, m_a_re, m_a_im, m_b_re, m_b_im, m_c_re, m_c_im, m_d_skip, m_log_dt, m_w_glu, m_b_glu, m_g_out_mla, m_g_out_ssm, m_g_out_dil, m_w_o, m_g_ffn, m_w_gate, m_w_up, m_w_down, m_g_final, v_g_mix, v_w_in, v_g_q, v_w_uq, v_g_kv, v_w_ukv, v_a_re, v_a_im, v_b_re, v_b_im, v_c_re, v_c_im, v_d_skip, v_log_dt, v_w_glu, v_b_glu, v_g_out_mla, v_g_out_ssm, v_g_out_dil, v_w_o, v_g_ffn, v_w_gate, v_w_up, v_w_down, v_g_final):
    W = dict(zip(PARAMS, (g_mix, w_in, g_q, w_uq, g_kv, w_ukv, a_re, a_im, b_re, b_im, c_re, c_im, d_skip, log_dt,
                          w_glu, b_glu, g_out_mla, g_out_ssm, g_out_dil, w_o, g_ffn, w_gate, w_up, w_down, g_final)))
    M = dict(zip(PARAMS, (m_g_mix, m_w_in, m_g_q, m_w_uq, m_g_kv, m_w_ukv, m_a_re, m_a_im, m_b_re, m_b_im, m_c_re,
                          m_c_im, m_d_skip, m_log_dt, m_w_glu, m_b_glu, m_g_out_mla, m_g_out_ssm, m_g_out_dil, m_w_o,
                          m_g_ffn, m_w_gate, m_w_up, m_w_down, m_g_final)))
    V = dict(zip(PARAMS, (v_g_mix, v_w_in, v_g_q, v_w_uq, v_g_kv, v_w_ukv, v_a_re, v_a_im, v_b_re, v_b_im, v_c_re,
                          v_c_im, v_d_skip, v_log_dt, v_w_glu, v_b_glu, v_g_out_mla, v_g_out_ssm, v_g_out_dil, v_w_o,
                          v_g_ffn, v_w_gate, v_w_up, v_w_down, v_g_final)))
    cx, cy, cc = _place()
    core = cc.astype(jnp.int32).reshape(1)
    chip = (2 * cx + cy).astype(jnp.int32).reshape(1)
    rope = _rope_tables()
    small = [{n: W[n][l] for n in SMALL} for l in range(DEPTH)]
    for sp in small:
        for n in ('g_mix', 'g_q', 'g_kv', 'g_ffn'):
            sp[n] = sp[n].reshape(1, -1)

    def tok_of(tokens):
        return sum(t[0, 0] for t in tokens) if tokens else None

    def shard_view(a, n):
        return a.swapaxes(1, 2) if BIG[n] == 't' else a

    def gather_start(l, group, names, after):
        return _gather_start(f"gather_{group}_start_{l}", [shard_view(W[n], n)[l].astype(BF16) for n in names], after)

    xa = x[0]
    h1_mix = gather_start(0, "mix", MIXER_W, jnp.zeros((8, LANES), F32))
    h1_ffn = gather_start(0, "ffn", FFN_W, h1_mix[4])
    h2_mix = _gather_pass_on("gather_mix_0", h1_mix, xa)
    saved, full = [], []
    tokens = [h2_mix[4], h1_ffn[4]]
    for l in range(DEPTH):
        last = l + 1 == DEPTH
        wm = dict(zip(MIXER_W, _gather_finish(f"gather_mix_{l}", h2_mix, xa)))
        sp = dict(small[l])
        sp['g_mix'] = _behind(sp['g_mix'], tok_of(tokens))
        s = _mixer_fwd_in(xa, wm, sp, rope)
        tokens = []
        first_ffn = {}
        if l == 0:
            h1_first = h1_ffn
            def mid(dep):
                first_ffn['h'] = _gather_pass_on("gather_ffn_0", h1_first, dep)
                return first_ffn['h'][4][0, 0]
        else:
            mid = None
        if not last:
            h1_mix = gather_start(l + 1, "mix", MIXER_W, s['y_mla'])
            h1_ffn = gather_start(l + 1, "ffn", FFN_W, h1_mix[4])
            tokens += [h1_mix[4], h1_ffn[4]]
        x1 = _mixer_fwd_out(s, wm, small[l], tok_of(tokens), mid)
        if l == 0:
            h2_ffn = first_ffn['h']
        tokens = []
        wf = dict(zip(FFN_W, _gather_finish(f"gather_ffn_{l}", h2_ffn, x1)))
        if not last:
            h2_mix = _gather_pass_on(f"gather_mix_{l + 1}", h1_mix, x1)
            tokens.append(h2_mix[4])
        xa = _ffn_fwd(x1, wf, small[l], s, tok_of(tokens))
        tokens = []
        if not last:
            h2_ffn = _gather_pass_on(f"gather_ffn_{l + 1}", h1_ffn, xa)
            tokens.append(h2_ffn[4])
        saved.append(s)
        full.append({**wm, **wf})
    gf = g_final.reshape(1, D)
    ones = jnp.ones((T, 1), F32)
    dxa, dgf, loss_rows = _rows_vjp("loss", _loss_fn, [_row(xa)], [gf], [_row(loss_target[0])], [_row(ones)],
                                    primal=True)
    loss_here = _sum_rows("loss_sum", loss_rows)[0, 0]

    bufs = {n: None for n in BIG}
    pending = []

    def advance(dep):
        tokens = []
        for g in pending:
            names, tag = g['names'], g['tag']
            step = g['steps'].pop(0)
            if step == 'sibling':
                p4 = [a.reshape((4, 2) + a.shape[1:]) for a in g['gw']]
                gots = [lax.empty((4,) + a.shape[1:], F32) for a in g['gw']]
                g['h'] = _split_start("rs_sibling_start_" + tag, p4, gots, 4 * len(p4), _sibling_plan, dep)
                tokens.append(g['h'][4])
            elif step == 'chips':
                p4, gots = _split_wait("rs_sibling_wait_" + tag, g['h'], 4 * len(names), _sibling_plan, dep)
                s4 = [_pair_sum("rs_pair_sum_" + n, p, q, core) for n, p, q in zip(names, p4, gots)]
                gots = [lax.empty((3,) + a.shape[1:], a.dtype) for a in s4]
                g['h'] = _split_start("rs_chips_start_" + tag, s4, gots, 3 * len(s4), _chips_plan, dep)
                tokens.append(g['h'][4])
            elif step == 'update':
                s4, gots = _split_wait("rs_chips_wait_" + tag, g['h'], 3 * len(names), _chips_plan, dep)
                for n, s4n, got in zip(names, s4, gots):
                    bufs[n] = _adamw_shard("adamw_" + n, g['layer'], shard_view(W[n], n), shard_view(M[n], n),
                                           shard_view(V[n], n), s4n, got, chip, bufs[n])
        pending[:] = [g for g in pending if g['steps']]
        return tokens

    def group(names, l, gw, kind):
        steps = ['sibling'] + (['rest'] if kind == "ffn" and l > 0 else []) + ['chips', 'rest', 'update']
        return dict(names=names, layer=l, gw=[gw[n] for n in names], steps=steps, tag=f"{kind}_{l}")

    def pack_small(parts):
        n = sum(int(p.shape[0]) for p in parts)
        rows = -(-n // (PACK_C * SMALL_ROW_ALIGN)) * SMALL_ROW_ALIGN
        return jnp.concatenate(parts + [jnp.zeros((rows * PACK_C - n,), F32)]).reshape(rows, PACK_C)

    def small_of(layers):
        return [g_small[k][n].reshape(-1) for k in layers for n in SMALL]

    g_small = [None] * DEPTH
    tokens = []
    upper = list(range(1, DEPTH))
    tot_upper = None
    for l in reversed(range(DEPTH)):
        early = l == 0 and upper
        if early:
            flat_upper = pack_small(small_of(upper))
            h_upper = _gather_start("gather_upper_start", [flat_upper], dxa)
            tokens.append(h_upper[4])
        dx1, gw_f, gs_f = _ffn_bwd(dxa, saved[l], full[l], small[l], tok_of(tokens))
        pending.append(group(FFN_W, l, gw_f, "ffn"))
        tokens = advance(dx1)
        if early:
            h_upper = _gather_pass_on("gather_upper", h_upper, dx1)
            tokens.append(h_upper[4])
        cts, gw_o, gs_o = _mixer_bwd_out(dx1, saved[l], full[l], small[l], tok_of(tokens))
        pending.append(group(OUT_W, l, gw_o, "out"))
        tokens = advance(cts[0])
        if early:
            (g_upper,) = _gather_finish("gather_upper", h_upper, cts[0])
            tot_upper = _sum_devices("small_sum_upper", g_upper).reshape(-1)
        dxa, gw_i, gs_i = _mixer_bwd_in(cts, dx1, saved[l], full[l], small[l], rope, tok_of(tokens))
        pending.append(group(IN_W, l, gw_i, "in"))
        tokens = advance(dxa)
        g_small[l] = {**gs_f, **gs_o, **gs_i}

    first = [0] if upper else list(range(DEPTH))
    flat = pack_small(small_of(first) + [dgf.reshape(-1), loss_here.reshape(1)])
    h_small = _gather_start("gather_small_start", [flat], dxa)
    advance(h_small[4])
    h_small = _gather_pass_on("gather_small", h_small, flat)
    tokens = advance(h_small[4])
    (gathered,) = _gather_finish("gather_small", h_small, flat)
    tot = _behind(_sum_devices("small_sum", gathered).reshape(-1), tok_of(tokens))

    def unpack(flat_sum, layers):
        out, off = {}, 0
        for k in layers:
            for n, shp in SMALL.items():
                size = int(np.prod(shp))
                out[k, n] = flat_sum[off:off + size].reshape(shp)
                off += size
        return out, off

    per_layer, off = unpack(tot, first)
    if upper:
        per_layer.update(unpack(tot_upper, upper)[0])
    grads = {n: jnp.stack([per_layer[k, n] for k in range(DEPTH)]) for n in SMALL}
    grads['g_final'] = tot[off:off + D]
    loss = tot[off + D]

    delta, new_m, new_v = {}, {}, {}
    for n in PARAMS:
        if n not in BIG:
            delta[n], new_m[n], new_v[n] = _adamw("adamw_" + n, W[n], grads[n], M[n], V[n])
    while pending:
        advance(delta['g_final'])
    for n in BIG:
        grads[n], delta[n], new_m[n], new_v[n] = [shard_view(b, n) for b in bufs[n]]
    return (loss, dxa[None], *[grads[n] for n in PARAMS], *[delta[n] for n in PARAMS],
            *[new_m[n] for n in PARAMS], *[new_v[n] for n in PARAMS])
```

```python
import jax
import jax.numpy as jnp
import numpy as np
from jax import lax
from jax.experimental import pallas as pl
from jax.experimental.pallas import tpu as pltpu

F32 = jnp.float32
BF16 = jnp.bfloat16

T = 2048
D = 2048
DEPTH = 4
N_DEV = 8
H_MLA, NOPE, ROPE, VDIM = 8, 128, 64, 128
QK = NOPE + ROPE
Q_LORA, KV_LORA = 512, 256
SSM_W, SSM_G, SSM_P, SSM_N = 512, 32, 16, 64
SSM_S = SSM_G * SSM_N
DIL_W, DIL_H, DIL_D = 512, 8, 64
BLK = 128
IN_SPLITS = (Q_LORA, KV_LORA, ROPE, SSM_W, DIL_W, DIL_W, DIL_W)
IN_W = sum(IN_SPLITS)
D_FF = 5632
EPS = 1e-6
ROPE_THETA = 10000.0
MLA_SCALE = QK ** -0.5
DIL_SCALE = DIL_D ** -0.5

ADAM_LR, ADAM_B1, ADAM_B2, ADAM_EPS, ADAM_WD, ADAM_STEP = 0.001, 0.9, 0.999, 1e-08, 0.01, 10

VMEM_LIMIT_V7X = 52 * 1024 * 1024
LANES = 128
PACK_C = 1024
ROW_BLOCK_BYTES = 2 * 1024 * 1024
MM_TM, MM_TN, MM_TK = 1408, 1024, 5632
MM_DEEP = IN_W
MM_TB = 512

NT = (((1,), (1,)), ((), ()))
TN = (((0,), (0,)), ((), ()))
HI = lax.Precision.HIGHEST
MESH = pl.DeviceIdType.MESH

PARAMS = ['g_mix', 'w_in', 'g_q', 'w_uq', 'g_kv', 'w_ukv', 'a_re', 'a_im', 'b_re', 'b_im', 'c_re', 'c_im',
          'd_skip', 'log_dt', 'w_glu', 'b_glu', 'g_out_mla', 'g_out_ssm', 'g_out_dil', 'w_o', 'g_ffn',
          'w_gate', 'w_up', 'w_down', 'g_final']
BIG = {'w_in': 't', 'w_uq': 't', 'w_ukv': 'c', 'w_glu': 'c', 'w_o': 'r', 'w_gate': 't', 'w_up': 't', 'w_down': 'r'}
MIXER_W = ['w_in', 'w_uq', 'w_ukv', 'w_glu', 'w_o']
FFN_W = ['w_gate', 'w_up', 'w_down']
OUT_W, IN_W = ['w_o', 'w_glu'], ['w_in', 'w_uq', 'w_ukv']
SMALL = {'g_mix': (D,), 'g_q': (Q_LORA,), 'g_kv': (KV_LORA,), 'a_re': (SSM_G, SSM_N), 'a_im': (SSM_G, SSM_N),
         'b_re': (SSM_G, SSM_N, SSM_P), 'b_im': (SSM_G, SSM_N, SSM_P), 'c_re': (SSM_G, SSM_P, SSM_N),
         'c_im': (SSM_G, SSM_P, SSM_N), 'd_skip': (SSM_G, SSM_P), 'log_dt': (SSM_G,), 'b_glu': (2 * SSM_W,),
         'g_out_mla': (H_MLA * VDIM,), 'g_out_ssm': (SSM_W,), 'g_out_dil': (DIL_W,), 'g_ffn': (D,)}
SMALL_ROW_ALIGN = 64


def _tile(dim, target, align=LANES):
    best = None
    for t in range(align, min(dim, target) + 1, align):
        if dim % t == 0:
            best = t
    return best if best is not None else dim


def _params(sem=None):
    return pltpu.CompilerParams(dimension_semantics=sem, vmem_limit_bytes=VMEM_LIMIT_V7X)


def _dot(a, b, dims=None, prec=None):
    if dims is None:
        return jnp.dot(a, b, preferred_element_type=F32, precision=prec)
    return lax.dot_general(a, b, dims, preferred_element_type=F32, precision=prec)


def _mm_spec(shape, blk, t_r, t_c, rc):
    if blk is None:
        return pl.BlockSpec((t_r, t_c), rc)
    _, R, C = shape
    if blk == 'r':
        per = R // t_r
        return pl.BlockSpec((1, t_r, t_c), lambda i, j, k: (rc(i, j, k)[0] // per, rc(i, j, k)[0] % per, rc(i, j, k)[1]))
    per = C // t_c
    return pl.BlockSpec((1, t_r, t_c), lambda i, j, k: (rc(i, j, k)[1] // per, rc(i, j, k)[0], rc(i, j, k)[1] % per))


def _logical(shape, blk):
    if blk is None:
        return tuple(shape)
    G, R, C = shape
    return (G * R, C) if blk == 'r' else (R, G * C)


def _mm(name, a, b, mode, ab=None, bb=None, ob=None, res=None, prec=None, epilogue=None):
    la, lb = _logical(a.shape, ab), _logical(b.shape, bb)
    am, ak = (0, 1) if mode != 'tn' else (1, 0)
    bk, bn = (0, 1) if mode != 'nt' else (1, 0)
    M, K, N = la[am], la[ak], lb[bn]
    assert lb[bk] == K, (name, a.shape, b.shape, mode)
    if ob is None:
        out_shape = (M, N)
    elif ob == 'r':
        G = N_DEV
        out_shape = (G, M // G, N)
    else:
        G = N_DEV
        out_shape = (G, M, N // G)
    em = min(a.shape[-2:][am], out_shape[-2])
    en = min(b.shape[-2:][bn], out_shape[-1])
    ek = min(a.shape[-2:][ak], b.shape[-2:][bk])
    dims = {'nn': None, 'nt': NT, 'tn': TN}[mode]
    a_kb = mode != 'tn' and ab == 'c'
    b_kb = (mode == 'nn' and bb == 'r') or (mode == 'nt' and bb == 'c')
    blocks = K // ek if (a_kb or b_kb) else 1
    assert blocks == 1 or ((a_kb or ab is None) and (b_kb or bb is None)), (name, ab, bb, mode)
    tk = ek if blocks > 1 else _tile(ek, MM_TK)
    nk = 1 if blocks > 1 else K // tk
    small = blocks > 1 or nk > 1 or K > MM_DEEP
    tn = _tile(en, MM_TB if small else MM_TN)
    tm = _tile(em, MM_TB if small else (2 * MM_TM if K <= MM_TB else MM_TM))
    if tn > MM_TN:
        tm = _tile(em, MM_TB)
    if tm > 2 * MM_TM:
        tn = _tile(en, MM_TB)

    def val(ref):
        return ref[...] if len(ref.shape) == 2 else ref[0]

    def put(o_ref, r):
        if len(o_ref.shape) == 2:
            o_ref[...] = r
        else:
            o_ref[0] = r

    def k_block(ref, d, blocked, lanes):
        if blocked:
            return ref[d]
        return ref[:, d * ek:(d + 1) * ek] if lanes else ref[d * ek:(d + 1) * ek, :]

    if epilogue is not None:
        epi_fn, extras, out_dtypes = epilogue
        assert nk == 1 and blocks == 1 and ob is None and res is None, name

    def body(*refs):
        if epilogue is not None:
            a_ref, b_ref = refs[:2]
            extra_refs, out_refs = refs[2:2 + len(extras)], refs[2 + len(extras):]
            outs = epi_fn(_dot(val(a_ref), val(b_ref), dims, prec), *[e[...] for e in extra_refs])
            for o_ref, o in zip(out_refs, outs):
                o_ref[...] = o.astype(o_ref.dtype)
            return
        if res is None:
            a_ref, b_ref, o_ref = refs[:3]
            r_ref = None
        else:
            a_ref, b_ref, r_ref, o_ref = refs[:4]
        if blocks > 1:
            part = None
            for d in range(blocks):
                p = _dot(k_block(a_ref, d, a_kb, True), k_block(b_ref, d, b_kb, mode == 'nt'), dims, prec)
                part = p if part is None else part + p
        else:
            part = _dot(val(a_ref), val(b_ref), dims, prec)
        if nk == 1:
            put(o_ref, part if r_ref is None else part + val(r_ref))
            return
        acc_ref = refs[-1]
        k = pl.program_id(2)

        @pl.when(k == 0)
        def _():
            acc_ref[...] = part

        @pl.when((k > 0) & (k < nk - 1))
        def _():
            acc_ref[...] += part

        @pl.when(k == nk - 1)
        def _():
            r = acc_ref[...] + part
            put(o_ref, r if r_ref is None else r + val(r_ref))

    if blocks > 1:
        G = blocks
        a_spec = (pl.BlockSpec((G, tm, ek), lambda i, j, k: (0, i, 0)) if a_kb
                  else pl.BlockSpec((tm, K), lambda i, j, k: (i, 0)))
        if b_kb:
            b_spec = (pl.BlockSpec((G, ek, tn), lambda i, j, k: (0, 0, j)) if mode == 'nn'
                      else pl.BlockSpec((G, tn, ek), lambda i, j, k: (0, j, 0)))
        else:
            b_spec = (pl.BlockSpec((K, tn), lambda i, j, k: (0, j)) if mode == 'nn'
                      else pl.BlockSpec((tn, K), lambda i, j, k: (j, 0)))
    else:
        if mode == 'tn':
            a_spec = _mm_spec(a.shape, ab, tk, tm, lambda i, j, k: (k, i))
        else:
            a_spec = _mm_spec(a.shape, ab, tm, tk, lambda i, j, k: (i, k))
        if mode == 'nt':
            b_spec = _mm_spec(b.shape, bb, tn, tk, lambda i, j, k: (j, k))
        else:
            b_spec = _mm_spec(b.shape, bb, tk, tn, lambda i, j, k: (k, j))
    o_spec = _mm_spec(out_shape, ob, tm, tn, lambda i, j, k: (i, j))
    if epilogue is not None:
        return pl.pallas_call(
            body, name=name, out_shape=[jax.ShapeDtypeStruct(out_shape, dt) for dt in out_dtypes],
            grid=(M // tm, N // tn, nk), in_specs=[a_spec, b_spec] + [o_spec] * len(extras),
            out_specs=[o_spec] * len(out_dtypes),
            compiler_params=_params(("parallel", "parallel", "arbitrary")),
        )(a, b, *extras)
    in_specs = [a_spec, b_spec] + ([o_spec] if res is not None else [])
    args = (a, b) + ((res,) if res is not None else ())
    return pl.pallas_call(
        body, name=name, out_shape=jax.ShapeDtypeStruct(out_shape, F32),
        grid=(M // tm, N // tn, nk), in_specs=in_specs, out_specs=o_spec,
        scratch_shapes=[pltpu.VMEM((tm, tn), F32)] if nk > 1 else [],
        compiler_params=_params(("parallel", "parallel", "arbitrary")),
    )(*args)


def _row(a, tr=None, axis=-2):
    axis = axis % a.ndim
    n = a.shape[axis]
    if tr is None:
        row_bytes = a.size // n * 4
        tr = _tile(n, max(8, min(256, ROW_BLOCK_BYTES // row_bytes)), 8)
    return (a, axis, tr)


def _row_spec(shape, axis, tr):
    nd = len(shape)
    blk = tuple(tr if d == axis else s for d, s in enumerate(shape))
    return pl.BlockSpec(blk, lambda i: tuple(i if d == axis else 0 for d in range(nd)))


def _full_spec(shape):
    nd = len(shape)
    return pl.BlockSpec(tuple(shape), lambda i: (0,) * nd)


def _steps(entries):
    ns = {a.shape[ax] // tr for a, ax, tr in entries}
    assert len(ns) == 1, [(a.shape, ax, tr) for a, ax, tr in entries]
    return ns.pop()


def _as_tuple(r):
    return tuple(r) if isinstance(r, (tuple, list)) else (r,)


def _rows_fwd(name, fn, rows, bcast, outs):
    steps = _steps(rows)
    nr, nb = len(rows), len(bcast)

    def body(*refs):
        vals = [r[...] for r in refs[:nr + nb]]
        res = _as_tuple(fn(*vals))
        for o_ref, r in zip(refs[nr + nb:], res):
            o_ref[...] = r.astype(o_ref.dtype)

    in_specs = [_row_spec(a.shape, ax, tr) for a, ax, tr in rows] + [_full_spec(b.shape) for b in bcast]
    out_specs = [_row_spec(s, ax % len(s), tr) for s, _, ax, tr in outs]
    res = pl.pallas_call(
        body, name=name, out_shape=[jax.ShapeDtypeStruct(s, dt) for s, dt, _, _ in outs],
        grid=(steps,), in_specs=in_specs, out_specs=out_specs,
        compiler_params=_params(("parallel",)),
    )(*[a for a, _, _ in rows], *bcast)
    return res


def _rows_vjp(name, fn, drows, dbc, arows, cts, primal=False, grad_dtypes=None):
    entries = list(drows) + list(arows) + list(cts)
    steps = _steps(entries)
    ndr, ndb, nar, nct = len(drows), len(dbc), len(arows), len(cts)
    gdt = list(grad_dtypes) if grad_dtypes is not None else [F32] * ndr

    def body(*refs):
        p = 0
        dr = [r[...] for r in refs[p:p + ndr]]; p += ndr
        db = [r[...] for r in refs[p:p + ndb]]; p += ndb
        ar = [r[...] for r in refs[p:p + nar]]; p += nar
        ct = [r[...] for r in refs[p:p + nct]]; p += nct
        g_rows = refs[p:p + ndr]; p += ndr
        g_bc = refs[p:p + ndb]; p += ndb
        prim_refs = refs[p:]

        def f(*d):
            return _as_tuple(fn(*d, *ar))

        outs, pullback = jax.vjp(f, *dr, *db)
        grads = pullback(tuple(c.astype(o.dtype) for c, o in zip(ct, outs)))
        for k in range(ndr):
            g_rows[k][...] = grads[k].astype(g_rows[k].dtype)
        if ndb:
            @pl.when(pl.program_id(0) == 0)
            def _():
                for r in g_bc:
                    r[...] = jnp.zeros_like(r)
            for k in range(ndb):
                g_bc[k][...] += grads[ndr + k]
        for r, o in zip(prim_refs, outs):
            r[...] = o.astype(r.dtype)

    in_specs = ([_row_spec(a.shape, ax, tr) for a, ax, tr in drows] + [_full_spec(b.shape) for b in dbc]
                + [_row_spec(a.shape, ax, tr) for a, ax, tr in arows]
                + [_row_spec(a.shape, ax, tr) for a, ax, tr in cts])
    out_shape = ([jax.ShapeDtypeStruct(a.shape, dt) for (a, _, _), dt in zip(drows, gdt)]
                 + [jax.ShapeDtypeStruct(b.shape, F32) for b in dbc])
    out_specs = ([_row_spec(a.shape, ax, tr) for a, ax, tr in drows] + [_full_spec(b.shape) for b in dbc])
    if primal:
        out_shape += [jax.ShapeDtypeStruct(a.shape, F32) for a, _, _ in cts]
        out_specs += [_row_spec(a.shape, ax, tr) for a, ax, tr in cts]
    return pl.pallas_call(
        body, name=name, out_shape=out_shape, grid=(steps,), in_specs=in_specs, out_specs=out_specs,
        compiler_params=_params(("arbitrary",)),
    )(*[a for a, _, _ in drows], *dbc, *[a for a, _, _ in arows], *[a for a, _, _ in cts])


def _rms_fn(x, g):
    return x * lax.rsqrt(jnp.mean(x * x, axis=-1, keepdims=True) + EPS) * g


def _rms_res_fn(x, g):
    return _rms_fn(x, g), x


def _s5_act_fn(ymm, u, d):
    return jax.nn.gelu(ymm + d * u)


def _glu_fn(z1, z2, b1, b2):
    return (z1 + b1) * jax.nn.sigmoid(z2 + b2)


def _outnorm_fn(ym, ys, yd, gm, gs, gd):
    return jnp.concatenate([_rms_fn(ym, gm), _rms_fn(ys, gs), _rms_fn(yd, gd)], axis=-1)


def _swiglu_fn(a, b):
    return jax.nn.silu(a) * b


def _loss_fn(x, g, tgt):
    err = _rms_fn(x, g) - tgt
    return 0.5 * jnp.mean(err * err, axis=-1, keepdims=True)


def _dil_mix_fn(o0, o1, o2, l0, l1, l2):
    m = jnp.maximum(jnp.maximum(l0, l1), l2)
    e0, e1, e2 = jnp.exp(l0 - m), jnp.exp(l1 - m), jnp.exp(l2 - m)
    s = e0 + e1 + e2
    return (e0 / s) * o0 + (e1 / s) * o1 + (e2 / s) * o2


def _s5_disc_fn(a_re, a_im, ldt, b_r, b_i):
    lr = jnp.minimum(a_re.reshape(1, SSM_N), -1e-4)
    li = a_im.reshape(1, SSM_N)
    dt = jnp.exp(ldt.reshape(1, 1))
    e = jnp.exp(lr * dt)
    ar = e * jnp.cos(li * dt)
    ai = e * jnp.sin(li * dt)
    nr, ni = ar - 1.0, ai
    den = lr * lr + li * li
    cr = (nr * lr + ni * li) / den
    ci = (ni * lr - nr * li) / den
    return ar.reshape(1, 1, SSM_N), ai.reshape(1, 1, SSM_N), cr * b_r - ci * b_i, cr * b_i + ci * b_r


def _adamw_fn(w, g, m, v):
    m = ADAM_B1 * m + (1.0 - ADAM_B1) * g
    v = ADAM_B2 * v + (1.0 - ADAM_B2) * jnp.square(g)
    m_hat = m / (1.0 - ADAM_B1 ** ADAM_STEP)
    v_hat = v / (1.0 - ADAM_B2 ** ADAM_STEP)
    delta = -ADAM_LR * (m_hat / (jnp.sqrt(v_hat) + ADAM_EPS) + ADAM_WD * w)
    return delta, m, v


def _rms_fwd(name, x, g):
    (h,) = _rows_fwd(name, _rms_fn, [_row(x)], [g], [(x.shape, BF16, -2, _row(x)[2])])
    return h


def _rms_bwd(name, x, g, dh, dres=None):
    if dres is None:
        dx, dg = _rows_vjp(name, _rms_fn, [_row(x)], [g], [], [_row(dh)])
    else:
        dx, dg = _rows_vjp(name, _rms_res_fn, [_row(x)], [g], [], [_row(dh), _row(dres)])
    return dx, dg


MLA_TQ = 256
MLA_EXT = 256


def _mla_fwd(q, k, v):
    tq = MLA_TQ

    def body(q_ref, k_ref, v_ref, o_ref, lse_ref):
        i = pl.program_id(1)
        q = q_ref[0]

        def rows_below(ext):
            s = _dot(q, k_ref[0, :ext, :], NT) * MLA_SCALE
            row = i * tq + lax.broadcasted_iota(jnp.int32, (tq, ext), 0)
            col = lax.broadcasted_iota(jnp.int32, (tq, ext), 1)
            s = jnp.where(row >= col, s, -jnp.inf)
            m = jnp.max(s, axis=-1, keepdims=True)
            p = jnp.exp(s - m)
            l = jnp.sum(p, axis=-1, keepdims=True)
            o_ref[...] = _dot((p / l).astype(BF16), v_ref[0, :ext, :])
            lse_ref[0] = m + jnp.log(l)

        for g in range(T // MLA_EXT):
            pl.when(i // (MLA_EXT // tq) == g)(lambda g=g: rows_below((g + 1) * MLA_EXT))

    return pl.pallas_call(
        body, name="mla_fwd",
        out_shape=[jax.ShapeDtypeStruct((T, H_MLA * VDIM), F32), jax.ShapeDtypeStruct((H_MLA, T, 1), F32)],
        grid=(H_MLA, T // tq),
        in_specs=[pl.BlockSpec((1, tq, QK), lambda h, i: (h, i, 0)),
                  pl.BlockSpec((1, T, QK), lambda h, i: (h, 0, 0)),
                  pl.BlockSpec((1, T, VDIM), lambda h, i: (h, 0, 0))],
        out_specs=[pl.BlockSpec((tq, VDIM), lambda h, i: (i, h)),
                   pl.BlockSpec((1, tq, 1), lambda h, i: (h, i, 0))],
        compiler_params=_params(("parallel", "parallel")),
    )(q, k, v)


def _mla_bwd(q, k, v, o, do, lse):
    tq = MLA_TQ

    def body(q_ref, k_ref, v_ref, o_ref, do_ref, lse_ref, dq_ref, dk_ref, dv_ref):
        i = pl.program_id(1)

        @pl.when(i == 0)
        def _():
            dk_ref[...] = jnp.zeros_like(dk_ref)
            dv_ref[...] = jnp.zeros_like(dv_ref)

        q, lse = q_ref[0], lse_ref[0]
        delta = jnp.sum(do_ref[...] * o_ref[...], axis=-1, keepdims=True)
        do = do_ref[...].astype(BF16)

        def rows_below(ext):
            k, v = k_ref[0, :ext, :], v_ref[0, :ext, :]
            s = _dot(q, k, NT) * MLA_SCALE
            row = i * tq + lax.broadcasted_iota(jnp.int32, (tq, ext), 0)
            col = lax.broadcasted_iota(jnp.int32, (tq, ext), 1)
            p = jnp.where(row >= col, jnp.exp(s - lse), 0.0)
            ds = (p * (_dot(do, v, NT) - delta) * MLA_SCALE).astype(BF16)
            dq_ref[0] = _dot(ds, k)
            dk_ref[0, :ext, :] += _dot(ds, q, TN)
            dv_ref[0, :ext, :] += _dot(p.astype(BF16), do, TN)

        for g in range(T // MLA_EXT):
            pl.when(i // (MLA_EXT // tq) == g)(lambda g=g: rows_below((g + 1) * MLA_EXT))

    return pl.pallas_call(
        body, name="mla_bwd",
        out_shape=[jax.ShapeDtypeStruct((H_MLA, T, QK), F32), jax.ShapeDtypeStruct((H_MLA, T, QK), F32),
                   jax.ShapeDtypeStruct((H_MLA, T, VDIM), F32)],
        grid=(H_MLA, T // tq),
        in_specs=[pl.BlockSpec((1, tq, QK), lambda h, i: (h, i, 0)),
                  pl.BlockSpec((1, T, QK), lambda h, i: (h, 0, 0)),
                  pl.BlockSpec((1, T, VDIM), lambda h, i: (h, 0, 0)),
                  pl.BlockSpec((tq, VDIM), lambda h, i: (i, h)),
                  pl.BlockSpec((tq, VDIM), lambda h, i: (i, h)),
                  pl.BlockSpec((1, tq, 1), lambda h, i: (h, i, 0))],
        out_specs=[pl.BlockSpec((1, tq, QK), lambda h, i: (h, i, 0)),
                   pl.BlockSpec((1, T, QK), lambda h, i: (h, 0, 0)),
                   pl.BlockSpec((1, T, VDIM), lambda h, i: (h, 0, 0))],
        compiler_params=_params(("parallel", "arbitrary")),
    )(q, k, v, o, do, lse)


NBLK = T // BLK


BAND_GL = 256
BAND_GH = BAND_GL // DIL_D
BAND_ROWS = BAND_GH * BLK
BAND_GROUPS = DIL_W // BAND_GL


def _band_masks():
    r = lax.broadcasted_iota(jnp.int32, (BAND_ROWS, BLK), 0) & (BLK - 1)
    j = lax.broadcasted_iota(jnp.int32, (BAND_ROWS, BLK), 1)
    return j <= r, j >= r


def _head_lanes():
    lane_head = lax.broadcasted_iota(jnp.int32, (1, BAND_GL), 1) // DIL_D
    return [lane_head == h for h in range(BAND_GH)]


def _stack_heads(x, lanes):
    return jnp.concatenate([jnp.where(m, x, jnp.zeros_like(x)) for m in lanes], axis=0)


def _merge_heads(xs, lanes):
    out = None
    for h, m in enumerate(lanes):
        part = jnp.where(m, xs[h * BLK:(h + 1) * BLK], 0.0)
        out = part if out is None else out + part
    return out


def _per_head(x, lanes):
    return jnp.concatenate([jnp.sum(jnp.where(m, x, 0.0), axis=-1, keepdims=True) for m in lanes], axis=0)


def _lane_group(ref, g):
    return ref[0, :, g * BAND_GL:(g + 1) * BAND_GL]


def _seq_start(p, i):
    per_seq = lax.shift_right_logical(jnp.int32(NBLK), 2 * p)
    return lax.rem(i, per_seq) == 0


def _band_fwd(qkv):
    def body(q_ref, kp_ref, kc_ref, vp_ref, vc_ref, o_ref, lse_ref):
        p, i = pl.program_id(0), pl.program_id(1)
        has_prev = jnp.logical_not(_seq_start(p, i))
        m_cur, m_prev = _band_masks()
        m_prev = m_prev & has_prev
        lanes = _head_lanes()
        for g in range(BAND_GROUPS):
            qs = _stack_heads(_lane_group(q_ref, g), lanes)
            s_c = jnp.where(m_cur, _dot(qs, _lane_group(kc_ref, g), NT) * DIL_SCALE, -jnp.inf)
            s_p = jnp.where(m_prev, _dot(qs, _lane_group(kp_ref, g), NT) * DIL_SCALE, -jnp.inf)
            m = jnp.maximum(jnp.max(s_c, axis=-1, keepdims=True), jnp.max(s_p, axis=-1, keepdims=True))
            e_c, e_p = jnp.exp(s_c - m), jnp.exp(s_p - m)
            l = jnp.sum(e_c, axis=-1, keepdims=True) + jnp.sum(e_p, axis=-1, keepdims=True)
            os = (_dot((e_p / l).astype(BF16), _lane_group(vp_ref, g))
                  + _dot((e_c / l).astype(BF16), _lane_group(vc_ref, g)))
            cols = slice(g * BAND_GL, (g + 1) * BAND_GL)
            o_ref[0, :, cols] = _merge_heads(os, lanes)
            lse_ref[0, :, cols] = _merge_heads(m + jnp.log(l), lanes)

    blk = (1, BLK, DIL_W)
    cur = lambda part: (lambda p, i: (p, i, part))
    prev = lambda part: (lambda p, i: (p, jnp.maximum(i - 1, 0), part))
    return pl.pallas_call(
        body, name="band_fwd",
        out_shape=[jax.ShapeDtypeStruct((3, T, DIL_W), F32)] * 2,
        grid=(3, NBLK),
        in_specs=[pl.BlockSpec(blk, cur(0)), pl.BlockSpec(blk, prev(1)), pl.BlockSpec(blk, cur(1)),
                  pl.BlockSpec(blk, prev(2)), pl.BlockSpec(blk, cur(2))],
        out_specs=[pl.BlockSpec(blk, cur(0)), pl.BlockSpec(blk, cur(0))],
        compiler_params=_params(("parallel", "parallel")),
    )(qkv, qkv, qkv, qkv, qkv)


def _band_bwd(qkv, o, lse, do, dlse):
    def body(qc_ref, qn_ref, kp_ref, kc_ref, vp_ref, vc_ref, oc_ref, on_ref, lc_ref, ln_ref,
             doc_ref, don_ref, dlc_ref, dln_ref, dqkv_ref):
        p, i = pl.program_id(0), pl.program_id(1)
        has_prev = jnp.logical_not(_seq_start(p, i))
        has_next = jnp.logical_not(_seq_start(p, i + 1)) & (i + 1 < NBLK)
        m_cur, m_prev = _band_masks()
        lanes = _head_lanes()

        def probs(qs, k, lse, mask):
            return jnp.where(mask, jnp.exp(_dot(qs, k, NT) * DIL_SCALE - lse), 0.0)

        def dscore(pr, dos, v, shift):
            return (pr * (_dot(dos, v, NT) + shift) * DIL_SCALE).astype(BF16)

        for g in range(BAND_GROUPS):
            grp = lambda ref: _lane_group(ref, g)
            kp, kc, vp, vc = grp(kp_ref), grp(kc_ref), grp(vp_ref), grp(vc_ref)
            qc, qn = _stack_heads(grp(qc_ref), lanes), _stack_heads(grp(qn_ref), lanes)
            doc, don = grp(doc_ref), grp(don_ref)
            lse_c = _per_head(grp(lc_ref), lanes) * (1.0 / DIL_D)
            lse_n = _per_head(grp(ln_ref), lanes) * (1.0 / DIL_D)
            sh_c = _per_head(grp(dlc_ref) - doc * grp(oc_ref), lanes)
            sh_n = _per_head(grp(dln_ref) - don * grp(on_ref), lanes)
            doc, don = _stack_heads(doc.astype(BF16), lanes), _stack_heads(don.astype(BF16), lanes)
            p_cc = probs(qc, kc, lse_c, m_cur)
            p_cp = probs(qc, kp, lse_c, m_prev & has_prev)
            p_nc = probs(qn, kc, lse_n, m_prev & has_next)
            ds_cc = dscore(p_cc, doc, vc, sh_c)
            ds_cp = dscore(p_cp, doc, vp, sh_c)
            ds_nc = dscore(p_nc, don, vc, sh_n)
            cols = lambda part: slice(part * DIL_W + g * BAND_GL, part * DIL_W + (g + 1) * BAND_GL)
            dqkv_ref[0, :, cols(0)] = _merge_heads(_dot(ds_cc, kc) + _dot(ds_cp, kp), lanes)
            dqkv_ref[0, :, cols(1)] = _dot(ds_cc, qc, TN) + _dot(ds_nc, qn, TN)
            dqkv_ref[0, :, cols(2)] = _dot(p_cc.astype(BF16), doc, TN) + _dot(p_nc.astype(BF16), don, TN)

    blk = (1, BLK, DIL_W)
    cur = lambda part: pl.BlockSpec(blk, lambda p, i: (p, i, part))
    prev = lambda part: pl.BlockSpec(blk, lambda p, i: (p, jnp.maximum(i - 1, 0), part))
    nxt = lambda part: pl.BlockSpec(blk, lambda p, i: (p, jnp.minimum(i + 1, NBLK - 1), part))
    w, wn = cur(0), nxt(0)
    return pl.pallas_call(
        body, name="band_bwd",
        out_shape=jax.ShapeDtypeStruct((3, T, 3 * DIL_W), F32),
        grid=(3, NBLK),
        in_specs=[cur(0), nxt(0), prev(1), cur(1), prev(2), cur(2), w, wn, w, wn, w, wn, w, wn],
        out_specs=pl.BlockSpec((1, BLK, 3 * DIL_W), lambda p, i: (p, i, 0)),
        compiler_params=_params(("parallel", "parallel")),
    )(qkv, qkv, qkv, qkv, qkv, qkv, o, o, lse, lse, do, do, dlse, dlse)


SCAN_TC = 256


def _scan_fwd(bu, ar, ai):
    tc, S = SCAN_TC, SSM_S

    def body(bu_ref, ar_ref, ai_ref, h_ref, h16_ref, cr_ref, ci_ref):
        @pl.when(pl.program_id(0) == 0)
        def _():
            cr_ref[...] = jnp.zeros_like(cr_ref)
            ci_ref[...] = jnp.zeros_like(ci_ref)

        a_r, a_i = ar_ref[...], ai_ref[...]

        def step(j, carry):
            hr, hi = carry
            for r in range(8):
                t = pl.multiple_of(j * 8, 8) + r
                br = bu_ref[pl.ds(t, 1), pl.ds(0, S)]
                bi = bu_ref[pl.ds(t, 1), pl.ds(S, S)]
                hr, hi = a_r * hr - a_i * hi + br, a_r * hi + a_i * hr + bi
                h_ref[pl.ds(t, 1), pl.ds(0, S)] = hr
                h_ref[pl.ds(t, 1), pl.ds(S, S)] = hi
            return hr, hi

        hr, hi = lax.fori_loop(0, tc // 8, step, (cr_ref[...], ci_ref[...]))
        cr_ref[...] = hr
        ci_ref[...] = hi
        h16_ref[...] = h_ref[...].astype(BF16)

    return pl.pallas_call(
        body, name="s5_scan_fwd",
        out_shape=[jax.ShapeDtypeStruct((T, 2 * S), F32), jax.ShapeDtypeStruct((T, 2 * S), BF16)],
        grid=(T // tc,),
        in_specs=[pl.BlockSpec((tc, 2 * S), lambda i: (i, 0)), _full_spec((1, S)), _full_spec((1, S))],
        out_specs=[pl.BlockSpec((tc, 2 * S), lambda i: (i, 0)), pl.BlockSpec((tc, 2 * S), lambda i: (i, 0))],
        scratch_shapes=[pltpu.VMEM((1, S), F32), pltpu.VMEM((1, S), F32)],
        compiler_params=_params(("arbitrary",)),
    )(bu, ar, ai)


def _scan_bwd(dh, h, ar, ai):
    tc, S = SCAN_TC, SSM_S
    nc = T // tc

    def body(dh_ref, h_ref, hp_ref, ar_ref, ai_ref, g16_ref, dar_ref, dai_ref, cr_ref, ci_ref, g_ref):
        i = pl.program_id(0)

        @pl.when(i == 0)
        def _():
            cr_ref[...] = jnp.zeros_like(cr_ref)
            ci_ref[...] = jnp.zeros_like(ci_ref)
            dar_ref[...] = jnp.zeros_like(dar_ref)
            dai_ref[...] = jnp.zeros_like(dai_ref)

        a_r, a_i = ar_ref[...], ai_ref[...]
        first_chunk = (i == nc - 1)
        edge = jnp.where(first_chunk, 0.0, 1.0)
        hpr = hp_ref[pl.ds(7, 1), pl.ds(0, S)] * edge
        hpi = hp_ref[pl.ds(7, 1), pl.ds(S, S)] * edge

        def step(jj, carry):
            gr, gi, dar, dai = carry
            j = tc // 8 - 1 - jj
            for r in range(7, -1, -1):
                t = pl.multiple_of(j * 8, 8) + r
                tp = jnp.maximum(t - 1, 0)
                inside = t > 0
                pr = jnp.where(inside, h_ref[pl.ds(tp, 1), pl.ds(0, S)], hpr)
                pi = jnp.where(inside, h_ref[pl.ds(tp, 1), pl.ds(S, S)], hpi)
                gr, gi = (dh_ref[pl.ds(t, 1), pl.ds(0, S)] + a_r * gr + a_i * gi,
                          dh_ref[pl.ds(t, 1), pl.ds(S, S)] + a_r * gi - a_i * gr)
                g_ref[pl.ds(t, 1), pl.ds(0, S)] = gr
                g_ref[pl.ds(t, 1), pl.ds(S, S)] = gi
                dar = dar + gr * pr + gi * pi
                dai = dai + gi * pr - gr * pi
            return gr, gi, dar, dai

        zero = jnp.zeros((1, S), F32)
        gr, gi, dar, dai = lax.fori_loop(0, tc // 8, step, (cr_ref[...], ci_ref[...], zero, zero))
        cr_ref[...] = gr
        ci_ref[...] = gi
        dar_ref[...] += dar
        dai_ref[...] += dai
        g16_ref[...] = g_ref[...].astype(BF16)

    rev = lambda i: (nc - 1 - i, 0)
    before = lambda i: (jnp.maximum((nc - 1 - i) * (tc // 8) - 1, 0), 0)
    return pl.pallas_call(
        body, name="s5_scan_bwd",
        out_shape=[jax.ShapeDtypeStruct((T, 2 * S), BF16), jax.ShapeDtypeStruct((1, S), F32),
                   jax.ShapeDtypeStruct((1, S), F32)],
        grid=(nc,),
        in_specs=[pl.BlockSpec((tc, 2 * S), rev), pl.BlockSpec((tc, 2 * S), rev), pl.BlockSpec((8, 2 * S), before),
                  _full_spec((1, S)), _full_spec((1, S))],
        out_specs=[pl.BlockSpec((tc, 2 * S), rev), _full_spec((1, S)), _full_spec((1, S))],
        scratch_shapes=[pltpu.VMEM((1, S), F32), pltpu.VMEM((1, S), F32), pltpu.VMEM((tc, 2 * S), F32)],
        compiler_params=_params(("arbitrary",)),
    )(dh, h, h, ar, ai)


def _sum_rows(name, x):
    def body(x_ref, o_ref):
        o_ref[...] = jnp.sum(x_ref[...], axis=0, keepdims=True)

    return pl.pallas_call(body, name=name, out_shape=jax.ShapeDtypeStruct((1, 1), F32),
                          in_specs=[_full_spec(x.shape)], out_specs=_full_spec((1, 1)), grid=(1,))(x)


ANY = pl.BlockSpec(memory_space=pl.ANY)


def _place():
    return lax.axis_index("x"), lax.axis_index("y"), lax.axis_index("c")


HBM = pl.BlockSpec(memory_space=pltpu.HBM)
SEM = pl.BlockSpec(memory_space=pltpu.SEMAPHORE)
DATAFLOW = pltpu.SideEffectType.DATAFLOW_SIDE_EFFECTING


def _hbm(a):
    return pltpu.with_memory_space_constraint(a, pltpu.HBM)


def _split_start(name, srcs, lands, ncopies, plan, after):
    ns, nl = len(srcs), len(lands)

    def body(*refs):
        send_sems, recv_sems = refs[ns + nl + 1], refs[ns + nl + 2]
        token = refs[-1]
        for k, (src, dst, peer, _) in enumerate(plan(refs[:ns], refs[ns:ns + nl])):
            pltpu.make_async_remote_copy(src_ref=src, dst_ref=dst, send_sem=send_sems.at[k], recv_sem=recv_sems.at[k],
                                         device_id=peer, device_id_type=MESH).start()
        token[...] = jnp.zeros_like(token)

    out = pl.pallas_call(
        body, name=name,
        out_shape=(pltpu.SemaphoreType.DMA((ncopies,)), pltpu.SemaphoreType.DMA((ncopies,)),
                   *[pltpu.HBM(a.shape, a.dtype) for a in srcs], *[pltpu.HBM(a.shape, a.dtype) for a in lands],
                   jax.ShapeDtypeStruct((8, LANES), F32)),
        in_specs=[HBM] * (ns + nl) + [ANY],
        out_specs=(SEM, SEM, *[HBM] * (ns + nl), pl.BlockSpec(memory_space=pltpu.VMEM)),
        input_output_aliases={i: 2 + i for i in range(ns + nl)},
        compiler_params=pltpu.CompilerParams(has_side_effects=DATAFLOW),
    )(*[_hbm(a) for a in srcs], *[_hbm(a) for a in lands], after)
    return out[0], out[1], list(out[2:2 + ns]), list(out[2 + ns:2 + ns + nl]), out[-1]


def _split_wait(name, handle, ncopies, plan, after):
    send_sems, recv_sems, srcs, lands, _ = handle
    ns, nl = len(srcs), len(lands)

    def body(*refs):
        s_sems, r_sems = refs[ns + nl], refs[ns + nl + 1]
        for k, (src, dst, peer, mine) in enumerate(plan(refs[:ns], refs[ns:ns + nl])):
            pltpu.make_async_remote_copy(src_ref=src, dst_ref=dst, send_sem=s_sems.at[k], recv_sem=r_sems.at[k],
                                         device_id=peer, device_id_type=MESH).wait_send()
            pltpu.make_async_remote_copy(src_ref=src, dst_ref=mine, send_sem=s_sems.at[k], recv_sem=r_sems.at[k],
                                         device_id=peer, device_id_type=MESH).wait_recv()

    out = pl.pallas_call(
        body, name=name,
        out_shape=(*[pltpu.HBM(a.shape, a.dtype) for a in srcs], *[pltpu.HBM(a.shape, a.dtype) for a in lands]),
        in_specs=[HBM] * (ns + nl) + [SEM, SEM, ANY],
        out_specs=tuple([HBM] * (ns + nl)),
        input_output_aliases={i: i for i in range(ns + nl)},
        compiler_params=pltpu.CompilerParams(has_side_effects=DATAFLOW),
    )(*srcs, *lands, send_sems, recv_sems, after)
    return list(out[:ns]), list(out[ns:])


def _slot(px, py, pc):
    return 4 * px + 2 * py + pc


def _gather_plan(xs, lands):
    x, y, c = _place()
    peers = [(x, y, 1 - c), (1 - x, y, c), (x, 1 - y, c), (1 - x, 1 - y, c)]
    return [(xs[t], lands[t].at[_slot(x, y, c)], peer, lands[t].at[_slot(*peer)])
            for t in range(len(xs)) for peer in peers]


def _gather_start(name, shards, after):
    lands = [lax.empty((N_DEV,) + s.shape, s.dtype) for s in shards]
    return _split_start(name, shards, lands, 4 * len(shards), _gather_plan, after)


def _pass_on_plan(_, lands):
    x, y, c = _place()
    blocks = [((1 - x, y, c), (1 - x, y, 1 - c)), ((x, 1 - y, c), (x, 1 - y, 1 - c)),
              ((1 - x, 1 - y, c), (1 - x, 1 - y, 1 - c)), ((x, y, 1 - c), (x, y, c))]
    return [(lands[t].at[_slot(*out)], lands[t].at[_slot(*out)], (x, y, 1 - c), lands[t].at[_slot(*back)])
            for t in range(len(lands)) for out, back in blocks]


def _gather_pass_on(name, handle, after):
    n = len(handle[2])
    _, lands = _split_wait(name + "_wait", handle, 4 * n, _gather_plan, after)
    return _split_start(name + "_pass_on", [], lands, 4 * n, _pass_on_plan, after)


def _gather_finish(name, handle, after):
    n = len(handle[3])
    _, lands = _split_wait(name + "_done", handle, 4 * n, _pass_on_plan, after)
    return lands


def _sibling_plan(ps, gots):
    x, y, c = _place()
    return [(ps[t].at[j, 1 - c], gots[t].at[j], (x, y, 1 - c), gots[t].at[j]) for t in range(len(ps)) for j in range(4)]


def _chips_plan(ss, gots):
    x, y, c = _place()
    chips = [(1 - x, y), (x, 1 - y), (1 - x, 1 - y)]
    return [(ss[t].at[2 * px + py], gots[t].at[k], (px, py, c), gots[t].at[k])
            for t in range(len(ss)) for k, (px, py) in enumerate(chips)]


BF16_ROWS = 16


def _shard_tiles(R, C, block_bytes):
    tr = _tile(R, max(BF16_ROWS, min(1024, block_bytes // (C * 4))), BF16_ROWS)
    tc = C if tr * C * 4 <= 2 * block_bytes else _tile(C, max(LANES, block_bytes // (tr * 4)))
    return tr, tc


def _pair_sum(name, p4, got, core):
    _, _, R, C = p4.shape
    tr, tc = _shard_tiles(R, C, ROW_BLOCK_BYTES)

    def body(core_ref, p_ref, g_ref, o_ref):
        o_ref[...] = (p_ref[:, 0] + g_ref[...]).astype(o_ref.dtype)

    return pl.pallas_call(
        body, name=name, out_shape=jax.ShapeDtypeStruct((4, R, C), BF16),
        grid_spec=pltpu.PrefetchScalarGridSpec(
            num_scalar_prefetch=1, grid=(4, R // tr, C // tc),
            in_specs=[pl.BlockSpec((1, 1, tr, tc), lambda j, i, k, core: (j, core[0], i, k)),
                      pl.BlockSpec((1, tr, tc), lambda j, i, k, core: (j, i, k))],
            out_specs=pl.BlockSpec((1, tr, tc), lambda j, i, k, core: (j, i, k))),
        compiler_params=_params(("parallel", "parallel", "parallel")),
    )(core, p4, got)


def _sum_devices(name, g8):
    _, R, C = g8.shape
    tr = _tile(R, SMALL_ROW_ALIGN, 8)

    def body(g_ref, o_ref):
        acc = g_ref[0]
        for d in range(1, N_DEV):
            acc = acc + g_ref[d]
        o_ref[...] = acc

    return pl.pallas_call(
        body, name=name, out_shape=jax.ShapeDtypeStruct((R, C), F32), grid=(R // tr,),
        in_specs=[pl.BlockSpec((N_DEV, tr, C), lambda i: (0, i, 0))], out_specs=pl.BlockSpec((tr, C), lambda i: (i, 0)),
        compiler_params=_params(("parallel",)),
    )(g8)


def _adamw_shard(name, layer, w, m, v, s4, got, chip, bufs):
    _, R, C = w.shape
    tr, tc = _shard_tiles(R, C, ROW_BLOCK_BYTES // 2)

    def body(chip_ref, w_ref, m_ref, v_ref, s_ref, g_ref, b0, b1, b2, b3, og, od, om, ov):
        f = lambda a: a.astype(F32)
        g = ((f(s_ref[0]) + f(g_ref[0])) + f(g_ref[1])) + f(g_ref[2])
        d, nm, nv = _adamw_fn(w_ref[0], g, m_ref[0], v_ref[0])
        og[0], od[0], om[0], ov[0] = g, d, nm, nv

    lay = pl.BlockSpec((1, tr, tc), lambda i, k, chip: (layer, i, k))
    if bufs is None:
        bufs = [lax.empty(w.shape, F32) for _ in range(4)]
    return pl.pallas_call(
        body, name=name, out_shape=[jax.ShapeDtypeStruct(w.shape, F32)] * 4,
        grid_spec=pltpu.PrefetchScalarGridSpec(
            num_scalar_prefetch=1, grid=(R // tr, C // tc),
            in_specs=[lay, lay, lay, pl.BlockSpec((1, tr, tc), lambda i, k, chip: (chip[0], i, k)),
                      pl.BlockSpec((3, tr, tc), lambda i, k, chip: (0, i, k)), ANY, ANY, ANY, ANY],
            out_specs=[lay] * 4),
        input_output_aliases={6: 0, 7: 1, 8: 2, 9: 3},
        compiler_params=_params(("parallel", "parallel")),
    )(chip, w, m, v, s4, got, *bufs)


def _adamw(name, wt, g, m, v):
    shape = wt.shape
    two = (lambda a: a.reshape(1, -1)) if wt.ndim == 1 else (lambda a: a.reshape(-1, shape[-1]))
    w2, g2, m2, v2 = two(wt), two(g), two(m), two(v)
    tr = _row(w2, None, 0)[2]
    outs = [(w2.shape, F32, 0, tr)] * 3
    d, nm, nv = _rows_fwd(name, _adamw_fn, [_row(a, tr, 0) for a in (w2, g2, m2, v2)], [], outs)
    return d.reshape(shape), nm.reshape(shape), nv.reshape(shape)


def _lane_tiling():
    return (jnp.arange(SSM_N)[:, None] == (jnp.arange(SSM_S) % SSM_N)[None, :]).astype(BF16)


def _own_block():
    r = lax.broadcasted_iota(jnp.int32, (SSM_G * SSM_P, SSM_S), 0) // SSM_P
    c = lax.broadcasted_iota(jnp.int32, (SSM_G * SSM_P, SSM_S), 1) // SSM_N
    return r == c


def _bd_build(name, v_re, v_im, sign, tiling):
    def body(r_ref, i_ref, t_ref, o_ref):
        own = _own_block()
        o_ref[:, :SSM_S] = jnp.where(own, _dot(r_ref[...].astype(BF16), t_ref[...]), 0.0).astype(BF16)
        o_ref[:, SSM_S:] = jnp.where(own, sign * _dot(i_ref[...].astype(BF16), t_ref[...]), 0.0).astype(BF16)

    rows = SSM_G * SSM_P
    return pl.pallas_call(
        body, name=name, out_shape=jax.ShapeDtypeStruct((rows, 2 * SSM_S), BF16), grid=(1,),
        in_specs=[_full_spec((rows, SSM_N))] * 2 + [_full_spec((SSM_N, SSM_S))], out_specs=_full_spec((rows, 2 * SSM_S)),
        compiler_params=_params(("arbitrary",)),
    )(v_re, v_im, tiling)


def _bd_extract(name, m, sign, tiling):
    def body(m_ref, t_ref, r_ref, i_ref):
        own, t = _own_block(), t_ref[...]

        def pick(x):
            x = jnp.where(own, x, 0.0)
            hi = x.astype(BF16)
            rest = x - hi.astype(F32)
            mid = rest.astype(BF16)
            lo = (rest - mid.astype(F32)).astype(BF16)
            return _dot(hi, t, NT) + _dot(mid, t, NT) + _dot(lo, t, NT)

        r_ref[...] = pick(m_ref[:, :SSM_S])
        i_ref[...] = sign * pick(m_ref[:, SSM_S:])

    rows = SSM_G * SSM_P
    return pl.pallas_call(
        body, name=name, out_shape=[jax.ShapeDtypeStruct((rows, SSM_N), F32)] * 2, grid=(1,),
        in_specs=[_full_spec((rows, 2 * SSM_S)), _full_spec((SSM_N, SSM_S))], out_specs=[_full_spec((rows, SSM_N))] * 2,
        compiler_params=_params(("arbitrary",)),
    )(m, tiling)


def _fold(a, dil):
    if dil == 1:
        return a
    return a.reshape((T // dil, dil) + a.shape[1:]).swapaxes(0, 1).reshape(a.shape)


def _unfold(a, dil):
    if dil == 1:
        return a
    return a.reshape((dil, T // dil) + a.shape[1:]).swapaxes(0, 1).reshape(a.shape)


DILS = (1, 4, 16)


def _fold3(parts):
    parts = [parts] * 3 if not isinstance(parts, (list, tuple)) else parts
    return jnp.stack([_fold(a, d) for a, d in zip(parts, DILS)])


def _unfold3(a):
    return [_unfold(a[p], d) for p, d in enumerate(DILS)]


def _rope_tables():
    half = ROPE // 2
    inv_freq = ROPE_THETA ** (-jnp.arange(half, dtype=F32) / half)
    ang = jnp.arange(T).astype(F32)[:, None] * inv_freq[None, :]
    i, j = jnp.arange(ROPE)[:, None], jnp.arange(ROPE)[None, :]
    rot = jnp.where(i == j + half, -1.0, jnp.where(i + half == j, 1.0, 0.0)).astype(F32)
    return jnp.tile(jnp.cos(ang), (1, 2)), jnp.tile(jnp.sin(ang), (1, 2)), rot


def _rot_half(x, rot):
    return _dot(x.reshape(-1, ROPE), rot, prec=HI).reshape(x.shape)


def _mla_pack_fn(q, kv, k_rope, cos, sin, rot):
    rope = lambda x: x * cos + _rot_half(x, rot) * sin
    q_out = jnp.concatenate([q[:, :, :NOPE], rope(q[:, :, NOPE:])], axis=-1)
    k_pe = jnp.broadcast_to(rope(k_rope)[None], (H_MLA,) + k_rope.shape)
    return q_out, jnp.concatenate([kv[:, :, :NOPE], k_pe], axis=-1), kv[:, :, NOPE:]


def _mla_unpack_fn(dq, dk, dv, cos, sin, rot):
    unrope = lambda g: g * cos - _rot_half(g * sin, rot)
    dq_out = jnp.concatenate([dq[:, :, :NOPE], unrope(dq[:, :, NOPE:])], axis=-1)
    dk_rope = unrope(jnp.sum(dk[:, :, NOPE:], axis=0))
    return dq_out, jnp.concatenate([dk[:, :, :NOPE], dv], axis=-1), dk_rope


def _flat(w8):
    return w8.reshape(-1, w8.shape[-1])


def _blocks(m):
    return m.reshape(N_DEV, -1, m.shape[-1])


def _behind(a, tok):
    return a if tok is None else a + tok


def _mixer_fwd_in(x, w, sp, rope):
    s = {}
    s['x'] = x
    h = _rms_fwd("rms_mix", x, sp['g_mix'])
    proj = _mm("mm_in", h, _flat(w['w_in']), 'nt')
    offs = np.cumsum((0,) + IN_SPLITS)
    c_q, c_kv, k_rope, u = [proj[:, offs[i]:offs[i + 1]] for i in range(4)]
    qkv_d = proj[:, offs[4]:]
    s.update(h=h, c_q=c_q, c_kv=c_kv, u=u)

    cqn = _rms_fwd("rms_q", c_q, sp['g_q'])
    ckvn = _rms_fwd("rms_kv", c_kv, sp['g_kv'])
    q8 = _mm("mm_uq", cqn, w['w_uq'], 'nt', bb='r', ob='c')
    kv8 = _mm("mm_ukv", ckvn, w['w_ukv'], 'nn', bb='c', ob='c')
    cos, sin, rot = rope
    tr = MLA_TQ
    qh, kh, vh = _rows_fwd(
        "mla_pack", _mla_pack_fn,
        [_row(q8, tr, 1), _row(kv8, tr, 1), _row(k_rope, tr, 0), _row(cos, tr, 0), _row(sin, tr, 0)], [rot],
        [((H_MLA, T, QK), BF16, 1, tr), ((H_MLA, T, QK), BF16, 1, tr), ((H_MLA, T, VDIM), BF16, 1, tr)])
    y_mla, lse_mla = _mla_fwd(qh, kh, vh)
    s.update(cqn=cqn, ckvn=ckvn, qh=qh, kh=kh, vh=vh, lse_mla=lse_mla, y_mla=y_mla, qkv_d=qkv_d)
    return s


def _mixer_fwd_out(s, w, sp, tok=None, after_ssm=None):
    x, u, y_mla, qkv_d = s['x'], s['u'], s['y_mla'], s['qkv_d']
    a3 = lambda n: sp[n].reshape(SSM_G, 1, SSM_N)
    b2 = lambda n: sp[n].transpose(0, 2, 1).reshape(SSM_G * SSM_P, SSM_N)
    disc_rows = [_row(a3('a_re'), 1, 0), _row(a3('a_im'), 1, 0), _row(sp['log_dt'].reshape(SSM_G, 1, 1), 1, 0),
                 _row(b2('b_re'), SSM_P, 0), _row(b2('b_im'), SSM_P, 0)]
    abr, abi, bbr, bbi = _rows_fwd(
        "s5_disc", _s5_disc_fn, disc_rows, [],
        [((SSM_G, 1, SSM_N), F32, 0, 1), ((SSM_G, 1, SSM_N), F32, 0, 1),
         ((SSM_G * SSM_P, SSM_N), F32, 0, SSM_P), ((SSM_G * SSM_P, SSM_N), F32, 0, SSM_P)])
    ar, ai = abr.reshape(1, SSM_S), abi.reshape(1, SSM_S)
    tiling = _lane_tiling()
    b_mat = _bd_build("s5_b_matrix", bbr, bbi, 1.0, tiling)
    c2 = lambda n: sp[n].reshape(SSM_G * SSM_P, SSM_N)
    c_mat = _bd_build("s5_c_matrix", c2('c_re'), c2('c_im'), -1.0, tiling)
    u16 = _behind(u, tok).astype(BF16)
    bu = _mm("mm_s5_b", u16, b_mat, 'nn')
    hst, hst16 = _scan_fwd(bu, ar, ai)
    ymm = _mm("mm_s5_c", hst16, c_mat, 'nt')
    d_row = sp['d_skip'].reshape(1, SSM_W)
    (yg,) = _rows_fwd("s5_act", _s5_act_fn, [_row(ymm), _row(u)], [d_row], [((T, SSM_W), BF16, -2, _row(u)[2])])
    z = _mm("mm_glu", yg, w['w_glu'], 'nn', bb='c')
    glu_rows = [_row(z[:, :SSM_W]), _row(z[:, SSM_W:])]
    glu_b = [sp['b_glu'][:SSM_W].reshape(1, -1), sp['b_glu'][SSM_W:].reshape(1, -1)]
    (y_ssm,) = _rows_fwd("s5_glu", _glu_fn, glu_rows, glu_b, [((T, SSM_W), F32, -2, glu_rows[0][2])])
    if after_ssm is not None:
        qkv_d = _behind(qkv_d, after_ssm(y_ssm))
    s.update(disc_rows=disc_rows, ar=ar, ai=ai, b_mat=b_mat, c_mat=c_mat, hst=hst, hst16=hst16, u16=u16, ymm=ymm,
             d_row=d_row, yg=yg,
             glu_rows=glu_rows, glu_b=glu_b)

    qkv_f = _fold3(qkv_d).astype(BF16)
    o_f, lse_f = _band_fwd(qkv_f)
    mix_rows = [_row(a) for a in _unfold3(o_f) + _unfold3(lse_f)]
    (y_dil,) = _rows_fwd("dil_mix", _dil_mix_fn, mix_rows, [], [((T, DIL_W), F32, -2, mix_rows[0][2])])
    s.update(qkv_f=qkv_f, o_f=o_f, lse_f=lse_f, mix_rows=mix_rows)

    gm, gs, gd = sp['g_out_mla'].reshape(1, -1), sp['g_out_ssm'].reshape(1, -1), sp['g_out_dil'].reshape(1, -1)
    on_rows = [_row(y_mla), _row(y_ssm), _row(y_dil)]
    (ycat,) = _rows_fwd("out_norm", _outnorm_fn, on_rows, [gm, gs, gd], [((T, D), BF16, -2, on_rows[0][2])])
    x1_ = _mm("mm_o", ycat, _flat(w['w_o']), 'nn', res=x)
    s.update(on_rows=on_rows, on_g=[gm, gs, gd], ycat=ycat, x1=x1_)
    del s['qkv_d']
    return x1_


def _ffn_fwd(x1_, w, sp, s, tok=None):
    h2 = _rms_fwd("rms_ffn", x1_, _behind(sp['g_ffn'], tok))
    ga = _mm("mm_gate", h2, _flat(w['w_gate']), 'nt')
    gb, zf = _mm("mm_up", h2, _flat(w['w_up']), 'nt',
                 epilogue=(lambda up, gate: (up, _swiglu_fn(gate, up)), [ga], [F32, BF16]))
    x2_ = _mm("mm_down", zf, _flat(w['w_down']), 'nn', res=x1_)
    s.update(h2=h2, ga=ga, gb=gb, zf=zf)
    return x2_


def _b16(a):
    return a.astype(BF16)


def _ffn_bwd(dx2, s, w, sp, tok=None):
    gw, gs_ = {}, {}
    b16 = _b16
    dx2b = b16(_behind(dx2, tok))
    def gate_pullback(dz, gate, up):
        return jax.vjp(_swiglu_fn, gate, up)[1](dz)

    dga, dgb = _mm("mm_down_dx", dx2b, _flat(w['w_down']), 'nt', epilogue=(gate_pullback, [s['ga'], s['gb']], [BF16, BF16]))
    gw['w_down'] = _blocks(_mm("mm_down_dw", s['zf'], dx2b, 'tn'))
    gw['w_gate'] = _blocks(_mm("mm_gate_dw", dga, s['h2'], 'tn'))
    gw['w_up'] = _blocks(_mm("mm_up_dw", dgb, s['h2'], 'tn'))
    dh2 = _mm("mm_up_dx", dgb, _flat(w['w_up']), 'nn', res=_mm("mm_gate_dx", dga, _flat(w['w_gate']), 'nn'))
    dx1, gs_['g_ffn'] = _rms_bwd("rms_ffn_bwd", s['x1'], sp['g_ffn'], dh2, dx2)
    return dx1, gw, gs_


def _mixer_bwd_out(dx1, s, w, sp, tok=None):
    gw, gs_ = {}, {}
    b16 = _b16
    dx1b = b16(_behind(dx1, tok))
    dycat = _mm("mm_o_dx", dx1b, _flat(w['w_o']), 'nt')
    gw['w_o'] = _blocks(_mm("mm_o_dw", s['ycat'], dx1b, 'tn'))
    dy_mla, dy_ssm, dy_dil, gs_['g_out_mla'], gs_['g_out_ssm'], gs_['g_out_dil'] = _rows_vjp(
        "out_norm_bwd", _outnorm_fn, s['on_rows'], s['on_g'], [], [_row(dycat)])

    dmix = _rows_vjp("dil_mix_bwd", _dil_mix_fn, s['mix_rows'], [], [], [_row(dy_dil)])
    dqkv_f = _band_bwd(s['qkv_f'], s['o_f'], s['lse_f'], _fold3(dmix[:3]), _fold3(dmix[3:]))
    dqkv_d = sum(_unfold3(dqkv_f))

    dz1, dz2, db1, db2 = _rows_vjp("s5_glu_bwd", _glu_fn, s['glu_rows'], s['glu_b'], [], [_row(dy_ssm)])
    gs_['b_glu'] = jnp.concatenate([db1, db2], axis=1)
    dzb = b16(jnp.concatenate([dz1, dz2], axis=1))
    dyg = _mm("mm_glu_dx", dzb, w['w_glu'], 'nt', bb='c')
    gw['w_glu'] = _mm("mm_glu_dw", s['yg'], dzb, 'tn', ob='c')
    dymm, du_act, dd = _rows_vjp("s5_act_bwd", _s5_act_fn, [_row(s['ymm']), _row(s['u'])], [s['d_row']], [], [_row(dyg)],
                                 grad_dtypes=[BF16, F32])
    gs_['d_skip'] = dd
    dhst = _mm("mm_s5_c_dx", dymm, s['c_mat'], 'nn')
    dc_mat = _mm("mm_s5_c_dw", dymm, s['hst16'], 'tn')
    g, dar, dai = _scan_bwd(dhst, s['hst'], s['ar'], s['ai'])
    du = _mm("mm_s5_b_dx", g, s['b_mat'], 'nt', res=du_act)
    db_mat = _mm("mm_s5_b_dw", s['u16'], g, 'tn')
    tiling = _lane_tiling()
    gs_['c_re'], gs_['c_im'] = _bd_extract("s5_c_blocks", dc_mat, -1.0, tiling)
    dbbr, dbbi = _bd_extract("s5_b_blocks", db_mat, 1.0, tiling)
    disc_cts = [_row(dar.reshape(SSM_G, 1, SSM_N), 1, 0), _row(dai.reshape(SSM_G, 1, SSM_N), 1, 0),
                _row(dbbr, SSM_P, 0), _row(dbbi, SSM_P, 0)]
    da_re, da_im, dldt, db_r, db_i = _rows_vjp("s5_disc_bwd", _s5_disc_fn, s['disc_rows'], [], [], disc_cts)
    gs_['a_re'], gs_['a_im'], gs_['log_dt'] = da_re, da_im, dldt
    unb = lambda a: a.reshape(SSM_G, SSM_P, SSM_N).transpose(0, 2, 1)
    gs_['b_re'], gs_['b_im'] = unb(db_r), unb(db_i)
    return (dy_mla, du, dqkv_d), gw, gs_


def _mixer_bwd_in(cts, dx1, s, w, sp, rope, tok=None):
    gw, gs_ = {}, {}
    b16 = _b16
    dy_mla, du, dqkv_d = cts
    dqh, dkh, dvh = _mla_bwd(s['qh'], s['kh'], s['vh'], s['y_mla'], dy_mla, _behind(s['lse_mla'], tok))
    cos, sin, rot = rope
    tr = MLA_TQ
    dq8, dkv8, dk_rope = _rows_fwd(
        "mla_unpack", _mla_unpack_fn,
        [_row(dqh, tr, 1), _row(dkh, tr, 1), _row(dvh, tr, 1), _row(cos, tr, 0), _row(sin, tr, 0)], [rot],
        [((H_MLA, T, QK), BF16, 1, tr), ((H_MLA, T, NOPE + VDIM), BF16, 1, tr), ((T, ROPE), F32, 0, tr)])
    dcqn = _mm("mm_uq_dx", dq8, w['w_uq'], 'nn', ab='c', bb='r')
    gw['w_uq'] = _mm("mm_uq_dw", dq8, s['cqn'], 'tn', ab='c', ob='r')
    dckvn = _mm("mm_ukv_dx", dkv8, w['w_ukv'], 'nt', ab='c', bb='c')
    gw['w_ukv'] = _mm("mm_ukv_dw", s['ckvn'], dkv8, 'tn', bb='c', ob='c')
    dc_q, gs_['g_q'] = _rms_bwd("rms_q_bwd", s['c_q'], sp['g_q'], dcqn)
    dc_kv, gs_['g_kv'] = _rms_bwd("rms_kv_bwd", s['c_kv'], sp['g_kv'], dckvn)

    dproj = b16(jnp.concatenate([dc_q, dc_kv, dk_rope, du, dqkv_d], axis=1))
    dh = _mm("mm_in_dx", dproj, _flat(w['w_in']), 'nn')
    gw['w_in'] = _blocks(_mm("mm_in_dw", dproj, s['h'], 'tn'))
    dx, gs_['g_mix'] = _rms_bwd("rms_mix_bwd", s['x'], sp['g_mix'], dh, dx1)
    return dx, gw, gs_


def kernel(x, g_mix, w_in, g_q, w_uq, g_kv, w_ukv, a_re, a_im, b_re, b_im, c_re, c_im, d_skip, log_dt, w_glu, b_glu, g_out_mla, g_out_ssm, g_out_dil, w_o, g_ffn, w_gate, w_up, w_down, g_final, loss_target, m_g_mix, m_w_in, m_g_q, m_w_uq, m_g_kv, m_w_ukv, m_a_re, m_a_im, m_b_re, m_b_im, m_c_re, m_c_im, m_d_skip, m_log_dt, m_w_glu, m_b_glu, m_g_out_mla, m_g_out_ssm, m_g_out_dil, m_w_o, m_g_ffn, m_w_gate, m_w_up, m_w_down, m_g_final, v_g_mix, v_w_in, v_g_q, v_w_uq, v_g_kv, v_w_ukv, v_a_re, v_a_im, v_b_re, v_b_im, v_c_re, v_c_im, v_d_skip, v_log_dt, v_w_glu, v_b_glu, v_g_out_mla, v_g_out_ssm, v_g_out_dil, v_w_o, v_g_ffn, v_w_gate, v_w_up, v_w_down, v_g_final):
    W = dict(zip(PARAMS, (g_mix, w_in, g_q, w_uq, g_kv, w_ukv, a_re, a_im, b_re, b_im, c_re, c_im, d_skip, log_dt,
                          w_glu, b_glu, g_out_mla, g_out_ssm, g_out_dil, w_o, g_ffn, w_gate, w_up, w_down, g_final)))
    M = dict(zip(PARAMS, (m_g_mix, m_w_in, m_g_q, m_w_uq, m_g_kv, m_w_ukv, m_a_re, m_a_im, m_b_re, m_b_im, m_c_re,
                          m_c_im, m_d_skip, m_log_dt, m_w_glu, m_b_glu, m_g_out_mla, m_g_out_ssm, m_g_out_dil, m_w_o,
                          m_g_ffn, m_w_gate, m_w_up, m_w_down, m_g_final)))
    V = dict(zip(PARAMS, (v_g_mix, v_w_in, v_g_q, v_w_uq, v_g_kv, v_w_ukv, v_a_re, v_a_im, v_b_re, v_b_im, v_c_re,
                          v_c_im, v_d_skip, v_log_dt, v_w_glu, v_b_glu, v_g_out_mla, v_g_out_ssm, v_g_out_dil, v_w_o,
                          v_g_ffn, v_w_gate, v_w_up, v_w_down, v_g_final)))
    cx, cy, cc = _place()
    core = cc.astype(jnp.int32).reshape(1)
    chip = (2 * cx + cy).astype(jnp.int32).reshape(1)
    rope = _rope_tables()
    small = [{n: W[n][l] for n in SMALL} for l in range(DEPTH)]
    for sp in small:
        for n in ('g_mix', 'g_q', 'g_kv', 'g_ffn'):
            sp[n] = sp[n].reshape(1, -1)

    def tok_of(tokens):
        return sum(t[0, 0] for t in tokens) if tokens else None

    def shard_view(a, n):
        return a.swapaxes(1, 2) if BIG[n] == 't' else a

    def gather_start(l, group, names, after):
        return _gather_start(f"gather_{group}_start_{l}", [shard_view(W[n], n)[l].astype(BF16) for n in names], after)

    xa = x[0]
    h1_mix = gather_start(0, "mix", MIXER_W, jnp.zeros((8, LANES), F32))
    h1_ffn = gather_start(0, "ffn", FFN_W, h1_mix[4])
    h2_mix = _gather_pass_on("gather_mix_0", h1_mix, xa)
    saved, full = [], []
    tokens = [h2_mix[4], h1_ffn[4]]
    for l in range(DEPTH):
        last = l + 1 == DEPTH
        wm = dict(zip(MIXER_W, _gather_finish(f"gather_mix_{l}", h2_mix, xa)))
        sp = dict(small[l])
        sp['g_mix'] = _behind(sp['g_mix'], tok_of(tokens))
        s = _mixer_fwd_in(xa, wm, sp, rope)
        tokens = []
        first_ffn = {}
        if l == 0:
            h1_first = h1_ffn
            def mid(dep):
                first_ffn['h'] = _gather_pass_on("gather_ffn_0", h1_first, dep)
                return first_ffn['h'][4][0, 0]
        else:
            mid = None
        if not last:
            h1_mix = gather_start(l + 1, "mix", MIXER_W, s['y_mla'])
            h1_ffn = gather_start(l + 1, "ffn", FFN_W, h1_mix[4])
            tokens += [h1_mix[4], h1_ffn[4]]
        x1 = _mixer_fwd_out(s, wm, small[l], tok_of(tokens), mid)
        if l == 0:
            h2_ffn = first_ffn['h']
        tokens = []
        wf = dict(zip(FFN_W, _gather_finish(f"gather_ffn_{l}", h2_ffn, x1)))
        if not last:
            h2_mix = _gather_pass_on(f"gather_mix_{l + 1}", h1_mix, x1)
            tokens.append(h2_mix[4])
        xa = _ffn_fwd(x1, wf, small[l], s, tok_of(tokens))
        tokens = []
        if not last:
            h2_ffn = _gather_pass_on(f"gather_ffn_{l + 1}", h1_ffn, xa)
            tokens.append(h2_ffn[4])
        saved.append(s)
        full.append({**wm, **wf})
    gf = g_final.reshape(1, D)
    ones = jnp.ones((T, 1), F32)
    dxa, dgf, loss_rows = _rows_vjp("loss", _loss_fn, [_row(xa)], [gf], [_row(loss_target[0])], [_row(ones)],
                                    primal=True)
    loss_here = _sum_rows("loss_sum", loss_rows)[0, 0]

    bufs = {n: None for n in BIG}
    pending = []

    def advance(dep):
        tokens = []
        for g in pending:
            names, tag = g['names'], g['tag']
            step = g['steps'].pop(0)
            if step == 'sibling':
                p4 = [a.reshape((4, 2) + a.shape[1:]) for a in g['gw']]
                gots = [lax.empty((4,) + a.shape[1:], F32) for a in g['gw']]
                g['h'] = _split_start("rs_sibling_start_" + tag, p4, gots, 4 * len(p4), _sibling_plan, dep)
                tokens.append(g['h'][4])
            elif step == 'chips':
                p4, gots = _split_wait("rs_sibling_wait_" + tag, g['h'], 4 * len(names), _sibling_plan, dep)
                s4 = [_pair_sum("rs_pair_sum_" + n, p, q, core) for n, p, q in zip(names, p4, gots)]
                gots = [lax.empty((3,) + a.shape[1:], a.dtype) for a in s4]
                g['h'] = _split_start("rs_chips_start_" + tag, s4, gots, 3 * len(s4), _chips_plan, dep)
                tokens.append(g['h'][4])
            elif step == 'update':
                s4, gots = _split_wait("rs_chips_wait_" + tag, g['h'], 3 * len(names), _chips_plan, dep)
                for n, s4n, got in zip(names, s4, gots):
                    bufs[n] = _adamw_shard("adamw_" + n, g['layer'], shard_view(W[n], n), shard_view(M[n], n),
                                           shard_view(V[n], n), s4n, got, chip, bufs[n])
        pending[:] = [g for g in pending if g['steps']]
        return tokens

    def group(names, l, gw, kind):
        steps = ['sibling'] + (['rest'] if kind == "ffn" and l > 0 else []) + ['chips', 'rest', 'update']
        return dict(names=names, layer=l, gw=[gw[n] for n in names], steps=steps, tag=f"{kind}_{l}")

    def pack_small(parts):
        n = sum(int(p.shape[0]) for p in parts)
        rows = -(-n // (PACK_C * SMALL_ROW_ALIGN)) * SMALL_ROW_ALIGN
        return jnp.concatenate(parts + [jnp.zeros((rows * PACK_C - n,), F32)]).reshape(rows, PACK_C)

    def small_of(layers):
        return [g_small[k][n].reshape(-1) for k in layers for n in SMALL]

    g_small = [None] * DEPTH
    tokens = []
    upper = list(range(1, DEPTH))
    tot_upper = None
    for l in reversed(range(DEPTH)):
        early = l == 0 and upper
        if early:
            flat_upper = pack_small(small_of(upper))
            h_upper = _gather_start("gather_upper_start", [flat_upper], dxa)
            tokens.append(h_upper[4])
        dx1, gw_f, gs_f = _ffn_bwd(dxa, saved[l], full[l], small[l], tok_of(tokens))
        pending.append(group(FFN_W, l, gw_f, "ffn"))
        tokens = advance(dx1)
        if early:
            h_upper = _gather_pass_on("gather_upper", h_upper, dx1)
            tokens.append(h_upper[4])
        cts, gw_o, gs_o = _mixer_bwd_out(dx1, saved[l], full[l], small[l], tok_of(tokens))
        pending.append(group(OUT_W, l, gw_o, "out"))
        tokens = advance(cts[0])
        if early:
            (g_upper,) = _gather_finish("gather_upper", h_upper, cts[0])
            tot_upper = _sum_devices("small_sum_upper", g_upper).reshape(-1)
        dxa, gw_i, gs_i = _mixer_bwd_in(cts, dx1, saved[l], full[l], small[l], rope, tok_of(tokens))
        pending.append(group(IN_W, l, gw_i, "in"))
        tokens = advance(dxa)
        g_small[l] = {**gs_f, **gs_o, **gs_i}

    first = [0] if upper else list(range(DEPTH))
    flat = pack_small(small_of(first) + [dgf.reshape(-1), loss_here.reshape(1)])
    h_small = _gather_start("gather_small_start", [flat], dxa)
    advance(h_small[4])
    h_small = _gather_pass_on("gather_small", h_small, flat)
    tokens = advance(h_small[4])
    (gathered,) = _gather_finish("gather_small", h_small, flat)
    tot = _behind(_sum_devices("small_sum", gathered).reshape(-1), tok_of(tokens))

    def unpack(flat_sum, layers):
        out, off = {}, 0
        for k in layers:
            for n, shp in SMALL.items():
                size = int(np.prod(shp))
                out[k, n] = flat_sum[off:off + size].reshape(shp)
                off += size
        return out, off

    per_layer, off = unpack(tot, first)
    if upper:
        per_layer.update(unpack(tot_upper, upper)[0])
    grads = {n: jnp.stack([per_layer[k, n] for k in range(DEPTH)]) for n in SMALL}
    grads['g_final'] = tot[off:off + D]
    loss = tot[off + D]

    delta, new_m, new_v = {}, {}, {}
    for n in PARAMS:
        if n not in BIG:
            delta[n], new_m[n], new_v[n] = _adamw("adamw_" + n, W[n], grads[n], M[n], V[n])
    while pending:
        advance(delta['g_final'])
    for n in BIG:
        grads[n], delta[n], new_m[n], new_v[n] = [shard_view(b, n) for b in bufs[n]]
    return (loss, dxa[None], *[grads[n] for n in PARAMS], *[delta[n] for n in PARAMS],
            *[new_m[n] for n in PARAMS], *[new_v[n] for n in PARAMS])
```

```python
import jax
import jax.numpy as jnp
import numpy as np
from jax import lax
from jax.experimental import pallas as pl
from jax.experimental.pallas import tpu as pltpu

F32 = jnp.float32
BF16 = jnp.bfloat16

T = 2048
D = 2048
DEPTH = 4
N_DEV = 8
H_MLA, NOPE, ROPE, VDIM = 8, 128, 64, 128
QK = NOPE + ROPE
Q_LORA, KV_LORA = 512, 256
SSM_W, SSM_G, SSM_P, SSM_N = 512, 32, 16, 64
SSM_S = SSM_G * SSM_N
DIL_W, DIL_H, DIL_D = 512, 8, 64
BLK = 128
IN_SPLITS = (Q_LORA, KV_LORA, ROPE, SSM_W, DIL_W, DIL_W, DIL_W)
IN_W = sum(IN_SPLITS)
D_FF = 5632
EPS = 1e-6
ROPE_THETA = 10000.0
MLA_SCALE = QK ** -0.5
DIL_SCALE = DIL_D ** -0.5

ADAM_LR, ADAM_B1, ADAM_B2, ADAM_EPS, ADAM_WD, ADAM_STEP = 0.001, 0.9, 0.999, 1e-08, 0.01, 10

VMEM_LIMIT_V7X = 52 * 1024 * 1024
LANES = 128
PACK_C = 1024
ROW_BLOCK_BYTES = 2 * 1024 * 1024
MM_TM, MM_TN, MM_TK = 1408, 1024, 5632
MM_DEEP = IN_W
MM_TB = 512
EPILOGUE_SLABS = 4

NT = (((1,), (1,)), ((), ()))
TN = (((0,), (0,)), ((), ()))
HI = lax.Precision.HIGHEST
MESH = pl.DeviceIdType.MESH

PARAMS = ['g_mix', 'w_in', 'g_q', 'w_uq', 'g_kv', 'w_ukv', 'a_re', 'a_im', 'b_re', 'b_im', 'c_re', 'c_im',
          'd_skip', 'log_dt', 'w_glu', 'b_glu', 'g_out_mla', 'g_out_ssm', 'g_out_dil', 'w_o', 'g_ffn',
          'w_gate', 'w_up', 'w_down', 'g_final']
BIG = {'w_in': 't', 'w_uq': 't', 'w_ukv': 'c', 'w_glu': 'c', 'w_o': 'r', 'w_gate': 't', 'w_up': 't', 'w_down': 'r'}
MIXER_W = ['w_in', 'w_uq', 'w_ukv', 'w_glu', 'w_o']
FFN_W = ['w_gate', 'w_up', 'w_down']
OUT_W, IN_W = ['w_o', 'w_glu'], ['w_in', 'w_uq', 'w_ukv']
SMALL = {'g_mix': (D,), 'g_q': (Q_LORA,), 'g_kv': (KV_LORA,), 'a_re': (SSM_G, SSM_N), 'a_im': (SSM_G, SSM_N),
         'b_re': (SSM_G, SSM_N, SSM_P), 'b_im': (SSM_G, SSM_N, SSM_P), 'c_re': (SSM_G, SSM_P, SSM_N),
         'c_im': (SSM_G, SSM_P, SSM_N), 'd_skip': (SSM_G, SSM_P), 'log_dt': (SSM_G,), 'b_glu': (2 * SSM_W,),
         'g_out_mla': (H_MLA * VDIM,), 'g_out_ssm': (SSM_W,), 'g_out_dil': (DIL_W,), 'g_ffn': (D,)}
SMALL_ROW_ALIGN = 64


def _tile(dim, target, align=LANES):
    best = None
    for t in range(align, min(dim, target) + 1, align):
        if dim % t == 0:
            best = t
    return best if best is not None else dim


def _params(sem=None):
    return pltpu.CompilerParams(dimension_semantics=sem, vmem_limit_bytes=VMEM_LIMIT_V7X)


def _dot(a, b, dims=None, prec=None):
    if dims is None:
        return jnp.dot(a, b, preferred_element_type=F32, precision=prec)
    return lax.dot_general(a, b, dims, preferred_element_type=F32, precision=prec)


def _mm_spec(shape, blk, t_r, t_c, rc):
    if blk is None:
        return pl.BlockSpec((t_r, t_c), rc)
    _, R, C = shape
    if blk == 'r':
        per = R // t_r
        return pl.BlockSpec((1, t_r, t_c), lambda i, j, k: (rc(i, j, k)[0] // per, rc(i, j, k)[0] % per, rc(i, j, k)[1]))
    per = C // t_c
    return pl.BlockSpec((1, t_r, t_c), lambda i, j, k: (rc(i, j, k)[1] // per, rc(i, j, k)[0], rc(i, j, k)[1] % per))


def _logical(shape, blk):
    if blk is None:
        return tuple(shape)
    G, R, C = shape
    return (G * R, C) if blk == 'r' else (R, G * C)


def _mm(name, a, b, mode, ab=None, bb=None, ob=None, res=None, prec=None, epilogue=None):
    la, lb = _logical(a.shape, ab), _logical(b.shape, bb)
    am, ak = (0, 1) if mode != 'tn' else (1, 0)
    bk, bn = (0, 1) if mode != 'nt' else (1, 0)
    M, K, N = la[am], la[ak], lb[bn]
    assert lb[bk] == K, (name, a.shape, b.shape, mode)
    if ob is None:
        out_shape = (M, N)
    elif ob == 'r':
        G = N_DEV
        out_shape = (G, M // G, N)
    else:
        G = N_DEV
        out_shape = (G, M, N // G)
    em = min(a.shape[-2:][am], out_shape[-2])
    en = min(b.shape[-2:][bn], out_shape[-1])
    ek = min(a.shape[-2:][ak], b.shape[-2:][bk])
    dims = {'nn': None, 'nt': NT, 'tn': TN}[mode]
    a_kb = mode != 'tn' and ab == 'c'
    b_kb = (mode == 'nn' and bb == 'r') or (mode == 'nt' and bb == 'c')
    blocks = K // ek if (a_kb or b_kb) else 1
    assert blocks == 1 or ((a_kb or ab is None) and (b_kb or bb is None)), (name, ab, bb, mode)
    tk = ek if blocks > 1 else _tile(ek, MM_TK)
    nk = 1 if blocks > 1 else K // tk
    small = blocks > 1 or nk > 1 or K > MM_DEEP
    tn = _tile(en, MM_TB if small else MM_TN)
    tm = _tile(em, MM_TB if small else (2 * MM_TM if K <= MM_TB else MM_TM))
    if tn > MM_TN:
        tm = _tile(em, MM_TB)
    if tm > 2 * MM_TM:
        tn = _tile(en, MM_TB)

    def val(ref):
        return ref[...] if len(ref.shape) == 2 else ref[0]

    def put(o_ref, r):
        if len(o_ref.shape) == 2:
            o_ref[...] = r
        else:
            o_ref[0] = r

    def k_block(ref, d, blocked, lanes):
        if blocked:
            return ref[d]
        return ref[:, d * ek:(d + 1) * ek] if lanes else ref[d * ek:(d + 1) * ek, :]

    if epilogue is not None:
        epi_fn, extras, out_dtypes = epilogue
        assert nk == 1 and blocks == 1 and ob is None and res is None, name

    def body(*refs):
        if epilogue is not None:
            a_ref, b_ref = refs[:2]
            extra_refs, out_refs = refs[2:2 + len(extras)], refs[2 + len(extras):]
            slab = tm // EPILOGUE_SLABS if (mode != 'tn' and tm % (8 * EPILOGUE_SLABS) == 0) else tm
            for r0 in range(0, tm, slab):
                rows = slice(r0, r0 + slab)
                a_rows = a_ref[rows, :] if slab != tm else val(a_ref)
                outs = epi_fn(_dot(a_rows, val(b_ref), dims, prec), *[e[rows, :] for e in extra_refs])
                for o_ref, o in zip(out_refs, outs):
                    o_ref[rows, :] = o.astype(o_ref.dtype)
            return
        if res is None:
            a_ref, b_ref, o_ref = refs[:3]
            r_ref = None
        else:
            a_ref, b_ref, r_ref, o_ref = refs[:4]
        if blocks > 1:
            part = None
            for d in range(blocks):
                p = _dot(k_block(a_ref, d, a_kb, True), k_block(b_ref, d, b_kb, mode == 'nt'), dims, prec)
                part = p if part is None else part + p
        else:
            part = _dot(val(a_ref), val(b_ref), dims, prec)
        if nk == 1:
            put(o_ref, part if r_ref is None else part + val(r_ref))
            return
        acc_ref = refs[-1]
        k = pl.program_id(2)

        @pl.when(k == 0)
        def _():
            acc_ref[...] = part

        @pl.when((k > 0) & (k < nk - 1))
        def _():
            acc_ref[...] += part

        @pl.when(k == nk - 1)
        def _():
            r = acc_ref[...] + part
            put(o_ref, r if r_ref is None else r + val(r_ref))

    if blocks > 1:
        G = blocks
        a_spec = (pl.BlockSpec((G, tm, ek), lambda i, j, k: (0, i, 0)) if a_kb
                  else pl.BlockSpec((tm, K), lambda i, j, k: (i, 0)))
        if b_kb:
            b_spec = (pl.BlockSpec((G, ek, tn), lambda i, j, k: (0, 0, j)) if mode == 'nn'
                      else pl.BlockSpec((G, tn, ek), lambda i, j, k: (0, j, 0)))
        else:
            b_spec = (pl.BlockSpec((K, tn), lambda i, j, k: (0, j)) if mode == 'nn'
                      else pl.BlockSpec((tn, K), lambda i, j, k: (j, 0)))
    else:
        if mode == 'tn':
            a_spec = _mm_spec(a.shape, ab, tk, tm, lambda i, j, k: (k, i))
        else:
            a_spec = _mm_spec(a.shape, ab, tm, tk, lambda i, j, k: (i, k))
        if mode == 'nt':
            b_spec = _mm_spec(b.shape, bb, tn, tk, lambda i, j, k: (j, k))
        else:
            b_spec = _mm_spec(b.shape, bb, tk, tn, lambda i, j, k: (k, j))
    o_spec = _mm_spec(out_shape, ob, tm, tn, lambda i, j, k: (i, j))
    if epilogue is not None:
        return pl.pallas_call(
            body, name=name, out_shape=[jax.ShapeDtypeStruct(out_shape, dt) for dt in out_dtypes],
            grid=(M // tm, N // tn, nk), in_specs=[a_spec, b_spec] + [o_spec] * len(extras),
            out_specs=[o_spec] * len(out_dtypes),
            compiler_params=_params(("parallel", "parallel", "arbitrary")),
        )(a, b, *extras)
    in_specs = [a_spec, b_spec] + ([o_spec] if res is not None else [])
    args = (a, b) + ((res,) if res is not None else ())
    return pl.pallas_call(
        body, name=name, out_shape=jax.ShapeDtypeStruct(out_shape, F32),
        grid=(M // tm, N // tn, nk), in_specs=in_specs, out_specs=o_spec,
        scratch_shapes=[pltpu.VMEM((tm, tn), F32)] if nk > 1 else [],
        compiler_params=_params(("parallel", "parallel", "arbitrary")),
    )(*args)


def _row(a, tr=None, axis=-2):
    axis = axis % a.ndim
    n = a.shape[axis]
    if tr is None:
        row_bytes = a.size // n * 4
        tr = _tile(n, max(8, min(256, ROW_BLOCK_BYTES // row_bytes)), 8)
    return (a, axis, tr)


def _row_spec(shape, axis, tr):
    nd = len(shape)
    blk = tuple(tr if d == axis else s for d, s in enumerate(shape))
    return pl.BlockSpec(blk, lambda i: tuple(i if d == axis else 0 for d in range(nd)))


def _full_spec(shape):
    nd = len(shape)
    return pl.BlockSpec(tuple(shape), lambda i: (0,) * nd)


def _steps(entries):
    ns = {a.shape[ax] // tr for a, ax, tr in entries}
    assert len(ns) == 1, [(a.shape, ax, tr) for a, ax, tr in entries]
    return ns.pop()


def _as_tuple(r):
    return tuple(r) if isinstance(r, (tuple, list)) else (r,)


def _rows_fwd(name, fn, rows, bcast, outs):
    steps = _steps(rows)
    nr, nb = len(rows), len(bcast)

    def body(*refs):
        vals = [r[...] for r in refs[:nr + nb]]
        res = _as_tuple(fn(*vals))
        for o_ref, r in zip(refs[nr + nb:], res):
            o_ref[...] = r.astype(o_ref.dtype)

    in_specs = [_row_spec(a.shape, ax, tr) for a, ax, tr in rows] + [_full_spec(b.shape) for b in bcast]
    out_specs = [_row_spec(s, ax % len(s), tr) for s, _, ax, tr in outs]
    res = pl.pallas_call(
        body, name=name, out_shape=[jax.ShapeDtypeStruct(s, dt) for s, dt, _, _ in outs],
        grid=(steps,), in_specs=in_specs, out_specs=out_specs,
        compiler_params=_params(("parallel",)),
    )(*[a for a, _, _ in rows], *bcast)
    return res


def _rows_vjp(name, fn, drows, dbc, arows, cts, primal=False, grad_dtypes=None):
    entries = list(drows) + list(arows) + list(cts)
    steps = _steps(entries)
    ndr, ndb, nar, nct = len(drows), len(dbc), len(arows), len(cts)
    gdt = list(grad_dtypes) if grad_dtypes is not None else [F32] * ndr

    def body(*refs):
        p = 0
        dr = [r[...] for r in refs[p:p + ndr]]; p += ndr
        db = [r[...] for r in refs[p:p + ndb]]; p += ndb
        ar = [r[...] for r in refs[p:p + nar]]; p += nar
        ct = [r[...] for r in refs[p:p + nct]]; p += nct
        g_rows = refs[p:p + ndr]; p += ndr
        g_bc = refs[p:p + ndb]; p += ndb
        prim_refs = refs[p:]

        def f(*d):
            return _as_tuple(fn(*d, *ar))

        outs, pullback = jax.vjp(f, *dr, *db)
        grads = pullback(tuple(c.astype(o.dtype) for c, o in zip(ct, outs)))
        for k in range(ndr):
            g_rows[k][...] = grads[k].astype(g_rows[k].dtype)
        if ndb:
            @pl.when(pl.program_id(0) == 0)
            def _():
                for r in g_bc:
                    r[...] = jnp.zeros_like(r)
            for k in range(ndb):
                g_bc[k][...] += grads[ndr + k]
        for r, o in zip(prim_refs, outs):
            r[...] = o.astype(r.dtype)

    in_specs = ([_row_spec(a.shape, ax, tr) for a, ax, tr in drows] + [_full_spec(b.shape) for b in dbc]
                + [_row_spec(a.shape, ax, tr) for a, ax, tr in arows]
                + [_row_spec(a.shape, ax, tr) for a, ax, tr in cts])
    out_shape = ([jax.ShapeDtypeStruct(a.shape, dt) for (a, _, _), dt in zip(drows, gdt)]
                 + [jax.ShapeDtypeStruct(b.shape, F32) for b in dbc])
    out_specs = ([_row_spec(a.shape, ax, tr) for a, ax, tr in drows] + [_full_spec(b.shape) for b in dbc])
    if primal:
        out_shape += [jax.ShapeDtypeStruct(a.shape, F32) for a, _, _ in cts]
        out_specs += [_row_spec(a.shape, ax, tr) for a, ax, tr in cts]
    return pl.pallas_call(
        body, name=name, out_shape=out_shape, grid=(steps,), in_specs=in_specs, out_specs=out_specs,
        compiler_params=_params(("arbitrary",)),
    )(*[a for a, _, _ in drows], *dbc, *[a for a, _, _ in arows], *[a for a, _, _ in cts])


def _rms_fn(x, g):
    return x * lax.rsqrt(jnp.mean(x * x, axis=-1, keepdims=True) + EPS) * g


def _rms_res_fn(x, g):
    return _rms_fn(x, g), x


def _s5_act_fn(ymm, u, d):
    return jax.nn.gelu(ymm + d * u)


def _glu_fn(z1, z2, b1, b2):
    return (z1 + b1) * jax.nn.sigmoid(z2 + b2)


def _outnorm_fn(ym, ys, yd, gm, gs, gd):
    return jnp.concatenate([_rms_fn(ym, gm), _rms_fn(ys, gs), _rms_fn(yd, gd)], axis=-1)


def _swiglu_fn(a, b):
    return jax.nn.silu(a) * b


def _loss_fn(x, g, tgt):
    err = _rms_fn(x, g) - tgt
    return 0.5 * jnp.mean(err * err, axis=-1, keepdims=True)


def _dil_mix_fn(o0, o1, o2, l0, l1, l2):
    m = jnp.maximum(jnp.maximum(l0, l1), l2)
    e0, e1, e2 = jnp.exp(l0 - m), jnp.exp(l1 - m), jnp.exp(l2 - m)
    s = e0 + e1 + e2
    return (e0 / s) * o0 + (e1 / s) * o1 + (e2 / s) * o2


def _s5_disc_fn(a_re, a_im, ldt, b_r, b_i):
    lr = jnp.minimum(a_re.reshape(1, SSM_N), -1e-4)
    li = a_im.reshape(1, SSM_N)
    dt = jnp.exp(ldt.reshape(1, 1))
    e = jnp.exp(lr * dt)
    ar = e * jnp.cos(li * dt)
    ai = e * jnp.sin(li * dt)
    nr, ni = ar - 1.0, ai
    den = lr * lr + li * li
    cr = (nr * lr + ni * li) / den
    ci = (ni * lr - nr * li) / den
    return ar.reshape(1, 1, SSM_N), ai.reshape(1, 1, SSM_N), cr * b_r - ci * b_i, cr * b_i + ci * b_r


def _adamw_fn(w, g, m, v):
    m = ADAM_B1 * m + (1.0 - ADAM_B1) * g
    v = ADAM_B2 * v + (1.0 - ADAM_B2) * jnp.square(g)
    m_hat = m / (1.0 - ADAM_B1 ** ADAM_STEP)
    v_hat = v / (1.0 - ADAM_B2 ** ADAM_STEP)
    delta = -ADAM_LR * (m_hat / (jnp.sqrt(v_hat) + ADAM_EPS) + ADAM_WD * w)
    return delta, m, v


def _rms_fwd(name, x, g):
    (h,) = _rows_fwd(name, _rms_fn, [_row(x)], [g], [(x.shape, BF16, -2, _row(x)[2])])
    return h


def _rms_bwd(name, x, g, dh, dres=None):
    if dres is None:
        dx, dg = _rows_vjp(name, _rms_fn, [_row(x)], [g], [], [_row(dh)])
    else:
        dx, dg = _rows_vjp(name, _rms_res_fn, [_row(x)], [g], [], [_row(dh), _row(dres)])
    return dx, dg


MLA_TQ = 256
MLA_EXT = 256


def _mla_fwd(q, k, v):
    tq = MLA_TQ

    def body(q_ref, k_ref, v_ref, o_ref, lse_ref):
        i = pl.program_id(1)
        q = q_ref[0]

        def rows_below(ext):
            s = _dot(q, k_ref[0, :ext, :], NT) * MLA_SCALE
            row = i * tq + lax.broadcasted_iota(jnp.int32, (tq, ext), 0)
            col = lax.broadcasted_iota(jnp.int32, (tq, ext), 1)
            s = jnp.where(row >= col, s, -jnp.inf)
            m = jnp.max(s, axis=-1, keepdims=True)
            p = jnp.exp(s - m)
            l = jnp.sum(p, axis=-1, keepdims=True)
            o_ref[...] = _dot((p / l).astype(BF16), v_ref[0, :ext, :])
            lse_ref[0] = m + jnp.log(l)

        for g in range(T // MLA_EXT):
            pl.when(i // (MLA_EXT // tq) == g)(lambda g=g: rows_below((g + 1) * MLA_EXT))

    return pl.pallas_call(
        body, name="mla_fwd",
        out_shape=[jax.ShapeDtypeStruct((T, H_MLA * VDIM), F32), jax.ShapeDtypeStruct((H_MLA, T, 1), F32)],
        grid=(H_MLA, T // tq),
        in_specs=[pl.BlockSpec((1, tq, QK), lambda h, i: (h, i, 0)),
                  pl.BlockSpec((1, T, QK), lambda h, i: (h, 0, 0)),
                  pl.BlockSpec((1, T, VDIM), lambda h, i: (h, 0, 0))],
        out_specs=[pl.BlockSpec((tq, VDIM), lambda h, i: (i, h)),
                   pl.BlockSpec((1, tq, 1), lambda h, i: (h, i, 0))],
        compiler_params=_params(("parallel", "parallel")),
    )(q, k, v)


def _mla_bwd(q, k, v, o, do, lse):
    tq = MLA_TQ

    def body(q_ref, k_ref, v_ref, o_ref, do_ref, lse_ref, dq_ref, dk_ref, dv_ref):
        i = pl.program_id(1)

        @pl.when(i == 0)
        def _():
            dk_ref[...] = jnp.zeros_like(dk_ref)
            dv_ref[...] = jnp.zeros_like(dv_ref)

        q, lse = q_ref[0], lse_ref[0]
        delta = jnp.sum(do_ref[...] * o_ref[...], axis=-1, keepdims=True)
        do = do_ref[...].astype(BF16)

        def rows_below(ext):
            k, v = k_ref[0, :ext, :], v_ref[0, :ext, :]
            s = _dot(q, k, NT) * MLA_SCALE
            row = i * tq + lax.broadcasted_iota(jnp.int32, (tq, ext), 0)
            col = lax.broadcasted_iota(jnp.int32, (tq, ext), 1)
            p = jnp.where(row >= col, jnp.exp(s - lse), 0.0)
            ds = (p * (_dot(do, v, NT) - delta) * MLA_SCALE).astype(BF16)
            dq_ref[0] = _dot(ds, k)
            dk_ref[0, :ext, :] += _dot(ds, q, TN)
            dv_ref[0, :ext, :] += _dot(p.astype(BF16), do, TN)

        for g in range(T // MLA_EXT):
            pl.when(i // (MLA_EXT // tq) == g)(lambda g=g: rows_below((g + 1) * MLA_EXT))

    return pl.pallas_call(
        body, name="mla_bwd",
        out_shape=[jax.ShapeDtypeStruct((H_MLA, T, QK), F32), jax.ShapeDtypeStruct((H_MLA, T, QK), F32),
                   jax.ShapeDtypeStruct((H_MLA, T, VDIM), F32)],
        grid=(H_MLA, T // tq),
        in_specs=[pl.BlockSpec((1, tq, QK), lambda h, i: (h, i, 0)),
                  pl.BlockSpec((1, T, QK), lambda h, i: (h, 0, 0)),
                  pl.BlockSpec((1, T, VDIM), lambda h, i: (h, 0, 0)),
                  pl.BlockSpec((tq, VDIM), lambda h, i: (i, h)),
                  pl.BlockSpec((tq, VDIM), lambda h, i: (i, h)),
                  pl.BlockSpec((1, tq, 1), lambda h, i: (h, i, 0))],
        out_specs=[pl.BlockSpec((1, tq, QK), lambda h, i: (h, i, 0)),
                   pl.BlockSpec((1, T, QK), lambda h, i: (h, 0, 0)),
                   pl.BlockSpec((1, T, VDIM), lambda h, i: (h, 0, 0))],
        compiler_params=_params(("parallel", "arbitrary")),
    )(q, k, v, o, do, lse)


NBLK = T // BLK


BAND_GL = 256
BAND_GH = BAND_GL // DIL_D
BAND_ROWS = BAND_GH * BLK
BAND_GROUPS = DIL_W // BAND_GL


def _band_masks():
    r = lax.broadcasted_iota(jnp.int32, (BAND_ROWS, BLK), 0) & (BLK - 1)
    j = lax.broadcasted_iota(jnp.int32, (BAND_ROWS, BLK), 1)
    return j <= r, j >= r


def _head_lanes():
    lane_head = lax.broadcasted_iota(jnp.int32, (1, BAND_GL), 1) // DIL_D
    return [lane_head == h for h in range(BAND_GH)]


def _stack_heads(x, lanes):
    return jnp.concatenate([jnp.where(m, x, jnp.zeros_like(x)) for m in lanes], axis=0)


def _merge_heads(xs, lanes):
    out = None
    for h, m in enumerate(lanes):
        part = jnp.where(m, xs[h * BLK:(h + 1) * BLK], 0.0)
        out = part if out is None else out + part
    return out


def _per_head(x, lanes):
    return jnp.concatenate([jnp.sum(jnp.where(m, x, 0.0), axis=-1, keepdims=True) for m in lanes], axis=0)


def _lane_group(ref, g):
    return ref[0, :, g * BAND_GL:(g + 1) * BAND_GL]


def _seq_start(p, i):
    per_seq = lax.shift_right_logical(jnp.int32(NBLK), 2 * p)
    return lax.rem(i, per_seq) == 0


def _band_fwd(qkv):
    def body(q_ref, kp_ref, kc_ref, vp_ref, vc_ref, o_ref, lse_ref):
        p, i = pl.program_id(0), pl.program_id(1)
        has_prev = jnp.logical_not(_seq_start(p, i))
        m_cur, m_prev = _band_masks()
        m_prev = m_prev & has_prev
        lanes = _head_lanes()
        for g in range(BAND_GROUPS):
            qs = _stack_heads(_lane_group(q_ref, g), lanes)
            s_c = jnp.where(m_cur, _dot(qs, _lane_group(kc_ref, g), NT) * DIL_SCALE, -jnp.inf)
            s_p = jnp.where(m_prev, _dot(qs, _lane_group(kp_ref, g), NT) * DIL_SCALE, -jnp.inf)
            m = jnp.maximum(jnp.max(s_c, axis=-1, keepdims=True), jnp.max(s_p, axis=-1, keepdims=True))
            e_c, e_p = jnp.exp(s_c - m), jnp.exp(s_p - m)
            l = jnp.sum(e_c, axis=-1, keepdims=True) + jnp.sum(e_p, axis=-1, keepdims=True)
            os = (_dot((e_p / l).astype(BF16), _lane_group(vp_ref, g))
                  + _dot((e_c / l).astype(BF16), _lane_group(vc_ref, g)))
            cols = slice(g * BAND_GL, (g + 1) * BAND_GL)
            o_ref[0, :, cols] = _merge_heads(os, lanes)
            lse_ref[0, :, cols] = _merge_heads(m + jnp.log(l), lanes)

    blk = (1, BLK, DIL_W)
    cur = lambda part: (lambda p, i: (p, i, part))
    prev = lambda part: (lambda p, i: (p, jnp.maximum(i - 1, 0), part))
    return pl.pallas_call(
        body, name="band_fwd",
        out_shape=[jax.ShapeDtypeStruct((3, T, DIL_W), F32)] * 2,
        grid=(3, NBLK),
        in_specs=[pl.BlockSpec(blk, cur(0)), pl.BlockSpec(blk, prev(1)), pl.BlockSpec(blk, cur(1)),
                  pl.BlockSpec(blk, prev(2)), pl.BlockSpec(blk, cur(2))],
        out_specs=[pl.BlockSpec(blk, cur(0)), pl.BlockSpec(blk, cur(0))],
        compiler_params=_params(("parallel", "parallel")),
    )(qkv, qkv, qkv, qkv, qkv)


def _band_bwd(qkv, o, lse, do, dlse):
    def body(qc_ref, qn_ref, kp_ref, kc_ref, vp_ref, vc_ref, oc_ref, on_ref, lc_ref, ln_ref,
             doc_ref, don_ref, dlc_ref, dln_ref, dqkv_ref):
        p, i = pl.program_id(0), pl.program_id(1)
        has_prev = jnp.logical_not(_seq_start(p, i))
        has_next = jnp.logical_not(_seq_start(p, i + 1)) & (i + 1 < NBLK)
        m_cur, m_prev = _band_masks()
        lanes = _head_lanes()

        def probs(qs, k, lse, mask):
            return jnp.where(mask, jnp.exp(_dot(qs, k, NT) * DIL_SCALE - lse), 0.0)

        def dscore(pr, dos, v, shift):
            return (pr * (_dot(dos, v, NT) + shift) * DIL_SCALE).astype(BF16)

        for g in range(BAND_GROUPS):
            grp = lambda ref: _lane_group(ref, g)
            kp, kc, vp, vc = grp(kp_ref), grp(kc_ref), grp(vp_ref), grp(vc_ref)
            qc, qn = _stack_heads(grp(qc_ref), lanes), _stack_heads(grp(qn_ref), lanes)
            doc, don = grp(doc_ref), grp(don_ref)
            lse_c = _per_head(grp(lc_ref), lanes) * (1.0 / DIL_D)
            lse_n = _per_head(grp(ln_ref), lanes) * (1.0 / DIL_D)
            sh_c = _per_head(grp(dlc_ref) - doc * grp(oc_ref), lanes)
            sh_n = _per_head(grp(dln_ref) - don * grp(on_ref), lanes)
            doc, don = _stack_heads(doc.astype(BF16), lanes), _stack_heads(don.astype(BF16), lanes)
            p_cc = probs(qc, kc, lse_c, m_cur)
            p_cp = probs(qc, kp, lse_c, m_prev & has_prev)
            p_nc = probs(qn, kc, lse_n, m_prev & has_next)
            ds_cc = dscore(p_cc, doc, vc, sh_c)
            ds_cp = dscore(p_cp, doc, vp, sh_c)
            ds_nc = dscore(p_nc, don, vc, sh_n)
            cols = lambda part: slice(part * DIL_W + g * BAND_GL, part * DIL_W + (g + 1) * BAND_GL)
            dqkv_ref[0, :, cols(0)] = _merge_heads(_dot(ds_cc, kc) + _dot(ds_cp, kp), lanes)
            dqkv_ref[0, :, cols(1)] = _dot(ds_cc, qc, TN) + _dot(ds_nc, qn, TN)
            dqkv_ref[0, :, cols(2)] = _dot(p_cc.astype(BF16), doc, TN) + _dot(p_nc.astype(BF16), don, TN)

    blk = (1, BLK, DIL_W)
    cur = lambda part: pl.BlockSpec(blk, lambda p, i: (p, i, part))
    prev = lambda part: pl.BlockSpec(blk, lambda p, i: (p, jnp.maximum(i - 1, 0), part))
    nxt = lambda part: pl.BlockSpec(blk, lambda p, i: (p, jnp.minimum(i + 1, NBLK - 1), part))
    w, wn = cur(0), nxt(0)
    return pl.pallas_call(
        body, name="band_bwd",
        out_shape=jax.ShapeDtypeStruct((3, T, 3 * DIL_W), F32),
        grid=(3, NBLK),
        in_specs=[cur(0), nxt(0), prev(1), cur(1), prev(2), cur(2), w, wn, w, wn, w, wn, w, wn],
        out_specs=pl.BlockSpec((1, BLK, 3 * DIL_W), lambda p, i: (p, i, 0)),
        compiler_params=_params(("parallel", "parallel")),
    )(qkv, qkv, qkv, qkv, qkv, qkv, o, o, lse, lse, do, do, dlse, dlse)


SCAN_TC = 256


def _scan_fwd(bu, ar, ai):
    tc, S = SCAN_TC, SSM_S

    def body(bu_ref, ar_ref, ai_ref, h_ref, h16_ref, cr_ref, ci_ref):
        @pl.when(pl.program_id(0) == 0)
        def _():
            cr_ref[...] = jnp.zeros_like(cr_ref)
            ci_ref[...] = jnp.zeros_like(ci_ref)

        a_r, a_i = ar_ref[...], ai_ref[...]

        def step(j, carry):
            hr, hi = carry
            for r in range(8):
                t = pl.multiple_of(j * 8, 8) + r
                br = bu_ref[pl.ds(t, 1), pl.ds(0, S)]
                bi = bu_ref[pl.ds(t, 1), pl.ds(S, S)]
                hr, hi = a_r * hr - a_i * hi + br, a_r * hi + a_i * hr + bi
                h_ref[pl.ds(t, 1), pl.ds(0, S)] = hr
                h_ref[pl.ds(t, 1), pl.ds(S, S)] = hi
            return hr, hi

        hr, hi = lax.fori_loop(0, tc // 8, step, (cr_ref[...], ci_ref[...]))
        cr_ref[...] = hr
        ci_ref[...] = hi
        h16_ref[...] = h_ref[...].astype(BF16)

    return pl.pallas_call(
        body, name="s5_scan_fwd",
        out_shape=[jax.ShapeDtypeStruct((T, 2 * S), F32), jax.ShapeDtypeStruct((T, 2 * S), BF16)],
        grid=(T // tc,),
        in_specs=[pl.BlockSpec((tc, 2 * S), lambda i: (i, 0)), _full_spec((1, S)), _full_spec((1, S))],
        out_specs=[pl.BlockSpec((tc, 2 * S), lambda i: (i, 0)), pl.BlockSpec((tc, 2 * S), lambda i: (i, 0))],
        scratch_shapes=[pltpu.VMEM((1, S), F32), pltpu.VMEM((1, S), F32)],
        compiler_params=_params(("arbitrary",)),
    )(bu, ar, ai)


def _scan_bwd(dh, h, ar, ai):
    tc, S = SCAN_TC, SSM_S
    nc = T // tc

    def body(dh_ref, h_ref, hp_ref, ar_ref, ai_ref, g16_ref, dar_ref, dai_ref, cr_ref, ci_ref, g_ref):
        i = pl.program_id(0)

        @pl.when(i == 0)
        def _():
            cr_ref[...] = jnp.zeros_like(cr_ref)
            ci_ref[...] = jnp.zeros_like(ci_ref)
            dar_ref[...] = jnp.zeros_like(dar_ref)
            dai_ref[...] = jnp.zeros_like(dai_ref)

        a_r, a_i = ar_ref[...], ai_ref[...]
        first_chunk = (i == nc - 1)
        edge = jnp.where(first_chunk, 0.0, 1.0)
        hpr = hp_ref[pl.ds(7, 1), pl.ds(0, S)] * edge
        hpi = hp_ref[pl.ds(7, 1), pl.ds(S, S)] * edge

        def step(jj, carry):
            gr, gi, dar, dai = carry
            j = tc // 8 - 1 - jj
            for r in range(7, -1, -1):
                t = pl.multiple_of(j * 8, 8) + r
                tp = jnp.maximum(t - 1, 0)
                inside = t > 0
                pr = jnp.where(inside, h_ref[pl.ds(tp, 1), pl.ds(0, S)], hpr)
                pi = jnp.where(inside, h_ref[pl.ds(tp, 1), pl.ds(S, S)], hpi)
                gr, gi = (dh_ref[pl.ds(t, 1), pl.ds(0, S)] + a_r * gr + a_i * gi,
                          dh_ref[pl.ds(t, 1), pl.ds(S, S)] + a_r * gi - a_i * gr)
                g_ref[pl.ds(t, 1), pl.ds(0, S)] = gr
                g_ref[pl.ds(t, 1), pl.ds(S, S)] = gi
                dar = dar + gr * pr + gi * pi
                dai = dai + gi * pr - gr * pi
            return gr, gi, dar, dai

        zero = jnp.zeros((1, S), F32)
        gr, gi, dar, dai = lax.fori_loop(0, tc // 8, step, (cr_ref[...], ci_ref[...], zero, zero))
        cr_ref[...] = gr
        ci_ref[...] = gi
        dar_ref[...] += dar
        dai_ref[...] += dai
        g16_ref[...] = g_ref[...].astype(BF16)

    rev = lambda i: (nc - 1 - i, 0)
    before = lambda i: (jnp.maximum((nc - 1 - i) * (tc // 8) - 1, 0), 0)
    return pl.pallas_call(
        body, name="s5_scan_bwd",
        out_shape=[jax.ShapeDtypeStruct((T, 2 * S), BF16), jax.ShapeDtypeStruct((1, S), F32),
                   jax.ShapeDtypeStruct((1, S), F32)],
        grid=(nc,),
        in_specs=[pl.BlockSpec((tc, 2 * S), rev), pl.BlockSpec((tc, 2 * S), rev), pl.BlockSpec((8, 2 * S), before),
                  _full_spec((1, S)), _full_spec((1, S))],
        out_specs=[pl.BlockSpec((tc, 2 * S), rev), _full_spec((1, S)), _full_spec((1, S))],
        scratch_shapes=[pltpu.VMEM((1, S), F32), pltpu.VMEM((1, S), F32), pltpu.VMEM((tc, 2 * S), F32)],
        compiler_params=_params(("arbitrary",)),
    )(dh, h, h, ar, ai)


def _sum_rows(name, x):
    def body(x_ref, o_ref):
        o_ref[...] = jnp.sum(x_ref[...], axis=0, keepdims=True)

    return pl.pallas_call(body, name=name, out_shape=jax.ShapeDtypeStruct((1, 1), F32),
                          in_specs=[_full_spec(x.shape)], out_specs=_full_spec((1, 1)), grid=(1,))(x)


ANY = pl.BlockSpec(memory_space=pl.ANY)


def _place():
    return lax.axis_index("x"), lax.axis_index("y"), lax.axis_index("c")


HBM = pl.BlockSpec(memory_space=pltpu.HBM)
SEM = pl.BlockSpec(memory_space=pltpu.SEMAPHORE)
DATAFLOW = pltpu.SideEffectType.DATAFLOW_SIDE_EFFECTING


def _hbm(a):
    return pltpu.with_memory_space_constraint(a, pltpu.HBM)


def _split_start(name, srcs, lands, ncopies, plan, after):
    ns, nl = len(srcs), len(lands)

    def body(*refs):
        send_sems, recv_sems = refs[ns + nl + 1], refs[ns + nl + 2]
        token = refs[-1]
        for k, (src, dst, peer, _) in enumerate(plan(refs[:ns], refs[ns:ns + nl])):
            pltpu.make_async_remote_copy(src_ref=src, dst_ref=dst, send_sem=send_sems.at[k], recv_sem=recv_sems.at[k],
                                         device_id=peer, device_id_type=MESH).start()
        token[...] = jnp.zeros_like(token)

    out = pl.pallas_call(
        body, name=name,
        out_shape=(pltpu.SemaphoreType.DMA((ncopies,)), pltpu.SemaphoreType.DMA((ncopies,)),
                   *[pltpu.HBM(a.shape, a.dtype) for a in srcs], *[pltpu.HBM(a.shape, a.dtype) for a in lands],
                   jax.ShapeDtypeStruct((8, LANES), F32)),
        in_specs=[HBM] * (ns + nl) + [ANY],
        out_specs=(SEM, SEM, *[HBM] * (ns + nl), pl.BlockSpec(memory_space=pltpu.VMEM)),
        input_output_aliases={i: 2 + i for i in range(ns + nl)},
        compiler_params=pltpu.CompilerParams(has_side_effects=DATAFLOW),
    )(*[_hbm(a) for a in srcs], *[_hbm(a) for a in lands], after)
    return out[0], out[1], list(out[2:2 + ns]), list(out[2 + ns:2 + ns + nl]), out[-1]


def _split_wait(name, handle, ncopies, plan, after):
    send_sems, recv_sems, srcs, lands, _ = handle
    ns, nl = len(srcs), len(lands)

    def body(*refs):
        s_sems, r_sems = refs[ns + nl], refs[ns + nl + 1]
        for k, (src, dst, peer, mine) in enumerate(plan(refs[:ns], refs[ns:ns + nl])):
            pltpu.make_async_remote_copy(src_ref=src, dst_ref=dst, send_sem=s_sems.at[k], recv_sem=r_sems.at[k],
                                         device_id=peer, device_id_type=MESH).wait_send()
            pltpu.make_async_remote_copy(src_ref=src, dst_ref=mine, send_sem=s_sems.at[k], recv_sem=r_sems.at[k],
                                         device_id=peer, device_id_type=MESH).wait_recv()

    out = pl.pallas_call(
        body, name=name,
        out_shape=(*[pltpu.HBM(a.shape, a.dtype) for a in srcs], *[pltpu.HBM(a.shape, a.dtype) for a in lands]),
        in_specs=[HBM] * (ns + nl) + [SEM, SEM, ANY],
        out_specs=tuple([HBM] * (ns + nl)),
        input_output_aliases={i: i for i in range(ns + nl)},
        compiler_params=pltpu.CompilerParams(has_side_effects=DATAFLOW),
    )(*srcs, *lands, send_sems, recv_sems, after)
    return list(out[:ns]), list(out[ns:])


def _slot(px, py, pc):
    return 4 * px + 2 * py + pc


def _gather_plan(xs, lands):
    x, y, c = _place()
    peers = [(x, y, 1 - c), (1 - x, y, c), (x, 1 - y, c), (1 - x, 1 - y, c)]
    return [(xs[t], lands[t].at[_slot(x, y, c)], peer, lands[t].at[_slot(*peer)])
            for t in range(len(xs)) for peer in peers]


def _gather_start(name, shards, after):
    lands = [lax.empty((N_DEV,) + s.shape, s.dtype) for s in shards]
    return _split_start(name, shards, lands, 4 * len(shards), _gather_plan, after)


def _pass_on_plan(_, lands):
    x, y, c = _place()
    blocks = [((1 - x, y, c), (1 - x, y, 1 - c)), ((x, 1 - y, c), (x, 1 - y, 1 - c)),
              ((1 - x, 1 - y, c), (1 - x, 1 - y, 1 - c)), ((x, y, 1 - c), (x, y, c))]
    return [(lands[t].at[_slot(*out)], lands[t].at[_slot(*out)], (x, y, 1 - c), lands[t].at[_slot(*back)])
            for t in range(len(lands)) for out, back in blocks]


def _gather_pass_on(name, handle, after):
    n = len(handle[2])
    _, lands = _split_wait(name + "_wait", handle, 4 * n, _gather_plan, after)
    return _split_start(name + "_pass_on", [], lands, 4 * n, _pass_on_plan, after)


def _gather_finish(name, handle, after):
    n = len(handle[3])
    _, lands = _split_wait(name + "_done", handle, 4 * n, _pass_on_plan, after)
    return lands


def _sibling_plan(ps, gots):
    x, y, c = _place()
    return [(ps[t].at[j, 1 - c], gots[t].at[j], (x, y, 1 - c), gots[t].at[j]) for t in range(len(ps)) for j in range(4)]


def _chips_plan(ss, gots):
    x, y, c = _place()
    chips = [(1 - x, y), (x, 1 - y), (1 - x, 1 - y)]
    return [(ss[t].at[2 * px + py], gots[t].at[k], (px, py, c), gots[t].at[k])
            for t in range(len(ss)) for k, (px, py) in enumerate(chips)]


BF16_ROWS = 16


def _shard_tiles(R, C, block_bytes):
    tr = _tile(R, max(BF16_ROWS, min(1024, block_bytes // (C * 4))), BF16_ROWS)
    tc = C if tr * C * 4 <= 2 * block_bytes else _tile(C, max(LANES, block_bytes // (tr * 4)))
    return tr, tc


def _pair_sum(name, p4, got, core):
    _, _, R, C = p4.shape
    tr, tc = _shard_tiles(R, C, ROW_BLOCK_BYTES)

    def body(core_ref, p_ref, g_ref, o_ref):
        o_ref[...] = (p_ref[:, 0] + g_ref[...]).astype(o_ref.dtype)

    return pl.pallas_call(
        body, name=name, out_shape=jax.ShapeDtypeStruct((4, R, C), BF16),
        grid_spec=pltpu.PrefetchScalarGridSpec(
            num_scalar_prefetch=1, grid=(4, R // tr, C // tc),
            in_specs=[pl.BlockSpec((1, 1, tr, tc), lambda j, i, k, core: (j, core[0], i, k)),
                      pl.BlockSpec((1, tr, tc), lambda j, i, k, core: (j, i, k))],
            out_specs=pl.BlockSpec((1, tr, tc), lambda j, i, k, core: (j, i, k))),
        compiler_params=_params(("parallel", "parallel", "parallel")),
    )(core, p4, got)


def _sum_devices(name, g8):
    _, R, C = g8.shape
    tr = _tile(R, SMALL_ROW_ALIGN, 8)

    def body(g_ref, o_ref):
        acc = g_ref[0]
        for d in range(1, N_DEV):
            acc = acc + g_ref[d]
        o_ref[...] = acc

    return pl.pallas_call(
        body, name=name, out_shape=jax.ShapeDtypeStruct((R, C), F32), grid=(R // tr,),
        in_specs=[pl.BlockSpec((N_DEV, tr, C), lambda i: (0, i, 0))], out_specs=pl.BlockSpec((tr, C), lambda i: (i, 0)),
        compiler_params=_params(("parallel",)),
    )(g8)


def _adamw_shard(name, layer, w, m, v, s4, got, chip, bufs):
    _, R, C = w.shape
    tr, tc = _shard_tiles(R, C, ROW_BLOCK_BYTES // 2)

    def body(chip_ref, w_ref, m_ref, v_ref, s_ref, g_ref, b0, b1, b2, b3, og, od, om, ov):
        f = lambda a: a.astype(F32)
        g = ((f(s_ref[0]) + f(g_ref[0])) + f(g_ref[1])) + f(g_ref[2])
        d, nm, nv = _adamw_fn(w_ref[0], g, m_ref[0], v_ref[0])
        og[0], od[0], om[0], ov[0] = g, d, nm, nv

    lay = pl.BlockSpec((1, tr, tc), lambda i, k, chip: (layer, i, k))
    if bufs is None:
        bufs = [lax.empty(w.shape, F32) for _ in range(4)]
    return pl.pallas_call(
        body, name=name, out_shape=[jax.ShapeDtypeStruct(w.shape, F32)] * 4,
        grid_spec=pltpu.PrefetchScalarGridSpec(
            num_scalar_prefetch=1, grid=(R // tr, C // tc),
            in_specs=[lay, lay, lay, pl.BlockSpec((1, tr, tc), lambda i, k, chip: (chip[0], i, k)),
                      pl.BlockSpec((3, tr, tc), lambda i, k, chip: (0, i, k)), ANY, ANY, ANY, ANY],
            out_specs=[lay] * 4),
        input_output_aliases={6: 0, 7: 1, 8: 2, 9: 3},
        compiler_params=_params(("parallel", "parallel")),
    )(chip, w, m, v, s4, got, *bufs)


def _adamw(name, wt, g, m, v):
    shape = wt.shape
    two = (lambda a: a.reshape(1, -1)) if wt.ndim == 1 else (lambda a: a.reshape(-1, shape[-1]))
    w2, g2, m2, v2 = two(wt), two(g), two(m), two(v)
    tr = _row(w2, None, 0)[2]
    outs = [(w2.shape, F32, 0, tr)] * 3
    d, nm, nv = _rows_fwd(name, _adamw_fn, [_row(a, tr, 0) for a in (w2, g2, m2, v2)], [], outs)
    return d.reshape(shape), nm.reshape(shape), nv.reshape(shape)


def _lane_tiling():
    return (jnp.arange(SSM_N)[:, None] == (jnp.arange(SSM_S) % SSM_N)[None, :]).astype(BF16)


def _own_block():
    r = lax.broadcasted_iota(jnp.int32, (SSM_G * SSM_P, SSM_S), 0) // SSM_P
    c = lax.broadcasted_iota(jnp.int32, (SSM_G * SSM_P, SSM_S), 1) // SSM_N
    return r == c


def _bd_build(name, v_re, v_im, sign, tiling):
    def body(r_ref, i_ref, t_ref, o_ref):
        own = _own_block()
        o_ref[:, :SSM_S] = jnp.where(own, _dot(r_ref[...].astype(BF16), t_ref[...]), 0.0).astype(BF16)
        o_ref[:, SSM_S:] = jnp.where(own, sign * _dot(i_ref[...].astype(BF16), t_ref[...]), 0.0).astype(BF16)

    rows = SSM_G * SSM_P
    return pl.pallas_call(
        body, name=name, out_shape=jax.ShapeDtypeStruct((rows, 2 * SSM_S), BF16), grid=(1,),
        in_specs=[_full_spec((rows, SSM_N))] * 2 + [_full_spec((SSM_N, SSM_S))], out_specs=_full_spec((rows, 2 * SSM_S)),
        compiler_params=_params(("arbitrary",)),
    )(v_re, v_im, tiling)


def _bd_extract(name, m, sign, tiling):
    def body(m_ref, t_ref, r_ref, i_ref):
        own, t = _own_block(), t_ref[...]

        def pick(x):
            x = jnp.where(own, x, 0.0)
            hi = x.astype(BF16)
            rest = x - hi.astype(F32)
            mid = rest.astype(BF16)
            lo = (rest - mid.astype(F32)).astype(BF16)
            return _dot(hi, t, NT) + _dot(mid, t, NT) + _dot(lo, t, NT)

        r_ref[...] = pick(m_ref[:, :SSM_S])
        i_ref[...] = sign * pick(m_ref[:, SSM_S:])

    rows = SSM_G * SSM_P
    return pl.pallas_call(
        body, name=name, out_shape=[jax.ShapeDtypeStruct((rows, SSM_N), F32)] * 2, grid=(1,),
        in_specs=[_full_spec((rows, 2 * SSM_S)), _full_spec((SSM_N, SSM_S))], out_specs=[_full_spec((rows, SSM_N))] * 2,
        compiler_params=_params(("arbitrary",)),
    )(m, tiling)


def _fold(a, dil):
    if dil == 1:
        return a
    return a.reshape((T // dil, dil) + a.shape[1:]).swapaxes(0, 1).reshape(a.shape)


def _unfold(a, dil):
    if dil == 1:
        return a
    return a.reshape((dil, T // dil) + a.shape[1:]).swapaxes(0, 1).reshape(a.shape)


DILS = (1, 4, 16)


def _fold3(parts):
    parts = [parts] * 3 if not isinstance(parts, (list, tuple)) else parts
    return jnp.stack([_fold(a, d) for a, d in zip(parts, DILS)])


def _unfold3(a):
    return [_unfold(a[p], d) for p, d in enumerate(DILS)]


def _rope_tables():
    half = ROPE // 2
    inv_freq = ROPE_THETA ** (-jnp.arange(half, dtype=F32) / half)
    ang = jnp.arange(T).astype(F32)[:, None] * inv_freq[None, :]
    i, j = jnp.arange(ROPE)[:, None], jnp.arange(ROPE)[None, :]
    rot = jnp.where(i == j + half, -1.0, jnp.where(i + half == j, 1.0, 0.0)).astype(F32)
    return jnp.tile(jnp.cos(ang), (1, 2)), jnp.tile(jnp.sin(ang), (1, 2)), rot


def _rot_half(x, rot):
    return _dot(x.reshape(-1, ROPE), rot, prec=HI).reshape(x.shape)


def _mla_pack_fn(q, kv, k_rope, cos, sin, rot):
    rope = lambda x: x * cos + _rot_half(x, rot) * sin
    q_out = jnp.concatenate([q[:, :, :NOPE], rope(q[:, :, NOPE:])], axis=-1)
    k_pe = jnp.broadcast_to(rope(k_rope)[None], (H_MLA,) + k_rope.shape)
    return q_out, jnp.concatenate([kv[:, :, :NOPE], k_pe], axis=-1), kv[:, :, NOPE:]


def _mla_unpack_fn(dq, dk, dv, cos, sin, rot):
    unrope = lambda g: g * cos - _rot_half(g * sin, rot)
    dq_out = jnp.concatenate([dq[:, :, :NOPE], unrope(dq[:, :, NOPE:])], axis=-1)
    dk_rope = unrope(jnp.sum(dk[:, :, NOPE:], axis=0))
    return dq_out, jnp.concatenate([dk[:, :, :NOPE], dv], axis=-1), dk_rope


def _flat(w8):
    return w8.reshape(-1, w8.shape[-1])


def _blocks(m):
    return m.reshape(N_DEV, -1, m.shape[-1])


def _behind(a, tok):
    return a if tok is None else a + tok


def _mixer_fwd_in(x, w, sp, rope):
    s = {}
    s['x'] = x
    h = _rms_fwd("rms_mix", x, sp['g_mix'])
    proj = _mm("mm_in", h, _flat(w['w_in']), 'nt')
    offs = np.cumsum((0,) + IN_SPLITS)
    c_q, c_kv, k_rope, u = [proj[:, offs[i]:offs[i + 1]] for i in range(4)]
    qkv_d = proj[:, offs[4]:]
    s.update(h=h, c_q=c_q, c_kv=c_kv, u=u)

    cqn = _rms_fwd("rms_q", c_q, sp['g_q'])
    ckvn = _rms_fwd("rms_kv", c_kv, sp['g_kv'])
    q8 = _mm("mm_uq", cqn, w['w_uq'], 'nt', bb='r', ob='c')
    kv8 = _mm("mm_ukv", ckvn, w['w_ukv'], 'nn', bb='c', ob='c')
    cos, sin, rot = rope
    tr = MLA_TQ
    qh, kh, vh = _rows_fwd(
        "mla_pack", _mla_pack_fn,
        [_row(q8, tr, 1), _row(kv8, tr, 1), _row(k_rope, tr, 0), _row(cos, tr, 0), _row(sin, tr, 0)], [rot],
        [((H_MLA, T, QK), BF16, 1, tr), ((H_MLA, T, QK), BF16, 1, tr), ((H_MLA, T, VDIM), BF16, 1, tr)])
    y_mla, lse_mla = _mla_fwd(qh, kh, vh)
    s.update(cqn=cqn, ckvn=ckvn, qh=qh, kh=kh, vh=vh, lse_mla=lse_mla, y_mla=y_mla, qkv_d=qkv_d)
    return s


def _mixer_fwd_out(s, w, sp, tok=None, after_ssm=None):
    x, u, y_mla, qkv_d = s['x'], s['u'], s['y_mla'], s['qkv_d']
    a3 = lambda n: sp[n].reshape(SSM_G, 1, SSM_N)
    b2 = lambda n: sp[n].transpose(0, 2, 1).reshape(SSM_G * SSM_P, SSM_N)
    disc_rows = [_row(a3('a_re'), 1, 0), _row(a3('a_im'), 1, 0), _row(sp['log_dt'].reshape(SSM_G, 1, 1), 1, 0),
                 _row(b2('b_re'), SSM_P, 0), _row(b2('b_im'), SSM_P, 0)]
    abr, abi, bbr, bbi = _rows_fwd(
        "s5_disc", _s5_disc_fn, disc_rows, [],
        [((SSM_G, 1, SSM_N), F32, 0, 1), ((SSM_G, 1, SSM_N), F32, 0, 1),
         ((SSM_G * SSM_P, SSM_N), F32, 0, SSM_P), ((SSM_G * SSM_P, SSM_N), F32, 0, SSM_P)])
    ar, ai = abr.reshape(1, SSM_S), abi.reshape(1, SSM_S)
    tiling = _lane_tiling()
    b_mat = _bd_build("s5_b_matrix", bbr, bbi, 1.0, tiling)
    c2 = lambda n: sp[n].reshape(SSM_G * SSM_P, SSM_N)
    c_mat = _bd_build("s5_c_matrix", c2('c_re'), c2('c_im'), -1.0, tiling)
    u16 = _behind(u, tok).astype(BF16)
    bu = _mm("mm_s5_b", u16, b_mat, 'nn')
    hst, hst16 = _scan_fwd(bu, ar, ai)
    ymm = _mm("mm_s5_c", hst16, c_mat, 'nt')
    d_row = sp['d_skip'].reshape(1, SSM_W)
    (yg,) = _rows_fwd("s5_act", _s5_act_fn, [_row(ymm), _row(u)], [d_row], [((T, SSM_W), BF16, -2, _row(u)[2])])
    z = _mm("mm_glu", yg, w['w_glu'], 'nn', bb='c')
    glu_rows = [_row(z[:, :SSM_W]), _row(z[:, SSM_W:])]
    glu_b = [sp['b_glu'][:SSM_W].reshape(1, -1), sp['b_glu'][SSM_W:].reshape(1, -1)]
    (y_ssm,) = _rows_fwd("s5_glu", _glu_fn, glu_rows, glu_b, [((T, SSM_W), F32, -2, glu_rows[0][2])])
    if after_ssm is not None:
        qkv_d = _behind(qkv_d, after_ssm(y_ssm))
    s.update(disc_rows=disc_rows, ar=ar, ai=ai, b_mat=b_mat, c_mat=c_mat, hst=hst, hst16=hst16, u16=u16, ymm=ymm,
             d_row=d_row, yg=yg,
             glu_rows=glu_rows, glu_b=glu_b)

    qkv_f = _fold3(qkv_d).astype(BF16)
    o_f, lse_f = _band_fwd(qkv_f)
    mix_rows = [_row(a) for a in _unfold3(o_f) + _unfold3(lse_f)]
    (y_dil,) = _rows_fwd("dil_mix", _dil_mix_fn, mix_rows, [], [((T, DIL_W), F32, -2, mix_rows[0][2])])
    s.update(qkv_f=qkv_f, o_f=o_f, lse_f=lse_f, mix_rows=mix_rows)

    gm, gs, gd = sp['g_out_mla'].reshape(1, -1), sp['g_out_ssm'].reshape(1, -1), sp['g_out_dil'].reshape(1, -1)
    on_rows = [_row(y_mla), _row(y_ssm), _row(y_dil)]
    (ycat,) = _rows_fwd("out_norm", _outnorm_fn, on_rows, [gm, gs, gd], [((T, D), BF16, -2, on_rows[0][2])])
    x1_ = _mm("mm_o", ycat, _flat(w['w_o']), 'nn', res=x)
    s.update(on_rows=on_rows, on_g=[gm, gs, gd], ycat=ycat, x1=x1_)
    del s['qkv_d']
    return x1_


def _ffn_fwd(x1_, w, sp, s, tok=None):
    h2 = _rms_fwd("rms_ffn", x1_, _behind(sp['g_ffn'], tok))
    ga = _mm("mm_gate", h2, _flat(w['w_gate']), 'nt')
    gb, zf = _mm("mm_up", h2, _flat(w['w_up']), 'nt',
                 epilogue=(lambda up, gate: (up, _swiglu_fn(gate, up)), [ga], [F32, BF16]))
    x2_ = _mm("mm_down", zf, _flat(w['w_down']), 'nn', res=x1_)
    s.update(h2=h2, ga=ga, gb=gb, zf=zf)
    return x2_


def _b16(a):
    return a.astype(BF16)


def _ffn_bwd(dx2, s, w, sp, tok=None):
    gw, gs_ = {}, {}
    b16 = _b16
    dx2b = b16(_behind(dx2, tok))
    def gate_pullback(dz, gate, up):
        return jax.vjp(_swiglu_fn, gate, up)[1](dz)

    dga, dgb = _mm("mm_down_dx", dx2b, _flat(w['w_down']), 'nt', epilogue=(gate_pullback, [s['ga'], s['gb']], [BF16, BF16]))
    gw['w_down'] = _blocks(_mm("mm_down_dw", s['zf'], dx2b, 'tn'))
    gw['w_gate'] = _blocks(_mm("mm_gate_dw", dga, s['h2'], 'tn'))
    gw['w_up'] = _blocks(_mm("mm_up_dw", dgb, s['h2'], 'tn'))
    dh2 = _mm("mm_up_dx", dgb, _flat(w['w_up']), 'nn', res=_mm("mm_gate_dx", dga, _flat(w['w_gate']), 'nn'))
    dx1, gs_['g_ffn'] = _rms_bwd("rms_ffn_bwd", s['x1'], sp['g_ffn'], dh2, dx2)
    return dx1, gw, gs_


def _mixer_bwd_out(dx1, s, w, sp, tok=None):
    gw, gs_ = {}, {}
    b16 = _b16
    dx1b = b16(_behind(dx1, tok))
    dycat = _mm("mm_o_dx", dx1b, _flat(w['w_o']), 'nt')
    gw['w_o'] = _blocks(_mm("mm_o_dw", s['ycat'], dx1b, 'tn'))
    dy_mla, dy_ssm, dy_dil, gs_['g_out_mla'], gs_['g_out_ssm'], gs_['g_out_dil'] = _rows_vjp(
        "out_norm_bwd", _outnorm_fn, s['on_rows'], s['on_g'], [], [_row(dycat)])

    dmix = _rows_vjp("dil_mix_bwd", _dil_mix_fn, s['mix_rows'], [], [], [_row(dy_dil)])
    dqkv_f = _band_bwd(s['qkv_f'], s['o_f'], s['lse_f'], _fold3(dmix[:3]), _fold3(dmix[3:]))
    dqkv_d = sum(_unfold3(dqkv_f))

    dz1, dz2, db1, db2 = _rows_vjp("s5_glu_bwd", _glu_fn, s['glu_rows'], s['glu_b'], [], [_row(dy_ssm)])
    gs_['b_glu'] = jnp.concatenate([db1, db2], axis=1)
    dzb = b16(jnp.concatenate([dz1, dz2], axis=1))
    dyg = _mm("mm_glu_dx", dzb, w['w_glu'], 'nt', bb='c')
    gw['w_glu'] = _mm("mm_glu_dw", s['yg'], dzb, 'tn', ob='c')
    dymm, du_act, dd = _rows_vjp("s5_act_bwd", _s5_act_fn, [_row(s['ymm']), _row(s['u'])], [s['d_row']], [], [_row(dyg)],
                                 grad_dtypes=[BF16, F32])
    gs_['d_skip'] = dd
    dhst = _mm("mm_s5_c_dx", dymm, s['c_mat'], 'nn')
    dc_mat = _mm("mm_s5_c_dw", dymm, s['hst16'], 'tn')
    g, dar, dai = _scan_bwd(dhst, s['hst'], s['ar'], s['ai'])
    du = _mm("mm_s5_b_dx", g, s['b_mat'], 'nt', res=du_act)
    db_mat = _mm("mm_s5_b_dw", s['u16'], g, 'tn')
    tiling = _lane_tiling()
    gs_['c_re'], gs_['c_im'] = _bd_extract("s5_c_blocks", dc_mat, -1.0, tiling)
    dbbr, dbbi = _bd_extract("s5_b_blocks", db_mat, 1.0, tiling)
    disc_cts = [_row(dar.reshape(SSM_G, 1, SSM_N), 1, 0), _row(dai.reshape(SSM_G, 1, SSM_N), 1, 0),
                _row(dbbr, SSM_P, 0), _row(dbbi, SSM_P, 0)]
    da_re, da_im, dldt, db_r, db_i = _rows_vjp("s5_disc_bwd", _s5_disc_fn, s['disc_rows'], [], [], disc_cts)
    gs_['a_re'], gs_['a_im'], gs_['log_dt'] = da_re, da_im, dldt
    unb = lambda a: a.reshape(SSM_G, SSM_P, SSM_N).transpose(0, 2, 1)
    gs_['b_re'], gs_['b_im'] = unb(db_r), unb(db_i)
    return (dy_mla, du, dqkv_d), gw, gs_


def _mixer_bwd_in(cts, dx1, s, w, sp, rope, tok=None):
    gw, gs_ = {}, {}
    b16 = _b16
    dy_mla, du, dqkv_d = cts
    dqh, dkh, dvh = _mla_bwd(s['qh'], s['kh'], s['vh'], s['y_mla'], dy_mla, _behind(s['lse_mla'], tok))
    cos, sin, rot = rope
    tr = MLA_TQ
    dq8, dkv8, dk_rope = _rows_fwd(
        "mla_unpack", _mla_unpack_fn,
        [_row(dqh, tr, 1), _row(dkh, tr, 1), _row(dvh, tr, 1), _row(cos, tr, 0), _row(sin, tr, 0)], [rot],
        [((H_MLA, T, QK), BF16, 1, tr), ((H_MLA, T, NOPE + VDIM), BF16, 1, tr), ((T, ROPE), F32, 0, tr)])
    dcqn = _mm("mm_uq_dx", dq8, w['w_uq'], 'nn', ab='c', bb='r')
    gw['w_uq'] = _mm("mm_uq_dw", dq8, s['cqn'], 'tn', ab='c', ob='r')
    dckvn = _mm("mm_ukv_dx", dkv8, w['w_ukv'], 'nt', ab='c', bb='c')
    gw['w_ukv'] = _mm("mm_ukv_dw", s['ckvn'], dkv8, 'tn', bb='c', ob='c')
    dc_q, gs_['g_q'] = _rms_bwd("rms_q_bwd", s['c_q'], sp['g_q'], dcqn)
    dc_kv, gs_['g_kv'] = _rms_bwd("rms_kv_bwd", s['c_kv'], sp['g_kv'], dckvn)

    dproj = b16(jnp.concatenate([dc_q, dc_kv, dk_rope, du, dqkv_d], axis=1))
    dh = _mm("mm_in_dx", dproj, _flat(w['w_in']), 'nn')
    gw['w_in'] = _blocks(_mm("mm_in_dw", dproj, s['h'], 'tn'))
    dx, gs_['g_mix'] = _rms_bwd("rms_mix_bwd", s['x'], sp['g_mix'], dh, dx1)
    return dx, gw, gs_


def kernel(x, g_mix, w_in, g_q, w_uq, g_kv, w_ukv, a_re, a_im, b_re, b_im, c_re, c_im, d_skip, log_dt, w_glu, b_glu, g_out_mla, g_out_ssm, g_out_dil, w_o, g_ffn, w_gate, w_up, w_down, g_final, loss_target, m_g_mix, m_w_in, m_g_q, m_w_uq, m_g_kv, m_w_ukv, m_a_re, m_a_im, m_b_re, m_b_im, m_c_re, m_c_im, m_d_skip, m_log_dt, m_w_glu, m_b_glu, m_g_out_mla, m_g_out_ssm, m_g_out_dil, m_w_o, m_g_ffn, m_w_gate, m_w_up, m_w_down, m_g_final, v_g_mix, v_w_in, v_g_q, v_w_uq, v_g_kv, v_w_ukv, v_a_re, v_a_im, v_b_re, v_b_im, v_c_re, v_c_im, v_d_skip, v_log_dt, v_w_glu, v_b_glu, v_g_out_mla, v_g_out_ssm, v_g_out_dil, v_w_o, v_g_ffn, v_w_gate, v_w_up, v_w_down, v_g_final):
    W = dict(zip(PARAMS, (g_mix, w_in, g_q, w_uq, g_kv, w_ukv, a_re, a_im, b_re, b_im, c_re, c_im, d_skip, log_dt,
                          w_glu, b_glu, g_out_mla, g_out_ssm, g_out_dil, w_o, g_ffn, w_gate, w_up, w_down, g_final)))
    M = dict(zip(PARAMS, (m_g_mix, m_w_in, m_g_q, m_w_uq, m_g_kv, m_w_ukv, m_a_re, m_a_im, m_b_re, m_b_im, m_c_re,
                          m_c_im, m_d_skip, m_log_dt, m_w_glu, m_b_glu, m_g_out_mla, m_g_out_ssm, m_g_out_dil, m_w_o,
                          m_g_ffn, m_w_gate, m_w_up, m_w_down, m_g_final)))
    V = dict(zip(PARAMS, (v_g_mix, v_w_in, v_g_q, v_w_uq, v_g_kv, v_w_ukv, v_a_re, v_a_im, v_b_re, v_b_im, v_c_re,
                          v_c_im, v_d_skip, v_log_dt, v_w_glu, v_b_glu, v_g_out_mla, v_g_out_ssm, v_g_out_dil, v_w_o,
                          v_g_ffn, v_w_gate, v_w_up, v_w_down, v_g_final)))
    cx, cy, cc = _place()
    core = cc.astype(jnp.int32).reshape(1)
    chip = (2 * cx + cy).astype(jnp.int32).reshape(1)
    rope = _rope_tables()
    small = [{n: W[n][l] for n in SMALL} for l in range(DEPTH)]
    for sp in small:
        for n in ('g_mix', 'g_q', 'g_kv', 'g_ffn'):
            sp[n] = sp[n].reshape(1, -1)

    def tok_of(tokens):
        return sum(t[0, 0] for t in tokens) if tokens else None

    def shard_view(a, n):
        return a.swapaxes(1, 2) if BIG[n] == 't' else a

    def gather_start(l, group, names, after):
        return _gather_start(f"gather_{group}_start_{l}", [shard_view(W[n], n)[l].astype(BF16) for n in names], after)

    xa = x[0]
    h1_mix = gather_start(0, "mix", MIXER_W, jnp.zeros((8, LANES), F32))
    h1_ffn = gather_start(0, "ffn", FFN_W, h1_mix[4])
    h2_mix = _gather_pass_on("gather_mix_0", h1_mix, xa)
    saved, full = [], []
    tokens = [h2_mix[4], h1_ffn[4]]
    for l in range(DEPTH):
        last = l + 1 == DEPTH
        wm = dict(zip(MIXER_W, _gather_finish(f"gather_mix_{l}", h2_mix, xa)))
        sp = dict(small[l])
        sp['g_mix'] = _behind(sp['g_mix'], tok_of(tokens))
        s = _mixer_fwd_in(xa, wm, sp, rope)
        tokens = []
        first_ffn = {}
        if l == 0:
            h1_first = h1_ffn
            def mid(dep):
                first_ffn['h'] = _gather_pass_on("gather_ffn_0", h1_first, dep)
                return first_ffn['h'][4][0, 0]
        else:
            mid = None
        if not last:
            h1_mix = gather_start(l + 1, "mix", MIXER_W, s['y_mla'])
            h1_ffn = gather_start(l + 1, "ffn", FFN_W, h1_mix[4])
            tokens += [h1_mix[4], h1_ffn[4]]
        x1 = _mixer_fwd_out(s, wm, small[l], tok_of(tokens), mid)
        if l == 0:
            h2_ffn = first_ffn['h']
        tokens = []
        wf = dict(zip(FFN_W, _gather_finish(f"gather_ffn_{l}", h2_ffn, x1)))
        if not last:
            h2_mix = _gather_pass_on(f"gather_mix_{l + 1}", h1_mix, x1)
            tokens.append(h2_mix[4])
        xa = _ffn_fwd(x1, wf, small[l], s, tok_of(tokens))
        tokens = []
        if not last:
            h2_ffn = _gather_pass_on(f"gather_ffn_{l + 1}", h1_ffn, xa)
            tokens.append(h2_ffn[4])
        saved.append(s)
        full.append({**wm, **wf})
    gf = g_final.reshape(1, D)
    ones = jnp.ones((T, 1), F32)
    dxa, dgf, loss_rows = _rows_vjp("loss", _loss_fn, [_row(xa)], [gf], [_row(loss_target[0])], [_row(ones)],
                                    primal=True)
    loss_here = _sum_rows("loss_sum", loss_rows)[0, 0]

    bufs = {n: None for n in BIG}
    pending = []

    def advance(dep):
        tokens = []
        for g in pending:
            names, tag = g['names'], g['tag']
            step = g['steps'].pop(0)
            if step == 'sibling':
                p4 = [a.reshape((4, 2) + a.shape[1:]) for a in g['gw']]
                gots = [lax.empty((4,) + a.shape[1:], F32) for a in g['gw']]
                g['h'] = _split_start("rs_sibling_start_" + tag, p4, gots, 4 * len(p4), _sibling_plan, dep)
                tokens.append(g['h'][4])
            elif step == 'chips':
                p4, gots = _split_wait("rs_sibling_wait_" + tag, g['h'], 4 * len(names), _sibling_plan, dep)
                s4 = [_pair_sum("rs_pair_sum_" + n, p, q, core) for n, p, q in zip(names, p4, gots)]
                gots = [lax.empty((3,) + a.shape[1:], a.dtype) for a in s4]
                g['h'] = _split_start("rs_chips_start_" + tag, s4, gots, 3 * len(s4), _chips_plan, dep)
                tokens.append(g['h'][4])
            elif step == 'update':
                s4, gots = _split_wait("rs_chips_wait_" + tag, g['h'], 3 * len(names), _chips_plan, dep)
                for n, s4n, got in zip(names, s4, gots):
                    bufs[n] = _adamw_shard("adamw_" + n, g['layer'], shard_view(W[n], n), shard_view(M[n], n),
                                           shard_view(V[n], n), s4n, got, chip, bufs[n])
        pending[:] = [g for g in pending if g['steps']]
        return tokens

    def group(names, l, gw, kind):
        steps = ['sibling'] + (['rest'] if kind == "ffn" and l > 0 else []) + ['chips', 'rest', 'update']
        return dict(names=names, layer=l, gw=[gw[n] for n in names], steps=steps, tag=f"{kind}_{l}")

    def pack_small(parts):
        n = sum(int(p.shape[0]) for p in parts)
        rows = -(-n // (PACK_C * SMALL_ROW_ALIGN)) * SMALL_ROW_ALIGN
        return jnp.concatenate(parts + [jnp.zeros((rows * PACK_C - n,), F32)]).reshape(rows, PACK_C)

    def small_of(layers):
        return [g_small[k][n].reshape(-1) for k in layers for n in SMALL]

    g_small = [None] * DEPTH
    tokens = []
    upper = list(range(1, DEPTH))
    tot_upper = None
    for l in reversed(range(DEPTH)):
        early = l == 0 and upper
        if early:
            flat_upper = pack_small(small_of(upper))
            h_upper = _gather_start("gather_upper_start", [flat_upper], dxa)
            tokens.append(h_upper[4])
        dx1, gw_f, gs_f = _ffn_bwd(dxa, saved[l], full[l], small[l], tok_of(tokens))
        pending.append(group(FFN_W, l, gw_f, "ffn"))
        tokens = advance(dx1)
        if early:
            h_upper = _gather_pass_on("gather_upper", h_upper, dx1)
            tokens.append(h_upper[4])
        cts, gw_o, gs_o = _mixer_bwd_out(dx1, saved[l], full[l], small[l], tok_of(tokens))
        pending.append(group(OUT_W, l, gw_o, "out"))
        tokens = advance(cts[0])
        if early:
            (g_upper,) = _gather_finish("gather_upper", h_upper, cts[0])
            tot_upper = _sum_devices("small_sum_upper", g_upper).reshape(-1)
        dxa, gw_i, gs_i = _mixer_bwd_in(cts, dx1, saved[l], full[l], small[l], rope, tok_of(tokens))
        pending.append(group(IN_W, l, gw_i, "in"))
        tokens = advance(dxa)
        g_small[l] = {**gs_f, **gs_o, **gs_i}

    first = [0] if upper else list(range(DEPTH))
    flat = pack_small(small_of(first) + [dgf.reshape(-1), loss_here.reshape(1)])
    h_small = _gather_start("gather_small_start", [flat], dxa)
    advance(h_small[4])
    h_small = _gather_pass_on("gather_small", h_small, flat)
    tokens = advance(h_small[4])
    (gathered,) = _gather_finish("gather_small", h_small, flat)
    tot = _behind(_sum_devices("small_sum", gathered).reshape(-1), tok_of(tokens))

    def unpack(flat_sum, layers):
        out, off = {}, 0
        for k in layers:
            for n, shp in SMALL.items():
                size = int(np.prod(shp))
                out[k, n] = flat_sum[off:off + size].reshape(shp)
                off += size
        return out, off

    per_layer, off = unpack(tot, first)
    if upper:
        per_layer.update(unpack(tot_upper, upper)[0])
    grads = {n: jnp.stack([per_layer[k, n] for k in range(DEPTH)]) for n in SMALL}
    grads['g_final'] = tot[off:off + D]
    loss = tot[off + D]

    delta, new_m, new_v = {}, {}, {}
    for n in PARAMS:
        if n not in BIG:
            delta[n], new_m[n], new_v[n] = _adamw("adamw_" + n, W[n], grads[n], M[n], V[n])
    while pending:
        advance(delta['g_final'])
    for n in BIG:
        grads[n], delta[n], new_m[n], new_v[n] = [shard_view(b, n) for b in bufs[n]]
    return (loss, dxa[None], *[grads[n] for n in PARAMS], *[delta[n] for n in PARAMS],
            *[new_m[n] for n in PARAMS], *[new_v[n] for n in PARAMS])
```

```python
import jax
import jax.numpy as jnp
import numpy as np
from jax import lax
from jax.experimental import pallas as pl
from jax.experimental.pallas import tpu as pltpu

F32 = jnp.float32
BF16 = jnp.bfloat16

T = 2048
D = 2048
DEPTH = 4
N_DEV = 8
H_MLA, NOPE, ROPE, VDIM = 8, 128, 64, 128
QK = NOPE + ROPE
Q_LORA, KV_LORA = 512, 256
SSM_W, SSM_G, SSM_P, SSM_N = 512, 32, 16, 64
SSM_S = SSM_G * SSM_N
DIL_W, DIL_H, DIL_D = 512, 8, 64
BLK = 128
IN_SPLITS = (Q_LORA, KV_LORA, ROPE, SSM_W, DIL_W, DIL_W, DIL_W)
IN_W = sum(IN_SPLITS)
D_FF = 5632
EPS = 1e-6
ROPE_THETA = 10000.0
MLA_SCALE = QK ** -0.5
DIL_SCALE = DIL_D ** -0.5

ADAM_LR, ADAM_B1, ADAM_B2, ADAM_EPS, ADAM_WD, ADAM_STEP = 0.001, 0.9, 0.999, 1e-08, 0.01, 10

VMEM_LIMIT_V7X = 52 * 1024 * 1024
LANES = 128
PACK_C = 1024
ROW_BLOCK_BYTES = 2 * 1024 * 1024
MM_TM, MM_TN, MM_TK = 1408, 1024, 5632
MM_DEEP = IN_W
MM_TB = 512
EPILOGUE_SLABS = 4

NT = (((1,), (1,)), ((), ()))
TN = (((0,), (0,)), ((), ()))
HI = lax.Precision.HIGHEST
MESH = pl.DeviceIdType.MESH

PARAMS = ['g_mix', 'w_in', 'g_q', 'w_uq', 'g_kv', 'w_ukv', 'a_re', 'a_im', 'b_re', 'b_im', 'c_re', 'c_im',
          'd_skip', 'log_dt', 'w_glu', 'b_glu', 'g_out_mla', 'g_out_ssm', 'g_out_dil', 'w_o', 'g_ffn',
          'w_gate', 'w_up', 'w_down', 'g_final']
BIG = {'w_in': 't', 'w_uq': 't', 'w_ukv': 'c', 'w_glu': 'c', 'w_o': 'r', 'w_gate': 't', 'w_up': 't', 'w_down': 'r'}
MIXER_W = ['w_in', 'w_uq', 'w_ukv', 'w_glu', 'w_o']
FFN_W = ['w_gate', 'w_up', 'w_down']
OUT_W, IN_W = ['w_o', 'w_glu'], ['w_in', 'w_uq', 'w_ukv']
SMALL = {'g_mix': (D,), 'g_q': (Q_LORA,), 'g_kv': (KV_LORA,), 'a_re': (SSM_G, SSM_N), 'a_im': (SSM_G, SSM_N),
         'b_re': (SSM_G, SSM_N, SSM_P), 'b_im': (SSM_G, SSM_N, SSM_P), 'c_re': (SSM_G, SSM_P, SSM_N),
         'c_im': (SSM_G, SSM_P, SSM_N), 'd_skip': (SSM_G, SSM_P), 'log_dt': (SSM_G,), 'b_glu': (2 * SSM_W,),
         'g_out_mla': (H_MLA * VDIM,), 'g_out_ssm': (SSM_W,), 'g_out_dil': (DIL_W,), 'g_ffn': (D,)}
SMALL_ROW_ALIGN = 64


def _tile(dim, target, align=LANES):
    best = None
    for t in range(align, min(dim, target) + 1, align):
        if dim % t == 0:
            best = t
    return best if best is not None else dim


def _params(sem=None):
    return pltpu.CompilerParams(dimension_semantics=sem, vmem_limit_bytes=VMEM_LIMIT_V7X)


def _dot(a, b, dims=None, prec=None):
    if dims is None:
        return jnp.dot(a, b, preferred_element_type=F32, precision=prec)
    return lax.dot_general(a, b, dims, preferred_element_type=F32, precision=prec)


def _mm_spec(shape, blk, t_r, t_c, rc):
    if blk is None:
        return pl.BlockSpec((t_r, t_c), rc)
    _, R, C = shape
    if blk == 'r':
        per = R // t_r
        return pl.BlockSpec((1, t_r, t_c), lambda i, j, k: (rc(i, j, k)[0] // per, rc(i, j, k)[0] % per, rc(i, j, k)[1]))
    per = C // t_c
    return pl.BlockSpec((1, t_r, t_c), lambda i, j, k: (rc(i, j, k)[1] // per, rc(i, j, k)[0], rc(i, j, k)[1] % per))


def _logical(shape, blk):
    if blk is None:
        return tuple(shape)
    G, R, C = shape
    return (G * R, C) if blk == 'r' else (R, G * C)


def _mm(name, a, b, mode, ab=None, bb=None, ob=None, res=None, prec=None, epilogue=None):
    la, lb = _logical(a.shape, ab), _logical(b.shape, bb)
    am, ak = (0, 1) if mode != 'tn' else (1, 0)
    bk, bn = (0, 1) if mode != 'nt' else (1, 0)
    M, K, N = la[am], la[ak], lb[bn]
    assert lb[bk] == K, (name, a.shape, b.shape, mode)
    if ob is None:
        out_shape = (M, N)
    elif ob == 'r':
        G = N_DEV
        out_shape = (G, M // G, N)
    else:
        G = N_DEV
        out_shape = (G, M, N // G)
    em = min(a.shape[-2:][am], out_shape[-2])
    en = min(b.shape[-2:][bn], out_shape[-1])
    ek = min(a.shape[-2:][ak], b.shape[-2:][bk])
    dims = {'nn': None, 'nt': NT, 'tn': TN}[mode]
    a_kb = mode != 'tn' and ab == 'c'
    b_kb = (mode == 'nn' and bb == 'r') or (mode == 'nt' and bb == 'c')
    blocks = K // ek if (a_kb or b_kb) else 1
    assert blocks == 1 or ((a_kb or ab is None) and (b_kb or bb is None)), (name, ab, bb, mode)
    tk = ek if blocks > 1 else _tile(ek, MM_TK)
    nk = 1 if blocks > 1 else K // tk
    small = blocks > 1 or nk > 1 or K > MM_DEEP
    tn = _tile(en, MM_TB if small else MM_TN)
    tm = _tile(em, MM_TB if small else (2 * MM_TM if K <= MM_TB else MM_TM))
    if tn > MM_TN:
        tm = _tile(em, MM_TB)
    if tm > 2 * MM_TM:
        tn = _tile(en, MM_TB)

    def val(ref):
        return ref[...] if len(ref.shape) == 2 else ref[0]

    def put(o_ref, r):
        if len(o_ref.shape) == 2:
            o_ref[...] = r
        else:
            o_ref[0] = r

    def k_block(ref, d, blocked, lanes):
        if blocked:
            return ref[d]
        return ref[:, d * ek:(d + 1) * ek] if lanes else ref[d * ek:(d + 1) * ek, :]

    if epilogue is not None:
        epi_fn, extras, out_dtypes = epilogue
        assert nk == 1 and blocks == 1 and ob is None and res is None, name

    def body(*refs):
        if epilogue is not None:
            a_ref, b_ref = refs[:2]
            extra_refs, out_refs = refs[2:2 + len(extras)], refs[2 + len(extras):]
            slab = tm // EPILOGUE_SLABS if (mode != 'tn' and tm % (8 * EPILOGUE_SLABS) == 0) else tm
            for r0 in range(0, tm, slab):
                rows = slice(r0, r0 + slab)
                a_rows = a_ref[rows, :] if slab != tm else val(a_ref)
                outs = epi_fn(_dot(a_rows, val(b_ref), dims, prec), *[e[rows, :] for e in extra_refs])
                for o_ref, o in zip(out_refs, outs):
                    o_ref[rows, :] = o.astype(o_ref.dtype)
            return
        if res is None:
            a_ref, b_ref, o_ref = refs[:3]
            r_ref = None
        else:
            a_ref, b_ref, r_ref, o_ref = refs[:4]
        if blocks > 1:
            part = None
            for d in range(blocks):
                p = _dot(k_block(a_ref, d, a_kb, True), k_block(b_ref, d, b_kb, mode == 'nt'), dims, prec)
                part = p if part is None else part + p
        else:
            part = _dot(val(a_ref), val(b_ref), dims, prec)
        if nk == 1:
            put(o_ref, part if r_ref is None else part + val(r_ref))
            return
        acc_ref = refs[-1]
        k = pl.program_id(2)

        @pl.when(k == 0)
        def _():
            acc_ref[...] = part

        @pl.when((k > 0) & (k < nk - 1))
        def _():
            acc_ref[...] += part

        @pl.when(k == nk - 1)
        def _():
            r = acc_ref[...] + part
            put(o_ref, r if r_ref is None else r + val(r_ref))

    if blocks > 1:
        G = blocks
        a_spec = (pl.BlockSpec((G, tm, ek), lambda i, j, k: (0, i, 0)) if a_kb
                  else pl.BlockSpec((tm, K), lambda i, j, k: (i, 0)))
        if b_kb:
            b_spec = (pl.BlockSpec((G, ek, tn), lambda i, j, k: (0, 0, j)) if mode == 'nn'
                      else pl.BlockSpec((G, tn, ek), lambda i, j, k: (0, j, 0)))
        else:
            b_spec = (pl.BlockSpec((K, tn), lambda i, j, k: (0, j)) if mode == 'nn'
                      else pl.BlockSpec((tn, K), lambda i, j, k: (j, 0)))
    else:
        if mode == 'tn':
            a_spec = _mm_spec(a.shape, ab, tk, tm, lambda i, j, k: (k, i))
        else:
            a_spec = _mm_spec(a.shape, ab, tm, tk, lambda i, j, k: (i, k))
        if mode == 'nt':
            b_spec = _mm_spec(b.shape, bb, tn, tk, lambda i, j, k: (j, k))
        else:
            b_spec = _mm_spec(b.shape, bb, tk, tn, lambda i, j, k: (k, j))
    o_spec = _mm_spec(out_shape, ob, tm, tn, lambda i, j, k: (i, j))
    if epilogue is not None:
        return pl.pallas_call(
            body, name=name, out_shape=[jax.ShapeDtypeStruct(out_shape, dt) for dt in out_dtypes],
            grid=(M // tm, N // tn, nk), in_specs=[a_spec, b_spec] + [o_spec] * len(extras),
            out_specs=[o_spec] * len(out_dtypes),
            compiler_params=_params(("parallel", "parallel", "arbitrary")),
        )(a, b, *extras)
    in_specs = [a_spec, b_spec] + ([o_spec] if res is not None else [])
    args = (a, b) + ((res,) if res is not None else ())
    return pl.pallas_call(
        body, name=name, out_shape=jax.ShapeDtypeStruct(out_shape, F32),
        grid=(M // tm, N // tn, nk), in_specs=in_specs, out_specs=o_spec,
        scratch_shapes=[pltpu.VMEM((tm, tn), F32)] if nk > 1 else [],
        compiler_params=_params(("parallel", "parallel", "arbitrary")),
    )(*args)


def _row(a, tr=None, axis=-2):
    axis = axis % a.ndim
    n = a.shape[axis]
    if tr is None:
        row_bytes = a.size // n * 4
        tr = _tile(n, max(8, min(256, ROW_BLOCK_BYTES // row_bytes)), 8)
    return (a, axis, tr)


def _row_spec(shape, axis, tr):
    nd = len(shape)
    blk = tuple(tr if d == axis else s for d, s in enumerate(shape))
    return pl.BlockSpec(blk, lambda i: tuple(i if d == axis else 0 for d in range(nd)))


def _full_spec(shape):
    nd = len(shape)
    return pl.BlockSpec(tuple(shape), lambda i: (0,) * nd)


def _steps(entries):
    ns = {a.shape[ax] // tr for a, ax, tr in entries}
    assert len(ns) == 1, [(a.shape, ax, tr) for a, ax, tr in entries]
    return ns.pop()


def _as_tuple(r):
    return tuple(r) if isinstance(r, (tuple, list)) else (r,)


def _rows_fwd(name, fn, rows, bcast, outs):
    steps = _steps(rows)
    nr, nb = len(rows), len(bcast)

    def body(*refs):
        vals = [r[...] for r in refs[:nr + nb]]
        res = _as_tuple(fn(*vals))
        for o_ref, r in zip(refs[nr + nb:], res):
            o_ref[...] = r.astype(o_ref.dtype)

    in_specs = [_row_spec(a.shape, ax, tr) for a, ax, tr in rows] + [_full_spec(b.shape) for b in bcast]
    out_specs = [_row_spec(s, ax % len(s), tr) for s, _, ax, tr in outs]
    res = pl.pallas_call(
        body, name=name, out_shape=[jax.ShapeDtypeStruct(s, dt) for s, dt, _, _ in outs],
        grid=(steps,), in_specs=in_specs, out_specs=out_specs,
        compiler_params=_params(("parallel",)),
    )(*[a for a, _, _ in rows], *bcast)
    return res


def _rows_vjp(name, fn, drows, dbc, arows, cts, primal=False, grad_dtypes=None):
    entries = list(drows) + list(arows) + list(cts)
    steps = _steps(entries)
    ndr, ndb, nar, nct = len(drows), len(dbc), len(arows), len(cts)
    gdt = list(grad_dtypes) if grad_dtypes is not None else [F32] * ndr

    def body(*refs):
        p = 0
        dr = [r[...] for r in refs[p:p + ndr]]; p += ndr
        db = [r[...] for r in refs[p:p + ndb]]; p += ndb
        ar = [r[...] for r in refs[p:p + nar]]; p += nar
        ct = [r[...] for r in refs[p:p + nct]]; p += nct
        g_rows = refs[p:p + ndr]; p += ndr
        g_bc = refs[p:p + ndb]; p += ndb
        prim_refs = refs[p:]

        def f(*d):
            return _as_tuple(fn(*d, *ar))

        outs, pullback = jax.vjp(f, *dr, *db)
        grads = pullback(tuple(c.astype(o.dtype) for c, o in zip(ct, outs)))
        for k in range(ndr):
            g_rows[k][...] = grads[k].astype(g_rows[k].dtype)
        if ndb:
            @pl.when(pl.program_id(0) == 0)
            def _():
                for r in g_bc:
                    r[...] = jnp.zeros_like(r)
            for k in range(ndb):
                g_bc[k][...] += grads[ndr + k]
        for r, o in zip(prim_refs, outs):
            r[...] = o.astype(r.dtype)

    in_specs = ([_row_spec(a.shape, ax, tr) for a, ax, tr in drows] + [_full_spec(b.shape) for b in dbc]
                + [_row_spec(a.shape, ax, tr) for a, ax, tr in arows]
                + [_row_spec(a.shape, ax, tr) for a, ax, tr in cts])
    out_shape = ([jax.ShapeDtypeStruct(a.shape, dt) for (a, _, _), dt in zip(drows, gdt)]
                 + [jax.ShapeDtypeStruct(b.shape, F32) for b in dbc])
    out_specs = ([_row_spec(a.shape, ax, tr) for a, ax, tr in drows] + [_full_spec(b.shape) for b in dbc])
    if primal:
        out_shape += [jax.ShapeDtypeStruct(a.shape, F32) for a, _, _ in cts]
        out_specs += [_row_spec(a.shape, ax, tr) for a, ax, tr in cts]
    return pl.pallas_call(
        body, name=name, out_shape=out_shape, grid=(steps,), in_specs=in_specs, out_specs=out_specs,
        compiler_params=_params(("arbitrary",)),
    )(*[a for a, _, _ in drows], *dbc, *[a for a, _, _ in arows], *[a for a, _, _ in cts])


def _rms_fn(x, g):
    return x * lax.rsqrt(jnp.mean(x * x, axis=-1, keepdims=True) + EPS) * g


def _rms_res_fn(x, g):
    return _rms_fn(x, g), x


def _s5_act_fn(ymm, u, d):
    return jax.nn.gelu(ymm + d * u)


def _glu_fn(z1, z2, b1, b2):
    return (z1 + b1) * jax.nn.sigmoid(z2 + b2)


def _outnorm_fn(ym, ys, yd, gm, gs, gd):
    return jnp.concatenate([_rms_fn(ym, gm), _rms_fn(ys, gs), _rms_fn(yd, gd)], axis=-1)


def _swiglu_fn(a, b):
    return jax.nn.silu(a) * b


def _loss_fn(x, g, tgt):
    err = _rms_fn(x, g) - tgt
    return 0.5 * jnp.mean(err * err, axis=-1, keepdims=True)


def _dil_mix_fn(o0, o1, o2, l0, l1, l2):
    m = jnp.maximum(jnp.maximum(l0, l1), l2)
    e0, e1, e2 = jnp.exp(l0 - m), jnp.exp(l1 - m), jnp.exp(l2 - m)
    s = e0 + e1 + e2
    return (e0 / s) * o0 + (e1 / s) * o1 + (e2 / s) * o2


def _s5_disc_fn(a_re, a_im, ldt, b_r, b_i):
    lr = jnp.minimum(a_re.reshape(1, SSM_N), -1e-4)
    li = a_im.reshape(1, SSM_N)
    dt = jnp.exp(ldt.reshape(1, 1))
    e = jnp.exp(lr * dt)
    ar = e * jnp.cos(li * dt)
    ai = e * jnp.sin(li * dt)
    nr, ni = ar - 1.0, ai
    den = lr * lr + li * li
    cr = (nr * lr + ni * li) / den
    ci = (ni * lr - nr * li) / den
    return ar.reshape(1, 1, SSM_N), ai.reshape(1, 1, SSM_N), cr * b_r - ci * b_i, cr * b_i + ci * b_r


def _adamw_fn(w, g, m, v):
    m = ADAM_B1 * m + (1.0 - ADAM_B1) * g
    v = ADAM_B2 * v + (1.0 - ADAM_B2) * jnp.square(g)
    m_hat = m / (1.0 - ADAM_B1 ** ADAM_STEP)
    v_hat = v / (1.0 - ADAM_B2 ** ADAM_STEP)
    delta = -ADAM_LR * (m_hat / (jnp.sqrt(v_hat) + ADAM_EPS) + ADAM_WD * w)
    return delta, m, v


def _rms_fwd(name, x, g):
    (h,) = _rows_fwd(name, _rms_fn, [_row(x)], [g], [(x.shape, BF16, -2, _row(x)[2])])
    return h


def _rms_bwd(name, x, g, dh, dres=None):
    if dres is None:
        dx, dg = _rows_vjp(name, _rms_fn, [_row(x)], [g], [], [_row(dh)])
    else:
        dx, dg = _rows_vjp(name, _rms_res_fn, [_row(x)], [g], [], [_row(dh), _row(dres)])
    return dx, dg


MLA_TQ = 256
MLA_EXT = 256


def _mla_fwd(q, k, v):
    tq = MLA_TQ

    def body(q_ref, k_ref, v_ref, o_ref, lse_ref):
        i = pl.program_id(1)
        q = q_ref[0]

        def rows_below(ext):
            s = _dot(q, k_ref[0, :ext, :], NT) * MLA_SCALE
            row = i * tq + lax.broadcasted_iota(jnp.int32, (tq, ext), 0)
            col = lax.broadcasted_iota(jnp.int32, (tq, ext), 1)
            s = jnp.where(row >= col, s, -jnp.inf)
            m = jnp.max(s, axis=-1, keepdims=True)
            p = jnp.exp(s - m)
            l = jnp.sum(p, axis=-1, keepdims=True)
            o_ref[...] = _dot((p / l).astype(BF16), v_ref[0, :ext, :])
            lse_ref[0] = m + jnp.log(l)

        for g in range(T // MLA_EXT):
            pl.when(i // (MLA_EXT // tq) == g)(lambda g=g: rows_below((g + 1) * MLA_EXT))

    return pl.pallas_call(
        body, name="mla_fwd",
        out_shape=[jax.ShapeDtypeStruct((T, H_MLA * VDIM), F32), jax.ShapeDtypeStruct((H_MLA, T, 1), F32)],
        grid=(H_MLA, T // tq),
        in_specs=[pl.BlockSpec((1, tq, QK), lambda h, i: (h, i, 0)),
                  pl.BlockSpec((1, T, QK), lambda h, i: (h, 0, 0)),
                  pl.BlockSpec((1, T, VDIM), lambda h, i: (h, 0, 0))],
        out_specs=[pl.BlockSpec((tq, VDIM), lambda h, i: (i, h)),
                   pl.BlockSpec((1, tq, 1), lambda h, i: (h, i, 0))],
        compiler_params=_params(("parallel", "parallel")),
    )(q, k, v)


def _mla_bwd(q, k, v, o, do, lse):
    tq = MLA_TQ

    def body(q_ref, k_ref, v_ref, o_ref, do_ref, lse_ref, dq_ref, dk_ref, dv_ref):
        i = pl.program_id(1)

        @pl.when(i == 0)
        def _():
            dk_ref[...] = jnp.zeros_like(dk_ref)
            dv_ref[...] = jnp.zeros_like(dv_ref)

        q, lse = q_ref[0], lse_ref[0]
        delta = jnp.sum(do_ref[...] * o_ref[...], axis=-1, keepdims=True)
        do = do_ref[...].astype(BF16)

        def rows_below(ext):
            k, v = k_ref[0, :ext, :], v_ref[0, :ext, :]
            s = _dot(q, k, NT) * MLA_SCALE
            row = i * tq + lax.broadcasted_iota(jnp.int32, (tq, ext), 0)
            col = lax.broadcasted_iota(jnp.int32, (tq, ext), 1)
            p = jnp.where(row >= col, jnp.exp(s - lse), 0.0)
            ds = (p * (_dot(do, v, NT) - delta) * MLA_SCALE).astype(BF16)
            dq_ref[0] = _dot(ds, k)
            dk_ref[0, :ext, :] += _dot(ds, q, TN)
            dv_ref[0, :ext, :] += _dot(p.astype(BF16), do, TN)

        for g in range(T // MLA_EXT):
            pl.when(i // (MLA_EXT // tq) == g)(lambda g=g: rows_below((g + 1) * MLA_EXT))

    return pl.pallas_call(
        body, name="mla_bwd",
        out_shape=[jax.ShapeDtypeStruct((H_MLA, T, QK), F32), jax.ShapeDtypeStruct((H_MLA, T, QK), F32),
                   jax.ShapeDtypeStruct((H_MLA, T, VDIM), F32)],
        grid=(H_MLA, T // tq),
        in_specs=[pl.BlockSpec((1, tq, QK), lambda h, i: (h, i, 0)),
                  pl.BlockSpec((1, T, QK), lambda h, i: (h, 0, 0)),
                  pl.BlockSpec((1, T, VDIM), lambda h, i: (h, 0, 0)),
                  pl.BlockSpec((tq, VDIM), lambda h, i: (i, h)),
                  pl.BlockSpec((tq, VDIM), lambda h, i: (i, h)),
                  pl.BlockSpec((1, tq, 1), lambda h, i: (h, i, 0))],
        out_specs=[pl.BlockSpec((1, tq, QK), lambda h, i: (h, i, 0)),
                   pl.BlockSpec((1, T, QK), lambda h, i: (h, 0, 0)),
                   pl.BlockSpec((1, T, VDIM), lambda h, i: (h, 0, 0))],
        compiler_params=_params(("parallel", "arbitrary")),
    )(q, k, v, o, do, lse)


NBLK = T // BLK


BAND_GL = 256
BAND_GH = BAND_GL // DIL_D
BAND_ROWS = BAND_GH * BLK
BAND_GROUPS = DIL_W // BAND_GL


def _band_masks():
    r = lax.broadcasted_iota(jnp.int32, (BAND_ROWS, BLK), 0) & (BLK - 1)
    j = lax.broadcasted_iota(jnp.int32, (BAND_ROWS, BLK), 1)
    return j <= r, j >= r


def _head_lanes():
    lane_head = lax.broadcasted_iota(jnp.int32, (1, BAND_GL), 1) // DIL_D
    return [lane_head == h for h in range(BAND_GH)]


def _stack_heads(x, lanes):
    return jnp.concatenate([jnp.where(m, x, jnp.zeros_like(x)) for m in lanes], axis=0)


def _merge_heads(xs, lanes):
    out = None
    for h, m in enumerate(lanes):
        part = jnp.where(m, xs[h * BLK:(h + 1) * BLK], 0.0)
        out = part if out is None else out + part
    return out


def _per_head(x, lanes):
    return jnp.concatenate([jnp.sum(jnp.where(m, x, 0.0), axis=-1, keepdims=True) for m in lanes], axis=0)


def _lane_group(ref, g):
    return ref[0, :, g * BAND_GL:(g + 1) * BAND_GL]


def _seq_start(p, i):
    per_seq = lax.shift_right_logical(jnp.int32(NBLK), 2 * p)
    return lax.rem(i, per_seq) == 0


def _band_fwd(qkv):
    def body(q_ref, kp_ref, kc_ref, vp_ref, vc_ref, o_ref, lse_ref):
        p, i = pl.program_id(0), pl.program_id(1)
        has_prev = jnp.logical_not(_seq_start(p, i))
        m_cur, m_prev = _band_masks()
        m_prev = m_prev & has_prev
        lanes = _head_lanes()
        for g in range(BAND_GROUPS):
            qs = _stack_heads(_lane_group(q_ref, g), lanes)
            s_c = jnp.where(m_cur, _dot(qs, _lane_group(kc_ref, g), NT) * DIL_SCALE, -jnp.inf)
            s_p = jnp.where(m_prev, _dot(qs, _lane_group(kp_ref, g), NT) * DIL_SCALE, -jnp.inf)
            m = jnp.maximum(jnp.max(s_c, axis=-1, keepdims=True), jnp.max(s_p, axis=-1, keepdims=True))
            e_c, e_p = jnp.exp(s_c - m), jnp.exp(s_p - m)
            l = jnp.sum(e_c, axis=-1, keepdims=True) + jnp.sum(e_p, axis=-1, keepdims=True)
            os = (_dot((e_p / l).astype(BF16), _lane_group(vp_ref, g))
                  + _dot((e_c / l).astype(BF16), _lane_group(vc_ref, g)))
            cols = slice(g * BAND_GL, (g + 1) * BAND_GL)
            o_ref[0, :, cols] = _merge_heads(os, lanes)
            lse_ref[0, :, cols] = _merge_heads(m + jnp.log(l), lanes)

    blk = (1, BLK, DIL_W)
    cur = lambda part: (lambda p, i: (p, i, part))
    prev = lambda part: (lambda p, i: (p, jnp.maximum(i - 1, 0), part))
    return pl.pallas_call(
        body, name="band_fwd",
        out_shape=[jax.ShapeDtypeStruct((3, T, DIL_W), F32)] * 2,
        grid=(3, NBLK),
        in_specs=[pl.BlockSpec(blk, cur(0)), pl.BlockSpec(blk, prev(1)), pl.BlockSpec(blk, cur(1)),
                  pl.BlockSpec(blk, prev(2)), pl.BlockSpec(blk, cur(2))],
        out_specs=[pl.BlockSpec(blk, cur(0)), pl.BlockSpec(blk, cur(0))],
        compiler_params=_params(("parallel", "parallel")),
    )(qkv, qkv, qkv, qkv, qkv)


def _band_bwd(qkv, o, lse, do, dlse):
    def body(qc_ref, qn_ref, kp_ref, kc_ref, vp_ref, vc_ref, oc_ref, on_ref, lc_ref, ln_ref,
             doc_ref, don_ref, dlc_ref, dln_ref, dqkv_ref):
        p, i = pl.program_id(0), pl.program_id(1)
        has_prev = jnp.logical_not(_seq_start(p, i))
        has_next = jnp.logical_not(_seq_start(p, i + 1)) & (i + 1 < NBLK)
        m_cur, m_prev = _band_masks()
        lanes = _head_lanes()

        def probs(qs, k, lse, mask):
            return jnp.where(mask, jnp.exp(_dot(qs, k, NT) * DIL_SCALE - lse), 0.0)

        def dscore(pr, dos, v, shift):
            return (pr * (_dot(dos, v, NT) + shift) * DIL_SCALE).astype(BF16)

        for g in range(BAND_GROUPS):
            grp = lambda ref: _lane_group(ref, g)
            kp, kc, vp, vc = grp(kp_ref), grp(kc_ref), grp(vp_ref), grp(vc_ref)
            qc, qn = _stack_heads(grp(qc_ref), lanes), _stack_heads(grp(qn_ref), lanes)
            doc, don = grp(doc_ref), grp(don_ref)
            lse_c = _per_head(grp(lc_ref), lanes) * (1.0 / DIL_D)
            lse_n = _per_head(grp(ln_ref), lanes) * (1.0 / DIL_D)
            sh_c = _per_head(grp(dlc_ref) - doc * grp(oc_ref), lanes)
            sh_n = _per_head(grp(dln_ref) - don * grp(on_ref), lanes)
            doc, don = _stack_heads(doc.astype(BF16), lanes), _stack_heads(don.astype(BF16), lanes)
            p_cc = probs(qc, kc, lse_c, m_cur)
            p_cp = probs(qc, kp, lse_c, m_prev & has_prev)
            p_nc = probs(qn, kc, lse_n, m_prev & has_next)
            ds_cc = dscore(p_cc, doc, vc, sh_c)
            ds_cp = dscore(p_cp, doc, vp, sh_c)
            ds_nc = dscore(p_nc, don, vc, sh_n)
            cols = lambda part: slice(part * DIL_W + g * BAND_GL, part * DIL_W + (g + 1) * BAND_GL)
            dqkv_ref[0, :, cols(0)] = _merge_heads(_dot(ds_cc, kc) + _dot(ds_cp, kp), lanes)
            dqkv_ref[0, :, cols(1)] = _dot(ds_cc, qc, TN) + _dot(ds_nc, qn, TN)
            dqkv_ref[0, :, cols(2)] = _dot(p_cc.astype(BF16), doc, TN) + _dot(p_nc.astype(BF16), don, TN)

    blk = (1, BLK, DIL_W)
    cur = lambda part: pl.BlockSpec(blk, lambda p, i: (p, i, part))
    prev = lambda part: pl.BlockSpec(blk, lambda p, i: (p, jnp.maximum(i - 1, 0), part))
    nxt = lambda part: pl.BlockSpec(blk, lambda p, i: (p, jnp.minimum(i + 1, NBLK - 1), part))
    w, wn = cur(0), nxt(0)
    return pl.pallas_call(
        body, name="band_bwd",
        out_shape=jax.ShapeDtypeStruct((3, T, 3 * DIL_W), F32),
        grid=(3, NBLK),
        in_specs=[cur(0), nxt(0), prev(1), cur(1), prev(2), cur(2), w, wn, w, wn, w, wn, w, wn],
        out_specs=pl.BlockSpec((1, BLK, 3 * DIL_W), lambda p, i: (p, i, 0)),
        compiler_params=_params(("parallel", "parallel")),
    )(qkv, qkv, qkv, qkv, qkv, qkv, o, o, lse, lse, do, do, dlse, dlse)


SCAN_TC = 256


def _scan_fwd(bu, ar, ai):
    tc, S = SCAN_TC, SSM_S

    def body(bu_ref, ar_ref, ai_ref, h_ref, h16_ref, cr_ref, ci_ref):
        @pl.when(pl.program_id(0) == 0)
        def _():
            cr_ref[...] = jnp.zeros_like(cr_ref)
            ci_ref[...] = jnp.zeros_like(ci_ref)

        a_r, a_i = ar_ref[...], ai_ref[...]

        def step(j, carry):
            hr, hi = carry
            for r in range(8):
                t = pl.multiple_of(j * 8, 8) + r
                br = bu_ref[pl.ds(t, 1), pl.ds(0, S)]
                bi = bu_ref[pl.ds(t, 1), pl.ds(S, S)]
                hr, hi = a_r * hr - a_i * hi + br, a_r * hi + a_i * hr + bi
                h_ref[pl.ds(t, 1), pl.ds(0, S)] = hr
                h_ref[pl.ds(t, 1), pl.ds(S, S)] = hi
            return hr, hi

        hr, hi = lax.fori_loop(0, tc // 8, step, (cr_ref[...], ci_ref[...]))
        cr_ref[...] = hr
        ci_ref[...] = hi
        h16_ref[...] = h_ref[...].astype(BF16)

    return pl.pallas_call(
        body, name="s5_scan_fwd",
        out_shape=[jax.ShapeDtypeStruct((T, 2 * S), F32), jax.ShapeDtypeStruct((T, 2 * S), BF16)],
        grid=(T // tc,),
        in_specs=[pl.BlockSpec((tc, 2 * S), lambda i: (i, 0)), _full_spec((1, S)), _full_spec((1, S))],
        out_specs=[pl.BlockSpec((tc, 2 * S), lambda i: (i, 0)), pl.BlockSpec((tc, 2 * S), lambda i: (i, 0))],
        scratch_shapes=[pltpu.VMEM((1, S), F32), pltpu.VMEM((1, S), F32)],
        compiler_params=_params(("arbitrary",)),
    )(bu, ar, ai)


def _scan_bwd(dh, h, ar, ai):
    tc, S = SCAN_TC, SSM_S
    nc = T // tc

    def body(dh_ref, h_ref, hp_ref, ar_ref, ai_ref, g16_ref, dar_ref, dai_ref, cr_ref, ci_ref, g_ref):
        i = pl.program_id(0)

        @pl.when(i == 0)
        def _():
            cr_ref[...] = jnp.zeros_like(cr_ref)
            ci_ref[...] = jnp.zeros_like(ci_ref)
            dar_ref[...] = jnp.zeros_like(dar_ref)
            dai_ref[...] = jnp.zeros_like(dai_ref)

        a_r, a_i = ar_ref[...], ai_ref[...]
        first_chunk = (i == nc - 1)
        edge = jnp.where(first_chunk, 0.0, 1.0)
        hpr = hp_ref[pl.ds(7, 1), pl.ds(0, S)] * edge
        hpi = hp_ref[pl.ds(7, 1), pl.ds(S, S)] * edge

        def step(jj, carry):
            gr, gi, dar, dai = carry
            j = tc // 8 - 1 - jj
            for r in range(7, -1, -1):
                t = pl.multiple_of(j * 8, 8) + r
                tp = jnp.maximum(t - 1, 0)
                inside = t > 0
                pr = jnp.where(inside, h_ref[pl.ds(tp, 1), pl.ds(0, S)], hpr)
                pi = jnp.where(inside, h_ref[pl.ds(tp, 1), pl.ds(S, S)], hpi)
                gr, gi = (dh_ref[pl.ds(t, 1), pl.ds(0, S)] + a_r * gr + a_i * gi,
                          dh_ref[pl.ds(t, 1), pl.ds(S, S)] + a_r * gi - a_i * gr)
                g_ref[pl.ds(t, 1), pl.ds(0, S)] = gr
                g_ref[pl.ds(t, 1), pl.ds(S, S)] = gi
                dar = dar + gr * pr + gi * pi
                dai = dai + gi * pr - gr * pi
            return gr, gi, dar, dai

        zero = jnp.zeros((1, S), F32)
        gr, gi, dar, dai = lax.fori_loop(0, tc // 8, step, (cr_ref[...], ci_ref[...], zero, zero))
        cr_ref[...] = gr
        ci_ref[...] = gi
        dar_ref[...] += dar
        dai_ref[...] += dai
        g16_ref[...] = g_ref[...].astype(BF16)

    rev = lambda i: (nc - 1 - i, 0)
    before = lambda i: (jnp.maximum((nc - 1 - i) * (tc // 8) - 1, 0), 0)
    return pl.pallas_call(
        body, name="s5_scan_bwd",
        out_shape=[jax.ShapeDtypeStruct((T, 2 * S), BF16), jax.ShapeDtypeStruct((1, S), F32),
                   jax.ShapeDtypeStruct((1, S), F32)],
        grid=(nc,),
        in_specs=[pl.BlockSpec((tc, 2 * S), rev), pl.BlockSpec((tc, 2 * S), rev), pl.BlockSpec((8, 2 * S), before),
                  _full_spec((1, S)), _full_spec((1, S))],
        out_specs=[pl.BlockSpec((tc, 2 * S), rev), _full_spec((1, S)), _full_spec((1, S))],
        scratch_shapes=[pltpu.VMEM((1, S), F32), pltpu.VMEM((1, S), F32), pltpu.VMEM((tc, 2 * S), F32)],
        compiler_params=_params(("arbitrary",)),
    )(dh, h, h, ar, ai)


def _sum_rows(name, x):
    def body(x_ref, o_ref):
        o_ref[...] = jnp.sum(x_ref[...], axis=0, keepdims=True)

    return pl.pallas_call(body, name=name, out_shape=jax.ShapeDtypeStruct((1, 1), F32),
                          in_specs=[_full_spec(x.shape)], out_specs=_full_spec((1, 1)), grid=(1,))(x)


ANY = pl.BlockSpec(memory_space=pl.ANY)


def _place():
    return lax.axis_index("x"), lax.axis_index("y"), lax.axis_index("c")


HBM = pl.BlockSpec(memory_space=pltpu.HBM)
SEM = pl.BlockSpec(memory_space=pltpu.SEMAPHORE)
DATAFLOW = pltpu.SideEffectType.DATAFLOW_SIDE_EFFECTING


def _hbm(a):
    return pltpu.with_memory_space_constraint(a, pltpu.HBM)


def _split_start(name, srcs, lands, ncopies, plan, after):
    ns, nl = len(srcs), len(lands)

    def body(*refs):
        send_sems, recv_sems = refs[ns + nl + 1], refs[ns + nl + 2]
        token = refs[-1]
        for k, (src, dst, peer, _) in enumerate(plan(refs[:ns], refs[ns:ns + nl])):
            pltpu.make_async_remote_copy(src_ref=src, dst_ref=dst, send_sem=send_sems.at[k], recv_sem=recv_sems.at[k],
                                         device_id=peer, device_id_type=MESH).start()
        token[...] = jnp.zeros_like(token)

    out = pl.pallas_call(
        body, name=name,
        out_shape=(pltpu.SemaphoreType.DMA((ncopies,)), pltpu.SemaphoreType.DMA((ncopies,)),
                   *[pltpu.HBM(a.shape, a.dtype) for a in srcs], *[pltpu.HBM(a.shape, a.dtype) for a in lands],
                   jax.ShapeDtypeStruct((8, LANES), F32)),
        in_specs=[HBM] * (ns + nl) + [ANY],
        out_specs=(SEM, SEM, *[HBM] * (ns + nl), pl.BlockSpec(memory_space=pltpu.VMEM)),
        input_output_aliases={i: 2 + i for i in range(ns + nl)},
        compiler_params=pltpu.CompilerParams(has_side_effects=DATAFLOW),
    )(*[_hbm(a) for a in srcs], *[_hbm(a) for a in lands], after)
    return out[0], out[1], list(out[2:2 + ns]), list(out[2 + ns:2 + ns + nl]), out[-1]


def _split_wait(name, handle, ncopies, plan, after):
    send_sems, recv_sems, srcs, lands, _ = handle
    ns, nl = len(srcs), len(lands)

    def body(*refs):
        s_sems, r_sems = refs[ns + nl], refs[ns + nl + 1]
        for k, (src, dst, peer, mine) in enumerate(plan(refs[:ns], refs[ns:ns + nl])):
            pltpu.make_async_remote_copy(src_ref=src, dst_ref=dst, send_sem=s_sems.at[k], recv_sem=r_sems.at[k],
                                         device_id=peer, device_id_type=MESH).wait_send()
            pltpu.make_async_remote_copy(src_ref=src, dst_ref=mine, send_sem=s_sems.at[k], recv_sem=r_sems.at[k],
                                         device_id=peer, device_id_type=MESH).wait_recv()

    out = pl.pallas_call(
        body, name=name,
        out_shape=(*[pltpu.HBM(a.shape, a.dtype) for a in srcs], *[pltpu.HBM(a.shape, a.dtype) for a in lands]),
        in_specs=[HBM] * (ns + nl) + [SEM, SEM, ANY],
        out_specs=tuple([HBM] * (ns + nl)),
        input_output_aliases={i: i for i in range(ns + nl)},
        compiler_params=pltpu.CompilerParams(has_side_effects=DATAFLOW),
    )(*srcs, *lands, send_sems, recv_sems, after)
    return list(out[:ns]), list(out[ns:])


def _slot(px, py, pc):
    return 4 * px + 2 * py + pc


def _gather_plan(xs, lands):
    x, y, c = _place()
    peers = [(x, y, 1 - c), (1 - x, y, c), (x, 1 - y, c), (1 - x, 1 - y, c)]
    return [(xs[t], lands[t].at[_slot(x, y, c)], peer, lands[t].at[_slot(*peer)])
            for t in range(len(xs)) for peer in peers]


def _gather_start(name, shards, after):
    lands = [lax.empty((N_DEV,) + s.shape, s.dtype) for s in shards]
    return _split_start(name, shards, lands, 4 * len(shards), _gather_plan, after)


def _pass_on_plan(_, lands):
    x, y, c = _place()
    blocks = [((1 - x, y, c), (1 - x, y, 1 - c)), ((x, 1 - y, c), (x, 1 - y, 1 - c)),
              ((1 - x, 1 - y, c), (1 - x, 1 - y, 1 - c)), ((x, y, 1 - c), (x, y, c))]
    return [(lands[t].at[_slot(*out)], lands[t].at[_slot(*out)], (x, y, 1 - c), lands[t].at[_slot(*back)])
            for t in range(len(lands)) for out, back in blocks]


def _gather_pass_on(name, handle, after):
    n = len(handle[2])
    _, lands = _split_wait(name + "_wait", handle, 4 * n, _gather_plan, after)
    return _split_start(name + "_pass_on", [], lands, 4 * n, _pass_on_plan, after)


def _gather_finish(name, handle, after):
    n = len(handle[3])
    _, lands = _split_wait(name + "_done", handle, 4 * n, _pass_on_plan, after)
    return lands


def _sibling_plan(ps, gots):
    x, y, c = _place()
    return [(ps[t].at[j, 1 - c], gots[t].at[j], (x, y, 1 - c), gots[t].at[j]) for t in range(len(ps)) for j in range(4)]


def _chips_plan(ss, gots):
    x, y, c = _place()
    chips = [(1 - x, y), (x, 1 - y), (1 - x, 1 - y)]
    return [(ss[t].at[2 * px + py], gots[t].at[k], (px, py, c), gots[t].at[k])
            for t in range(len(ss)) for k, (px, py) in enumerate(chips)]


BF16_ROWS = 16


def _shard_tiles(R, C, block_bytes):
    tr = _tile(R, max(BF16_ROWS, min(1024, block_bytes // (C * 4))), BF16_ROWS)
    tc = C if tr * C * 4 <= 2 * block_bytes else _tile(C, max(LANES, block_bytes // (tr * 4)))
    return tr, tc


def _pair_sum(name, p4, got, core):
    _, _, R, C = p4.shape
    tr, tc = _shard_tiles(R, C, ROW_BLOCK_BYTES)

    def body(core_ref, p_ref, g_ref, o_ref):
        o_ref[...] = (p_ref[:, 0] + g_ref[...]).astype(o_ref.dtype)

    return pl.pallas_call(
        body, name=name, out_shape=jax.ShapeDtypeStruct((4, R, C), BF16),
        grid_spec=pltpu.PrefetchScalarGridSpec(
            num_scalar_prefetch=1, grid=(4, R // tr, C // tc),
            in_specs=[pl.BlockSpec((1, 1, tr, tc), lambda j, i, k, core: (j, core[0], i, k)),
                      pl.BlockSpec((1, tr, tc), lambda j, i, k, core: (j, i, k))],
            out_specs=pl.BlockSpec((1, tr, tc), lambda j, i, k, core: (j, i, k))),
        compiler_params=_params(("parallel", "parallel", "parallel")),
    )(core, p4, got)


def _sum_devices(name, g8):
    _, R, C = g8.shape
    tr = _tile(R, SMALL_ROW_ALIGN, 8)

    def body(g_ref, o_ref):
        acc = g_ref[0]
        for d in range(1, N_DEV):
            acc = acc + g_ref[d]
        o_ref[...] = acc

    return pl.pallas_call(
        body, name=name, out_shape=jax.ShapeDtypeStruct((R, C), F32), grid=(R // tr,),
        in_specs=[pl.BlockSpec((N_DEV, tr, C), lambda i: (0, i, 0))], out_specs=pl.BlockSpec((tr, C), lambda i: (i, 0)),
        compiler_params=_params(("parallel",)),
    )(g8)


def _adamw_shard(name, layer, w, m, v, s4, got, chip, bufs):
    _, R, C = w.shape
    tr, tc = _shard_tiles(R, C, ROW_BLOCK_BYTES // 2)

    def body(chip_ref, w_ref, m_ref, v_ref, s_ref, g_ref, b0, b1, b2, b3, og, od, om, ov):
        f = lambda a: a.astype(F32)
        g = ((f(s_ref[0]) + f(g_ref[0])) + f(g_ref[1])) + f(g_ref[2])
        d, nm, nv = _adamw_fn(w_ref[0], g, m_ref[0], v_ref[0])
        og[0], od[0], om[0], ov[0] = g, d, nm, nv

    lay = pl.BlockSpec((1, tr, tc), lambda i, k, chip: (layer, i, k))
    if bufs is None:
        bufs = [lax.empty(w.shape, F32) for _ in range(4)]
    return pl.pallas_call(
        body, name=name, out_shape=[jax.ShapeDtypeStruct(w.shape, F32)] * 4,
        grid_spec=pltpu.PrefetchScalarGridSpec(
            num_scalar_prefetch=1, grid=(R // tr, C // tc),
            in_specs=[lay, lay, lay, pl.BlockSpec((1, tr, tc), lambda i, k, chip: (chip[0], i, k)),
                      pl.BlockSpec((3, tr, tc), lambda i, k, chip: (0, i, k)), ANY, ANY, ANY, ANY],
            out_specs=[lay] * 4),
        input_output_aliases={6: 0, 7: 1, 8: 2, 9: 3},
        compiler_params=_params(("parallel", "parallel")),
    )(chip, w, m, v, s4, got, *bufs)


def _adamw(name, wt, g, m, v):
    shape = wt.shape
    two = (lambda a: a.reshape(1, -1)) if wt.ndim == 1 else (lambda a: a.reshape(-1, shape[-1]))
    w2, g2, m2, v2 = two(wt), two(g), two(m), two(v)
    tr = _row(w2, None, 0)[2]
    outs = [(w2.shape, F32, 0, tr)] * 3
    d, nm, nv = _rows_fwd(name, _adamw_fn, [_row(a, tr, 0) for a in (w2, g2, m2, v2)], [], outs)
    return d.reshape(shape), nm.reshape(shape), nv.reshape(shape)


def _lane_tiling():
    return (jnp.arange(SSM_N)[:, None] == (jnp.arange(SSM_S) % SSM_N)[None, :]).astype(BF16)


def _own_block():
    r = lax.broadcasted_iota(jnp.int32, (SSM_G * SSM_P, SSM_S), 0) // SSM_P
    c = lax.broadcasted_iota(jnp.int32, (SSM_G * SSM_P, SSM_S), 1) // SSM_N
    return r == c


def _bd_build(name, v_re, v_im, sign, tiling):
    def body(r_ref, i_ref, t_ref, o_ref):
        own = _own_block()
        o_ref[:, :SSM_S] = jnp.where(own, _dot(r_ref[...].astype(BF16), t_ref[...]), 0.0).astype(BF16)
        o_ref[:, SSM_S:] = jnp.where(own, sign * _dot(i_ref[...].astype(BF16), t_ref[...]), 0.0).astype(BF16)

    rows = SSM_G * SSM_P
    return pl.pallas_call(
        body, name=name, out_shape=jax.ShapeDtypeStruct((rows, 2 * SSM_S), BF16), grid=(1,),
        in_specs=[_full_spec((rows, SSM_N))] * 2 + [_full_spec((SSM_N, SSM_S))], out_specs=_full_spec((rows, 2 * SSM_S)),
        compiler_params=_params(("arbitrary",)),
    )(v_re, v_im, tiling)


def _bd_extract(name, m, sign, tiling):
    def body(m_ref, t_ref, r_ref, i_ref):
        own, t = _own_block(), t_ref[...]

        def pick(x):
            x = jnp.where(own, x, 0.0)
            hi = x.astype(BF16)
            rest = x - hi.astype(F32)
            mid = rest.astype(BF16)
            lo = (rest - mid.astype(F32)).astype(BF16)
            return _dot(hi, t, NT) + _dot(mid, t, NT) + _dot(lo, t, NT)

        r_ref[...] = pick(m_ref[:, :SSM_S])
        i_ref[...] = sign * pick(m_ref[:, SSM_S:])

    rows = SSM_G * SSM_P
    return pl.pallas_call(
        body, name=name, out_shape=[jax.ShapeDtypeStruct((rows, SSM_N), F32)] * 2, grid=(1,),
        in_specs=[_full_spec((rows, 2 * SSM_S)), _full_spec((SSM_N, SSM_S))], out_specs=[_full_spec((rows, SSM_N))] * 2,
        compiler_params=_params(("arbitrary",)),
    )(m, tiling)


def _fold(a, dil):
    if dil == 1:
        return a
    return a.reshape((T // dil, dil) + a.shape[1:]).swapaxes(0, 1).reshape(a.shape)


def _unfold(a, dil):
    if dil == 1:
        return a
    return a.reshape((dil, T // dil) + a.shape[1:]).swapaxes(0, 1).reshape(a.shape)


DILS = (1, 4, 16)


def _fold3(parts):
    parts = [parts] * 3 if not isinstance(parts, (list, tuple)) else parts
    return jnp.stack([_fold(a, d) for a, d in zip(parts, DILS)])


def _unfold3(a):
    return [_unfold(a[p], d) for p, d in enumerate(DILS)]


def _rope_tables():
    half = ROPE // 2
    inv_freq = ROPE_THETA ** (-jnp.arange(half, dtype=F32) / half)
    ang = jnp.arange(T).astype(F32)[:, None] * inv_freq[None, :]
    i, j = jnp.arange(ROPE)[:, None], jnp.arange(ROPE)[None, :]
    rot = jnp.where(i == j + half, -1.0, jnp.where(i + half == j, 1.0, 0.0)).astype(F32)
    return jnp.tile(jnp.cos(ang), (1, 2)), jnp.tile(jnp.sin(ang), (1, 2)), rot


def _rot_half(x, rot):
    return _dot(x.reshape(-1, ROPE), rot, prec=HI).reshape(x.shape)


def _mla_pack_fn(q, kv, k_rope, cos, sin, rot):
    rope = lambda x: x * cos + _rot_half(x, rot) * sin
    q_out = jnp.concatenate([q[:, :, :NOPE], rope(q[:, :, NOPE:])], axis=-1)
    k_pe = jnp.broadcast_to(rope(k_rope)[None], (H_MLA,) + k_rope.shape)
    return q_out, jnp.concatenate([kv[:, :, :NOPE], k_pe], axis=-1), kv[:, :, NOPE:]


def _mla_unpack_fn(dq, dk, dv, cos, sin, rot):
    unrope = lambda g: g * cos - _rot_half(g * sin, rot)
    dq_out = jnp.concatenate([dq[:, :, :NOPE], unrope(dq[:, :, NOPE:])], axis=-1)
    dk_rope = unrope(jnp.sum(dk[:, :, NOPE:], axis=0))
    return dq_out, jnp.concatenate([dk[:, :, :NOPE], dv], axis=-1), dk_rope


def _flat(w8):
    return w8.reshape(-1, w8.shape[-1])


def _blocks(m):
    return m.reshape(N_DEV, -1, m.shape[-1])


def _behind(a, tok):
    return a if tok is None else a + tok


def _mixer_fwd_in(x, w, sp, rope):
    s = {}
    s['x'] = x
    h = _rms_fwd("rms_mix", x, sp['g_mix'])
    proj = _mm("mm_in", h, _flat(w['w_in']), 'nt')
    offs = np.cumsum((0,) + IN_SPLITS)
    c_q, c_kv, k_rope, u = [proj[:, offs[i]:offs[i + 1]] for i in range(4)]
    qkv_d = proj[:, offs[4]:]
    s.update(h=h, c_q=c_q, c_kv=c_kv, u=u)

    cqn = _rms_fwd("rms_q", c_q, sp['g_q'])
    ckvn = _rms_fwd("rms_kv", c_kv, sp['g_kv'])
    q8 = _mm("mm_uq", cqn, w['w_uq'], 'nt', bb='r', ob='c')
    kv8 = _mm("mm_ukv", ckvn, w['w_ukv'], 'nn', bb='c', ob='c')
    cos, sin, rot = rope
    tr = MLA_TQ
    qh, kh, vh = _rows_fwd(
        "mla_pack", _mla_pack_fn,
        [_row(q8, tr, 1), _row(kv8, tr, 1), _row(k_rope, tr, 0), _row(cos, tr, 0), _row(sin, tr, 0)], [rot],
        [((H_MLA, T, QK), BF16, 1, tr), ((H_MLA, T, QK), BF16, 1, tr), ((H_MLA, T, VDIM), BF16, 1, tr)])
    y_mla, lse_mla = _mla_fwd(qh, kh, vh)
    s.update(cqn=cqn, ckvn=ckvn, qh=qh, kh=kh, vh=vh, lse_mla=lse_mla, y_mla=y_mla, qkv_d=qkv_d)
    return s


def _mixer_fwd_out(s, w, sp, tok=None, after_ssm=None):
    x, u, y_mla, qkv_d = s['x'], s['u'], s['y_mla'], s['qkv_d']
    a3 = lambda n: sp[n].reshape(SSM_G, 1, SSM_N)
    b2 = lambda n: sp[n].transpose(0, 2, 1).reshape(SSM_G * SSM_P, SSM_N)
    disc_rows = [_row(a3('a_re'), 1, 0), _row(a3('a_im'), 1, 0), _row(sp['log_dt'].reshape(SSM_G, 1, 1), 1, 0),
                 _row(b2('b_re'), SSM_P, 0), _row(b2('b_im'), SSM_P, 0)]
    abr, abi, bbr, bbi = _rows_fwd(
        "s5_disc", _s5_disc_fn, disc_rows, [],
        [((SSM_G, 1, SSM_N), F32, 0, 1), ((SSM_G, 1, SSM_N), F32, 0, 1),
         ((SSM_G * SSM_P, SSM_N), F32, 0, SSM_P), ((SSM_G * SSM_P, SSM_N), F32, 0, SSM_P)])
    ar, ai = abr.reshape(1, SSM_S), abi.reshape(1, SSM_S)
    tiling = _lane_tiling()
    b_mat = _bd_build("s5_b_matrix", bbr, bbi, 1.0, tiling)
    c2 = lambda n: sp[n].reshape(SSM_G * SSM_P, SSM_N)
    c_mat = _bd_build("s5_c_matrix", c2('c_re'), c2('c_im'), -1.0, tiling)
    u16 = _behind(u, tok).astype(BF16)
    bu = _mm("mm_s5_b", u16, b_mat, 'nn')
    hst, hst16 = _scan_fwd(bu, ar, ai)
    ymm = _mm("mm_s5_c", hst16, c_mat, 'nt')
    d_row = sp['d_skip'].reshape(1, SSM_W)
    (yg,) = _rows_fwd("s5_act", _s5_act_fn, [_row(ymm), _row(u)], [d_row], [((T, SSM_W), BF16, -2, _row(u)[2])])
    z = _mm("mm_glu", yg, w['w_glu'], 'nn', bb='c')
    glu_rows = [_row(z[:, :SSM_W]), _row(z[:, SSM_W:])]
    glu_b = [sp['b_glu'][:SSM_W].reshape(1, -1), sp['b_glu'][SSM_W:].reshape(1, -1)]
    (y_ssm,) = _rows_fwd("s5_glu", _glu_fn, glu_rows, glu_b, [((T, SSM_W), F32, -2, glu_rows[0][2])])
    if after_ssm is not None:
        qkv_d = _behind(qkv_d, after_ssm(y_ssm))
    s.update(disc_rows=disc_rows, ar=ar, ai=ai, b_mat=b_mat, c_mat=c_mat, hst=hst, hst16=hst16, u16=u16, ymm=ymm,
             d_row=d_row, yg=yg,
             glu_rows=glu_rows, glu_b=glu_b)

    qkv_f = _fold3(qkv_d).astype(BF16)
    o_f, lse_f = _band_fwd(qkv_f)
    mix_rows = [_row(a) for a in _unfold3(o_f) + _unfold3(lse_f)]
    (y_dil,) = _rows_fwd("dil_mix", _dil_mix_fn, mix_rows, [], [((T, DIL_W), F32, -2, mix_rows[0][2])])
    s.update(qkv_f=qkv_f, o_f=o_f, lse_f=lse_f, mix_rows=mix_rows)

    gm, gs, gd = sp['g_out_mla'].reshape(1, -1), sp['g_out_ssm'].reshape(1, -1), sp['g_out_dil'].reshape(1, -1)
    on_rows = [_row(y_mla), _row(y_ssm), _row(y_dil)]
    (ycat,) = _rows_fwd("out_norm", _outnorm_fn, on_rows, [gm, gs, gd], [((T, D), BF16, -2, on_rows[0][2])])
    x1_ = _mm("mm_o", ycat, _flat(w['w_o']), 'nn', res=x)
    s.update(on_rows=on_rows, on_g=[gm, gs, gd], ycat=ycat, x1=x1_)
    del s['qkv_d']
    return x1_


def _ffn_fwd(x1_, w, sp, s, tok=None):
    h2 = _rms_fwd("rms_ffn", x1_, _behind(sp['g_ffn'], tok))
    ga = _mm("mm_gate", h2, _flat(w['w_gate']), 'nt')
    gb, zf = _mm("mm_up", h2, _flat(w['w_up']), 'nt',
                 epilogue=(lambda up, gate: (up, _swiglu_fn(gate, up)), [ga], [F32, BF16]))
    x2_ = _mm("mm_down", zf, _flat(w['w_down']), 'nn', res=x1_)
    s.update(h2=h2, ga=ga, gb=gb, zf=zf)
    return x2_


def _b16(a):
    return a.astype(BF16)


def _ffn_bwd(dx2, s, w, sp, tok=None):
    gw, gs_ = {}, {}
    b16 = _b16
    dx2b = b16(_behind(dx2, tok))
    def gate_pullback(dz, gate, up):
        return jax.vjp(_swiglu_fn, gate, up)[1](dz)

    dga, dgb = _mm("mm_down_dx", dx2b, _flat(w['w_down']), 'nt', epilogue=(gate_pullback, [s['ga'], s['gb']], [BF16, BF16]))
    gw['w_down'] = _blocks(_mm("mm_down_dw", s['zf'], dx2b, 'tn'))
    gw['w_gate'] = _blocks(_mm("mm_gate_dw", dga, s['h2'], 'tn'))
    gw['w_up'] = _blocks(_mm("mm_up_dw", dgb, s['h2'], 'tn'))
    dh2 = _mm("mm_up_dx", dgb, _flat(w['w_up']), 'nn', res=_mm("mm_gate_dx", dga, _flat(w['w_gate']), 'nn'))
    dx1, gs_['g_ffn'] = _rms_bwd("rms_ffn_bwd", s['x1'], sp['g_ffn'], dh2, dx2)
    return dx1, gw, gs_


def _mixer_bwd_out(dx1, s, w, sp, tok=None):
    gw, gs_ = {}, {}
    b16 = _b16
    dx1b = b16(_behind(dx1, tok))
    dycat = _mm("mm_o_dx", dx1b, _flat(w['w_o']), 'nt')
    gw['w_o'] = _blocks(_mm("mm_o_dw", s['ycat'], dx1b, 'tn'))
    dy_mla, dy_ssm, dy_dil, gs_['g_out_mla'], gs_['g_out_ssm'], gs_['g_out_dil'] = _rows_vjp(
        "out_norm_bwd", _outnorm_fn, s['on_rows'], s['on_g'], [], [_row(dycat)])

    dmix = _rows_vjp("dil_mix_bwd", _dil_mix_fn, s['mix_rows'], [], [], [_row(dy_dil)])
    dqkv_f = _band_bwd(s['qkv_f'], s['o_f'], s['lse_f'], _fold3(dmix[:3]), _fold3(dmix[3:]))
    dqkv_d = sum(_unfold3(dqkv_f))

    dz1, dz2, db1, db2 = _rows_vjp("s5_glu_bwd", _glu_fn, s['glu_rows'], s['glu_b'], [], [_row(dy_ssm)])
    gs_['b_glu'] = jnp.concatenate([db1, db2], axis=1)
    dzb = b16(jnp.concatenate([dz1, dz2], axis=1))
    dyg = _mm("mm_glu_dx", dzb, w['w_glu'], 'nt', bb='c')
    gw['w_glu'] = _mm("mm_glu_dw", s['yg'], dzb, 'tn', ob='c')
    dymm, du_act, dd = _rows_vjp("s5_act_bwd", _s5_act_fn, [_row(s['ymm']), _row(s['u'])], [s['d_row']], [], [_row(dyg)],
                                 grad_dtypes=[BF16, F32])
    gs_['d_skip'] = dd
    dhst = _mm("mm_s5_c_dx", dymm, s['c_mat'], 'nn')
    dc_mat = _mm("mm_s5_c_dw", dymm, s['hst16'], 'tn')
    g, dar, dai = _scan_bwd(dhst, s['hst'], s['ar'], s['ai'])
    du = _mm("mm_s5_b_dx", g, s['b_mat'], 'nt', res=du_act)
    db_mat = _mm("mm_s5_b_dw", s['u16'], g, 'tn')
    tiling = _lane_tiling()
    gs_['c_re'], gs_['c_im'] = _bd_extract("s5_c_blocks", dc_mat, -1.0, tiling)
    dbbr, dbbi = _bd_extract("s5_b_blocks", db_mat, 1.0, tiling)
    disc_cts = [_row(dar.reshape(SSM_G, 1, SSM_N), 1, 0), _row(dai.reshape(SSM_G, 1, SSM_N), 1, 0),
                _row(dbbr, SSM_P, 0), _row(dbbi, SSM_P, 0)]
    da_re, da_im, dldt, db_r, db_i = _rows_vjp("s5_disc_bwd", _s5_disc_fn, s['disc_rows'], [], [], disc_cts)
    gs_['a_re'], gs_['a_im'], gs_['log_dt'] = da_re, da_im, dldt
    unb = lambda a: a.reshape(SSM_G, SSM_P, SSM_N).transpose(0, 2, 1)
    gs_['b_re'], gs_['b_im'] = unb(db_r), unb(db_i)
    return (dy_mla, du, dqkv_d), gw, gs_


def _mixer_bwd_in(cts, dx1, s, w, sp, rope, tok=None):
    gw, gs_ = {}, {}
    b16 = _b16
    dy_mla, du, dqkv_d = cts
    dqh, dkh, dvh = _mla_bwd(s['qh'], s['kh'], s['vh'], s['y_mla'], dy_mla, _behind(s['lse_mla'], tok))
    cos, sin, rot = rope
    tr = MLA_TQ
    dq8, dkv8, dk_rope = _rows_fwd(
        "mla_unpack", _mla_unpack_fn,
        [_row(dqh, tr, 1), _row(dkh, tr, 1), _row(dvh, tr, 1), _row(cos, tr, 0), _row(sin, tr, 0)], [rot],
        [((H_MLA, T, QK), BF16, 1, tr), ((H_MLA, T, NOPE + VDIM), BF16, 1, tr), ((T, ROPE), F32, 0, tr)])
    dcqn = _mm("mm_uq_dx", dq8, w['w_uq'], 'nn', ab='c', bb='r')
    gw['w_uq'] = _mm("mm_uq_dw", dq8, s['cqn'], 'tn', ab='c', ob='r')
    dckvn = _mm("mm_ukv_dx", dkv8, w['w_ukv'], 'nt', ab='c', bb='c')
    gw['w_ukv'] = _mm("mm_ukv_dw", s['ckvn'], dkv8, 'tn', bb='c', ob='c')
    dc_q, gs_['g_q'] = _rms_bwd("rms_q_bwd", s['c_q'], sp['g_q'], dcqn)
    dc_kv, gs_['g_kv'] = _rms_bwd("rms_kv_bwd", s['c_kv'], sp['g_kv'], dckvn)

    dproj = b16(jnp.concatenate([dc_q, dc_kv, dk_rope, du, dqkv_d], axis=1))
    dh = _mm("mm_in_dx", dproj, _flat(w['w_in']), 'nn')
    gw['w_in'] = _blocks(_mm("mm_in_dw", dproj, s['h'], 'tn'))
    dx, gs_['g_mix'] = _rms_bwd("rms_mix_bwd", s['x'], sp['g_mix'], dh, dx1)
    return dx, gw, gs_


def kernel(x, g_mix, w_in, g_q, w_uq, g_kv, w_ukv, a_re, a_im, b_re, b_im, c_re, c_im, d_skip, log_dt, w_glu, b_glu, g_out_mla, g_out_ssm, g_out_dil, w_o, g_ffn, w_gate, w_up, w_down, g_final, loss_target, m_g_mix, m_w_in, m_g_q, m_w_uq, m_g_kv, m_w_ukv, m_a_re, m_a_im, m_b_re, m_b_im, m_c_re, m_c_im, m_d_skip, m_log_dt, m_w_glu, m_b_glu, m_g_out_mla, m_g_out_ssm, m_g_out_dil, m_w_o, m_g_ffn, m_w_gate, m_w_up, m_w_down, m_g_final, v_g_mix, v_w_in, v_g_q, v_w_uq, v_g_kv, v_w_ukv, v_a_re, v_a_im, v_b_re, v_b_im, v_c_re, v_c_im, v_d_skip, v_log_dt, v_w_glu, v_b_glu, v_g_out_mla, v_g_out_ssm, v_g_out_dil, v_w_o, v_g_ffn, v_w_gate, v_w_up, v_w_down, v_g_final):
    W = dict(zip(PARAMS, (g_mix, w_in, g_q, w_uq, g_kv, w_ukv, a_re, a_im, b_re, b_im, c_re, c_im, d_skip, log_dt,
                          w_glu, b_glu, g_out_mla, g_out_ssm, g_out_dil, w_o, g_ffn, w_gate, w_up, w_down, g_final)))
    M = dict(zip(PARAMS, (m_g_mix, m_w_in, m_g_q, m_w_uq, m_g_kv, m_w_ukv, m_a_re, m_a_im, m_b_re, m_b_im, m_c_re,
                          m_c_im, m_d_skip, m_log_dt, m_w_glu, m_b_glu, m_g_out_mla, m_g_out_ssm, m_g_out_dil, m_w_o,
                          m_g_ffn, m_w_gate, m_w_up, m_w_down, m_g_final)))
    V = dict(zip(PARAMS, (v_g_mix, v_w_in, v_g_q, v_w_uq, v_g_kv, v_w_ukv, v_a_re, v_a_im, v_b_re, v_b_im, v_c_re,
                          v_c_im, v_d_skip, v_log_dt, v_w_glu, v_b_glu, v_g_out_mla, v_g_out_ssm, v_g_out_dil, v_w_o,
                          v_g_ffn, v_w_gate, v_w_up, v_w_down, v_g_final)))
    cx, cy, cc = _place()
    core = cc.astype(jnp.int32).reshape(1)
    chip = (2 * cx + cy).astype(jnp.int32).reshape(1)
    rope = _rope_tables()
    small = [{n: W[n][l] for n in SMALL} for l in range(DEPTH)]
    for sp in small:
        for n in ('g_mix', 'g_q', 'g_kv', 'g_ffn'):
            sp[n] = sp[n].reshape(1, -1)

    def tok_of(tokens):
        return sum(t[0, 0] for t in tokens) if tokens else None

    def shard_view(a, n):
        return a.swapaxes(1, 2) if BIG[n] == 't' else a

    def gather_start(l, group, names, after):
        return _gather_start(f"gather_{group}_start_{l}", [shard_view(W[n], n)[l].astype(BF16) for n in names], after)

    xa = x[0]
    h1_mix = gather_start(0, "mix", MIXER_W, jnp.zeros((8, LANES), F32))
    h1_ffn = gather_start(0, "ffn", FFN_W, h1_mix[4])
    h2_mix = _gather_pass_on("gather_mix_0", h1_mix, xa)
    saved, full = [], []
    tokens = [h2_mix[4], h1_ffn[4]]
    for l in range(DEPTH):
        last = l + 1 == DEPTH
        wm = dict(zip(MIXER_W, _gather_finish(f"gather_mix_{l}", h2_mix, xa)))
        sp = dict(small[l])
        sp['g_mix'] = _behind(sp['g_mix'], tok_of(tokens))
        s = _mixer_fwd_in(xa, wm, sp, rope)
        tokens = []
        first_ffn = {}
        if l == 0:
            h1_first = h1_ffn
            def mid(dep):
                first_ffn['h'] = _gather_pass_on("gather_ffn_0", h1_first, dep)
                return first_ffn['h'][4][0, 0]
        else:
            mid = None
        if not last:
            h1_mix = gather_start(l + 1, "mix", MIXER_W, s['y_mla'])
            h1_ffn = gather_start(l + 1, "ffn", FFN_W, h1_mix[4])
            tokens += [h1_mix[4], h1_ffn[4]]
        x1 = _mixer_fwd_out(s, wm, small[l], tok_of(tokens), mid)
        if l == 0:
            h2_ffn = first_ffn['h']
        tokens = []
        wf = dict(zip(FFN_W, _gather_finish(f"gather_ffn_{l}", h2_ffn, x1)))
        if not last:
            h2_mix = _gather_pass_on(f"gather_mix_{l + 1}", h1_mix, x1)
            tokens.append(h2_mix[4])
        xa = _ffn_fwd(x1, wf, small[l], s, tok_of(tokens))
        tokens = []
        if not last:
            h2_ffn = _gather_pass_on(f"gather_ffn_{l + 1}", h1_ffn, xa)
            tokens.append(h2_ffn[4])
        saved.append(s)
        full.append({**wm, **wf})
    gf = g_final.reshape(1, D)
    ones = jnp.ones((T, 1), F32)
    dxa, dgf, loss_rows = _rows_vjp("loss", _loss_fn, [_row(xa)], [gf], [_row(loss_target[0])], [_row(ones)],
                                    primal=True)
    loss_here = _sum_rows("loss_sum", loss_rows)[0, 0]

    bufs = {n: None for n in BIG}
    pending = []

    def advance(dep):
        tokens = []
        for g in pending:
            names, tag = g['names'], g['tag']
            step = g['steps'].pop(0)
            if step == 'sibling':
                p4 = [a.reshape((4, 2) + a.shape[1:]) for a in g['gw']]
                gots = [lax.empty((4,) + a.shape[1:], F32) for a in g['gw']]
                g['h'] = _split_start("rs_sibling_start_" + tag, p4, gots, 4 * len(p4), _sibling_plan, dep)
                tokens.append(g['h'][4])
            elif step == 'chips':
                p4, gots = _split_wait("rs_sibling_wait_" + tag, g['h'], 4 * len(names), _sibling_plan, dep)
                s4 = [_pair_sum("rs_pair_sum_" + n, p, q, core) for n, p, q in zip(names, p4, gots)]
                gots = [lax.empty((3,) + a.shape[1:], a.dtype) for a in s4]
                g['h'] = _split_start("rs_chips_start_" + tag, s4, gots, 3 * len(s4), _chips_plan, dep)
                tokens.append(g['h'][4])
            elif step == 'update':
                s4, gots = _split_wait("rs_chips_wait_" + tag, g['h'], 3 * len(names), _chips_plan, dep)
                for n, s4n, got in zip(names, s4, gots):
                    bufs[n] = _adamw_shard("adamw_" + n, g['layer'], shard_view(W[n], n), shard_view(M[n], n),
                                           shard_view(V[n], n), s4n, got, chip, bufs[n])
        pending[:] = [g for g in pending if g['steps']]
        return tokens

    def group(names, l, gw, kind):
        steps = (['sibling'] + (['rest'] if kind == "ffn" and l > 0 else []) + ['chips']
                 + (['rest'] if kind == "ffn" else []) + ['update'])
        return dict(names=names, layer=l, gw=[gw[n] for n in names], steps=steps, tag=f"{kind}_{l}")

    def pack_small(parts):
        n = sum(int(p.shape[0]) for p in parts)
        rows = -(-n // (PACK_C * SMALL_ROW_ALIGN)) * SMALL_ROW_ALIGN
        return jnp.concatenate(parts + [jnp.zeros((rows * PACK_C - n,), F32)]).reshape(rows, PACK_C)

    def small_of(layers):
        return [g_small[k][n].reshape(-1) for k in layers for n in SMALL]

    g_small = [None] * DEPTH
    tokens = []
    upper = list(range(1, DEPTH))
    tot_upper = None
    for l in reversed(range(DEPTH)):
        early = l == 0 and upper
        if early:
            flat_upper = pack_small(small_of(upper))
            h_upper = _gather_start("gather_upper_start", [flat_upper], dxa)
            tokens.append(h_upper[4])
        dx1, gw_f, gs_f = _ffn_bwd(dxa, saved[l], full[l], small[l], tok_of(tokens))
        pending.append(group(FFN_W, l, gw_f, "ffn"))
        tokens = advance(dx1)
        if early:
            h_upper = _gather_pass_on("gather_upper", h_upper, dx1)
            tokens.append(h_upper[4])
        cts, gw_o, gs_o = _mixer_bwd_out(dx1, saved[l], full[l], small[l], tok_of(tokens))
        pending.append(group(OUT_W, l, gw_o, "out"))
        tokens = advance(cts[0])
        if early:
            (g_upper,) = _gather_finish("gather_upper", h_upper, cts[0])
            tot_upper = _sum_devices("small_sum_upper", g_upper).reshape(-1)
        dxa, gw_i, gs_i = _mixer_bwd_in(cts, dx1, saved[l], full[l], small[l], rope, tok_of(tokens))
        pending.append(group(IN_W, l, gw_i, "in"))
        tokens = advance(dxa)
        g_small[l] = {**gs_f, **gs_o, **gs_i}

    first = [0] if upper else list(range(DEPTH))
    flat = pack_small(small_of(first) + [dgf.reshape(-1), loss_here.reshape(1)])
    h_small = _gather_start("gather_small_start", [flat], dxa)
    advance(h_small[4])
    h_small = _gather_pass_on("gather_small", h_small, flat)
    tokens = advance(h_small[4])
    (gathered,) = _gather_finish("gather_small", h_small, flat)
    tot = _behind(_sum_devices("small_sum", gathered).reshape(-1), tok_of(tokens))

    def unpack(flat_sum, layers):
        out, off = {}, 0
        for k in layers:
            for n, shp in SMALL.items():
                size = int(np.prod(shp))
                out[k, n] = flat_sum[off:off + size].reshape(shp)
                off += size
        return out, off

    per_layer, off = unpack(tot, first)
    if upper:
        per_layer.update(unpack(tot_upper, upper)[0])
    grads = {n: jnp.stack([per_layer[k, n] for k in range(DEPTH)]) for n in SMALL}
    grads['g_final'] = tot[off:off + D]
    loss = tot[off + D]

    delta, new_m, new_v = {}, {}, {}
    for n in PARAMS:
        if n not in BIG:
            delta[n], new_m[n], new_v[n] = _adamw("adamw_" + n, W[n], grads[n], M[n], V[n])
    while pending:
        advance(delta['g_final'])
    for n in BIG:
        grads[n], delta[n], new_m[n], new_v[n] = [shard_view(b, n) for b in bufs[n]]
    return (loss, dxa[None], *[grads[n] for n in PARAMS], *[delta[n] for n in PARAMS],
            *[new_m[n] for n in PARAMS], *[new_v[n] for n in PARAMS])
```
